```python
import jax, jax.numpy as jnp
from jax import lax
import numpy as np

D_MODEL = 1024
BATCH = 32
SEQ = 2048
DEPTH = 1

CHUNK = 64
Q_BLOCK = 128
SB_HEADS = 8
SB_HEAD_DIM = 64
SB_WIDTH = SB_HEADS * SB_HEAD_DIM
RW_HEADS = 8
RW_HEAD_DIM = 64
RW_WIDTH = RW_HEADS * RW_HEAD_DIM
W_LORA = 64
A_LORA = 64
G_LORA = 128
GN_EPS = RW_HEAD_DIM * 1e-5
N_BRANCH = 2
SB_COLS = 3 * SB_WIDTH
RW_COLS = 3 * RW_WIDTH + W_LORA + A_LORA + G_LORA
GATE_COLS = N_BRANCH * D_MODEL
IN_COLS = SB_COLS + RW_COLS + GATE_COLS
D_FF = ((8 * D_MODEL + 3 * 256 - 1) // (3 * 256)) * 256
RMS_EPS = 1e-6

kernel_name = "hybrid_stickbreak_rwkv7_swiglu_sandwich"


def rms_norm(x, g):
    xf = x.astype(jnp.float32)
    y = xf * lax.rsqrt(jnp.mean(xf * xf, axis=-1, keepdims=True) + RMS_EPS)
    return (y * g).astype(x.dtype)


def token_shift(p):
    return jnp.pad(p, ((0, 0), (1, 0), (0, 0)))[:, :-1]


def stick_breaking_attention(q, k, v):
    B, S, H, Dh = q.shape
    scale = Dh ** -0.5
    outs = []
    for i in range(S // Q_BLOCK):
        q0 = i * Q_BLOCK
        kv_len = q0 + Q_BLOCK
        qb = q[:, q0:kv_len]
        kb = k[:, :kv_len]
        vb = v[:, :kv_len]
        z = jnp.einsum('bqhd,bkhd->bhqk', qb, kb).astype(jnp.float32) * scale
        t_idx = q0 + jnp.arange(Q_BLOCK)[:, None]
        s_idx = jnp.arange(kv_len)[None, :]
        strict = s_idx < t_idx
        log_fail = jnp.where(strict, jax.nn.log_sigmoid(-z), 0.0)
        after = lax.cumsum(log_fail, axis=3, reverse=True) - log_fail
        w = jnp.where(strict, jnp.exp(jax.nn.log_sigmoid(z) + after), 0.0)
        outs.append(jnp.einsum('bhqk,bkhd->bqhd', w.astype(vb.dtype), vb))
    return jnp.concatenate(outs, axis=1)


def wkv7_scan(r, decay, k, v, kk, a):
    B, S, H, N = r.shape

    def to_chunks(t):
        return t.astype(jnp.float32).transpose(1, 0, 2, 3).reshape(S // CHUNK, CHUNK, B, H, N)

    def step(state, inp):
        r_t, w_t, k_t, v_t, kk_t, a_t = inp
        sa = jnp.einsum('bhvk,bhk->bhv', state, -kk_t)
        state = (state * w_t[:, :, None, :]
                 + sa[..., None] * (kk_t * a_t)[:, :, None, :]
                 + v_t[..., None] * k_t[:, :, None, :])
        return state, jnp.einsum('bhvk,bhk->bhv', state, r_t)

    def chunk_step(state, chunk_inp):
        return lax.scan(step, state, chunk_inp)

    state0 = jnp.zeros((B, H, N, N), jnp.float32)
    _, y = lax.scan(chunk_step, state0, tuple(to_chunks(t) for t in (r, decay, k, v, kk, a)))
    return y.reshape(S, B, H, N).transpose(1, 0, 2, 3)


def rwkv7_time_mix(p, mu, w0, w_up, a0, a_up, g_up, k_k, k_a, r_k, lnx_w, lnx_b):
    B, S, _ = p.shape
    H, N = RW_HEADS, RW_HEAD_DIM
    p = p + (token_shift(p) - p) * mu
    o1, o2, o3 = RW_WIDTH, 2 * RW_WIDTH, 3 * RW_WIDTH
    r, k, v = p[..., :o1], p[..., o1:o2], p[..., o2:o3]
    xw = p[..., o3:o3 + W_LORA]
    xa = p[..., o3 + W_LORA:o3 + W_LORA + A_LORA]
    xg = p[..., o3 + W_LORA + A_LORA:]
    w_raw = (w0 + jnp.tanh(xw) @ w_up).astype(jnp.float32)
    decay = jnp.exp(-jnp.exp(-jax.nn.softplus(-w_raw) - 0.5))
    a = jax.nn.sigmoid((a0 + xa @ a_up).astype(jnp.float32))
    g = jax.nn.sigmoid(xg) @ g_up
    kk = (k * k_k).astype(jnp.float32).reshape(B, S, H, N)
    kk = kk / jnp.maximum(jnp.linalg.norm(kk, axis=-1, keepdims=True), 1e-12)
    k = k.astype(jnp.float32) * (1.0 + (a - 1.0) * k_a)
    heads = lambda t: t.astype(jnp.float32).reshape(B, S, H, N)
    r_h, k_h, v_h = heads(r), heads(k), heads(v)
    y = wkv7_scan(r_h, heads(decay), k_h, v_h, kk, heads(a))
    mean = jnp.mean(y, axis=-1, keepdims=True)
    var = jnp.mean(jnp.square(y - mean), axis=-1, keepdims=True)
    y = ((y - mean) * lax.rsqrt(var + GN_EPS)).reshape(B, S, RW_WIDTH) * lnx_w + lnx_b
    bonus = (jnp.sum(r_h * k_h * r_k, axis=-1, keepdims=True) * v_h).reshape(B, S, RW_WIDTH)
    return ((y + bonus) * g).astype(p.dtype)


def _fwd_setup_inputs(seed: int = 0) -> dict:
    key = jax.random.key(seed)
    ks = jax.random.split(key, 24)
    L, D = DEPTH, D_MODEL
    nrm = lambda k_, shape, s: jax.random.normal(k_, shape, jnp.float32) * s
    return {
        "x": nrm(ks[0], (BATCH, SEQ, D), 1.0),
        "norm_mix_pre": 1.0 + nrm(ks[1], (L, D), 0.1),
        "w_in": nrm(ks[2], (L, D, IN_COLS), D ** -0.5),
        "b_gate": nrm(ks[3], (L, GATE_COLS), 0.1),
        "mu_rw": jax.random.uniform(ks[4], (L, RW_COLS), jnp.float32),
        "w0": jax.random.uniform(ks[5], (L, RW_WIDTH), jnp.float32, -6.0, 0.0),
        "w_up": nrm(ks[6], (L, W_LORA, RW_WIDTH), 0.1),
        "a0": nrm(ks[7], (L, RW_WIDTH), 0.1),
        "a_up": nrm(ks[8], (L, A_LORA, RW_WIDTH), 0.5 * A_LORA ** -0.5),
        "g_up": nrm(ks[9], (L, G_LORA, RW_WIDTH), G_LORA ** -0.5),
        "k_k": 0.85 + nrm(ks[10], (L, RW_WIDTH), 0.05),
        "k_a": 1.0 + nrm(ks[11], (L, RW_WIDTH), 0.05),
        "r_k": nrm(ks[12], (L, RW_HEADS, RW_HEAD_DIM), 0.1),
        "lnx_w": 1.0 + nrm(ks[13], (L, RW_WIDTH), 0.1),
        "lnx_b": nrm(ks[14], (L, RW_WIDTH), 0.02),
        "w_sb_out": nrm(ks[15], (L, SB_WIDTH, D), SB_WIDTH ** -0.5),
        "w_rw_out": nrm(ks[16], (L, RW_WIDTH, D), RW_WIDTH ** -0.5),
        "w_o": nrm(ks[17], (L, D, D), D ** -0.5),
        "norm_mix_post": 1.0 + nrm(ks[18], (L, D), 0.1),
        "norm_ffn_pre": 1.0 + nrm(ks[19], (L, D), 0.1),
        "w_ffn_gate": nrm(ks[20], (L, D, D_FF), D ** -0.5),
        "w_ffn_up": nrm(ks[21], (L, D, D_FF), D ** -0.5),
        "w_ffn_down": nrm(ks[22], (L, D_FF, D), D_FF ** -0.5),
        "norm_ffn_post": 1.0 + nrm(ks[23], (L, D), 0.1),
    }


def _fwd_reference(x, norm_mix_pre, w_in, b_gate, mu_rw, w0, w_up, a0, a_up, g_up, k_k, k_a, r_k,
              lnx_w, lnx_b, w_sb_out, w_rw_out, w_o, norm_mix_post, norm_ffn_pre,
              w_ffn_gate, w_ffn_up, w_ffn_down, norm_ffn_post):
    B, S, D = x.shape
    for l in range(DEPTH):
        h = rms_norm(x, norm_mix_pre[l])
        proj = h @ w_in[l]
        p_sb = proj[..., :SB_COLS]
        p_rw = proj[..., SB_COLS:SB_COLS + RW_COLS]
        gates = jax.nn.sigmoid(proj[..., SB_COLS + RW_COLS:] + b_gate[l])
        q = p_sb[..., :SB_WIDTH].reshape(B, S, SB_HEADS, SB_HEAD_DIM)
        k = p_sb[..., SB_WIDTH:2 * SB_WIDTH].reshape(B, S, SB_HEADS, SB_HEAD_DIM)
        v = p_sb[..., 2 * SB_WIDTH:].reshape(B, S, SB_HEADS, SB_HEAD_DIM)
        o_sb = stick_breaking_attention(q, k, v).reshape(B, S, SB_WIDTH)
        o_rw = rwkv7_time_mix(p_rw, mu_rw[l], w0[l], w_up[l], a0[l], a_up[l], g_up[l],
                              k_k[l], k_a[l], r_k[l], lnx_w[l], lnx_b[l])
        merged = (gates[..., :D] * (o_sb @ w_sb_out[l])
                  + gates[..., D:] * (o_rw @ w_rw_out[l]))
        x = x + rms_norm(merged @ w_o[l], norm_mix_post[l])
        h = rms_norm(x, norm_ffn_pre[l])
        f = (jax.nn.silu(h @ w_ffn_gate[l]) * (h @ w_ffn_up[l])) @ w_ffn_down[l]
        x = x + rms_norm(f, norm_ffn_post[l])
    return x


import jax as _jax
import jax.numpy as _jnp

TWIN_FORMAT = 'train_step'
FWD_PARAMS = ['x', 'norm_mix_pre', 'w_in', 'b_gate', 'mu_rw', 'w0', 'w_up', 'a0', 'a_up', 'g_up', 'k_k', 'k_a', 'r_k', 'lnx_w', 'lnx_b', 'w_sb_out', 'w_rw_out', 'w_o', 'norm_mix_post', 'norm_ffn_pre', 'w_ffn_gate', 'w_ffn_up', 'w_ffn_down', 'norm_ffn_post']
TWIN_WEIGHTS = ['norm_mix_pre', 'w_in', 'b_gate', 'mu_rw', 'w0', 'w_up', 'a0', 'a_up', 'g_up', 'k_k', 'k_a', 'r_k', 'lnx_w', 'lnx_b', 'w_sb_out', 'w_rw_out', 'w_o', 'norm_mix_post', 'norm_ffn_pre', 'w_ffn_gate', 'w_ffn_up', 'w_ffn_down', 'norm_ffn_post']
TWIN_DIFF_INPUT = 'x'
TWIN_INPUTS = ['x', 'norm_mix_pre', 'w_in', 'b_gate', 'mu_rw', 'w0', 'w_up', 'a0', 'a_up', 'g_up', 'k_k', 'k_a', 'r_k', 'lnx_w', 'lnx_b', 'w_sb_out', 'w_rw_out', 'w_o', 'norm_mix_post', 'norm_ffn_pre', 'w_ffn_gate', 'w_ffn_up', 'w_ffn_down', 'norm_ffn_post', 'loss_target', 'm_norm_mix_pre', 'm_w_in', 'm_b_gate', 'm_mu_rw', 'm_w0', 'm_w_up', 'm_a0', 'm_a_up', 'm_g_up', 'm_k_k', 'm_k_a', 'm_r_k', 'm_lnx_w', 'm_lnx_b', 'm_w_sb_out', 'm_w_rw_out', 'm_w_o', 'm_norm_mix_post', 'm_norm_ffn_pre', 'm_w_ffn_gate', 'm_w_ffn_up', 'm_w_ffn_down', 'm_norm_ffn_post', 'v_norm_mix_pre', 'v_w_in', 'v_b_gate', 'v_mu_rw', 'v_w0', 'v_w_up', 'v_a0', 'v_a_up', 'v_g_up', 'v_k_k', 'v_k_a', 'v_r_k', 'v_lnx_w', 'v_lnx_b', 'v_w_sb_out', 'v_w_rw_out', 'v_w_o', 'v_norm_mix_post', 'v_norm_ffn_pre', 'v_w_ffn_gate', 'v_w_ffn_up', 'v_w_ffn_down', 'v_norm_ffn_post']
TWIN_OUTPUTS = ['loss', 'grad_x', 'grad_norm_mix_pre', 'grad_w_in', 'grad_b_gate', 'grad_mu_rw', 'grad_w0', 'grad_w_up', 'grad_a0', 'grad_a_up', 'grad_g_up', 'grad_k_k', 'grad_k_a', 'grad_r_k', 'grad_lnx_w', 'grad_lnx_b', 'grad_w_sb_out', 'grad_w_rw_out', 'grad_w_o', 'grad_norm_mix_post', 'grad_norm_ffn_pre', 'grad_w_ffn_gate', 'grad_w_ffn_up', 'grad_w_ffn_down', 'grad_norm_ffn_post', 'delta_norm_mix_pre', 'delta_w_in', 'delta_b_gate', 'delta_mu_rw', 'delta_w0', 'delta_w_up', 'delta_a0', 'delta_a_up', 'delta_g_up', 'delta_k_k', 'delta_k_a', 'delta_r_k', 'delta_lnx_w', 'delta_lnx_b', 'delta_w_sb_out', 'delta_w_rw_out', 'delta_w_o', 'delta_norm_mix_post', 'delta_norm_ffn_pre', 'delta_w_ffn_gate', 'delta_w_ffn_up', 'delta_w_ffn_down', 'delta_norm_ffn_post', 'new_m_norm_mix_pre', 'new_m_w_in', 'new_m_b_gate', 'new_m_mu_rw', 'new_m_w0', 'new_m_w_up', 'new_m_a0', 'new_m_a_up', 'new_m_g_up', 'new_m_k_k', 'new_m_k_a', 'new_m_r_k', 'new_m_lnx_w', 'new_m_lnx_b', 'new_m_w_sb_out', 'new_m_w_rw_out', 'new_m_w_o', 'new_m_norm_mix_post', 'new_m_norm_ffn_pre', 'new_m_w_ffn_gate', 'new_m_w_ffn_up', 'new_m_w_ffn_down', 'new_m_norm_ffn_post', 'new_v_norm_mix_pre', 'new_v_w_in', 'new_v_b_gate', 'new_v_mu_rw', 'new_v_w0', 'new_v_w_up', 'new_v_a0', 'new_v_a_up', 'new_v_g_up', 'new_v_k_k', 'new_v_k_a', 'new_v_r_k', 'new_v_lnx_w', 'new_v_lnx_b', 'new_v_w_sb_out', 'new_v_w_rw_out', 'new_v_w_o', 'new_v_norm_mix_post', 'new_v_norm_ffn_pre', 'new_v_w_ffn_gate', 'new_v_w_ffn_up', 'new_v_w_ffn_down', 'new_v_norm_ffn_post']
TWIN_LEAF_KINDS = {'loss': 'loss', 'grad_x': 'grad_x', 'grad_norm_mix_pre': 'grad_w', 'grad_w_in': 'grad_w', 'grad_b_gate': 'grad_w', 'grad_mu_rw': 'grad_w', 'grad_w0': 'grad_w', 'grad_w_up': 'grad_w', 'grad_a0': 'grad_w', 'grad_a_up': 'grad_w', 'grad_g_up': 'grad_w', 'grad_k_k': 'grad_w', 'grad_k_a': 'grad_w', 'grad_r_k': 'grad_w', 'grad_lnx_w': 'grad_w', 'grad_lnx_b': 'grad_w', 'grad_w_sb_out': 'grad_w', 'grad_w_rw_out': 'grad_w', 'grad_w_o': 'grad_w', 'grad_norm_mix_post': 'grad_w', 'grad_norm_ffn_pre': 'grad_w', 'grad_w_ffn_gate': 'grad_w', 'grad_w_ffn_up': 'grad_w', 'grad_w_ffn_down': 'grad_w', 'grad_norm_ffn_post': 'grad_w', 'delta_norm_mix_pre': 'delta_w', 'delta_w_in': 'delta_w', 'delta_b_gate': 'delta_w', 'delta_mu_rw': 'delta_w', 'delta_w0': 'delta_w', 'delta_w_up': 'delta_w', 'delta_a0': 'delta_w', 'delta_a_up': 'delta_w', 'delta_g_up': 'delta_w', 'delta_k_k': 'delta_w', 'delta_k_a': 'delta_w', 'delta_r_k': 'delta_w', 'delta_lnx_w': 'delta_w', 'delta_lnx_b': 'delta_w', 'delta_w_sb_out': 'delta_w', 'delta_w_rw_out': 'delta_w', 'delta_w_o': 'delta_w', 'delta_norm_mix_post': 'delta_w', 'delta_norm_ffn_pre': 'delta_w', 'delta_w_ffn_gate': 'delta_w', 'delta_w_ffn_up': 'delta_w', 'delta_w_ffn_down': 'delta_w', 'delta_norm_ffn_post': 'delta_w', 'new_m_norm_mix_pre': 'new_m', 'new_m_w_in': 'new_m', 'new_m_b_gate': 'new_m', 'new_m_mu_rw': 'new_m', 'new_m_w0': 'new_m', 'new_m_w_up': 'new_m', 'new_m_a0': 'new_m', 'new_m_a_up': 'new_m', 'new_m_g_up': 'new_m', 'new_m_k_k': 'new_m', 'new_m_k_a': 'new_m', 'new_m_r_k': 'new_m', 'new_m_lnx_w': 'new_m', 'new_m_lnx_b': 'new_m', 'new_m_w_sb_out': 'new_m', 'new_m_w_rw_out': 'new_m', 'new_m_w_o': 'new_m', 'new_m_norm_mix_post': 'new_m', 'new_m_norm_ffn_pre': 'new_m', 'new_m_w_ffn_gate': 'new_m', 'new_m_w_ffn_up': 'new_m', 'new_m_w_ffn_down': 'new_m', 'new_m_norm_ffn_post': 'new_m', 'new_v_norm_mix_pre': 'new_v', 'new_v_w_in': 'new_v', 'new_v_b_gate': 'new_v', 'new_v_mu_rw': 'new_v', 'new_v_w0': 'new_v', 'new_v_w_up': 'new_v', 'new_v_a0': 'new_v', 'new_v_a_up': 'new_v', 'new_v_g_up': 'new_v', 'new_v_k_k': 'new_v', 'new_v_k_a': 'new_v', 'new_v_r_k': 'new_v', 'new_v_lnx_w': 'new_v', 'new_v_lnx_b': 'new_v', 'new_v_w_sb_out': 'new_v', 'new_v_w_rw_out': 'new_v', 'new_v_w_o': 'new_v', 'new_v_norm_mix_post': 'new_v', 'new_v_norm_ffn_pre': 'new_v', 'new_v_w_ffn_gate': 'new_v', 'new_v_w_ffn_up': 'new_v', 'new_v_w_ffn_down': 'new_v', 'new_v_norm_ffn_post': 'new_v'}


def _forward(args):
    return _fwd_reference(*[args[k] for k in FWD_PARAMS])


def _output_shape():
    out = _jax.eval_shape(lambda: _forward(_fwd_setup_inputs(0)))
    return out.shape, out.dtype

N_MICROBATCH = 1
ADAM_LR = 0.001
ADAM_B1 = 0.9
ADAM_B2 = 0.999
ADAM_EPS = 1e-08
ADAM_WD = 0.01
ADAM_STEP = 10
PER_EXAMPLE_BATCH_AXIS = {'x': 0, 'loss_target': 0}
SHARED_INPUTS = []
_WEIGHT_DTYPES = {'norm_mix_pre': _jnp.float32, 'w_in': _jnp.float32, 'b_gate': _jnp.float32, 'mu_rw': _jnp.float32, 'w0': _jnp.float32, 'w_up': _jnp.float32, 'a0': _jnp.float32, 'a_up': _jnp.float32, 'g_up': _jnp.float32, 'k_k': _jnp.float32, 'k_a': _jnp.float32, 'r_k': _jnp.float32, 'lnx_w': _jnp.float32, 'lnx_b': _jnp.float32, 'w_sb_out': _jnp.float32, 'w_rw_out': _jnp.float32, 'w_o': _jnp.float32, 'norm_mix_post': _jnp.float32, 'norm_ffn_pre': _jnp.float32, 'w_ffn_gate': _jnp.float32, 'w_ffn_up': _jnp.float32, 'w_ffn_down': _jnp.float32, 'norm_ffn_post': _jnp.float32}
MOMENT_SCALE = {'norm_mix_pre': 1.086082e+00, 'w_in': 4.399282e-01, 'b_gate': 2.370318e-01, 'mu_rw': 9.098342e-01, 'w0': 2.693706e-01, 'w_up': 2.952686e-02, 'a0': 2.458792e-01, 'a_up': 2.306466e-01, 'g_up': 7.098141e-01, 'k_k': 1.156772e+00, 'k_a': 7.249566e-01, 'r_k': 1.354983e+00, 'lnx_w': 8.335079e-01, 'lnx_b': 1.757669e+00, 'w_sb_out': 5.566847e-01, 'w_rw_out': 4.940325e-01, 'w_o': 8.239255e-01, 'norm_mix_post': 6.442560e+01, 'norm_ffn_pre': 8.016224e-01, 'w_ffn_gate': 2.767568e-01, 'w_ffn_up': 4.379604e-01, 'w_ffn_down': 7.341088e-01, 'norm_ffn_post': 6.392080e+01}


def _to_microbatches(a, axis):
    t = _jnp.moveaxis(a, axis, 0)
    t = t.reshape((N_MICROBATCH, t.shape[0] // N_MICROBATCH) + t.shape[1:])
    return _jnp.moveaxis(t, 1, axis + 1)


def setup_inputs(seed: int = 0) -> dict:
    inp = _fwd_setup_inputs(seed)
    key = _jax.random.fold_in(_jax.random.key(seed), 7919)
    shape, _ = _output_shape()
    out = dict(inp)
    out["loss_target"] = _jax.random.normal(_jax.random.fold_in(key, 0), shape, _jnp.float32)
    for i, name in enumerate(TWIN_WEIGHTS):
        w = inp[name].astype(_jnp.float32)
        if MOMENT_SCALE is None:
            s = _jnp.sqrt(_jnp.mean(_jnp.square(w)) + 1e-30)
        else:
            s = MOMENT_SCALE[name]
        km, kv = _jax.random.split(_jax.random.fold_in(key, i + 1))
        out[name] = w
        out["m_" + name] = s * _jax.random.normal(km, w.shape, _jnp.float32)
        out["v_" + name] = (s * s) * _jax.random.uniform(kv, w.shape, _jnp.float32, 0.5, 1.5)
    if N_MICROBATCH > 1:
        for name, axis in PER_EXAMPLE_BATCH_AXIS.items():
            out[name] = _to_microbatches(out[name], axis)
    return {'x': out['x'], 'norm_mix_pre': out['norm_mix_pre'], 'w_in': out['w_in'], 'b_gate': out['b_gate'], 'mu_rw': out['mu_rw'], 'w0': out['w0'], 'w_up': out['w_up'], 'a0': out['a0'], 'a_up': out['a_up'], 'g_up': out['g_up'], 'k_k': out['k_k'], 'k_a': out['k_a'], 'r_k': out['r_k'], 'lnx_w': out['lnx_w'], 'lnx_b': out['lnx_b'], 'w_sb_out': out['w_sb_out'], 'w_rw_out': out['w_rw_out'], 'w_o': out['w_o'], 'norm_mix_post': out['norm_mix_post'], 'norm_ffn_pre': out['norm_ffn_pre'], 'w_ffn_gate': out['w_ffn_gate'], 'w_ffn_up': out['w_ffn_up'], 'w_ffn_down': out['w_ffn_down'], 'norm_ffn_post': out['norm_ffn_post'], 'loss_target': out['loss_target'], 'm_norm_mix_pre': out['m_norm_mix_pre'], 'm_w_in': out['m_w_in'], 'm_b_gate': out['m_b_gate'], 'm_mu_rw': out['m_mu_rw'], 'm_w0': out['m_w0'], 'm_w_up': out['m_w_up'], 'm_a0': out['m_a0'], 'm_a_up': out['m_a_up'], 'm_g_up': out['m_g_up'], 'm_k_k': out['m_k_k'], 'm_k_a': out['m_k_a'], 'm_r_k': out['m_r_k'], 'm_lnx_w': out['m_lnx_w'], 'm_lnx_b': out['m_lnx_b'], 'm_w_sb_out': out['m_w_sb_out'], 'm_w_rw_out': out['m_w_rw_out'], 'm_w_o': out['m_w_o'], 'm_norm_mix_post': out['m_norm_mix_post'], 'm_norm_ffn_pre': out['m_norm_ffn_pre'], 'm_w_ffn_gate': out['m_w_ffn_gate'], 'm_w_ffn_up': out['m_w_ffn_up'], 'm_w_ffn_down': out['m_w_ffn_down'], 'm_norm_ffn_post': out['m_norm_ffn_post'], 'v_norm_mix_pre': out['v_norm_mix_pre'], 'v_w_in': out['v_w_in'], 'v_b_gate': out['v_b_gate'], 'v_mu_rw': out['v_mu_rw'], 'v_w0': out['v_w0'], 'v_w_up': out['v_w_up'], 'v_a0': out['v_a0'], 'v_a_up': out['v_a_up'], 'v_g_up': out['v_g_up'], 'v_k_k': out['v_k_k'], 'v_k_a': out['v_k_a'], 'v_r_k': out['v_r_k'], 'v_lnx_w': out['v_lnx_w'], 'v_lnx_b': out['v_lnx_b'], 'v_w_sb_out': out['v_w_sb_out'], 'v_w_rw_out': out['v_w_rw_out'], 'v_w_o': out['v_w_o'], 'v_norm_mix_post': out['v_norm_mix_post'], 'v_norm_ffn_pre': out['v_norm_ffn_pre'], 'v_w_ffn_gate': out['v_w_ffn_gate'], 'v_w_ffn_up': out['v_w_ffn_up'], 'v_w_ffn_down': out['v_w_ffn_down'], 'v_norm_ffn_post': out['v_norm_ffn_post']}


def _loss(weights, diff, rest, loss_target):
    with _jax.named_scope("forward"):
        args = {**rest, TWIN_DIFF_INPUT: diff, **{k: w.astype(_WEIGHT_DTYPES[k]) for k, w in weights.items()}}
        y = _forward(args)
    with _jax.named_scope("loss_head"):
        err = _jnp.square(y.astype(_jnp.float32) - loss_target)
        return 0.5 * _jnp.sum(_jnp.mean(err, axis=-1)) if err.ndim else 0.5 * err


def _adamw(w, g, m, v):
    m = ADAM_B1 * m + (1.0 - ADAM_B1) * g
    v = ADAM_B2 * v + (1.0 - ADAM_B2) * _jnp.square(g)
    m_hat = m / (1.0 - ADAM_B1 ** ADAM_STEP)
    v_hat = v / (1.0 - ADAM_B2 ** ADAM_STEP)
    delta = -ADAM_LR * (m_hat / (_jnp.sqrt(v_hat) + ADAM_EPS) + ADAM_WD * w)
    return delta, m, v


def reference(x, norm_mix_pre, w_in, b_gate, mu_rw, w0, w_up, a0, a_up, g_up, k_k, k_a, r_k, lnx_w, lnx_b, w_sb_out, w_rw_out, w_o, norm_mix_post, norm_ffn_pre, w_ffn_gate, w_ffn_up, w_ffn_down, norm_ffn_post, loss_target, m_norm_mix_pre, m_w_in, m_b_gate, m_mu_rw, m_w0, m_w_up, m_a0, m_a_up, m_g_up, m_k_k, m_k_a, m_r_k, m_lnx_w, m_lnx_b, m_w_sb_out, m_w_rw_out, m_w_o, m_norm_mix_post, m_norm_ffn_pre, m_w_ffn_gate, m_w_ffn_up, m_w_ffn_down, m_norm_ffn_post, v_norm_mix_pre, v_w_in, v_b_gate, v_mu_rw, v_w0, v_w_up, v_a0, v_a_up, v_g_up, v_k_k, v_k_a, v_r_k, v_lnx_w, v_lnx_b, v_w_sb_out, v_w_rw_out, v_w_o, v_norm_mix_post, v_norm_ffn_pre, v_w_ffn_gate, v_w_ffn_up, v_w_ffn_down, v_norm_ffn_post):
    given = dict(x=x, norm_mix_pre=norm_mix_pre, w_in=w_in, b_gate=b_gate, mu_rw=mu_rw, w0=w0, w_up=w_up, a0=a0, a_up=a_up, g_up=g_up, k_k=k_k, k_a=k_a, r_k=r_k, lnx_w=lnx_w, lnx_b=lnx_b, w_sb_out=w_sb_out, w_rw_out=w_rw_out, w_o=w_o, norm_mix_post=norm_mix_post, norm_ffn_pre=norm_ffn_pre, w_ffn_gate=w_ffn_gate, w_ffn_up=w_ffn_up, w_ffn_down=w_ffn_down, norm_ffn_post=norm_ffn_post, loss_target=loss_target, m_norm_mix_pre=m_norm_mix_pre, m_w_in=m_w_in, m_b_gate=m_b_gate, m_mu_rw=m_mu_rw, m_w0=m_w0, m_w_up=m_w_up, m_a0=m_a0, m_a_up=m_a_up, m_g_up=m_g_up, m_k_k=m_k_k, m_k_a=m_k_a, m_r_k=m_r_k, m_lnx_w=m_lnx_w, m_lnx_b=m_lnx_b, m_w_sb_out=m_w_sb_out, m_w_rw_out=m_w_rw_out, m_w_o=m_w_o, m_norm_mix_post=m_norm_mix_post, m_norm_ffn_pre=m_norm_ffn_pre, m_w_ffn_gate=m_w_ffn_gate, m_w_ffn_up=m_w_ffn_up, m_w_ffn_down=m_w_ffn_down, m_norm_ffn_post=m_norm_ffn_post, v_norm_mix_pre=v_norm_mix_pre, v_w_in=v_w_in, v_b_gate=v_b_gate, v_mu_rw=v_mu_rw, v_w0=v_w0, v_w_up=v_w_up, v_a0=v_a0, v_a_up=v_a_up, v_g_up=v_g_up, v_k_k=v_k_k, v_k_a=v_k_a, v_r_k=v_r_k, v_lnx_w=v_lnx_w, v_lnx_b=v_lnx_b, v_w_sb_out=v_w_sb_out, v_w_rw_out=v_w_rw_out, v_w_o=v_w_o, v_norm_mix_post=v_norm_mix_post, v_norm_ffn_pre=v_norm_ffn_pre, v_w_ffn_gate=v_w_ffn_gate, v_w_ffn_up=v_w_ffn_up, v_w_ffn_down=v_w_ffn_down, v_norm_ffn_post=v_norm_ffn_post)
    weights = {n: given[n] for n in TWIN_WEIGHTS}
    shared = {n: given[n] for n in SHARED_INPUTS}
    per_example = {n: given[n] for n in ['x']}
    grad_fn = _jax.value_and_grad(_loss, argnums=(0, 1))

    def one_microbatch(ex, loss_target):
        ex = dict(ex)
        diff = ex.pop(TWIN_DIFF_INPUT)
        return grad_fn(weights, diff, {**shared, **ex}, loss_target)

    if N_MICROBATCH == 1:
        loss, (grad_w, grad_x) = one_microbatch(per_example, given["loss_target"])
    else:
        def body(carry, xs):
            loss_sum, grad_sum = carry
            l_k, (gw_k, gx_k) = one_microbatch(xs[0], xs[1])
            with _jax.named_scope("update"):
                return (loss_sum + l_k, _jax.tree.map(_jnp.add, grad_sum, gw_k)), gx_k

        init = (_jnp.zeros((), _jnp.float32), _jax.tree.map(_jnp.zeros_like, weights))
        (loss, grad_w), grad_x = _jax.lax.scan(body, init, (per_example, given["loss_target"]))
    with _jax.named_scope("update"):
        delta_w, new_m, new_v = {}, {}, {}
        for n in TWIN_WEIGHTS:
            delta_w[n], new_m[n], new_v[n] = _adamw(weights[n], grad_w[n], given["m_" + n], given["v_" + n])
    return (loss, grad_x, *[grad_w[n] for n in TWIN_WEIGHTS], *[delta_w[n] for n in TWIN_WEIGHTS],
            *[new_m[n] for n in TWIN_WEIGHTS], *[new_v[n] for n in TWIN_WEIGHTS])
```

```python
import functools

import jax
import jax.numpy as jnp
from jax import lax
from jax.experimental import pallas as pl
from jax.experimental.pallas import tpu as pltpu

F32 = jnp.float32
BF16 = jnp.bfloat16

D_MODEL = 1024
HEADS = 8
HEAD_DIM = 64
WIDTH = HEADS * HEAD_DIM
W_LORA, A_LORA, G_LORA = 64, 64, 128
SB_COLS = 3 * WIDTH
RW_COLS = 3 * WIDTH + W_LORA + A_LORA + G_LORA
GATE_COLS = 2 * D_MODEL
D_FF = 2816
RMS_EPS = 1e-6
GN_EPS = HEAD_DIM * 1e-5
N_DEV = 8

ADAM_LR, ADAM_B1, ADAM_B2, ADAM_EPS, ADAM_WD, ADAM_STEP = 0.001, 0.9, 0.999, 1e-08, 0.01, 10

ROW_TILE = 256
SCAN_CHUNK = 64
ATT_BLOCK = 128
SCAN_PASSES = 3
VMEM_LIMIT = 56 * 2 ** 20

MESH = pl.DeviceIdType.MESH


def _params(sem=None, vmem=VMEM_LIMIT):
    kw = dict(vmem_limit_bytes=vmem)
    if sem is not None:
        kw["dimension_semantics"] = sem
    return pltpu.CompilerParams(**kw)


def _const_spec(shape):
    nd = len(shape)
    return pl.BlockSpec(shape, lambda *_: (0,) * nd, pipeline_mode=pl.Buffered(1))


def _row_spec(tm, n):
    return pl.BlockSpec((tm, n), lambda i: (i, 0))


def _mm(a, b):
    return lax.dot_general(a, b, (((1,), (0,)), ((), ())), preferred_element_type=F32)


def _mm_nt(a, b):
    return lax.dot_general(a, b, (((1,), (1,)), ((), ())), preferred_element_type=F32)


def _mm_tn(a, b):
    return lax.dot_general(a, b, (((0,), (0,)), ((), ())), preferred_element_type=F32)


def _softplus(z):
    return jnp.maximum(z, 0.0) + jnp.log1p(jnp.exp(-jnp.abs(z)))


def _rms_fwd(x, gain):
    rstd = lax.rsqrt(jnp.mean(x * x, axis=-1, keepdims=True) + RMS_EPS)
    xn = x * rstd
    return xn * gain, xn, rstd


def _rms_bwd(dy, xn, rstd, gain):
    u = dy * gain
    dx = rstd * (u - xn * jnp.mean(u * xn, axis=-1, keepdims=True))
    return dx, jnp.sum(dy * xn, axis=0, keepdims=True)


def _acc_out(ref, val, first):
    @pl.when(first)
    def _():
        ref[...] = val

    @pl.when(jnp.logical_not(first))
    def _():
        ref[...] += val


def _in_proj_fwd(x2, g_pre, w_qkv, w_rw, w_gate, b_gate, tm):
    T = x2.shape[0]

    def body(x_ref, g_ref, wq_ref, wr_ref, wg_ref, b_ref, h_ref, qkv_ref, prw_ref, gate_ref):
        h = _rms_fwd(x_ref[...], g_ref[...])[0].astype(BF16)
        h_ref[...] = h
        qkv_ref[...] = _mm(h, wq_ref[...]).astype(BF16)
        prw_ref[...] = _mm(h, wr_ref[...])
        gate_ref[...] = jax.nn.sigmoid(_mm(h, wg_ref[...]) + b_ref[...])

    return pl.pallas_call(
        body, name="in_proj_fwd", grid=(T // tm,),
        in_specs=[_row_spec(tm, D_MODEL), _const_spec((1, D_MODEL)), _const_spec((D_MODEL, SB_COLS)),
                  _const_spec((D_MODEL, RW_COLS)), _const_spec((D_MODEL, GATE_COLS)), _const_spec((1, GATE_COLS))],
        out_specs=[_row_spec(tm, D_MODEL), _row_spec(tm, SB_COLS), _row_spec(tm, RW_COLS), _row_spec(tm, GATE_COLS)],
        out_shape=[jax.ShapeDtypeStruct((T, D_MODEL), BF16), jax.ShapeDtypeStruct((T, SB_COLS), BF16),
                   jax.ShapeDtypeStruct((T, RW_COLS), F32), jax.ShapeDtypeStruct((T, GATE_COLS), F32)],
        compiler_params=_params(("parallel",)),
    )(x2, g_pre, w_qkv, w_rw, w_gate, b_gate)


def _in_proj_bwd(x2, g_pre, dx1, dqkv, dprw, dgate, w_qkv, w_rw, w_gate, tm):
    T = x2.shape[0]

    def body(x_ref, g_ref, dx1_ref, dq_ref, dr_ref, dg_ref, wq_ref, wr_ref, wg_ref, gx_ref, dgain_ref):
        dh = _mm_nt(dq_ref[...], wq_ref[...]) + _mm_nt(dr_ref[...], wr_ref[...]) + _mm_nt(dg_ref[...], wg_ref[...])
        gain = g_ref[...]
        _, xn, rstd = _rms_fwd(x_ref[...], gain)
        dx, dgain = _rms_bwd(dh, xn, rstd, gain)
        gx_ref[...] = dx1_ref[...] + dx
        _acc_out(dgain_ref, dgain, pl.program_id(0) == 0)

    return pl.pallas_call(
        body, name="in_proj_bwd", grid=(T // tm,),
        in_specs=[_row_spec(tm, D_MODEL), _const_spec((1, D_MODEL)), _row_spec(tm, D_MODEL), _row_spec(tm, SB_COLS),
                  _row_spec(tm, RW_COLS), _row_spec(tm, GATE_COLS), _const_spec((D_MODEL, SB_COLS)),
                  _const_spec((D_MODEL, RW_COLS)), _const_spec((D_MODEL, GATE_COLS))],
        out_specs=[_row_spec(tm, D_MODEL), pl.BlockSpec((1, D_MODEL), lambda i: (0, 0))],
        out_shape=[jax.ShapeDtypeStruct((T, D_MODEL), F32), jax.ShapeDtypeStruct((1, D_MODEL), F32)],
        compiler_params=_params(("arbitrary",)),
    )(x2, g_pre, dx1, dqkv, dprw, dgate, w_qkv, w_rw, w_gate)


def _pick_tile(n, cap):
    best = None
    for t in range(128, min(n, cap) + 1, 128):
        if n % t == 0:
            best = t
    return n if best is None else best


def _grad_w(a, b, name):
    T, K = a.shape
    N = b.shape[1]
    tk, tn, tt = _pick_tile(K, 1024), _pick_tile(N, 1024), min(T, 512)

    def body(a_ref, b_ref, o_ref):
        _acc_out(o_ref, _mm_tn(a_ref[...], b_ref[...]), pl.program_id(2) == 0)

    return pl.pallas_call(
        body, name=name, grid=(K // tk, N // tn, T // tt),
        in_specs=[pl.BlockSpec((tt, tk), lambda i, j, t: (t, i)), pl.BlockSpec((tt, tn), lambda i, j, t: (t, j))],
        out_specs=pl.BlockSpec((tk, tn), lambda i, j, t: (i, j)),
        out_shape=jax.ShapeDtypeStruct((K, N), F32),
        compiler_params=_params(("parallel", "parallel", "arbitrary")),
    )(a, b)


def _tri(n, kind):
    r = lax.broadcasted_iota(jnp.int32, (n, n), 0)
    c = lax.broadcasted_iota(jnp.int32, (n, n), 1)
    return {"gt": r > c, "le": r <= c, "lt": r < c, "ge": r >= c}[kind]


def _split_mm(x, u):
    hi = x.astype(BF16)
    lo = (x - hi.astype(F32)).astype(BF16)
    return _mm(hi, u) + _mm(lo, u)


def _sb_scores(q, k, i, j, blk):
    z = _mm_nt(q, k) * (HEAD_DIM ** -0.5)
    sp = _softplus(z)
    row = lax.broadcasted_iota(jnp.int32, (blk, blk), 0) + i * blk
    col = lax.broadcasted_iota(jnp.int32, (blk, blk), 1) + j * blk
    strict = col < row
    return z, sp, strict


def _sb_fwd(q, k, v):
    G, S, N = q.shape
    blk = ATT_BLOCK
    nb = S // blk

    def body(q_ref, k_ref, v_ref, o_ref, l_ref):
        u_after = _tri(blk, "gt").astype(BF16)

        def qblock(i, _):
            rows = pl.ds(pl.multiple_of(i * blk, blk), blk)
            qb = q_ref[rows, :]

            def kblock(jj, carry):
                acc, c = carry
                j = i - jj
                cols = pl.ds(pl.multiple_of(j * blk, blk), blk)
                z, sp, strict = _sb_scores(qb, k_ref[cols, :], i, j, blk)
                lfm = jnp.where(strict, -sp, 0.0)
                after = _split_mm(lfm, u_after) + c
                w = jnp.where(strict, jnp.exp(z - sp + after), 0.0)
                acc = acc + _mm(w.astype(BF16), v_ref[cols, :])
                return acc, c + jnp.sum(lfm, axis=1, keepdims=True)

            acc, c = lax.fori_loop(0, i + 1, kblock, (jnp.zeros((blk, N), F32), jnp.zeros((blk, 1), F32)))
            o_ref[rows, :] = acc.astype(BF16)
            l_ref[rows, :] = c
            return 0

        lax.fori_loop(0, nb, qblock, 0)

    spec = pl.BlockSpec((None, S, N), lambda g: (g, 0, 0))
    return pl.pallas_call(
        body, name="sb_fwd", grid=(G,), in_specs=[spec, spec, spec],
        out_specs=[spec, pl.BlockSpec((None, S, 1), lambda g: (g, 0, 0))],
        out_shape=[jax.ShapeDtypeStruct((G, S, N), BF16), jax.ShapeDtypeStruct((G, S, 1), F32)],
        compiler_params=_params(("parallel",)),
    )(q, k, v)


def _sb_bwd(q, k, v, do, lsum):
    G, S, N = q.shape
    blk = ATT_BLOCK
    nb = S // blk

    def body(q_ref, k_ref, v_ref, do_ref, l_ref, dq_ref, dk_ref, dv_ref, dk_acc, dv_acc):
        u_incl = _tri(blk, "le").astype(BF16)
        u_excl = _tri(blk, "lt").astype(BF16)
        dk_acc[...] = jnp.zeros_like(dk_acc)
        dv_acc[...] = jnp.zeros_like(dv_acc)

        def qblock(i, _):
            rows = pl.ds(pl.multiple_of(i * blk, blk), blk)
            qb = q_ref[rows, :]
            dob = do_ref[rows, :]
            ltot = l_ref[rows, :]

            def kblock(j, carry):
                dq, p, e_sum = carry
                cols = pl.ds(pl.multiple_of(j * blk, blk), blk)
                kb = k_ref[cols, :]
                vb = v_ref[cols, :]
                z, sp, strict = _sb_scores(qb, kb, i, j, blk)
                lfm = jnp.where(strict, -sp, 0.0)
                after = ltot - p - _split_mm(lfm, u_incl)
                w = jnp.where(strict, jnp.exp(z - sp + after), 0.0)
                e = _mm_nt(dob, vb) * w
                dlf = e_sum + _split_mm(e, u_excl)
                sig = jnp.exp(z - sp)
                dz = jnp.where(strict, e * (1.0 - sig) - dlf * sig, 0.0) * (HEAD_DIM ** -0.5)
                dzb = dz.astype(BF16)
                dv_acc[cols, :] += _mm_tn(w.astype(BF16), dob)
                dk_acc[cols, :] += _mm_tn(dzb, qb)
                return (dq + _mm(dzb, kb), p + jnp.sum(lfm, axis=1, keepdims=True),
                        e_sum + jnp.sum(e, axis=1, keepdims=True))

            zero = jnp.zeros((blk, 1), F32)
            dq, _, _ = lax.fori_loop(0, i + 1, kblock, (jnp.zeros((blk, N), F32), zero, zero))
            dq_ref[rows, :] = dq.astype(BF16)
            return 0

        lax.fori_loop(0, nb, qblock, 0)
        dk_ref[...] = dk_acc[...].astype(BF16)
        dv_ref[...] = dv_acc[...].astype(BF16)

    spec = pl.BlockSpec((None, S, N), lambda g: (g, 0, 0))
    return pl.pallas_call(
        body, name="sb_bwd", grid=(G,),
        in_specs=[spec, spec, spec, spec, pl.BlockSpec((None, S, 1), lambda g: (g, 0, 0))],
        out_specs=[spec, spec, spec],
        out_shape=[jax.ShapeDtypeStruct((G, S, N), BF16)] * 3,
        scratch_shapes=[pltpu.VMEM((S, N), F32), pltpu.VMEM((S, N), F32)],
        compiler_params=_params(("parallel",)),
    )(q, k, v, do, lsum)


@jax.custom_vjp
def _lora_mm(x, w):
    return _mm(x.astype(BF16), w.astype(BF16))


_lora_mm.defvjp(
    lambda x, w: (_mm(x.astype(BF16), w.astype(BF16)), (x, w)),
    lambda res, ct: (_mm_nt(ct.astype(BF16), res[1].astype(BF16)), _mm_tn(res[0].astype(BF16), ct.astype(BF16))))


def _rw_prep_math(p, ps, mu, w0, w_up, a0, a_up, g_up, k_k, k_a):
    pm = p + (ps - p) * mu
    r, k, v = pm[:, :WIDTH], pm[:, WIDTH:2 * WIDTH], pm[:, 2 * WIDTH:3 * WIDTH]
    o = 3 * WIDTH
    xw, xa, xg = pm[:, o:o + W_LORA], pm[:, o + W_LORA:o + W_LORA + A_LORA], pm[:, o + W_LORA + A_LORA:]
    w_raw = w0 + _lora_mm(jnp.tanh(xw), w_up)
    lw = -jnp.exp(-_softplus(-w_raw) - 0.5)
    a = jax.nn.sigmoid(a0 + _lora_mm(xa, a_up))
    g = _lora_mm(jax.nn.sigmoid(xg), g_up)
    kk = k * k_k
    k2 = k * (1.0 + (a - 1.0) * k_a)
    return r, lw, k2, v, kk, a, g


def _shift_down(p, first_row):
    row = lax.broadcasted_iota(jnp.int32, p.shape, 0)
    return jnp.where(row == 0, first_row, pltpu.roll(p, 1, 0))


def _shift_up(p, last_row):
    row = lax.broadcasted_iota(jnp.int32, p.shape, 0)
    return jnp.where(row == p.shape[0] - 1, last_row, pltpu.roll(p, p.shape[0] - 1, 0))


_PREP_PARAM_SHAPES = [(1, RW_COLS), (1, WIDTH), (W_LORA, WIDTH), (1, WIDTH), (A_LORA, WIDTH), (G_LORA, WIDTH),
                      (1, WIDTH), (1, WIDTH)]


def _prev_rows_spec(tm):
    return pl.BlockSpec((8, RW_COLS), lambda i: (jnp.maximum(i * (tm // 8) - 1, 0), 0))


def _rw_prep_fwd(prw, params, seq, tm):
    T = prw.shape[0]

    def body(p_ref, prev_ref, *rest):
        prm = [r_[...] for r_ in rest[:8]]
        outs = rest[8:]
        i = pl.program_id(0)
        first = jnp.where((i * tm) % seq == 0, 0.0, prev_ref[7:8, :])
        p = p_ref[...]
        vals = _rw_prep_math(p, _shift_down(p, first), *prm)
        for o_ref, val in zip(outs, vals):
            o_ref[...] = val

    return pl.pallas_call(
        body, name="rw_prep_fwd", grid=(T // tm,),
        in_specs=[_row_spec(tm, RW_COLS), _prev_rows_spec(tm)] + [_const_spec(s) for s in _PREP_PARAM_SHAPES],
        out_specs=[_row_spec(tm, WIDTH)] * 7,
        out_shape=[jax.ShapeDtypeStruct((T, WIDTH), F32)] * 7,
        compiler_params=_params(("parallel",)),
    )(prw, prw, *params)


def _rw_prep_bwd(prw, params, cts, seq, tm):
    T = prw.shape[0]
    n = T // tm

    def body(p_ref, prev_ref, *rest):
        prm = [r_[...] for r_ in rest[:8]]
        ct = tuple(r_[...] for r_ in rest[8:15])
        dp_ref = rest[15]
        dprm_refs = rest[16:24]
        carry = rest[24]
        step = pl.program_id(0)
        i = n - 1 - step
        first = jnp.where((i * tm) % seq == 0, 0.0, prev_ref[7:8, :])
        p = p_ref[...]
        _, vjp = jax.vjp(_rw_prep_math, p, _shift_down(p, first), *prm)
        grads = vjp(ct)
        dp, dps = grads[0], grads[1]
        nxt = jnp.where(jnp.logical_or(step == 0, ((i + 1) * tm) % seq == 0), 0.0, carry[0:1, :])
        dp_ref[...] = (dp + _shift_up(dps, nxt)).astype(BF16)
        carry[...] = dps[0:8, :]
        for ref, gval in zip(dprm_refs, grads[2:]):
            _acc_out(ref, gval, step == 0)

    rev = lambda w: pl.BlockSpec((tm, w), lambda s: (n - 1 - s, 0))
    prev = pl.BlockSpec((8, RW_COLS), lambda s: (jnp.maximum((n - 1 - s) * (tm // 8) - 1, 0), 0))
    return pl.pallas_call(
        body, name="rw_prep_bwd", grid=(n,),
        in_specs=[rev(RW_COLS), prev] + [_const_spec(s) for s in _PREP_PARAM_SHAPES] + [rev(WIDTH)] * 7,
        out_specs=[rev(RW_COLS)] + [pl.BlockSpec(s, lambda s_: (0, 0)) for s in _PREP_PARAM_SHAPES],
        out_shape=[jax.ShapeDtypeStruct((T, RW_COLS), BF16)] + [jax.ShapeDtypeStruct(s, F32) for s in _PREP_PARAM_SHAPES],
        scratch_shapes=[pltpu.VMEM((8, RW_COLS), F32)],
        compiler_params=_params(("arbitrary",)),
    )(prw, prw, *params, *cts)


def _make_bmm(passes):
    def raw(dn, a, b):
        d = lambda x, y: lax.dot_general(x, y, dn, preferred_element_type=F32)
        ah = a.astype(BF16)
        bh = b.astype(BF16)
        if passes == 1:
            return d(ah, bh)
        al = (a - ah.astype(F32)).astype(BF16)
        bl = (b - bh.astype(F32)).astype(BF16)
        return d(ah, bh) + (d(ah, bl) + d(al, bh))

    dn_nn = (((2,), (1,)), ((0,), (0,)))
    dn_nt = (((2,), (2,)), ((0,), (0,)))
    dn_tn = (((1,), (1,)), ((0,), (0,)))

    @jax.custom_vjp
    def nn(a, b):
        return raw(dn_nn, a, b)

    @jax.custom_vjp
    def nt(a, b):
        return raw(dn_nt, a, b)

    @jax.custom_vjp
    def tn(a, b):
        return raw(dn_tn, a, b)

    nn.defvjp(lambda a, b: (raw(dn_nn, a, b), (a, b)), lambda res, ct: (nt(ct, res[1]), tn(res[0], ct)))
    nt.defvjp(lambda a, b: (raw(dn_nt, a, b), (a, b)), lambda res, ct: (nn(ct, res[1]), tn(ct, res[0])))
    tn.defvjp(lambda a, b: (raw(dn_tn, a, b), (a, b)), lambda res, ct: (nt(res[1], ct), nn(res[0], ct)))

    def unit_lower_inverse(m):
        n = m.shape[-1]
        row = lax.broadcasted_iota(jnp.int32, (n, n), 0)
        col = lax.broadcasted_iota(jnp.int32, (n, n), 1)
        m16 = ((row // 16) == (col // 16)).astype(F32)
        m32 = ((row // 32) == (col // 32)).astype(F32)
        a1 = m * m16
        a2 = nn(a1, a1)
        a4 = nn(a2, a2)
        a8 = nn(a4, a4)
        inv = (row == col).astype(F32) - a1
        inv = inv + nn(inv, a2)
        inv = inv + nn(inv, a4)
        inv = inv + nn(inv, a8)
        inv = inv - nn(nn(inv, m * (m32 - m16)), inv)
        return inv - nn(nn(inv, m * (1.0 - m32)), inv)

    @jax.custom_vjp
    def inverse(m):
        return unit_lower_inverse(m)

    def inverse_fwd(m):
        inv = unit_lower_inverse(m)
        return inv, inv

    inverse.defvjp(inverse_fwd, lambda inv, ct: (-nt(tn(inv, ct), inv),))
    return nn, nt, tn, inverse


def _wkv_chunk(s0, r, lw, k, v, kk, a, lnw, lnb, rk):
    nn, nt, tn, inverse = _make_bmm(SCAN_PASSES)
    G, L, N = r.shape
    rep = lambda t: jnp.broadcast_to(t[None], (G // HEADS, HEADS, 1, N)).reshape(G, 1, N)
    kap = kk * lax.rsqrt(jnp.maximum(jnp.sum(kk * kk, axis=-1, keepdims=True), 1e-24))
    b = a * kap
    row = lax.broadcasted_iota(jnp.int32, (L, L), 0)
    col = lax.broadcasted_iota(jnp.int32, (L, L), 1)
    low_incl = (col <= row).astype(F32)
    low_strict = (col < row).astype(F32)
    c = nn(jnp.broadcast_to(low_incl[None], (G, L, L)), lw)
    c_all = jnp.sum(lw, axis=1, keepdims=True)
    g_inv = jnp.exp(-c)
    kap_t = kap * jnp.exp(c - lw)
    b_t = b * g_inv
    k_t = k * g_inv
    r_t = r * jnp.exp(c)
    g_all = jnp.exp(c_all)
    m_b = nt(kap_t, b_t) * low_strict
    m_k = nt(kap_t, k_t) * low_strict
    n_b = nt(r_t, b_t) * low_incl
    n_k = nt(r_t, k_t) * low_incl
    rhs = -(nt(kap_t, s0) + nn(m_k, v))
    sa = nn(inverse(m_b), rhs)
    y = nt(r_t, s0) + nn(n_b, sa) + nn(n_k, v)
    s1 = s0 * g_all + tn(sa, b_t * g_all) + tn(v, k_t * g_all)
    mean = jnp.mean(y, axis=-1, keepdims=True)
    yc = y - mean
    var = jnp.mean(yc * yc, axis=-1, keepdims=True)
    out = yc * lax.rsqrt(var + GN_EPS) * rep(lnw) + rep(lnb)
    out = out + jnp.sum(r * k * rep(rk), axis=-1, keepdims=True) * v
    return out, s1


def _wkv_fwd(seqs, lnw, lnb, rk):
    G, S, N = seqs[0].shape
    L = SCAN_CHUNK
    nc = S // L

    def body(*refs):
        ins = [r_[...] for r_ in refs[:6]]
        prm = [r_[...] for r_ in refs[6:9]]
        out_ref, st_ref, state = refs[9], refs[10], refs[11]

        @pl.when(pl.program_id(1) == 0)
        def _():
            state[...] = jnp.zeros_like(state)

        s0 = state[...]
        st_ref[...] = s0
        out, s1 = _wkv_chunk(s0, *ins, *prm)
        out_ref[...] = out
        state[...] = s1

    blk = pl.BlockSpec((HEADS, L, N), lambda b, i: (b, i, 0))
    pspec = _const_spec((HEADS, 1, N))
    return pl.pallas_call(
        body, name="wkv_fwd", grid=(G // HEADS, nc), in_specs=[blk] * 6 + [pspec] * 3,
        out_specs=[blk, pl.BlockSpec((None, HEADS, N, N), lambda b, i: (i, b, 0, 0))],
        out_shape=[jax.ShapeDtypeStruct((G, S, N), F32), jax.ShapeDtypeStruct((nc, G, N, N), F32)],
        scratch_shapes=[pltpu.VMEM((HEADS, N, N), F32)],
        compiler_params=_params(("parallel", "arbitrary")),
    )(*seqs, lnw, lnb, rk)


def _wkv_bwd(seqs, states, dout, lnw, lnb, rk):
    G, S, N = seqs[0].shape
    L = SCAN_CHUNK
    nc = S // L

    def body(*refs):
        ins = [r_[...] for r_ in refs[:6]]
        s0 = refs[6][...]
        ct_out = refs[7][...]
        prm = [r_[...] for r_ in refs[8:11]]
        d_refs = refs[11:17]
        dprm_refs = refs[17:20]
        dstate = refs[20]
        step = pl.program_id(1)

        @pl.when(step == 0)
        def _():
            dstate[...] = jnp.zeros_like(dstate)

        _, vjp = jax.vjp(_wkv_chunk, s0, *ins, *prm)
        grads = vjp((ct_out, dstate[...]))
        dstate[...] = grads[0]
        for ref, gval in zip(d_refs, grads[1:7]):
            ref[...] = gval
        for ref, gval in zip(dprm_refs, grads[7:]):
            _acc_out(ref, gval, jnp.logical_and(step == 0, pl.program_id(0) == 0))

    blk = pl.BlockSpec((HEADS, L, N), lambda b, s: (b, nc - 1 - s, 0))
    pspec = _const_spec((HEADS, 1, N))
    pout = pl.BlockSpec((HEADS, 1, N), lambda b, s: (0, 0, 0))
    return pl.pallas_call(
        body, name="wkv_bwd", grid=(G // HEADS, nc),
        in_specs=[blk] * 6 + [pl.BlockSpec((None, HEADS, N, N), lambda b, s: (nc - 1 - s, b, 0, 0)), blk] + [pspec] * 3,
        out_specs=[blk] * 6 + [pout] * 3,
        out_shape=[jax.ShapeDtypeStruct((G, S, N), F32)] * 6 + [jax.ShapeDtypeStruct((HEADS, 1, N), F32)] * 3,
        scratch_shapes=[pltpu.VMEM((HEADS, N, N), F32)],
        compiler_params=_params(("arbitrary", "arbitrary")),
    )(*seqs, states, dout, lnw, lnb, rk)


def _merge_math(o_sb, rw_out, g_rw, gates, w_sb, w_rw, w_o):
    o_rw = (rw_out * g_rw).astype(BF16)
    a = _mm(o_sb, w_sb)
    b = _mm(o_rw, w_rw)
    g1, g2 = gates[:, :D_MODEL], gates[:, D_MODEL:]
    merged = (g1 * a + g2 * b).astype(BF16)
    return o_rw, a, b, g1, g2, merged, _mm(merged, w_o)


def _merge_fwd(x2, o_sb, rw_out, g_rw, gates, w_sb, w_rw, w_o, g_post, tm):
    T = x2.shape[0]

    def body(x_ref, osb_ref, rw_ref, g_ref, gate_ref, wsb_ref, wrw_ref, wo_ref, gp_ref, x1_ref):
        z = _merge_math(osb_ref[...], rw_ref[...], g_ref[...], gate_ref[...], wsb_ref[...], wrw_ref[...], wo_ref[...])[-1]
        x1_ref[...] = x_ref[...] + _rms_fwd(z, gp_ref[...])[0]

    return pl.pallas_call(
        body, name="merge_fwd", grid=(T // tm,),
        in_specs=[_row_spec(tm, D_MODEL), _row_spec(tm, WIDTH), _row_spec(tm, WIDTH), _row_spec(tm, WIDTH),
                  _row_spec(tm, GATE_COLS), _const_spec((WIDTH, D_MODEL)), _const_spec((WIDTH, D_MODEL)),
                  _const_spec((D_MODEL, D_MODEL)), _const_spec((1, D_MODEL))],
        out_specs=_row_spec(tm, D_MODEL),
        out_shape=jax.ShapeDtypeStruct((T, D_MODEL), F32),
        compiler_params=_params(("parallel",)),
    )(x2, o_sb, rw_out, g_rw, gates, w_sb, w_rw, w_o, g_post)


def _merge_bwd(dx1, o_sb, rw_out, g_rw, gates, w_sb, w_rw, w_o, g_post, tm):
    T = dx1.shape[0]

    def body(dx1_ref, osb_ref, rw_ref, g_ref, gate_ref, wsb_ref, wrw_ref, wo_ref, gp_ref,
             orw_o, mrg_o, dz_o, da_o, db_o, dgate_o, dosb_o, drw_o, dg_o, dgp_o, dbg_o):
        rw_out_v, g_rw_v = rw_ref[...], g_ref[...]
        w_sb_v, w_rw_v, w_o_v = wsb_ref[...], wrw_ref[...], wo_ref[...]
        o_rw, a, b, g1, g2, merged, z = _merge_math(osb_ref[...], rw_out_v, g_rw_v, gate_ref[...], w_sb_v, w_rw_v, w_o_v)
        gain = gp_ref[...]
        _, zn, rstd = _rms_fwd(z, gain)
        dz, dgain = _rms_bwd(dx1_ref[...], zn, rstd, gain)
        dzb = dz.astype(BF16)
        dm = _mm_nt(dzb, w_o_v)
        dab = (dm * g1).astype(BF16)
        dbb = (dm * g2).astype(BF16)
        dgate = jnp.concatenate([dm * a * g1 * (1.0 - g1), dm * b * g2 * (1.0 - g2)], axis=1)
        do_rw = _mm_nt(dbb, w_rw_v)
        orw_o[...] = o_rw
        mrg_o[...] = merged
        dz_o[...] = dzb
        da_o[...] = dab
        db_o[...] = dbb
        dgate_o[...] = dgate.astype(BF16)
        dosb_o[...] = _mm_nt(dab, w_sb_v).astype(BF16)
        drw_o[...] = do_rw * g_rw_v
        dg_o[...] = do_rw * rw_out_v
        first = pl.program_id(0) == 0
        _acc_out(dgp_o, dgain, first)
        _acc_out(dbg_o, jnp.sum(dgate, axis=0, keepdims=True), first)

    acc = lambda n: pl.BlockSpec((1, n), lambda i: (0, 0))
    sd = jax.ShapeDtypeStruct
    return pl.pallas_call(
        body, name="merge_bwd", grid=(T // tm,),
        in_specs=[_row_spec(tm, D_MODEL), _row_spec(tm, WIDTH), _row_spec(tm, WIDTH), _row_spec(tm, WIDTH),
                  _row_spec(tm, GATE_COLS), _const_spec((WIDTH, D_MODEL)), _const_spec((WIDTH, D_MODEL)),
                  _const_spec((D_MODEL, D_MODEL)), _const_spec((1, D_MODEL))],
        out_specs=[_row_spec(tm, WIDTH), _row_spec(tm, D_MODEL), _row_spec(tm, D_MODEL), _row_spec(tm, D_MODEL),
                   _row_spec(tm, D_MODEL), _row_spec(tm, GATE_COLS), _row_spec(tm, WIDTH), _row_spec(tm, WIDTH),
                   _row_spec(tm, WIDTH), acc(D_MODEL), acc(GATE_COLS)],
        out_shape=[sd((T, WIDTH), BF16), sd((T, D_MODEL), BF16), sd((T, D_MODEL), BF16), sd((T, D_MODEL), BF16),
                   sd((T, D_MODEL), BF16), sd((T, GATE_COLS), BF16), sd((T, WIDTH), BF16), sd((T, WIDTH), F32),
                   sd((T, WIDTH), F32), sd((1, D_MODEL), F32), sd((1, GATE_COLS), F32)],
        compiler_params=_params(("arbitrary",)),
    )(dx1, o_sb, rw_out, g_rw, gates, w_sb, w_rw, w_o, g_post)


def _ffn(x1, target, g_pre, g_post, w_gate, w_up, w_down, tm):
    T = x1.shape[0]

    def body(x1_ref, tgt_ref, gpre_ref, gpost_ref, wg_ref, wu_ref, wd_ref,
             loss_o, dx1_o, h_o, dgate_o, dup_o, act_o, df_o, dgpre_o, dgpost_o):
        x1v = x1_ref[...]
        gpre, gpost = gpre_ref[...], gpost_ref[...]
        wg, wu, wd = wg_ref[...], wu_ref[...], wd_ref[...]
        hn, xn1, rstd1 = _rms_fwd(x1v, gpre)
        h = hn.astype(BF16)
        gate = _mm(h, wg)
        up = _mm(h, wu)
        sg = jax.nn.sigmoid(gate)
        act = (gate * sg * up).astype(BF16)
        f = _mm(act, wd)
        fo, fn, rstd2 = _rms_fwd(f, gpost)
        diff = x1v + fo - tgt_ref[...]
        dy = diff * (1.0 / D_MODEL)
        df, dgpost = _rms_bwd(dy, fn, rstd2, gpost)
        dfb = df.astype(BF16)
        dact = _mm_nt(dfb, wd)
        dup = (dact * gate * sg).astype(BF16)
        dgate = (dact * up * (sg * (1.0 + gate * (1.0 - sg)))).astype(BF16)
        dh = _mm_nt(dgate, wg) + _mm_nt(dup, wu)
        dxn, dgpre = _rms_bwd(dh, xn1, rstd1, gpre)
        dx1_o[...] = dy + dxn
        h_o[...] = h
        dgate_o[...] = dgate
        dup_o[...] = dup
        act_o[...] = act
        df_o[...] = dfb
        first = pl.program_id(0) == 0
        part = jnp.sum(jnp.sum(diff * diff, axis=1, keepdims=True), axis=0, keepdims=True) * (0.5 / D_MODEL)
        _acc_out(loss_o, jnp.broadcast_to(part, (8, 128)), first)
        _acc_out(dgpre_o, dgpre, first)
        _acc_out(dgpost_o, dgpost, first)

    acc = lambda r, n: pl.BlockSpec((r, n), lambda i: (0, 0))
    sd = jax.ShapeDtypeStruct
    return pl.pallas_call(
        body, name="ffn", grid=(T // tm,),
        in_specs=[_row_spec(tm, D_MODEL), _row_spec(tm, D_MODEL), _const_spec((1, D_MODEL)), _const_spec((1, D_MODEL)),
                  _const_spec((D_MODEL, D_FF)), _const_spec((D_MODEL, D_FF)), _const_spec((D_FF, D_MODEL))],
        out_specs=[acc(8, 128), _row_spec(tm, D_MODEL), _row_spec(tm, D_MODEL), _row_spec(tm, D_FF), _row_spec(tm, D_FF),
                   _row_spec(tm, D_FF), _row_spec(tm, D_MODEL), acc(1, D_MODEL), acc(1, D_MODEL)],
        out_shape=[sd((8, 128), F32), sd((T, D_MODEL), F32), sd((T, D_MODEL), BF16), sd((T, D_FF), BF16),
                   sd((T, D_FF), BF16), sd((T, D_FF), BF16), sd((T, D_MODEL), BF16), sd((1, D_MODEL), F32),
                   sd((1, D_MODEL), F32)],
        compiler_params=_params(("arbitrary",)),
    )(x1, target, g_pre, g_post, w_gate, w_up, w_down)


def _to_heads(t, bl, seq):
    return t.reshape(bl, seq, HEADS, HEAD_DIM).transpose(0, 2, 1, 3).reshape(bl * HEADS, seq, HEAD_DIM)


def _from_heads(t, bl, seq):
    return t.reshape(bl, HEADS, seq, HEAD_DIM).transpose(0, 2, 1, 3).reshape(bl * seq, WIDTH)


def _local_step(x, target, sm, wt):
    bl, seq, _ = x.shape
    T = bl * seq
    tm = min(ROW_TILE, T)
    x2 = x.reshape(T, D_MODEL)
    tgt2 = target.reshape(T, D_MODEL)
    w_qkv, w_prw, w_gate = wt["w_in"][:, :SB_COLS], wt["w_in"][:, SB_COLS:SB_COLS + RW_COLS], wt["w_in"][:, SB_COLS + RW_COLS:]
    h, qkv, prw, gates = _in_proj_fwd(x2, sm["norm_mix_pre"], w_qkv, w_prw, w_gate, sm["b_gate"], tm)
    q, k, v = (_to_heads(qkv[:, i * WIDTH:(i + 1) * WIDTH], bl, seq) for i in range(3))
    o_sb_h, lsum = _sb_fwd(q, k, v)
    o_sb = _from_heads(o_sb_h, bl, seq)
    prep_params = [sm["mu_rw"], sm["w0"], wt["w_up"].astype(F32), sm["a0"], wt["a_up"].astype(F32),
                   wt["g_up"].astype(F32), sm["k_k"], sm["k_a"]]
    prep = _rw_prep_fwd(prw, prep_params, seq, tm)
    seqs = [_to_heads(t, bl, seq) for t in prep[:6]]
    g_rw = prep[6]
    lnw, lnb, rk = (sm[n].reshape(HEADS, 1, HEAD_DIM) for n in ("lnx_w", "lnx_b", "r_k"))
    rw_out_h, states = _wkv_fwd(seqs, lnw, lnb, rk)
    rw_out = _from_heads(rw_out_h, bl, seq)
    x1 = _merge_fwd(x2, o_sb, rw_out, g_rw, gates, wt["w_sb_out"], wt["w_rw_out"], wt["w_o"], sm["norm_mix_post"], tm)
    (loss_part, dx1, h2, dffg, dffu, act, dff, d_nfpre, d_nfpost) = _ffn(
        x1, tgt2, sm["norm_ffn_pre"], sm["norm_ffn_post"], wt["w_ffn_gate"], wt["w_ffn_up"], wt["w_ffn_down"], tm)
    (o_rw, merged, dz, da, db, dgate, do_sb, d_rw_out, d_g_rw, d_npost, d_bgate) = _merge_bwd(
        dx1, o_sb, rw_out, g_rw, gates, wt["w_sb_out"], wt["w_rw_out"], wt["w_o"], sm["norm_mix_post"], tm)
    dq, dk, dv = _sb_bwd(q, k, v, _to_heads(do_sb, bl, seq), lsum)
    dqkv = jnp.concatenate([_from_heads(t, bl, seq) for t in (dq, dk, dv)], axis=1)
    wkv_g = _wkv_bwd(seqs, states, _to_heads(d_rw_out, bl, seq), lnw, lnb, rk)
    cts = [_from_heads(t, bl, seq) for t in wkv_g[:6]] + [d_g_rw]
    prep_g = _rw_prep_bwd(prw, prep_params, cts, seq, tm)
    dprw = prep_g[0]
    d_mu, d_w0, d_wup, d_a0, d_aup, d_gup, d_kk, d_ka = prep_g[1:]
    grad_x, d_npre = _in_proj_bwd(x2, sm["norm_mix_pre"], dx1, dqkv, dprw, dgate, w_qkv, w_prw, w_gate, tm)
    gw = {
        "w_in": jnp.concatenate([_grad_w(h, dqkv, "gw_in_qkv"), _grad_w(h, dprw, "gw_in_rw"), _grad_w(h, dgate, "gw_in_gate")], axis=1),
        "w_up": d_wup, "a_up": d_aup, "g_up": d_gup,
        "w_sb_out": _grad_w(o_sb, da, "gw_sb_out"), "w_rw_out": _grad_w(o_rw, db, "gw_rw_out"),
        "w_o": _grad_w(merged, dz, "gw_o"),
        "w_ffn_gate": _grad_w(h2, dffg, "gw_ffn_gate"), "w_ffn_up": _grad_w(h2, dffu, "gw_ffn_up"),
        "w_ffn_down": _grad_w(act, dff, "gw_ffn_down"),
    }
    gs = {
        "norm_mix_pre": d_npre, "b_gate": d_bgate, "mu_rw": d_mu, "w0": d_w0, "a0": d_a0, "k_k": d_kk, "k_a": d_ka,
        "r_k": wkv_g[8].reshape(1, WIDTH), "lnx_w": wkv_g[6].reshape(1, WIDTH), "lnx_b": wkv_g[7].reshape(1, WIDTH),
        "norm_mix_post": d_npost, "norm_ffn_pre": d_nfpre, "norm_ffn_post": d_nfpost,
    }
    return loss_part, grad_x.reshape(x.shape), gw, gs


_SHARDED = [("w_in", 1, (D_MODEL, (SB_COLS + RW_COLS + GATE_COLS) // N_DEV)), ("w_up", 1, (W_LORA, WIDTH // N_DEV)),
            ("a_up", 1, (A_LORA, WIDTH // N_DEV)), ("g_up", 1, (G_LORA, WIDTH // N_DEV)),
            ("w_sb_out", 1, (WIDTH, D_MODEL // N_DEV)), ("w_rw_out", 1, (WIDTH, D_MODEL // N_DEV)),
            ("w_o", 0, (D_MODEL // N_DEV, D_MODEL)), ("w_ffn_gate", 1, (D_MODEL, D_FF // N_DEV)),
            ("w_ffn_up", 1, (D_MODEL, D_FF // N_DEV)), ("w_ffn_down", 0, (D_FF // N_DEV, D_MODEL))]
_LANES = 128
_PACK_ROWS = [s[0] * s[1] // _LANES for _, _, s in _SHARDED]
_PACK_TOTAL = sum(_PACK_ROWS)
_SMALL = [("norm_mix_pre", D_MODEL), ("b_gate", GATE_COLS), ("mu_rw", RW_COLS), ("w0", WIDTH), ("a0", WIDTH),
          ("k_k", WIDTH), ("k_a", WIDTH), ("r_k", WIDTH), ("lnx_w", WIDTH), ("lnx_b", WIDTH),
          ("norm_mix_post", D_MODEL), ("norm_ffn_pre", D_MODEL), ("norm_ffn_post", D_MODEL)]
_SMALL_ROWS = 96


def _pack_shards(shards, dtype):
    return jnp.concatenate([shards[n].astype(dtype).reshape(-1, _LANES) for n, _, _ in _SHARDED], axis=0)


def _unpack_shards(packed):
    out, r0 = {}, 0
    for (n, _, shp), rows in zip(_SHARDED, _PACK_ROWS):
        out[n] = packed[r0:r0 + rows].reshape(shp)
        r0 += rows
    return out


def _unpack_gathered(g):
    out, r0 = {}, 0
    for (n, axis, shp), rows in zip(_SHARDED, _PACK_ROWS):
        blk = g[:, r0:r0 + rows].reshape((N_DEV,) + shp)
        out[n] = blk.reshape(N_DEV * shp[0], shp[1]) if axis == 0 else blk.transpose(1, 0, 2).reshape(shp[0], N_DEV * shp[1])
        r0 += rows
    return out


def _pack_full_grads(gw):
    parts = []
    for n, axis, shp in _SHARDED:
        g = gw[n]
        blk = g.reshape((N_DEV,) + shp) if axis == 0 else g.reshape(shp[0], N_DEV, shp[1]).transpose(1, 0, 2)
        parts.append(blk.reshape(N_DEV, -1, _LANES))
    return jnp.concatenate(parts, axis=1)


def _pack_small(vals, extra=None):
    flat = [vals[n].reshape(-1) for n, _ in _SMALL]
    used = sum(sz for _, sz in _SMALL)
    tail = jnp.zeros((_SMALL_ROWS * _LANES - used,), F32)
    if extra is not None:
        tail = tail.at[0].set(extra)
    return jnp.concatenate(flat + [tail]).reshape(_SMALL_ROWS, _LANES)


def _unpack_small(packed):
    flat, out, o = packed.reshape(-1), {}, 0
    for n, sz in _SMALL:
        out[n] = flat[o:o + sz]
        o += sz
    return out, flat[o]


_ANY = pl.BlockSpec(memory_space=pl.ANY)


def _all_gather(block):
    rows, lanes = block.shape

    def body(x_ref, out_ref, send_sems, recv_sems, local_sem):
        x, y, c = lax.axis_index("x"), lax.axis_index("y"), lax.axis_index("c")
        me, sibling = (x, y, c), (x, y, 1 - c)
        chips = [(1 - x, y), (x, 1 - y), (1 - x, 1 - y)]

        def slot(px, py, pc):
            return out_ref.at[4 * px + 2 * py + pc]

        def copy(k, blk, to, src=None):
            return pltpu.make_async_remote_copy(
                src_ref=slot(*blk) if src is None else src, dst_ref=slot(*blk),
                send_sem=send_sems.at[k], recv_sem=recv_sems.at[k], device_id=to, device_id_type=MESH)

        mine = pltpu.make_async_copy(x_ref, slot(*me), local_sem)
        mine.start()
        first = [copy(0, me, sibling, src=x_ref)]
        first += [copy(1 + j, me, (*chip, c), src=x_ref) for j, chip in enumerate(chips)]
        for cp in first:
            cp.start()
        passed = [copy(4 + j, (*chip, c), sibling) for j, chip in enumerate(chips)]
        for j, chip in enumerate(chips):
            copy(1 + j, (*chip, c), me).wait_recv()
            passed[j].start()
        copy(0, sibling, me).wait_recv()
        for j, chip in enumerate(chips):
            copy(4 + j, (*chip, 1 - c), me).wait_recv()
        for cp in first + passed:
            cp.wait_send()
        mine.wait()

    return pl.pallas_call(
        body, name="all_gather_weights", in_specs=[_ANY], out_specs=_ANY,
        out_shape=jax.ShapeDtypeStruct((N_DEV, rows, lanes), block.dtype),
        scratch_shapes=[pltpu.SemaphoreType.DMA((7,)), pltpu.SemaphoreType.DMA((7,)), pltpu.SemaphoreType.DMA],
    )(block)


def _exchange_core(pack, small):
    _, _, rows, lanes = pack.shape

    def body(pack_ref, small_ref, got_ref, parts_ref, send_sems, recv_sems, s_send, s_recv, local_sem):
        x, y, c = lax.axis_index("x"), lax.axis_index("y"), lax.axis_index("c")
        sibling = (x, y, 1 - c)
        me = 4 * x + 2 * y + c
        mine = pltpu.make_async_copy(small_ref, parts_ref.at[me], local_sem)
        mine.start()
        big = [pltpu.make_async_remote_copy(
            src_ref=pack_ref.at[1 - c, j], dst_ref=got_ref.at[j], send_sem=send_sems.at[j], recv_sem=recv_sems.at[j],
            device_id=sibling, device_id_type=MESH) for j in range(4)]
        for cp in big:
            cp.start()
        others = [(k, (x ^ (k >> 2), y ^ ((k >> 1) & 1), c ^ (k & 1))) for k in range(1, N_DEV)]
        tiny = [pltpu.make_async_remote_copy(
            src_ref=small_ref, dst_ref=parts_ref.at[me], send_sem=s_send.at[k], recv_sem=s_recv.at[k],
            device_id=to, device_id_type=MESH) for k, to in others]
        for cp in tiny:
            cp.start()
        for cp in big:
            cp.wait_recv()
        for (k, (px, py, pc)), cp in zip(others, tiny):
            pltpu.make_async_remote_copy(
                src_ref=small_ref, dst_ref=parts_ref.at[4 * px + 2 * py + pc], send_sem=s_send.at[k],
                recv_sem=s_recv.at[k], device_id=(px, py, pc), device_id_type=MESH).wait_recv()
        for cp in big + tiny:
            cp.wait_send()
        mine.wait()

    return pl.pallas_call(
        body, name="exchange_core", in_specs=[_ANY, _ANY], out_specs=[_ANY, _ANY],
        out_shape=[jax.ShapeDtypeStruct((4, rows, lanes), F32), jax.ShapeDtypeStruct((N_DEV,) + small.shape, F32)],
        scratch_shapes=[pltpu.SemaphoreType.DMA((4,)), pltpu.SemaphoreType.DMA((4,)), pltpu.SemaphoreType.DMA((N_DEV,)),
                        pltpu.SemaphoreType.DMA((N_DEV,)), pltpu.SemaphoreType.DMA],
    )(pack, small)


def _add_core_parts(pack, got, core):
    _, _, rows, lanes = pack.shape
    tr = 2000

    def body(core_ref, a_ref, b_ref, o_ref):
        o_ref[...] = a_ref[...] + b_ref[...]

    return pl.pallas_call(
        body, name="add_core_parts",
        grid_spec=pltpu.PrefetchScalarGridSpec(
            num_scalar_prefetch=1, grid=(4, rows // tr),
            in_specs=[pl.BlockSpec((None, None, tr, lanes), lambda j, i, core_ref: (core_ref[0], j, i, 0)),
                      pl.BlockSpec((None, tr, lanes), lambda j, i, core_ref: (j, i, 0))],
            out_specs=pl.BlockSpec((None, tr, lanes), lambda j, i, core_ref: (j, i, 0))),
        out_shape=jax.ShapeDtypeStruct((4, rows, lanes), F32),
        compiler_params=_params(("parallel", "parallel")),
    )(core, pack, got)


def _exchange_chips(chip_sums):
    _, rows, lanes = chip_sums.shape

    def body(src_ref, got_ref, send_sems, recv_sems):
        x, y, c = lax.axis_index("x"), lax.axis_index("y"), lax.axis_index("c")
        flips = [(1, 0), (0, 1), (1, 1)]
        copies = []
        for k, (fx, fy) in enumerate(flips):
            px, py = x ^ fx, y ^ fy
            copies.append(pltpu.make_async_remote_copy(
                src_ref=src_ref.at[2 * px + py], dst_ref=got_ref.at[k], send_sem=send_sems.at[k],
                recv_sem=recv_sems.at[k], device_id=(px, py, c), device_id_type=MESH))
        for cp in copies:
            cp.start()
        for cp in copies:
            cp.wait_recv()
        for cp in copies:
            cp.wait_send()

    return pl.pallas_call(
        body, name="exchange_chips", in_specs=[_ANY], out_specs=_ANY,
        out_shape=jax.ShapeDtypeStruct((3, rows, lanes), F32),
        scratch_shapes=[pltpu.SemaphoreType.DMA((3,)), pltpu.SemaphoreType.DMA((3,))],
    )(chip_sums)


def _sum_chip_parts(chip_sums, got, chip):
    _, rows, lanes = chip_sums.shape
    tr = 2000

    def body(chip_ref, own_ref, got_ref, o_ref):
        o_ref[...] = ((own_ref[...] + got_ref[0]) + got_ref[1]) + got_ref[2]

    return pl.pallas_call(
        body, name="sum_chip_parts",
        grid_spec=pltpu.PrefetchScalarGridSpec(
            num_scalar_prefetch=1, grid=(rows // tr,),
            in_specs=[pl.BlockSpec((None, tr, lanes), lambda i, chip_ref: (chip_ref[0], i, 0)),
                      pl.BlockSpec((3, tr, lanes), lambda i, chip_ref: (0, i, 0))],
            out_specs=pl.BlockSpec((tr, lanes), lambda i, chip_ref: (i, 0))),
        out_shape=jax.ShapeDtypeStruct((rows, lanes), F32),
        compiler_params=_params(("parallel",)),
    )(chip, chip_sums, got)


def _adamw_math(w, g, m, v):
    m = ADAM_B1 * m + (1.0 - ADAM_B1) * g
    v = ADAM_B2 * v + (1.0 - ADAM_B2) * (g * g)
    m_hat = m / (1.0 - ADAM_B1 ** ADAM_STEP)
    v_hat = v / (1.0 - ADAM_B2 ** ADAM_STEP)
    return -ADAM_LR * (m_hat / (jnp.sqrt(v_hat) + ADAM_EPS) + ADAM_WD * w), m, v


def _adamw(w, g, m, v, name):
    rows, cols = w.shape
    tr = 256 if rows % 256 == 0 and rows * cols > 2 ** 19 else rows

    def body(w_ref, g_ref, m_ref, v_ref, d_o, m_o, v_o):
        d_o[...], m_o[...], v_o[...] = _adamw_math(w_ref[...], g_ref[...], m_ref[...], v_ref[...])

    spec = pl.BlockSpec((tr, cols), lambda i: (i, 0))
    return pl.pallas_call(
        body, name=name, grid=(rows // tr,), in_specs=[spec] * 4, out_specs=[spec] * 3,
        out_shape=[jax.ShapeDtypeStruct((rows, cols), F32)] * 3, compiler_params=_params(("parallel",)),
    )(w, g, m, v)


def _adamw_small(parts, w, m, v):
    def body(p_ref, w_ref, m_ref, v_ref, g_o, d_o, m_o, v_o):
        g = p_ref[0]
        for d in range(1, N_DEV):
            g = g + p_ref[d]
        g_o[...] = g
        d_o[...], m_o[...], v_o[...] = _adamw_math(w_ref[...], g, m_ref[...], v_ref[...])

    return pl.pallas_call(
        body, name="adamw_small", out_shape=[jax.ShapeDtypeStruct(w.shape, F32)] * 4, compiler_params=_params(),
    )(parts, w, m, v)


_WEIGHT_NAMES = ['norm_mix_pre', 'w_in', 'b_gate', 'mu_rw', 'w0', 'w_up', 'a0', 'a_up', 'g_up', 'k_k', 'k_a', 'r_k',
                 'lnx_w', 'lnx_b', 'w_sb_out', 'w_rw_out', 'w_o', 'norm_mix_post', 'norm_ffn_pre', 'w_ffn_gate',
                 'w_ffn_up', 'w_ffn_down', 'norm_ffn_post']


def _step(x, target, w, m, v):
    sharded = [n for n, _, _ in _SHARDED]
    sm = {n: w[n].reshape(1, -1) for n, _ in _SMALL}
    own = {n: w[n][0] for n in sharded}
    wt = _unpack_gathered(_all_gather(_pack_shards(own, BF16)))
    loss_part, grad_x, gw, gs = _local_step(x, target, sm, wt)

    cx, cy, cc = lax.axis_index("x"), lax.axis_index("y"), lax.axis_index("c")
    core = jnp.reshape(cc, (1,)).astype(jnp.int32)
    chip = jnp.reshape(2 * cx + cy, (1,)).astype(jnp.int32)
    pack = _pack_full_grads(gw).reshape(4, 2, _PACK_TOTAL, _LANES).transpose(1, 0, 2, 3)
    got_core, small_parts = _exchange_core(pack, _pack_small(gs, loss_part[0, 0]))
    chip_sums = _add_core_parts(pack, got_core, core)
    grads_packed = _sum_chip_parts(chip_sums, _exchange_chips(chip_sums), chip)
    g_sh = _unpack_shards(grads_packed)

    g_small, d_small, m_small, v_small = _adamw_small(
        small_parts, _pack_small({n: w[n] for n, _ in _SMALL}), _pack_small({n: m[n] for n, _ in _SMALL}),
        _pack_small({n: v[n] for n, _ in _SMALL}))
    (g_s, loss), (d_s, _), (m_s, _), (v_s, _) = (_unpack_small(t) for t in (g_small, d_small, m_small, v_small))

    grads, deltas, new_m, new_v = {}, {}, {}, {}
    for n in _WEIGHT_NAMES:
        if n in g_sh:
            d_, m_, v_ = _adamw(own[n], g_sh[n], m[n][0], v[n][0], "adamw_" + n)
            grads[n], deltas[n], new_m[n], new_v[n] = (t.reshape(w[n].shape) for t in (g_sh[n], d_, m_, v_))
        else:
            grads[n], deltas[n], new_m[n], new_v[n] = (t[n].reshape(w[n].shape) for t in (g_s, d_s, m_s, v_s))
    return (loss, grad_x, *[grads[n] for n in _WEIGHT_NAMES], *[deltas[n] for n in _WEIGHT_NAMES],
            *[new_m[n] for n in _WEIGHT_NAMES], *[new_v[n] for n in _WEIGHT_NAMES])


def kernel(x, norm_mix_pre, w_in, b_gate, mu_rw, w0, w_up, a0, a_up, g_up, k_k, k_a, r_k, lnx_w, lnx_b, w_sb_out, w_rw_out, w_o, norm_mix_post, norm_ffn_pre, w_ffn_gate, w_ffn_up, w_ffn_down, norm_ffn_post, loss_target, m_norm_mix_pre, m_w_in, m_b_gate, m_mu_rw, m_w0, m_w_up, m_a0, m_a_up, m_g_up, m_k_k, m_k_a, m_r_k, m_lnx_w, m_lnx_b, m_w_sb_out, m_w_rw_out, m_w_o, m_norm_mix_post, m_norm_ffn_pre, m_w_ffn_gate, m_w_ffn_up, m_w_ffn_down, m_norm_ffn_post, v_norm_mix_pre, v_w_in, v_b_gate, v_mu_rw, v_w0, v_w_up, v_a0, v_a_up, v_g_up, v_k_k, v_k_a, v_r_k, v_lnx_w, v_lnx_b, v_w_sb_out, v_w_rw_out, v_w_o, v_norm_mix_post, v_norm_ffn_pre, v_w_ffn_gate, v_w_ffn_up, v_w_ffn_down, v_norm_ffn_post):
    args = locals()
    w = {n: args[n] for n in _WEIGHT_NAMES}
    m = {n: args["m_" + n] for n in _WEIGHT_NAMES}
    v = {n: args["v_" + n] for n in _WEIGHT_NAMES}
    return _step(x, loss_target, w, m, v)
```

```python
import functools

import jax
import jax.numpy as jnp
from jax import lax
from jax.experimental import pallas as pl
from jax.experimental.pallas import tpu as pltpu

F32 = jnp.float32
BF16 = jnp.bfloat16

D_MODEL = 1024
HEADS = 8
HEAD_DIM = 64
WIDTH = HEADS * HEAD_DIM
W_LORA, A_LORA, G_LORA = 64, 64, 128
SB_COLS = 3 * WIDTH
RW_COLS = 3 * WIDTH + W_LORA + A_LORA + G_LORA
GATE_COLS = 2 * D_MODEL
D_FF = 2816
RMS_EPS = 1e-6
GN_EPS = HEAD_DIM * 1e-5
N_DEV = 8

ADAM_LR, ADAM_B1, ADAM_B2, ADAM_EPS, ADAM_WD, ADAM_STEP = 0.001, 0.9, 0.999, 1e-08, 0.01, 10

ROW_TILE = 256
SCAN_CHUNK = 64
ATT_BLOCK = 128
ATT_Q = 256
SB_DEAD = -104.0
SCAN_PASSES = 1
VMEM_LIMIT = 56 * 2 ** 20

MESH = pl.DeviceIdType.MESH


def _params(sem=None, vmem=VMEM_LIMIT):
    kw = dict(vmem_limit_bytes=vmem)
    if sem is not None:
        kw["dimension_semantics"] = sem
    return pltpu.CompilerParams(**kw)


def _const_spec(shape):
    nd = len(shape)
    return pl.BlockSpec(shape, lambda *_: (0,) * nd, pipeline_mode=pl.Buffered(1))


def _row_spec(tm, n):
    return pl.BlockSpec((tm, n), lambda i: (i, 0))


def _mm(a, b):
    return lax.dot_general(a, b, (((1,), (0,)), ((), ())), preferred_element_type=F32)


def _mm_nt(a, b):
    return lax.dot_general(a, b, (((1,), (1,)), ((), ())), preferred_element_type=F32)


def _mm_tn(a, b):
    return lax.dot_general(a, b, (((0,), (0,)), ((), ())), preferred_element_type=F32)


def _softplus(z):
    return jnp.maximum(z, 0.0) + jnp.log1p(jnp.exp(-jnp.abs(z)))


def _rms_fwd(x, gain):
    rstd = lax.rsqrt(jnp.mean(x * x, axis=-1, keepdims=True) + RMS_EPS)
    xn = x * rstd
    return xn * gain, xn, rstd


def _rms_bwd(dy, xn, rstd, gain):
    u = dy * gain
    dx = rstd * (u - xn * jnp.mean(u * xn, axis=-1, keepdims=True))
    return dx, jnp.sum(dy * xn, axis=0, keepdims=True)


def _acc_out(ref, val, first):
    @pl.when(first)
    def _():
        ref[...] = val

    @pl.when(jnp.logical_not(first))
    def _():
        ref[...] += val


def _in_proj_fwd(x2, g_pre, w_qkv, w_rw, w_gate, b_gate, tm):
    T = x2.shape[0]

    def body(x_ref, g_ref, wq_ref, wr_ref, wg_ref, b_ref, h_ref, qkv_ref, prw_ref, gate_ref):
        h = _rms_fwd(x_ref[...], g_ref[...])[0].astype(BF16)
        h_ref[...] = h
        qkv_ref[...] = _mm(h, wq_ref[...]).astype(BF16)
        prw_ref[...] = _mm(h, wr_ref[...])
        gate_ref[...] = jax.nn.sigmoid(_mm(h, wg_ref[...]) + b_ref[...])

    return pl.pallas_call(
        body, name="in_proj_fwd", grid=(T // tm,),
        in_specs=[_row_spec(tm, D_MODEL), _const_spec((1, D_MODEL)), _const_spec((D_MODEL, SB_COLS)),
                  _const_spec((D_MODEL, RW_COLS)), _const_spec((D_MODEL, GATE_COLS)), _const_spec((1, GATE_COLS))],
        out_specs=[_row_spec(tm, D_MODEL), _row_spec(tm, SB_COLS), _row_spec(tm, RW_COLS), _row_spec(tm, GATE_COLS)],
        out_shape=[jax.ShapeDtypeStruct((T, D_MODEL), BF16), jax.ShapeDtypeStruct((T, SB_COLS), BF16),
                   jax.ShapeDtypeStruct((T, RW_COLS), F32), jax.ShapeDtypeStruct((T, GATE_COLS), F32)],
        compiler_params=_params(("parallel",)),
    )(x2, g_pre, w_qkv, w_rw, w_gate, b_gate)


def _in_proj_bwd(x2, g_pre, dx1, dqkv, dprw, dgate, w_qkv, w_rw, w_gate, tm):
    T = x2.shape[0]

    def body(x_ref, g_ref, dx1_ref, dq_ref, dr_ref, dg_ref, wq_ref, wr_ref, wg_ref, gx_ref, dgain_ref):
        dh = _mm_nt(dq_ref[...], wq_ref[...]) + _mm_nt(dr_ref[...], wr_ref[...]) + _mm_nt(dg_ref[...], wg_ref[...])
        gain = g_ref[...]
        _, xn, rstd = _rms_fwd(x_ref[...], gain)
        dx, dgain = _rms_bwd(dh, xn, rstd, gain)
        gx_ref[...] = dx1_ref[...] + dx
        _acc_out(dgain_ref, dgain, pl.program_id(0) == 0)

    return pl.pallas_call(
        body, name="in_proj_bwd", grid=(T // tm,),
        in_specs=[_row_spec(tm, D_MODEL), _const_spec((1, D_MODEL)), _row_spec(tm, D_MODEL), _row_spec(tm, SB_COLS),
                  _row_spec(tm, RW_COLS), _row_spec(tm, GATE_COLS), _const_spec((D_MODEL, SB_COLS)),
                  _const_spec((D_MODEL, RW_COLS)), _const_spec((D_MODEL, GATE_COLS))],
        out_specs=[_row_spec(tm, D_MODEL), pl.BlockSpec((1, D_MODEL), lambda i: (0, 0))],
        out_shape=[jax.ShapeDtypeStruct((T, D_MODEL), F32), jax.ShapeDtypeStruct((1, D_MODEL), F32)],
        compiler_params=_params(("arbitrary",)),
    )(x2, g_pre, dx1, dqkv, dprw, dgate, w_qkv, w_rw, w_gate)


def _pick_tile(n, cap):
    best = None
    for t in range(128, min(n, cap) + 1, 128):
        if n % t == 0:
            best = t
    return n if best is None else best


def _grad_w(a, b, name):
    T, K = a.shape
    N = b.shape[1]
    tk, tn, tt = _pick_tile(K, 1408), _pick_tile(N, 2048), min(T, 512)

    def body(a_ref, b_ref, o_ref):
        _acc_out(o_ref, _mm_tn(a_ref[...], b_ref[...]), pl.program_id(2) == 0)

    return pl.pallas_call(
        body, name=name, grid=(K // tk, N // tn, T // tt),
        in_specs=[pl.BlockSpec((tt, tk), lambda i, j, t: (t, i)), pl.BlockSpec((tt, tn), lambda i, j, t: (t, j))],
        out_specs=pl.BlockSpec((tk, tn), lambda i, j, t: (i, j)),
        out_shape=jax.ShapeDtypeStruct((K, N), F32),
        compiler_params=_params(("parallel", "parallel", "arbitrary")),
    )(a, b)


def _tri(n, kind):
    r = lax.broadcasted_iota(jnp.int32, (n, n), 0)
    c = lax.broadcasted_iota(jnp.int32, (n, n), 1)
    return {"gt": r > c, "le": r <= c, "lt": r < c, "ge": r >= c}[kind]


def _split_mm(x, u):
    hi = x.astype(BF16)
    lo = (x - hi.astype(F32)).astype(BF16)
    return _mm(hi, u) + _mm(lo, u)


def _sb_scores(qs, k, row0, col0, qb, kb):
    z = _mm_nt(qs, k)
    sp = _softplus(z)
    row = lax.broadcasted_iota(jnp.int32, (qb, kb), 0) + row0
    col = lax.broadcasted_iota(jnp.int32, (qb, kb), 1) + col0
    strict = col < row
    return z, sp, strict


def _sb_fwd(q, k, v):
    G, S, N = q.shape
    qb, kb = min(ATT_Q, S), ATT_BLOCK
    nq, per = S // qb, qb // kb

    def body(q_ref, k_ref, v_ref, o_ref, l_ref, stop_ref):
        g = pl.program_id(0)
        u_after = _tri(kb, "gt").astype(BF16)

        def qblock(i, _):
            rows = pl.ds(pl.multiple_of(i * qb, qb), qb)
            qs = q_ref[rows, :] * (HEAD_DIM ** -0.5)

            def live(carry):
                return jnp.logical_and(carry[0] >= 0, carry[3] > 0)

            def kblock(carry):
                j, acc, c, _ = carry
                cols = pl.ds(pl.multiple_of(j * kb, kb), kb)
                z, sp, strict = _sb_scores(qs, k_ref[cols, :], i * qb, j * kb, qb, kb)
                lfm = jnp.where(strict, -sp, 0.0)
                after = _split_mm(lfm, u_after) + c
                w = jnp.where(strict, jnp.exp(z - sp + after), 0.0)
                acc = acc + _mm(w.astype(BF16), v_ref[cols, :])
                c = c + jnp.sum(lfm, axis=1, keepdims=True)
                return j - 1, acc, c, (jnp.max(c) > SB_DEAD).astype(jnp.int32)

            j, acc, c, _ = lax.while_loop(
                live, kblock, (per * i + per - 1, jnp.zeros((qb, N), F32), jnp.zeros((qb, 1), F32), jnp.int32(1)))
            o_ref[rows, :] = acc.astype(BF16)
            l_ref[rows, :] = c
            stop_ref[g, i] = j + 1
            return 0

        lax.fori_loop(0, nq, qblock, 0)

    spec = pl.BlockSpec((None, S, N), lambda g: (g, 0, 0))
    return pl.pallas_call(
        body, name="sb_fwd", grid=(G,), in_specs=[spec, spec, spec],
        out_specs=[spec, pl.BlockSpec((None, S, 1), lambda g: (g, 0, 0)), pl.BlockSpec(memory_space=pltpu.SMEM)],
        out_shape=[jax.ShapeDtypeStruct((G, S, N), BF16), jax.ShapeDtypeStruct((G, S, 1), F32),
                   jax.ShapeDtypeStruct((G, nq), jnp.int32)],
        compiler_params=_params(("arbitrary",)),
    )(q, k, v)


def _sb_bwd(q, k, v, do, lsum, stop):
    G, S, N = q.shape
    qb, kb = min(ATT_Q, S), ATT_BLOCK
    nq, per = S // qb, qb // kb

    def body(stop_ref, q_ref, k_ref, v_ref, do_ref, l_ref, dq_ref, dk_ref, dv_ref, dk_acc, dv_acc):
        g = pl.program_id(0)
        u_incl = _tri(kb, "le").astype(BF16)
        u_excl = _tri(kb, "lt").astype(BF16)
        dk_acc[...] = jnp.zeros_like(dk_acc)
        dv_acc[...] = jnp.zeros_like(dv_acc)

        def qblock(i, _):
            rows = pl.ds(pl.multiple_of(i * qb, qb), qb)
            qv = q_ref[rows, :]
            qs = qv * (HEAD_DIM ** -0.5)
            dob = do_ref[rows, :]
            ltot = l_ref[rows, :]

            def kblock(j, carry):
                dq, p, e_sum = carry
                cols = pl.ds(pl.multiple_of(j * kb, kb), kb)
                kv = k_ref[cols, :]
                vv = v_ref[cols, :]
                z, sp, strict = _sb_scores(qs, kv, i * qb, j * kb, qb, kb)
                lfm = jnp.where(strict, -sp, 0.0)
                after = ltot - p - _split_mm(lfm, u_incl)
                w = jnp.where(strict, jnp.exp(z - sp + after), 0.0)
                e = _mm_nt(dob, vv) * w
                dlf = e_sum + _split_mm(e, u_excl)
                sig = jnp.exp(z - sp)
                dz = jnp.where(strict, e * (1.0 - sig) - dlf * sig, 0.0) * (HEAD_DIM ** -0.5)
                dzb = dz.astype(BF16)
                dv_acc[cols, :] += _mm_tn(w.astype(BF16), dob)
                dk_acc[cols, :] += _mm_tn(dzb, qv)
                return (dq + _mm(dzb, kv), p + jnp.sum(lfm, axis=1, keepdims=True),
                        e_sum + jnp.sum(e, axis=1, keepdims=True))

            zero = jnp.zeros((qb, 1), F32)
            first = jnp.clip(stop_ref[g, i], 0, per * i + per - 1)
            dq, _, _ = lax.fori_loop(first, per * i + per, kblock, (jnp.zeros((qb, N), F32), zero, zero))
            dq_ref[rows, :] = dq.astype(BF16)
            return 0

        lax.fori_loop(0, nq, qblock, 0)
        dk_ref[...] = dk_acc[...].astype(BF16)
        dv_ref[...] = dv_acc[...].astype(BF16)

    spec = pl.BlockSpec((None, S, N), lambda g, stop_ref: (g, 0, 0))
    return pl.pallas_call(
        body, name="sb_bwd",
        grid_spec=pltpu.PrefetchScalarGridSpec(
            num_scalar_prefetch=1, grid=(G,),
            in_specs=[spec, spec, spec, spec, pl.BlockSpec((None, S, 1), lambda g, stop_ref: (g, 0, 0))],
            out_specs=[spec, spec, spec],
            scratch_shapes=[pltpu.VMEM((S, N), F32), pltpu.VMEM((S, N), F32)]),
        out_shape=[jax.ShapeDtypeStruct((G, S, N), BF16)] * 3,
        compiler_params=_params(("parallel",)),
    )(stop, q, k, v, do, lsum)


@jax.custom_vjp
def _lora_mm(x, w):
    return _mm(x.astype(BF16), w.astype(BF16))


_lora_mm.defvjp(
    lambda x, w: (_mm(x.astype(BF16), w.astype(BF16)), (x, w)),
    lambda res, ct: (_mm_nt(ct.astype(BF16), res[1].astype(BF16)), _mm_tn(res[0].astype(BF16), ct.astype(BF16))))


def _rw_prep_math(p, ps, mu, w0, w_up, a0, a_up, g_up, k_k, k_a):
    pm = p + (ps - p) * mu
    r, k, v = pm[:, :WIDTH], pm[:, WIDTH:2 * WIDTH], pm[:, 2 * WIDTH:3 * WIDTH]
    o = 3 * WIDTH
    xw, xa, xg = pm[:, o:o + W_LORA], pm[:, o + W_LORA:o + W_LORA + A_LORA], pm[:, o + W_LORA + A_LORA:]
    w_raw = w0 + _lora_mm(jnp.tanh(xw), w_up)
    lw = -jnp.exp(-_softplus(-w_raw) - 0.5)
    a = jax.nn.sigmoid(a0 + _lora_mm(xa, a_up))
    g = _lora_mm(jax.nn.sigmoid(xg), g_up)
    kk = k * k_k
    k2 = k * (1.0 + (a - 1.0) * k_a)
    return r, lw, k2, v, kk, a, g


def _shift_down(p, first_row):
    row = lax.broadcasted_iota(jnp.int32, p.shape, 0)
    return jnp.where(row == 0, first_row, pltpu.roll(p, 1, 0))


def _shift_up(p, last_row):
    row = lax.broadcasted_iota(jnp.int32, p.shape, 0)
    return jnp.where(row == p.shape[0] - 1, last_row, pltpu.roll(p, p.shape[0] - 1, 0))


_PREP_PARAM_SHAPES = [(1, RW_COLS), (1, WIDTH), (W_LORA, WIDTH), (1, WIDTH), (A_LORA, WIDTH), (G_LORA, WIDTH),
                      (1, WIDTH), (1, WIDTH)]


def _prev_rows_spec(tm):
    return pl.BlockSpec((8, RW_COLS), lambda i: (jnp.maximum(i * (tm // 8) - 1, 0), 0))


def _rw_prep_fwd(prw, params, seq, tm):
    T = prw.shape[0]

    def body(p_ref, prev_ref, *rest):
        prm = [r_[...] for r_ in rest[:8]]
        outs = rest[8:]
        i = pl.program_id(0)
        first = jnp.where((i * tm) % seq == 0, 0.0, prev_ref[7:8, :])
        p = p_ref[...]
        vals = _rw_prep_math(p, _shift_down(p, first), *prm)
        for o_ref, val in zip(outs, vals):
            o_ref[...] = val

    return pl.pallas_call(
        body, name="rw_prep_fwd", grid=(T // tm,),
        in_specs=[_row_spec(tm, RW_COLS), _prev_rows_spec(tm)] + [_const_spec(s) for s in _PREP_PARAM_SHAPES],
        out_specs=[_row_spec(tm, WIDTH)] * 7,
        out_shape=[jax.ShapeDtypeStruct((T, WIDTH), F32)] * 7,
        compiler_params=_params(("parallel",)),
    )(prw, prw, *params)


def _rw_prep_bwd(prw, params, cts, seq, tm):
    T = prw.shape[0]
    n = T // tm

    def body(p_ref, prev_ref, *rest):
        prm = [r_[...] for r_ in rest[:8]]
        ct = tuple(r_[...] for r_ in rest[8:15])
        dp_ref = rest[15]
        dprm_refs = rest[16:24]
        carry = rest[24]
        step = pl.program_id(0)
        i = n - 1 - step
        first = jnp.where((i * tm) % seq == 0, 0.0, prev_ref[7:8, :])
        p = p_ref[...]
        _, vjp = jax.vjp(_rw_prep_math, p, _shift_down(p, first), *prm)
        grads = vjp(ct)
        dp, dps = grads[0], grads[1]
        nxt = jnp.where(jnp.logical_or(step == 0, ((i + 1) * tm) % seq == 0), 0.0, carry[0:1, :])
        dp_ref[...] = (dp + _shift_up(dps, nxt)).astype(BF16)
        carry[...] = dps[0:8, :]
        for ref, gval in zip(dprm_refs, grads[2:]):
            _acc_out(ref, gval, step == 0)

    rev = lambda w: pl.BlockSpec((tm, w), lambda s: (n - 1 - s, 0))
    prev = pl.BlockSpec((8, RW_COLS), lambda s: (jnp.maximum((n - 1 - s) * (tm // 8) - 1, 0), 0))
    return pl.pallas_call(
        body, name="rw_prep_bwd", grid=(n,),
        in_specs=[rev(RW_COLS), prev] + [_const_spec(s) for s in _PREP_PARAM_SHAPES] + [rev(WIDTH)] * 7,
        out_specs=[rev(RW_COLS)] + [pl.BlockSpec(s, lambda s_: (0, 0)) for s in _PREP_PARAM_SHAPES],
        out_shape=[jax.ShapeDtypeStruct((T, RW_COLS), BF16)] + [jax.ShapeDtypeStruct(s, F32) for s in _PREP_PARAM_SHAPES],
        scratch_shapes=[pltpu.VMEM((8, RW_COLS), F32)],
        compiler_params=_params(("arbitrary",)),
    )(prw, prw, *params, *cts)


def _make_bmm(passes):
    def raw(dn, a, b):
        d = lambda x, y: lax.dot_general(x, y, dn, preferred_element_type=F32)
        ah = a.astype(BF16)
        bh = b.astype(BF16)
        if passes == 1:
            return d(ah, bh)
        al = (a - ah.astype(F32)).astype(BF16)
        bl = (b - bh.astype(F32)).astype(BF16)
        return d(ah, bh) + (d(ah, bl) + d(al, bh))

    dn_nn = (((2,), (1,)), ((0,), (0,)))
    dn_nt = (((2,), (2,)), ((0,), (0,)))
    dn_tn = (((1,), (1,)), ((0,), (0,)))

    @jax.custom_vjp
    def nn(a, b):
        return raw(dn_nn, a, b)

    @jax.custom_vjp
    def nt(a, b):
        return raw(dn_nt, a, b)

    @jax.custom_vjp
    def tn(a, b):
        return raw(dn_tn, a, b)

    nn.defvjp(lambda a, b: (raw(dn_nn, a, b), (a, b)), lambda res, ct: (nt(ct, res[1]), tn(res[0], ct)))
    nt.defvjp(lambda a, b: (raw(dn_nt, a, b), (a, b)), lambda res, ct: (nn(ct, res[1]), tn(ct, res[0])))
    tn.defvjp(lambda a, b: (raw(dn_tn, a, b), (a, b)), lambda res, ct: (nt(res[1], ct), nn(res[0], ct)))

    def unit_lower_inverse(m):
        n = m.shape[-1]
        row = lax.broadcasted_iota(jnp.int32, (n, n), 0)
        col = lax.broadcasted_iota(jnp.int32, (n, n), 1)
        m16 = ((row // 16) == (col // 16)).astype(F32)
        m32 = ((row // 32) == (col // 32)).astype(F32)
        a1 = m * m16
        a2 = nn(a1, a1)
        a4 = nn(a2, a2)
        a8 = nn(a4, a4)
        inv = (row == col).astype(F32) - a1
        inv = inv + nn(inv, a2)
        inv = inv + nn(inv, a4)
        inv = inv + nn(inv, a8)
        inv = inv - nn(nn(inv, m * (m32 - m16)), inv)
        return inv - nn(nn(inv, m * (1.0 - m32)), inv)

    @jax.custom_vjp
    def inverse(m):
        return unit_lower_inverse(m)

    def inverse_fwd(m):
        inv = unit_lower_inverse(m)
        return inv, inv

    inverse.defvjp(inverse_fwd, lambda inv, ct: (-nt(tn(inv, ct), inv),))
    return nn, nt, tn, inverse


def _wkv_chunk(s0, r, lw, k, v, kk, a, lnw, lnb, rk):
    nn, nt, tn, inverse = _make_bmm(SCAN_PASSES)
    G, L, N = r.shape
    rep = lambda t: jnp.broadcast_to(t[None], (G // HEADS, HEADS, 1, N)).reshape(G, 1, N)
    kap = kk * lax.rsqrt(jnp.maximum(jnp.sum(kk * kk, axis=-1, keepdims=True), 1e-24))
    b = a * kap
    row = lax.broadcasted_iota(jnp.int32, (L, L), 0)
    col = lax.broadcasted_iota(jnp.int32, (L, L), 1)
    low_incl = (col <= row).astype(F32)
    low_strict = (col < row).astype(F32)
    c = _make_bmm(3)[0](jnp.broadcast_to(low_incl[None], (G, L, L)), lw)
    c_all = jnp.sum(lw, axis=1, keepdims=True)
    g_inv = jnp.exp(-c)
    kap_t = kap * jnp.exp(c - lw)
    b_t = b * g_inv
    k_t = k * g_inv
    r_t = r * jnp.exp(c)
    g_all = jnp.exp(c_all)
    m_b = nt(kap_t, b_t) * low_strict
    m_k = nt(kap_t, k_t) * low_strict
    n_b = nt(r_t, b_t) * low_incl
    n_k = nt(r_t, k_t) * low_incl
    rhs = -(nt(kap_t, s0) + nn(m_k, v))
    sa = nn(inverse(m_b), rhs)
    y = nt(r_t, s0) + nn(n_b, sa) + nn(n_k, v)
    s1 = s0 * g_all + tn(sa, b_t * g_all) + tn(v, k_t * g_all)
    mean = jnp.mean(y, axis=-1, keepdims=True)
    yc = y - mean
    var = jnp.mean(yc * yc, axis=-1, keepdims=True)
    out = yc * lax.rsqrt(var + GN_EPS) * rep(lnw) + rep(lnb)
    out = out + jnp.sum(r * k * rep(rk), axis=-1, keepdims=True) * v
    return out, s1


def _wkv_fwd(seqs, lnw, lnb, rk):
    G, S, N = seqs[0].shape
    L = SCAN_CHUNK
    nc = S // L

    def body(*refs):
        ins = [r_[...] for r_ in refs[:6]]
        prm = [r_[...] for r_ in refs[6:9]]
        out_ref, st_ref, state = refs[9], refs[10], refs[11]

        @pl.when(pl.program_id(1) == 0)
        def _():
            state[...] = jnp.zeros_like(state)

        s0 = state[...]
        st_ref[...] = s0
        out, s1 = _wkv_chunk(s0, *ins, *prm)
        out_ref[...] = out
        state[...] = s1

    blk = pl.BlockSpec((HEADS, L, N), lambda b, i: (b, i, 0))
    pspec = _const_spec((HEADS, 1, N))
    return pl.pallas_call(
        body, name="wkv_fwd", grid=(G // HEADS, nc), in_specs=[blk] * 6 + [pspec] * 3,
        out_specs=[blk, pl.BlockSpec((None, HEADS, N, N), lambda b, i: (i, b, 0, 0))],
        out_shape=[jax.ShapeDtypeStruct((G, S, N), F32), jax.ShapeDtypeStruct((nc, G, N, N), F32)],
        scratch_shapes=[pltpu.VMEM((HEADS, N, N), F32)],
        compiler_params=_params(("parallel", "arbitrary")),
    )(*seqs, lnw, lnb, rk)


def _wkv_bwd(seqs, states, dout, lnw, lnb, rk):
    G, S, N = seqs[0].shape
    L = SCAN_CHUNK
    nc = S // L

    def body(*refs):
        ins = [r_[...] for r_ in refs[:6]]
        s0 = refs[6][...]
        ct_out = refs[7][...]
        prm = [r_[...] for r_ in refs[8:11]]
        d_refs = refs[11:17]
        dprm_refs = refs[17:20]
        dstate = refs[20]
        step = pl.program_id(1)

        @pl.when(step == 0)
        def _():
            dstate[...] = jnp.zeros_like(dstate)

        _, vjp = jax.vjp(_wkv_chunk, s0, *ins, *prm)
        grads = vjp((ct_out, dstate[...]))
        dstate[...] = grads[0]
        for ref, gval in zip(d_refs, grads[1:7]):
            ref[...] = gval
        for ref, gval in zip(dprm_refs, grads[7:]):
            _acc_out(ref, gval, jnp.logical_and(step == 0, pl.program_id(0) == 0))

    blk = pl.BlockSpec((HEADS, L, N), lambda b, s: (b, nc - 1 - s, 0))
    pspec = _const_spec((HEADS, 1, N))
    pout = pl.BlockSpec((HEADS, 1, N), lambda b, s: (0, 0, 0))
    return pl.pallas_call(
        body, name="wkv_bwd", grid=(G // HEADS, nc),
        in_specs=[blk] * 6 + [pl.BlockSpec((None, HEADS, N, N), lambda b, s: (nc - 1 - s, b, 0, 0)), blk] + [pspec] * 3,
        out_specs=[blk] * 6 + [pout] * 3,
        out_shape=[jax.ShapeDtypeStruct((G, S, N), F32)] * 6 + [jax.ShapeDtypeStruct((HEADS, 1, N), F32)] * 3,
        scratch_shapes=[pltpu.VMEM((HEADS, N, N), F32)],
        compiler_params=_params(("arbitrary", "arbitrary")),
    )(*seqs, states, dout, lnw, lnb, rk)


def _merge_math(o_sb, rw_out, g_rw, gates, w_sb, w_rw, w_o):
    o_rw = (rw_out * g_rw).astype(BF16)
    a = _mm(o_sb, w_sb)
    b = _mm(o_rw, w_rw)
    g1, g2 = gates[:, :D_MODEL], gates[:, D_MODEL:]
    merged = (g1 * a + g2 * b).astype(BF16)
    return o_rw, a, b, g1, g2, merged, _mm(merged, w_o)


def _merge_fwd(x2, o_sb, rw_out, g_rw, gates, w_sb, w_rw, w_o, g_post, tm):
    T = x2.shape[0]

    def body(x_ref, osb_ref, rw_ref, g_ref, gate_ref, wsb_ref, wrw_ref, wo_ref, gp_ref, x1_ref):
        z = _merge_math(osb_ref[...], rw_ref[...], g_ref[...], gate_ref[...], wsb_ref[...], wrw_ref[...], wo_ref[...])[-1]
        x1_ref[...] = x_ref[...] + _rms_fwd(z, gp_ref[...])[0]

    return pl.pallas_call(
        body, name="merge_fwd", grid=(T // tm,),
        in_specs=[_row_spec(tm, D_MODEL), _row_spec(tm, WIDTH), _row_spec(tm, WIDTH), _row_spec(tm, WIDTH),
                  _row_spec(tm, GATE_COLS), _const_spec((WIDTH, D_MODEL)), _const_spec((WIDTH, D_MODEL)),
                  _const_spec((D_MODEL, D_MODEL)), _const_spec((1, D_MODEL))],
        out_specs=_row_spec(tm, D_MODEL),
        out_shape=jax.ShapeDtypeStruct((T, D_MODEL), F32),
        compiler_params=_params(("parallel",)),
    )(x2, o_sb, rw_out, g_rw, gates, w_sb, w_rw, w_o, g_post)


def _merge_bwd(dx1, o_sb, rw_out, g_rw, gates, w_sb, w_rw, w_o, g_post, tm):
    T = dx1.shape[0]

    def body(dx1_ref, osb_ref, rw_ref, g_ref, gate_ref, wsb_ref, wrw_ref, wo_ref, gp_ref,
             orw_o, mrg_o, dz_o, da_o, db_o, dgate_o, dosb_o, drw_o, dg_o, dgp_o, dbg_o):
        rw_out_v, g_rw_v = rw_ref[...], g_ref[...]
        w_sb_v, w_rw_v, w_o_v = wsb_ref[...], wrw_ref[...], wo_ref[...]
        o_rw, a, b, g1, g2, merged, z = _merge_math(osb_ref[...], rw_out_v, g_rw_v, gate_ref[...], w_sb_v, w_rw_v, w_o_v)
        gain = gp_ref[...]
        _, zn, rstd = _rms_fwd(z, gain)
        dz, dgain = _rms_bwd(dx1_ref[...], zn, rstd, gain)
        dzb = dz.astype(BF16)
        dm = _mm_nt(dzb, w_o_v)
        dab = (dm * g1).astype(BF16)
        dbb = (dm * g2).astype(BF16)
        dgate = jnp.concatenate([dm * a * g1 * (1.0 - g1), dm * b * g2 * (1.0 - g2)], axis=1)
        do_rw = _mm_nt(dbb, w_rw_v)
        orw_o[...] = o_rw
        mrg_o[...] = merged
        dz_o[...] = dzb
        da_o[...] = dab
        db_o[...] = dbb
        dgate_o[...] = dgate.astype(BF16)
        dosb_o[...] = _mm_nt(dab, w_sb_v).astype(BF16)
        drw_o[...] = do_rw * g_rw_v
        dg_o[...] = do_rw * rw_out_v
        first = pl.program_id(0) == 0
        _acc_out(dgp_o, dgain, first)
        _acc_out(dbg_o, jnp.sum(dgate, axis=0, keepdims=True), first)

    acc = lambda n: pl.BlockSpec((1, n), lambda i: (0, 0))
    sd = jax.ShapeDtypeStruct
    return pl.pallas_call(
        body, name="merge_bwd", grid=(T // tm,),
        in_specs=[_row_spec(tm, D_MODEL), _row_spec(tm, WIDTH), _row_spec(tm, WIDTH), _row_spec(tm, WIDTH),
                  _row_spec(tm, GATE_COLS), _const_spec((WIDTH, D_MODEL)), _const_spec((WIDTH, D_MODEL)),
                  _const_spec((D_MODEL, D_MODEL)), _const_spec((1, D_MODEL))],
        out_specs=[_row_spec(tm, WIDTH), _row_spec(tm, D_MODEL), _row_spec(tm, D_MODEL), _row_spec(tm, D_MODEL),
                   _row_spec(tm, D_MODEL), _row_spec(tm, GATE_COLS), _row_spec(tm, WIDTH), _row_spec(tm, WIDTH),
                   _row_spec(tm, WIDTH), acc(D_MODEL), acc(GATE_COLS)],
        out_shape=[sd((T, WIDTH), BF16), sd((T, D_MODEL), BF16), sd((T, D_MODEL), BF16), sd((T, D_MODEL), BF16),
                   sd((T, D_MODEL), BF16), sd((T, GATE_COLS), BF16), sd((T, WIDTH), BF16), sd((T, WIDTH), F32),
                   sd((T, WIDTH), F32), sd((1, D_MODEL), F32), sd((1, GATE_COLS), F32)],
        compiler_params=_params(("arbitrary",)),
    )(dx1, o_sb, rw_out, g_rw, gates, w_sb, w_rw, w_o, g_post)


def _ffn(x1, target, g_pre, g_post, w_gate, w_up, w_down, tm):
    T = x1.shape[0]

    def body(x1_ref, tgt_ref, gpre_ref, gpost_ref, wg_ref, wu_ref, wd_ref,
             loss_o, dx1_o, h_o, dgate_o, dup_o, act_o, df_o, dgpre_o, dgpost_o):
        x1v = x1_ref[...]
        gpre, gpost = gpre_ref[...], gpost_ref[...]
        wg, wu, wd = wg_ref[...], wu_ref[...], wd_ref[...]
        hn, xn1, rstd1 = _rms_fwd(x1v, gpre)
        h = hn.astype(BF16)
        gate = _mm(h, wg)
        up = _mm(h, wu)
        sg = jax.nn.sigmoid(gate)
        act = (gate * sg * up).astype(BF16)
        f = _mm(act, wd)
        fo, fn, rstd2 = _rms_fwd(f, gpost)
        diff = x1v + fo - tgt_ref[...]
        dy = diff * (1.0 / D_MODEL)
        df, dgpost = _rms_bwd(dy, fn, rstd2, gpost)
        dfb = df.astype(BF16)
        dact = _mm_nt(dfb, wd)
        dup = (dact * gate * sg).astype(BF16)
        dgate = (dact * up * (sg * (1.0 + gate * (1.0 - sg)))).astype(BF16)
        dh = _mm_nt(dgate, wg) + _mm_nt(dup, wu)
        dxn, dgpre = _rms_bwd(dh, xn1, rstd1, gpre)
        dx1_o[...] = dy + dxn
        h_o[...] = h
        dgate_o[...] = dgate
        dup_o[...] = dup
        act_o[...] = act
        df_o[...] = dfb
        first = pl.program_id(0) == 0
        part = jnp.sum(jnp.sum(diff * diff, axis=1, keepdims=True), axis=0, keepdims=True) * (0.5 / D_MODEL)
        _acc_out(loss_o, jnp.broadcast_to(part, (8, 128)), first)
        _acc_out(dgpre_o, dgpre, first)
        _acc_out(dgpost_o, dgpost, first)

    acc = lambda r, n: pl.BlockSpec((r, n), lambda i: (0, 0))
    sd = jax.ShapeDtypeStruct
    return pl.pallas_call(
        body, name="ffn", grid=(T // tm,),
        in_specs=[_row_spec(tm, D_MODEL), _row_spec(tm, D_MODEL), _const_spec((1, D_MODEL)), _const_spec((1, D_MODEL)),
                  _const_spec((D_MODEL, D_FF)), _const_spec((D_MODEL, D_FF)), _const_spec((D_FF, D_MODEL))],
        out_specs=[acc(8, 128), _row_spec(tm, D_MODEL), _row_spec(tm, D_MODEL), _row_spec(tm, D_FF), _row_spec(tm, D_FF),
                   _row_spec(tm, D_FF), _row_spec(tm, D_MODEL), acc(1, D_MODEL), acc(1, D_MODEL)],
        out_shape=[sd((8, 128), F32), sd((T, D_MODEL), F32), sd((T, D_MODEL), BF16), sd((T, D_FF), BF16),
                   sd((T, D_FF), BF16), sd((T, D_FF), BF16), sd((T, D_MODEL), BF16), sd((1, D_MODEL), F32),
                   sd((1, D_MODEL), F32)],
        compiler_params=_params(("arbitrary",)),
    )(x1, target, g_pre, g_post, w_gate, w_up, w_down)


def _to_heads(t, bl, seq):
    return t.reshape(bl, seq, HEADS, HEAD_DIM).transpose(0, 2, 1, 3).reshape(bl * HEADS, seq, HEAD_DIM)


def _from_heads(t, bl, seq):
    return t.reshape(bl, HEADS, seq, HEAD_DIM).transpose(0, 2, 1, 3).reshape(bl * seq, WIDTH)


def _local_step(x, target, sm, wt):
    bl, seq, _ = x.shape
    T = bl * seq
    tm = min(ROW_TILE, T)
    x2 = x.reshape(T, D_MODEL)
    tgt2 = target.reshape(T, D_MODEL)
    w_qkv, w_prw, w_gate = wt["w_in"][:, :SB_COLS], wt["w_in"][:, SB_COLS:SB_COLS + RW_COLS], wt["w_in"][:, SB_COLS + RW_COLS:]
    h, qkv, prw, gates = _in_proj_fwd(x2, sm["norm_mix_pre"], w_qkv, w_prw, w_gate, sm["b_gate"], tm)
    q, k, v = (_to_heads(qkv[:, i * WIDTH:(i + 1) * WIDTH], bl, seq) for i in range(3))
    o_sb_h, lsum, sb_stop = _sb_fwd(q, k, v)
    o_sb = _from_heads(o_sb_h, bl, seq)
    prep_params = [sm["mu_rw"], sm["w0"], wt["w_up"].astype(F32), sm["a0"], wt["a_up"].astype(F32),
                   wt["g_up"].astype(F32), sm["k_k"], sm["k_a"]]
    prep = _rw_prep_fwd(prw, prep_params, seq, tm)
    seqs = [_to_heads(t, bl, seq) for t in prep[:6]]
    g_rw = prep[6]
    lnw, lnb, rk = (sm[n].reshape(HEADS, 1, HEAD_DIM) for n in ("lnx_w", "lnx_b", "r_k"))
    rw_out_h, states = _wkv_fwd(seqs, lnw, lnb, rk)
    rw_out = _from_heads(rw_out_h, bl, seq)
    x1 = _merge_fwd(x2, o_sb, rw_out, g_rw, gates, wt["w_sb_out"], wt["w_rw_out"], wt["w_o"], sm["norm_mix_post"], tm)
    (loss_part, dx1, h2, dffg, dffu, act, dff, d_nfpre, d_nfpost) = _ffn(
        x1, tgt2, sm["norm_ffn_pre"], sm["norm_ffn_post"], wt["w_ffn_gate"], wt["w_ffn_up"], wt["w_ffn_down"], tm)
    (o_rw, merged, dz, da, db, dgate, do_sb, d_rw_out, d_g_rw, d_npost, d_bgate) = _merge_bwd(
        dx1, o_sb, rw_out, g_rw, gates, wt["w_sb_out"], wt["w_rw_out"], wt["w_o"], sm["norm_mix_post"], tm)
    dq, dk, dv = _sb_bwd(q, k, v, _to_heads(do_sb, bl, seq), lsum, sb_stop)
    dqkv = jnp.concatenate([_from_heads(t, bl, seq) for t in (dq, dk, dv)], axis=1)
    wkv_g = _wkv_bwd(seqs, states, _to_heads(d_rw_out, bl, seq), lnw, lnb, rk)
    cts = [_from_heads(t, bl, seq) for t in wkv_g[:6]] + [d_g_rw]
    prep_g = _rw_prep_bwd(prw, prep_params, cts, seq, tm)
    dprw = prep_g[0]
    d_mu, d_w0, d_wup, d_a0, d_aup, d_gup, d_kk, d_ka = prep_g[1:]
    grad_x, d_npre = _in_proj_bwd(x2, sm["norm_mix_pre"], dx1, dqkv, dprw, dgate, w_qkv, w_prw, w_gate, tm)
    gw = {
        "w_in": jnp.concatenate([_grad_w(h, dqkv, "gw_in_qkv"), _grad_w(h, dprw, "gw_in_rw"), _grad_w(h, dgate, "gw_in_gate")], axis=1),
        "w_up": d_wup, "a_up": d_aup, "g_up": d_gup,
        "w_sb_out": _grad_w(o_sb, da, "gw_sb_out"), "w_rw_out": _grad_w(o_rw, db, "gw_rw_out"),
        "w_o": _grad_w(merged, dz, "gw_o"),
        "w_ffn_gate": _grad_w(h2, dffg, "gw_ffn_gate"), "w_ffn_up": _grad_w(h2, dffu, "gw_ffn_up"),
        "w_ffn_down": _grad_w(act, dff, "gw_ffn_down"),
    }
    gs = {
        "norm_mix_pre": d_npre, "b_gate": d_bgate, "mu_rw": d_mu, "w0": d_w0, "a0": d_a0, "k_k": d_kk, "k_a": d_ka,
        "r_k": wkv_g[8].reshape(1, WIDTH), "lnx_w": wkv_g[6].reshape(1, WIDTH), "lnx_b": wkv_g[7].reshape(1, WIDTH),
        "norm_mix_post": d_npost, "norm_ffn_pre": d_nfpre, "norm_ffn_post": d_nfpost,
    }
    return loss_part, grad_x.reshape(x.shape), gw, gs


_SHARDED = [("w_in", 1, (D_MODEL, (SB_COLS + RW_COLS + GATE_COLS) // N_DEV)), ("w_up", 1, (W_LORA, WIDTH // N_DEV)),
            ("a_up", 1, (A_LORA, WIDTH // N_DEV)), ("g_up", 1, (G_LORA, WIDTH // N_DEV)),
            ("w_sb_out", 1, (WIDTH, D_MODEL // N_DEV)), ("w_rw_out", 1, (WIDTH, D_MODEL // N_DEV)),
            ("w_o", 0, (D_MODEL // N_DEV, D_MODEL)), ("w_ffn_gate", 1, (D_MODEL, D_FF // N_DEV)),
            ("w_ffn_up", 1, (D_MODEL, D_FF // N_DEV)), ("w_ffn_down", 0, (D_FF // N_DEV, D_MODEL))]
_LANES = 128
_PACK_ROWS = [s[0] * s[1] // _LANES for _, _, s in _SHARDED]
_PACK_TOTAL = sum(_PACK_ROWS)
_SMALL = [("norm_mix_pre", D_MODEL), ("b_gate", GATE_COLS), ("mu_rw", RW_COLS), ("w0", WIDTH), ("a0", WIDTH),
          ("k_k", WIDTH), ("k_a", WIDTH), ("r_k", WIDTH), ("lnx_w", WIDTH), ("lnx_b", WIDTH),
          ("norm_mix_post", D_MODEL), ("norm_ffn_pre", D_MODEL), ("norm_ffn_post", D_MODEL)]
_SMALL_ROWS = 96


def _pack_shards(shards, dtype):
    return jnp.concatenate([shards[n].astype(dtype).reshape(-1, _LANES) for n, _, _ in _SHARDED], axis=0)


def _unpack_shards(packed):
    out, r0 = {}, 0
    for (n, _, shp), rows in zip(_SHARDED, _PACK_ROWS):
        out[n] = packed[r0:r0 + rows].reshape(shp)
        r0 += rows
    return out


def _unpack_gathered(g):
    out, r0 = {}, 0
    for (n, axis, shp), rows in zip(_SHARDED, _PACK_ROWS):
        blk = g[:, r0:r0 + rows].reshape((N_DEV,) + shp)
        out[n] = blk.reshape(N_DEV * shp[0], shp[1]) if axis == 0 else blk.transpose(1, 0, 2).reshape(shp[0], N_DEV * shp[1])
        r0 += rows
    return out


def _pack_full_grads(gw):
    parts = []
    for n, axis, shp in _SHARDED:
        g = gw[n]
        blk = g.reshape((N_DEV,) + shp) if axis == 0 else g.reshape(shp[0], N_DEV, shp[1]).transpose(1, 0, 2)
        parts.append(blk.reshape(N_DEV, -1, _LANES))
    return jnp.concatenate(parts, axis=1)


def _pack_small(vals, extra=None):
    flat = [vals[n].reshape(-1) for n, _ in _SMALL]
    used = sum(sz for _, sz in _SMALL)
    tail = jnp.zeros((_SMALL_ROWS * _LANES - used,), F32)
    if extra is not None:
        tail = tail.at[0].set(extra)
    return jnp.concatenate(flat + [tail]).reshape(_SMALL_ROWS, _LANES)


def _unpack_small(packed):
    flat, out, o = packed.reshape(-1), {}, 0
    for n, sz in _SMALL:
        out[n] = flat[o:o + sz]
        o += sz
    return out, flat[o]


_ANY = pl.BlockSpec(memory_space=pl.ANY)


def _all_gather(block):
    rows, lanes = block.shape

    def body(x_ref, out_ref, send_sems, recv_sems, local_sem):
        x, y, c = lax.axis_index("x"), lax.axis_index("y"), lax.axis_index("c")
        me, sibling = (x, y, c), (x, y, 1 - c)
        chips = [(1 - x, y), (x, 1 - y), (1 - x, 1 - y)]

        def slot(px, py, pc):
            return out_ref.at[4 * px + 2 * py + pc]

        def copy(k, blk, to, src=None):
            return pltpu.make_async_remote_copy(
                src_ref=slot(*blk) if src is None else src, dst_ref=slot(*blk),
                send_sem=send_sems.at[k], recv_sem=recv_sems.at[k], device_id=to, device_id_type=MESH)

        mine = pltpu.make_async_copy(x_ref, slot(*me), local_sem)
        mine.start()
        first = [copy(0, me, sibling, src=x_ref)]
        first += [copy(1 + j, me, (*chip, c), src=x_ref) for j, chip in enumerate(chips)]
        for cp in first:
            cp.start()
        passed = [copy(4 + j, (*chip, c), sibling) for j, chip in enumerate(chips)]
        for j, chip in enumerate(chips):
            copy(1 + j, (*chip, c), me).wait_recv()
            passed[j].start()
        copy(0, sibling, me).wait_recv()
        for j, chip in enumerate(chips):
            copy(4 + j, (*chip, 1 - c), me).wait_recv()
        for cp in first + passed:
            cp.wait_send()
        mine.wait()

    return pl.pallas_call(
        body, name="all_gather_weights", in_specs=[_ANY], out_specs=_ANY,
        out_shape=jax.ShapeDtypeStruct((N_DEV, rows, lanes), block.dtype),
        scratch_shapes=[pltpu.SemaphoreType.DMA((7,)), pltpu.SemaphoreType.DMA((7,)), pltpu.SemaphoreType.DMA],
    )(block)


def _exchange_core(pack, small):
    _, _, rows, lanes = pack.shape

    def body(pack_ref, small_ref, got_ref, parts_ref, send_sems, recv_sems, s_send, s_recv, local_sem):
        x, y, c = lax.axis_index("x"), lax.axis_index("y"), lax.axis_index("c")
        sibling = (x, y, 1 - c)
        me = 4 * x + 2 * y + c
        mine = pltpu.make_async_copy(small_ref, parts_ref.at[me], local_sem)
        mine.start()
        big = [pltpu.make_async_remote_copy(
            src_ref=pack_ref.at[1 - c, j], dst_ref=got_ref.at[j], send_sem=send_sems.at[j], recv_sem=recv_sems.at[j],
            device_id=sibling, device_id_type=MESH) for j in range(4)]
        for cp in big:
            cp.start()
        others = [(k, (x ^ (k >> 2), y ^ ((k >> 1) & 1), c ^ (k & 1))) for k in range(1, N_DEV)]
        tiny = [pltpu.make_async_remote_copy(
            src_ref=small_ref, dst_ref=parts_ref.at[me], send_sem=s_send.at[k], recv_sem=s_recv.at[k],
            device_id=to, device_id_type=MESH) for k, to in others]
        for cp in tiny:
            cp.start()
        for cp in big:
            cp.wait_recv()
        for (k, (px, py, pc)), cp in zip(others, tiny):
            pltpu.make_async_remote_copy(
                src_ref=small_ref, dst_ref=parts_ref.at[4 * px + 2 * py + pc], send_sem=s_send.at[k],
                recv_sem=s_recv.at[k], device_id=(px, py, pc), device_id_type=MESH).wait_recv()
        for cp in big + tiny:
            cp.wait_send()
        mine.wait()

    return pl.pallas_call(
        body, name="exchange_core", in_specs=[_ANY, _ANY], out_specs=[_ANY, _ANY],
        out_shape=[jax.ShapeDtypeStruct((4, rows, lanes), F32), jax.ShapeDtypeStruct((N_DEV,) + small.shape, F32)],
        scratch_shapes=[pltpu.SemaphoreType.DMA((4,)), pltpu.SemaphoreType.DMA((4,)), pltpu.SemaphoreType.DMA((N_DEV,)),
                        pltpu.SemaphoreType.DMA((N_DEV,)), pltpu.SemaphoreType.DMA],
    )(pack, small)


def _add_core_parts(pack, got, core):
    _, _, rows, lanes = pack.shape
    tr = 2000

    def body(core_ref, a_ref, b_ref, o_ref):
        o_ref[...] = a_ref[...] + b_ref[...]

    return pl.pallas_call(
        body, name="add_core_parts",
        grid_spec=pltpu.PrefetchScalarGridSpec(
            num_scalar_prefetch=1, grid=(4, rows // tr),
            in_specs=[pl.BlockSpec((None, None, tr, lanes), lambda j, i, core_ref: (core_ref[0], j, i, 0)),
                      pl.BlockSpec((None, tr, lanes), lambda j, i, core_ref: (j, i, 0))],
            out_specs=pl.BlockSpec((None, tr, lanes), lambda j, i, core_ref: (j, i, 0))),
        out_shape=jax.ShapeDtypeStruct((4, rows, lanes), F32),
        compiler_params=_params(("parallel", "parallel")),
    )(core, pack, got)


def _exchange_chips(chip_sums):
    _, rows, lanes = chip_sums.shape

    def body(src_ref, got_ref, send_sems, recv_sems):
        x, y, c = lax.axis_index("x"), lax.axis_index("y"), lax.axis_index("c")
        flips = [(1, 0), (0, 1), (1, 1)]
        copies = []
        for k, (fx, fy) in enumerate(flips):
            px, py = x ^ fx, y ^ fy
            copies.append(pltpu.make_async_remote_copy(
                src_ref=src_ref.at[2 * px + py], dst_ref=got_ref.at[k], send_sem=send_sems.at[k],
                recv_sem=recv_sems.at[k], device_id=(px, py, c), device_id_type=MESH))
        for cp in copies:
            cp.start()
        for cp in copies:
            cp.wait_recv()
        for cp in copies:
            cp.wait_send()

    return pl.pallas_call(
        body, name="exchange_chips", in_specs=[_ANY], out_specs=_ANY,
        out_shape=jax.ShapeDtypeStruct((3, rows, lanes), F32),
        scratch_shapes=[pltpu.SemaphoreType.DMA((3,)), pltpu.SemaphoreType.DMA((3,))],
    )(chip_sums)


def _sum_chip_parts(chip_sums, got, chip):
    _, rows, lanes = chip_sums.shape
    tr = 2000

    def body(chip_ref, own_ref, got_ref, o_ref):
        o_ref[...] = ((own_ref[...] + got_ref[0]) + got_ref[1]) + got_ref[2]

    return pl.pallas_call(
        body, name="sum_chip_parts",
        grid_spec=pltpu.PrefetchScalarGridSpec(
            num_scalar_prefetch=1, grid=(rows // tr,),
            in_specs=[pl.BlockSpec((None, tr, lanes), lambda i, chip_ref: (chip_ref[0], i, 0)),
                      pl.BlockSpec((3, tr, lanes), lambda i, chip_ref: (0, i, 0))],
            out_specs=pl.BlockSpec((tr, lanes), lambda i, chip_ref: (i, 0))),
        out_shape=jax.ShapeDtypeStruct((rows, lanes), F32),
        compiler_params=_params(("parallel",)),
    )(chip, chip_sums, got)


def _adamw_math(w, g, m, v):
    m = ADAM_B1 * m + (1.0 - ADAM_B1) * g
    v = ADAM_B2 * v + (1.0 - ADAM_B2) * (g * g)
    m_hat = m / (1.0 - ADAM_B1 ** ADAM_STEP)
    v_hat = v / (1.0 - ADAM_B2 ** ADAM_STEP)
    return -ADAM_LR * (m_hat / (jnp.sqrt(v_hat) + ADAM_EPS) + ADAM_WD * w), m, v


def _adamw(w, g, m, v, name):
    rows, cols = w.shape
    tr = 256 if rows % 256 == 0 and rows * cols > 2 ** 19 else rows

    def body(w_ref, g_ref, m_ref, v_ref, d_o, m_o, v_o):
        d_o[...], m_o[...], v_o[...] = _adamw_math(w_ref[...], g_ref[...], m_ref[...], v_ref[...])

    spec = pl.BlockSpec((tr, cols), lambda i: (i, 0))
    return pl.pallas_call(
        body, name=name, grid=(rows // tr,), in_specs=[spec] * 4, out_specs=[spec] * 3,
        out_shape=[jax.ShapeDtypeStruct((rows, cols), F32)] * 3, compiler_params=_params(("parallel",)),
    )(w, g, m, v)


def _adamw_small(parts, w, m, v):
    def body(p_ref, w_ref, m_ref, v_ref, g_o, d_o, m_o, v_o):
        g = p_ref[0]
        for d in range(1, N_DEV):
            g = g + p_ref[d]
        g_o[...] = g
        d_o[...], m_o[...], v_o[...] = _adamw_math(w_ref[...], g, m_ref[...], v_ref[...])

    return pl.pallas_call(
        body, name="adamw_small", out_shape=[jax.ShapeDtypeStruct(w.shape, F32)] * 4, compiler_params=_params(),
    )(parts, w, m, v)


_WEIGHT_NAMES = ['norm_mix_pre', 'w_in', 'b_gate', 'mu_rw', 'w0', 'w_up', 'a0', 'a_up', 'g_up', 'k_k', 'k_a', 'r_k',
                 'lnx_w', 'lnx_b', 'w_sb_out', 'w_rw_out', 'w_o', 'norm_mix_post', 'norm_ffn_pre', 'w_ffn_gate',
                 'w_ffn_up', 'w_ffn_down', 'norm_ffn_post']


def _step(x, target, w, m, v):
    sharded = [n for n, _, _ in _SHARDED]
    sm = {n: w[n].reshape(1, -1) for n, _ in _SMALL}
    own = {n: w[n][0] for n in sharded}
    wt = _unpack_gathered(_all_gather(_pack_shards(own, BF16)))
    loss_part, grad_x, gw, gs = _local_step(x, target, sm, wt)

    cx, cy, cc = lax.axis_index("x"), lax.axis_index("y"), lax.axis_index("c")
    core = jnp.reshape(cc, (1,)).astype(jnp.int32)
    chip = jnp.reshape(2 * cx + cy, (1,)).astype(jnp.int32)
    pack = _pack_full_grads(gw).reshape(4, 2, _PACK_TOTAL, _LANES).transpose(1, 0, 2, 3)
    got_core, small_parts = _exchange_core(pack, _pack_small(gs, loss_part[0, 0]))
    chip_sums = _add_core_parts(pack, got_core, core)
    grads_packed = _sum_chip_parts(chip_sums, _exchange_chips(chip_sums), chip)
    g_sh = _unpack_shards(grads_packed)

    g_small, d_small, m_small, v_small = _adamw_small(
        small_parts, _pack_small({n: w[n] for n, _ in _SMALL}), _pack_small({n: m[n] for n, _ in _SMALL}),
        _pack_small({n: v[n] for n, _ in _SMALL}))
    (g_s, loss), (d_s, _), (m_s, _), (v_s, _) = (_unpack_small(t) for t in (g_small, d_small, m_small, v_small))

    grads, deltas, new_m, new_v = {}, {}, {}, {}
    for n in _WEIGHT_NAMES:
        if n in g_sh:
            d_, m_, v_ = _adamw(own[n], g_sh[n], m[n][0], v[n][0], "adamw_" + n)
            grads[n], deltas[n], new_m[n], new_v[n] = (t.reshape(w[n].shape) for t in (g_sh[n], d_, m_, v_))
        else:
            grads[n], deltas[n], new_m[n], new_v[n] = (t[n].reshape(w[n].shape) for t in (g_s, d_s, m_s, v_s))
    return (loss, grad_x, *[grads[n] for n in _WEIGHT_NAMES], *[deltas[n] for n in _WEIGHT_NAMES],
            *[new_m[n] for n in _WEIGHT_NAMES], *[new_v[n] for n in _WEIGHT_NAMES])


def kernel(x, norm_mix_pre, w_in, b_gate, mu_rw, w0, w_up, a0, a_up, g_up, k_k, k_a, r_k, lnx_w, lnx_b, w_sb_out, w_rw_out, w_o, norm_mix_post, norm_ffn_pre, w_ffn_gate, w_ffn_up, w_ffn_down, norm_ffn_post, loss_target, m_norm_mix_pre, m_w_in, m_b_gate, m_mu_rw, m_w0, m_w_up, m_a0, m_a_up, m_g_up, m_k_k, m_k_a, m_r_k, m_lnx_w, m_lnx_b, m_w_sb_out, m_w_rw_out, m_w_o, m_norm_mix_post, m_norm_ffn_pre, m_w_ffn_gate, m_w_ffn_up, m_w_ffn_down, m_norm_ffn_post, v_norm_mix_pre, v_w_in, v_b_gate, v_mu_rw, v_w0, v_w_up, v_a0, v_a_up, v_g_up, v_k_k, v_k_a, v_r_k, v_lnx_w, v_lnx_b, v_w_sb_out, v_w_rw_out, v_w_o, v_norm_mix_post, v_norm_ffn_pre, v_w_ffn_gate, v_w_ffn_up, v_w_ffn_down, v_norm_ffn_post):
    args = locals()
    w = {n: args[n] for n in _WEIGHT_NAMES}
    m = {n: args["m_" + n] for n in _WEIGHT_NAMES}
    v = {n: args["v_" + n] for n in _WEIGHT_NAMES}
    return _step(x, loss_target, w, m, v)
```

```python
import functools

import jax
import jax.numpy as jnp
from jax import lax
from jax.experimental import pallas as pl
from jax.experimental.pallas import tpu as pltpu

F32 = jnp.float32
BF16 = jnp.bfloat16

D_MODEL = 1024
HEADS = 8
HEAD_DIM = 64
WIDTH = HEADS * HEAD_DIM
W_LORA, A_LORA, G_LORA = 64, 64, 128
SB_COLS = 3 * WIDTH
RW_COLS = 3 * WIDTH + W_LORA + A_LORA + G_LORA
GATE_COLS = 2 * D_MODEL
D_FF = 2816
RMS_EPS = 1e-6
GN_EPS = HEAD_DIM * 1e-5
N_DEV = 8

ADAM_LR, ADAM_B1, ADAM_B2, ADAM_EPS, ADAM_WD, ADAM_STEP = 0.001, 0.9, 0.999, 1e-08, 0.01, 10

ROW_TILE = 256
SCAN_CHUNK = 64
ATT_BLOCK = 128
ATT_Q = 128
SB_DEAD = -104.0
SCAN_PASSES = 1
VMEM_LIMIT = 56 * 2 ** 20

MESH = pl.DeviceIdType.MESH


def _params(sem=None, vmem=VMEM_LIMIT):
    kw = dict(vmem_limit_bytes=vmem)
    if sem is not None:
        kw["dimension_semantics"] = sem
    return pltpu.CompilerParams(**kw)


def _const_spec(shape):
    nd = len(shape)
    return pl.BlockSpec(shape, lambda *_: (0,) * nd, pipeline_mode=pl.Buffered(1))


def _row_spec(tm, n):
    return pl.BlockSpec((tm, n), lambda i: (i, 0))


def _mm(a, b):
    return lax.dot_general(a, b, (((1,), (0,)), ((), ())), preferred_element_type=F32)


def _mm_nt(a, b):
    return lax.dot_general(a, b, (((1,), (1,)), ((), ())), preferred_element_type=F32)


def _mm_tn(a, b):
    return lax.dot_general(a, b, (((0,), (0,)), ((), ())), preferred_element_type=F32)


def _softplus(z):
    return jnp.maximum(z, 0.0) + jnp.log1p(jnp.exp(-jnp.abs(z)))


def _rms_fwd(x, gain):
    rstd = lax.rsqrt(jnp.mean(x * x, axis=-1, keepdims=True) + RMS_EPS)
    xn = x * rstd
    return xn * gain, xn, rstd


def _rms_bwd(dy, xn, rstd, gain):
    u = dy * gain
    dx = rstd * (u - xn * jnp.mean(u * xn, axis=-1, keepdims=True))
    return dx, jnp.sum(dy * xn, axis=0, keepdims=True)


def _acc_out(ref, val, first):
    @pl.when(first)
    def _():
        ref[...] = val

    @pl.when(jnp.logical_not(first))
    def _():
        ref[...] += val


def _in_proj_fwd(x2, g_pre, w_qkv, w_rw, w_gate, b_gate, tm):
    T = x2.shape[0]

    def body(x_ref, g_ref, wq_ref, wr_ref, wg_ref, b_ref, h_ref, qkv_ref, prw_ref, gate_ref):
        h = _rms_fwd(x_ref[...], g_ref[...])[0].astype(BF16)
        h_ref[...] = h
        qkv_ref[...] = _mm(h, wq_ref[...]).astype(BF16)
        prw_ref[...] = _mm(h, wr_ref[...])
        gate_ref[...] = jax.nn.sigmoid(_mm(h, wg_ref[...]) + b_ref[...])

    return pl.pallas_call(
        body, name="in_proj_fwd", grid=(T // tm,),
        in_specs=[_row_spec(tm, D_MODEL), _const_spec((1, D_MODEL)), _const_spec((D_MODEL, SB_COLS)),
                  _const_spec((D_MODEL, RW_COLS)), _const_spec((D_MODEL, GATE_COLS)), _const_spec((1, GATE_COLS))],
        out_specs=[_row_spec(tm, D_MODEL), _row_spec(tm, SB_COLS), _row_spec(tm, RW_COLS), _row_spec(tm, GATE_COLS)],
        out_shape=[jax.ShapeDtypeStruct((T, D_MODEL), BF16), jax.ShapeDtypeStruct((T, SB_COLS), BF16),
                   jax.ShapeDtypeStruct((T, RW_COLS), F32), jax.ShapeDtypeStruct((T, GATE_COLS), F32)],
        compiler_params=_params(("parallel",)),
    )(x2, g_pre, w_qkv, w_rw, w_gate, b_gate)


def _in_proj_bwd(x2, g_pre, dx1, dqkv, dprw, dgate, w_qkv, w_rw, w_gate, tm):
    T = x2.shape[0]

    def body(x_ref, g_ref, dx1_ref, dq_ref, dr_ref, dg_ref, wq_ref, wr_ref, wg_ref, gx_ref, dgain_ref):
        dh = _mm_nt(dq_ref[...], wq_ref[...]) + _mm_nt(dr_ref[...], wr_ref[...]) + _mm_nt(dg_ref[...], wg_ref[...])
        gain = g_ref[...]
        _, xn, rstd = _rms_fwd(x_ref[...], gain)
        dx, dgain = _rms_bwd(dh, xn, rstd, gain)
        gx_ref[...] = dx1_ref[...] + dx
        _acc_out(dgain_ref, dgain, pl.program_id(0) == 0)

    return pl.pallas_call(
        body, name="in_proj_bwd", grid=(T // tm,),
        in_specs=[_row_spec(tm, D_MODEL), _const_spec((1, D_MODEL)), _row_spec(tm, D_MODEL), _row_spec(tm, SB_COLS),
                  _row_spec(tm, RW_COLS), _row_spec(tm, GATE_COLS), _const_spec((D_MODEL, SB_COLS)),
                  _const_spec((D_MODEL, RW_COLS)), _const_spec((D_MODEL, GATE_COLS))],
        out_specs=[_row_spec(tm, D_MODEL), pl.BlockSpec((1, D_MODEL), lambda i: (0, 0))],
        out_shape=[jax.ShapeDtypeStruct((T, D_MODEL), F32), jax.ShapeDtypeStruct((1, D_MODEL), F32)],
        compiler_params=_params(("arbitrary",)),
    )(x2, g_pre, dx1, dqkv, dprw, dgate, w_qkv, w_rw, w_gate)


def _pick_tile(n, cap):
    best = None
    for t in range(128, min(n, cap) + 1, 128):
        if n % t == 0:
            best = t
    return n if best is None else best


def _grad_w(a, b, name):
    T, K = a.shape
    N = b.shape[1]
    tk, tn, tt = _pick_tile(K, 1408), _pick_tile(N, 2048), min(T, 512)

    def body(a_ref, b_ref, o_ref):
        _acc_out(o_ref, _mm_tn(a_ref[...], b_ref[...]), pl.program_id(2) == 0)

    return pl.pallas_call(
        body, name=name, grid=(K // tk, N // tn, T // tt),
        in_specs=[pl.BlockSpec((tt, tk), lambda i, j, t: (t, i)), pl.BlockSpec((tt, tn), lambda i, j, t: (t, j))],
        out_specs=pl.BlockSpec((tk, tn), lambda i, j, t: (i, j)),
        out_shape=jax.ShapeDtypeStruct((K, N), F32),
        compiler_params=_params(("parallel", "parallel", "arbitrary")),
    )(a, b)


def _tri(n, kind):
    r = lax.broadcasted_iota(jnp.int32, (n, n), 0)
    c = lax.broadcasted_iota(jnp.int32, (n, n), 1)
    return {"gt": r > c, "le": r <= c, "lt": r < c, "ge": r >= c}[kind]


def _split_mm(x, u):
    hi = x.astype(BF16)
    lo = (x - hi.astype(F32)).astype(BF16)
    return _mm(hi, u) + _mm(lo, u)


def _sb_scores(qs, k, row0, col0, qb, kb):
    z = _mm_nt(qs, k)
    sp = _softplus(z)
    row = lax.broadcasted_iota(jnp.int32, (qb, kb), 0) + row0
    col = lax.broadcasted_iota(jnp.int32, (qb, kb), 1) + col0
    strict = col < row
    return z, sp, strict


_PAIR = 2 * HEAD_DIM
_PAIRS = WIDTH // _PAIR


def _first_head_lanes():
    return lax.broadcasted_iota(jnp.int32, (1, _PAIR), 1) < HEAD_DIM


def _per_head(t, first_head):
    zero = jnp.zeros_like(t)
    return jnp.where(first_head, t, zero), jnp.where(first_head, zero, t)


def _sb_fwd(qkv, bl, seq):
    qb, kb = min(ATT_Q, seq), ATT_BLOCK
    nq, per = seq // qb, qb // kb

    def body(q_ref, k_ref, v_ref, o_ref, l_ref, stop_ref):
        g = pl.program_id(0) * _PAIRS + pl.program_id(1)
        first_head = _first_head_lanes()
        u_after = _tri(kb, "gt").astype(BF16)

        def qblock(i, _):
            rows = pl.ds(pl.multiple_of(i * qb, qb), qb)
            qh = _per_head(q_ref[rows, :] * (HEAD_DIM ** -0.5), first_head)

            def live(carry):
                return jnp.logical_and(carry[0] >= 0, carry[3] > 0)

            def kblock(carry):
                j, accs, cs, _ = carry
                cols = pl.ds(pl.multiple_of(j * kb, kb), kb)
                kv, vv = k_ref[cols, :], v_ref[cols, :]
                new_accs, new_cs = [], []
                for h in range(2):
                    z, sp, strict = _sb_scores(qh[h], kv, i * qb, j * kb, qb, kb)
                    lfm = jnp.where(strict, -sp, 0.0)
                    after = _split_mm(lfm, u_after) + cs[h]
                    w = jnp.where(strict, jnp.exp(z - sp + after), 0.0)
                    new_accs.append(accs[h] + _mm(w.astype(BF16), vv))
                    new_cs.append(cs[h] + jnp.sum(lfm, axis=1, keepdims=True))
                alive = jnp.maximum(jnp.max(new_cs[0]), jnp.max(new_cs[1])) > SB_DEAD
                return j - 1, tuple(new_accs), tuple(new_cs), alive.astype(jnp.int32)

            zero_acc, zero_c = jnp.zeros((qb, _PAIR), F32), jnp.zeros((qb, 1), F32)
            j, accs, cs, _ = lax.while_loop(
                live, kblock, (per * i + per - 1, (zero_acc, zero_acc), (zero_c, zero_c), jnp.int32(1)))
            o_ref[rows, :] = jnp.where(first_head, accs[0], accs[1]).astype(BF16)
            l_ref[0, rows, :] = cs[0]
            l_ref[1, rows, :] = cs[1]
            stop_ref[g, i] = j + 1
            return 0

        lax.fori_loop(0, nq, qblock, 0)

    col = lambda off: pl.BlockSpec((seq, _PAIR), lambda b, p: (b, off + p))
    return pl.pallas_call(
        body, name="sb_fwd", grid=(bl, _PAIRS), in_specs=[col(0), col(_PAIRS), col(2 * _PAIRS)],
        out_specs=[col(0), pl.BlockSpec((None, 2, seq, 1), lambda b, p: (b, p, 0, 0)),
                   pl.BlockSpec(memory_space=pltpu.SMEM)],
        out_shape=[jax.ShapeDtypeStruct((bl * seq, WIDTH), BF16), jax.ShapeDtypeStruct((bl, HEADS, seq, 1), F32),
                   jax.ShapeDtypeStruct((bl * _PAIRS, nq), jnp.int32)],
        compiler_params=_params(("arbitrary", "arbitrary")),
    )(qkv, qkv, qkv)


def _sb_bwd(qkv, do, lsum, stop, bl, seq):
    qb, kb = min(ATT_Q, seq), ATT_BLOCK
    nq, per = seq // qb, qb // kb

    def body(stop_ref, q_ref, k_ref, v_ref, do_ref, l_ref, dq_ref, dk_ref, dv_ref, dk_acc, dv_acc):
        g = pl.program_id(0) * _PAIRS + pl.program_id(1)
        first_head = _first_head_lanes()
        u_incl = _tri(kb, "le").astype(BF16)
        u_excl = _tri(kb, "lt").astype(BF16)
        dk_acc[...] = jnp.zeros_like(dk_acc)
        dv_acc[...] = jnp.zeros_like(dv_acc)

        def qblock(i, _):
            rows = pl.ds(pl.multiple_of(i * qb, qb), qb)
            qv = q_ref[rows, :]
            qh = _per_head(qv * (HEAD_DIM ** -0.5), first_head)
            dob = do_ref[rows, :]
            doh = _per_head(dob, first_head)
            ltot = (l_ref[0, rows, :], l_ref[1, rows, :])

            def kblock(j, carry):
                dqs, ps, es = carry
                cols = pl.ds(pl.multiple_of(j * kb, kb), kb)
                kv, vv = k_ref[cols, :], v_ref[cols, :]
                new_dqs, new_ps, new_es, dks, dvs = [], [], [], [], []
                for h in range(2):
                    z, sp, strict = _sb_scores(qh[h], kv, i * qb, j * kb, qb, kb)
                    lfm = jnp.where(strict, -sp, 0.0)
                    after = ltot[h] - ps[h] - _split_mm(lfm, u_incl)
                    w = jnp.where(strict, jnp.exp(z - sp + after), 0.0)
                    e = _mm_nt(doh[h], vv) * w
                    dlf = es[h] + _split_mm(e, u_excl)
                    sig = jnp.exp(z - sp)
                    dz = jnp.where(strict, e * (1.0 - sig) - dlf * sig, 0.0) * (HEAD_DIM ** -0.5)
                    dzb = dz.astype(BF16)
                    dvs.append(_mm_tn(w.astype(BF16), dob))
                    dks.append(_mm_tn(dzb, qv))
                    new_dqs.append(dqs[h] + _mm(dzb, kv))
                    new_ps.append(ps[h] + jnp.sum(lfm, axis=1, keepdims=True))
                    new_es.append(es[h] + jnp.sum(e, axis=1, keepdims=True))
                dv_acc[cols, :] += jnp.where(first_head, dvs[0], dvs[1])
                dk_acc[cols, :] += jnp.where(first_head, dks[0], dks[1])
                return tuple(new_dqs), tuple(new_ps), tuple(new_es)

            zero_q, zero_c = jnp.zeros((qb, _PAIR), F32), jnp.zeros((qb, 1), F32)
            first = jnp.clip(stop_ref[g, i], 0, per * i + per - 1)
            dqs, _, _ = lax.fori_loop(first, per * i + per, kblock,
                                      ((zero_q, zero_q), (zero_c, zero_c), (zero_c, zero_c)))
            dq_ref[rows, :] = jnp.where(first_head, dqs[0], dqs[1]).astype(BF16)
            return 0

        lax.fori_loop(0, nq, qblock, 0)
        dk_ref[...] = dk_acc[...].astype(BF16)
        dv_ref[...] = dv_acc[...].astype(BF16)

    col = lambda off: pl.BlockSpec((seq, _PAIR), lambda b, p, stop_ref: (b, off + p))
    return pl.pallas_call(
        body, name="sb_bwd",
        grid_spec=pltpu.PrefetchScalarGridSpec(
            num_scalar_prefetch=1, grid=(bl, _PAIRS),
            in_specs=[col(0), col(_PAIRS), col(2 * _PAIRS), col(0),
                      pl.BlockSpec((None, 2, seq, 1), lambda b, p, stop_ref: (b, p, 0, 0))],
            out_specs=[col(0), col(0), col(0)],
            scratch_shapes=[pltpu.VMEM((seq, _PAIR), F32), pltpu.VMEM((seq, _PAIR), F32)]),
        out_shape=[jax.ShapeDtypeStruct((bl * seq, WIDTH), BF16)] * 3,
        compiler_params=_params(("parallel", "parallel")),
    )(stop, qkv, qkv, qkv, do, lsum)


@jax.custom_vjp
def _lora_mm(x, w):
    return _mm(x.astype(BF16), w.astype(BF16))


_lora_mm.defvjp(
    lambda x, w: (_mm(x.astype(BF16), w.astype(BF16)), (x, w)),
    lambda res, ct: (_mm_nt(ct.astype(BF16), res[1].astype(BF16)), _mm_tn(res[0].astype(BF16), ct.astype(BF16))))


def _rw_prep_math(p, ps, mu, w0, w_up, a0, a_up, g_up, k_k, k_a):
    pm = p + (ps - p) * mu
    r, k, v = pm[:, :WIDTH], pm[:, WIDTH:2 * WIDTH], pm[:, 2 * WIDTH:3 * WIDTH]
    o = 3 * WIDTH
    xw, xa, xg = pm[:, o:o + W_LORA], pm[:, o + W_LORA:o + W_LORA + A_LORA], pm[:, o + W_LORA + A_LORA:]
    w_raw = w0 + _lora_mm(jnp.tanh(xw), w_up)
    lw = -jnp.exp(-_softplus(-w_raw) - 0.5)
    a = jax.nn.sigmoid(a0 + _lora_mm(xa, a_up))
    g = _lora_mm(jax.nn.sigmoid(xg), g_up)
    kk = k * k_k
    k2 = k * (1.0 + (a - 1.0) * k_a)
    return r, lw, k2, v, kk, a, g


def _shift_down(p, first_row):
    row = lax.broadcasted_iota(jnp.int32, p.shape, 0)
    return jnp.where(row == 0, first_row, pltpu.roll(p, 1, 0))


def _shift_up(p, last_row):
    row = lax.broadcasted_iota(jnp.int32, p.shape, 0)
    return jnp.where(row == p.shape[0] - 1, last_row, pltpu.roll(p, p.shape[0] - 1, 0))


_PREP_PARAM_SHAPES = [(1, RW_COLS), (1, WIDTH), (W_LORA, WIDTH), (1, WIDTH), (A_LORA, WIDTH), (G_LORA, WIDTH),
                      (1, WIDTH), (1, WIDTH)]


def _prev_rows_spec(tm):
    return pl.BlockSpec((8, RW_COLS), lambda i: (jnp.maximum(i * (tm // 8) - 1, 0), 0))


def _head_spec(tm, seq, tile_of=lambda i: i):
    per_seq = seq // tm
    return pl.BlockSpec((None, HEADS, tm, HEAD_DIM),
                        lambda i: (tile_of(i) // per_seq, 0, tile_of(i) % per_seq, 0))


def _split_heads(val, ref):
    for h in range(HEADS):
        ref[h] = val[:, h * HEAD_DIM:(h + 1) * HEAD_DIM]


def _join_heads(ref):
    return jnp.concatenate([ref[h] for h in range(HEADS)], axis=1)


def _rw_prep_fwd(prw, params, seq, tm):
    T = prw.shape[0]

    def body(p_ref, prev_ref, *rest):
        prm = [r_[...] for r_ in rest[:8]]
        outs = rest[8:]
        i = pl.program_id(0)
        first = jnp.where((i * tm) % seq == 0, 0.0, prev_ref[7:8, :])
        p = p_ref[...]
        vals = _rw_prep_math(p, _shift_down(p, first), *prm)
        for o_ref, val in zip(outs[:6], vals[:6]):
            _split_heads(val, o_ref)
        outs[6][...] = vals[6]

    by_head = jax.ShapeDtypeStruct((T // seq, HEADS, seq, HEAD_DIM), F32)
    return pl.pallas_call(
        body, name="rw_prep_fwd", grid=(T // tm,),
        in_specs=[_row_spec(tm, RW_COLS), _prev_rows_spec(tm)] + [_const_spec(s) for s in _PREP_PARAM_SHAPES],
        out_specs=[_head_spec(tm, seq)] * 6 + [_row_spec(tm, WIDTH)],
        out_shape=[by_head] * 6 + [jax.ShapeDtypeStruct((T, WIDTH), F32)],
        compiler_params=_params(("parallel",)),
    )(prw, prw, *params)


def _rw_prep_bwd(prw, params, cts, seq, tm):
    T = prw.shape[0]
    n = T // tm

    def body(p_ref, prev_ref, *rest):
        prm = [r_[...] for r_ in rest[:8]]
        ct = tuple(_join_heads(r_) for r_ in rest[8:14]) + (rest[14][...],)
        dp_ref = rest[15]
        dprm_refs = rest[16:24]
        carry = rest[24]
        step = pl.program_id(0)
        i = n - 1 - step
        first = jnp.where((i * tm) % seq == 0, 0.0, prev_ref[7:8, :])
        p = p_ref[...]
        _, vjp = jax.vjp(_rw_prep_math, p, _shift_down(p, first), *prm)
        grads = vjp(ct)
        dp, dps = grads[0], grads[1]
        nxt = jnp.where(jnp.logical_or(step == 0, ((i + 1) * tm) % seq == 0), 0.0, carry[0:1, :])
        dp_ref[...] = (dp + _shift_up(dps, nxt)).astype(BF16)
        carry[...] = dps[0:8, :]
        for ref, gval in zip(dprm_refs, grads[2:]):
            _acc_out(ref, gval, step == 0)

    rev = lambda w: pl.BlockSpec((tm, w), lambda s: (n - 1 - s, 0))
    prev = pl.BlockSpec((8, RW_COLS), lambda s: (jnp.maximum((n - 1 - s) * (tm // 8) - 1, 0), 0))
    return pl.pallas_call(
        body, name="rw_prep_bwd", grid=(n,),
        in_specs=([rev(RW_COLS), prev] + [_const_spec(s) for s in _PREP_PARAM_SHAPES]
                  + [_head_spec(tm, seq, lambda s: n - 1 - s)] * 6 + [rev(WIDTH)]),
        out_specs=[rev(RW_COLS)] + [pl.BlockSpec(s, lambda s_: (0, 0)) for s in _PREP_PARAM_SHAPES],
        out_shape=[jax.ShapeDtypeStruct((T, RW_COLS), BF16)] + [jax.ShapeDtypeStruct(s, F32) for s in _PREP_PARAM_SHAPES],
        scratch_shapes=[pltpu.VMEM((8, RW_COLS), F32)],
        compiler_params=_params(("arbitrary",)),
    )(prw, prw, *params, *cts)


def _make_bmm(passes):
    def raw(dn, a, b):
        d = lambda x, y: lax.dot_general(x, y, dn, preferred_element_type=F32)
        ah = a.astype(BF16)
        bh = b.astype(BF16)
        if passes == 1:
            return d(ah, bh)
        al = (a - ah.astype(F32)).astype(BF16)
        bl = (b - bh.astype(F32)).astype(BF16)
        return d(ah, bh) + (d(ah, bl) + d(al, bh))

    dn_nn = (((2,), (1,)), ((0,), (0,)))
    dn_nt = (((2,), (2,)), ((0,), (0,)))
    dn_tn = (((1,), (1,)), ((0,), (0,)))

    @jax.custom_vjp
    def nn(a, b):
        return raw(dn_nn, a, b)

    @jax.custom_vjp
    def nt(a, b):
        return raw(dn_nt, a, b)

    @jax.custom_vjp
    def tn(a, b):
        return raw(dn_tn, a, b)

    nn.defvjp(lambda a, b: (raw(dn_nn, a, b), (a, b)), lambda res, ct: (nt(ct, res[1]), tn(res[0], ct)))
    nt.defvjp(lambda a, b: (raw(dn_nt, a, b), (a, b)), lambda res, ct: (nn(ct, res[1]), tn(ct, res[0])))
    tn.defvjp(lambda a, b: (raw(dn_tn, a, b), (a, b)), lambda res, ct: (nt(res[1], ct), nn(res[0], ct)))

    def unit_lower_inverse(m):
        n = m.shape[-1]
        row = lax.broadcasted_iota(jnp.int32, (n, n), 0)
        col = lax.broadcasted_iota(jnp.int32, (n, n), 1)
        m16 = ((row // 16) == (col // 16)).astype(F32)
        m32 = ((row // 32) == (col // 32)).astype(F32)
        a1 = m * m16
        a2 = nn(a1, a1)
        a4 = nn(a2, a2)
        a8 = nn(a4, a4)
        inv = (row == col).astype(F32) - a1
        inv = inv + nn(inv, a2)
        inv = inv + nn(inv, a4)
        inv = inv + nn(inv, a8)
        inv = inv - nn(nn(inv, m * (m32 - m16)), inv)
        return inv - nn(nn(inv, m * (1.0 - m32)), inv)

    @jax.custom_vjp
    def inverse(m):
        return unit_lower_inverse(m)

    def inverse_fwd(m):
        inv = unit_lower_inverse(m)
        return inv, inv

    inverse.defvjp(inverse_fwd, lambda inv, ct: (-nt(tn(inv, ct), inv),))
    return nn, nt, tn, inverse


def _wkv_chunk(s0, r, lw, k, v, kk, a, lnw, lnb, rk):
    nn, nt, tn, inverse = _make_bmm(SCAN_PASSES)
    G, L, N = r.shape
    rep = lambda t: jnp.broadcast_to(t[None], (G // HEADS, HEADS, 1, N)).reshape(G, 1, N)
    kap = kk * lax.rsqrt(jnp.maximum(jnp.sum(kk * kk, axis=-1, keepdims=True), 1e-24))
    b = a * kap
    row = lax.broadcasted_iota(jnp.int32, (L, L), 0)
    col = lax.broadcasted_iota(jnp.int32, (L, L), 1)
    low_incl = (col <= row).astype(F32)
    low_strict = (col < row).astype(F32)
    c = _make_bmm(3)[0](jnp.broadcast_to(low_incl[None], (G, L, L)), lw)
    c_all = jnp.sum(lw, axis=1, keepdims=True)
    g_inv = jnp.exp(-c)
    kap_t = kap * jnp.exp(c - lw)
    b_t = b * g_inv
    k_t = k * g_inv
    r_t = r * jnp.exp(c)
    g_all = jnp.exp(c_all)
    m_b = nt(kap_t, b_t) * low_strict
    m_k = nt(kap_t, k_t) * low_strict
    n_b = nt(r_t, b_t) * low_incl
    n_k = nt(r_t, k_t) * low_incl
    rhs = -(nt(kap_t, s0) + nn(m_k, v))
    sa = nn(inverse(m_b), rhs)
    y = nt(r_t, s0) + nn(n_b, sa) + nn(n_k, v)
    s1 = s0 * g_all + tn(sa, b_t * g_all) + tn(v, k_t * g_all)
    mean = jnp.mean(y, axis=-1, keepdims=True)
    yc = y - mean
    var = jnp.mean(yc * yc, axis=-1, keepdims=True)
    out = yc * lax.rsqrt(var + GN_EPS) * rep(lnw) + rep(lnb)
    out = out + jnp.sum(r * k * rep(rk), axis=-1, keepdims=True) * v
    return out, s1


def _wkv_fwd(seqs, lnw, lnb, rk):
    G, S, N = seqs[0].shape
    L = SCAN_CHUNK
    nc = S // L

    def body(*refs):
        ins = [r_[...] for r_ in refs[:6]]
        prm = [r_[...] for r_ in refs[6:9]]
        out_ref, st_ref, state = refs[9], refs[10], refs[11]

        @pl.when(pl.program_id(1) == 0)
        def _():
            state[...] = jnp.zeros_like(state)

        s0 = state[...]
        st_ref[...] = s0
        out, s1 = _wkv_chunk(s0, *ins, *prm)
        out_ref[...] = out
        state[...] = s1

    blk = pl.BlockSpec((HEADS, L, N), lambda b, i: (b, i, 0))
    pspec = _const_spec((HEADS, 1, N))
    return pl.pallas_call(
        body, name="wkv_fwd", grid=(G // HEADS, nc), in_specs=[blk] * 6 + [pspec] * 3,
        out_specs=[blk, pl.BlockSpec((None, HEADS, N, N), lambda b, i: (i, b, 0, 0))],
        out_shape=[jax.ShapeDtypeStruct((G, S, N), F32), jax.ShapeDtypeStruct((nc, G, N, N), F32)],
        scratch_shapes=[pltpu.VMEM((HEADS, N, N), F32)],
        compiler_params=_params(("parallel", "arbitrary")),
    )(*seqs, lnw, lnb, rk)


def _wkv_bwd(seqs, states, dout, lnw, lnb, rk):
    G, S, N = seqs[0].shape
    L = SCAN_CHUNK
    nc = S // L

    def body(*refs):
        ins = [r_[...] for r_ in refs[:6]]
        s0 = refs[6][...]
        ct_out = refs[7][...]
        prm = [r_[...] for r_ in refs[8:11]]
        d_refs = refs[11:17]
        dprm_refs = refs[17:20]
        dstate = refs[20]
        step = pl.program_id(1)

        @pl.when(step == 0)
        def _():
            dstate[...] = jnp.zeros_like(dstate)

        _, vjp = jax.vjp(_wkv_chunk, s0, *ins, *prm)
        grads = vjp((ct_out, dstate[...]))
        dstate[...] = grads[0]
        for ref, gval in zip(d_refs, grads[1:7]):
            ref[...] = gval
        for ref, gval in zip(dprm_refs, grads[7:]):
            _acc_out(ref, gval, jnp.logical_and(step == 0, pl.program_id(0) == 0))

    blk = pl.BlockSpec((HEADS, L, N), lambda b, s: (b, nc - 1 - s, 0))
    pspec = _const_spec((HEADS, 1, N))
    pout = pl.BlockSpec((HEADS, 1, N), lambda b, s: (0, 0, 0))
    return pl.pallas_call(
        body, name="wkv_bwd", grid=(G // HEADS, nc),
        in_specs=[blk] * 6 + [pl.BlockSpec((None, HEADS, N, N), lambda b, s: (nc - 1 - s, b, 0, 0)), blk] + [pspec] * 3,
        out_specs=[blk] * 6 + [pout] * 3,
        out_shape=[jax.ShapeDtypeStruct((G, S, N), F32)] * 6 + [jax.ShapeDtypeStruct((HEADS, 1, N), F32)] * 3,
        scratch_shapes=[pltpu.VMEM((HEADS, N, N), F32)],
        compiler_params=_params(("arbitrary", "arbitrary")),
    )(*seqs, states, dout, lnw, lnb, rk)


def _merge_math(o_sb, rw_out, g_rw, gates, w_sb, w_rw, w_o):
    o_rw = (rw_out * g_rw).astype(BF16)
    a = _mm(o_sb, w_sb)
    b = _mm(o_rw, w_rw)
    g1, g2 = gates[:, :D_MODEL], gates[:, D_MODEL:]
    merged = (g1 * a + g2 * b).astype(BF16)
    return o_rw, a, b, g1, g2, merged, _mm(merged, w_o)


def _merge_fwd(x2, o_sb, rw_out, g_rw, gates, w_sb, w_rw, w_o, g_post, seq, tm):
    T = x2.shape[0]

    def body(x_ref, osb_ref, rw_ref, g_ref, gate_ref, wsb_ref, wrw_ref, wo_ref, gp_ref, x1_ref):
        z = _merge_math(osb_ref[...], _join_heads(rw_ref), g_ref[...], gate_ref[...], wsb_ref[...], wrw_ref[...], wo_ref[...])[-1]
        x1_ref[...] = x_ref[...] + _rms_fwd(z, gp_ref[...])[0]

    return pl.pallas_call(
        body, name="merge_fwd", grid=(T // tm,),
        in_specs=[_row_spec(tm, D_MODEL), _row_spec(tm, WIDTH), _head_spec(tm, seq), _row_spec(tm, WIDTH),
                  _row_spec(tm, GATE_COLS), _const_spec((WIDTH, D_MODEL)), _const_spec((WIDTH, D_MODEL)),
                  _const_spec((D_MODEL, D_MODEL)), _const_spec((1, D_MODEL))],
        out_specs=_row_spec(tm, D_MODEL),
        out_shape=jax.ShapeDtypeStruct((T, D_MODEL), F32),
        compiler_params=_params(("parallel",)),
    )(x2, o_sb, rw_out, g_rw, gates, w_sb, w_rw, w_o, g_post)


def _merge_bwd(dx1, o_sb, rw_out, g_rw, gates, w_sb, w_rw, w_o, g_post, seq, tm):
    T = dx1.shape[0]

    def body(dx1_ref, osb_ref, rw_ref, g_ref, gate_ref, wsb_ref, wrw_ref, wo_ref, gp_ref,
             orw_o, mrg_o, dz_o, da_o, db_o, dgate_o, dosb_o, drw_o, dg_o, dgp_o, dbg_o):
        rw_out_v, g_rw_v = _join_heads(rw_ref), g_ref[...]
        w_sb_v, w_rw_v, w_o_v = wsb_ref[...], wrw_ref[...], wo_ref[...]
        o_rw, a, b, g1, g2, merged, z = _merge_math(osb_ref[...], rw_out_v, g_rw_v, gate_ref[...], w_sb_v, w_rw_v, w_o_v)
        gain = gp_ref[...]
        _, zn, rstd = _rms_fwd(z, gain)
        dz, dgain = _rms_bwd(dx1_ref[...], zn, rstd, gain)
        dzb = dz.astype(BF16)
        dm = _mm_nt(dzb, w_o_v)
        dab = (dm * g1).astype(BF16)
        dbb = (dm * g2).astype(BF16)
        dgate = jnp.concatenate([dm * a * g1 * (1.0 - g1), dm * b * g2 * (1.0 - g2)], axis=1)
        do_rw = _mm_nt(dbb, w_rw_v)
        orw_o[...] = o_rw
        mrg_o[...] = merged
        dz_o[...] = dzb
        da_o[...] = dab
        db_o[...] = dbb
        dgate_o[...] = dgate.astype(BF16)
        dosb_o[...] = _mm_nt(dab, w_sb_v).astype(BF16)
        _split_heads(do_rw * g_rw_v, drw_o)
        dg_o[...] = do_rw * rw_out_v
        first = pl.program_id(0) == 0
        _acc_out(dgp_o, dgain, first)
        _acc_out(dbg_o, jnp.sum(dgate, axis=0, keepdims=True), first)

    acc = lambda n: pl.BlockSpec((1, n), lambda i: (0, 0))
    sd = jax.ShapeDtypeStruct
    return pl.pallas_call(
        body, name="merge_bwd", grid=(T // tm,),
        in_specs=[_row_spec(tm, D_MODEL), _row_spec(tm, WIDTH), _head_spec(tm, seq), _row_spec(tm, WIDTH),
                  _row_spec(tm, GATE_COLS), _const_spec((WIDTH, D_MODEL)), _const_spec((WIDTH, D_MODEL)),
                  _const_spec((D_MODEL, D_MODEL)), _const_spec((1, D_MODEL))],
        out_specs=[_row_spec(tm, WIDTH), _row_spec(tm, D_MODEL), _row_spec(tm, D_MODEL), _row_spec(tm, D_MODEL),
                   _row_spec(tm, D_MODEL), _row_spec(tm, GATE_COLS), _row_spec(tm, WIDTH), _head_spec(tm, seq),
                   _row_spec(tm, WIDTH), acc(D_MODEL), acc(GATE_COLS)],
        out_shape=[sd((T, WIDTH), BF16), sd((T, D_MODEL), BF16), sd((T, D_MODEL), BF16), sd((T, D_MODEL), BF16),
                   sd((T, D_MODEL), BF16), sd((T, GATE_COLS), BF16), sd((T, WIDTH), BF16),
                   sd((T // seq, HEADS, seq, HEAD_DIM), F32), sd((T, WIDTH), F32), sd((1, D_MODEL), F32),
                   sd((1, GATE_COLS), F32)],
        compiler_params=_params(("arbitrary",)),
    )(dx1, o_sb, rw_out, g_rw, gates, w_sb, w_rw, w_o, g_post)


def _ffn(x1, target, g_pre, g_post, w_gate, w_up, w_down, tm):
    T = x1.shape[0]

    def body(x1_ref, tgt_ref, gpre_ref, gpost_ref, wg_ref, wu_ref, wd_ref,
             loss_o, dx1_o, h_o, dgate_o, dup_o, act_o, df_o, dgpre_o, dgpost_o):
        x1v = x1_ref[...]
        gpre, gpost = gpre_ref[...], gpost_ref[...]
        wg, wu, wd = wg_ref[...], wu_ref[...], wd_ref[...]
        hn, xn1, rstd1 = _rms_fwd(x1v, gpre)
        h = hn.astype(BF16)
        gate = _mm(h, wg)
        up = _mm(h, wu)
        sg = jax.nn.sigmoid(gate)
        act = (gate * sg * up).astype(BF16)
        f = _mm(act, wd)
        fo, fn, rstd2 = _rms_fwd(f, gpost)
        diff = x1v + fo - tgt_ref[...]
        dy = diff * (1.0 / D_MODEL)
        df, dgpost = _rms_bwd(dy, fn, rstd2, gpost)
        dfb = df.astype(BF16)
        dact = _mm_nt(dfb, wd)
        dup = (dact * gate * sg).astype(BF16)
        dgate = (dact * up * (sg * (1.0 + gate * (1.0 - sg)))).astype(BF16)
        dh = _mm_nt(dgate, wg) + _mm_nt(dup, wu)
        dxn, dgpre = _rms_bwd(dh, xn1, rstd1, gpre)
        dx1_o[...] = dy + dxn
        h_o[...] = h
        dgate_o[...] = dgate
        dup_o[...] = dup
        act_o[...] = act
        df_o[...] = dfb
        first = pl.program_id(0) == 0
        part = jnp.sum(jnp.sum(diff * diff, axis=1, keepdims=True), axis=0, keepdims=True) * (0.5 / D_MODEL)
        _acc_out(loss_o, jnp.broadcast_to(part, (8, 128)), first)
        _acc_out(dgpre_o, dgpre, first)
        _acc_out(dgpost_o, dgpost, first)

    acc = lambda r, n: pl.BlockSpec((r, n), lambda i: (0, 0))
    sd = jax.ShapeDtypeStruct
    return pl.pallas_call(
        body, name="ffn", grid=(T // tm,),
        in_specs=[_row_spec(tm, D_MODEL), _row_spec(tm, D_MODEL), _const_spec((1, D_MODEL)), _const_spec((1, D_MODEL)),
                  _const_spec((D_MODEL, D_FF)), _const_spec((D_MODEL, D_FF)), _const_spec((D_FF, D_MODEL))],
        out_specs=[acc(8, 128), _row_spec(tm, D_MODEL), _row_spec(tm, D_MODEL), _row_spec(tm, D_FF), _row_spec(tm, D_FF),
                   _row_spec(tm, D_FF), _row_spec(tm, D_MODEL), acc(1, D_MODEL), acc(1, D_MODEL)],
        out_shape=[sd((8, 128), F32), sd((T, D_MODEL), F32), sd((T, D_MODEL), BF16), sd((T, D_FF), BF16),
                   sd((T, D_FF), BF16), sd((T, D_FF), BF16), sd((T, D_MODEL), BF16), sd((1, D_MODEL), F32),
                   sd((1, D_MODEL), F32)],
        compiler_params=_params(("arbitrary",)),
    )(x1, target, g_pre, g_post, w_gate, w_up, w_down)


def _local_step(x, target, sm, wt):
    bl, seq, _ = x.shape
    T = bl * seq
    tm = min(ROW_TILE, T)
    x2 = x.reshape(T, D_MODEL)
    tgt2 = target.reshape(T, D_MODEL)
    w_qkv, w_prw, w_gate = wt["w_in"][:, :SB_COLS], wt["w_in"][:, SB_COLS:SB_COLS + RW_COLS], wt["w_in"][:, SB_COLS + RW_COLS:]
    h, qkv, prw, gates = _in_proj_fwd(x2, sm["norm_mix_pre"], w_qkv, w_prw, w_gate, sm["b_gate"], tm)
    o_sb, lsum, sb_stop = _sb_fwd(qkv, bl, seq)
    prep_params = [sm["mu_rw"], sm["w0"], wt["w_up"].astype(F32), sm["a0"], wt["a_up"].astype(F32),
                   wt["g_up"].astype(F32), sm["k_k"], sm["k_a"]]
    prep = _rw_prep_fwd(prw, prep_params, seq, tm)
    by_head = lambda t: t.reshape(bl, HEADS, seq, HEAD_DIM)
    seqs = [t.reshape(bl * HEADS, seq, HEAD_DIM) for t in prep[:6]]
    g_rw = prep[6]
    lnw, lnb, rk = (sm[n].reshape(HEADS, 1, HEAD_DIM) for n in ("lnx_w", "lnx_b", "r_k"))
    rw_out_h, states = _wkv_fwd(seqs, lnw, lnb, rk)
    rw_out = by_head(rw_out_h)
    x1 = _merge_fwd(x2, o_sb, rw_out, g_rw, gates, wt["w_sb_out"], wt["w_rw_out"], wt["w_o"], sm["norm_mix_post"],
                    seq, tm)
    (loss_part, dx1, h2, dffg, dffu, act, dff, d_nfpre, d_nfpost) = _ffn(
        x1, tgt2, sm["norm_ffn_pre"], sm["norm_ffn_post"], wt["w_ffn_gate"], wt["w_ffn_up"], wt["w_ffn_down"], tm)
    (o_rw, merged, dz, da, db, dgate, do_sb, d_rw_out, d_g_rw, d_npost, d_bgate) = _merge_bwd(
        dx1, o_sb, rw_out, g_rw, gates, wt["w_sb_out"], wt["w_rw_out"], wt["w_o"], sm["norm_mix_post"], seq, tm)
    dqkv = jnp.concatenate(_sb_bwd(qkv, do_sb, lsum, sb_stop, bl, seq), axis=1)
    wkv_g = _wkv_bwd(seqs, states, d_rw_out.reshape(bl * HEADS, seq, HEAD_DIM), lnw, lnb, rk)
    cts = [by_head(t) for t in wkv_g[:6]] + [d_g_rw]
    prep_g = _rw_prep_bwd(prw, prep_params, cts, seq, tm)
    dprw = prep_g[0]
    d_mu, d_w0, d_wup, d_a0, d_aup, d_gup, d_kk, d_ka = prep_g[1:]
    grad_x, d_npre = _in_proj_bwd(x2, sm["norm_mix_pre"], dx1, dqkv, dprw, dgate, w_qkv, w_prw, w_gate, tm)
    gw = {
        "w_in": jnp.concatenate([_grad_w(h, dqkv, "gw_in_qkv"), _grad_w(h, dprw, "gw_in_rw"), _grad_w(h, dgate, "gw_in_gate")], axis=1),
        "w_up": d_wup, "a_up": d_aup, "g_up": d_gup,
        "w_sb_out": _grad_w(o_sb, da, "gw_sb_out"), "w_rw_out": _grad_w(o_rw, db, "gw_rw_out"),
        "w_o": _grad_w(merged, dz, "gw_o"),
        "w_ffn_gate": _grad_w(h2, dffg, "gw_ffn_gate"), "w_ffn_up": _grad_w(h2, dffu, "gw_ffn_up"),
        "w_ffn_down": _grad_w(act, dff, "gw_ffn_down"),
    }
    gs = {
        "norm_mix_pre": d_npre, "b_gate": d_bgate, "mu_rw": d_mu, "w0": d_w0, "a0": d_a0, "k_k": d_kk, "k_a": d_ka,
        "r_k": wkv_g[8].reshape(1, WIDTH), "lnx_w": wkv_g[6].reshape(1, WIDTH), "lnx_b": wkv_g[7].reshape(1, WIDTH),
        "norm_mix_post": d_npost, "norm_ffn_pre": d_nfpre, "norm_ffn_post": d_nfpost,
    }
    return loss_part, grad_x.reshape(x.shape), gw, gs


_SHARDED = [("w_in", 1, (D_MODEL, (SB_COLS + RW_COLS + GATE_COLS) // N_DEV)), ("w_up", 1, (W_LORA, WIDTH // N_DEV)),
            ("a_up", 1, (A_LORA, WIDTH // N_DEV)), ("g_up", 1, (G_LORA, WIDTH // N_DEV)),
            ("w_sb_out", 1, (WIDTH, D_MODEL // N_DEV)), ("w_rw_out", 1, (WIDTH, D_MODEL // N_DEV)),
            ("w_o", 0, (D_MODEL // N_DEV, D_MODEL)), ("w_ffn_gate", 1, (D_MODEL, D_FF // N_DEV)),
            ("w_ffn_up", 1, (D_MODEL, D_FF // N_DEV)), ("w_ffn_down", 0, (D_FF // N_DEV, D_MODEL))]
_LANES = 128
_PACK_ROWS = [s[0] * s[1] // _LANES for _, _, s in _SHARDED]
_PACK_TOTAL = sum(_PACK_ROWS)
_SMALL = [("norm_mix_pre", D_MODEL), ("b_gate", GATE_COLS), ("mu_rw", RW_COLS), ("w0", WIDTH), ("a0", WIDTH),
          ("k_k", WIDTH), ("k_a", WIDTH), ("r_k", WIDTH), ("lnx_w", WIDTH), ("lnx_b", WIDTH),
          ("norm_mix_post", D_MODEL), ("norm_ffn_pre", D_MODEL), ("norm_ffn_post", D_MODEL)]
_SMALL_ROWS = 96


def _pack_shards(shards, dtype):
    return jnp.concatenate([shards[n].astype(dtype).reshape(-1, _LANES) for n, _, _ in _SHARDED], axis=0)


def _unpack_shards(packed):
    out, r0 = {}, 0
    for (n, _, shp), rows in zip(_SHARDED, _PACK_ROWS):
        out[n] = packed[r0:r0 + rows].reshape(shp)
        r0 += rows
    return out


def _unpack_gathered(g):
    out, r0 = {}, 0
    for (n, axis, shp), rows in zip(_SHARDED, _PACK_ROWS):
        blk = g[:, r0:r0 + rows].reshape((N_DEV,) + shp)
        out[n] = blk.reshape(N_DEV * shp[0], shp[1]) if axis == 0 else blk.transpose(1, 0, 2).reshape(shp[0], N_DEV * shp[1])
        r0 += rows
    return out


def _pack_full_grads(gw):
    parts = []
    for n, axis, shp in _SHARDED:
        g = gw[n]
        blk = g.reshape((N_DEV,) + shp) if axis == 0 else g.reshape(shp[0], N_DEV, shp[1]).transpose(1, 0, 2)
        parts.append(blk.reshape(N_DEV, -1, _LANES))
    return jnp.concatenate(parts, axis=1)


def _pack_small(vals, extra=None):
    flat = [vals[n].reshape(-1) for n, _ in _SMALL]
    used = sum(sz for _, sz in _SMALL)
    tail = jnp.zeros((_SMALL_ROWS * _LANES - used,), F32)
    if extra is not None:
        tail = tail.at[0].set(extra)
    return jnp.concatenate(flat + [tail]).reshape(_SMALL_ROWS, _LANES)


def _unpack_small(packed):
    flat, out, o = packed.reshape(-1), {}, 0
    for n, sz in _SMALL:
        out[n] = flat[o:o + sz]
        o += sz
    return out, flat[o]


_ANY = pl.BlockSpec(memory_space=pl.ANY)


def _all_gather(block):
    rows, lanes = block.shape

    def body(x_ref, out_ref, send_sems, recv_sems, local_sem):
        x, y, c = lax.axis_index("x"), lax.axis_index("y"), lax.axis_index("c")
        me, sibling = (x, y, c), (x, y, 1 - c)
        chips = [(1 - x, y), (x, 1 - y), (1 - x, 1 - y)]

        def slot(px, py, pc):
            return out_ref.at[4 * px + 2 * py + pc]

        def copy(k, blk, to, src=None):
            return pltpu.make_async_remote_copy(
                src_ref=slot(*blk) if src is None else src, dst_ref=slot(*blk),
                send_sem=send_sems.at[k], recv_sem=recv_sems.at[k], device_id=to, device_id_type=MESH)

        mine = pltpu.make_async_copy(x_ref, slot(*me), local_sem)
        mine.start()
        first = [copy(0, me, sibling, src=x_ref)]
        first += [copy(1 + j, me, (*chip, c), src=x_ref) for j, chip in enumerate(chips)]
        for cp in first:
            cp.start()
        passed = [copy(4 + j, (*chip, c), sibling) for j, chip in enumerate(chips)]
        for j, chip in enumerate(chips):
            copy(1 + j, (*chip, c), me).wait_recv()
            passed[j].start()
        copy(0, sibling, me).wait_recv()
        for j, chip in enumerate(chips):
            copy(4 + j, (*chip, 1 - c), me).wait_recv()
        for cp in first + passed:
            cp.wait_send()
        mine.wait()

    return pl.pallas_call(
        body, name="all_gather_weights", in_specs=[_ANY], out_specs=_ANY,
        out_shape=jax.ShapeDtypeStruct((N_DEV, rows, lanes), block.dtype),
        scratch_shapes=[pltpu.SemaphoreType.DMA((7,)), pltpu.SemaphoreType.DMA((7,)), pltpu.SemaphoreType.DMA],
    )(block)


def _exchange_core(pack, small):
    _, _, rows, lanes = pack.shape

    def body(pack_ref, small_ref, got_ref, parts_ref, send_sems, recv_sems, s_send, s_recv, local_sem):
        x, y, c = lax.axis_index("x"), lax.axis_index("y"), lax.axis_index("c")
        sibling = (x, y, 1 - c)
        me = 4 * x + 2 * y + c
        mine = pltpu.make_async_copy(small_ref, parts_ref.at[me], local_sem)
        mine.start()
        big = [pltpu.make_async_remote_copy(
            src_ref=pack_ref.at[1 - c, j], dst_ref=got_ref.at[j], send_sem=send_sems.at[j], recv_sem=recv_sems.at[j],
            device_id=sibling, device_id_type=MESH) for j in range(4)]
        for cp in big:
            cp.start()
        others = [(k, (x ^ (k >> 2), y ^ ((k >> 1) & 1), c ^ (k & 1))) for k in range(1, N_DEV)]
        tiny = [pltpu.make_async_remote_copy(
            src_ref=small_ref, dst_ref=parts_ref.at[me], send_sem=s_send.at[k], recv_sem=s_recv.at[k],
            device_id=to, device_id_type=MESH) for k, to in others]
        for cp in tiny:
            cp.start()
        for cp in big:
            cp.wait_recv()
        for (k, (px, py, pc)), cp in zip(others, tiny):
            pltpu.make_async_remote_copy(
                src_ref=small_ref, dst_ref=parts_ref.at[4 * px + 2 * py + pc], send_sem=s_send.at[k],
                recv_sem=s_recv.at[k], device_id=(px, py, pc), device_id_type=MESH).wait_recv()
        for cp in big + tiny:
            cp.wait_send()
        mine.wait()

    return pl.pallas_call(
        body, name="exchange_core", in_specs=[_ANY, _ANY], out_specs=[_ANY, _ANY],
        out_shape=[jax.ShapeDtypeStruct((4, rows, lanes), F32), jax.ShapeDtypeStruct((N_DEV,) + small.shape, F32)],
        scratch_shapes=[pltpu.SemaphoreType.DMA((4,)), pltpu.SemaphoreType.DMA((4,)), pltpu.SemaphoreType.DMA((N_DEV,)),
                        pltpu.SemaphoreType.DMA((N_DEV,)), pltpu.SemaphoreType.DMA],
    )(pack, small)


def _add_core_parts(pack, got, core):
    _, _, rows, lanes = pack.shape
    tr = 2000

    def body(core_ref, a_ref, b_ref, o_ref):
        o_ref[...] = a_ref[...] + b_ref[...]

    return pl.pallas_call(
        body, name="add_core_parts",
        grid_spec=pltpu.PrefetchScalarGridSpec(
            num_scalar_prefetch=1, grid=(4, rows // tr),
            in_specs=[pl.BlockSpec((None, None, tr, lanes), lambda j, i, core_ref: (core_ref[0], j, i, 0)),
                      pl.BlockSpec((None, tr, lanes), lambda j, i, core_ref: (j, i, 0))],
            out_specs=pl.BlockSpec((None, tr, lanes), lambda j, i, core_ref: (j, i, 0))),
        out_shape=jax.ShapeDtypeStruct((4, rows, lanes), F32),
        compiler_params=_params(("parallel", "parallel")),
    )(core, pack, got)


def _exchange_chips(chip_sums):
    _, rows, lanes = chip_sums.shape

    def body(src_ref, got_ref, send_sems, recv_sems):
        x, y, c = lax.axis_index("x"), lax.axis_index("y"), lax.axis_index("c")
        flips = [(1, 0), (0, 1), (1, 1)]
        copies = []
        for k, (fx, fy) in enumerate(flips):
            px, py = x ^ fx, y ^ fy
            copies.append(pltpu.make_async_remote_copy(
                src_ref=src_ref.at[2 * px + py], dst_ref=got_ref.at[k], send_sem=send_sems.at[k],
                recv_sem=recv_sems.at[k], device_id=(px, py, c), device_id_type=MESH))
        for cp in copies:
            cp.start()
        for cp in copies:
            cp.wait_recv()
        for cp in copies:
            cp.wait_send()

    return pl.pallas_call(
        body, name="exchange_chips", in_specs=[_ANY], out_specs=_ANY,
        out_shape=jax.ShapeDtypeStruct((3, rows, lanes), F32),
        scratch_shapes=[pltpu.SemaphoreType.DMA((3,)), pltpu.SemaphoreType.DMA((3,))],
    )(chip_sums)


def _sum_chip_parts(chip_sums, got, chip):
    _, rows, lanes = chip_sums.shape
    tr = 2000

    def body(chip_ref, own_ref, got_ref, o_ref):
        o_ref[...] = ((own_ref[...] + got_ref[0]) + got_ref[1]) + got_ref[2]

    return pl.pallas_call(
        body, name="sum_chip_parts",
        grid_spec=pltpu.PrefetchScalarGridSpec(
            num_scalar_prefetch=1, grid=(rows // tr,),
            in_specs=[pl.BlockSpec((None, tr, lanes), lambda i, chip_ref: (chip_ref[0], i, 0)),
                      pl.BlockSpec((3, tr, lanes), lambda i, chip_ref: (0, i, 0))],
            out_specs=pl.BlockSpec((tr, lanes), lambda i, chip_ref: (i, 0))),
        out_shape=jax.ShapeDtypeStruct((rows, lanes), F32),
        compiler_params=_params(("parallel",)),
    )(chip, chip_sums, got)


def _adamw_math(w, g, m, v):
    m = ADAM_B1 * m + (1.0 - ADAM_B1) * g
    v = ADAM_B2 * v + (1.0 - ADAM_B2) * (g * g)
    m_hat = m / (1.0 - ADAM_B1 ** ADAM_STEP)
    v_hat = v / (1.0 - ADAM_B2 ** ADAM_STEP)
    return -ADAM_LR * (m_hat / (jnp.sqrt(v_hat) + ADAM_EPS) + ADAM_WD * w), m, v


def _adamw(w, g, m, v, name):
    rows, cols = w.shape
    tr = 256 if rows % 256 == 0 and rows * cols > 2 ** 19 else rows

    def body(w_ref, g_ref, m_ref, v_ref, d_o, m_o, v_o):
        d_o[...], m_o[...], v_o[...] = _adamw_math(w_ref[...], g_ref[...], m_ref[...], v_ref[...])

    spec = pl.BlockSpec((tr, cols), lambda i: (i, 0))
    return pl.pallas_call(
        body, name=name, grid=(rows // tr,), in_specs=[spec] * 4, out_specs=[spec] * 3,
        out_shape=[jax.ShapeDtypeStruct((rows, cols), F32)] * 3, compiler_params=_params(("parallel",)),
    )(w, g, m, v)


def _adamw_small(parts, w, m, v):
    def body(p_ref, w_ref, m_ref, v_ref, g_o, d_o, m_o, v_o):
        g = p_ref[0]
        for d in range(1, N_DEV):
            g = g + p_ref[d]
        g_o[...] = g
        d_o[...], m_o[...], v_o[...] = _adamw_math(w_ref[...], g, m_ref[...], v_ref[...])

    return pl.pallas_call(
        body, name="adamw_small", out_shape=[jax.ShapeDtypeStruct(w.shape, F32)] * 4, compiler_params=_params(),
    )(parts, w, m, v)


_WEIGHT_NAMES = ['norm_mix_pre', 'w_in', 'b_gate', 'mu_rw', 'w0', 'w_up', 'a0', 'a_up', 'g_up', 'k_k', 'k_a', 'r_k',
                 'lnx_w', 'lnx_b', 'w_sb_out', 'w_rw_out', 'w_o', 'norm_mix_post', 'norm_ffn_pre', 'w_ffn_gate',
                 'w_ffn_up', 'w_ffn_down', 'norm_ffn_post']


def _step(x, target, w, m, v):
    sharded = [n for n, _, _ in _SHARDED]
    sm = {n: w[n].reshape(1, -1) for n, _ in _SMALL}
    own = {n: w[n][0] for n in sharded}
    wt = _unpack_gathered(_all_gather(_pack_shards(own, BF16)))
    loss_part, grad_x, gw, gs = _local_step(x, target, sm, wt)

    cx, cy, cc = lax.axis_index("x"), lax.axis_index("y"), lax.axis_index("c")
    core = jnp.reshape(cc, (1,)).astype(jnp.int32)
    chip = jnp.reshape(2 * cx + cy, (1,)).astype(jnp.int32)
    pack = _pack_full_grads(gw).reshape(4, 2, _PACK_TOTAL, _LANES).transpose(1, 0, 2, 3)
    got_core, small_parts = _exchange_core(pack, _pack_small(gs, loss_part[0, 0]))
    chip_sums = _add_core_parts(pack, got_core, core)
    grads_packed = _sum_chip_parts(chip_sums, _exchange_chips(chip_sums), chip)
    g_sh = _unpack_shards(grads_packed)

    g_small, d_small, m_small, v_small = _adamw_small(
        small_parts, _pack_small({n: w[n] for n, _ in _SMALL}), _pack_small({n: m[n] for n, _ in _SMALL}),
        _pack_small({n: v[n] for n, _ in _SMALL}))
    (g_s, loss), (d_s, _), (m_s, _), (v_s, _) = (_unpack_small(t) for t in (g_small, d_small, m_small, v_small))

    grads, deltas, new_m, new_v = {}, {}, {}, {}
    for n in _WEIGHT_NAMES:
        if n in g_sh:
            d_, m_, v_ = _adamw(own[n], g_sh[n], m[n][0], v[n][0], "adamw_" + n)
            grads[n], deltas[n], new_m[n], new_v[n] = (t.reshape(w[n].shape) for t in (g_sh[n], d_, m_, v_))
        else:
            grads[n], deltas[n], new_m[n], new_v[n] = (t[n].reshape(w[n].shape) for t in (g_s, d_s, m_s, v_s))
    return (loss, grad_x, *[grads[n] for n in _WEIGHT_NAMES], *[deltas[n] for n in _WEIGHT_NAMES],
            *[new_m[n] for n in _WEIGHT_NAMES], *[new_v[n] for n in _WEIGHT_NAMES])


def kernel(x, norm_mix_pre, w_in, b_gate, mu_rw, w0, w_up, a0, a_up, g_up, k_k, k_a, r_k, lnx_w, lnx_b, w_sb_out, w_rw_out, w_o, norm_mix_post, norm_ffn_pre, w_ffn_gate, w_ffn_up, w_ffn_down, norm_ffn_post, loss_target, m_norm_mix_pre, m_w_in, m_b_gate, m_mu_rw, m_w0, m_w_up, m_a0, m_a_up, m_g_up, m_k_k, m_k_a, m_r_k, m_lnx_w, m_lnx_b, m_w_sb_out, m_w_rw_out, m_w_o, m_norm_mix_post, m_norm_ffn_pre, m_w_ffn_gate, m_w_ffn_up, m_w_ffn_down, m_norm_ffn_post, v_norm_mix_pre, v_w_in, v_b_gate, v_mu_rw, v_w0, v_w_up, v_a0, v_a_up, v_g_up, v_k_k, v_k_a, v_r_k, v_lnx_w, v_lnx_b, v_w_sb_out, v_w_rw_out, v_w_o, v_norm_mix_post, v_norm_ffn_pre, v_w_ffn_gate, v_w_ffn_up, v_w_ffn_down, v_norm_ffn_post):
    args = locals()
    w = {n: args[n] for n in _WEIGHT_NAMES}
    m = {n: args["m_" + n] for n in _WEIGHT_NAMES}
    v = {n: args["v_" + n] for n in _WEIGHT_NAMES}
    return _step(x, loss_target, w, m, v)
```

```python
import functools

import jax
import jax.numpy as jnp
from jax import lax
from jax.experimental import pallas as pl
from jax.experimental.pallas import tpu as pltpu

F32 = jnp.float32
BF16 = jnp.bfloat16

D_MODEL = 1024
HEADS = 8
HEAD_DIM = 64
WIDTH = HEADS * HEAD_DIM
W_LORA, A_LORA, G_LORA = 64, 64, 128
SB_COLS = 3 * WIDTH
RW_COLS = 3 * WIDTH + W_LORA + A_LORA + G_LORA
GATE_COLS = 2 * D_MODEL
D_FF = 2816
RMS_EPS = 1e-6
GN_EPS = HEAD_DIM * 1e-5
N_DEV = 8

ADAM_LR, ADAM_B1, ADAM_B2, ADAM_EPS, ADAM_WD, ADAM_STEP = 0.001, 0.9, 0.999, 1e-08, 0.01, 10

ROW_TILE = 256
SCAN_CHUNK = 64
ATT_ALIGN = 128
ATT_WINDOW = 384
ATT_Q = 128
SB_DEAD = -104.0
SCAN_PASSES = 1
VMEM_LIMIT = 56 * 2 ** 20

MESH = pl.DeviceIdType.MESH


def _params(sem=None, vmem=VMEM_LIMIT):
    kw = dict(vmem_limit_bytes=vmem)
    if sem is not None:
        kw["dimension_semantics"] = sem
    return pltpu.CompilerParams(**kw)


def _const_spec(shape):
    nd = len(shape)
    return pl.BlockSpec(shape, lambda *_: (0,) * nd, pipeline_mode=pl.Buffered(1))


def _row_spec(tm, n):
    return pl.BlockSpec((tm, n), lambda i: (i, 0))


def _mm(a, b):
    return lax.dot_general(a, b, (((1,), (0,)), ((), ())), preferred_element_type=F32)


def _mm_nt(a, b):
    return lax.dot_general(a, b, (((1,), (1,)), ((), ())), preferred_element_type=F32)


def _mm_tn(a, b):
    return lax.dot_general(a, b, (((0,), (0,)), ((), ())), preferred_element_type=F32)


def _softplus(z):
    return jnp.maximum(z, 0.0) + jnp.log1p(jnp.exp(-jnp.abs(z)))


def _rms_fwd(x, gain):
    rstd = lax.rsqrt(jnp.mean(x * x, axis=-1, keepdims=True) + RMS_EPS)
    xn = x * rstd
    return xn * gain, xn, rstd


def _rms_bwd(dy, xn, rstd, gain):
    u = dy * gain
    dx = rstd * (u - xn * jnp.mean(u * xn, axis=-1, keepdims=True))
    return dx, jnp.sum(dy * xn, axis=0, keepdims=True)


def _acc_out(ref, val, first):
    @pl.when(first)
    def _():
        ref[...] = val

    @pl.when(jnp.logical_not(first))
    def _():
        ref[...] += val


def _in_proj_fwd(x2, g_pre, w_qkv, w_rw, w_gate, b_gate, tm):
    T = x2.shape[0]

    def body(x_ref, g_ref, wq_ref, wr_ref, wg_ref, b_ref, h_ref, qkv_ref, prw_ref, gate_ref):
        h = _rms_fwd(x_ref[...], g_ref[...])[0].astype(BF16)
        h_ref[...] = h
        qkv_ref[...] = _mm(h, wq_ref[...]).astype(BF16)
        prw_ref[...] = _mm(h, wr_ref[...])
        gate_ref[...] = jax.nn.sigmoid(_mm(h, wg_ref[...]) + b_ref[...])

    return pl.pallas_call(
        body, name="in_proj_fwd", grid=(T // tm,),
        in_specs=[_row_spec(tm, D_MODEL), _const_spec((1, D_MODEL)), _const_spec((D_MODEL, SB_COLS)),
                  _const_spec((D_MODEL, RW_COLS)), _const_spec((D_MODEL, GATE_COLS)), _const_spec((1, GATE_COLS))],
        out_specs=[_row_spec(tm, D_MODEL), _row_spec(tm, SB_COLS), _row_spec(tm, RW_COLS), _row_spec(tm, GATE_COLS)],
        out_shape=[jax.ShapeDtypeStruct((T, D_MODEL), BF16), jax.ShapeDtypeStruct((T, SB_COLS), BF16),
                   jax.ShapeDtypeStruct((T, RW_COLS), F32), jax.ShapeDtypeStruct((T, GATE_COLS), F32)],
        compiler_params=_params(("parallel",)),
    )(x2, g_pre, w_qkv, w_rw, w_gate, b_gate)


def _in_proj_bwd(x2, g_pre, dx1, dqkv, dprw, dgate, w_qkv, w_rw, w_gate, tm):
    T = x2.shape[0]

    def body(x_ref, g_ref, dx1_ref, dq_ref, dr_ref, dg_ref, wq_ref, wr_ref, wg_ref, gx_ref, dgain_ref):
        dh = _mm_nt(dq_ref[...], wq_ref[...]) + _mm_nt(dr_ref[...], wr_ref[...]) + _mm_nt(dg_ref[...], wg_ref[...])
        gain = g_ref[...]
        _, xn, rstd = _rms_fwd(x_ref[...], gain)
        dx, dgain = _rms_bwd(dh, xn, rstd, gain)
        gx_ref[...] = dx1_ref[...] + dx
        _acc_out(dgain_ref, dgain, pl.program_id(0) == 0)

    return pl.pallas_call(
        body, name="in_proj_bwd", grid=(T // tm,),
        in_specs=[_row_spec(tm, D_MODEL), _const_spec((1, D_MODEL)), _row_spec(tm, D_MODEL), _row_spec(tm, SB_COLS),
                  _row_spec(tm, RW_COLS), _row_spec(tm, GATE_COLS), _const_spec((D_MODEL, SB_COLS)),
                  _const_spec((D_MODEL, RW_COLS)), _const_spec((D_MODEL, GATE_COLS))],
        out_specs=[_row_spec(tm, D_MODEL), pl.BlockSpec((1, D_MODEL), lambda i: (0, 0))],
        out_shape=[jax.ShapeDtypeStruct((T, D_MODEL), F32), jax.ShapeDtypeStruct((1, D_MODEL), F32)],
        compiler_params=_params(("arbitrary",)),
    )(x2, g_pre, dx1, dqkv, dprw, dgate, w_qkv, w_rw, w_gate)


def _pick_tile(n, cap):
    best = None
    for t in range(128, min(n, cap) + 1, 128):
        if n % t == 0:
            best = t
    return n if best is None else best


def _grad_w(a, b, name):
    T, K = a.shape
    N = b.shape[1]
    tk, tn, tt = _pick_tile(K, 1408), _pick_tile(N, 2048), min(T, 512)

    def body(a_ref, b_ref, o_ref):
        _acc_out(o_ref, _mm_tn(a_ref[...], b_ref[...]), pl.program_id(2) == 0)

    return pl.pallas_call(
        body, name=name, grid=(K // tk, N // tn, T // tt),
        in_specs=[pl.BlockSpec((tt, tk), lambda i, j, t: (t, i)), pl.BlockSpec((tt, tn), lambda i, j, t: (t, j))],
        out_specs=pl.BlockSpec((tk, tn), lambda i, j, t: (i, j)),
        out_shape=jax.ShapeDtypeStruct((K, N), F32),
        compiler_params=_params(("parallel", "parallel", "arbitrary")),
    )(a, b)


def _tri(n, kind):
    r = lax.broadcasted_iota(jnp.int32, (n, n), 0)
    c = lax.broadcasted_iota(jnp.int32, (n, n), 1)
    return {"gt": r > c, "le": r <= c, "lt": r < c, "ge": r >= c}[kind]


def _split_mm(x, u):
    hi = x.astype(BF16)
    lo = (x - hi.astype(F32)).astype(BF16)
    return _mm(hi, u) + _mm(lo, u)


def _sb_scores(qs, k, row0, col0, first, last, qb, kb):
    z = _mm_nt(qs, k)
    sp = _softplus(z)
    row = lax.broadcasted_iota(jnp.int32, (qb, kb), 0) + row0
    col = lax.broadcasted_iota(jnp.int32, (qb, kb), 1) + col0
    valid = jnp.logical_and(col < row, jnp.logical_and(col >= first, col < last))
    return z, sp, valid


_PAIR = 2 * HEAD_DIM
_PAIRS = WIDTH // _PAIR


def _first_head_lanes():
    return lax.broadcasted_iota(jnp.int32, (1, _PAIR), 1) < HEAD_DIM


def _per_head(t, first_head):
    zero = jnp.zeros_like(t)
    return jnp.where(first_head, t, zero), jnp.where(first_head, zero, t)


def _sb_fwd(qkv, bl, seq):
    qb, win = min(ATT_Q, seq), min(ATT_WINDOW, seq)
    nq = seq // qb

    def body(q_ref, k_ref, v_ref, o_ref, l_ref, stop_ref):
        g = pl.program_id(0) * _PAIRS + pl.program_id(1)
        first_head = _first_head_lanes()
        u_after = _tri(win, "gt").astype(BF16)

        def qblock(i, _):
            rows = pl.ds(pl.multiple_of(i * qb, qb), qb)
            qh = _per_head(q_ref[rows, :] * (HEAD_DIM ** -0.5), first_head)

            def live(carry):
                return jnp.logical_and(carry[0] > 0, carry[3] > 0)

            def window(carry):
                hi, accs, cs, _ = carry
                lo = pl.multiple_of(jnp.maximum(hi - win, 0), ATT_ALIGN)
                cols = pl.ds(lo, win)
                kv, vv = k_ref[cols, :], v_ref[cols, :]
                new_accs, new_cs = [], []
                for h in range(2):
                    z, sp, valid = _sb_scores(qh[h], kv, i * qb, lo, lo, hi, qb, win)
                    lfm = jnp.where(valid, -sp, 0.0)
                    after = _split_mm(lfm, u_after) + cs[h]
                    w = jnp.where(valid, jnp.exp(z - sp + after), 0.0)
                    new_accs.append(accs[h] + _mm(w.astype(BF16), vv))
                    new_cs.append(cs[h] + jnp.sum(lfm, axis=1, keepdims=True))
                alive = jnp.maximum(jnp.max(new_cs[0]), jnp.max(new_cs[1])) > SB_DEAD
                return lo, tuple(new_accs), tuple(new_cs), alive.astype(jnp.int32)

            zero_acc, zero_c = jnp.zeros((qb, _PAIR), F32), jnp.zeros((qb, 1), F32)
            lo, accs, cs, _ = lax.while_loop(
                live, window, ((i + 1) * qb, (zero_acc, zero_acc), (zero_c, zero_c), jnp.int32(1)))
            o_ref[rows, :] = jnp.where(first_head, accs[0], accs[1]).astype(BF16)
            l_ref[0, rows, :] = cs[0]
            l_ref[1, rows, :] = cs[1]
            stop_ref[g, i] = lo
            return 0

        lax.fori_loop(0, nq, qblock, 0)

    col = lambda off: pl.BlockSpec((seq, _PAIR), lambda b, p: (b, off + p))
    return pl.pallas_call(
        body, name="sb_fwd", grid=(bl, _PAIRS), in_specs=[col(0), col(_PAIRS), col(2 * _PAIRS)],
        out_specs=[col(0), pl.BlockSpec((None, 2, seq, 1), lambda b, p: (b, p, 0, 0)),
                   pl.BlockSpec(memory_space=pltpu.SMEM)],
        out_shape=[jax.ShapeDtypeStruct((bl * seq, WIDTH), BF16), jax.ShapeDtypeStruct((bl, HEADS, seq, 1), F32),
                   jax.ShapeDtypeStruct((bl * _PAIRS, nq), jnp.int32)],
        compiler_params=_params(("arbitrary", "arbitrary")),
    )(qkv, qkv, qkv)


def _sb_bwd(qkv, do, lsum, stop, bl, seq):
    qb, win = min(ATT_Q, seq), min(ATT_WINDOW, seq)
    nq = seq // qb

    def body(stop_ref, q_ref, k_ref, v_ref, do_ref, l_ref, dq_ref, dk_ref, dv_ref, dk_acc, dv_acc):
        g = pl.program_id(0) * _PAIRS + pl.program_id(1)
        first_head = _first_head_lanes()
        u_incl = _tri(win, "le").astype(BF16)
        u_excl = _tri(win, "lt").astype(BF16)
        dk_acc[...] = jnp.zeros_like(dk_acc)
        dv_acc[...] = jnp.zeros_like(dv_acc)

        def qblock(i, _):
            rows = pl.ds(pl.multiple_of(i * qb, qb), qb)
            qv = q_ref[rows, :]
            qh = _per_head(qv * (HEAD_DIM ** -0.5), first_head)
            dob = do_ref[rows, :]
            doh = _per_head(dob, first_head)
            ltot = (l_ref[0, rows, :], l_ref[1, rows, :])

            first = (jnp.clip(stop_ref[g, i], 0, i * qb) // ATT_ALIGN) * ATT_ALIGN

            def window(n, carry):
                dqs, ps, es = carry
                start = first + n * win
                lo = pl.multiple_of(jnp.minimum(start, seq - win), ATT_ALIGN)
                cols = pl.ds(lo, win)
                kv, vv = k_ref[cols, :], v_ref[cols, :]
                new_dqs, new_ps, new_es, dks, dvs = [], [], [], [], []
                for h in range(2):
                    z, sp, valid = _sb_scores(qh[h], kv, i * qb, lo, start, seq, qb, win)
                    lfm = jnp.where(valid, -sp, 0.0)
                    after = ltot[h] - ps[h] - _split_mm(lfm, u_incl)
                    w = jnp.where(valid, jnp.exp(z - sp + after), 0.0)
                    e = _mm_nt(doh[h], vv) * w
                    dlf = es[h] + _split_mm(e, u_excl)
                    sig = jnp.exp(z - sp)
                    dz = jnp.where(valid, e * (1.0 - sig) - dlf * sig, 0.0) * (HEAD_DIM ** -0.5)
                    dzb = dz.astype(BF16)
                    dvs.append(_mm_tn(w.astype(BF16), dob))
                    dks.append(_mm_tn(dzb, qv))
                    new_dqs.append(dqs[h] + _mm(dzb, kv))
                    new_ps.append(ps[h] + jnp.sum(lfm, axis=1, keepdims=True))
                    new_es.append(es[h] + jnp.sum(e, axis=1, keepdims=True))
                dv_acc[cols, :] += jnp.where(first_head, dvs[0], dvs[1])
                dk_acc[cols, :] += jnp.where(first_head, dks[0], dks[1])
                return tuple(new_dqs), tuple(new_ps), tuple(new_es)

            zero_q, zero_c = jnp.zeros((qb, _PAIR), F32), jnp.zeros((qb, 1), F32)
            dqs, _, _ = lax.fori_loop(0, ((i + 1) * qb - first + win - 1) // win, window,
                                      ((zero_q, zero_q), (zero_c, zero_c), (zero_c, zero_c)))
            dq_ref[rows, :] = jnp.where(first_head, dqs[0], dqs[1]).astype(BF16)
            return 0

        lax.fori_loop(0, nq, qblock, 0)
        dk_ref[...] = dk_acc[...].astype(BF16)
        dv_ref[...] = dv_acc[...].astype(BF16)

    col = lambda off: pl.BlockSpec((seq, _PAIR), lambda b, p, stop_ref: (b, off + p))
    return pl.pallas_call(
        body, name="sb_bwd",
        grid_spec=pltpu.PrefetchScalarGridSpec(
            num_scalar_prefetch=1, grid=(bl, _PAIRS),
            in_specs=[col(0), col(_PAIRS), col(2 * _PAIRS), col(0),
                      pl.BlockSpec((None, 2, seq, 1), lambda b, p, stop_ref: (b, p, 0, 0))],
            out_specs=[col(0), col(0), col(0)],
            scratch_shapes=[pltpu.VMEM((seq, _PAIR), F32), pltpu.VMEM((seq, _PAIR), F32)]),
        out_shape=[jax.ShapeDtypeStruct((bl * seq, WIDTH), BF16)] * 3,
        compiler_params=_params(("parallel", "parallel")),
    )(stop, qkv, qkv, qkv, do, lsum)


@jax.custom_vjp
def _lora_mm(x, w):
    return _mm(x.astype(BF16), w.astype(BF16))


_lora_mm.defvjp(
    lambda x, w: (_mm(x.astype(BF16), w.astype(BF16)), (x, w)),
    lambda res, ct: (_mm_nt(ct.astype(BF16), res[1].astype(BF16)), _mm_tn(res[0].astype(BF16), ct.astype(BF16))))


def _rw_prep_math(p, ps, mu, w0, w_up, a0, a_up, g_up, k_k, k_a):
    pm = p + (ps - p) * mu
    r, k, v = pm[:, :WIDTH], pm[:, WIDTH:2 * WIDTH], pm[:, 2 * WIDTH:3 * WIDTH]
    o = 3 * WIDTH
    xw, xa, xg = pm[:, o:o + W_LORA], pm[:, o + W_LORA:o + W_LORA + A_LORA], pm[:, o + W_LORA + A_LORA:]
    w_raw = w0 + _lora_mm(jnp.tanh(xw), w_up)
    lw = -jnp.exp(-_softplus(-w_raw) - 0.5)
    a = jax.nn.sigmoid(a0 + _lora_mm(xa, a_up))
    g = _lora_mm(jax.nn.sigmoid(xg), g_up)
    kk = k * k_k
    k2 = k * (1.0 + (a - 1.0) * k_a)
    return r, lw, k2, v, kk, a, g


def _shift_down(p, first_row):
    row = lax.broadcasted_iota(jnp.int32, p.shape, 0)
    return jnp.where(row == 0, first_row, pltpu.roll(p, 1, 0))


def _shift_up(p, last_row):
    row = lax.broadcasted_iota(jnp.int32, p.shape, 0)
    return jnp.where(row == p.shape[0] - 1, last_row, pltpu.roll(p, p.shape[0] - 1, 0))


_PREP_PARAM_SHAPES = [(1, RW_COLS), (1, WIDTH), (W_LORA, WIDTH), (1, WIDTH), (A_LORA, WIDTH), (G_LORA, WIDTH),
                      (1, WIDTH), (1, WIDTH)]


def _prev_rows_spec(tm):
    return pl.BlockSpec((8, RW_COLS), lambda i: (jnp.maximum(i * (tm // 8) - 1, 0), 0))


def _head_spec(tm, seq, tile_of=lambda i: i):
    per_seq = seq // tm
    return pl.BlockSpec((None, HEADS, tm, HEAD_DIM),
                        lambda i: (tile_of(i) // per_seq, 0, tile_of(i) % per_seq, 0))


def _split_heads(val, ref):
    for h in range(HEADS):
        ref[h] = val[:, h * HEAD_DIM:(h + 1) * HEAD_DIM]


def _join_heads(ref):
    return jnp.concatenate([ref[h] for h in range(HEADS)], axis=1)


def _rw_prep_fwd(prw, params, seq, tm):
    T = prw.shape[0]

    def body(p_ref, prev_ref, *rest):
        prm = [r_[...] for r_ in rest[:8]]
        outs = rest[8:]
        i = pl.program_id(0)
        first = jnp.where((i * tm) % seq == 0, 0.0, prev_ref[7:8, :])
        p = p_ref[...]
        vals = _rw_prep_math(p, _shift_down(p, first), *prm)
        for o_ref, val in zip(outs[:6], vals[:6]):
            _split_heads(val, o_ref)
        outs[6][...] = vals[6]

    by_head = jax.ShapeDtypeStruct((T // seq, HEADS, seq, HEAD_DIM), F32)
    return pl.pallas_call(
        body, name="rw_prep_fwd", grid=(T // tm,),
        in_specs=[_row_spec(tm, RW_COLS), _prev_rows_spec(tm)] + [_const_spec(s) for s in _PREP_PARAM_SHAPES],
        out_specs=[_head_spec(tm, seq)] * 6 + [_row_spec(tm, WIDTH)],
        out_shape=[by_head] * 6 + [jax.ShapeDtypeStruct((T, WIDTH), F32)],
        compiler_params=_params(("parallel",)),
    )(prw, prw, *params)


def _rw_prep_bwd(prw, params, cts, seq, tm):
    T = prw.shape[0]
    n = T // tm

    def body(p_ref, prev_ref, *rest):
        prm = [r_[...] for r_ in rest[:8]]
        ct = tuple(_join_heads(r_) for r_ in rest[8:14]) + (rest[14][...],)
        dp_ref = rest[15]
        dprm_refs = rest[16:24]
        carry = rest[24]
        step = pl.program_id(0)
        i = n - 1 - step
        first = jnp.where((i * tm) % seq == 0, 0.0, prev_ref[7:8, :])
        p = p_ref[...]
        _, vjp = jax.vjp(_rw_prep_math, p, _shift_down(p, first), *prm)
        grads = vjp(ct)
        dp, dps = grads[0], grads[1]
        nxt = jnp.where(jnp.logical_or(step == 0, ((i + 1) * tm) % seq == 0), 0.0, carry[0:1, :])
        dp_ref[...] = (dp + _shift_up(dps, nxt)).astype(BF16)
        carry[...] = dps[0:8, :]
        for ref, gval in zip(dprm_refs, grads[2:]):
            _acc_out(ref, gval, step == 0)

    rev = lambda w: pl.BlockSpec((tm, w), lambda s: (n - 1 - s, 0))
    prev = pl.BlockSpec((8, RW_COLS), lambda s: (jnp.maximum((n - 1 - s) * (tm // 8) - 1, 0), 0))
    return pl.pallas_call(
        body, name="rw_prep_bwd", grid=(n,),
        in_specs=([rev(RW_COLS), prev] + [_const_spec(s) for s in _PREP_PARAM_SHAPES]
                  + [_head_spec(tm, seq, lambda s: n - 1 - s)] * 6 + [rev(WIDTH)]),
        out_specs=[rev(RW_COLS)] + [pl.BlockSpec(s, lambda s_: (0, 0)) for s in _PREP_PARAM_SHAPES],
        out_shape=[jax.ShapeDtypeStruct((T, RW_COLS), BF16)] + [jax.ShapeDtypeStruct(s, F32) for s in _PREP_PARAM_SHAPES],
        scratch_shapes=[pltpu.VMEM((8, RW_COLS), F32)],
        compiler_params=_params(("arbitrary",)),
    )(prw, prw, *params, *cts)


def _make_bmm(passes):
    def raw(dn, a, b):
        d = lambda x, y: lax.dot_general(x, y, dn, preferred_element_type=F32)
        ah = a.astype(BF16)
        bh = b.astype(BF16)
        if passes == 1:
            return d(ah, bh)
        al = (a - ah.astype(F32)).astype(BF16)
        bl = (b - bh.astype(F32)).astype(BF16)
        return d(ah, bh) + (d(ah, bl) + d(al, bh))

    dn_nn = (((2,), (1,)), ((0,), (0,)))
    dn_nt = (((2,), (2,)), ((0,), (0,)))
    dn_tn = (((1,), (1,)), ((0,), (0,)))

    @jax.custom_vjp
    def nn(a, b):
        return raw(dn_nn, a, b)

    @jax.custom_vjp
    def nt(a, b):
        return raw(dn_nt, a, b)

    @jax.custom_vjp
    def tn(a, b):
        return raw(dn_tn, a, b)

    nn.defvjp(lambda a, b: (raw(dn_nn, a, b), (a, b)), lambda res, ct: (nt(ct, res[1]), tn(res[0], ct)))
    nt.defvjp(lambda a, b: (raw(dn_nt, a, b), (a, b)), lambda res, ct: (nn(ct, res[1]), tn(ct, res[0])))
    tn.defvjp(lambda a, b: (raw(dn_tn, a, b), (a, b)), lambda res, ct: (nt(res[1], ct), nn(res[0], ct)))

    def unit_lower_inverse(m):
        n = m.shape[-1]
        row = lax.broadcasted_iota(jnp.int32, (n, n), 0)
        col = lax.broadcasted_iota(jnp.int32, (n, n), 1)
        m16 = ((row // 16) == (col // 16)).astype(F32)
        m32 = ((row // 32) == (col // 32)).astype(F32)
        a1 = m * m16
        a2 = nn(a1, a1)
        a4 = nn(a2, a2)
        a8 = nn(a4, a4)
        inv = (row == col).astype(F32) - a1
        inv = inv + nn(inv, a2)
        inv = inv + nn(inv, a4)
        inv = inv + nn(inv, a8)
        inv = inv - nn(nn(inv, m * (m32 - m16)), inv)
        return inv - nn(nn(inv, m * (1.0 - m32)), inv)

    @jax.custom_vjp
    def inverse(m):
        return unit_lower_inverse(m)

    def inverse_fwd(m):
        inv = unit_lower_inverse(m)
        return inv, inv

    inverse.defvjp(inverse_fwd, lambda inv, ct: (-nt(tn(inv, ct), inv),))
    return nn, nt, tn, inverse


def _wkv_chunk(s0, r, lw, k, v, kk, a, lnw, lnb, rk):
    nn, nt, tn, inverse = _make_bmm(SCAN_PASSES)
    G, L, N = r.shape
    rep = lambda t: jnp.broadcast_to(t[None], (G // HEADS, HEADS, 1, N)).reshape(G, 1, N)
    kap = kk * lax.rsqrt(jnp.maximum(jnp.sum(kk * kk, axis=-1, keepdims=True), 1e-24))
    b = a * kap
    row = lax.broadcasted_iota(jnp.int32, (L, L), 0)
    col = lax.broadcasted_iota(jnp.int32, (L, L), 1)
    low_incl = (col <= row).astype(F32)
    low_strict = (col < row).astype(F32)
    c = _make_bmm(3)[0](jnp.broadcast_to(low_incl[None], (G, L, L)), lw)
    c_all = jnp.sum(lw, axis=1, keepdims=True)
    g_inv = jnp.exp(-c)
    kap_t = kap * jnp.exp(c - lw)
    b_t = b * g_inv
    k_t = k * g_inv
    r_t = r * jnp.exp(c)
    g_all = jnp.exp(c_all)
    m_b = nt(kap_t, b_t) * low_strict
    m_k = nt(kap_t, k_t) * low_strict
    n_b = nt(r_t, b_t) * low_incl
    n_k = nt(r_t, k_t) * low_incl
    rhs = -(nt(kap_t, s0) + nn(m_k, v))
    sa = nn(inverse(m_b), rhs)
    y = nt(r_t, s0) + nn(n_b, sa) + nn(n_k, v)
    s1 = s0 * g_all + tn(sa, b_t * g_all) + tn(v, k_t * g_all)
    mean = jnp.mean(y, axis=-1, keepdims=True)
    yc = y - mean
    var = jnp.mean(yc * yc, axis=-1, keepdims=True)
    out = yc * lax.rsqrt(var + GN_EPS) * rep(lnw) + rep(lnb)
    out = out + jnp.sum(r * k * rep(rk), axis=-1, keepdims=True) * v
    return out, s1


def _wkv_fwd(seqs, lnw, lnb, rk):
    G, S, N = seqs[0].shape
    L = SCAN_CHUNK
    nc = S // L

    def body(*refs):
        ins = [r_[...] for r_ in refs[:6]]
        prm = [r_[...] for r_ in refs[6:9]]
        out_ref, st_ref, state = refs[9], refs[10], refs[11]

        @pl.when(pl.program_id(1) == 0)
        def _():
            state[...] = jnp.zeros_like(state)

        s0 = state[...]
        st_ref[...] = s0
        out, s1 = _wkv_chunk(s0, *ins, *prm)
        out_ref[...] = out
        state[...] = s1

    blk = pl.BlockSpec((HEADS, L, N), lambda b, i: (b, i, 0))
    pspec = _const_spec((HEADS, 1, N))
    return pl.pallas_call(
        body, name="wkv_fwd", grid=(G // HEADS, nc), in_specs=[blk] * 6 + [pspec] * 3,
        out_specs=[blk, pl.BlockSpec((None, HEADS, N, N), lambda b, i: (i, b, 0, 0))],
        out_shape=[jax.ShapeDtypeStruct((G, S, N), F32), jax.ShapeDtypeStruct((nc, G, N, N), F32)],
        scratch_shapes=[pltpu.VMEM((HEADS, N, N), F32)],
        compiler_params=_params(("parallel", "arbitrary")),
    )(*seqs, lnw, lnb, rk)


def _wkv_bwd(seqs, states, dout, lnw, lnb, rk):
    G, S, N = seqs[0].shape
    L = SCAN_CHUNK
    nc = S // L

    def body(*refs):
        ins = [r_[...] for r_ in refs[:6]]
        s0 = refs[6][...]
        ct_out = refs[7][...]
        prm = [r_[...] for r_ in refs[8:11]]
        d_refs = refs[11:17]
        dprm_refs = refs[17:20]
        dstate = refs[20]
        step = pl.program_id(1)

        @pl.when(step == 0)
        def _():
            dstate[...] = jnp.zeros_like(dstate)

        _, vjp = jax.vjp(_wkv_chunk, s0, *ins, *prm)
        grads = vjp((ct_out, dstate[...]))
        dstate[...] = grads[0]
        for ref, gval in zip(d_refs, grads[1:7]):
            ref[...] = gval
        for ref, gval in zip(dprm_refs, grads[7:]):
            _acc_out(ref, gval, jnp.logical_and(step == 0, pl.program_id(0) == 0))

    blk = pl.BlockSpec((HEADS, L, N), lambda b, s: (b, nc - 1 - s, 0))
    pspec = _const_spec((HEADS, 1, N))
    pout = pl.BlockSpec((HEADS, 1, N), lambda b, s: (0, 0, 0))
    return pl.pallas_call(
        body, name="wkv_bwd", grid=(G // HEADS, nc),
        in_specs=[blk] * 6 + [pl.BlockSpec((None, HEADS, N, N), lambda b, s: (nc - 1 - s, b, 0, 0)), blk] + [pspec] * 3,
        out_specs=[blk] * 6 + [pout] * 3,
        out_shape=[jax.ShapeDtypeStruct((G, S, N), F32)] * 6 + [jax.ShapeDtypeStruct((HEADS, 1, N), F32)] * 3,
        scratch_shapes=[pltpu.VMEM((HEADS, N, N), F32)],
        compiler_params=_params(("arbitrary", "arbitrary")),
    )(*seqs, states, dout, lnw, lnb, rk)


def _merge_math(o_sb, rw_out, g_rw, gates, w_sb, w_rw, w_o):
    o_rw = (rw_out * g_rw).astype(BF16)
    a = _mm(o_sb, w_sb)
    b = _mm(o_rw, w_rw)
    g1, g2 = gates[:, :D_MODEL], gates[:, D_MODEL:]
    merged = (g1 * a + g2 * b).astype(BF16)
    return o_rw, a, b, g1, g2, merged, _mm(merged, w_o)


def _merge_fwd(x2, o_sb, rw_out, g_rw, gates, w_sb, w_rw, w_o, g_post, seq, tm):
    T = x2.shape[0]

    def body(x_ref, osb_ref, rw_ref, g_ref, gate_ref, wsb_ref, wrw_ref, wo_ref, gp_ref, x1_ref):
        z = _merge_math(osb_ref[...], _join_heads(rw_ref), g_ref[...], gate_ref[...], wsb_ref[...], wrw_ref[...], wo_ref[...])[-1]
        x1_ref[...] = x_ref[...] + _rms_fwd(z, gp_ref[...])[0]

    return pl.pallas_call(
        body, name="merge_fwd", grid=(T // tm,),
        in_specs=[_row_spec(tm, D_MODEL), _row_spec(tm, WIDTH), _head_spec(tm, seq), _row_spec(tm, WIDTH),
                  _row_spec(tm, GATE_COLS), _const_spec((WIDTH, D_MODEL)), _const_spec((WIDTH, D_MODEL)),
                  _const_spec((D_MODEL, D_MODEL)), _const_spec((1, D_MODEL))],
        out_specs=_row_spec(tm, D_MODEL),
        out_shape=jax.ShapeDtypeStruct((T, D_MODEL), F32),
        compiler_params=_params(("parallel",)),
    )(x2, o_sb, rw_out, g_rw, gates, w_sb, w_rw, w_o, g_post)


def _merge_bwd(dx1, o_sb, rw_out, g_rw, gates, w_sb, w_rw, w_o, g_post, seq, tm):
    T = dx1.shape[0]

    def body(dx1_ref, osb_ref, rw_ref, g_ref, gate_ref, wsb_ref, wrw_ref, wo_ref, gp_ref,
             orw_o, mrg_o, dz_o, da_o, db_o, dgate_o, dosb_o, drw_o, dg_o, dgp_o, dbg_o):
        rw_out_v, g_rw_v = _join_heads(rw_ref), g_ref[...]
        w_sb_v, w_rw_v, w_o_v = wsb_ref[...], wrw_ref[...], wo_ref[...]
        o_rw, a, b, g1, g2, merged, z = _merge_math(osb_ref[...], rw_out_v, g_rw_v, gate_ref[...], w_sb_v, w_rw_v, w_o_v)
        gain = gp_ref[...]
        _, zn, rstd = _rms_fwd(z, gain)
        dz, dgain = _rms_bwd(dx1_ref[...], zn, rstd, gain)
        dzb = dz.astype(BF16)
        dm = _mm_nt(dzb, w_o_v)
        dab = (dm * g1).astype(BF16)
        dbb = (dm * g2).astype(BF16)
        dgate = jnp.concatenate([dm * a * g1 * (1.0 - g1), dm * b * g2 * (1.0 - g2)], axis=1)
        do_rw = _mm_nt(dbb, w_rw_v)
        orw_o[...] = o_rw
        mrg_o[...] = merged
        dz_o[...] = dzb
        da_o[...] = dab
        db_o[...] = dbb
        dgate_o[...] = dgate.astype(BF16)
        dosb_o[...] = _mm_nt(dab, w_sb_v).astype(BF16)
        _split_heads(do_rw * g_rw_v, drw_o)
        dg_o[...] = do_rw * rw_out_v
        first = pl.program_id(0) == 0
        _acc_out(dgp_o, dgain, first)
        _acc_out(dbg_o, jnp.sum(dgate, axis=0, keepdims=True), first)

    acc = lambda n: pl.BlockSpec((1, n), lambda i: (0, 0))
    sd = jax.ShapeDtypeStruct
    return pl.pallas_call(
        body, name="merge_bwd", grid=(T // tm,),
        in_specs=[_row_spec(tm, D_MODEL), _row_spec(tm, WIDTH), _head_spec(tm, seq), _row_spec(tm, WIDTH),
                  _row_spec(tm, GATE_COLS), _const_spec((WIDTH, D_MODEL)), _const_spec((WIDTH, D_MODEL)),
                  _const_spec((D_MODEL, D_MODEL)), _const_spec((1, D_MODEL))],
        out_specs=[_row_spec(tm, WIDTH), _row_spec(tm, D_MODEL), _row_spec(tm, D_MODEL), _row_spec(tm, D_MODEL),
                   _row_spec(tm, D_MODEL), _row_spec(tm, GATE_COLS), _row_spec(tm, WIDTH), _head_spec(tm, seq),
                   _row_spec(tm, WIDTH), acc(D_MODEL), acc(GATE_COLS)],
        out_shape=[sd((T, WIDTH), BF16), sd((T, D_MODEL), BF16), sd((T, D_MODEL), BF16), sd((T, D_MODEL), BF16),
                   sd((T, D_MODEL), BF16), sd((T, GATE_COLS), BF16), sd((T, WIDTH), BF16),
                   sd((T // seq, HEADS, seq, HEAD_DIM), F32), sd((T, WIDTH), F32), sd((1, D_MODEL), F32),
                   sd((1, GATE_COLS), F32)],
        compiler_params=_params(("arbitrary",)),
    )(dx1, o_sb, rw_out, g_rw, gates, w_sb, w_rw, w_o, g_post)


def _ffn(x1, target, g_pre, g_post, w_gate, w_up, w_down, tm):
    T = x1.shape[0]

    def body(x1_ref, tgt_ref, gpre_ref, gpost_ref, wg_ref, wu_ref, wd_ref,
             loss_o, dx1_o, h_o, dgate_o, dup_o, act_o, df_o, dgpre_o, dgpost_o):
        x1v = x1_ref[...]
        gpre, gpost = gpre_ref[...], gpost_ref[...]
        wg, wu, wd = wg_ref[...], wu_ref[...], wd_ref[...]
        hn, xn1, rstd1 = _rms_fwd(x1v, gpre)
        h = hn.astype(BF16)
        gate = _mm(h, wg)
        up = _mm(h, wu)
        sg = jax.nn.sigmoid(gate)
        act = (gate * sg * up).astype(BF16)
        f = _mm(act, wd)
        fo, fn, rstd2 = _rms_fwd(f, gpost)
        diff = x1v + fo - tgt_ref[...]
        dy = diff * (1.0 / D_MODEL)
        df, dgpost = _rms_bwd(dy, fn, rstd2, gpost)
        dfb = df.astype(BF16)
        dact = _mm_nt(dfb, wd)
        dup = (dact * gate * sg).astype(BF16)
        dgate = (dact * up * (sg * (1.0 + gate * (1.0 - sg)))).astype(BF16)
        dh = _mm_nt(dgate, wg) + _mm_nt(dup, wu)
        dxn, dgpre = _rms_bwd(dh, xn1, rstd1, gpre)
        dx1_o[...] = dy + dxn
        h_o[...] = h
        dgate_o[...] = dgate
        dup_o[...] = dup
        act_o[...] = act
        df_o[...] = dfb
        first = pl.program_id(0) == 0
        part = jnp.sum(jnp.sum(diff * diff, axis=1, keepdims=True), axis=0, keepdims=True) * (0.5 / D_MODEL)
        _acc_out(loss_o, jnp.broadcast_to(part, (8, 128)), first)
        _acc_out(dgpre_o, dgpre, first)
        _acc_out(dgpost_o, dgpost, first)

    acc = lambda r, n: pl.BlockSpec((r, n), lambda i: (0, 0))
    sd = jax.ShapeDtypeStruct
    return pl.pallas_call(
        body, name="ffn", grid=(T // tm,),
        in_specs=[_row_spec(tm, D_MODEL), _row_spec(tm, D_MODEL), _const_spec((1, D_MODEL)), _const_spec((1, D_MODEL)),
                  _const_spec((D_MODEL, D_FF)), _const_spec((D_MODEL, D_FF)), _const_spec((D_FF, D_MODEL))],
        out_specs=[acc(8, 128), _row_spec(tm, D_MODEL), _row_spec(tm, D_MODEL), _row_spec(tm, D_FF), _row_spec(tm, D_FF),
                   _row_spec(tm, D_FF), _row_spec(tm, D_MODEL), acc(1, D_MODEL), acc(1, D_MODEL)],
        out_shape=[sd((8, 128), F32), sd((T, D_MODEL), F32), sd((T, D_MODEL), BF16), sd((T, D_FF), BF16),
                   sd((T, D_FF), BF16), sd((T, D_FF), BF16), sd((T, D_MODEL), BF16), sd((1, D_MODEL), F32),
                   sd((1, D_MODEL), F32)],
        compiler_params=_params(("arbitrary",)),
    )(x1, target, g_pre, g_post, w_gate, w_up, w_down)


def _local_step(x, target, sm, wt):
    bl, seq, _ = x.shape
    T = bl * seq
    tm = min(ROW_TILE, T)
    x2 = x.reshape(T, D_MODEL)
    tgt2 = target.reshape(T, D_MODEL)
    w_qkv, w_prw, w_gate = wt["w_in"][:, :SB_COLS], wt["w_in"][:, SB_COLS:SB_COLS + RW_COLS], wt["w_in"][:, SB_COLS + RW_COLS:]
    h, qkv, prw, gates = _in_proj_fwd(x2, sm["norm_mix_pre"], w_qkv, w_prw, w_gate, sm["b_gate"], tm)
    o_sb, lsum, sb_stop = _sb_fwd(qkv, bl, seq)
    prep_params = [sm["mu_rw"], sm["w0"], wt["w_up"].astype(F32), sm["a0"], wt["a_up"].astype(F32),
                   wt["g_up"].astype(F32), sm["k_k"], sm["k_a"]]
    prep = _rw_prep_fwd(prw, prep_params, seq, tm)
    by_head = lambda t: t.reshape(bl, HEADS, seq, HEAD_DIM)
    seqs = [t.reshape(bl * HEADS, seq, HEAD_DIM) for t in prep[:6]]
    g_rw = prep[6]
    lnw, lnb, rk = (sm[n].reshape(HEADS, 1, HEAD_DIM) for n in ("lnx_w", "lnx_b", "r_k"))
    rw_out_h, states = _wkv_fwd(seqs, lnw, lnb, rk)
    rw_out = by_head(rw_out_h)
    x1 = _merge_fwd(x2, o_sb, rw_out, g_rw, gates, wt["w_sb_out"], wt["w_rw_out"], wt["w_o"], sm["norm_mix_post"],
                    seq, tm)
    (loss_part, dx1, h2, dffg, dffu, act, dff, d_nfpre, d_nfpost) = _ffn(
        x1, tgt2, sm["norm_ffn_pre"], sm["norm_ffn_post"], wt["w_ffn_gate"], wt["w_ffn_up"], wt["w_ffn_down"], tm)
    (o_rw, merged, dz, da, db, dgate, do_sb, d_rw_out, d_g_rw, d_npost, d_bgate) = _merge_bwd(
        dx1, o_sb, rw_out, g_rw, gates, wt["w_sb_out"], wt["w_rw_out"], wt["w_o"], sm["norm_mix_post"], seq, tm)
    dqkv = jnp.concatenate(_sb_bwd(qkv, do_sb, lsum, sb_stop, bl, seq), axis=1)
    wkv_g = _wkv_bwd(seqs, states, d_rw_out.reshape(bl * HEADS, seq, HEAD_DIM), lnw, lnb, rk)
    cts = [by_head(t) for t in wkv_g[:6]] + [d_g_rw]
    prep_g = _rw_prep_bwd(prw, prep_params, cts, seq, tm)
    dprw = prep_g[0]
    d_mu, d_w0, d_wup, d_a0, d_aup, d_gup, d_kk, d_ka = prep_g[1:]
    grad_x, d_npre = _in_proj_bwd(x2, sm["norm_mix_pre"], dx1, dqkv, dprw, dgate, w_qkv, w_prw, w_gate, tm)
    gw = {
        "w_in": jnp.concatenate([_grad_w(h, dqkv, "gw_in_qkv"), _grad_w(h, dprw, "gw_in_rw"), _grad_w(h, dgate, "gw_in_gate")], axis=1),
        "w_up": d_wup, "a_up": d_aup, "g_up": d_gup,
        "w_sb_out": _grad_w(o_sb, da, "gw_sb_out"), "w_rw_out": _grad_w(o_rw, db, "gw_rw_out"),
        "w_o": _grad_w(merged, dz, "gw_o"),
        "w_ffn_gate": _grad_w(h2, dffg, "gw_ffn_gate"), "w_ffn_up": _grad_w(h2, dffu, "gw_ffn_up"),
        "w_ffn_down": _grad_w(act, dff, "gw_ffn_down"),
    }
    gs = {
        "norm_mix_pre": d_npre, "b_gate": d_bgate, "mu_rw": d_mu, "w0": d_w0, "a0": d_a0, "k_k": d_kk, "k_a": d_ka,
        "r_k": wkv_g[8].reshape(1, WIDTH), "lnx_w": wkv_g[6].reshape(1, WIDTH), "lnx_b": wkv_g[7].reshape(1, WIDTH),
        "norm_mix_post": d_npost, "norm_ffn_pre": d_nfpre, "norm_ffn_post": d_nfpost,
    }
    return loss_part, grad_x.reshape(x.shape), gw, gs


_SHARDED = [("w_in", 1, (D_MODEL, (SB_COLS + RW_COLS + GATE_COLS) // N_DEV)), ("w_up", 1, (W_LORA, WIDTH // N_DEV)),
            ("a_up", 1, (A_LORA, WIDTH // N_DEV)), ("g_up", 1, (G_LORA, WIDTH // N_DEV)),
            ("w_sb_out", 1, (WIDTH, D_MODEL // N_DEV)), ("w_rw_out", 1, (WIDTH, D_MODEL // N_DEV)),
            ("w_o", 0, (D_MODEL // N_DEV, D_MODEL)), ("w_ffn_gate", 1, (D_MODEL, D_FF // N_DEV)),
            ("w_ffn_up", 1, (D_MODEL, D_FF // N_DEV)), ("w_ffn_down", 0, (D_FF // N_DEV, D_MODEL))]
_LANES = 128
_PACK_ROWS = [s[0] * s[1] // _LANES for _, _, s in _SHARDED]
_PACK_TOTAL = sum(_PACK_ROWS)
_SMALL = [("norm_mix_pre", D_MODEL), ("b_gate", GATE_COLS), ("mu_rw", RW_COLS), ("w0", WIDTH), ("a0", WIDTH),
          ("k_k", WIDTH), ("k_a", WIDTH), ("r_k", WIDTH), ("lnx_w", WIDTH), ("lnx_b", WIDTH),
          ("norm_mix_post", D_MODEL), ("norm_ffn_pre", D_MODEL), ("norm_ffn_post", D_MODEL)]
_SMALL_ROWS = 96


def _pack_shards(shards, dtype):
    return jnp.concatenate([shards[n].astype(dtype).reshape(-1, _LANES) for n, _, _ in _SHARDED], axis=0)


def _unpack_shards(packed):
    out, r0 = {}, 0
    for (n, _, shp), rows in zip(_SHARDED, _PACK_ROWS):
        out[n] = packed[r0:r0 + rows].reshape(shp)
        r0 += rows
    return out


def _unpack_gathered(g):
    out, r0 = {}, 0
    for (n, axis, shp), rows in zip(_SHARDED, _PACK_ROWS):
        blk = g[:, r0:r0 + rows].reshape((N_DEV,) + shp)
        out[n] = blk.reshape(N_DEV * shp[0], shp[1]) if axis == 0 else blk.transpose(1, 0, 2).reshape(shp[0], N_DEV * shp[1])
        r0 += rows
    return out


def _pack_full_grads(gw):
    parts = []
    for n, axis, shp in _SHARDED:
        g = gw[n]
        blk = g.reshape((N_DEV,) + shp) if axis == 0 else g.reshape(shp[0], N_DEV, shp[1]).transpose(1, 0, 2)
        parts.append(blk.reshape(N_DEV, -1, _LANES))
    return jnp.concatenate(parts, axis=1)


def _pack_small(vals, extra=None):
    flat = [vals[n].reshape(-1) for n, _ in _SMALL]
    used = sum(sz for _, sz in _SMALL)
    tail = jnp.zeros((_SMALL_ROWS * _LANES - used,), F32)
    if extra is not None:
        tail = tail.at[0].set(extra)
    return jnp.concatenate(flat + [tail]).reshape(_SMALL_ROWS, _LANES)


def _unpack_small(packed):
    flat, out, o = packed.reshape(-1), {}, 0
    for n, sz in _SMALL:
        out[n] = flat[o:o + sz]
        o += sz
    return out, flat[o]


_ANY = pl.BlockSpec(memory_space=pl.ANY)


def _all_gather(block):
    rows, lanes = block.shape

    def body(x_ref, out_ref, send_sems, recv_sems, local_sem):
        x, y, c = lax.axis_index("x"), lax.axis_index("y"), lax.axis_index("c")
        me, sibling = (x, y, c), (x, y, 1 - c)
        chips = [(1 - x, y), (x, 1 - y), (1 - x, 1 - y)]

        def slot(px, py, pc):
            return out_ref.at[4 * px + 2 * py + pc]

        def copy(k, blk, to, src=None):
            return pltpu.make_async_remote_copy(
                src_ref=slot(*blk) if src is None else src, dst_ref=slot(*blk),
                send_sem=send_sems.at[k], recv_sem=recv_sems.at[k], device_id=to, device_id_type=MESH)

        mine = pltpu.make_async_copy(x_ref, slot(*me), local_sem)
        mine.start()
        first = [copy(0, me, sibling, src=x_ref)]
        first += [copy(1 + j, me, (*chip, c), src=x_ref) for j, chip in enumerate(chips)]
        for cp in first:
            cp.start()
        passed = [copy(4 + j, (*chip, c), sibling) for j, chip in enumerate(chips)]
        for j, chip in enumerate(chips):
            copy(1 + j, (*chip, c), me).wait_recv()
            passed[j].start()
        copy(0, sibling, me).wait_recv()
        for j, chip in enumerate(chips):
            copy(4 + j, (*chip, 1 - c), me).wait_recv()
        for cp in first + passed:
            cp.wait_send()
        mine.wait()

    return pl.pallas_call(
        body, name="all_gather_weights", in_specs=[_ANY], out_specs=_ANY,
        out_shape=jax.ShapeDtypeStruct((N_DEV, rows, lanes), block.dtype),
        scratch_shapes=[pltpu.SemaphoreType.DMA((7,)), pltpu.SemaphoreType.DMA((7,)), pltpu.SemaphoreType.DMA],
    )(block)


def _exchange_core(pack, small):
    _, _, rows, lanes = pack.shape

    def body(pack_ref, small_ref, got_ref, parts_ref, send_sems, recv_sems, s_send, s_recv, local_sem):
        x, y, c = lax.axis_index("x"), lax.axis_index("y"), lax.axis_index("c")
        sibling = (x, y, 1 - c)
        me = 4 * x + 2 * y + c
        mine = pltpu.make_async_copy(small_ref, parts_ref.at[me], local_sem)
        mine.start()
        big = [pltpu.make_async_remote_copy(
            src_ref=pack_ref.at[1 - c, j], dst_ref=got_ref.at[j], send_sem=send_sems.at[j], recv_sem=recv_sems.at[j],
            device_id=sibling, device_id_type=MESH) for j in range(4)]
        for cp in big:
            cp.start()
        others = [(k, (x ^ (k >> 2), y ^ ((k >> 1) & 1), c ^ (k & 1))) for k in range(1, N_DEV)]
        tiny = [pltpu.make_async_remote_copy(
            src_ref=small_ref, dst_ref=parts_ref.at[me], send_sem=s_send.at[k], recv_sem=s_recv.at[k],
            device_id=to, device_id_type=MESH) for k, to in others]
        for cp in tiny:
            cp.start()
        for cp in big:
            cp.wait_recv()
        for (k, (px, py, pc)), cp in zip(others, tiny):
            pltpu.make_async_remote_copy(
                src_ref=small_ref, dst_ref=parts_ref.at[4 * px + 2 * py + pc], send_sem=s_send.at[k],
                recv_sem=s_recv.at[k], device_id=(px, py, pc), device_id_type=MESH).wait_recv()
        for cp in big + tiny:
            cp.wait_send()
        mine.wait()

    return pl.pallas_call(
        body, name="exchange_core", in_specs=[_ANY, _ANY], out_specs=[_ANY, _ANY],
        out_shape=[jax.ShapeDtypeStruct((4, rows, lanes), F32), jax.ShapeDtypeStruct((N_DEV,) + small.shape, F32)],
        scratch_shapes=[pltpu.SemaphoreType.DMA((4,)), pltpu.SemaphoreType.DMA((4,)), pltpu.SemaphoreType.DMA((N_DEV,)),
                        pltpu.SemaphoreType.DMA((N_DEV,)), pltpu.SemaphoreType.DMA],
    )(pack, small)


def _add_core_parts(pack, got, core):
    _, _, rows, lanes = pack.shape
    tr = 2000

    def body(core_ref, a_ref, b_ref, o_ref):
        o_ref[...] = a_ref[...] + b_ref[...]

    return pl.pallas_call(
        body, name="add_core_parts",
        grid_spec=pltpu.PrefetchScalarGridSpec(
            num_scalar_prefetch=1, grid=(4, rows // tr),
            in_specs=[pl.BlockSpec((None, None, tr, lanes), lambda j, i, core_ref: (core_ref[0], j, i, 0)),
                      pl.BlockSpec((None, tr, lanes), lambda j, i, core_ref: (j, i, 0))],
            out_specs=pl.BlockSpec((None, tr, lanes), lambda j, i, core_ref: (j, i, 0))),
        out_shape=jax.ShapeDtypeStruct((4, rows, lanes), F32),
        compiler_params=_params(("parallel", "parallel")),
    )(core, pack, got)


def _exchange_chips(chip_sums):
    _, rows, lanes = chip_sums.shape

    def body(src_ref, got_ref, send_sems, recv_sems):
        x, y, c = lax.axis_index("x"), lax.axis_index("y"), lax.axis_index("c")
        flips = [(1, 0), (0, 1), (1, 1)]
        copies = []
        for k, (fx, fy) in enumerate(flips):
            px, py = x ^ fx, y ^ fy
            copies.append(pltpu.make_async_remote_copy(
                src_ref=src_ref.at[2 * px + py], dst_ref=got_ref.at[k], send_sem=send_sems.at[k],
                recv_sem=recv_sems.at[k], device_id=(px, py, c), device_id_type=MESH))
        for cp in copies:
            cp.start()
        for cp in copies:
            cp.wait_recv()
        for cp in copies:
            cp.wait_send()

    return pl.pallas_call(
        body, name="exchange_chips", in_specs=[_ANY], out_specs=_ANY,
        out_shape=jax.ShapeDtypeStruct((3, rows, lanes), F32),
        scratch_shapes=[pltpu.SemaphoreType.DMA((3,)), pltpu.SemaphoreType.DMA((3,))],
    )(chip_sums)


def _sum_chip_parts(chip_sums, got, chip):
    _, rows, lanes = chip_sums.shape
    tr = 2000

    def body(chip_ref, own_ref, got_ref, o_ref):
        o_ref[...] = ((own_ref[...] + got_ref[0]) + got_ref[1]) + got_ref[2]

    return pl.pallas_call(
        body, name="sum_chip_parts",
        grid_spec=pltpu.PrefetchScalarGridSpec(
            num_scalar_prefetch=1, grid=(rows // tr,),
            in_specs=[pl.BlockSpec((None, tr, lanes), lambda i, chip_ref: (chip_ref[0], i, 0)),
                      pl.BlockSpec((3, tr, lanes), lambda i, chip_ref: (0, i, 0))],
            out_specs=pl.BlockSpec((tr, lanes), lambda i, chip_ref: (i, 0))),
        out_shape=jax.ShapeDtypeStruct((rows, lanes), F32),
        compiler_params=_params(("parallel",)),
    )(chip, chip_sums, got)


def _adamw_math(w, g, m, v):
    m = ADAM_B1 * m + (1.0 - ADAM_B1) * g
    v = ADAM_B2 * v + (1.0 - ADAM_B2) * (g * g)
    m_hat = m / (1.0 - ADAM_B1 ** ADAM_STEP)
    v_hat = v / (1.0 - ADAM_B2 ** ADAM_STEP)
    return -ADAM_LR * (m_hat / (jnp.sqrt(v_hat) + ADAM_EPS) + ADAM_WD * w), m, v


def _adamw(w, g, m, v, name):
    rows, cols = w.shape
    tr = 256 if rows % 256 == 0 and rows * cols > 2 ** 19 else rows

    def body(w_ref, g_ref, m_ref, v_ref, d_o, m_o, v_o):
        d_o[...], m_o[...], v_o[...] = _adamw_math(w_ref[...], g_ref[...], m_ref[...], v_ref[...])

    spec = pl.BlockSpec((tr, cols), lambda i: (i, 0))
    return pl.pallas_call(
        body, name=name, grid=(rows // tr,), in_specs=[spec] * 4, out_specs=[spec] * 3,
        out_shape=[jax.ShapeDtypeStruct((rows, cols), F32)] * 3, compiler_params=_params(("parallel",)),
    )(w, g, m, v)


def _adamw_small(parts, w, m, v):
    def body(p_ref, w_ref, m_ref, v_ref, g_o, d_o, m_o, v_o):
        g = p_ref[0]
        for d in range(1, N_DEV):
            g = g + p_ref[d]
        g_o[...] = g
        d_o[...], m_o[...], v_o[...] = _adamw_math(w_ref[...], g, m_ref[...], v_ref[...])

    return pl.pallas_call(
        body, name="adamw_small", out_shape=[jax.ShapeDtypeStruct(w.shape, F32)] * 4, compiler_params=_params(),
    )(parts, w, m, v)


_WEIGHT_NAMES = ['norm_mix_pre', 'w_in', 'b_gate', 'mu_rw', 'w0', 'w_up', 'a0', 'a_up', 'g_up', 'k_k', 'k_a', 'r_k',
                 'lnx_w', 'lnx_b', 'w_sb_out', 'w_rw_out', 'w_o', 'norm_mix_post', 'norm_ffn_pre', 'w_ffn_gate',
                 'w_ffn_up', 'w_ffn_down', 'norm_ffn_post']


def _step(x, target, w, m, v):
    sharded = [n for n, _, _ in _SHARDED]
    sm = {n: w[n].reshape(1, -1) for n, _ in _SMALL}
    own = {n: w[n][0] for n in sharded}
    wt = _unpack_gathered(_all_gather(_pack_shards(own, BF16)))
    loss_part, grad_x, gw, gs = _local_step(x, target, sm, wt)

    cx, cy, cc = lax.axis_index("x"), lax.axis_index("y"), lax.axis_index("c")
    core = jnp.reshape(cc, (1,)).astype(jnp.int32)
    chip = jnp.reshape(2 * cx + cy, (1,)).astype(jnp.int32)
    pack = _pack_full_grads(gw).reshape(4, 2, _PACK_TOTAL, _LANES).transpose(1, 0, 2, 3)
    got_core, small_parts = _exchange_core(pack, _pack_small(gs, loss_part[0, 0]))
    chip_sums = _add_core_parts(pack, got_core, core)
    grads_packed = _sum_chip_parts(chip_sums, _exchange_chips(chip_sums), chip)
    g_sh = _unpack_shards(grads_packed)

    g_small, d_small, m_small, v_small = _adamw_small(
        small_parts, _pack_small({n: w[n] for n, _ in _SMALL}), _pack_small({n: m[n] for n, _ in _SMALL}),
        _pack_small({n: v[n] for n, _ in _SMALL}))
    (g_s, loss), (d_s, _), (m_s, _), (v_s, _) = (_unpack_small(t) for t in (g_small, d_small, m_small, v_small))

    grads, deltas, new_m, new_v = {}, {}, {}, {}
    for n in _WEIGHT_NAMES:
        if n in g_sh:
            d_, m_, v_ = _adamw(own[n], g_sh[n], m[n][0], v[n][0], "adamw_" + n)
            grads[n], deltas[n], new_m[n], new_v[n] = (t.reshape(w[n].shape) for t in (g_sh[n], d_, m_, v_))
        else:
            grads[n], deltas[n], new_m[n], new_v[n] = (t[n].reshape(w[n].shape) for t in (g_s, d_s, m_s, v_s))
    return (loss, grad_x, *[grads[n] for n in _WEIGHT_NAMES], *[deltas[n] for n in _WEIGHT_NAMES],
            *[new_m[n] for n in _WEIGHT_NAMES], *[new_v[n] for n in _WEIGHT_NAMES])


def kernel(x, norm_mix_pre, w_in, b_gate, mu_rw, w0, w_up, a0, a_up, g_up, k_k, k_a, r_k, lnx_w, lnx_b, w_sb_out, w_rw_out, w_o, norm_mix_post, norm_ffn_pre, w_ffn_gate, w_ffn_up, w_ffn_down, norm_ffn_post, loss_target, m_norm_mix_pre, m_w_in, m_b_gate, m_mu_rw, m_w0, m_w_up, m_a0, m_a_up, m_g_up, m_k_k, m_k_a, m_r_k, m_lnx_w, m_lnx_b, m_w_sb_out, m_w_rw_out, m_w_o, m_norm_mix_post, m_norm_ffn_pre, m_w_ffn_gate, m_w_ffn_up, m_w_ffn_down, m_norm_ffn_post, v_norm_mix_pre, v_w_in, v_b_gate, v_mu_rw, v_w0, v_w_up, v_a0, v_a_up, v_g_up, v_k_k, v_k_a, v_r_k, v_lnx_w, v_lnx_b, v_w_sb_out, v_w_rw_out, v_w_o, v_norm_mix_post, v_norm_ffn_pre, v_w_ffn_gate, v_w_ffn_up, v_w_ffn_down, v_norm_ffn_post):
    args = locals()
    w = {n: args[n] for n in _WEIGHT_NAMES}
    m = {n: args["m_" + n] for n in _WEIGHT_NAMES}
    v = {n: args["v_" + n] for n in _WEIGHT_NAMES}
    return _step(x, loss_target, w, m, v)
```

```python
import functools

import jax
import jax.numpy as jnp
from jax import lax
from jax.experimental import pallas as pl
from jax.experimental.pallas import tpu as pltpu

F32 = jnp.float32
BF16 = jnp.bfloat16

D_MODEL = 1024
HEADS = 8
HEAD_DIM = 64
WIDTH = HEADS * HEAD_DIM
W_LORA, A_LORA, G_LORA = 64, 64, 128
SB_COLS = 3 * WIDTH
RW_COLS = 3 * WIDTH + W_LORA + A_LORA + G_LORA
GATE_COLS = 2 * D_MODEL
D_FF = 2816
RMS_EPS = 1e-6
GN_EPS = HEAD_DIM * 1e-5
N_DEV = 8

ADAM_LR, ADAM_B1, ADAM_B2, ADAM_EPS, ADAM_WD, ADAM_STEP = 0.001, 0.9, 0.999, 1e-08, 0.01, 10

ROW_TILE = 256
SCAN_CHUNK = 64
ATT_ALIGN = 128
ATT_WINDOW = 384
ATT_Q = 128
SB_DEAD = -104.0
SCAN_SEQS_FWD = 4
SCAN_SEQS_BWD = 2
SCAN_PASSES = 1
VMEM_LIMIT = 56 * 2 ** 20

MESH = pl.DeviceIdType.MESH


def _params(sem=None, vmem=VMEM_LIMIT):
    kw = dict(vmem_limit_bytes=vmem)
    if sem is not None:
        kw["dimension_semantics"] = sem
    return pltpu.CompilerParams(**kw)


def _const_spec(shape):
    nd = len(shape)
    return pl.BlockSpec(shape, lambda *_: (0,) * nd, pipeline_mode=pl.Buffered(1))


def _row_spec(tm, n):
    return pl.BlockSpec((tm, n), lambda i: (i, 0))


def _mm(a, b):
    return lax.dot_general(a, b, (((1,), (0,)), ((), ())), preferred_element_type=F32)


def _mm_nt(a, b):
    return lax.dot_general(a, b, (((1,), (1,)), ((), ())), preferred_element_type=F32)


def _mm_tn(a, b):
    return lax.dot_general(a, b, (((0,), (0,)), ((), ())), preferred_element_type=F32)


def _softplus(z):
    return jnp.maximum(z, 0.0) + jnp.log1p(jnp.exp(-jnp.abs(z)))


def _rms_fwd(x, gain):
    rstd = lax.rsqrt(jnp.mean(x * x, axis=-1, keepdims=True) + RMS_EPS)
    xn = x * rstd
    return xn * gain, xn, rstd


def _rms_bwd(dy, xn, rstd, gain):
    u = dy * gain
    dx = rstd * (u - xn * jnp.mean(u * xn, axis=-1, keepdims=True))
    return dx, jnp.sum(dy * xn, axis=0, keepdims=True)


def _acc_out(ref, val, first):
    @pl.when(first)
    def _():
        ref[...] = val

    @pl.when(jnp.logical_not(first))
    def _():
        ref[...] += val


def _in_proj_fwd(x2, g_pre, w_qkv, w_rw, w_gate, b_gate, tm):
    T = x2.shape[0]

    def body(x_ref, g_ref, wq_ref, wr_ref, wg_ref, b_ref, h_ref, qkv_ref, prw_ref, gate_ref):
        h = _rms_fwd(x_ref[...], g_ref[...])[0].astype(BF16)
        h_ref[...] = h
        qkv_ref[...] = _mm(h, wq_ref[...]).astype(BF16)
        prw_ref[...] = _mm(h, wr_ref[...])
        gate_ref[...] = jax.nn.sigmoid(_mm(h, wg_ref[...]) + b_ref[...])

    return pl.pallas_call(
        body, name="in_proj_fwd", grid=(T // tm,),
        in_specs=[_row_spec(tm, D_MODEL), _const_spec((1, D_MODEL)), _const_spec((D_MODEL, SB_COLS)),
                  _const_spec((D_MODEL, RW_COLS)), _const_spec((D_MODEL, GATE_COLS)), _const_spec((1, GATE_COLS))],
        out_specs=[_row_spec(tm, D_MODEL), _row_spec(tm, SB_COLS), _row_spec(tm, RW_COLS), _row_spec(tm, GATE_COLS)],
        out_shape=[jax.ShapeDtypeStruct((T, D_MODEL), BF16), jax.ShapeDtypeStruct((T, SB_COLS), BF16),
                   jax.ShapeDtypeStruct((T, RW_COLS), F32), jax.ShapeDtypeStruct((T, GATE_COLS), F32)],
        compiler_params=_params(("parallel",)),
    )(x2, g_pre, w_qkv, w_rw, w_gate, b_gate)


def _in_proj_bwd(x2, g_pre, dx1, dqkv, dprw, dgate, w_qkv, w_rw, w_gate, tm):
    T = x2.shape[0]

    def body(x_ref, g_ref, dx1_ref, dq_ref, dr_ref, dg_ref, wq_ref, wr_ref, wg_ref, gx_ref, dgain_ref):
        dh = _mm_nt(dq_ref[...], wq_ref[...]) + _mm_nt(dr_ref[...], wr_ref[...]) + _mm_nt(dg_ref[...], wg_ref[...])
        gain = g_ref[...]
        _, xn, rstd = _rms_fwd(x_ref[...], gain)
        dx, dgain = _rms_bwd(dh, xn, rstd, gain)
        gx_ref[...] = dx1_ref[...] + dx
        _acc_out(dgain_ref, dgain, pl.program_id(0) == 0)

    return pl.pallas_call(
        body, name="in_proj_bwd", grid=(T // tm,),
        in_specs=[_row_spec(tm, D_MODEL), _const_spec((1, D_MODEL)), _row_spec(tm, D_MODEL), _row_spec(tm, SB_COLS),
                  _row_spec(tm, RW_COLS), _row_spec(tm, GATE_COLS), _const_spec((D_MODEL, SB_COLS)),
                  _const_spec((D_MODEL, RW_COLS)), _const_spec((D_MODEL, GATE_COLS))],
        out_specs=[_row_spec(tm, D_MODEL), pl.BlockSpec((1, D_MODEL), lambda i: (0, 0))],
        out_shape=[jax.ShapeDtypeStruct((T, D_MODEL), F32), jax.ShapeDtypeStruct((1, D_MODEL), F32)],
        compiler_params=_params(("arbitrary",)),
    )(x2, g_pre, dx1, dqkv, dprw, dgate, w_qkv, w_rw, w_gate)


def _pick_tile(n, cap):
    best = None
    for t in range(128, min(n, cap) + 1, 128):
        if n % t == 0:
            best = t
    return n if best is None else best


def _grad_w(a, b, name):
    T, K = a.shape
    N = b.shape[1]
    tk, tn, tt = _pick_tile(K, 1408), _pick_tile(N, 2048), min(T, 512)

    def body(a_ref, b_ref, o_ref):
        _acc_out(o_ref, _mm_tn(a_ref[...], b_ref[...]), pl.program_id(2) == 0)

    return pl.pallas_call(
        body, name=name, grid=(K // tk, N // tn, T // tt),
        in_specs=[pl.BlockSpec((tt, tk), lambda i, j, t: (t, i)), pl.BlockSpec((tt, tn), lambda i, j, t: (t, j))],
        out_specs=pl.BlockSpec((tk, tn), lambda i, j, t: (i, j)),
        out_shape=jax.ShapeDtypeStruct((K, N), F32),
        compiler_params=_params(("parallel", "parallel", "arbitrary")),
    )(a, b)


def _tri(n, kind):
    r = lax.broadcasted_iota(jnp.int32, (n, n), 0)
    c = lax.broadcasted_iota(jnp.int32, (n, n), 1)
    return {"gt": r > c, "le": r <= c, "lt": r < c, "ge": r >= c}[kind]


def _split_mm(x, u):
    hi = x.astype(BF16)
    lo = (x - hi.astype(F32)).astype(BF16)
    return _mm(hi, u) + _mm(lo, u)


def _sb_scores(qs, k, row0, col0, first, last, qb, kb):
    z = _mm_nt(qs, k)
    sp = _softplus(z)
    row = lax.broadcasted_iota(jnp.int32, (qb, kb), 0) + row0
    col = lax.broadcasted_iota(jnp.int32, (qb, kb), 1) + col0
    valid = jnp.logical_and(col < row, jnp.logical_and(col >= first, col < last))
    return z, sp, valid


_PAIR = 2 * HEAD_DIM
_PAIRS = WIDTH // _PAIR


def _first_head_lanes():
    return lax.broadcasted_iota(jnp.int32, (1, _PAIR), 1) < HEAD_DIM


def _per_head(t, first_head):
    zero = jnp.zeros_like(t)
    return jnp.where(first_head, t, zero), jnp.where(first_head, zero, t)


def _sb_fwd(qkv, bl, seq):
    qb, win = min(ATT_Q, seq), min(ATT_WINDOW, seq)
    nq = seq // qb

    def body(q_ref, k_ref, v_ref, o_ref, l_ref, stop_ref):
        g = pl.program_id(0) * _PAIRS + pl.program_id(1)
        first_head = _first_head_lanes()
        u_after = _tri(win, "gt").astype(BF16)

        def qblock(i, _):
            rows = pl.ds(pl.multiple_of(i * qb, qb), qb)
            qh = _per_head(q_ref[rows, :] * (HEAD_DIM ** -0.5), first_head)

            def live(carry):
                return jnp.logical_and(carry[0] > 0, carry[3] > 0)

            def window(carry):
                hi, accs, cs, _ = carry
                lo = pl.multiple_of(jnp.maximum(hi - win, 0), ATT_ALIGN)
                cols = pl.ds(lo, win)
                kv, vv = k_ref[cols, :], v_ref[cols, :]
                new_accs, new_cs = [], []
                for h in range(2):
                    z, sp, valid = _sb_scores(qh[h], kv, i * qb, lo, lo, hi, qb, win)
                    lfm = jnp.where(valid, -sp, 0.0)
                    after = _split_mm(lfm, u_after) + cs[h]
                    w = jnp.where(valid, jnp.exp(z - sp + after), 0.0)
                    new_accs.append(accs[h] + _mm(w.astype(BF16), vv))
                    new_cs.append(cs[h] + jnp.sum(lfm, axis=1, keepdims=True))
                alive = jnp.maximum(jnp.max(new_cs[0]), jnp.max(new_cs[1])) > SB_DEAD
                return lo, tuple(new_accs), tuple(new_cs), alive.astype(jnp.int32)

            zero_acc, zero_c = jnp.zeros((qb, _PAIR), F32), jnp.zeros((qb, 1), F32)
            lo, accs, cs, _ = lax.while_loop(
                live, window, ((i + 1) * qb, (zero_acc, zero_acc), (zero_c, zero_c), jnp.int32(1)))
            o_ref[rows, :] = jnp.where(first_head, accs[0], accs[1]).astype(BF16)
            l_ref[0, rows, :] = cs[0]
            l_ref[1, rows, :] = cs[1]
            stop_ref[g, i] = lo
            return 0

        lax.fori_loop(0, nq, qblock, 0)

    col = lambda off: pl.BlockSpec((seq, _PAIR), lambda b, p: (b, off + p))
    return pl.pallas_call(
        body, name="sb_fwd", grid=(bl, _PAIRS), in_specs=[col(0), col(_PAIRS), col(2 * _PAIRS)],
        out_specs=[col(0), pl.BlockSpec((None, 2, seq, 1), lambda b, p: (b, p, 0, 0)),
                   pl.BlockSpec(memory_space=pltpu.SMEM)],
        out_shape=[jax.ShapeDtypeStruct((bl * seq, WIDTH), BF16), jax.ShapeDtypeStruct((bl, HEADS, seq, 1), F32),
                   jax.ShapeDtypeStruct((bl * _PAIRS, nq), jnp.int32)],
        compiler_params=_params(("arbitrary", "arbitrary")),
    )(qkv, qkv, qkv)


def _sb_bwd(qkv, do, lsum, stop, bl, seq):
    qb, win = min(ATT_Q, seq), min(ATT_WINDOW, seq)
    nq = seq // qb

    def body(stop_ref, q_ref, k_ref, v_ref, do_ref, l_ref, dq_ref, dk_ref, dv_ref, dk_acc, dv_acc):
        g = pl.program_id(0) * _PAIRS + pl.program_id(1)
        first_head = _first_head_lanes()
        u_incl = _tri(win, "le").astype(BF16)
        u_excl = _tri(win, "lt").astype(BF16)
        dk_acc[...] = jnp.zeros_like(dk_acc)
        dv_acc[...] = jnp.zeros_like(dv_acc)

        def qblock(i, _):
            rows = pl.ds(pl.multiple_of(i * qb, qb), qb)
            qv = q_ref[rows, :]
            qh = _per_head(qv * (HEAD_DIM ** -0.5), first_head)
            dob = do_ref[rows, :]
            doh = _per_head(dob, first_head)
            ltot = (l_ref[0, rows, :], l_ref[1, rows, :])

            first = (jnp.clip(stop_ref[g, i], 0, i * qb) // ATT_ALIGN) * ATT_ALIGN

            def window(n, carry):
                dqs, ps, es = carry
                start = first + n * win
                lo = pl.multiple_of(jnp.minimum(start, seq - win), ATT_ALIGN)
                cols = pl.ds(lo, win)
                kv, vv = k_ref[cols, :], v_ref[cols, :]
                new_dqs, new_ps, new_es, dks, dvs = [], [], [], [], []
                for h in range(2):
                    z, sp, valid = _sb_scores(qh[h], kv, i * qb, lo, start, seq, qb, win)
                    lfm = jnp.where(valid, -sp, 0.0)
                    after = ltot[h] - ps[h] - _split_mm(lfm, u_incl)
                    w = jnp.where(valid, jnp.exp(z - sp + after), 0.0)
                    e = _mm_nt(doh[h], vv) * w
                    dlf = es[h] + _split_mm(e, u_excl)
                    sig = jnp.exp(z - sp)
                    dz = jnp.where(valid, e * (1.0 - sig) - dlf * sig, 0.0) * (HEAD_DIM ** -0.5)
                    dzb = dz.astype(BF16)
                    dvs.append(_mm_tn(w.astype(BF16), dob))
                    dks.append(_mm_tn(dzb, qv))
                    new_dqs.append(dqs[h] + _mm(dzb, kv))
                    new_ps.append(ps[h] + jnp.sum(lfm, axis=1, keepdims=True))
                    new_es.append(es[h] + jnp.sum(e, axis=1, keepdims=True))
                dv_acc[cols, :] += jnp.where(first_head, dvs[0], dvs[1])
                dk_acc[cols, :] += jnp.where(first_head, dks[0], dks[1])
                return tuple(new_dqs), tuple(new_ps), tuple(new_es)

            zero_q, zero_c = jnp.zeros((qb, _PAIR), F32), jnp.zeros((qb, 1), F32)
            dqs, _, _ = lax.fori_loop(0, ((i + 1) * qb - first + win - 1) // win, window,
                                      ((zero_q, zero_q), (zero_c, zero_c), (zero_c, zero_c)))
            dq_ref[rows, :] = jnp.where(first_head, dqs[0], dqs[1]).astype(BF16)
            return 0

        lax.fori_loop(0, nq, qblock, 0)
        dk_ref[...] = dk_acc[...].astype(BF16)
        dv_ref[...] = dv_acc[...].astype(BF16)

    col = lambda off: pl.BlockSpec((seq, _PAIR), lambda b, p, stop_ref: (b, off + p))
    return pl.pallas_call(
        body, name="sb_bwd",
        grid_spec=pltpu.PrefetchScalarGridSpec(
            num_scalar_prefetch=1, grid=(bl, _PAIRS),
            in_specs=[col(0), col(_PAIRS), col(2 * _PAIRS), col(0),
                      pl.BlockSpec((None, 2, seq, 1), lambda b, p, stop_ref: (b, p, 0, 0))],
            out_specs=[col(0), col(0), col(0)],
            scratch_shapes=[pltpu.VMEM((seq, _PAIR), F32), pltpu.VMEM((seq, _PAIR), F32)]),
        out_shape=[jax.ShapeDtypeStruct((bl * seq, WIDTH), BF16)] * 3,
        compiler_params=_params(("parallel", "parallel")),
    )(stop, qkv, qkv, qkv, do, lsum)


@jax.custom_vjp
def _lora_mm(x, w):
    return _mm(x.astype(BF16), w.astype(BF16))


_lora_mm.defvjp(
    lambda x, w: (_mm(x.astype(BF16), w.astype(BF16)), (x, w)),
    lambda res, ct: (_mm_nt(ct.astype(BF16), res[1].astype(BF16)), _mm_tn(res[0].astype(BF16), ct.astype(BF16))))


def _rw_prep_math(p, ps, mu, w0, w_up, a0, a_up, g_up, k_k, k_a):
    pm = p + (ps - p) * mu
    r, k, v = pm[:, :WIDTH], pm[:, WIDTH:2 * WIDTH], pm[:, 2 * WIDTH:3 * WIDTH]
    o = 3 * WIDTH
    xw, xa, xg = pm[:, o:o + W_LORA], pm[:, o + W_LORA:o + W_LORA + A_LORA], pm[:, o + W_LORA + A_LORA:]
    w_raw = w0 + _lora_mm(jnp.tanh(xw), w_up)
    lw = -jnp.exp(-_softplus(-w_raw) - 0.5)
    a = jax.nn.sigmoid(a0 + _lora_mm(xa, a_up))
    g = _lora_mm(jax.nn.sigmoid(xg), g_up)
    kk = k * k_k
    k2 = k * (1.0 + (a - 1.0) * k_a)
    return r, lw, k2, v, kk, a, g


def _shift_down(p, first_row):
    row = lax.broadcasted_iota(jnp.int32, p.shape, 0)
    return jnp.where(row == 0, first_row, pltpu.roll(p, 1, 0))


def _shift_up(p, last_row):
    row = lax.broadcasted_iota(jnp.int32, p.shape, 0)
    return jnp.where(row == p.shape[0] - 1, last_row, pltpu.roll(p, p.shape[0] - 1, 0))


_PREP_PARAM_SHAPES = [(1, RW_COLS), (1, WIDTH), (W_LORA, WIDTH), (1, WIDTH), (A_LORA, WIDTH), (G_LORA, WIDTH),
                      (1, WIDTH), (1, WIDTH)]


def _prev_rows_spec(tm):
    return pl.BlockSpec((8, RW_COLS), lambda i: (jnp.maximum(i * (tm // 8) - 1, 0), 0))


def _head_spec(tm, seq, tile_of=lambda i: i):
    per_seq = seq // tm
    return pl.BlockSpec((None, HEADS, tm, HEAD_DIM),
                        lambda i: (tile_of(i) // per_seq, 0, tile_of(i) % per_seq, 0))


def _split_heads(val, ref):
    for h in range(HEADS):
        ref[h] = val[:, h * HEAD_DIM:(h + 1) * HEAD_DIM]


def _join_heads(ref):
    return jnp.concatenate([ref[h] for h in range(HEADS)], axis=1)


def _rw_prep_fwd(prw, params, seq, tm):
    T = prw.shape[0]

    def body(p_ref, prev_ref, *rest):
        prm = [r_[...] for r_ in rest[:8]]
        outs = rest[8:]
        i = pl.program_id(0)
        first = jnp.where((i * tm) % seq == 0, 0.0, prev_ref[7:8, :])
        p = p_ref[...]
        vals = _rw_prep_math(p, _shift_down(p, first), *prm)
        for o_ref, val in zip(outs[:6], vals[:6]):
            _split_heads(val, o_ref)
        outs[6][...] = vals[6]

    by_head = jax.ShapeDtypeStruct((T // seq, HEADS, seq, HEAD_DIM), F32)
    return pl.pallas_call(
        body, name="rw_prep_fwd", grid=(T // tm,),
        in_specs=[_row_spec(tm, RW_COLS), _prev_rows_spec(tm)] + [_const_spec(s) for s in _PREP_PARAM_SHAPES],
        out_specs=[_head_spec(tm, seq)] * 6 + [_row_spec(tm, WIDTH)],
        out_shape=[by_head] * 6 + [jax.ShapeDtypeStruct((T, WIDTH), F32)],
        compiler_params=_params(("parallel",)),
    )(prw, prw, *params)


def _rw_prep_bwd(prw, params, cts, seq, tm):
    T = prw.shape[0]
    n = T // tm

    def body(p_ref, prev_ref, *rest):
        prm = [r_[...] for r_ in rest[:8]]
        ct = tuple(_join_heads(r_) for r_ in rest[8:14]) + (rest[14][...],)
        dp_ref = rest[15]
        dprm_refs = rest[16:24]
        carry = rest[24]
        step = pl.program_id(0)
        i = n - 1 - step
        first = jnp.where((i * tm) % seq == 0, 0.0, prev_ref[7:8, :])
        p = p_ref[...]
        _, vjp = jax.vjp(_rw_prep_math, p, _shift_down(p, first), *prm)
        grads = vjp(ct)
        dp, dps = grads[0], grads[1]
        nxt = jnp.where(jnp.logical_or(step == 0, ((i + 1) * tm) % seq == 0), 0.0, carry[0:1, :])
        dp_ref[...] = (dp + _shift_up(dps, nxt)).astype(BF16)
        carry[...] = dps[0:8, :]
        for ref, gval in zip(dprm_refs, grads[2:]):
            _acc_out(ref, gval, step == 0)

    rev = lambda w: pl.BlockSpec((tm, w), lambda s: (n - 1 - s, 0))
    prev = pl.BlockSpec((8, RW_COLS), lambda s: (jnp.maximum((n - 1 - s) * (tm // 8) - 1, 0), 0))
    return pl.pallas_call(
        body, name="rw_prep_bwd", grid=(n,),
        in_specs=([rev(RW_COLS), prev] + [_const_spec(s) for s in _PREP_PARAM_SHAPES]
                  + [_head_spec(tm, seq, lambda s: n - 1 - s)] * 6 + [rev(WIDTH)]),
        out_specs=[rev(RW_COLS)] + [pl.BlockSpec(s, lambda s_: (0, 0)) for s in _PREP_PARAM_SHAPES],
        out_shape=[jax.ShapeDtypeStruct((T, RW_COLS), BF16)] + [jax.ShapeDtypeStruct(s, F32) for s in _PREP_PARAM_SHAPES],
        scratch_shapes=[pltpu.VMEM((8, RW_COLS), F32)],
        compiler_params=_params(("arbitrary",)),
    )(prw, prw, *params, *cts)


def _make_bmm(passes):
    def raw(dn, a, b):
        d = lambda x, y: lax.dot_general(x, y, dn, preferred_element_type=F32)
        ah = a.astype(BF16)
        bh = b.astype(BF16)
        if passes == 1:
            return d(ah, bh)
        al = (a - ah.astype(F32)).astype(BF16)
        bl = (b - bh.astype(F32)).astype(BF16)
        return d(ah, bh) + (d(ah, bl) + d(al, bh))

    dn_nn = (((2,), (1,)), ((0,), (0,)))
    dn_nt = (((2,), (2,)), ((0,), (0,)))
    dn_tn = (((1,), (1,)), ((0,), (0,)))

    @jax.custom_vjp
    def nn(a, b):
        return raw(dn_nn, a, b)

    @jax.custom_vjp
    def nt(a, b):
        return raw(dn_nt, a, b)

    @jax.custom_vjp
    def tn(a, b):
        return raw(dn_tn, a, b)

    nn.defvjp(lambda a, b: (raw(dn_nn, a, b), (a, b)), lambda res, ct: (nt(ct, res[1]), tn(res[0], ct)))
    nt.defvjp(lambda a, b: (raw(dn_nt, a, b), (a, b)), lambda res, ct: (nn(ct, res[1]), tn(ct, res[0])))
    tn.defvjp(lambda a, b: (raw(dn_tn, a, b), (a, b)), lambda res, ct: (nt(res[1], ct), nn(res[0], ct)))

    def unit_lower_inverse(m):
        n = m.shape[-1]
        row = lax.broadcasted_iota(jnp.int32, (n, n), 0)
        col = lax.broadcasted_iota(jnp.int32, (n, n), 1)
        m16 = ((row // 16) == (col // 16)).astype(F32)
        m32 = ((row // 32) == (col // 32)).astype(F32)
        a1 = m * m16
        a2 = nn(a1, a1)
        a4 = nn(a2, a2)
        a8 = nn(a4, a4)
        inv = (row == col).astype(F32) - a1
        inv = inv + nn(inv, a2)
        inv = inv + nn(inv, a4)
        inv = inv + nn(inv, a8)
        inv = inv - nn(nn(inv, m * (m32 - m16)), inv)
        return inv - nn(nn(inv, m * (1.0 - m32)), inv)

    @jax.custom_vjp
    def inverse(m):
        return unit_lower_inverse(m)

    def inverse_fwd(m):
        inv = unit_lower_inverse(m)
        return inv, inv

    inverse.defvjp(inverse_fwd, lambda inv, ct: (-nt(tn(inv, ct), inv),))
    return nn, nt, tn, inverse


def _wkv_chunk(s0, r, lw, k, v, kk, a, lnw, lnb, rk):
    nn, nt, tn, inverse = _make_bmm(SCAN_PASSES)
    G, L, N = r.shape
    rep = lambda t: jnp.broadcast_to(t[None], (G // HEADS, HEADS, 1, N)).reshape(G, 1, N)
    kap = kk * lax.rsqrt(jnp.maximum(jnp.sum(kk * kk, axis=-1, keepdims=True), 1e-24))
    b = a * kap
    row = lax.broadcasted_iota(jnp.int32, (L, L), 0)
    col = lax.broadcasted_iota(jnp.int32, (L, L), 1)
    low_incl = (col <= row).astype(F32)
    low_strict = (col < row).astype(F32)
    c = _make_bmm(3)[0](jnp.broadcast_to(low_incl[None], (G, L, L)), lw)
    c_all = jnp.sum(lw, axis=1, keepdims=True)
    g_inv = jnp.exp(-c)
    kap_t = kap * jnp.exp(c - lw)
    b_t = b * g_inv
    k_t = k * g_inv
    r_t = r * jnp.exp(c)
    g_all = jnp.exp(c_all)
    m_b = nt(kap_t, b_t) * low_strict
    m_k = nt(kap_t, k_t) * low_strict
    n_b = nt(r_t, b_t) * low_incl
    n_k = nt(r_t, k_t) * low_incl
    rhs = -(nt(kap_t, s0) + nn(m_k, v))
    sa = nn(inverse(m_b), rhs)
    y = nt(r_t, s0) + nn(n_b, sa) + nn(n_k, v)
    s1 = s0 * g_all + tn(sa, b_t * g_all) + tn(v, k_t * g_all)
    mean = jnp.mean(y, axis=-1, keepdims=True)
    yc = y - mean
    var = jnp.mean(yc * yc, axis=-1, keepdims=True)
    out = yc * lax.rsqrt(var + GN_EPS) * rep(lnw) + rep(lnb)
    out = out + jnp.sum(r * k * rep(rk), axis=-1, keepdims=True) * v
    return out, s1


def _scan_heads_per_step(total_heads, seqs_wanted):
    n_seq = total_heads // HEADS
    return HEADS * max(d for d in range(1, seqs_wanted + 1) if n_seq % d == 0)


def _wkv_fwd(seqs, lnw, lnb, rk):
    G, S, N = seqs[0].shape
    L = SCAN_CHUNK
    nc = S // L

    def body(*refs):
        ins = [r_[...] for r_ in refs[:6]]
        prm = [r_[...] for r_ in refs[6:9]]
        out_ref, st_ref, state = refs[9], refs[10], refs[11]

        @pl.when(pl.program_id(1) == 0)
        def _():
            state[...] = jnp.zeros_like(state)

        s0 = state[...]
        st_ref[...] = s0
        out, s1 = _wkv_chunk(s0, *ins, *prm)
        out_ref[...] = out
        state[...] = s1

    gb = _scan_heads_per_step(G, SCAN_SEQS_FWD)
    blk = pl.BlockSpec((gb, L, N), lambda b, i: (b, i, 0))
    pspec = _const_spec((HEADS, 1, N))
    return pl.pallas_call(
        body, name="wkv_fwd", grid=(G // gb, nc), in_specs=[blk] * 6 + [pspec] * 3,
        out_specs=[blk, pl.BlockSpec((None, gb, N, N), lambda b, i: (i, b, 0, 0))],
        out_shape=[jax.ShapeDtypeStruct((G, S, N), F32), jax.ShapeDtypeStruct((nc, G, N, N), F32)],
        scratch_shapes=[pltpu.VMEM((gb, N, N), F32)],
        compiler_params=_params(("parallel", "arbitrary")),
    )(*seqs, lnw, lnb, rk)


def _wkv_bwd(seqs, states, dout, lnw, lnb, rk):
    G, S, N = seqs[0].shape
    L = SCAN_CHUNK
    nc = S // L

    def body(*refs):
        ins = [r_[...] for r_ in refs[:6]]
        s0 = refs[6][...]
        ct_out = refs[7][...]
        prm = [r_[...] for r_ in refs[8:11]]
        d_refs = refs[11:17]
        dprm_refs = refs[17:20]
        dstate = refs[20]
        step = pl.program_id(1)

        @pl.when(step == 0)
        def _():
            dstate[...] = jnp.zeros_like(dstate)

        _, vjp = jax.vjp(_wkv_chunk, s0, *ins, *prm)
        grads = vjp((ct_out, dstate[...]))
        dstate[...] = grads[0]
        for ref, gval in zip(d_refs, grads[1:7]):
            ref[...] = gval
        for ref, gval in zip(dprm_refs, grads[7:]):
            _acc_out(ref, gval, jnp.logical_and(step == 0, pl.program_id(0) == 0))

    gb = _scan_heads_per_step(G, SCAN_SEQS_BWD)
    blk = pl.BlockSpec((gb, L, N), lambda b, s: (b, nc - 1 - s, 0))
    pspec = _const_spec((HEADS, 1, N))
    pout = pl.BlockSpec((HEADS, 1, N), lambda b, s: (0, 0, 0))
    return pl.pallas_call(
        body, name="wkv_bwd", grid=(G // gb, nc),
        in_specs=[blk] * 6 + [pl.BlockSpec((None, gb, N, N), lambda b, s: (nc - 1 - s, b, 0, 0)), blk] + [pspec] * 3,
        out_specs=[blk] * 6 + [pout] * 3,
        out_shape=[jax.ShapeDtypeStruct((G, S, N), F32)] * 6 + [jax.ShapeDtypeStruct((HEADS, 1, N), F32)] * 3,
        scratch_shapes=[pltpu.VMEM((gb, N, N), F32)],
        compiler_params=_params(("arbitrary", "arbitrary")),
    )(*seqs, states, dout, lnw, lnb, rk)


def _merge_math(o_sb, rw_out, g_rw, gates, w_sb, w_rw, w_o):
    o_rw = (rw_out * g_rw).astype(BF16)
    a = _mm(o_sb, w_sb)
    b = _mm(o_rw, w_rw)
    g1, g2 = gates[:, :D_MODEL], gates[:, D_MODEL:]
    merged = (g1 * a + g2 * b).astype(BF16)
    return o_rw, a, b, g1, g2, merged, _mm(merged, w_o)


def _merge_fwd(x2, o_sb, rw_out, g_rw, gates, w_sb, w_rw, w_o, g_post, seq, tm):
    T = x2.shape[0]

    def body(x_ref, osb_ref, rw_ref, g_ref, gate_ref, wsb_ref, wrw_ref, wo_ref, gp_ref, x1_ref):
        z = _merge_math(osb_ref[...], _join_heads(rw_ref), g_ref[...], gate_ref[...], wsb_ref[...], wrw_ref[...], wo_ref[...])[-1]
        x1_ref[...] = x_ref[...] + _rms_fwd(z, gp_ref[...])[0]

    return pl.pallas_call(
        body, name="merge_fwd", grid=(T // tm,),
        in_specs=[_row_spec(tm, D_MODEL), _row_spec(tm, WIDTH), _head_spec(tm, seq), _row_spec(tm, WIDTH),
                  _row_spec(tm, GATE_COLS), _const_spec((WIDTH, D_MODEL)), _const_spec((WIDTH, D_MODEL)),
                  _const_spec((D_MODEL, D_MODEL)), _const_spec((1, D_MODEL))],
        out_specs=_row_spec(tm, D_MODEL),
        out_shape=jax.ShapeDtypeStruct((T, D_MODEL), F32),
        compiler_params=_params(("parallel",)),
    )(x2, o_sb, rw_out, g_rw, gates, w_sb, w_rw, w_o, g_post)


def _merge_bwd(dx1, o_sb, rw_out, g_rw, gates, w_sb, w_rw, w_o, g_post, seq, tm):
    T = dx1.shape[0]

    def body(dx1_ref, osb_ref, rw_ref, g_ref, gate_ref, wsb_ref, wrw_ref, wo_ref, gp_ref,
             orw_o, mrg_o, dz_o, da_o, db_o, dgate_o, dosb_o, drw_o, dg_o, dgp_o, dbg_o):
        rw_out_v, g_rw_v = _join_heads(rw_ref), g_ref[...]
        w_sb_v, w_rw_v, w_o_v = wsb_ref[...], wrw_ref[...], wo_ref[...]
        o_rw, a, b, g1, g2, merged, z = _merge_math(osb_ref[...], rw_out_v, g_rw_v, gate_ref[...], w_sb_v, w_rw_v, w_o_v)
        gain = gp_ref[...]
        _, zn, rstd = _rms_fwd(z, gain)
        dz, dgain = _rms_bwd(dx1_ref[...], zn, rstd, gain)
        dzb = dz.astype(BF16)
        dm = _mm_nt(dzb, w_o_v)
        dab = (dm * g1).astype(BF16)
        dbb = (dm * g2).astype(BF16)
        dgate = jnp.concatenate([dm * a * g1 * (1.0 - g1), dm * b * g2 * (1.0 - g2)], axis=1)
        do_rw = _mm_nt(dbb, w_rw_v)
        orw_o[...] = o_rw
        mrg_o[...] = merged
        dz_o[...] = dzb
        da_o[...] = dab
        db_o[...] = dbb
        dgate_o[...] = dgate.astype(BF16)
        dosb_o[...] = _mm_nt(dab, w_sb_v).astype(BF16)
        _split_heads(do_rw * g_rw_v, drw_o)
        dg_o[...] = do_rw * rw_out_v
        first = pl.program_id(0) == 0
        _acc_out(dgp_o, dgain, first)
        _acc_out(dbg_o, jnp.sum(dgate, axis=0, keepdims=True), first)

    acc = lambda n: pl.BlockSpec((1, n), lambda i: (0, 0))
    sd = jax.ShapeDtypeStruct
    return pl.pallas_call(
        body, name="merge_bwd", grid=(T // tm,),
        in_specs=[_row_spec(tm, D_MODEL), _row_spec(tm, WIDTH), _head_spec(tm, seq), _row_spec(tm, WIDTH),
                  _row_spec(tm, GATE_COLS), _const_spec((WIDTH, D_MODEL)), _const_spec((WIDTH, D_MODEL)),
                  _const_spec((D_MODEL, D_MODEL)), _const_spec((1, D_MODEL))],
        out_specs=[_row_spec(tm, WIDTH), _row_spec(tm, D_MODEL), _row_spec(tm, D_MODEL), _row_spec(tm, D_MODEL),
                   _row_spec(tm, D_MODEL), _row_spec(tm, GATE_COLS), _row_spec(tm, WIDTH), _head_spec(tm, seq),
                   _row_spec(tm, WIDTH), acc(D_MODEL), acc(GATE_COLS)],
        out_shape=[sd((T, WIDTH), BF16), sd((T, D_MODEL), BF16), sd((T, D_MODEL), BF16), sd((T, D_MODEL), BF16),
                   sd((T, D_MODEL), BF16), sd((T, GATE_COLS), BF16), sd((T, WIDTH), BF16),
                   sd((T // seq, HEADS, seq, HEAD_DIM), F32), sd((T, WIDTH), F32), sd((1, D_MODEL), F32),
                   sd((1, GATE_COLS), F32)],
        compiler_params=_params(("arbitrary",)),
    )(dx1, o_sb, rw_out, g_rw, gates, w_sb, w_rw, w_o, g_post)


def _ffn(x1, target, g_pre, g_post, w_gate, w_up, w_down, tm):
    T = x1.shape[0]

    def body(x1_ref, tgt_ref, gpre_ref, gpost_ref, wg_ref, wu_ref, wd_ref,
             loss_o, dx1_o, h_o, dgate_o, dup_o, act_o, df_o, dgpre_o, dgpost_o):
        x1v = x1_ref[...]
        gpre, gpost = gpre_ref[...], gpost_ref[...]
        wg, wu, wd = wg_ref[...], wu_ref[...], wd_ref[...]
        hn, xn1, rstd1 = _rms_fwd(x1v, gpre)
        h = hn.astype(BF16)
        gate = _mm(h, wg)
        up = _mm(h, wu)
        sg = jax.nn.sigmoid(gate)
        act = (gate * sg * up).astype(BF16)
        f = _mm(act, wd)
        fo, fn, rstd2 = _rms_fwd(f, gpost)
        diff = x1v + fo - tgt_ref[...]
        dy = diff * (1.0 / D_MODEL)
        df, dgpost = _rms_bwd(dy, fn, rstd2, gpost)
        dfb = df.astype(BF16)
        dact = _mm_nt(dfb, wd)
        dup = (dact * gate * sg).astype(BF16)
        dgate = (dact * up * (sg * (1.0 + gate * (1.0 - sg)))).astype(BF16)
        dh = _mm_nt(dgate, wg) + _mm_nt(dup, wu)
        dxn, dgpre = _rms_bwd(dh, xn1, rstd1, gpre)
        dx1_o[...] = dy + dxn
        h_o[...] = h
        dgate_o[...] = dgate
        dup_o[...] = dup
        act_o[...] = act
        df_o[...] = dfb
        first = pl.program_id(0) == 0
        part = jnp.sum(jnp.sum(diff * diff, axis=1, keepdims=True), axis=0, keepdims=True) * (0.5 / D_MODEL)
        _acc_out(loss_o, jnp.broadcast_to(part, (8, 128)), first)
        _acc_out(dgpre_o, dgpre, first)
        _acc_out(dgpost_o, dgpost, first)

    acc = lambda r, n: pl.BlockSpec((r, n), lambda i: (0, 0))
    sd = jax.ShapeDtypeStruct
    return pl.pallas_call(
        body, name="ffn", grid=(T // tm,),
        in_specs=[_row_spec(tm, D_MODEL), _row_spec(tm, D_MODEL), _const_spec((1, D_MODEL)), _const_spec((1, D_MODEL)),
                  _const_spec((D_MODEL, D_FF)), _const_spec((D_MODEL, D_FF)), _const_spec((D_FF, D_MODEL))],
        out_specs=[acc(8, 128), _row_spec(tm, D_MODEL), _row_spec(tm, D_MODEL), _row_spec(tm, D_FF), _row_spec(tm, D_FF),
                   _row_spec(tm, D_FF), _row_spec(tm, D_MODEL), acc(1, D_MODEL), acc(1, D_MODEL)],
        out_shape=[sd((8, 128), F32), sd((T, D_MODEL), F32), sd((T, D_MODEL), BF16), sd((T, D_FF), BF16),
                   sd((T, D_FF), BF16), sd((T, D_FF), BF16), sd((T, D_MODEL), BF16), sd((1, D_MODEL), F32),
                   sd((1, D_MODEL), F32)],
        compiler_params=_params(("arbitrary",)),
    )(x1, target, g_pre, g_post, w_gate, w_up, w_down)


def _local_step(x, target, sm, wt):
    bl, seq, _ = x.shape
    T = bl * seq
    tm = min(ROW_TILE, T)
    x2 = x.reshape(T, D_MODEL)
    tgt2 = target.reshape(T, D_MODEL)
    w_qkv, w_prw, w_gate = wt["w_in"][:, :SB_COLS], wt["w_in"][:, SB_COLS:SB_COLS + RW_COLS], wt["w_in"][:, SB_COLS + RW_COLS:]
    h, qkv, prw, gates = _in_proj_fwd(x2, sm["norm_mix_pre"], w_qkv, w_prw, w_gate, sm["b_gate"], tm)
    o_sb, lsum, sb_stop = _sb_fwd(qkv, bl, seq)
    prep_params = [sm["mu_rw"], sm["w0"], wt["w_up"].astype(F32), sm["a0"], wt["a_up"].astype(F32),
                   wt["g_up"].astype(F32), sm["k_k"], sm["k_a"]]
    prep = _rw_prep_fwd(prw, prep_params, seq, tm)
    by_head = lambda t: t.reshape(bl, HEADS, seq, HEAD_DIM)
    seqs = [t.reshape(bl * HEADS, seq, HEAD_DIM) for t in prep[:6]]
    g_rw = prep[6]
    lnw, lnb, rk = (sm[n].reshape(HEADS, 1, HEAD_DIM) for n in ("lnx_w", "lnx_b", "r_k"))
    rw_out_h, states = _wkv_fwd(seqs, lnw, lnb, rk)
    rw_out = by_head(rw_out_h)
    x1 = _merge_fwd(x2, o_sb, rw_out, g_rw, gates, wt["w_sb_out"], wt["w_rw_out"], wt["w_o"], sm["norm_mix_post"],
                    seq, tm)
    (loss_part, dx1, h2, dffg, dffu, act, dff, d_nfpre, d_nfpost) = _ffn(
        x1, tgt2, sm["norm_ffn_pre"], sm["norm_ffn_post"], wt["w_ffn_gate"], wt["w_ffn_up"], wt["w_ffn_down"], tm)
    (o_rw, merged, dz, da, db, dgate, do_sb, d_rw_out, d_g_rw, d_npost, d_bgate) = _merge_bwd(
        dx1, o_sb, rw_out, g_rw, gates, wt["w_sb_out"], wt["w_rw_out"], wt["w_o"], sm["norm_mix_post"], seq, tm)
    dqkv = jnp.concatenate(_sb_bwd(qkv, do_sb, lsum, sb_stop, bl, seq), axis=1)
    wkv_g = _wkv_bwd(seqs, states, d_rw_out.reshape(bl * HEADS, seq, HEAD_DIM), lnw, lnb, rk)
    cts = [by_head(t) for t in wkv_g[:6]] + [d_g_rw]
    prep_g = _rw_prep_bwd(prw, prep_params, cts, seq, tm)
    dprw = prep_g[0]
    d_mu, d_w0, d_wup, d_a0, d_aup, d_gup, d_kk, d_ka = prep_g[1:]
    grad_x, d_npre = _in_proj_bwd(x2, sm["norm_mix_pre"], dx1, dqkv, dprw, dgate, w_qkv, w_prw, w_gate, tm)
    gw = {
        "w_in": jnp.concatenate([_grad_w(h, dqkv, "gw_in_qkv"), _grad_w(h, dprw, "gw_in_rw"), _grad_w(h, dgate, "gw_in_gate")], axis=1),
        "w_up": d_wup, "a_up": d_aup, "g_up": d_gup,
        "w_sb_out": _grad_w(o_sb, da, "gw_sb_out"), "w_rw_out": _grad_w(o_rw, db, "gw_rw_out"),
        "w_o": _grad_w(merged, dz, "gw_o"),
        "w_ffn_gate": _grad_w(h2, dffg, "gw_ffn_gate"), "w_ffn_up": _grad_w(h2, dffu, "gw_ffn_up"),
        "w_ffn_down": _grad_w(act, dff, "gw_ffn_down"),
    }
    gs = {
        "norm_mix_pre": d_npre, "b_gate": d_bgate, "mu_rw": d_mu, "w0": d_w0, "a0": d_a0, "k_k": d_kk, "k_a": d_ka,
        "r_k": wkv_g[8].reshape(1, WIDTH), "lnx_w": wkv_g[6].reshape(1, WIDTH), "lnx_b": wkv_g[7].reshape(1, WIDTH),
        "norm_mix_post": d_npost, "norm_ffn_pre": d_nfpre, "norm_ffn_post": d_nfpost,
    }
    return loss_part, grad_x.reshape(x.shape), gw, gs


_SHARDED = [("w_in", 1, (D_MODEL, (SB_COLS + RW_COLS + GATE_COLS) // N_DEV)), ("w_up", 1, (W_LORA, WIDTH // N_DEV)),
            ("a_up", 1, (A_LORA, WIDTH // N_DEV)), ("g_up", 1, (G_LORA, WIDTH // N_DEV)),
            ("w_sb_out", 1, (WIDTH, D_MODEL // N_DEV)), ("w_rw_out", 1, (WIDTH, D_MODEL // N_DEV)),
            ("w_o", 0, (D_MODEL // N_DEV, D_MODEL)), ("w_ffn_gate", 1, (D_MODEL, D_FF // N_DEV)),
            ("w_ffn_up", 1, (D_MODEL, D_FF // N_DEV)), ("w_ffn_down", 0, (D_FF // N_DEV, D_MODEL))]
_LANES = 128
_PACK_ROWS = [s[0] * s[1] // _LANES for _, _, s in _SHARDED]
_PACK_TOTAL = sum(_PACK_ROWS)
_SMALL = [("norm_mix_pre", D_MODEL), ("b_gate", GATE_COLS), ("mu_rw", RW_COLS), ("w0", WIDTH), ("a0", WIDTH),
          ("k_k", WIDTH), ("k_a", WIDTH), ("r_k", WIDTH), ("lnx_w", WIDTH), ("lnx_b", WIDTH),
          ("norm_mix_post", D_MODEL), ("norm_ffn_pre", D_MODEL), ("norm_ffn_post", D_MODEL)]
_SMALL_ROWS = 96


def _pack_shards(shards, dtype):
    return jnp.concatenate([shards[n].astype(dtype).reshape(-1, _LANES) for n, _, _ in _SHARDED], axis=0)


def _unpack_shards(packed):
    out, r0 = {}, 0
    for (n, _, shp), rows in zip(_SHARDED, _PACK_ROWS):
        out[n] = packed[r0:r0 + rows].reshape(shp)
        r0 += rows
    return out


def _unpack_gathered(g):
    out, r0 = {}, 0
    for (n, axis, shp), rows in zip(_SHARDED, _PACK_ROWS):
        blk = g[:, r0:r0 + rows].reshape((N_DEV,) + shp)
        out[n] = blk.reshape(N_DEV * shp[0], shp[1]) if axis == 0 else blk.transpose(1, 0, 2).reshape(shp[0], N_DEV * shp[1])
        r0 += rows
    return out


def _pack_full_grads(gw):
    parts = []
    for n, axis, shp in _SHARDED:
        g = gw[n]
        blk = g.reshape((N_DEV,) + shp) if axis == 0 else g.reshape(shp[0], N_DEV, shp[1]).transpose(1, 0, 2)
        parts.append(blk.reshape(N_DEV, -1, _LANES))
    return jnp.concatenate(parts, axis=1)


def _pack_small(vals, extra=None):
    flat = [vals[n].reshape(-1) for n, _ in _SMALL]
    used = sum(sz for _, sz in _SMALL)
    tail = jnp.zeros((_SMALL_ROWS * _LANES - used,), F32)
    if extra is not None:
        tail = tail.at[0].set(extra)
    return jnp.concatenate(flat + [tail]).reshape(_SMALL_ROWS, _LANES)


def _unpack_small(packed):
    flat, out, o = packed.reshape(-1), {}, 0
    for n, sz in _SMALL:
        out[n] = flat[o:o + sz]
        o += sz
    return out, flat[o]


_ANY = pl.BlockSpec(memory_space=pl.ANY)


def _all_gather(block):
    rows, lanes = block.shape

    def body(x_ref, out_ref, send_sems, recv_sems, local_sem):
        x, y, c = lax.axis_index("x"), lax.axis_index("y"), lax.axis_index("c")
        me, sibling = (x, y, c), (x, y, 1 - c)
        chips = [(1 - x, y), (x, 1 - y), (1 - x, 1 - y)]

        def slot(px, py, pc):
            return out_ref.at[4 * px + 2 * py + pc]

        def copy(k, blk, to, src=None):
            return pltpu.make_async_remote_copy(
                src_ref=slot(*blk) if src is None else src, dst_ref=slot(*blk),
                send_sem=send_sems.at[k], recv_sem=recv_sems.at[k], device_id=to, device_id_type=MESH)

        mine = pltpu.make_async_copy(x_ref, slot(*me), local_sem)
        mine.start()
        first = [copy(0, me, sibling, src=x_ref)]
        first += [copy(1 + j, me, (*chip, c), src=x_ref) for j, chip in enumerate(chips)]
        for cp in first:
            cp.start()
        passed = [copy(4 + j, (*chip, c), sibling) for j, chip in enumerate(chips)]
        for j, chip in enumerate(chips):
            copy(1 + j, (*chip, c), me).wait_recv()
            passed[j].start()
        copy(0, sibling, me).wait_recv()
        for j, chip in enumerate(chips):
            copy(4 + j, (*chip, 1 - c), me).wait_recv()
        for cp in first + passed:
            cp.wait_send()
        mine.wait()

    return pl.pallas_call(
        body, name="all_gather_weights", in_specs=[_ANY], out_specs=_ANY,
        out_shape=jax.ShapeDtypeStruct((N_DEV, rows, lanes), block.dtype),
        scratch_shapes=[pltpu.SemaphoreType.DMA((7,)), pltpu.SemaphoreType.DMA((7,)), pltpu.SemaphoreType.DMA],
    )(block)


def _exchange_core(pack, small):
    _, _, rows, lanes = pack.shape

    def body(pack_ref, small_ref, got_ref, parts_ref, send_sems, recv_sems, s_send, s_recv, local_sem):
        x, y, c = lax.axis_index("x"), lax.axis_index("y"), lax.axis_index("c")
        sibling = (x, y, 1 - c)
        me = 4 * x + 2 * y + c
        mine = pltpu.make_async_copy(small_ref, parts_ref.at[me], local_sem)
        mine.start()
        big = [pltpu.make_async_remote_copy(
            src_ref=pack_ref.at[1 - c, j], dst_ref=got_ref.at[j], send_sem=send_sems.at[j], recv_sem=recv_sems.at[j],
            device_id=sibling, device_id_type=MESH) for j in range(4)]
        for cp in big:
            cp.start()
        others = [(k, (x ^ (k >> 2), y ^ ((k >> 1) & 1), c ^ (k & 1))) for k in range(1, N_DEV)]
        tiny = [pltpu.make_async_remote_copy(
            src_ref=small_ref, dst_ref=parts_ref.at[me], send_sem=s_send.at[k], recv_sem=s_recv.at[k],
            device_id=to, device_id_type=MESH) for k, to in others]
        for cp in tiny:
            cp.start()
        for cp in big:
            cp.wait_recv()
        for (k, (px, py, pc)), cp in zip(others, tiny):
            pltpu.make_async_remote_copy(
                src_ref=small_ref, dst_ref=parts_ref.at[4 * px + 2 * py + pc], send_sem=s_send.at[k],
                recv_sem=s_recv.at[k], device_id=(px, py, pc), device_id_type=MESH).wait_recv()
        for cp in big + tiny:
            cp.wait_send()
        mine.wait()

    return pl.pallas_call(
        body, name="exchange_core", in_specs=[_ANY, _ANY], out_specs=[_ANY, _ANY],
        out_shape=[jax.ShapeDtypeStruct((4, rows, lanes), F32), jax.ShapeDtypeStruct((N_DEV,) + small.shape, F32)],
        scratch_shapes=[pltpu.SemaphoreType.DMA((4,)), pltpu.SemaphoreType.DMA((4,)), pltpu.SemaphoreType.DMA((N_DEV,)),
                        pltpu.SemaphoreType.DMA((N_DEV,)), pltpu.SemaphoreType.DMA],
    )(pack, small)


def _add_core_parts(pack, got, core):
    _, _, rows, lanes = pack.shape
    tr = 2000

    def body(core_ref, a_ref, b_ref, o_ref):
        o_ref[...] = (a_ref[...] + b_ref[...]).astype(BF16)

    return pl.pallas_call(
        body, name="add_core_parts",
        grid_spec=pltpu.PrefetchScalarGridSpec(
            num_scalar_prefetch=1, grid=(4, rows // tr),
            in_specs=[pl.BlockSpec((None, None, tr, lanes), lambda j, i, core_ref: (core_ref[0], j, i, 0)),
                      pl.BlockSpec((None, tr, lanes), lambda j, i, core_ref: (j, i, 0))],
            out_specs=pl.BlockSpec((None, tr, lanes), lambda j, i, core_ref: (j, i, 0))),
        out_shape=jax.ShapeDtypeStruct((4, rows, lanes), BF16),
        compiler_params=_params(("parallel", "parallel")),
    )(core, pack, got)


def _exchange_chips(chip_sums):
    _, rows, lanes = chip_sums.shape

    def body(src_ref, got_ref, send_sems, recv_sems):
        x, y, c = lax.axis_index("x"), lax.axis_index("y"), lax.axis_index("c")
        flips = [(1, 0), (0, 1), (1, 1)]
        copies = []
        for k, (fx, fy) in enumerate(flips):
            px, py = x ^ fx, y ^ fy
            copies.append(pltpu.make_async_remote_copy(
                src_ref=src_ref.at[2 * px + py], dst_ref=got_ref.at[k], send_sem=send_sems.at[k],
                recv_sem=recv_sems.at[k], device_id=(px, py, c), device_id_type=MESH))
        for cp in copies:
            cp.start()
        for cp in copies:
            cp.wait_recv()
        for cp in copies:
            cp.wait_send()

    return pl.pallas_call(
        body, name="exchange_chips", in_specs=[_ANY], out_specs=_ANY,
        out_shape=jax.ShapeDtypeStruct((3, rows, lanes), chip_sums.dtype),
        scratch_shapes=[pltpu.SemaphoreType.DMA((3,)), pltpu.SemaphoreType.DMA((3,))],
    )(chip_sums)


def _sum_chip_parts(chip_sums, got, chip):
    _, rows, lanes = chip_sums.shape
    tr = 2000

    def body(chip_ref, own_ref, got_ref, o_ref):
        f32 = lambda t: t.astype(F32)
        o_ref[...] = ((f32(own_ref[...]) + f32(got_ref[0])) + f32(got_ref[1])) + f32(got_ref[2])

    return pl.pallas_call(
        body, name="sum_chip_parts",
        grid_spec=pltpu.PrefetchScalarGridSpec(
            num_scalar_prefetch=1, grid=(rows // tr,),
            in_specs=[pl.BlockSpec((None, tr, lanes), lambda i, chip_ref: (chip_ref[0], i, 0)),
                      pl.BlockSpec((3, tr, lanes), lambda i, chip_ref: (0, i, 0))],
            out_specs=pl.BlockSpec((tr, lanes), lambda i, chip_ref: (i, 0))),
        out_shape=jax.ShapeDtypeStruct((rows, lanes), F32),
        compiler_params=_params(("parallel",)),
    )(chip, chip_sums, got)


def _adamw_math(w, g, m, v):
    m = ADAM_B1 * m + (1.0 - ADAM_B1) * g
    v = ADAM_B2 * v + (1.0 - ADAM_B2) * (g * g)
    m_hat = m / (1.0 - ADAM_B1 ** ADAM_STEP)
    v_hat = v / (1.0 - ADAM_B2 ** ADAM_STEP)
    return -ADAM_LR * (m_hat / (jnp.sqrt(v_hat) + ADAM_EPS) + ADAM_WD * w), m, v


def _adamw(w, g, m, v, name):
    rows, cols = w.shape
    tr = 256 if rows % 256 == 0 and rows * cols > 2 ** 19 else rows

    def body(w_ref, g_ref, m_ref, v_ref, d_o, m_o, v_o):
        d_o[...], m_o[...], v_o[...] = _adamw_math(w_ref[...], g_ref[...], m_ref[...], v_ref[...])

    spec = pl.BlockSpec((tr, cols), lambda i: (i, 0))
    return pl.pallas_call(
        body, name=name, grid=(rows // tr,), in_specs=[spec] * 4, out_specs=[spec] * 3,
        out_shape=[jax.ShapeDtypeStruct((rows, cols), F32)] * 3, compiler_params=_params(("parallel",)),
    )(w, g, m, v)


def _adamw_small(parts, w, m, v):
    def body(p_ref, w_ref, m_ref, v_ref, g_o, d_o, m_o, v_o):
        g = p_ref[0]
        for d in range(1, N_DEV):
            g = g + p_ref[d]
        g_o[...] = g
        d_o[...], m_o[...], v_o[...] = _adamw_math(w_ref[...], g, m_ref[...], v_ref[...])

    return pl.pallas_call(
        body, name="adamw_small", out_shape=[jax.ShapeDtypeStruct(w.shape, F32)] * 4, compiler_params=_params(),
    )(parts, w, m, v)


_WEIGHT_NAMES = ['norm_mix_pre', 'w_in', 'b_gate', 'mu_rw', 'w0', 'w_up', 'a0', 'a_up', 'g_up', 'k_k', 'k_a', 'r_k',
                 'lnx_w', 'lnx_b', 'w_sb_out', 'w_rw_out', 'w_o', 'norm_mix_post', 'norm_ffn_pre', 'w_ffn_gate',
                 'w_ffn_up', 'w_ffn_down', 'norm_ffn_post']


def _step(x, target, w, m, v):
    sharded = [n for n, _, _ in _SHARDED]
    sm = {n: w[n].reshape(1, -1) for n, _ in _SMALL}
    own = {n: w[n][0] for n in sharded}
    wt = _unpack_gathered(_all_gather(_pack_shards(own, BF16)))
    loss_part, grad_x, gw, gs = _local_step(x, target, sm, wt)

    cx, cy, cc = lax.axis_index("x"), lax.axis_index("y"), lax.axis_index("c")
    core = jnp.reshape(cc, (1,)).astype(jnp.int32)
    chip = jnp.reshape(2 * cx + cy, (1,)).astype(jnp.int32)
    pack = _pack_full_grads(gw).reshape(4, 2, _PACK_TOTAL, _LANES).transpose(1, 0, 2, 3)
    got_core, small_parts = _exchange_core(pack, _pack_small(gs, loss_part[0, 0]))
    chip_sums = _add_core_parts(pack, got_core, core)
    grads_packed = _sum_chip_parts(chip_sums, _exchange_chips(chip_sums), chip)
    g_sh = _unpack_shards(grads_packed)

    g_small, d_small, m_small, v_small = _adamw_small(
        small_parts, _pack_small({n: w[n] for n, _ in _SMALL}), _pack_small({n: m[n] for n, _ in _SMALL}),
        _pack_small({n: v[n] for n, _ in _SMALL}))
    (g_s, loss), (d_s, _), (m_s, _), (v_s, _) = (_unpack_small(t) for t in (g_small, d_small, m_small, v_small))

    grads, deltas, new_m, new_v = {}, {}, {}, {}
    for n in _WEIGHT_NAMES:
        if n in g_sh:
            d_, m_, v_ = _adamw(own[n], g_sh[n], m[n][0], v[n][0], "adamw_" + n)
            grads[n], deltas[n], new_m[n], new_v[n] = (t.reshape(w[n].shape) for t in (g_sh[n], d_, m_, v_))
        else:
            grads[n], deltas[n], new_m[n], new_v[n] = (t[n].reshape(w[n].shape) for t in (g_s, d_s, m_s, v_s))
    return (loss, grad_x, *[grads[n] for n in _WEIGHT_NAMES], *[deltas[n] for n in _WEIGHT_NAMES],
            *[new_m[n] for n in _WEIGHT_NAMES], *[new_v[n] for n in _WEIGHT_NAMES])


def kernel(x, norm_mix_pre, w_in, b_gate, mu_rw, w0, w_up, a0, a_up, g_up, k_k, k_a, r_k, lnx_w, lnx_b, w_sb_out, w_rw_out, w_o, norm_mix_post, norm_ffn_pre, w_ffn_gate, w_ffn_up, w_ffn_down, norm_ffn_post, loss_target, m_norm_mix_pre, m_w_in, m_b_gate, m_mu_rw, m_w0, m_w_up, m_a0, m_a_up, m_g_up, m_k_k, m_k_a, m_r_k, m_lnx_w, m_lnx_b, m_w_sb_out, m_w_rw_out, m_w_o, m_norm_mix_post, m_norm_ffn_pre, m_w_ffn_gate, m_w_ffn_up, m_w_ffn_down, m_norm_ffn_post, v_norm_mix_pre, v_w_in, v_b_gate, v_mu_rw, v_w0, v_w_up, v_a0, v_a_up, v_g_up, v_k_k, v_k_a, v_r_k, v_lnx_w, v_lnx_b, v_w_sb_out, v_w_rw_out, v_w_o, v_norm_mix_post, v_norm_ffn_pre, v_w_ffn_gate, v_w_ffn_up, v_w_ffn_down, v_norm_ffn_post):
    args = locals()
    w = {n: args[n] for n in _WEIGHT_NAMES}
    m = {n: args["m_" + n] for n in _WEIGHT_NAMES}
    v = {n: args["v_" + n] for n in _WEIGHT_NAMES}
    return _step(x, loss_target, w, m, v)
```

```python
import functools

import jax
import jax.numpy as jnp
from jax import lax
from jax.experimental import pallas as pl
from jax.experimental.pallas import tpu as pltpu

F32 = jnp.float32
BF16 = jnp.bfloat16

D_MODEL = 1024
HEADS = 8
HEAD_DIM = 64
WIDTH = HEADS * HEAD_DIM
W_LORA, A_LORA, G_LORA = 64, 64, 128
SB_COLS = 3 * WIDTH
RW_COLS = 3 * WIDTH + W_LORA + A_LORA + G_LORA
GATE_COLS = 2 * D_MODEL
D_FF = 2816
RMS_EPS = 1e-6
GN_EPS = HEAD_DIM * 1e-5
N_DEV = 8

ADAM_LR, ADAM_B1, ADAM_B2, ADAM_EPS, ADAM_WD, ADAM_STEP = 0.001, 0.9, 0.999, 1e-08, 0.01, 10

ROW_TILE = 256
SCAN_CHUNK = 64
ATT_ALIGN = 128
ATT_WINDOW = 384
ATT_Q = 128
ATT_PAIRS = 2
SB_DEAD = -104.0
SCAN_SEQS_FWD = 4
SCAN_SEQS_BWD = 2
SCAN_PASSES = 1
VMEM_LIMIT = 56 * 2 ** 20

MESH = pl.DeviceIdType.MESH


def _params(sem=None, vmem=VMEM_LIMIT):
    kw = dict(vmem_limit_bytes=vmem)
    if sem is not None:
        kw["dimension_semantics"] = sem
    return pltpu.CompilerParams(**kw)


def _const_spec(shape):
    nd = len(shape)
    return pl.BlockSpec(shape, lambda *_: (0,) * nd, pipeline_mode=pl.Buffered(1))


def _row_spec(tm, n):
    return pl.BlockSpec((tm, n), lambda i: (i, 0))


def _mm(a, b):
    return lax.dot_general(a, b, (((1,), (0,)), ((), ())), preferred_element_type=F32)


def _mm_nt(a, b):
    return lax.dot_general(a, b, (((1,), (1,)), ((), ())), preferred_element_type=F32)


def _mm_tn(a, b):
    return lax.dot_general(a, b, (((0,), (0,)), ((), ())), preferred_element_type=F32)


def _softplus(z):
    return jnp.maximum(z, 0.0) + jnp.log1p(jnp.exp(-jnp.abs(z)))


def _rms_fwd(x, gain):
    rstd = lax.rsqrt(jnp.mean(x * x, axis=-1, keepdims=True) + RMS_EPS)
    xn = x * rstd
    return xn * gain, xn, rstd


def _rms_bwd(dy, xn, rstd, gain):
    u = dy * gain
    dx = rstd * (u - xn * jnp.mean(u * xn, axis=-1, keepdims=True))
    return dx, jnp.sum(dy * xn, axis=0, keepdims=True)


def _acc_out(ref, val, first):
    @pl.when(first)
    def _():
        ref[...] = val

    @pl.when(jnp.logical_not(first))
    def _():
        ref[...] += val


def _in_proj_fwd(x2, g_pre, w_qkv, w_rw, w_gate, b_gate, tm):
    T = x2.shape[0]

    def body(x_ref, g_ref, wq_ref, wr_ref, wg_ref, b_ref, h_ref, qkv_ref, prw_ref, gate_ref):
        h = _rms_fwd(x_ref[...], g_ref[...])[0].astype(BF16)
        h_ref[...] = h
        qkv_ref[...] = _mm(h, wq_ref[...]).astype(BF16)
        prw_ref[...] = _mm(h, wr_ref[...])
        gate_ref[...] = jax.nn.sigmoid(_mm(h, wg_ref[...]) + b_ref[...])

    return pl.pallas_call(
        body, name="in_proj_fwd", grid=(T // tm,),
        in_specs=[_row_spec(tm, D_MODEL), _const_spec((1, D_MODEL)), _const_spec((D_MODEL, SB_COLS)),
                  _const_spec((D_MODEL, RW_COLS)), _const_spec((D_MODEL, GATE_COLS)), _const_spec((1, GATE_COLS))],
        out_specs=[_row_spec(tm, D_MODEL), _row_spec(tm, SB_COLS), _row_spec(tm, RW_COLS), _row_spec(tm, GATE_COLS)],
        out_shape=[jax.ShapeDtypeStruct((T, D_MODEL), BF16), jax.ShapeDtypeStruct((T, SB_COLS), BF16),
                   jax.ShapeDtypeStruct((T, RW_COLS), F32), jax.ShapeDtypeStruct((T, GATE_COLS), F32)],
        compiler_params=_params(("parallel",)),
    )(x2, g_pre, w_qkv, w_rw, w_gate, b_gate)


def _in_proj_bwd(x2, g_pre, dx1, dqkv, dprw, dgate, w_qkv, w_rw, w_gate, tm):
    T = x2.shape[0]

    def body(x_ref, g_ref, dx1_ref, dq_ref, dr_ref, dg_ref, wq_ref, wr_ref, wg_ref, gx_ref, dgain_ref):
        dh = _mm_nt(dq_ref[...], wq_ref[...]) + _mm_nt(dr_ref[...], wr_ref[...]) + _mm_nt(dg_ref[...], wg_ref[...])
        gain = g_ref[...]
        _, xn, rstd = _rms_fwd(x_ref[...], gain)
        dx, dgain = _rms_bwd(dh, xn, rstd, gain)
        gx_ref[...] = dx1_ref[...] + dx
        _acc_out(dgain_ref, dgain, pl.program_id(0) == 0)

    return pl.pallas_call(
        body, name="in_proj_bwd", grid=(T // tm,),
        in_specs=[_row_spec(tm, D_MODEL), _const_spec((1, D_MODEL)), _row_spec(tm, D_MODEL), _row_spec(tm, SB_COLS),
                  _row_spec(tm, RW_COLS), _row_spec(tm, GATE_COLS), _const_spec((D_MODEL, SB_COLS)),
                  _const_spec((D_MODEL, RW_COLS)), _const_spec((D_MODEL, GATE_COLS))],
        out_specs=[_row_spec(tm, D_MODEL), pl.BlockSpec((1, D_MODEL), lambda i: (0, 0))],
        out_shape=[jax.ShapeDtypeStruct((T, D_MODEL), F32), jax.ShapeDtypeStruct((1, D_MODEL), F32)],
        compiler_params=_params(("arbitrary",)),
    )(x2, g_pre, dx1, dqkv, dprw, dgate, w_qkv, w_rw, w_gate)


def _pick_tile(n, cap):
    best = None
    for t in range(128, min(n, cap) + 1, 128):
        if n % t == 0:
            best = t
    return n if best is None else best


def _grad_w(a, b, name):
    T, K = a.shape
    N = b.shape[1]
    tk, tn, tt = _pick_tile(K, 1408), _pick_tile(N, 2048), min(T, 2048)

    def body(a_ref, b_ref, o_ref):
        _acc_out(o_ref, _mm_tn(a_ref[...], b_ref[...]), pl.program_id(2) == 0)

    return pl.pallas_call(
        body, name=name, grid=(K // tk, N // tn, T // tt),
        in_specs=[pl.BlockSpec((tt, tk), lambda i, j, t: (t, i)), pl.BlockSpec((tt, tn), lambda i, j, t: (t, j))],
        out_specs=pl.BlockSpec((tk, tn), lambda i, j, t: (i, j)),
        out_shape=jax.ShapeDtypeStruct((K, N), F32),
        compiler_params=_params(("parallel", "parallel", "arbitrary")),
    )(a, b)


def _tri(n, kind):
    r = lax.broadcasted_iota(jnp.int32, (n, n), 0)
    c = lax.broadcasted_iota(jnp.int32, (n, n), 1)
    return {"gt": r > c, "le": r <= c, "lt": r < c, "ge": r >= c}[kind]


def _split_mm(x, u):
    hi = x.astype(BF16)
    lo = (x - hi.astype(F32)).astype(BF16)
    return _mm(hi, u) + _mm(lo, u)


def _running_sums(x, carry, tri, kb, reverse=False):
    blocks = range(x.shape[1] // kb)
    parts = {}
    for b in (reversed(blocks) if reverse else blocks):
        piece = x[:, b * kb:(b + 1) * kb]
        parts[b] = carry + _split_mm(piece, tri)
        carry = carry + jnp.sum(piece, axis=1, keepdims=True)
    return jnp.concatenate([parts[b] for b in blocks], axis=1), carry


def _sb_valid(row0, col0, first, last, qb, kb):
    ahead = lax.broadcasted_iota(jnp.int32, (qb, kb), 1) - lax.broadcasted_iota(jnp.int32, (qb, kb), 0)
    col = lax.broadcasted_iota(jnp.int32, (1, kb), 1)
    return jnp.logical_and(ahead < row0 - col0, jnp.logical_and(col >= first - col0, col < last - col0))


def _sb_softplus(z):
    return jnp.maximum(z, 0.0) + jnp.log(1.0 + jnp.exp(-jnp.abs(z)))


_PAIR = 2 * HEAD_DIM
_PAIRS = WIDTH // _PAIR


def _first_head_lanes():
    return lax.broadcasted_iota(jnp.int32, (1, _PAIR), 1) < HEAD_DIM


def _per_head(t, first_head):
    zero = jnp.zeros_like(t)
    return jnp.where(first_head, t, zero), jnp.where(first_head, zero, t)


def _sb_fwd(qkv, bl, seq):
    qb, win, kb = min(ATT_Q, seq), min(ATT_WINDOW, seq), ATT_ALIGN
    nq = seq // qb
    nh, width, groups = 2 * ATT_PAIRS, ATT_PAIRS * _PAIR, _PAIRS // ATT_PAIRS
    pair_of = lambda h: slice((h // 2) * _PAIR, (h // 2 + 1) * _PAIR)

    def body(q_ref, k_ref, v_ref, o_ref, l_ref, stop_ref):
        g = pl.program_id(0) * groups + pl.program_id(1)
        first_head = _first_head_lanes()
        u_after = _tri(kb, "gt").astype(BF16)

        def qblock(i, _):
            rows = pl.ds(pl.multiple_of(i * qb, qb), qb)
            qs = q_ref[rows, :] * (HEAD_DIM ** -0.5)
            qh = [_per_head(qs[:, pair_of(h)], first_head)[h % 2] for h in range(nh)]

            def live(carry):
                return jnp.logical_and(carry[0] > 0, carry[3] > 0)

            def window(carry):
                hi, accs, cs, _ = carry
                lo = pl.multiple_of(jnp.maximum(hi - win, 0), kb)
                cols = pl.ds(lo, win)
                kv, vv = k_ref[cols, :], v_ref[cols, :]
                valid = _sb_valid(i * qb, lo, lo, hi, qb, win)
                accs, cs = list(accs), list(cs)
                for h in range(nh):
                    z = _mm_nt(qh[h], kv[:, pair_of(h)])
                    sp = _sb_softplus(z)
                    spm = jnp.where(valid, sp, 0.0)
                    after, cs[h] = _running_sums(spm, cs[h], u_after, kb, reverse=True)
                    w = jnp.where(valid, jnp.exp(z - sp - after), 0.0)
                    accs[h] = accs[h] + _mm(w.astype(BF16), vv[:, pair_of(h)])
                alive = functools.reduce(jnp.minimum, [jnp.min(c) for c in cs]) < -SB_DEAD
                return lo, tuple(accs), tuple(cs), alive.astype(jnp.int32)

            zero_acc, zero_c = jnp.zeros((qb, _PAIR), F32), jnp.zeros((qb, 1), F32)
            lo, accs, cs, _ = lax.while_loop(
                live, window, ((i + 1) * qb, (zero_acc,) * nh, (zero_c,) * nh, jnp.int32(1)))
            for pp in range(ATT_PAIRS):
                o_ref[rows, pp * _PAIR:(pp + 1) * _PAIR] = jnp.where(
                    first_head, accs[2 * pp], accs[2 * pp + 1]).astype(BF16)
            for h in range(nh):
                l_ref[h, rows, :] = cs[h]
            stop_ref[g, i] = lo
            return 0

        lax.fori_loop(0, nq, qblock, 0)

    col = lambda off: pl.BlockSpec((seq, width), lambda b, p: (b, off + p))
    return pl.pallas_call(
        body, name="sb_fwd", grid=(bl, groups), in_specs=[col(0), col(groups), col(2 * groups)],
        out_specs=[col(0), pl.BlockSpec((None, nh, seq, 1), lambda b, p: (b, p, 0, 0)),
                   pl.BlockSpec(memory_space=pltpu.SMEM)],
        out_shape=[jax.ShapeDtypeStruct((bl * seq, WIDTH), BF16), jax.ShapeDtypeStruct((bl, HEADS, seq, 1), F32),
                   jax.ShapeDtypeStruct((bl * groups, nq), jnp.int32)],
        compiler_params=_params(("arbitrary", "arbitrary")),
    )(qkv, qkv, qkv)


def _sb_bwd(qkv, do, lsum, stop, bl, seq):
    qb, win, kb = min(ATT_Q, seq), min(ATT_WINDOW, seq), ATT_ALIGN
    nq = seq // qb
    nh, width, groups = 2 * ATT_PAIRS, ATT_PAIRS * _PAIR, _PAIRS // ATT_PAIRS
    pair_of = lambda h: slice((h // 2) * _PAIR, (h // 2 + 1) * _PAIR)

    def body(stop_ref, q_ref, k_ref, v_ref, do_ref, l_ref, dq_ref, dk_ref, dv_ref, dk_acc, dv_acc):
        g = pl.program_id(0) * groups + pl.program_id(1)
        first_head = _first_head_lanes()
        u_incl = _tri(kb, "le").astype(BF16)
        u_excl = _tri(kb, "lt").astype(BF16)
        dk_acc[...] = jnp.zeros_like(dk_acc)
        dv_acc[...] = jnp.zeros_like(dv_acc)

        def qblock(i, _):
            rows = pl.ds(pl.multiple_of(i * qb, qb), qb)
            qv = q_ref[rows, :]
            qs = qv * (HEAD_DIM ** -0.5)
            dob = do_ref[rows, :]
            qh = [_per_head(qs[:, pair_of(h)], first_head)[h % 2] for h in range(nh)]
            doh = [_per_head(dob[:, pair_of(h)], first_head)[h % 2] for h in range(nh)]
            ltot = [l_ref[h, rows, :] for h in range(nh)]

            first = (jnp.clip(stop_ref[g, i], 0, i * qb) // ATT_ALIGN) * ATT_ALIGN

            def window(n, carry):
                dqs, ps, es = (list(t) for t in carry)
                start = first + n * win
                lo = pl.multiple_of(jnp.minimum(start, seq - win), kb)
                cols = pl.ds(lo, win)
                kv, vv = k_ref[cols, :], v_ref[cols, :]
                valid = _sb_valid(i * qb, lo, start, seq, qb, win)
                dks, dvs = [], []
                for h in range(nh):
                    kp, vp = kv[:, pair_of(h)], vv[:, pair_of(h)]
                    z = _mm_nt(qh[h], kp)
                    sp = _sb_softplus(z)
                    spm = jnp.where(valid, sp, 0.0)
                    upto, ps[h] = _running_sums(spm, ps[h], u_incl, kb)
                    w = jnp.where(valid, jnp.exp(z - sp - (ltot[h] - upto)), 0.0)
                    e = _mm_nt(doh[h], vp) * w
                    dlf, es[h] = _running_sums(e, es[h], u_excl, kb)
                    sig = jnp.exp(z - sp)
                    dz = jnp.where(valid, e * (1.0 - sig) - dlf * sig, 0.0) * (HEAD_DIM ** -0.5)
                    dzb = dz.astype(BF16)
                    dvs.append(_mm_tn(w.astype(BF16), dob[:, pair_of(h)]))
                    dks.append(_mm_tn(dzb, qv[:, pair_of(h)]))
                    dqs[h] = dqs[h] + _mm(dzb, kp)
                for pp in range(ATT_PAIRS):
                    lanes = slice(pp * _PAIR, (pp + 1) * _PAIR)
                    dv_acc[cols, lanes] += jnp.where(first_head, dvs[2 * pp], dvs[2 * pp + 1])
                    dk_acc[cols, lanes] += jnp.where(first_head, dks[2 * pp], dks[2 * pp + 1])
                return tuple(dqs), tuple(ps), tuple(es)

            zero_q, zero_c = jnp.zeros((qb, _PAIR), F32), jnp.zeros((qb, 1), F32)
            dqs, _, _ = lax.fori_loop(0, ((i + 1) * qb - first + win - 1) // win, window,
                                      ((zero_q,) * nh, (zero_c,) * nh, (zero_c,) * nh))
            for pp in range(ATT_PAIRS):
                dq_ref[rows, pp * _PAIR:(pp + 1) * _PAIR] = jnp.where(
                    first_head, dqs[2 * pp], dqs[2 * pp + 1]).astype(BF16)
            return 0

        lax.fori_loop(0, nq, qblock, 0)
        dk_ref[...] = dk_acc[...].astype(BF16)
        dv_ref[...] = dv_acc[...].astype(BF16)

    col = lambda off: pl.BlockSpec((seq, width), lambda b, p, stop_ref: (b, off + p))
    return pl.pallas_call(
        body, name="sb_bwd",
        grid_spec=pltpu.PrefetchScalarGridSpec(
            num_scalar_prefetch=1, grid=(bl, groups),
            in_specs=[col(0), col(groups), col(2 * groups), col(0),
                      pl.BlockSpec((None, nh, seq, 1), lambda b, p, stop_ref: (b, p, 0, 0))],
            out_specs=[col(0), col(0), col(0)],
            scratch_shapes=[pltpu.VMEM((seq, width), F32), pltpu.VMEM((seq, width), F32)]),
        out_shape=[jax.ShapeDtypeStruct((bl * seq, WIDTH), BF16)] * 3,
        compiler_params=_params(("parallel", "parallel")),
    )(stop, qkv, qkv, qkv, do, lsum)


@jax.custom_vjp
def _lora_mm(x, w):
    return _mm(x.astype(BF16), w.astype(BF16))


_lora_mm.defvjp(
    lambda x, w: (_mm(x.astype(BF16), w.astype(BF16)), (x, w)),
    lambda res, ct: (_mm_nt(ct.astype(BF16), res[1].astype(BF16)), _mm_tn(res[0].astype(BF16), ct.astype(BF16))))


def _rw_prep_math(p, ps, mu, w0, w_up, a0, a_up, g_up, k_k, k_a):
    pm = p + (ps - p) * mu
    r, k, v = pm[:, :WIDTH], pm[:, WIDTH:2 * WIDTH], pm[:, 2 * WIDTH:3 * WIDTH]
    o = 3 * WIDTH
    xw, xa, xg = pm[:, o:o + W_LORA], pm[:, o + W_LORA:o + W_LORA + A_LORA], pm[:, o + W_LORA + A_LORA:]
    w_raw = w0 + _lora_mm(jnp.tanh(xw), w_up)
    lw = -jnp.exp(-_softplus(-w_raw) - 0.5)
    a = jax.nn.sigmoid(a0 + _lora_mm(xa, a_up))
    g = _lora_mm(jax.nn.sigmoid(xg), g_up)
    kk = k * k_k
    k2 = k * (1.0 + (a - 1.0) * k_a)
    return r, lw, k2, v, kk, a, g


def _shift_down(p, first_row):
    row = lax.broadcasted_iota(jnp.int32, p.shape, 0)
    return jnp.where(row == 0, first_row, pltpu.roll(p, 1, 0))


def _shift_up(p, last_row):
    row = lax.broadcasted_iota(jnp.int32, p.shape, 0)
    return jnp.where(row == p.shape[0] - 1, last_row, pltpu.roll(p, p.shape[0] - 1, 0))


_PREP_PARAM_SHAPES = [(1, RW_COLS), (1, WIDTH), (W_LORA, WIDTH), (1, WIDTH), (A_LORA, WIDTH), (G_LORA, WIDTH),
                      (1, WIDTH), (1, WIDTH)]


def _prev_rows_spec(tm):
    return pl.BlockSpec((8, RW_COLS), lambda i: (jnp.maximum(i * (tm // 8) - 1, 0), 0))


def _head_spec(tm, seq, tile_of=lambda i: i):
    per_seq = seq // tm
    return pl.BlockSpec((None, HEADS, tm, HEAD_DIM),
                        lambda i: (tile_of(i) // per_seq, 0, tile_of(i) % per_seq, 0))


def _split_heads(val, ref):
    for h in range(HEADS):
        ref[h] = val[:, h * HEAD_DIM:(h + 1) * HEAD_DIM]


def _join_heads(ref):
    return jnp.concatenate([ref[h] for h in range(HEADS)], axis=1)


def _rw_prep_fwd(prw, params, seq, tm):
    T = prw.shape[0]

    def body(p_ref, prev_ref, *rest):
        prm = [r_[...] for r_ in rest[:8]]
        outs = rest[8:]
        i = pl.program_id(0)
        first = jnp.where((i * tm) % seq == 0, 0.0, prev_ref[7:8, :])
        p = p_ref[...]
        vals = _rw_prep_math(p, _shift_down(p, first), *prm)
        for o_ref, val in zip(outs[:6], vals[:6]):
            _split_heads(val, o_ref)
        outs[6][...] = vals[6]

    by_head = jax.ShapeDtypeStruct((T // seq, HEADS, seq, HEAD_DIM), F32)
    return pl.pallas_call(
        body, name="rw_prep_fwd", grid=(T // tm,),
        in_specs=[_row_spec(tm, RW_COLS), _prev_rows_spec(tm)] + [_const_spec(s) for s in _PREP_PARAM_SHAPES],
        out_specs=[_head_spec(tm, seq)] * 6 + [_row_spec(tm, WIDTH)],
        out_shape=[by_head] * 6 + [jax.ShapeDtypeStruct((T, WIDTH), F32)],
        compiler_params=_params(("parallel",)),
    )(prw, prw, *params)


def _rw_prep_bwd(prw, params, cts, seq, tm):
    T = prw.shape[0]
    n = T // tm

    def body(p_ref, prev_ref, *rest):
        prm = [r_[...] for r_ in rest[:8]]
        ct = tuple(_join_heads(r_) for r_ in rest[8:14]) + (rest[14][...],)
        dp_ref = rest[15]
        dprm_refs = rest[16:24]
        carry = rest[24]
        step = pl.program_id(0)
        i = n - 1 - step
        first = jnp.where((i * tm) % seq == 0, 0.0, prev_ref[7:8, :])
        p = p_ref[...]
        _, vjp = jax.vjp(_rw_prep_math, p, _shift_down(p, first), *prm)
        grads = vjp(ct)
        dp, dps = grads[0], grads[1]
        nxt = jnp.where(jnp.logical_or(step == 0, ((i + 1) * tm) % seq == 0), 0.0, carry[0:1, :])
        dp_ref[...] = (dp + _shift_up(dps, nxt)).astype(BF16)
        carry[...] = dps[0:8, :]
        for ref, gval in zip(dprm_refs, grads[2:]):
            _acc_out(ref, gval, step == 0)

    rev = lambda w: pl.BlockSpec((tm, w), lambda s: (n - 1 - s, 0))
    prev = pl.BlockSpec((8, RW_COLS), lambda s: (jnp.maximum((n - 1 - s) * (tm // 8) - 1, 0), 0))
    return pl.pallas_call(
        body, name="rw_prep_bwd", grid=(n,),
        in_specs=([rev(RW_COLS), prev] + [_const_spec(s) for s in _PREP_PARAM_SHAPES]
                  + [_head_spec(tm, seq, lambda s: n - 1 - s)] * 6 + [rev(WIDTH)]),
        out_specs=[rev(RW_COLS)] + [pl.BlockSpec(s, lambda s_: (0, 0)) for s in _PREP_PARAM_SHAPES],
        out_shape=[jax.ShapeDtypeStruct((T, RW_COLS), BF16)] + [jax.ShapeDtypeStruct(s, F32) for s in _PREP_PARAM_SHAPES],
        scratch_shapes=[pltpu.VMEM((8, RW_COLS), F32)],
        compiler_params=_params(("arbitrary",)),
    )(prw, prw, *params, *cts)


def _make_bmm(passes):
    def raw(dn, a, b):
        d = lambda x, y: lax.dot_general(x, y, dn, preferred_element_type=F32)
        ah = a.astype(BF16)
        bh = b.astype(BF16)
        if passes == 1:
            return d(ah, bh)
        al = (a - ah.astype(F32)).astype(BF16)
        bl = (b - bh.astype(F32)).astype(BF16)
        return d(ah, bh) + (d(ah, bl) + d(al, bh))

    dn_nn = (((2,), (1,)), ((0,), (0,)))
    dn_nt = (((2,), (2,)), ((0,), (0,)))
    dn_tn = (((1,), (1,)), ((0,), (0,)))

    @jax.custom_vjp
    def nn(a, b):
        return raw(dn_nn, a, b)

    @jax.custom_vjp
    def nt(a, b):
        return raw(dn_nt, a, b)

    @jax.custom_vjp
    def tn(a, b):
        return raw(dn_tn, a, b)

    nn.defvjp(lambda a, b: (raw(dn_nn, a, b), (a, b)), lambda res, ct: (nt(ct, res[1]), tn(res[0], ct)))
    nt.defvjp(lambda a, b: (raw(dn_nt, a, b), (a, b)), lambda res, ct: (nn(ct, res[1]), tn(ct, res[0])))
    tn.defvjp(lambda a, b: (raw(dn_tn, a, b), (a, b)), lambda res, ct: (nt(res[1], ct), nn(res[0], ct)))

    def unit_lower_inverse(m):
        n = m.shape[-1]
        row = lax.broadcasted_iota(jnp.int32, (n, n), 0)
        col = lax.broadcasted_iota(jnp.int32, (n, n), 1)
        m16 = ((row // 16) == (col // 16)).astype(F32)
        m32 = ((row // 32) == (col // 32)).astype(F32)
        a1 = m * m16
        a2 = nn(a1, a1)
        a4 = nn(a2, a2)
        a8 = nn(a4, a4)
        inv = (row == col).astype(F32) - a1
        inv = inv + nn(inv, a2)
        inv = inv + nn(inv, a4)
        inv = inv + nn(inv, a8)
        inv = inv - nn(nn(inv, m * (m32 - m16)), inv)
        return inv - nn(nn(inv, m * (1.0 - m32)), inv)

    @jax.custom_vjp
    def inverse(m):
        return unit_lower_inverse(m)

    def inverse_fwd(m):
        inv = unit_lower_inverse(m)
        return inv, inv

    inverse.defvjp(inverse_fwd, lambda inv, ct: (-nt(tn(inv, ct), inv),))
    return nn, nt, tn, inverse


def _wkv_chunk(s0, r, lw, k, v, kk, a, lnw, lnb, rk):
    nn, nt, tn, inverse = _make_bmm(SCAN_PASSES)
    G, L, N = r.shape
    rep = lambda t: jnp.broadcast_to(t[None], (G // HEADS, HEADS, 1, N)).reshape(G, 1, N)
    kap = kk * lax.rsqrt(jnp.maximum(jnp.sum(kk * kk, axis=-1, keepdims=True), 1e-24))
    b = a * kap
    row = lax.broadcasted_iota(jnp.int32, (L, L), 0)
    col = lax.broadcasted_iota(jnp.int32, (L, L), 1)
    low_incl = (col <= row).astype(F32)
    low_strict = (col < row).astype(F32)
    c = _make_bmm(3)[0](jnp.broadcast_to(low_incl[None], (G, L, L)), lw)
    c_all = jnp.sum(lw, axis=1, keepdims=True)
    g_inv = jnp.exp(-c)
    kap_t = kap * jnp.exp(c - lw)
    b_t = b * g_inv
    k_t = k * g_inv
    r_t = r * jnp.exp(c)
    g_all = jnp.exp(c_all)
    m_b = nt(kap_t, b_t) * low_strict
    m_k = nt(kap_t, k_t) * low_strict
    n_b = nt(r_t, b_t) * low_incl
    n_k = nt(r_t, k_t) * low_incl
    rhs = -(nt(kap_t, s0) + nn(m_k, v))
    sa = nn(inverse(m_b), rhs)
    y = nt(r_t, s0) + nn(n_b, sa) + nn(n_k, v)
    s1 = s0 * g_all + tn(sa, b_t * g_all) + tn(v, k_t * g_all)
    mean = jnp.mean(y, axis=-1, keepdims=True)
    yc = y - mean
    var = jnp.mean(yc * yc, axis=-1, keepdims=True)
    out = yc * lax.rsqrt(var + GN_EPS) * rep(lnw) + rep(lnb)
    out = out + jnp.sum(r * k * rep(rk), axis=-1, keepdims=True) * v
    return out, s1


def _scan_heads_per_step(total_heads, seqs_wanted):
    n_seq = total_heads // HEADS
    return HEADS * max(d for d in range(1, seqs_wanted + 1) if n_seq % d == 0)


def _wkv_fwd(seqs, lnw, lnb, rk):
    G, S, N = seqs[0].shape
    L = SCAN_CHUNK
    nc = S // L

    def body(*refs):
        ins = [r_[...] for r_ in refs[:6]]
        prm = [r_[...] for r_ in refs[6:9]]
        out_ref, st_ref, state = refs[9], refs[10], refs[11]

        @pl.when(pl.program_id(1) == 0)
        def _():
            state[...] = jnp.zeros_like(state)

        s0 = state[...]
        st_ref[...] = s0
        out, s1 = _wkv_chunk(s0, *ins, *prm)
        out_ref[...] = out
        state[...] = s1

    gb = _scan_heads_per_step(G, SCAN_SEQS_FWD)
    blk = pl.BlockSpec((gb, L, N), lambda b, i: (b, i, 0))
    pspec = _const_spec((HEADS, 1, N))
    return pl.pallas_call(
        body, name="wkv_fwd", grid=(G // gb, nc), in_specs=[blk] * 6 + [pspec] * 3,
        out_specs=[blk, pl.BlockSpec((None, gb, N, N), lambda b, i: (i, b, 0, 0))],
        out_shape=[jax.ShapeDtypeStruct((G, S, N), F32), jax.ShapeDtypeStruct((nc, G, N, N), F32)],
        scratch_shapes=[pltpu.VMEM((gb, N, N), F32)],
        compiler_params=_params(("parallel", "arbitrary")),
    )(*seqs, lnw, lnb, rk)


def _wkv_bwd(seqs, states, dout, lnw, lnb, rk):
    G, S, N = seqs[0].shape
    L = SCAN_CHUNK
    nc = S // L

    def body(*refs):
        ins = [r_[...] for r_ in refs[:6]]
        s0 = refs[6][...]
        ct_out = refs[7][...]
        prm = [r_[...] for r_ in refs[8:11]]
        d_refs = refs[11:17]
        dprm_refs = refs[17:20]
        dstate = refs[20]
        step = pl.program_id(1)

        @pl.when(step == 0)
        def _():
            dstate[...] = jnp.zeros_like(dstate)

        _, vjp = jax.vjp(_wkv_chunk, s0, *ins, *prm)
        grads = vjp((ct_out, dstate[...]))
        dstate[...] = grads[0]
        for ref, gval in zip(d_refs, grads[1:7]):
            ref[...] = gval
        for ref, gval in zip(dprm_refs, grads[7:]):
            _acc_out(ref, gval, jnp.logical_and(step == 0, pl.program_id(0) == 0))

    gb = _scan_heads_per_step(G, SCAN_SEQS_BWD)
    blk = pl.BlockSpec((gb, L, N), lambda b, s: (b, nc - 1 - s, 0))
    pspec = _const_spec((HEADS, 1, N))
    pout = pl.BlockSpec((HEADS, 1, N), lambda b, s: (0, 0, 0))
    return pl.pallas_call(
        body, name="wkv_bwd", grid=(G // gb, nc),
        in_specs=[blk] * 6 + [pl.BlockSpec((None, gb, N, N), lambda b, s: (nc - 1 - s, b, 0, 0)), blk] + [pspec] * 3,
        out_specs=[blk] * 6 + [pout] * 3,
        out_shape=[jax.ShapeDtypeStruct((G, S, N), F32)] * 6 + [jax.ShapeDtypeStruct((HEADS, 1, N), F32)] * 3,
        scratch_shapes=[pltpu.VMEM((gb, N, N), F32)],
        compiler_params=_params(("arbitrary", "arbitrary")),
    )(*seqs, states, dout, lnw, lnb, rk)


def _merge_math(o_sb, rw_out, g_rw, gates, w_sb, w_rw, w_o):
    o_rw = (rw_out * g_rw).astype(BF16)
    a = _mm(o_sb, w_sb)
    b = _mm(o_rw, w_rw)
    g1, g2 = gates[:, :D_MODEL], gates[:, D_MODEL:]
    merged = (g1 * a + g2 * b).astype(BF16)
    return o_rw, a, b, g1, g2, merged, _mm(merged, w_o)


def _merge_fwd(x2, o_sb, rw_out, g_rw, gates, w_sb, w_rw, w_o, g_post, seq, tm):
    T = x2.shape[0]

    def body(x_ref, osb_ref, rw_ref, g_ref, gate_ref, wsb_ref, wrw_ref, wo_ref, gp_ref, x1_ref):
        z = _merge_math(osb_ref[...], _join_heads(rw_ref), g_ref[...], gate_ref[...], wsb_ref[...], wrw_ref[...], wo_ref[...])[-1]
        x1_ref[...] = x_ref[...] + _rms_fwd(z, gp_ref[...])[0]

    return pl.pallas_call(
        body, name="merge_fwd", grid=(T // tm,),
        in_specs=[_row_spec(tm, D_MODEL), _row_spec(tm, WIDTH), _head_spec(tm, seq), _row_spec(tm, WIDTH),
                  _row_spec(tm, GATE_COLS), _const_spec((WIDTH, D_MODEL)), _const_spec((WIDTH, D_MODEL)),
                  _const_spec((D_MODEL, D_MODEL)), _const_spec((1, D_MODEL))],
        out_specs=_row_spec(tm, D_MODEL),
        out_shape=jax.ShapeDtypeStruct((T, D_MODEL), F32),
        compiler_params=_params(("parallel",)),
    )(x2, o_sb, rw_out, g_rw, gates, w_sb, w_rw, w_o, g_post)


def _merge_bwd(dx1, o_sb, rw_out, g_rw, gates, w_sb, w_rw, w_o, g_post, seq, tm):
    T = dx1.shape[0]

    def body(dx1_ref, osb_ref, rw_ref, g_ref, gate_ref, wsb_ref, wrw_ref, wo_ref, gp_ref,
             orw_o, mrg_o, dz_o, da_o, db_o, dgate_o, dosb_o, drw_o, dg_o, dgp_o, dbg_o):
        rw_out_v, g_rw_v = _join_heads(rw_ref), g_ref[...]
        w_sb_v, w_rw_v, w_o_v = wsb_ref[...], wrw_ref[...], wo_ref[...]
        o_rw, a, b, g1, g2, merged, z = _merge_math(osb_ref[...], rw_out_v, g_rw_v, gate_ref[...], w_sb_v, w_rw_v, w_o_v)
        gain = gp_ref[...]
        _, zn, rstd = _rms_fwd(z, gain)
        dz, dgain = _rms_bwd(dx1_ref[...], zn, rstd, gain)
        dzb = dz.astype(BF16)
        dm = _mm_nt(dzb, w_o_v)
        dab = (dm * g1).astype(BF16)
        dbb = (dm * g2).astype(BF16)
        dgate = jnp.concatenate([dm * a * g1 * (1.0 - g1), dm * b * g2 * (1.0 - g2)], axis=1)
        do_rw = _mm_nt(dbb, w_rw_v)
        orw_o[...] = o_rw
        mrg_o[...] = merged
        dz_o[...] = dzb
        da_o[...] = dab
        db_o[...] = dbb
        dgate_o[...] = dgate.astype(BF16)
        dosb_o[...] = _mm_nt(dab, w_sb_v).astype(BF16)
        _split_heads(do_rw * g_rw_v, drw_o)
        dg_o[...] = do_rw * rw_out_v
        first = pl.program_id(0) == 0
        _acc_out(dgp_o, dgain, first)
        _acc_out(dbg_o, jnp.sum(dgate, axis=0, keepdims=True), first)

    acc = lambda n: pl.BlockSpec((1, n), lambda i: (0, 0))
    sd = jax.ShapeDtypeStruct
    return pl.pallas_call(
        body, name="merge_bwd", grid=(T // tm,),
        in_specs=[_row_spec(tm, D_MODEL), _row_spec(tm, WIDTH), _head_spec(tm, seq), _row_spec(tm, WIDTH),
                  _row_spec(tm, GATE_COLS), _const_spec((WIDTH, D_MODEL)), _const_spec((WIDTH, D_MODEL)),
                  _const_spec((D_MODEL, D_MODEL)), _const_spec((1, D_MODEL))],
        out_specs=[_row_spec(tm, WIDTH), _row_spec(tm, D_MODEL), _row_spec(tm, D_MODEL), _row_spec(tm, D_MODEL),
                   _row_spec(tm, D_MODEL), _row_spec(tm, GATE_COLS), _row_spec(tm, WIDTH), _head_spec(tm, seq),
                   _row_spec(tm, WIDTH), acc(D_MODEL), acc(GATE_COLS)],
        out_shape=[sd((T, WIDTH), BF16), sd((T, D_MODEL), BF16), sd((T, D_MODEL), BF16), sd((T, D_MODEL), BF16),
                   sd((T, D_MODEL), BF16), sd((T, GATE_COLS), BF16), sd((T, WIDTH), BF16),
                   sd((T // seq, HEADS, seq, HEAD_DIM), F32), sd((T, WIDTH), F32), sd((1, D_MODEL), F32),
                   sd((1, GATE_COLS), F32)],
        compiler_params=_params(("arbitrary",)),
    )(dx1, o_sb, rw_out, g_rw, gates, w_sb, w_rw, w_o, g_post)


def _ffn(x1, target, g_pre, g_post, w_gate, w_up, w_down, tm):
    T = x1.shape[0]

    def body(x1_ref, tgt_ref, gpre_ref, gpost_ref, wg_ref, wu_ref, wd_ref,
             loss_o, dx1_o, h_o, dgate_o, dup_o, act_o, df_o, dgpre_o, dgpost_o):
        x1v = x1_ref[...]
        gpre, gpost = gpre_ref[...], gpost_ref[...]
        wg, wu, wd = wg_ref[...], wu_ref[...], wd_ref[...]
        hn, xn1, rstd1 = _rms_fwd(x1v, gpre)
        h = hn.astype(BF16)
        gate = _mm(h, wg)
        up = _mm(h, wu)
        sg = jax.nn.sigmoid(gate)
        act = (gate * sg * up).astype(BF16)
        f = _mm(act, wd)
        fo, fn, rstd2 = _rms_fwd(f, gpost)
        diff = x1v + fo - tgt_ref[...]
        dy = diff * (1.0 / D_MODEL)
        df, dgpost = _rms_bwd(dy, fn, rstd2, gpost)
        dfb = df.astype(BF16)
        dact = _mm_nt(dfb, wd)
        dup = (dact * gate * sg).astype(BF16)
        dgate = (dact * up * (sg * (1.0 + gate * (1.0 - sg)))).astype(BF16)
        dh = _mm_nt(dgate, wg) + _mm_nt(dup, wu)
        dxn, dgpre = _rms_bwd(dh, xn1, rstd1, gpre)
        dx1_o[...] = dy + dxn
        h_o[...] = h
        dgate_o[...] = dgate
        dup_o[...] = dup
        act_o[...] = act
        df_o[...] = dfb
        first = pl.program_id(0) == 0
        part = jnp.sum(jnp.sum(diff * diff, axis=1, keepdims=True), axis=0, keepdims=True) * (0.5 / D_MODEL)
        _acc_out(loss_o, jnp.broadcast_to(part, (8, 128)), first)
        _acc_out(dgpre_o, dgpre, first)
        _acc_out(dgpost_o, dgpost, first)

    acc = lambda r, n: pl.BlockSpec((r, n), lambda i: (0, 0))
    sd = jax.ShapeDtypeStruct
    return pl.pallas_call(
        body, name="ffn", grid=(T // tm,),
        in_specs=[_row_spec(tm, D_MODEL), _row_spec(tm, D_MODEL), _const_spec((1, D_MODEL)), _const_spec((1, D_MODEL)),
                  _const_spec((D_MODEL, D_FF)), _const_spec((D_MODEL, D_FF)), _const_spec((D_FF, D_MODEL))],
        out_specs=[acc(8, 128), _row_spec(tm, D_MODEL), _row_spec(tm, D_MODEL), _row_spec(tm, D_FF), _row_spec(tm, D_FF),
                   _row_spec(tm, D_FF), _row_spec(tm, D_MODEL), acc(1, D_MODEL), acc(1, D_MODEL)],
        out_shape=[sd((8, 128), F32), sd((T, D_MODEL), F32), sd((T, D_MODEL), BF16), sd((T, D_FF), BF16),
                   sd((T, D_FF), BF16), sd((T, D_FF), BF16), sd((T, D_MODEL), BF16), sd((1, D_MODEL), F32),
                   sd((1, D_MODEL), F32)],
        compiler_params=_params(("arbitrary",)),
    )(x1, target, g_pre, g_post, w_gate, w_up, w_down)


def _local_step(x, target, sm, wt):
    bl, seq, _ = x.shape
    T = bl * seq
    tm = min(ROW_TILE, T)
    x2 = x.reshape(T, D_MODEL)
    tgt2 = target.reshape(T, D_MODEL)
    w_qkv, w_prw, w_gate = wt["w_in"][:, :SB_COLS], wt["w_in"][:, SB_COLS:SB_COLS + RW_COLS], wt["w_in"][:, SB_COLS + RW_COLS:]
    h, qkv, prw, gates = _in_proj_fwd(x2, sm["norm_mix_pre"], w_qkv, w_prw, w_gate, sm["b_gate"], tm)
    o_sb, lsum, sb_stop = _sb_fwd(qkv, bl, seq)
    prep_params = [sm["mu_rw"], sm["w0"], wt["w_up"].astype(F32), sm["a0"], wt["a_up"].astype(F32),
                   wt["g_up"].astype(F32), sm["k_k"], sm["k_a"]]
    prep = _rw_prep_fwd(prw, prep_params, seq, tm)
    by_head = lambda t: t.reshape(bl, HEADS, seq, HEAD_DIM)
    seqs = [t.reshape(bl * HEADS, seq, HEAD_DIM) for t in prep[:6]]
    g_rw = prep[6]
    lnw, lnb, rk = (sm[n].reshape(HEADS, 1, HEAD_DIM) for n in ("lnx_w", "lnx_b", "r_k"))
    rw_out_h, states = _wkv_fwd(seqs, lnw, lnb, rk)
    rw_out = by_head(rw_out_h)
    x1 = _merge_fwd(x2, o_sb, rw_out, g_rw, gates, wt["w_sb_out"], wt["w_rw_out"], wt["w_o"], sm["norm_mix_post"],
                    seq, tm)
    (loss_part, dx1, h2, dffg, dffu, act, dff, d_nfpre, d_nfpost) = _ffn(
        x1, tgt2, sm["norm_ffn_pre"], sm["norm_ffn_post"], wt["w_ffn_gate"], wt["w_ffn_up"], wt["w_ffn_down"], tm)
    (o_rw, merged, dz, da, db, dgate, do_sb, d_rw_out, d_g_rw, d_npost, d_bgate) = _merge_bwd(
        dx1, o_sb, rw_out, g_rw, gates, wt["w_sb_out"], wt["w_rw_out"], wt["w_o"], sm["norm_mix_post"], seq, tm)
    dqkv = jnp.concatenate(_sb_bwd(qkv, do_sb, lsum, sb_stop, bl, seq), axis=1)
    wkv_g = _wkv_bwd(seqs, states, d_rw_out.reshape(bl * HEADS, seq, HEAD_DIM), lnw, lnb, rk)
    cts = [by_head(t) for t in wkv_g[:6]] + [d_g_rw]
    prep_g = _rw_prep_bwd(prw, prep_params, cts, seq, tm)
    dprw = prep_g[0]
    d_mu, d_w0, d_wup, d_a0, d_aup, d_gup, d_kk, d_ka = prep_g[1:]
    grad_x, d_npre = _in_proj_bwd(x2, sm["norm_mix_pre"], dx1, dqkv, dprw, dgate, w_qkv, w_prw, w_gate, tm)
    gw = {
        "w_in": jnp.concatenate([_grad_w(h, dqkv, "gw_in_qkv"), _grad_w(h, dprw, "gw_in_rw"), _grad_w(h, dgate, "gw_in_gate")], axis=1),
        "w_up": d_wup, "a_up": d_aup, "g_up": d_gup,
        "w_sb_out": _grad_w(o_sb, da, "gw_sb_out"), "w_rw_out": _grad_w(o_rw, db, "gw_rw_out"),
        "w_o": _grad_w(merged, dz, "gw_o"),
        "w_ffn_gate": _grad_w(h2, dffg, "gw_ffn_gate"), "w_ffn_up": _grad_w(h2, dffu, "gw_ffn_up"),
        "w_ffn_down": _grad_w(act, dff, "gw_ffn_down"),
    }
    gs = {
        "norm_mix_pre": d_npre, "b_gate": d_bgate, "mu_rw": d_mu, "w0": d_w0, "a0": d_a0, "k_k": d_kk, "k_a": d_ka,
        "r_k": wkv_g[8].reshape(1, WIDTH), "lnx_w": wkv_g[6].reshape(1, WIDTH), "lnx_b": wkv_g[7].reshape(1, WIDTH),
        "norm_mix_post": d_npost, "norm_ffn_pre": d_nfpre, "norm_ffn_post": d_nfpost,
    }
    return loss_part, grad_x.reshape(x.shape), gw, gs


_SHARDED = [("w_in", 1, (D_MODEL, (SB_COLS + RW_COLS + GATE_COLS) // N_DEV)), ("w_up", 1, (W_LORA, WIDTH // N_DEV)),
            ("a_up", 1, (A_LORA, WIDTH // N_DEV)), ("g_up", 1, (G_LORA, WIDTH // N_DEV)),
            ("w_sb_out", 1, (WIDTH, D_MODEL // N_DEV)), ("w_rw_out", 1, (WIDTH, D_MODEL // N_DEV)),
            ("w_o", 0, (D_MODEL // N_DEV, D_MODEL)), ("w_ffn_gate", 1, (D_MODEL, D_FF // N_DEV)),
            ("w_ffn_up", 1, (D_MODEL, D_FF // N_DEV)), ("w_ffn_down", 0, (D_FF // N_DEV, D_MODEL))]
_LANES = 128
_PACK_ROWS = [s[0] * s[1] // _LANES for _, _, s in _SHARDED]
_PACK_TOTAL = sum(_PACK_ROWS)
_SMALL = [("norm_mix_pre", D_MODEL), ("b_gate", GATE_COLS), ("mu_rw", RW_COLS), ("w0", WIDTH), ("a0", WIDTH),
          ("k_k", WIDTH), ("k_a", WIDTH), ("r_k", WIDTH), ("lnx_w", WIDTH), ("lnx_b", WIDTH),
          ("norm_mix_post", D_MODEL), ("norm_ffn_pre", D_MODEL), ("norm_ffn_post", D_MODEL)]
_SMALL_ROWS = 96


def _pack_shards(shards, dtype):
    return jnp.concatenate([shards[n].astype(dtype).reshape(-1, _LANES) for n, _, _ in _SHARDED], axis=0)


def _unpack_shards(packed):
    out, r0 = {}, 0
    for (n, _, shp), rows in zip(_SHARDED, _PACK_ROWS):
        out[n] = packed[r0:r0 + rows].reshape(shp)
        r0 += rows
    return out


def _unpack_gathered(g):
    out, r0 = {}, 0
    for (n, axis, shp), rows in zip(_SHARDED, _PACK_ROWS):
        blk = g[:, r0:r0 + rows].reshape((N_DEV,) + shp)
        out[n] = blk.reshape(N_DEV * shp[0], shp[1]) if axis == 0 else blk.transpose(1, 0, 2).reshape(shp[0], N_DEV * shp[1])
        r0 += rows
    return out


def _pack_full_grads(gw):
    parts = []
    for n, axis, shp in _SHARDED:
        g = gw[n]
        blk = g.reshape((N_DEV,) + shp) if axis == 0 else g.reshape(shp[0], N_DEV, shp[1]).transpose(1, 0, 2)
        parts.append(blk.reshape(N_DEV, -1, _LANES))
    return jnp.concatenate(parts, axis=1)


def _pack_small(vals, extra=None):
    rows = [vals[n].reshape(-1, _LANES) for n, _ in _SMALL]
    used = sum(sz for _, sz in _SMALL) // _LANES
    tail = jnp.zeros((_SMALL_ROWS - used, _LANES), F32)
    if extra is not None:
        tail = tail.at[0, 0].set(extra)
    return jnp.concatenate(rows + [tail], axis=0)


def _unpack_small(packed):
    out, o = {}, 0
    for n, sz in _SMALL:
        out[n] = packed[o:o + sz // _LANES]
        o += sz // _LANES
    return out, packed[o, 0]


_ANY = pl.BlockSpec(memory_space=pl.ANY)


def _all_gather(block):
    rows, lanes = block.shape

    def body(x_ref, out_ref, send_sems, recv_sems, local_sem):
        x, y, c = lax.axis_index("x"), lax.axis_index("y"), lax.axis_index("c")
        me, sibling = (x, y, c), (x, y, 1 - c)
        chips = [(1 - x, y), (x, 1 - y), (1 - x, 1 - y)]

        def slot(px, py, pc):
            return out_ref.at[4 * px + 2 * py + pc]

        def copy(k, blk, to, src=None):
            return pltpu.make_async_remote_copy(
                src_ref=slot(*blk) if src is None else src, dst_ref=slot(*blk),
                send_sem=send_sems.at[k], recv_sem=recv_sems.at[k], device_id=to, device_id_type=MESH)

        mine = pltpu.make_async_copy(x_ref, slot(*me), local_sem)
        mine.start()
        first = [copy(0, me, sibling, src=x_ref)]
        first += [copy(1 + j, me, (*chip, c), src=x_ref) for j, chip in enumerate(chips)]
        for cp in first:
            cp.start()
        passed = [copy(4 + j, (*chip, c), sibling) for j, chip in enumerate(chips)]
        for j, chip in enumerate(chips):
            copy(1 + j, (*chip, c), me).wait_recv()
            passed[j].start()
        copy(0, sibling, me).wait_recv()
        for j, chip in enumerate(chips):
            copy(4 + j, (*chip, 1 - c), me).wait_recv()
        for cp in first + passed:
            cp.wait_send()
        mine.wait()

    return pl.pallas_call(
        body, name="all_gather_weights", in_specs=[_ANY], out_specs=_ANY,
        out_shape=jax.ShapeDtypeStruct((N_DEV, rows, lanes), block.dtype),
        scratch_shapes=[pltpu.SemaphoreType.DMA((7,)), pltpu.SemaphoreType.DMA((7,)), pltpu.SemaphoreType.DMA],
    )(block)


def _exchange_core(pack, small):
    _, _, rows, lanes = pack.shape

    def body(pack_ref, small_ref, got_ref, parts_ref, send_sems, recv_sems, s_send, s_recv, local_sem):
        x, y, c = lax.axis_index("x"), lax.axis_index("y"), lax.axis_index("c")
        sibling = (x, y, 1 - c)
        me = 4 * x + 2 * y + c
        mine = pltpu.make_async_copy(small_ref, parts_ref.at[me], local_sem)
        mine.start()
        big = [pltpu.make_async_remote_copy(
            src_ref=pack_ref.at[1 - c, j], dst_ref=got_ref.at[j], send_sem=send_sems.at[j], recv_sem=recv_sems.at[j],
            device_id=sibling, device_id_type=MESH) for j in range(4)]
        for cp in big:
            cp.start()
        others = [(k, (x ^ (k >> 2), y ^ ((k >> 1) & 1), c ^ (k & 1))) for k in range(1, N_DEV)]
        tiny = [pltpu.make_async_remote_copy(
            src_ref=small_ref, dst_ref=parts_ref.at[me], send_sem=s_send.at[k], recv_sem=s_recv.at[k],
            device_id=to, device_id_type=MESH) for k, to in others]
        for cp in tiny:
            cp.start()
        for cp in big:
            cp.wait_recv()
        for (k, (px, py, pc)), cp in zip(others, tiny):
            pltpu.make_async_remote_copy(
                src_ref=small_ref, dst_ref=parts_ref.at[4 * px + 2 * py + pc], send_sem=s_send.at[k],
                recv_sem=s_recv.at[k], device_id=(px, py, pc), device_id_type=MESH).wait_recv()
        for cp in big + tiny:
            cp.wait_send()
        mine.wait()

    return pl.pallas_call(
        body, name="exchange_core", in_specs=[_ANY, _ANY], out_specs=[_ANY, _ANY],
        out_shape=[jax.ShapeDtypeStruct((4, rows, lanes), F32), jax.ShapeDtypeStruct((N_DEV,) + small.shape, F32)],
        scratch_shapes=[pltpu.SemaphoreType.DMA((4,)), pltpu.SemaphoreType.DMA((4,)), pltpu.SemaphoreType.DMA((N_DEV,)),
                        pltpu.SemaphoreType.DMA((N_DEV,)), pltpu.SemaphoreType.DMA],
    )(pack, small)


def _add_core_parts(pack, got, core):
    _, _, rows, lanes = pack.shape
    tr = 2000

    def body(core_ref, a_ref, b_ref, o_ref):
        o_ref[...] = (a_ref[...] + b_ref[...]).astype(BF16)

    return pl.pallas_call(
        body, name="add_core_parts",
        grid_spec=pltpu.PrefetchScalarGridSpec(
            num_scalar_prefetch=1, grid=(4, rows // tr),
            in_specs=[pl.BlockSpec((None, None, tr, lanes), lambda j, i, core_ref: (core_ref[0], j, i, 0)),
                      pl.BlockSpec((None, tr, lanes), lambda j, i, core_ref: (j, i, 0))],
            out_specs=pl.BlockSpec((None, tr, lanes), lambda j, i, core_ref: (j, i, 0))),
        out_shape=jax.ShapeDtypeStruct((4, rows, lanes), BF16),
        compiler_params=_params(("parallel", "parallel")),
    )(core, pack, got)


def _exchange_chips(chip_sums):
    _, rows, lanes = chip_sums.shape

    def body(src_ref, got_ref, send_sems, recv_sems):
        x, y, c = lax.axis_index("x"), lax.axis_index("y"), lax.axis_index("c")
        flips = [(1, 0), (0, 1), (1, 1)]
        copies = []
        for k, (fx, fy) in enumerate(flips):
            px, py = x ^ fx, y ^ fy
            copies.append(pltpu.make_async_remote_copy(
                src_ref=src_ref.at[2 * px + py], dst_ref=got_ref.at[k], send_sem=send_sems.at[k],
                recv_sem=recv_sems.at[k], device_id=(px, py, c), device_id_type=MESH))
        for cp in copies:
            cp.start()
        for cp in copies:
            cp.wait_recv()
        for cp in copies:
            cp.wait_send()

    return pl.pallas_call(
        body, name="exchange_chips", in_specs=[_ANY], out_specs=_ANY,
        out_shape=jax.ShapeDtypeStruct((3, rows, lanes), chip_sums.dtype),
        scratch_shapes=[pltpu.SemaphoreType.DMA((3,)), pltpu.SemaphoreType.DMA((3,))],
    )(chip_sums)


def _sum_chip_parts(chip_sums, got, chip):
    _, rows, lanes = chip_sums.shape
    tr = 2000

    def body(chip_ref, own_ref, got_ref, o_ref):
        f32 = lambda t: t.astype(F32)
        o_ref[...] = ((f32(own_ref[...]) + f32(got_ref[0])) + f32(got_ref[1])) + f32(got_ref[2])

    return pl.pallas_call(
        body, name="sum_chip_parts",
        grid_spec=pltpu.PrefetchScalarGridSpec(
            num_scalar_prefetch=1, grid=(rows // tr,),
            in_specs=[pl.BlockSpec((None, tr, lanes), lambda i, chip_ref: (chip_ref[0], i, 0)),
                      pl.BlockSpec((3, tr, lanes), lambda i, chip_ref: (0, i, 0))],
            out_specs=pl.BlockSpec((tr, lanes), lambda i, chip_ref: (i, 0))),
        out_shape=jax.ShapeDtypeStruct((rows, lanes), F32),
        compiler_params=_params(("parallel",)),
    )(chip, chip_sums, got)


def _adamw_math(w, g, m, v):
    m = ADAM_B1 * m + (1.0 - ADAM_B1) * g
    v = ADAM_B2 * v + (1.0 - ADAM_B2) * (g * g)
    m_hat = m / (1.0 - ADAM_B1 ** ADAM_STEP)
    v_hat = v / (1.0 - ADAM_B2 ** ADAM_STEP)
    return -ADAM_LR * (m_hat / (jnp.sqrt(v_hat) + ADAM_EPS) + ADAM_WD * w), m, v


def _adamw(w, g, m, v, name):
    rows, cols = w.shape
    tr = 256 if rows % 256 == 0 and rows * cols > 2 ** 19 else rows

    def body(w_ref, g_ref, m_ref, v_ref, d_o, m_o, v_o):
        d_o[...], m_o[...], v_o[...] = _adamw_math(w_ref[...], g_ref[...], m_ref[...], v_ref[...])

    spec = pl.BlockSpec((tr, cols), lambda i: (i, 0))
    return pl.pallas_call(
        body, name=name, grid=(rows // tr,), in_specs=[spec] * 4, out_specs=[spec] * 3,
        out_shape=[jax.ShapeDtypeStruct((rows, cols), F32)] * 3, compiler_params=_params(("parallel",)),
    )(w, g, m, v)


def _adamw_small(parts, w, m, v):
    def body(p_ref, w_ref, m_ref, v_ref, g_o, d_o, m_o, v_o):
        g = p_ref[0]
        for d in range(1, N_DEV):
            g = g + p_ref[d]
        g_o[...] = g
        d_o[...], m_o[...], v_o[...] = _adamw_math(w_ref[...], g, m_ref[...], v_ref[...])

    return pl.pallas_call(
        body, name="adamw_small", out_shape=[jax.ShapeDtypeStruct(w.shape, F32)] * 4, compiler_params=_params(),
    )(parts, w, m, v)


_WEIGHT_NAMES = ['norm_mix_pre', 'w_in', 'b_gate', 'mu_rw', 'w0', 'w_up', 'a0', 'a_up', 'g_up', 'k_k', 'k_a', 'r_k',
                 'lnx_w', 'lnx_b', 'w_sb_out', 'w_rw_out', 'w_o', 'norm_mix_post', 'norm_ffn_pre', 'w_ffn_gate',
                 'w_ffn_up', 'w_ffn_down', 'norm_ffn_post']


def _step(x, target, w, m, v):
    sharded = [n for n, _, _ in _SHARDED]
    sm = {n: w[n].reshape(1, -1) for n, _ in _SMALL}
    own = {n: w[n][0] for n in sharded}
    wt = _unpack_gathered(_all_gather(_pack_shards(own, BF16)))
    loss_part, grad_x, gw, gs = _local_step(x, target, sm, wt)

    cx, cy, cc = lax.axis_index("x"), lax.axis_index("y"), lax.axis_index("c")
    core = jnp.reshape(cc, (1,)).astype(jnp.int32)
    chip = jnp.reshape(2 * cx + cy, (1,)).astype(jnp.int32)
    pack = _pack_full_grads(gw).reshape(4, 2, _PACK_TOTAL, _LANES).transpose(1, 0, 2, 3)
    got_core, small_parts = _exchange_core(pack, _pack_small(gs, loss_part[0, 0]))
    chip_sums = _add_core_parts(pack, got_core, core)
    grads_packed = _sum_chip_parts(chip_sums, _exchange_chips(chip_sums), chip)
    g_sh = _unpack_shards(grads_packed)

    g_small, d_small, m_small, v_small = _adamw_small(
        small_parts, _pack_small({n: w[n] for n, _ in _SMALL}), _pack_small({n: m[n] for n, _ in _SMALL}),
        _pack_small({n: v[n] for n, _ in _SMALL}))
    (g_s, loss), (d_s, _), (m_s, _), (v_s, _) = (_unpack_small(t) for t in (g_small, d_small, m_small, v_small))

    grads, deltas, new_m, new_v = {}, {}, {}, {}
    for n in _WEIGHT_NAMES:
        if n in g_sh:
            d_, m_, v_ = _adamw(own[n], g_sh[n], m[n][0], v[n][0], "adamw_" + n)
            grads[n], deltas[n], new_m[n], new_v[n] = (t.reshape(w[n].shape) for t in (g_sh[n], d_, m_, v_))
        else:
            grads[n], deltas[n], new_m[n], new_v[n] = (t[n].reshape(w[n].shape) for t in (g_s, d_s, m_s, v_s))
    return (loss, grad_x, *[grads[n] for n in _WEIGHT_NAMES], *[deltas[n] for n in _WEIGHT_NAMES],
            *[new_m[n] for n in _WEIGHT_NAMES], *[new_v[n] for n in _WEIGHT_NAMES])


def kernel(x, norm_mix_pre, w_in, b_gate, mu_rw, w0, w_up, a0, a_up, g_up, k_k, k_a, r_k, lnx_w, lnx_b, w_sb_out, w_rw_out, w_o, norm_mix_post, norm_ffn_pre, w_ffn_gate, w_ffn_up, w_ffn_down, norm_ffn_post, loss_target, m_norm_mix_pre, m_w_in, m_b_gate, m_mu_rw, m_w0, m_w_up, m_a0, m_a_up, m_g_up, m_k_k, m_k_a, m_r_k, m_lnx_w, m_lnx_b, m_w_sb_out, m_w_rw_out, m_w_o, m_norm_mix_post, m_norm_ffn_pre, m_w_ffn_gate, m_w_ffn_up, m_w_ffn_down, m_norm_ffn_post, v_norm_mix_pre, v_w_in, v_b_gate, v_mu_rw, v_w0, v_w_up, v_a0, v_a_up, v_g_up, v_k_k, v_k_a, v_r_k, v_lnx_w, v_lnx_b, v_w_sb_out, v_w_rw_out, v_w_o, v_norm_mix_post, v_norm_ffn_pre, v_w_ffn_gate, v_w_ffn_up, v_w_ffn_down, v_norm_ffn_post):
    args = locals()
    w = {n: args[n] for n in _WEIGHT_NAMES}
    m = {n: args["m_" + n] for n in _WEIGHT_NAMES}
    v = {n: args["v_" + n] for n in _WEIGHT_NAMES}
    return _step(x, loss_target, w, m, v)
```

```python
import functools

import jax
import jax.numpy as jnp
from jax import lax
from jax.experimental import pallas as pl
from jax.experimental.pallas import tpu as pltpu

F32 = jnp.float32
BF16 = jnp.bfloat16

D_MODEL = 1024
HEADS = 8
HEAD_DIM = 64
WIDTH = HEADS * HEAD_DIM
W_LORA, A_LORA, G_LORA = 64, 64, 128
SB_COLS = 3 * WIDTH
RW_COLS = 3 * WIDTH + W_LORA + A_LORA + G_LORA
GATE_COLS = 2 * D_MODEL
D_FF = 2816
RMS_EPS = 1e-6
GN_EPS = HEAD_DIM * 1e-5
N_DEV = 8

ADAM_LR, ADAM_B1, ADAM_B2, ADAM_EPS, ADAM_WD, ADAM_STEP = 0.001, 0.9, 0.999, 1e-08, 0.01, 10

ROW_TILE = 256
SCAN_CHUNK = 64
ATT_ALIGN = 128
ATT_WINDOW = 384
ATT_Q = 128
ATT_PAIRS = 2
SB_DEAD = -104.0
SCAN_SEQS_FWD = 4
SCAN_SEQS_BWD = 2
SCAN_PASSES = 1
VMEM_LIMIT = 56 * 2 ** 20

MESH = pl.DeviceIdType.MESH


def _params(sem=None, vmem=VMEM_LIMIT):
    kw = dict(vmem_limit_bytes=vmem)
    if sem is not None:
        kw["dimension_semantics"] = sem
    return pltpu.CompilerParams(**kw)


def _const_spec(shape):
    nd = len(shape)
    return pl.BlockSpec(shape, lambda *_: (0,) * nd, pipeline_mode=pl.Buffered(1))


def _row_spec(tm, n):
    return pl.BlockSpec((tm, n), lambda i: (i, 0))


def _mm(a, b):
    return lax.dot_general(a, b, (((1,), (0,)), ((), ())), preferred_element_type=F32)


def _mm_nt(a, b):
    return lax.dot_general(a, b, (((1,), (1,)), ((), ())), preferred_element_type=F32)


def _mm_tn(a, b):
    return lax.dot_general(a, b, (((0,), (0,)), ((), ())), preferred_element_type=F32)


def _softplus(z):
    return jnp.maximum(z, 0.0) + jnp.log1p(jnp.exp(-jnp.abs(z)))


def _rms_fwd(x, gain):
    rstd = lax.rsqrt(jnp.mean(x * x, axis=-1, keepdims=True) + RMS_EPS)
    xn = x * rstd
    return xn * gain, xn, rstd


def _rms_bwd(dy, xn, rstd, gain):
    u = dy * gain
    dx = rstd * (u - xn * jnp.mean(u * xn, axis=-1, keepdims=True))
    return dx, jnp.sum(dy * xn, axis=0, keepdims=True)


def _acc_out(ref, val, first):
    @pl.when(first)
    def _():
        ref[...] = val

    @pl.when(jnp.logical_not(first))
    def _():
        ref[...] += val


_IN_COLS = SB_COLS + RW_COLS + GATE_COLS
_QKV_OF, _RW_OF, _GATE_OF = slice(0, SB_COLS), slice(SB_COLS, SB_COLS + RW_COLS), slice(SB_COLS + RW_COLS, _IN_COLS)


def _in_proj_fwd(x2, g_pre, w_in, b_gate, tm):
    T = x2.shape[0]

    def body(x_ref, g_ref, w_ref, b_ref, h_ref, qkv_ref, prw_ref, gate_ref):
        h = _rms_fwd(x_ref[...], g_ref[...])[0].astype(BF16)
        h_ref[...] = h
        qkv_ref[...] = _mm(h, w_ref[:, _QKV_OF]).astype(BF16)
        prw_ref[...] = _mm(h, w_ref[:, _RW_OF])
        gate_ref[...] = jax.nn.sigmoid(_mm(h, w_ref[:, _GATE_OF]) + b_ref[...])

    return pl.pallas_call(
        body, name="in_proj_fwd", grid=(T // tm,),
        in_specs=[_row_spec(tm, D_MODEL), _const_spec((1, D_MODEL)), _const_spec((D_MODEL, _IN_COLS)),
                  _const_spec((1, GATE_COLS))],
        out_specs=[_row_spec(tm, D_MODEL), _row_spec(tm, SB_COLS), _row_spec(tm, RW_COLS), _row_spec(tm, GATE_COLS)],
        out_shape=[jax.ShapeDtypeStruct((T, D_MODEL), BF16), jax.ShapeDtypeStruct((T, SB_COLS), BF16),
                   jax.ShapeDtypeStruct((T, RW_COLS), F32), jax.ShapeDtypeStruct((T, GATE_COLS), F32)],
        compiler_params=_params(("parallel",)),
    )(x2, g_pre, w_in, b_gate)


def _in_proj_bwd(x2, g_pre, dx1, dqkv, dprw, dgate, w_in, tm):
    T = x2.shape[0]

    def body(x_ref, g_ref, dx1_ref, dq_ref, dr_ref, dg_ref, w_ref, gx_ref, dgain_ref):
        dh = (_mm_nt(dq_ref[...], w_ref[:, _QKV_OF]) + _mm_nt(dr_ref[...], w_ref[:, _RW_OF])
              + _mm_nt(dg_ref[...], w_ref[:, _GATE_OF]))
        gain = g_ref[...]
        _, xn, rstd = _rms_fwd(x_ref[...], gain)
        dx, dgain = _rms_bwd(dh, xn, rstd, gain)
        gx_ref[...] = dx1_ref[...] + dx
        _acc_out(dgain_ref, dgain, pl.program_id(0) == 0)

    return pl.pallas_call(
        body, name="in_proj_bwd", grid=(T // tm,),
        in_specs=[_row_spec(tm, D_MODEL), _const_spec((1, D_MODEL)), _row_spec(tm, D_MODEL), _row_spec(tm, SB_COLS),
                  _row_spec(tm, RW_COLS), _row_spec(tm, GATE_COLS), _const_spec((D_MODEL, _IN_COLS))],
        out_specs=[_row_spec(tm, D_MODEL), pl.BlockSpec((1, D_MODEL), lambda i: (0, 0))],
        out_shape=[jax.ShapeDtypeStruct((T, D_MODEL), F32), jax.ShapeDtypeStruct((1, D_MODEL), F32)],
        compiler_params=_params(("arbitrary",)),
    )(x2, g_pre, dx1, dqkv, dprw, dgate, w_in)


def _pick_tile(n, cap):
    best = None
    for t in range(128, min(n, cap) + 1, 128):
        if n % t == 0:
            best = t
    return n if best is None else best


def _grad_w(a, b, name):
    T, K = a.shape
    N = b.shape[1]
    tk, tn, tt = _pick_tile(K, 1408), _pick_tile(N, 2048), min(T, 2048)

    def body(a_ref, b_ref, o_ref):
        _acc_out(o_ref, _mm_tn(a_ref[...], b_ref[...]), pl.program_id(2) == 0)

    return pl.pallas_call(
        body, name=name, grid=(K // tk, N // tn, T // tt),
        in_specs=[pl.BlockSpec((tt, tk), lambda i, j, t: (t, i)), pl.BlockSpec((tt, tn), lambda i, j, t: (t, j))],
        out_specs=pl.BlockSpec((tk, tn), lambda i, j, t: (i, j)),
        out_shape=jax.ShapeDtypeStruct((K, N), F32),
        compiler_params=_params(("parallel", "parallel", "arbitrary")),
    )(a, b)


def _tri(n, kind):
    r = lax.broadcasted_iota(jnp.int32, (n, n), 0)
    c = lax.broadcasted_iota(jnp.int32, (n, n), 1)
    return {"gt": r > c, "le": r <= c, "lt": r < c, "ge": r >= c}[kind]


def _split_mm(x, u):
    hi = x.astype(BF16)
    lo = (x - hi.astype(F32)).astype(BF16)
    return _mm(hi, u) + _mm(lo, u)


def _running_sums(x, carry, tri, kb, reverse=False):
    blocks = range(x.shape[1] // kb)
    parts = {}
    for b in (reversed(blocks) if reverse else blocks):
        piece = x[:, b * kb:(b + 1) * kb]
        parts[b] = carry + _split_mm(piece, tri)
        carry = carry + jnp.sum(piece, axis=1, keepdims=True)
    return jnp.concatenate([parts[b] for b in blocks], axis=1), carry


def _sb_valid(row0, col0, first, last, qb, kb):
    ahead = lax.broadcasted_iota(jnp.int32, (qb, kb), 1) - lax.broadcasted_iota(jnp.int32, (qb, kb), 0)
    col = lax.broadcasted_iota(jnp.int32, (1, kb), 1)
    return jnp.logical_and(ahead < row0 - col0, jnp.logical_and(col >= first - col0, col < last - col0))


def _sb_softplus(z):
    return jnp.maximum(z, 0.0) + jnp.log(1.0 + jnp.exp(-jnp.abs(z)))


_PAIR = 2 * HEAD_DIM
_PAIRS = WIDTH // _PAIR


def _first_head_lanes():
    return lax.broadcasted_iota(jnp.int32, (1, _PAIR), 1) < HEAD_DIM


def _per_head(t, first_head):
    zero = jnp.zeros_like(t)
    return jnp.where(first_head, t, zero), jnp.where(first_head, zero, t)


def _sb_fwd(qkv, bl, seq, ride=None):
    qb, win, kb = min(ATT_Q, seq), min(ATT_WINDOW, seq), ATT_ALIGN
    nq = seq // qb
    nh, width, groups = 2 * ATT_PAIRS, ATT_PAIRS * _PAIR, _PAIRS // ATT_PAIRS
    pair_of = lambda h: slice((h // 2) * _PAIR, (h // 2 + 1) * _PAIR)

    steps = bl * groups
    pass_on_at = (5 * steps) // 8

    def body(q_ref, k_ref, v_ref, *rest):
        g = pl.program_id(0) * groups + pl.program_id(1)
        if ride is None:
            o_ref, l_ref, stop_ref = rest
        else:
            ride_ref, o_ref, l_ref, stop_ref, gathered_ref, *sems = rest
            start, forward, finish = _gather_steps(ride_ref, gathered_ref, *sems)
            pl.when(g == 0)(start)
            pl.when(g == pass_on_at)(forward)
        first_head = _first_head_lanes()
        u_after = _tri(kb, "gt").astype(BF16)

        def qblock(i, _):
            rows = pl.ds(pl.multiple_of(i * qb, qb), qb)
            qs = q_ref[rows, :] * (HEAD_DIM ** -0.5)
            qh = [_per_head(qs[:, pair_of(h)], first_head)[h % 2] for h in range(nh)]

            def live(carry):
                return jnp.logical_and(carry[0] > 0, carry[3] > 0)

            def window(carry):
                hi, accs, cs, _ = carry
                lo = pl.multiple_of(jnp.maximum(hi - win, 0), kb)
                cols = pl.ds(lo, win)
                kv, vv = k_ref[cols, :], v_ref[cols, :]
                valid = _sb_valid(i * qb, lo, lo, hi, qb, win)
                accs, cs = list(accs), list(cs)
                for h in range(nh):
                    z = _mm_nt(qh[h], kv[:, pair_of(h)])
                    sp = _sb_softplus(z)
                    spm = jnp.where(valid, sp, 0.0)
                    after, cs[h] = _running_sums(spm, cs[h], u_after, kb, reverse=True)
                    w = jnp.where(valid, jnp.exp(z - sp - after), 0.0)
                    accs[h] = accs[h] + _mm(w.astype(BF16), vv[:, pair_of(h)])
                alive = functools.reduce(jnp.minimum, [jnp.min(c) for c in cs]) < -SB_DEAD
                return lo, tuple(accs), tuple(cs), alive.astype(jnp.int32)

            zero_acc, zero_c = jnp.zeros((qb, _PAIR), F32), jnp.zeros((qb, 1), F32)
            lo, accs, cs, _ = lax.while_loop(
                live, window, ((i + 1) * qb, (zero_acc,) * nh, (zero_c,) * nh, jnp.int32(1)))
            for pp in range(ATT_PAIRS):
                o_ref[rows, pp * _PAIR:(pp + 1) * _PAIR] = jnp.where(
                    first_head, accs[2 * pp], accs[2 * pp + 1]).astype(BF16)
            for h in range(nh):
                l_ref[h, rows, :] = cs[h]
            stop_ref[g, i] = lo
            return 0

        lax.fori_loop(0, nq, qblock, 0)
        if ride is not None:
            pl.when(g == steps - 1)(finish)

    col = lambda off: pl.BlockSpec((seq, width), lambda b, p: (b, off + p))
    in_specs = [col(0), col(groups), col(2 * groups)]
    out_specs = [col(0), pl.BlockSpec((None, nh, seq, 1), lambda b, p: (b, p, 0, 0)), pl.BlockSpec(memory_space=pltpu.SMEM)]
    out_shape = [jax.ShapeDtypeStruct((bl * seq, WIDTH), BF16), jax.ShapeDtypeStruct((bl, HEADS, seq, 1), F32),
                 jax.ShapeDtypeStruct((bl * groups, nq), jnp.int32)]
    if ride is None:
        return pl.pallas_call(body, name="sb_fwd", grid=(bl, groups), in_specs=in_specs, out_specs=out_specs,
                              out_shape=out_shape, compiler_params=_params(("arbitrary", "arbitrary")))(qkv, qkv, qkv)
    return pl.pallas_call(
        body, name="sb_fwd", grid=(bl, groups), in_specs=in_specs + [_ANY], out_specs=out_specs + [_ANY],
        out_shape=out_shape + [jax.ShapeDtypeStruct((N_DEV,) + ride.shape, ride.dtype)], scratch_shapes=_GATHER_SEMS,
        compiler_params=_params(("arbitrary", "arbitrary")),
    )(qkv, qkv, qkv, ride)


def _sb_bwd(qkv, do, lsum, stop, bl, seq):
    qb, win, kb = min(ATT_Q, seq), min(ATT_WINDOW, seq), ATT_ALIGN
    nq = seq // qb
    nh, width, groups = 2 * ATT_PAIRS, ATT_PAIRS * _PAIR, _PAIRS // ATT_PAIRS
    pair_of = lambda h: slice((h // 2) * _PAIR, (h // 2 + 1) * _PAIR)

    def body(stop_ref, q_ref, k_ref, v_ref, do_ref, l_ref, dq_ref, dk_ref, dv_ref, dk_acc, dv_acc):
        g = pl.program_id(0) * groups + pl.program_id(1)
        first_head = _first_head_lanes()
        u_incl = _tri(kb, "le").astype(BF16)
        u_excl = _tri(kb, "lt").astype(BF16)
        dk_acc[...] = jnp.zeros_like(dk_acc)
        dv_acc[...] = jnp.zeros_like(dv_acc)

        def qblock(i, _):
            rows = pl.ds(pl.multiple_of(i * qb, qb), qb)
            qv = q_ref[rows, :]
            qs = qv * (HEAD_DIM ** -0.5)
            dob = do_ref[rows, :]
            qh = [_per_head(qs[:, pair_of(h)], first_head)[h % 2] for h in range(nh)]
            doh = [_per_head(dob[:, pair_of(h)], first_head)[h % 2] for h in range(nh)]
            ltot = [l_ref[h, rows, :] for h in range(nh)]

            first = (jnp.clip(stop_ref[g, i], 0, i * qb) // ATT_ALIGN) * ATT_ALIGN

            def window(n, carry):
                dqs, ps, es = (list(t) for t in carry)
                start = first + n * win
                lo = pl.multiple_of(jnp.minimum(start, seq - win), kb)
                cols = pl.ds(lo, win)
                kv, vv = k_ref[cols, :], v_ref[cols, :]
                valid = _sb_valid(i * qb, lo, start, seq, qb, win)
                dks, dvs = [], []
                for h in range(nh):
                    kp, vp = kv[:, pair_of(h)], vv[:, pair_of(h)]
                    z = _mm_nt(qh[h], kp)
                    sp = _sb_softplus(z)
                    spm = jnp.where(valid, sp, 0.0)
                    upto, ps[h] = _running_sums(spm, ps[h], u_incl, kb)
                    w = jnp.where(valid, jnp.exp(z - sp - (ltot[h] - upto)), 0.0)
                    e = _mm_nt(doh[h], vp) * w
                    dlf, es[h] = _running_sums(e, es[h], u_excl, kb)
                    sig = jnp.exp(z - sp)
                    dz = jnp.where(valid, e * (1.0 - sig) - dlf * sig, 0.0) * (HEAD_DIM ** -0.5)
                    dzb = dz.astype(BF16)
                    dvs.append(_mm_tn(w.astype(BF16), dob[:, pair_of(h)]))
                    dks.append(_mm_tn(dzb, qv[:, pair_of(h)]))
                    dqs[h] = dqs[h] + _mm(dzb, kp)
                for pp in range(ATT_PAIRS):
                    lanes = slice(pp * _PAIR, (pp + 1) * _PAIR)
                    dv_acc[cols, lanes] += jnp.where(first_head, dvs[2 * pp], dvs[2 * pp + 1])
                    dk_acc[cols, lanes] += jnp.where(first_head, dks[2 * pp], dks[2 * pp + 1])
                return tuple(dqs), tuple(ps), tuple(es)

            zero_q, zero_c = jnp.zeros((qb, _PAIR), F32), jnp.zeros((qb, 1), F32)
            dqs, _, _ = lax.fori_loop(0, ((i + 1) * qb - first + win - 1) // win, window,
                                      ((zero_q,) * nh, (zero_c,) * nh, (zero_c,) * nh))
            for pp in range(ATT_PAIRS):
                dq_ref[rows, pp * _PAIR:(pp + 1) * _PAIR] = jnp.where(
                    first_head, dqs[2 * pp], dqs[2 * pp + 1]).astype(BF16)
            return 0

        lax.fori_loop(0, nq, qblock, 0)
        dk_ref[...] = dk_acc[...].astype(BF16)
        dv_ref[...] = dv_acc[...].astype(BF16)

    col = lambda off: pl.BlockSpec((seq, width), lambda b, p, stop_ref: (b, off + p))
    return pl.pallas_call(
        body, name="sb_bwd",
        grid_spec=pltpu.PrefetchScalarGridSpec(
            num_scalar_prefetch=1, grid=(bl, groups),
            in_specs=[col(0), col(groups), col(2 * groups), col(0),
                      pl.BlockSpec((None, nh, seq, 1), lambda b, p, stop_ref: (b, p, 0, 0))],
            out_specs=[col(0), col(0), col(0)],
            scratch_shapes=[pltpu.VMEM((seq, width), F32), pltpu.VMEM((seq, width), F32)]),
        out_shape=[jax.ShapeDtypeStruct((bl * seq, WIDTH), BF16)] * 3,
        compiler_params=_params(("parallel", "parallel")),
    )(stop, qkv, qkv, qkv, do, lsum)


@jax.custom_vjp
def _lora_mm(x, w):
    return _mm(x.astype(BF16), w.astype(BF16))


_lora_mm.defvjp(
    lambda x, w: (_mm(x.astype(BF16), w.astype(BF16)), (x, w)),
    lambda res, ct: (_mm_nt(ct.astype(BF16), res[1].astype(BF16)), _mm_tn(res[0].astype(BF16), ct.astype(BF16))))


def _rw_prep_math(p, ps, mu, w0, w_up, a0, a_up, g_up, k_k, k_a):
    pm = p + (ps - p) * mu
    r, k, v = pm[:, :WIDTH], pm[:, WIDTH:2 * WIDTH], pm[:, 2 * WIDTH:3 * WIDTH]
    o = 3 * WIDTH
    xw, xa, xg = pm[:, o:o + W_LORA], pm[:, o + W_LORA:o + W_LORA + A_LORA], pm[:, o + W_LORA + A_LORA:]
    w_raw = w0 + _lora_mm(jnp.tanh(xw), w_up)
    lw = -jnp.exp(-_softplus(-w_raw) - 0.5)
    a = jax.nn.sigmoid(a0 + _lora_mm(xa, a_up))
    g = _lora_mm(jax.nn.sigmoid(xg), g_up)
    kk = k * k_k
    k2 = k * (1.0 + (a - 1.0) * k_a)
    return r, lw, k2, v, kk, a, g


def _shift_down(p, first_row):
    row = lax.broadcasted_iota(jnp.int32, p.shape, 0)
    return jnp.where(row == 0, first_row, pltpu.roll(p, 1, 0))


def _shift_up(p, last_row):
    row = lax.broadcasted_iota(jnp.int32, p.shape, 0)
    return jnp.where(row == p.shape[0] - 1, last_row, pltpu.roll(p, p.shape[0] - 1, 0))


_PREP_PARAM_SHAPES = [(1, RW_COLS), (1, WIDTH), (W_LORA, WIDTH), (1, WIDTH), (A_LORA, WIDTH), (G_LORA, WIDTH),
                      (1, WIDTH), (1, WIDTH)]


def _prev_rows_spec(tm):
    return pl.BlockSpec((8, RW_COLS), lambda i: (jnp.maximum(i * (tm // 8) - 1, 0), 0))


def _head_spec(tm, seq, tile_of=lambda i: i):
    per_seq = seq // tm
    return pl.BlockSpec((None, HEADS, tm, HEAD_DIM),
                        lambda i: (tile_of(i) // per_seq, 0, tile_of(i) % per_seq, 0))


def _split_heads(val, ref):
    for h in range(HEADS):
        ref[h] = val[:, h * HEAD_DIM:(h + 1) * HEAD_DIM]


def _join_heads(ref):
    return jnp.concatenate([ref[h] for h in range(HEADS)], axis=1)


def _rw_prep_fwd(prw, params, seq, tm):
    T = prw.shape[0]

    def body(p_ref, prev_ref, *rest):
        prm = [r_[...] for r_ in rest[:8]]
        outs = rest[8:]
        i = pl.program_id(0)
        first = jnp.where((i * tm) % seq == 0, 0.0, prev_ref[7:8, :])
        p = p_ref[...]
        vals = _rw_prep_math(p, _shift_down(p, first), *prm)
        for o_ref, val in zip(outs[:6], vals[:6]):
            _split_heads(val, o_ref)
        outs[6][...] = vals[6]

    by_head = jax.ShapeDtypeStruct((T // seq, HEADS, seq, HEAD_DIM), F32)
    return pl.pallas_call(
        body, name="rw_prep_fwd", grid=(T // tm,),
        in_specs=[_row_spec(tm, RW_COLS), _prev_rows_spec(tm)] + [_const_spec(s) for s in _PREP_PARAM_SHAPES],
        out_specs=[_head_spec(tm, seq)] * 6 + [_row_spec(tm, WIDTH)],
        out_shape=[by_head] * 6 + [jax.ShapeDtypeStruct((T, WIDTH), F32)],
        compiler_params=_params(("parallel",)),
    )(prw, prw, *params)


def _rw_prep_bwd(prw, params, cts, seq, tm):
    T = prw.shape[0]
    n = T // tm

    def body(p_ref, prev_ref, *rest):
        prm = [r_[...] for r_ in rest[:8]]
        ct = tuple(_join_heads(r_) for r_ in rest[8:14]) + (rest[14][...],)
        dp_ref = rest[15]
        dprm_refs = rest[16:24]
        carry = rest[24]
        step = pl.program_id(0)
        i = n - 1 - step
        first = jnp.where((i * tm) % seq == 0, 0.0, prev_ref[7:8, :])
        p = p_ref[...]
        _, vjp = jax.vjp(_rw_prep_math, p, _shift_down(p, first), *prm)
        grads = vjp(ct)
        dp, dps = grads[0], grads[1]
        nxt = jnp.where(jnp.logical_or(step == 0, ((i + 1) * tm) % seq == 0), 0.0, carry[0:1, :])
        dp_ref[...] = (dp + _shift_up(dps, nxt)).astype(BF16)
        carry[...] = dps[0:8, :]
        for ref, gval in zip(dprm_refs, grads[2:]):
            _acc_out(ref, gval, step == 0)

    rev = lambda w: pl.BlockSpec((tm, w), lambda s: (n - 1 - s, 0))
    prev = pl.BlockSpec((8, RW_COLS), lambda s: (jnp.maximum((n - 1 - s) * (tm // 8) - 1, 0), 0))
    return pl.pallas_call(
        body, name="rw_prep_bwd", grid=(n,),
        in_specs=([rev(RW_COLS), prev] + [_const_spec(s) for s in _PREP_PARAM_SHAPES]
                  + [_head_spec(tm, seq, lambda s: n - 1 - s)] * 6 + [rev(WIDTH)]),
        out_specs=[rev(RW_COLS)] + [pl.BlockSpec(s, lambda s_: (0, 0)) for s in _PREP_PARAM_SHAPES],
        out_shape=[jax.ShapeDtypeStruct((T, RW_COLS), BF16)] + [jax.ShapeDtypeStruct(s, F32) for s in _PREP_PARAM_SHAPES],
        scratch_shapes=[pltpu.VMEM((8, RW_COLS), F32)],
        compiler_params=_params(("arbitrary",)),
    )(prw, prw, *params, *cts)


def _make_bmm(passes):
    def raw(dn, a, b):
        d = lambda x, y: lax.dot_general(x, y, dn, preferred_element_type=F32)
        ah = a.astype(BF16)
        bh = b.astype(BF16)
        if passes == 1:
            return d(ah, bh)
        al = (a - ah.astype(F32)).astype(BF16)
        bl = (b - bh.astype(F32)).astype(BF16)
        return d(ah, bh) + (d(ah, bl) + d(al, bh))

    dn_nn = (((2,), (1,)), ((0,), (0,)))
    dn_nt = (((2,), (2,)), ((0,), (0,)))
    dn_tn = (((1,), (1,)), ((0,), (0,)))

    @jax.custom_vjp
    def nn(a, b):
        return raw(dn_nn, a, b)

    @jax.custom_vjp
    def nt(a, b):
        return raw(dn_nt, a, b)

    @jax.custom_vjp
    def tn(a, b):
        return raw(dn_tn, a, b)

    nn.defvjp(lambda a, b: (raw(dn_nn, a, b), (a, b)), lambda res, ct: (nt(ct, res[1]), tn(res[0], ct)))
    nt.defvjp(lambda a, b: (raw(dn_nt, a, b), (a, b)), lambda res, ct: (nn(ct, res[1]), tn(ct, res[0])))
    tn.defvjp(lambda a, b: (raw(dn_tn, a, b), (a, b)), lambda res, ct: (nt(res[1], ct), nn(res[0], ct)))

    def unit_lower_inverse(m):
        n = m.shape[-1]
        row = lax.broadcasted_iota(jnp.int32, (n, n), 0)
        col = lax.broadcasted_iota(jnp.int32, (n, n), 1)
        m16 = ((row // 16) == (col // 16)).astype(F32)
        m32 = ((row // 32) == (col // 32)).astype(F32)
        a1 = m * m16
        a2 = nn(a1, a1)
        a4 = nn(a2, a2)
        a8 = nn(a4, a4)
        inv = (row == col).astype(F32) - a1
        inv = inv + nn(inv, a2)
        inv = inv + nn(inv, a4)
        inv = inv + nn(inv, a8)
        inv = inv - nn(nn(inv, m * (m32 - m16)), inv)
        return inv - nn(nn(inv, m * (1.0 - m32)), inv)

    @jax.custom_vjp
    def inverse(m):
        return unit_lower_inverse(m)

    def inverse_fwd(m):
        inv = unit_lower_inverse(m)
        return inv, inv

    inverse.defvjp(inverse_fwd, lambda inv, ct: (-nt(tn(inv, ct), inv),))
    return nn, nt, tn, inverse


def _wkv_chunk(s0, r, lw, k, v, kk, a, lnw, lnb, rk):
    nn, nt, tn, inverse = _make_bmm(SCAN_PASSES)
    G, L, N = r.shape
    rep = lambda t: jnp.broadcast_to(t[None], (G // HEADS, HEADS, 1, N)).reshape(G, 1, N)
    kap = kk * lax.rsqrt(jnp.maximum(jnp.sum(kk * kk, axis=-1, keepdims=True), 1e-24))
    b = a * kap
    row = lax.broadcasted_iota(jnp.int32, (L, L), 0)
    col = lax.broadcasted_iota(jnp.int32, (L, L), 1)
    low_incl = (col <= row).astype(F32)
    low_strict = (col < row).astype(F32)
    c = _make_bmm(3)[0](jnp.broadcast_to(low_incl[None], (G, L, L)), lw)
    c_all = jnp.sum(lw, axis=1, keepdims=True)
    g_inv = jnp.exp(-c)
    kap_t = kap * jnp.exp(c - lw)
    b_t = b * g_inv
    k_t = k * g_inv
    r_t = r * jnp.exp(c)
    g_all = jnp.exp(c_all)
    m_b = nt(kap_t, b_t) * low_strict
    m_k = nt(kap_t, k_t) * low_strict
    n_b = nt(r_t, b_t) * low_incl
    n_k = nt(r_t, k_t) * low_incl
    rhs = -(nt(kap_t, s0) + nn(m_k, v))
    sa = nn(inverse(m_b), rhs)
    y = nt(r_t, s0) + nn(n_b, sa) + nn(n_k, v)
    s1 = s0 * g_all + tn(sa, b_t * g_all) + tn(v, k_t * g_all)
    mean = jnp.mean(y, axis=-1, keepdims=True)
    yc = y - mean
    var = jnp.mean(yc * yc, axis=-1, keepdims=True)
    out = yc * lax.rsqrt(var + GN_EPS) * rep(lnw) + rep(lnb)
    out = out + jnp.sum(r * k * rep(rk), axis=-1, keepdims=True) * v
    return out, s1


def _scan_heads_per_step(total_heads, seqs_wanted):
    n_seq = total_heads // HEADS
    return HEADS * max(d for d in range(1, seqs_wanted + 1) if n_seq % d == 0)


def _wkv_fwd(seqs, lnw, lnb, rk):
    G, S, N = seqs[0].shape
    L = SCAN_CHUNK
    nc = S // L

    def body(*refs):
        ins = [r_[...] for r_ in refs[:6]]
        prm = [r_[...] for r_ in refs[6:9]]
        out_ref, st_ref, state = refs[9], refs[10], refs[11]

        @pl.when(pl.program_id(1) == 0)
        def _():
            state[...] = jnp.zeros_like(state)

        s0 = state[...]
        st_ref[...] = s0
        out, s1 = _wkv_chunk(s0, *ins, *prm)
        out_ref[...] = out
        state[...] = s1

    gb = _scan_heads_per_step(G, SCAN_SEQS_FWD)
    blk = pl.BlockSpec((gb, L, N), lambda b, i: (b, i, 0))
    pspec = _const_spec((HEADS, 1, N))
    return pl.pallas_call(
        body, name="wkv_fwd", grid=(G // gb, nc), in_specs=[blk] * 6 + [pspec] * 3,
        out_specs=[blk, pl.BlockSpec((None, gb, N, N), lambda b, i: (i, b, 0, 0))],
        out_shape=[jax.ShapeDtypeStruct((G, S, N), F32), jax.ShapeDtypeStruct((nc, G, N, N), F32)],
        scratch_shapes=[pltpu.VMEM((gb, N, N), F32)],
        compiler_params=_params(("parallel", "arbitrary")),
    )(*seqs, lnw, lnb, rk)


def _wkv_bwd(seqs, states, dout, lnw, lnb, rk):
    G, S, N = seqs[0].shape
    L = SCAN_CHUNK
    nc = S // L

    def body(*refs):
        ins = [r_[...] for r_ in refs[:6]]
        s0 = refs[6][...]
        ct_out = refs[7][...]
        prm = [r_[...] for r_ in refs[8:11]]
        d_refs = refs[11:17]
        dprm_refs = refs[17:20]
        dstate = refs[20]
        step = pl.program_id(1)

        @pl.when(step == 0)
        def _():
            dstate[...] = jnp.zeros_like(dstate)

        _, vjp = jax.vjp(_wkv_chunk, s0, *ins, *prm)
        grads = vjp((ct_out, dstate[...]))
        dstate[...] = grads[0]
        for ref, gval in zip(d_refs, grads[1:7]):
            ref[...] = gval
        for ref, gval in zip(dprm_refs, grads[7:]):
            _acc_out(ref, gval, jnp.logical_and(step == 0, pl.program_id(0) == 0))

    gb = _scan_heads_per_step(G, SCAN_SEQS_BWD)
    blk = pl.BlockSpec((gb, L, N), lambda b, s: (b, nc - 1 - s, 0))
    pspec = _const_spec((HEADS, 1, N))
    pout = pl.BlockSpec((HEADS, 1, N), lambda b, s: (0, 0, 0))
    return pl.pallas_call(
        body, name="wkv_bwd", grid=(G // gb, nc),
        in_specs=[blk] * 6 + [pl.BlockSpec((None, gb, N, N), lambda b, s: (nc - 1 - s, b, 0, 0)), blk] + [pspec] * 3,
        out_specs=[blk] * 6 + [pout] * 3,
        out_shape=[jax.ShapeDtypeStruct((G, S, N), F32)] * 6 + [jax.ShapeDtypeStruct((HEADS, 1, N), F32)] * 3,
        scratch_shapes=[pltpu.VMEM((gb, N, N), F32)],
        compiler_params=_params(("arbitrary", "arbitrary")),
    )(*seqs, states, dout, lnw, lnb, rk)


def _merge_math(o_sb, rw_out, g_rw, gates, w_sb, w_rw, w_o):
    o_rw = (rw_out * g_rw).astype(BF16)
    a = _mm(o_sb, w_sb)
    b = _mm(o_rw, w_rw)
    g1, g2 = gates[:, :D_MODEL], gates[:, D_MODEL:]
    merged = (g1 * a + g2 * b).astype(BF16)
    return o_rw, a, b, g1, g2, merged, _mm(merged, w_o)


def _merge_fwd(x2, o_sb, rw_out, g_rw, gates, w_sb, w_rw, w_o, g_post, seq, tm):
    T = x2.shape[0]

    def body(x_ref, osb_ref, rw_ref, g_ref, gate_ref, wsb_ref, wrw_ref, wo_ref, gp_ref, x1_ref):
        z = _merge_math(osb_ref[...], _join_heads(rw_ref), g_ref[...], gate_ref[...], wsb_ref[...], wrw_ref[...], wo_ref[...])[-1]
        x1_ref[...] = x_ref[...] + _rms_fwd(z, gp_ref[...])[0]

    return pl.pallas_call(
        body, name="merge_fwd", grid=(T // tm,),
        in_specs=[_row_spec(tm, D_MODEL), _row_spec(tm, WIDTH), _head_spec(tm, seq), _row_spec(tm, WIDTH),
                  _row_spec(tm, GATE_COLS), _const_spec((WIDTH, D_MODEL)), _const_spec((WIDTH, D_MODEL)),
                  _const_spec((D_MODEL, D_MODEL)), _const_spec((1, D_MODEL))],
        out_specs=_row_spec(tm, D_MODEL),
        out_shape=jax.ShapeDtypeStruct((T, D_MODEL), F32),
        compiler_params=_params(("parallel",)),
    )(x2, o_sb, rw_out, g_rw, gates, w_sb, w_rw, w_o, g_post)


def _merge_bwd(dx1, o_sb, rw_out, g_rw, gates, w_sb, w_rw, w_o, g_post, seq, tm):
    T = dx1.shape[0]

    def body(dx1_ref, osb_ref, rw_ref, g_ref, gate_ref, wsb_ref, wrw_ref, wo_ref, gp_ref,
             orw_o, mrg_o, dz_o, da_o, db_o, dgate_o, dosb_o, drw_o, dg_o, dgp_o, dbg_o):
        rw_out_v, g_rw_v = _join_heads(rw_ref), g_ref[...]
        w_sb_v, w_rw_v, w_o_v = wsb_ref[...], wrw_ref[...], wo_ref[...]
        o_rw, a, b, g1, g2, merged, z = _merge_math(osb_ref[...], rw_out_v, g_rw_v, gate_ref[...], w_sb_v, w_rw_v, w_o_v)
        gain = gp_ref[...]
        _, zn, rstd = _rms_fwd(z, gain)
        dz, dgain = _rms_bwd(dx1_ref[...], zn, rstd, gain)
        dzb = dz.astype(BF16)
        dm = _mm_nt(dzb, w_o_v)
        dab = (dm * g1).astype(BF16)
        dbb = (dm * g2).astype(BF16)
        dgate = jnp.concatenate([dm * a * g1 * (1.0 - g1), dm * b * g2 * (1.0 - g2)], axis=1)
        do_rw = _mm_nt(dbb, w_rw_v)
        orw_o[...] = o_rw
        mrg_o[...] = merged
        dz_o[...] = dzb
        da_o[...] = dab
        db_o[...] = dbb
        dgate_o[...] = dgate.astype(BF16)
        dosb_o[...] = _mm_nt(dab, w_sb_v).astype(BF16)
        _split_heads(do_rw * g_rw_v, drw_o)
        dg_o[...] = do_rw * rw_out_v
        first = pl.program_id(0) == 0
        _acc_out(dgp_o, dgain, first)
        _acc_out(dbg_o, jnp.sum(dgate, axis=0, keepdims=True), first)

    acc = lambda n: pl.BlockSpec((1, n), lambda i: (0, 0))
    sd = jax.ShapeDtypeStruct
    return pl.pallas_call(
        body, name="merge_bwd", grid=(T // tm,),
        in_specs=[_row_spec(tm, D_MODEL), _row_spec(tm, WIDTH), _head_spec(tm, seq), _row_spec(tm, WIDTH),
                  _row_spec(tm, GATE_COLS), _const_spec((WIDTH, D_MODEL)), _const_spec((WIDTH, D_MODEL)),
                  _const_spec((D_MODEL, D_MODEL)), _const_spec((1, D_MODEL))],
        out_specs=[_row_spec(tm, WIDTH), _row_spec(tm, D_MODEL), _row_spec(tm, D_MODEL), _row_spec(tm, D_MODEL),
                   _row_spec(tm, D_MODEL), _row_spec(tm, GATE_COLS), _row_spec(tm, WIDTH), _head_spec(tm, seq),
                   _row_spec(tm, WIDTH), acc(D_MODEL), acc(GATE_COLS)],
        out_shape=[sd((T, WIDTH), BF16), sd((T, D_MODEL), BF16), sd((T, D_MODEL), BF16), sd((T, D_MODEL), BF16),
                   sd((T, D_MODEL), BF16), sd((T, GATE_COLS), BF16), sd((T, WIDTH), BF16),
                   sd((T // seq, HEADS, seq, HEAD_DIM), F32), sd((T, WIDTH), F32), sd((1, D_MODEL), F32),
                   sd((1, GATE_COLS), F32)],
        compiler_params=_params(("arbitrary",)),
    )(dx1, o_sb, rw_out, g_rw, gates, w_sb, w_rw, w_o, g_post)


def _ffn(x1, target, g_pre, g_post, w_gate, w_up, w_down, tm):
    T = x1.shape[0]

    def body(x1_ref, tgt_ref, gpre_ref, gpost_ref, wg_ref, wu_ref, wd_ref,
             loss_o, dx1_o, h_o, dgate_o, dup_o, act_o, df_o, dgpre_o, dgpost_o):
        x1v = x1_ref[...]
        gpre, gpost = gpre_ref[...], gpost_ref[...]
        wg, wu, wd = wg_ref[...], wu_ref[...], wd_ref[...]
        hn, xn1, rstd1 = _rms_fwd(x1v, gpre)
        h = hn.astype(BF16)
        gate = _mm(h, wg)
        up = _mm(h, wu)
        sg = jax.nn.sigmoid(gate)
        act = (gate * sg * up).astype(BF16)
        f = _mm(act, wd)
        fo, fn, rstd2 = _rms_fwd(f, gpost)
        diff = x1v + fo - tgt_ref[...]
        dy = diff * (1.0 / D_MODEL)
        df, dgpost = _rms_bwd(dy, fn, rstd2, gpost)
        dfb = df.astype(BF16)
        dact = _mm_nt(dfb, wd)
        dup = (dact * gate * sg).astype(BF16)
        dgate = (dact * up * (sg * (1.0 + gate * (1.0 - sg)))).astype(BF16)
        dh = _mm_nt(dgate, wg) + _mm_nt(dup, wu)
        dxn, dgpre = _rms_bwd(dh, xn1, rstd1, gpre)
        dx1_o[...] = dy + dxn
        h_o[...] = h
        dgate_o[...] = dgate
        dup_o[...] = dup
        act_o[...] = act
        df_o[...] = dfb
        first = pl.program_id(0) == 0
        part = jnp.sum(jnp.sum(diff * diff, axis=1, keepdims=True), axis=0, keepdims=True) * (0.5 / D_MODEL)
        _acc_out(loss_o, jnp.broadcast_to(part, (8, 128)), first)
        _acc_out(dgpre_o, dgpre, first)
        _acc_out(dgpost_o, dgpost, first)

    acc = lambda r, n: pl.BlockSpec((r, n), lambda i: (0, 0))
    sd = jax.ShapeDtypeStruct
    return pl.pallas_call(
        body, name="ffn", grid=(T // tm,),
        in_specs=[_row_spec(tm, D_MODEL), _row_spec(tm, D_MODEL), _const_spec((1, D_MODEL)), _const_spec((1, D_MODEL)),
                  _const_spec((D_MODEL, D_FF)), _const_spec((D_MODEL, D_FF)), _const_spec((D_FF, D_MODEL))],
        out_specs=[acc(8, 128), _row_spec(tm, D_MODEL), _row_spec(tm, D_MODEL), _row_spec(tm, D_FF), _row_spec(tm, D_FF),
                   _row_spec(tm, D_FF), _row_spec(tm, D_MODEL), acc(1, D_MODEL), acc(1, D_MODEL)],
        out_shape=[sd((8, 128), F32), sd((T, D_MODEL), F32), sd((T, D_MODEL), BF16), sd((T, D_FF), BF16),
                   sd((T, D_FF), BF16), sd((T, D_FF), BF16), sd((T, D_MODEL), BF16), sd((1, D_MODEL), F32),
                   sd((1, D_MODEL), F32)],
        compiler_params=_params(("arbitrary",)),
    )(x1, target, g_pre, g_post, w_gate, w_up, w_down)


def _local_step(x, target, sm, wt, late=None):
    bl, seq, _ = x.shape
    T = bl * seq
    tm = min(ROW_TILE, T)
    x2 = x.reshape(T, D_MODEL)
    tgt2 = target.reshape(T, D_MODEL)
    h, qkv, prw, gates = _in_proj_fwd(x2, sm["norm_mix_pre"], wt["w_in"], sm["b_gate"], tm)
    if late is None:
        o_sb, lsum, sb_stop = _sb_fwd(qkv, bl, seq)
    else:
        o_sb, lsum, sb_stop, gathered = _sb_fwd(qkv, bl, seq, late)
        wt = {**wt, **_unpack_gathered(gathered, slice(_EARLY, None))}
    prep_params = [sm["mu_rw"], sm["w0"], wt["w_up"].astype(F32), sm["a0"], wt["a_up"].astype(F32),
                   wt["g_up"].astype(F32), sm["k_k"], sm["k_a"]]
    prep = _rw_prep_fwd(prw, prep_params, seq, tm)
    by_head = lambda t: t.reshape(bl, HEADS, seq, HEAD_DIM)
    seqs = [t.reshape(bl * HEADS, seq, HEAD_DIM) for t in prep[:6]]
    g_rw = prep[6]
    lnw, lnb, rk = (sm[n].reshape(HEADS, 1, HEAD_DIM) for n in ("lnx_w", "lnx_b", "r_k"))
    rw_out_h, states = _wkv_fwd(seqs, lnw, lnb, rk)
    rw_out = by_head(rw_out_h)
    x1 = _merge_fwd(x2, o_sb, rw_out, g_rw, gates, wt["w_sb_out"], wt["w_rw_out"], wt["w_o"], sm["norm_mix_post"],
                    seq, tm)
    (loss_part, dx1, h2, dffg, dffu, act, dff, d_nfpre, d_nfpost) = _ffn(
        x1, tgt2, sm["norm_ffn_pre"], sm["norm_ffn_post"], wt["w_ffn_gate"], wt["w_ffn_up"], wt["w_ffn_down"], tm)
    (o_rw, merged, dz, da, db, dgate, do_sb, d_rw_out, d_g_rw, d_npost, d_bgate) = _merge_bwd(
        dx1, o_sb, rw_out, g_rw, gates, wt["w_sb_out"], wt["w_rw_out"], wt["w_o"], sm["norm_mix_post"], seq, tm)
    dqkv = jnp.concatenate(_sb_bwd(qkv, do_sb, lsum, sb_stop, bl, seq), axis=1)
    wkv_g = _wkv_bwd(seqs, states, d_rw_out.reshape(bl * HEADS, seq, HEAD_DIM), lnw, lnb, rk)
    cts = [by_head(t) for t in wkv_g[:6]] + [d_g_rw]
    prep_g = _rw_prep_bwd(prw, prep_params, cts, seq, tm)
    dprw = prep_g[0]
    d_mu, d_w0, d_wup, d_a0, d_aup, d_gup, d_kk, d_ka = prep_g[1:]
    grad_x, d_npre = _in_proj_bwd(x2, sm["norm_mix_pre"], dx1, dqkv, dprw, dgate, wt["w_in"], tm)
    gw = {
        "w_in": jnp.concatenate([_grad_w(h, dqkv, "gw_in_qkv"), _grad_w(h, dprw, "gw_in_rw"), _grad_w(h, dgate, "gw_in_gate")], axis=1),
        "w_up": d_wup, "a_up": d_aup, "g_up": d_gup,
        "w_sb_out": _grad_w(o_sb, da, "gw_sb_out"), "w_rw_out": _grad_w(o_rw, db, "gw_rw_out"),
        "w_o": _grad_w(merged, dz, "gw_o"),
        "w_ffn_gate": _grad_w(h2, dffg, "gw_ffn_gate"), "w_ffn_up": _grad_w(h2, dffu, "gw_ffn_up"),
        "w_ffn_down": _grad_w(act, dff, "gw_ffn_down"),
    }
    gs = {
        "norm_mix_pre": d_npre, "b_gate": d_bgate, "mu_rw": d_mu, "w0": d_w0, "a0": d_a0, "k_k": d_kk, "k_a": d_ka,
        "r_k": wkv_g[8].reshape(1, WIDTH), "lnx_w": wkv_g[6].reshape(1, WIDTH), "lnx_b": wkv_g[7].reshape(1, WIDTH),
        "norm_mix_post": d_npost, "norm_ffn_pre": d_nfpre, "norm_ffn_post": d_nfpost,
    }
    return loss_part, grad_x.reshape(x.shape), gw, gs


_SHARDED = [("w_in", 1, (D_MODEL, (SB_COLS + RW_COLS + GATE_COLS) // N_DEV)), ("w_up", 1, (W_LORA, WIDTH // N_DEV)),
            ("a_up", 1, (A_LORA, WIDTH // N_DEV)), ("g_up", 1, (G_LORA, WIDTH // N_DEV)),
            ("w_sb_out", 1, (WIDTH, D_MODEL // N_DEV)), ("w_rw_out", 1, (WIDTH, D_MODEL // N_DEV)),
            ("w_o", 0, (D_MODEL // N_DEV, D_MODEL)), ("w_ffn_gate", 1, (D_MODEL, D_FF // N_DEV)),
            ("w_ffn_up", 1, (D_MODEL, D_FF // N_DEV)), ("w_ffn_down", 0, (D_FF // N_DEV, D_MODEL))]
_LANES = 128
_PACK_ROWS = [s[0] * s[1] // _LANES for _, _, s in _SHARDED]
_PACK_TOTAL = sum(_PACK_ROWS)
_SMALL = [("norm_mix_pre", D_MODEL), ("b_gate", GATE_COLS), ("mu_rw", RW_COLS), ("w0", WIDTH), ("a0", WIDTH),
          ("k_k", WIDTH), ("k_a", WIDTH), ("r_k", WIDTH), ("lnx_w", WIDTH), ("lnx_b", WIDTH),
          ("norm_mix_post", D_MODEL), ("norm_ffn_pre", D_MODEL), ("norm_ffn_post", D_MODEL)]
_SMALL_ROWS = 96


def _pack_shards(shards, dtype):
    return jnp.concatenate([shards[n].astype(dtype).reshape(-1, _LANES) for n, _, _ in _SHARDED], axis=0)


def _unpack_shards(packed):
    out, r0 = {}, 0
    for (n, _, shp), rows in zip(_SHARDED, _PACK_ROWS):
        out[n] = packed[r0:r0 + rows].reshape(shp)
        r0 += rows
    return out


_EARLY = 4
_EARLY_ROWS = sum(_PACK_ROWS[:_EARLY])


def _unpack_gathered(g, which):
    out, r0 = {}, 0
    for (n, axis, shp), rows in zip(_SHARDED[which], _PACK_ROWS[which]):
        blk = g[:, r0:r0 + rows].reshape((N_DEV,) + shp)
        out[n] = blk.reshape(N_DEV * shp[0], shp[1]) if axis == 0 else blk.transpose(1, 0, 2).reshape(shp[0], N_DEV * shp[1])
        r0 += rows
    return out


def _pack_full_grads(gw):
    parts = []
    for n, axis, shp in _SHARDED:
        g = gw[n]
        blk = g.reshape((N_DEV,) + shp) if axis == 0 else g.reshape(shp[0], N_DEV, shp[1]).transpose(1, 0, 2)
        parts.append(blk.reshape(N_DEV, -1, _LANES))
    return jnp.concatenate(parts, axis=1)


def _pack_small(vals, extra=None):
    used = sum(sz for _, sz in _SMALL)
    tail = jnp.zeros((1, _SMALL_ROWS * _LANES - used), F32).at[0, 0].set(extra)
    return jnp.concatenate([vals[n].reshape(1, -1) for n, _ in _SMALL] + [tail], axis=1)


_ANY = pl.BlockSpec(memory_space=pl.ANY)


def _all_gather(block):
    rows, lanes = block.shape

    def body(x_ref, out_ref, send_sems, recv_sems, local_sem):
        start, forward, finish = _gather_steps(x_ref, out_ref, send_sems, recv_sems, local_sem)
        start()
        forward()
        finish()

    return pl.pallas_call(
        body, name="all_gather_weights", in_specs=[_ANY], out_specs=_ANY,
        out_shape=jax.ShapeDtypeStruct((N_DEV, rows, lanes), block.dtype), scratch_shapes=_GATHER_SEMS,
    )(block)


_GATHER_SEMS = [pltpu.SemaphoreType.DMA((7,)), pltpu.SemaphoreType.DMA((7,)), pltpu.SemaphoreType.DMA]


def _gather_steps(x_ref, out_ref, send_sems, recv_sems, local_sem):
    x, y, c = lax.axis_index("x"), lax.axis_index("y"), lax.axis_index("c")
    me, sibling = (x, y, c), (x, y, 1 - c)
    chips = [(1 - x, y), (x, 1 - y), (1 - x, 1 - y)]

    def slot(px, py, pc):
        return out_ref.at[4 * px + 2 * py + pc]

    def copy(k, blk, to, src=None):
        return pltpu.make_async_remote_copy(
            src_ref=slot(*blk) if src is None else src, dst_ref=slot(*blk),
            send_sem=send_sems.at[k], recv_sem=recv_sems.at[k], device_id=to, device_id_type=MESH)

    def first():
        return [copy(0, me, sibling, src=x_ref)] + [copy(1 + j, me, (*chip, c), src=x_ref) for j, chip in enumerate(chips)]

    def passed():
        return [copy(4 + j, (*chip, c), sibling) for j, chip in enumerate(chips)]

    def start():
        pltpu.make_async_copy(x_ref, slot(*me), local_sem).start()
        for cp in first():
            cp.start()

    def forward():
        for j, (chip, cp) in enumerate(zip(chips, passed())):
            copy(1 + j, (*chip, c), me).wait_recv()
            cp.start()

    def finish():
        copy(0, sibling, me).wait_recv()
        for j, chip in enumerate(chips):
            copy(4 + j, (*chip, 1 - c), me).wait_recv()
        for cp in first() + passed():
            cp.wait_send()
        pltpu.make_async_copy(x_ref, slot(*me), local_sem).wait()

    return start, forward, finish


def _exchange_core(pack, small):
    _, _, rows, lanes = pack.shape

    def body(pack_ref, small_ref, got_ref, parts_ref, send_sems, recv_sems, s_send, s_recv, local_sem):
        x, y, c = lax.axis_index("x"), lax.axis_index("y"), lax.axis_index("c")
        sibling = (x, y, 1 - c)
        me = 4 * x + 2 * y + c
        mine = pltpu.make_async_copy(small_ref, parts_ref.at[me], local_sem)
        mine.start()
        big = [pltpu.make_async_remote_copy(
            src_ref=pack_ref.at[1 - c, j], dst_ref=got_ref.at[j], send_sem=send_sems.at[j], recv_sem=recv_sems.at[j],
            device_id=sibling, device_id_type=MESH) for j in range(4)]
        for cp in big:
            cp.start()
        others = [(k, (x ^ (k >> 2), y ^ ((k >> 1) & 1), c ^ (k & 1))) for k in range(1, N_DEV)]
        tiny = [pltpu.make_async_remote_copy(
            src_ref=small_ref, dst_ref=parts_ref.at[me], send_sem=s_send.at[k], recv_sem=s_recv.at[k],
            device_id=to, device_id_type=MESH) for k, to in others]
        for cp in tiny:
            cp.start()
        for cp in big:
            cp.wait_recv()
        for (k, (px, py, pc)), cp in zip(others, tiny):
            pltpu.make_async_remote_copy(
                src_ref=small_ref, dst_ref=parts_ref.at[4 * px + 2 * py + pc], send_sem=s_send.at[k],
                recv_sem=s_recv.at[k], device_id=(px, py, pc), device_id_type=MESH).wait_recv()
        for cp in big + tiny:
            cp.wait_send()
        mine.wait()

    return pl.pallas_call(
        body, name="exchange_core", in_specs=[_ANY, _ANY], out_specs=[_ANY, _ANY],
        out_shape=[jax.ShapeDtypeStruct((4, rows, lanes), F32), jax.ShapeDtypeStruct((N_DEV,) + small.shape, F32)],
        scratch_shapes=[pltpu.SemaphoreType.DMA((4,)), pltpu.SemaphoreType.DMA((4,)), pltpu.SemaphoreType.DMA((N_DEV,)),
                        pltpu.SemaphoreType.DMA((N_DEV,)), pltpu.SemaphoreType.DMA],
    )(pack, small)


def _add_core_parts(pack, got, core):
    _, _, rows, lanes = pack.shape
    tr = 2000

    def body(core_ref, a_ref, b_ref, o_ref):
        o_ref[...] = (a_ref[...] + b_ref[...]).astype(BF16)

    return pl.pallas_call(
        body, name="add_core_parts",
        grid_spec=pltpu.PrefetchScalarGridSpec(
            num_scalar_prefetch=1, grid=(4, rows // tr),
            in_specs=[pl.BlockSpec((None, None, tr, lanes), lambda j, i, core_ref: (core_ref[0], j, i, 0)),
                      pl.BlockSpec((None, tr, lanes), lambda j, i, core_ref: (j, i, 0))],
            out_specs=pl.BlockSpec((None, tr, lanes), lambda j, i, core_ref: (j, i, 0))),
        out_shape=jax.ShapeDtypeStruct((4, rows, lanes), BF16),
        compiler_params=_params(("parallel", "parallel")),
    )(core, pack, got)


def _exchange_chips(chip_sums):
    _, rows, lanes = chip_sums.shape

    def body(src_ref, got_ref, send_sems, recv_sems):
        x, y, c = lax.axis_index("x"), lax.axis_index("y"), lax.axis_index("c")
        flips = [(1, 0), (0, 1), (1, 1)]
        copies = []
        for k, (fx, fy) in enumerate(flips):
            px, py = x ^ fx, y ^ fy
            copies.append(pltpu.make_async_remote_copy(
                src_ref=src_ref.at[2 * px + py], dst_ref=got_ref.at[k], send_sem=send_sems.at[k],
                recv_sem=recv_sems.at[k], device_id=(px, py, c), device_id_type=MESH))
        for cp in copies:
            cp.start()
        for cp in copies:
            cp.wait_recv()
        for cp in copies:
            cp.wait_send()

    return pl.pallas_call(
        body, name="exchange_chips", in_specs=[_ANY], out_specs=_ANY,
        out_shape=jax.ShapeDtypeStruct((3, rows, lanes), chip_sums.dtype),
        scratch_shapes=[pltpu.SemaphoreType.DMA((3,)), pltpu.SemaphoreType.DMA((3,))],
    )(chip_sums)


def _sum_chip_parts(chip_sums, got, chip):
    _, rows, lanes = chip_sums.shape
    tr = 2000

    def body(chip_ref, own_ref, got_ref, o_ref):
        f32 = lambda t: t.astype(F32)
        o_ref[...] = ((f32(own_ref[...]) + f32(got_ref[0])) + f32(got_ref[1])) + f32(got_ref[2])

    return pl.pallas_call(
        body, name="sum_chip_parts",
        grid_spec=pltpu.PrefetchScalarGridSpec(
            num_scalar_prefetch=1, grid=(rows // tr,),
            in_specs=[pl.BlockSpec((None, tr, lanes), lambda i, chip_ref: (chip_ref[0], i, 0)),
                      pl.BlockSpec((3, tr, lanes), lambda i, chip_ref: (0, i, 0))],
            out_specs=pl.BlockSpec((tr, lanes), lambda i, chip_ref: (i, 0))),
        out_shape=jax.ShapeDtypeStruct((rows, lanes), F32),
        compiler_params=_params(("parallel",)),
    )(chip, chip_sums, got)


def _adamw_math(w, g, m, v):
    m = ADAM_B1 * m + (1.0 - ADAM_B1) * g
    v = ADAM_B2 * v + (1.0 - ADAM_B2) * (g * g)
    m_hat = m / (1.0 - ADAM_B1 ** ADAM_STEP)
    v_hat = v / (1.0 - ADAM_B2 ** ADAM_STEP)
    return -ADAM_LR * (m_hat / (jnp.sqrt(v_hat) + ADAM_EPS) + ADAM_WD * w), m, v


def _adamw(w, g, m, v, name):
    rows, cols = w.shape
    tr = 256 if rows % 256 == 0 and rows * cols > 2 ** 19 else rows

    def body(w_ref, g_ref, m_ref, v_ref, d_o, m_o, v_o):
        d_o[...], m_o[...], v_o[...] = _adamw_math(w_ref[...], g_ref[...], m_ref[...], v_ref[...])

    spec = pl.BlockSpec((tr, cols), lambda i: (i, 0))
    return pl.pallas_call(
        body, name=name, grid=(rows // tr,), in_specs=[spec] * 4, out_specs=[spec] * 3,
        out_shape=[jax.ShapeDtypeStruct((rows, cols), F32)] * 3, compiler_params=_params(("parallel",)),
    )(w, g, m, v)


def _adamw_small(parts, ws, ms, vs):
    k = len(_SMALL)

    def body(p_ref, *refs):
        w_refs, m_refs, v_refs = refs[:k], refs[k:2 * k], refs[2 * k:3 * k]
        outs = refs[3 * k:]
        g = p_ref[0]
        for d in range(1, N_DEV):
            g = g + p_ref[d]
        o = 0
        for i, (_, n) in enumerate(_SMALL):
            gp = g[:, o:o + n]
            outs[1 + i][...] = gp
            outs[1 + k + i][...], outs[1 + 2 * k + i][...], outs[1 + 3 * k + i][...] = _adamw_math(
                w_refs[i][...], gp, m_refs[i][...], v_refs[i][...])
            o += n
        outs[0][...] = g[:, o:o + _LANES]

    shapes = [jax.ShapeDtypeStruct((1, n), F32) for _, n in _SMALL]
    out = pl.pallas_call(
        body, name="adamw_small", out_shape=[jax.ShapeDtypeStruct((1, _LANES), F32)] + shapes * 4,
        compiler_params=_params(),
    )(parts, *ws, *ms, *vs)
    return out[0][0, 0], out[1:1 + k], out[1 + k:1 + 2 * k], out[1 + 2 * k:1 + 3 * k], out[1 + 3 * k:]


_WEIGHT_NAMES = ['norm_mix_pre', 'w_in', 'b_gate', 'mu_rw', 'w0', 'w_up', 'a0', 'a_up', 'g_up', 'k_k', 'k_a', 'r_k',
                 'lnx_w', 'lnx_b', 'w_sb_out', 'w_rw_out', 'w_o', 'norm_mix_post', 'norm_ffn_pre', 'w_ffn_gate',
                 'w_ffn_up', 'w_ffn_down', 'norm_ffn_post']


def _step(x, target, w, m, v):
    sharded = [n for n, _, _ in _SHARDED]
    sm = {n: w[n].reshape(1, -1) for n, _ in _SMALL}
    own = {n: w[n][0] for n in sharded}
    packed = _pack_shards(own, BF16)
    wt = _unpack_gathered(_all_gather(packed[:_EARLY_ROWS]), slice(0, _EARLY))
    loss_part, grad_x, gw, gs = _local_step(x, target, sm, wt, packed[_EARLY_ROWS:])

    cx, cy, cc = lax.axis_index("x"), lax.axis_index("y"), lax.axis_index("c")
    core = jnp.reshape(cc, (1,)).astype(jnp.int32)
    chip = jnp.reshape(2 * cx + cy, (1,)).astype(jnp.int32)
    pack = _pack_full_grads(gw).reshape(4, 2, _PACK_TOTAL, _LANES).transpose(1, 0, 2, 3)
    got_core, small_parts = _exchange_core(pack, _pack_small(gs, loss_part[0, 0]))
    chip_sums = _add_core_parts(pack, got_core, core)
    grads_packed = _sum_chip_parts(chip_sums, _exchange_chips(chip_sums), chip)
    g_sh = _unpack_shards(grads_packed)

    row = lambda t: [t[n].reshape(1, -1) for n, _ in _SMALL]
    loss, *by_kind = _adamw_small(small_parts, row(w), row(m), row(v))
    g_s, d_s, m_s, v_s = ({n: t[i] for i, (n, _) in enumerate(_SMALL)} for t in by_kind)

    grads, deltas, new_m, new_v = {}, {}, {}, {}
    for n in _WEIGHT_NAMES:
        if n in g_sh:
            d_, m_, v_ = _adamw(own[n], g_sh[n], m[n][0], v[n][0], "adamw_" + n)
            grads[n], deltas[n], new_m[n], new_v[n] = (t.reshape(w[n].shape) for t in (g_sh[n], d_, m_, v_))
        else:
            grads[n], deltas[n], new_m[n], new_v[n] = (t[n].reshape(w[n].shape) for t in (g_s, d_s, m_s, v_s))
    return (loss, grad_x, *[grads[n] for n in _WEIGHT_NAMES], *[deltas[n] for n in _WEIGHT_NAMES],
            *[new_m[n] for n in _WEIGHT_NAMES], *[new_v[n] for n in _WEIGHT_NAMES])


def kernel(x, norm_mix_pre, w_in, b_gate, mu_rw, w0, w_up, a0, a_up, g_up, k_k, k_a, r_k, lnx_w, lnx_b, w_sb_out, w_rw_out, w_o, norm_mix_post, norm_ffn_pre, w_ffn_gate, w_ffn_up, w_ffn_down, norm_ffn_post, loss_target, m_norm_mix_pre, m_w_in, m_b_gate, m_mu_rw, m_w0, m_w_up, m_a0, m_a_up, m_g_up, m_k_k, m_k_a, m_r_k, m_lnx_w, m_lnx_b, m_w_sb_out, m_w_rw_out, m_w_o, m_norm_mix_post, m_norm_ffn_pre, m_w_ffn_gate, m_w_ffn_up, m_w_ffn_down, m_norm_ffn_post, v_norm_mix_pre, v_w_in, v_b_gate, v_mu_rw, v_w0, v_w_up, v_a0, v_a_up, v_g_up, v_k_k, v_k_a, v_r_k, v_lnx_w, v_lnx_b, v_w_sb_out, v_w_rw_out, v_w_o, v_norm_mix_post, v_norm_ffn_pre, v_w_ffn_gate, v_w_ffn_up, v_w_ffn_down, v_norm_ffn_post):
    args = locals()
    w = {n: args[n] for n in _WEIGHT_NAMES}
    m = {n: args["m_" + n] for n in _WEIGHT_NAMES}
    v = {n: args["v_" + n] for n in _WEIGHT_NAMES}
    return _step(x, loss_target, w, m, v)
```

```python
import functools

import jax
import jax.numpy as jnp
from jax import lax
from jax.experimental import pallas as pl
from jax.experimental.pallas import tpu as pltpu

F32 = jnp.float32
BF16 = jnp.bfloat16

D_MODEL = 1024
HEADS = 8
HEAD_DIM = 64
WIDTH = HEADS * HEAD_DIM
W_LORA, A_LORA, G_LORA = 64, 64, 128
SB_COLS = 3 * WIDTH
RW_COLS = 3 * WIDTH + W_LORA + A_LORA + G_LORA
GATE_COLS = 2 * D_MODEL
D_FF = 2816
RMS_EPS = 1e-6
GN_EPS = HEAD_DIM * 1e-5
N_DEV = 8

ADAM_LR, ADAM_B1, ADAM_B2, ADAM_EPS, ADAM_WD, ADAM_STEP = 0.001, 0.9, 0.999, 1e-08, 0.01, 10

ROW_TILE = 256
SCAN_CHUNK = 64
ATT_ALIGN = 128
ATT_WINDOW = 384
ATT_Q = 128
ATT_PAIRS = 2
SB_DEAD = -104.0
SCAN_SEQS_FWD = 4
SCAN_SEQS_BWD = 2
SCAN_PASSES = 1
VMEM_LIMIT = 56 * 2 ** 20

MESH = pl.DeviceIdType.MESH


def _params(sem=None, vmem=VMEM_LIMIT):
    kw = dict(vmem_limit_bytes=vmem)
    if sem is not None:
        kw["dimension_semantics"] = sem
    return pltpu.CompilerParams(**kw)


def _const_spec(shape):
    nd = len(shape)
    return pl.BlockSpec(shape, lambda *_: (0,) * nd, pipeline_mode=pl.Buffered(1))


def _row_spec(tm, n):
    return pl.BlockSpec((tm, n), lambda i: (i, 0))


def _mm(a, b):
    return lax.dot_general(a, b, (((1,), (0,)), ((), ())), preferred_element_type=F32)


def _mm_nt(a, b):
    return lax.dot_general(a, b, (((1,), (1,)), ((), ())), preferred_element_type=F32)


def _mm_tn(a, b):
    return lax.dot_general(a, b, (((0,), (0,)), ((), ())), preferred_element_type=F32)


def _softplus(z):
    return jnp.maximum(z, 0.0) + jnp.log1p(jnp.exp(-jnp.abs(z)))


def _rms_fwd(x, gain):
    rstd = lax.rsqrt(jnp.mean(x * x, axis=-1, keepdims=True) + RMS_EPS)
    xn = x * rstd
    return xn * gain, xn, rstd


def _rms_bwd(dy, xn, rstd, gain):
    u = dy * gain
    dx = rstd * (u - xn * jnp.mean(u * xn, axis=-1, keepdims=True))
    return dx, jnp.sum(dy * xn, axis=0, keepdims=True)


def _acc_out(ref, val, first):
    @pl.when(first)
    def _():
        ref[...] = val

    @pl.when(jnp.logical_not(first))
    def _():
        ref[...] += val


_IN_COLS = SB_COLS + RW_COLS + GATE_COLS
_QKV_OF, _RW_OF, _GATE_OF = slice(0, SB_COLS), slice(SB_COLS, SB_COLS + RW_COLS), slice(SB_COLS + RW_COLS, _IN_COLS)


def _in_proj_fwd(x2, g_pre, w_in, b_gate, tm):
    T = x2.shape[0]

    def body(x_ref, g_ref, w_ref, b_ref, h_ref, qkv_ref, prw_ref, gate_ref):
        h = _rms_fwd(x_ref[...], g_ref[...])[0].astype(BF16)
        h_ref[...] = h
        qkv_ref[...] = _mm(h, w_ref[:, _QKV_OF]).astype(BF16)
        prw_ref[...] = _mm(h, w_ref[:, _RW_OF])
        gate_ref[...] = jax.nn.sigmoid(_mm(h, w_ref[:, _GATE_OF]) + b_ref[...])

    return pl.pallas_call(
        body, name="in_proj_fwd", grid=(T // tm,),
        in_specs=[_row_spec(tm, D_MODEL), _const_spec((1, D_MODEL)), _const_spec((D_MODEL, _IN_COLS)),
                  _const_spec((1, GATE_COLS))],
        out_specs=[_row_spec(tm, D_MODEL), _row_spec(tm, SB_COLS), _row_spec(tm, RW_COLS), _row_spec(tm, GATE_COLS)],
        out_shape=[jax.ShapeDtypeStruct((T, D_MODEL), BF16), jax.ShapeDtypeStruct((T, SB_COLS), BF16),
                   jax.ShapeDtypeStruct((T, RW_COLS), F32), jax.ShapeDtypeStruct((T, GATE_COLS), F32)],
        compiler_params=_params(("parallel",)),
    )(x2, g_pre, w_in, b_gate)


def _in_proj_bwd(x2, g_pre, dx1, dqkv, dprw, dgate, w_in, tm):
    T = x2.shape[0]

    def body(x_ref, g_ref, dx1_ref, dq_ref, dr_ref, dg_ref, w_ref, gx_ref, dgain_ref):
        dh = (_mm_nt(dq_ref[...], w_ref[:, _QKV_OF]) + _mm_nt(dr_ref[...], w_ref[:, _RW_OF])
              + _mm_nt(dg_ref[...], w_ref[:, _GATE_OF]))
        gain = g_ref[...]
        _, xn, rstd = _rms_fwd(x_ref[...], gain)
        dx, dgain = _rms_bwd(dh, xn, rstd, gain)
        gx_ref[...] = dx1_ref[...] + dx
        _acc_out(dgain_ref, dgain, pl.program_id(0) == 0)

    return pl.pallas_call(
        body, name="in_proj_bwd", grid=(T // tm,),
        in_specs=[_row_spec(tm, D_MODEL), _const_spec((1, D_MODEL)), _row_spec(tm, D_MODEL), _row_spec(tm, SB_COLS),
                  _row_spec(tm, RW_COLS), _row_spec(tm, GATE_COLS), _const_spec((D_MODEL, _IN_COLS))],
        out_specs=[_row_spec(tm, D_MODEL), pl.BlockSpec((1, D_MODEL), lambda i: (0, 0))],
        out_shape=[jax.ShapeDtypeStruct((T, D_MODEL), F32), jax.ShapeDtypeStruct((1, D_MODEL), F32)],
        compiler_params=_params(("arbitrary",)),
    )(x2, g_pre, dx1, dqkv, dprw, dgate, w_in)


def _pick_tile(n, cap):
    best = None
    for t in range(128, min(n, cap) + 1, 128):
        if n % t == 0:
            best = t
    return n if best is None else best


def _grad_w(a, b, name):
    T, K = a.shape
    N = b.shape[1]
    tk, tn, tt = _pick_tile(K, 1408), _pick_tile(N, 2048), min(T, 2048)

    def body(a_ref, b_ref, o_ref):
        _acc_out(o_ref, _mm_tn(a_ref[...], b_ref[...]), pl.program_id(2) == 0)

    return pl.pallas_call(
        body, name=name, grid=(K // tk, N // tn, T // tt),
        in_specs=[pl.BlockSpec((tt, tk), lambda i, j, t: (t, i)), pl.BlockSpec((tt, tn), lambda i, j, t: (t, j))],
        out_specs=pl.BlockSpec((tk, tn), lambda i, j, t: (i, j)),
        out_shape=jax.ShapeDtypeStruct((K, N), F32),
        compiler_params=_params(("parallel", "parallel", "arbitrary")),
    )(a, b)


def _tri(n, kind):
    r = lax.broadcasted_iota(jnp.int32, (n, n), 0)
    c = lax.broadcasted_iota(jnp.int32, (n, n), 1)
    return {"gt": r > c, "le": r <= c, "lt": r < c, "ge": r >= c}[kind]


def _split_mm(x, u):
    hi = x.astype(BF16)
    lo = (x - hi.astype(F32)).astype(BF16)
    return _mm(hi, u) + _mm(lo, u)


def _running_sums(x, carry, tri, kb, reverse=False):
    blocks = range(x.shape[1] // kb)
    parts = {}
    for b in (reversed(blocks) if reverse else blocks):
        piece = x[:, b * kb:(b + 1) * kb]
        parts[b] = carry + _split_mm(piece, tri)
        carry = carry + jnp.sum(piece, axis=1, keepdims=True)
    return jnp.concatenate([parts[b] for b in blocks], axis=1), carry


def _sb_valid(row0, col0, first, last, qb, kb):
    ahead = lax.broadcasted_iota(jnp.int32, (qb, kb), 1) - lax.broadcasted_iota(jnp.int32, (qb, kb), 0)
    col = lax.broadcasted_iota(jnp.int32, (1, kb), 1)
    return jnp.logical_and(ahead < row0 - col0, jnp.logical_and(col >= first - col0, col < last - col0))


def _sb_softplus(z):
    return jnp.maximum(z, 0.0) + jnp.log(1.0 + jnp.exp(-jnp.abs(z)))


_PAIR = 2 * HEAD_DIM
_PAIRS = WIDTH // _PAIR


def _first_head_lanes():
    return lax.broadcasted_iota(jnp.int32, (1, _PAIR), 1) < HEAD_DIM


def _per_head(t, first_head):
    zero = jnp.zeros_like(t)
    return jnp.where(first_head, t, zero), jnp.where(first_head, zero, t)


def _sb_fwd(qkv, bl, seq, ride=None):
    qb, win, kb = min(ATT_Q, seq), min(ATT_WINDOW, seq), ATT_ALIGN
    nq = seq // qb
    nh, width, groups = 2 * ATT_PAIRS, ATT_PAIRS * _PAIR, _PAIRS // ATT_PAIRS
    pair_of = lambda h: slice((h // 2) * _PAIR, (h // 2 + 1) * _PAIR)

    steps = bl * groups
    pass_on_at = (5 * steps) // 8

    def body(q_ref, k_ref, v_ref, *rest):
        g = pl.program_id(0) * groups + pl.program_id(1)
        if ride is None:
            o_ref, l_ref, stop_ref = rest
        else:
            ride_ref, o_ref, l_ref, stop_ref, gathered_ref, *sems = rest
            start, forward, finish = _gather_steps(ride_ref, gathered_ref, *sems)
            pl.when(g == 0)(start)
            pl.when(g == pass_on_at)(forward)
        first_head = _first_head_lanes()
        u_after = _tri(kb, "gt").astype(BF16)

        def qblock(i, _):
            rows = pl.ds(pl.multiple_of(i * qb, qb), qb)
            qs = q_ref[rows, :] * (HEAD_DIM ** -0.5)
            qh = [_per_head(qs[:, pair_of(h)], first_head)[h % 2] for h in range(nh)]

            def live(carry):
                return jnp.logical_and(carry[0] > 0, carry[3] > 0)

            def window(carry):
                hi, accs, cs, _ = carry
                lo = pl.multiple_of(jnp.maximum(hi - win, 0), kb)
                cols = pl.ds(lo, win)
                kv, vv = k_ref[cols, :], v_ref[cols, :]
                valid = _sb_valid(i * qb, lo, lo, hi, qb, win)
                accs, cs = list(accs), list(cs)
                for h in range(nh):
                    z = _mm_nt(qh[h], kv[:, pair_of(h)])
                    sp = _sb_softplus(z)
                    spm = jnp.where(valid, sp, 0.0)
                    after, cs[h] = _running_sums(spm, cs[h], u_after, kb, reverse=True)
                    w = jnp.where(valid, jnp.exp(z - sp - after), 0.0)
                    accs[h] = accs[h] + _mm(w.astype(BF16), vv[:, pair_of(h)])
                alive = functools.reduce(jnp.minimum, [jnp.min(c) for c in cs]) < -SB_DEAD
                return lo, tuple(accs), tuple(cs), alive.astype(jnp.int32)

            zero_acc, zero_c = jnp.zeros((qb, _PAIR), F32), jnp.zeros((qb, 1), F32)
            lo, accs, cs, _ = lax.while_loop(
                live, window, ((i + 1) * qb, (zero_acc,) * nh, (zero_c,) * nh, jnp.int32(1)))
            for pp in range(ATT_PAIRS):
                o_ref[rows, pp * _PAIR:(pp + 1) * _PAIR] = jnp.where(
                    first_head, accs[2 * pp], accs[2 * pp + 1]).astype(BF16)
            for h in range(nh):
                l_ref[h, rows, :] = cs[h]
            stop_ref[g, i] = lo
            return 0

        lax.fori_loop(0, nq, qblock, 0)
        if ride is not None:
            pl.when(g == steps - 1)(finish)

    col = lambda off: pl.BlockSpec((seq, width), lambda b, p: (b, off + p))
    in_specs = [col(0), col(groups), col(2 * groups)]
    out_specs = [col(0), pl.BlockSpec((None, nh, seq, 1), lambda b, p: (b, p, 0, 0)), pl.BlockSpec(memory_space=pltpu.SMEM)]
    out_shape = [jax.ShapeDtypeStruct((bl * seq, WIDTH), BF16), jax.ShapeDtypeStruct((bl, HEADS, seq, 1), F32),
                 jax.ShapeDtypeStruct((bl * groups, nq), jnp.int32)]
    if ride is None:
        return pl.pallas_call(body, name="sb_fwd", grid=(bl, groups), in_specs=in_specs, out_specs=out_specs,
                              out_shape=out_shape, compiler_params=_params(("arbitrary", "arbitrary")))(qkv, qkv, qkv)
    return pl.pallas_call(
        body, name="sb_fwd", grid=(bl, groups), in_specs=in_specs + [_ANY], out_specs=out_specs + [_ANY],
        out_shape=out_shape + [jax.ShapeDtypeStruct((N_DEV,) + ride.shape, ride.dtype)], scratch_shapes=_GATHER_SEMS,
        compiler_params=_params(("arbitrary", "arbitrary")),
    )(qkv, qkv, qkv, ride)


def _sb_bwd(qkv, do, lsum, stop, bl, seq):
    qb, win, kb = min(ATT_Q, seq), min(ATT_WINDOW, seq), ATT_ALIGN
    nq = seq // qb
    nh, width, groups = 2 * ATT_PAIRS, ATT_PAIRS * _PAIR, _PAIRS // ATT_PAIRS
    pair_of = lambda h: slice((h // 2) * _PAIR, (h // 2 + 1) * _PAIR)

    def body(stop_ref, q_ref, k_ref, v_ref, do_ref, l_ref, dq_ref, dk_ref, dv_ref, dk_acc, dv_acc):
        g = pl.program_id(0) * groups + pl.program_id(1)
        first_head = _first_head_lanes()
        u_incl = _tri(kb, "le").astype(BF16)
        u_excl = _tri(kb, "lt").astype(BF16)
        dk_acc[...] = jnp.zeros_like(dk_acc)
        dv_acc[...] = jnp.zeros_like(dv_acc)

        def qblock(i, _):
            rows = pl.ds(pl.multiple_of(i * qb, qb), qb)
            qv = q_ref[rows, :]
            qs = qv * (HEAD_DIM ** -0.5)
            dob = do_ref[rows, :]
            qh = [_per_head(qs[:, pair_of(h)], first_head)[h % 2] for h in range(nh)]
            doh = [_per_head(dob[:, pair_of(h)], first_head)[h % 2] for h in range(nh)]
            ltot = [l_ref[h, rows, :] for h in range(nh)]

            first = (jnp.clip(stop_ref[g, i], 0, i * qb) // ATT_ALIGN) * ATT_ALIGN

            def window(n, carry):
                dqs, ps, es = (list(t) for t in carry)
                start = first + n * win
                lo = pl.multiple_of(jnp.minimum(start, seq - win), kb)
                cols = pl.ds(lo, win)
                kv, vv = k_ref[cols, :], v_ref[cols, :]
                valid = _sb_valid(i * qb, lo, start, seq, qb, win)
                dks, dvs = [], []
                for h in range(nh):
                    kp, vp = kv[:, pair_of(h)], vv[:, pair_of(h)]
                    z = _mm_nt(qh[h], kp)
                    sp = _sb_softplus(z)
                    spm = jnp.where(valid, sp, 0.0)
                    upto, ps[h] = _running_sums(spm, ps[h], u_incl, kb)
                    w = jnp.where(valid, jnp.exp(z - sp - (ltot[h] - upto)), 0.0)
                    e = _mm_nt(doh[h], vp) * w
                    dlf, es[h] = _running_sums(e, es[h], u_excl, kb)
                    sig = jnp.exp(z - sp)
                    dz = jnp.where(valid, e * (1.0 - sig) - dlf * sig, 0.0) * (HEAD_DIM ** -0.5)
                    dzb = dz.astype(BF16)
                    dvs.append(_mm_tn(w.astype(BF16), dob[:, pair_of(h)]))
                    dks.append(_mm_tn(dzb, qv[:, pair_of(h)]))
                    dqs[h] = dqs[h] + _mm(dzb, kp)
                for pp in range(ATT_PAIRS):
                    lanes = slice(pp * _PAIR, (pp + 1) * _PAIR)
                    dv_acc[cols, lanes] += jnp.where(first_head, dvs[2 * pp], dvs[2 * pp + 1])
                    dk_acc[cols, lanes] += jnp.where(first_head, dks[2 * pp], dks[2 * pp + 1])
                return tuple(dqs), tuple(ps), tuple(es)

            zero_q, zero_c = jnp.zeros((qb, _PAIR), F32), jnp.zeros((qb, 1), F32)
            dqs, _, _ = lax.fori_loop(0, ((i + 1) * qb - first + win - 1) // win, window,
                                      ((zero_q,) * nh, (zero_c,) * nh, (zero_c,) * nh))
            for pp in range(ATT_PAIRS):
                dq_ref[rows, pp * _PAIR:(pp + 1) * _PAIR] = jnp.where(
                    first_head, dqs[2 * pp], dqs[2 * pp + 1]).astype(BF16)
            return 0

        lax.fori_loop(0, nq, qblock, 0)
        dk_ref[...] = dk_acc[...].astype(BF16)
        dv_ref[...] = dv_acc[...].astype(BF16)

    col = lambda off: pl.BlockSpec((seq, width), lambda b, p, stop_ref: (b, off + p))
    return pl.pallas_call(
        body, name="sb_bwd",
        grid_spec=pltpu.PrefetchScalarGridSpec(
            num_scalar_prefetch=1, grid=(bl, groups),
            in_specs=[col(0), col(groups), col(2 * groups), col(0),
                      pl.BlockSpec((None, nh, seq, 1), lambda b, p, stop_ref: (b, p, 0, 0))],
            out_specs=[col(0), col(0), col(0)],
            scratch_shapes=[pltpu.VMEM((seq, width), F32), pltpu.VMEM((seq, width), F32)]),
        out_shape=[jax.ShapeDtypeStruct((bl * seq, WIDTH), BF16)] * 3,
        compiler_params=_params(("parallel", "parallel")),
    )(stop, qkv, qkv, qkv, do, lsum)


@jax.custom_vjp
def _lora_mm(x, w):
    return _mm(x.astype(BF16), w.astype(BF16))


_lora_mm.defvjp(
    lambda x, w: (_mm(x.astype(BF16), w.astype(BF16)), (x, w)),
    lambda res, ct: (_mm_nt(ct.astype(BF16), res[1].astype(BF16)), _mm_tn(res[0].astype(BF16), ct.astype(BF16))))


def _rw_prep_math(p, ps, mu, w0, w_up, a0, a_up, g_up, k_k, k_a):
    pm = p + (ps - p) * mu
    r, k, v = pm[:, :WIDTH], pm[:, WIDTH:2 * WIDTH], pm[:, 2 * WIDTH:3 * WIDTH]
    o = 3 * WIDTH
    xw, xa, xg = pm[:, o:o + W_LORA], pm[:, o + W_LORA:o + W_LORA + A_LORA], pm[:, o + W_LORA + A_LORA:]
    w_raw = w0 + _lora_mm(jnp.tanh(xw), w_up)
    lw = -jnp.exp(-_softplus(-w_raw) - 0.5)
    a = jax.nn.sigmoid(a0 + _lora_mm(xa, a_up))
    g = _lora_mm(jax.nn.sigmoid(xg), g_up)
    kk = k * k_k
    k2 = k * (1.0 + (a - 1.0) * k_a)
    return r, lw, k2, v, kk, a, g


def _shift_down(p, first_row):
    row = lax.broadcasted_iota(jnp.int32, p.shape, 0)
    return jnp.where(row == 0, first_row, pltpu.roll(p, 1, 0))


def _shift_up(p, last_row):
    row = lax.broadcasted_iota(jnp.int32, p.shape, 0)
    return jnp.where(row == p.shape[0] - 1, last_row, pltpu.roll(p, p.shape[0] - 1, 0))


_PREP_PARAM_SHAPES = [(1, RW_COLS), (1, WIDTH), (W_LORA, WIDTH), (1, WIDTH), (A_LORA, WIDTH), (G_LORA, WIDTH),
                      (1, WIDTH), (1, WIDTH)]


def _prev_rows_spec(tm):
    return pl.BlockSpec((8, RW_COLS), lambda i: (jnp.maximum(i * (tm // 8) - 1, 0), 0))


def _head_spec(tm, seq, tile_of=lambda i: i):
    per_seq = seq // tm
    return pl.BlockSpec((None, HEADS, tm, HEAD_DIM),
                        lambda i: (tile_of(i) // per_seq, 0, tile_of(i) % per_seq, 0))


def _split_heads(val, ref):
    for h in range(HEADS):
        ref[h] = val[:, h * HEAD_DIM:(h + 1) * HEAD_DIM]


def _join_heads(ref):
    return jnp.concatenate([ref[h] for h in range(HEADS)], axis=1)


def _rw_prep_fwd(prw, params, seq, tm):
    T = prw.shape[0]

    def body(p_ref, prev_ref, *rest):
        prm = [r_[...] for r_ in rest[:8]]
        outs = rest[8:]
        i = pl.program_id(0)
        first = jnp.where((i * tm) % seq == 0, 0.0, prev_ref[7:8, :])
        p = p_ref[...]
        vals = _rw_prep_math(p, _shift_down(p, first), *prm)
        for o_ref, val in zip(outs[:6], vals[:6]):
            _split_heads(val, o_ref)
        outs[6][...] = vals[6]

    by_head = jax.ShapeDtypeStruct((T // seq, HEADS, seq, HEAD_DIM), F32)
    return pl.pallas_call(
        body, name="rw_prep_fwd", grid=(T // tm,),
        in_specs=[_row_spec(tm, RW_COLS), _prev_rows_spec(tm)] + [_const_spec(s) for s in _PREP_PARAM_SHAPES],
        out_specs=[_head_spec(tm, seq)] * 6 + [_row_spec(tm, WIDTH)],
        out_shape=[by_head] * 6 + [jax.ShapeDtypeStruct((T, WIDTH), F32)],
        compiler_params=_params(("parallel",)),
    )(prw, prw, *params)


def _rw_prep_bwd(prw, params, cts, seq, tm):
    T = prw.shape[0]
    n = T // tm

    def body(p_ref, prev_ref, *rest):
        prm = [r_[...] for r_ in rest[:8]]
        ct = tuple(_join_heads(r_) for r_ in rest[8:14]) + (rest[14][...],)
        dp_ref = rest[15]
        dprm_refs = rest[16:24]
        carry = rest[24]
        step = pl.program_id(0)
        i = n - 1 - step
        first = jnp.where((i * tm) % seq == 0, 0.0, prev_ref[7:8, :])
        p = p_ref[...]
        _, vjp = jax.vjp(_rw_prep_math, p, _shift_down(p, first), *prm)
        grads = vjp(ct)
        dp, dps = grads[0], grads[1]
        nxt = jnp.where(jnp.logical_or(step == 0, ((i + 1) * tm) % seq == 0), 0.0, carry[0:1, :])
        dp_ref[...] = (dp + _shift_up(dps, nxt)).astype(BF16)
        carry[...] = dps[0:8, :]
        for ref, gval in zip(dprm_refs, grads[2:]):
            _acc_out(ref, gval, step == 0)

    rev = lambda w: pl.BlockSpec((tm, w), lambda s: (n - 1 - s, 0))
    prev = pl.BlockSpec((8, RW_COLS), lambda s: (jnp.maximum((n - 1 - s) * (tm // 8) - 1, 0), 0))
    return pl.pallas_call(
        body, name="rw_prep_bwd", grid=(n,),
        in_specs=([rev(RW_COLS), prev] + [_const_spec(s) for s in _PREP_PARAM_SHAPES]
                  + [_head_spec(tm, seq, lambda s: n - 1 - s)] * 6 + [rev(WIDTH)]),
        out_specs=[rev(RW_COLS)] + [pl.BlockSpec(s, lambda s_: (0, 0)) for s in _PREP_PARAM_SHAPES],
        out_shape=[jax.ShapeDtypeStruct((T, RW_COLS), BF16)] + [jax.ShapeDtypeStruct(s, F32) for s in _PREP_PARAM_SHAPES],
        scratch_shapes=[pltpu.VMEM((8, RW_COLS), F32)],
        compiler_params=_params(("arbitrary",)),
    )(prw, prw, *params, *cts)


def _make_bmm(passes):
    def raw(dn, a, b):
        d = lambda x, y: lax.dot_general(x, y, dn, preferred_element_type=F32)
        ah = a.astype(BF16)
        bh = b.astype(BF16)
        if passes == 1:
            return d(ah, bh)
        al = (a - ah.astype(F32)).astype(BF16)
        bl = (b - bh.astype(F32)).astype(BF16)
        return d(ah, bh) + (d(ah, bl) + d(al, bh))

    dn_nn = (((2,), (1,)), ((0,), (0,)))
    dn_nt = (((2,), (2,)), ((0,), (0,)))
    dn_tn = (((1,), (1,)), ((0,), (0,)))

    @jax.custom_vjp
    def nn(a, b):
        return raw(dn_nn, a, b)

    @jax.custom_vjp
    def nt(a, b):
        return raw(dn_nt, a, b)

    @jax.custom_vjp
    def tn(a, b):
        return raw(dn_tn, a, b)

    nn.defvjp(lambda a, b: (raw(dn_nn, a, b), (a, b)), lambda res, ct: (nt(ct, res[1]), tn(res[0], ct)))
    nt.defvjp(lambda a, b: (raw(dn_nt, a, b), (a, b)), lambda res, ct: (nn(ct, res[1]), tn(ct, res[0])))
    tn.defvjp(lambda a, b: (raw(dn_tn, a, b), (a, b)), lambda res, ct: (nt(res[1], ct), nn(res[0], ct)))

    def unit_lower_inverse(m):
        n = m.shape[-1]
        row = lax.broadcasted_iota(jnp.int32, (n, n), 0)
        col = lax.broadcasted_iota(jnp.int32, (n, n), 1)
        m16 = ((row // 16) == (col // 16)).astype(F32)
        m32 = ((row // 32) == (col // 32)).astype(F32)
        a1 = m * m16
        a2 = nn(a1, a1)
        a4 = nn(a2, a2)
        a8 = nn(a4, a4)
        inv = (row == col).astype(F32) - a1
        inv = inv + nn(inv, a2)
        inv = inv + nn(inv, a4)
        inv = inv + nn(inv, a8)
        inv = inv - nn(nn(inv, m * (m32 - m16)), inv)
        return inv - nn(nn(inv, m * (1.0 - m32)), inv)

    @jax.custom_vjp
    def inverse(m):
        return unit_lower_inverse(m)

    def inverse_fwd(m):
        inv = unit_lower_inverse(m)
        return inv, inv

    inverse.defvjp(inverse_fwd, lambda inv, ct: (-nt(tn(inv, ct), inv),))
    return nn, nt, tn, inverse


def _wkv_chunk(s0, r, lw, k, v, kk, a, lnw, lnb, rk):
    nn, nt, tn, inverse = _make_bmm(SCAN_PASSES)
    G, L, N = r.shape
    rep = lambda t: jnp.broadcast_to(t[None], (G // HEADS, HEADS, 1, N)).reshape(G, 1, N)
    kap = kk * lax.rsqrt(jnp.maximum(jnp.sum(kk * kk, axis=-1, keepdims=True), 1e-24))
    b = a * kap
    row = lax.broadcasted_iota(jnp.int32, (L, L), 0)
    col = lax.broadcasted_iota(jnp.int32, (L, L), 1)
    low_incl = (col <= row).astype(F32)
    low_strict = (col < row).astype(F32)
    c = _make_bmm(3)[0](jnp.broadcast_to(low_incl[None], (G, L, L)), lw)
    c_all = jnp.sum(lw, axis=1, keepdims=True)
    g_inv = jnp.exp(-c)
    kap_t = kap * jnp.exp(c - lw)
    b_t = b * g_inv
    k_t = k * g_inv
    r_t = r * jnp.exp(c)
    g_all = jnp.exp(c_all)
    m_b = nt(kap_t, b_t) * low_strict
    m_k = nt(kap_t, k_t) * low_strict
    n_b = nt(r_t, b_t) * low_incl
    n_k = nt(r_t, k_t) * low_incl
    rhs = -(nt(kap_t, s0) + nn(m_k, v))
    sa = nn(inverse(m_b), rhs)
    y = nt(r_t, s0) + nn(n_b, sa) + nn(n_k, v)
    s1 = s0 * g_all + tn(sa, b_t * g_all) + tn(v, k_t * g_all)
    mean = jnp.mean(y, axis=-1, keepdims=True)
    yc = y - mean
    var = jnp.mean(yc * yc, axis=-1, keepdims=True)
    out = yc * lax.rsqrt(var + GN_EPS) * rep(lnw) + rep(lnb)
    out = out + jnp.sum(r * k * rep(rk), axis=-1, keepdims=True) * v
    return out, s1


def _scan_heads_per_step(total_heads, seqs_wanted):
    n_seq = total_heads // HEADS
    return HEADS * max(d for d in range(1, seqs_wanted + 1) if n_seq % d == 0)


def _wkv_fwd(seqs, lnw, lnb, rk):
    G, S, N = seqs[0].shape
    L = SCAN_CHUNK
    nc = S // L

    def body(*refs):
        ins = [r_[...] for r_ in refs[:6]]
        prm = [r_[...] for r_ in refs[6:9]]
        out_ref, st_ref, state = refs[9], refs[10], refs[11]

        @pl.when(pl.program_id(1) == 0)
        def _():
            state[...] = jnp.zeros_like(state)

        s0 = state[...]
        st_ref[...] = s0
        out, s1 = _wkv_chunk(s0, *ins, *prm)
        out_ref[...] = out
        state[...] = s1

    gb = _scan_heads_per_step(G, SCAN_SEQS_FWD)
    blk = pl.BlockSpec((gb, L, N), lambda b, i: (b, i, 0))
    pspec = _const_spec((HEADS, 1, N))
    return pl.pallas_call(
        body, name="wkv_fwd", grid=(G // gb, nc), in_specs=[blk] * 6 + [pspec] * 3,
        out_specs=[blk, pl.BlockSpec((None, gb, N, N), lambda b, i: (i, b, 0, 0))],
        out_shape=[jax.ShapeDtypeStruct((G, S, N), F32), jax.ShapeDtypeStruct((nc, G, N, N), F32)],
        scratch_shapes=[pltpu.VMEM((gb, N, N), F32)],
        compiler_params=_params(("parallel", "arbitrary")),
    )(*seqs, lnw, lnb, rk)


def _wkv_bwd(seqs, states, dout, lnw, lnb, rk, ride=None):
    G, S, N = seqs[0].shape
    L = SCAN_CHUNK
    nc = S // L
    gb = _scan_heads_per_step(G, SCAN_SEQS_BWD)

    def body(*refs):
        ins = [r_[...] for r_ in refs[:6]]
        s0 = refs[6][...]
        ct_out = refs[7][...]
        prm = [r_[...] for r_ in refs[8:11]]
        refs = refs[11:]
        if ride is not None:
            start, finish = _scatter_steps(refs[0], refs[10], refs[12], refs[13])
            pl.when(jnp.logical_and(pl.program_id(0) == 0, pl.program_id(1) == 0))(start)
            refs = refs[1:]
        d_refs = refs[0:6]
        dprm_refs = refs[6:9]
        dstate = refs[10] if ride is not None else refs[9]
        step = pl.program_id(1)

        @pl.when(step == 0)
        def _():
            dstate[...] = jnp.zeros_like(dstate)

        _, vjp = jax.vjp(_wkv_chunk, s0, *ins, *prm)
        grads = vjp((ct_out, dstate[...]))
        dstate[...] = grads[0]
        for ref, gval in zip(d_refs, grads[1:7]):
            ref[...] = gval
        for ref, gval in zip(dprm_refs, grads[7:]):
            _acc_out(ref, gval, jnp.logical_and(step == 0, pl.program_id(0) == 0))
        if ride is not None:
            pl.when(jnp.logical_and(pl.program_id(0) == G // gb - 1, step == nc - 1))(finish)

    blk = pl.BlockSpec((gb, L, N), lambda b, s: (b, nc - 1 - s, 0))
    pspec = _const_spec((HEADS, 1, N))
    pout = pl.BlockSpec((HEADS, 1, N), lambda b, s: (0, 0, 0))
    in_specs = [blk] * 6 + [pl.BlockSpec((None, gb, N, N), lambda b, s: (nc - 1 - s, b, 0, 0)), blk] + [pspec] * 3
    out_specs = [blk] * 6 + [pout] * 3
    out_shape = [jax.ShapeDtypeStruct((G, S, N), F32)] * 6 + [jax.ShapeDtypeStruct((HEADS, 1, N), F32)] * 3
    scratch = [pltpu.VMEM((gb, N, N), F32)]
    args = (*seqs, states, dout, lnw, lnb, rk)
    if ride is not None:
        in_specs, out_specs, args = in_specs + [_ANY], out_specs + [_ANY], args + (ride,)
        out_shape = out_shape + [jax.ShapeDtypeStruct((N_DEV - 1,) + ride.shape[1:], ride.dtype)]
        scratch = scratch + [pltpu.SemaphoreType.DMA((N_DEV - 1,)), pltpu.SemaphoreType.DMA((N_DEV - 1,))]
    return pl.pallas_call(
        body, name="wkv_bwd", grid=(G // gb, nc), in_specs=in_specs, out_specs=out_specs, out_shape=out_shape,
        scratch_shapes=scratch, compiler_params=_params(("arbitrary", "arbitrary")),
    )(*args)


def _merge_math(o_sb, rw_out, g_rw, gates, w_sb, w_rw, w_o):
    o_rw = (rw_out * g_rw).astype(BF16)
    a = _mm(o_sb, w_sb)
    b = _mm(o_rw, w_rw)
    g1, g2 = gates[:, :D_MODEL], gates[:, D_MODEL:]
    merged = (g1 * a + g2 * b).astype(BF16)
    return o_rw, a, b, g1, g2, merged, _mm(merged, w_o)


def _merge_fwd(x2, o_sb, rw_out, g_rw, gates, w_sb, w_rw, w_o, g_post, seq, tm):
    T = x2.shape[0]

    def body(x_ref, osb_ref, rw_ref, g_ref, gate_ref, wsb_ref, wrw_ref, wo_ref, gp_ref, x1_ref):
        z = _merge_math(osb_ref[...], _join_heads(rw_ref), g_ref[...], gate_ref[...], wsb_ref[...], wrw_ref[...], wo_ref[...])[-1]
        x1_ref[...] = x_ref[...] + _rms_fwd(z, gp_ref[...])[0]

    return pl.pallas_call(
        body, name="merge_fwd", grid=(T // tm,),
        in_specs=[_row_spec(tm, D_MODEL), _row_spec(tm, WIDTH), _head_spec(tm, seq), _row_spec(tm, WIDTH),
                  _row_spec(tm, GATE_COLS), _const_spec((WIDTH, D_MODEL)), _const_spec((WIDTH, D_MODEL)),
                  _const_spec((D_MODEL, D_MODEL)), _const_spec((1, D_MODEL))],
        out_specs=_row_spec(tm, D_MODEL),
        out_shape=jax.ShapeDtypeStruct((T, D_MODEL), F32),
        compiler_params=_params(("parallel",)),
    )(x2, o_sb, rw_out, g_rw, gates, w_sb, w_rw, w_o, g_post)


def _merge_bwd(dx1, o_sb, rw_out, g_rw, gates, w_sb, w_rw, w_o, g_post, seq, tm):
    T = dx1.shape[0]

    def body(dx1_ref, osb_ref, rw_ref, g_ref, gate_ref, wsb_ref, wrw_ref, wo_ref, gp_ref,
             orw_o, mrg_o, dz_o, da_o, db_o, dgate_o, dosb_o, drw_o, dg_o, dgp_o, dbg_o):
        rw_out_v, g_rw_v = _join_heads(rw_ref), g_ref[...]
        w_sb_v, w_rw_v, w_o_v = wsb_ref[...], wrw_ref[...], wo_ref[...]
        o_rw, a, b, g1, g2, merged, z = _merge_math(osb_ref[...], rw_out_v, g_rw_v, gate_ref[...], w_sb_v, w_rw_v, w_o_v)
        gain = gp_ref[...]
        _, zn, rstd = _rms_fwd(z, gain)
        dz, dgain = _rms_bwd(dx1_ref[...], zn, rstd, gain)
        dzb = dz.astype(BF16)
        dm = _mm_nt(dzb, w_o_v)
        dab = (dm * g1).astype(BF16)
        dbb = (dm * g2).astype(BF16)
        dgate = jnp.concatenate([dm * a * g1 * (1.0 - g1), dm * b * g2 * (1.0 - g2)], axis=1)
        do_rw = _mm_nt(dbb, w_rw_v)
        orw_o[...] = o_rw
        mrg_o[...] = merged
        dz_o[...] = dzb
        da_o[...] = dab
        db_o[...] = dbb
        dgate_o[...] = dgate.astype(BF16)
        dosb_o[...] = _mm_nt(dab, w_sb_v).astype(BF16)
        _split_heads(do_rw * g_rw_v, drw_o)
        dg_o[...] = do_rw * rw_out_v
        first = pl.program_id(0) == 0
        _acc_out(dgp_o, dgain, first)
        _acc_out(dbg_o, jnp.sum(dgate, axis=0, keepdims=True), first)

    acc = lambda n: pl.BlockSpec((1, n), lambda i: (0, 0))
    sd = jax.ShapeDtypeStruct
    return pl.pallas_call(
        body, name="merge_bwd", grid=(T // tm,),
        in_specs=[_row_spec(tm, D_MODEL), _row_spec(tm, WIDTH), _head_spec(tm, seq), _row_spec(tm, WIDTH),
                  _row_spec(tm, GATE_COLS), _const_spec((WIDTH, D_MODEL)), _const_spec((WIDTH, D_MODEL)),
                  _const_spec((D_MODEL, D_MODEL)), _const_spec((1, D_MODEL))],
        out_specs=[_row_spec(tm, WIDTH), _row_spec(tm, D_MODEL), _row_spec(tm, D_MODEL), _row_spec(tm, D_MODEL),
                   _row_spec(tm, D_MODEL), _row_spec(tm, GATE_COLS), _row_spec(tm, WIDTH), _head_spec(tm, seq),
                   _row_spec(tm, WIDTH), acc(D_MODEL), acc(GATE_COLS)],
        out_shape=[sd((T, WIDTH), BF16), sd((T, D_MODEL), BF16), sd((T, D_MODEL), BF16), sd((T, D_MODEL), BF16),
                   sd((T, D_MODEL), BF16), sd((T, GATE_COLS), BF16), sd((T, WIDTH), BF16),
                   sd((T // seq, HEADS, seq, HEAD_DIM), F32), sd((T, WIDTH), F32), sd((1, D_MODEL), F32),
                   sd((1, GATE_COLS), F32)],
        compiler_params=_params(("arbitrary",)),
    )(dx1, o_sb, rw_out, g_rw, gates, w_sb, w_rw, w_o, g_post)


def _ffn(x1, target, g_pre, g_post, w_gate, w_up, w_down, tm):
    T = x1.shape[0]

    def body(x1_ref, tgt_ref, gpre_ref, gpost_ref, wg_ref, wu_ref, wd_ref,
             loss_o, dx1_o, h_o, dgate_o, dup_o, act_o, df_o, dgpre_o, dgpost_o):
        x1v = x1_ref[...]
        gpre, gpost = gpre_ref[...], gpost_ref[...]
        wg, wu, wd = wg_ref[...], wu_ref[...], wd_ref[...]
        hn, xn1, rstd1 = _rms_fwd(x1v, gpre)
        h = hn.astype(BF16)
        gate = _mm(h, wg)
        up = _mm(h, wu)
        sg = jax.nn.sigmoid(gate)
        act = (gate * sg * up).astype(BF16)
        f = _mm(act, wd)
        fo, fn, rstd2 = _rms_fwd(f, gpost)
        diff = x1v + fo - tgt_ref[...]
        dy = diff * (1.0 / D_MODEL)
        df, dgpost = _rms_bwd(dy, fn, rstd2, gpost)
        dfb = df.astype(BF16)
        dact = _mm_nt(dfb, wd)
        dup = (dact * gate * sg).astype(BF16)
        dgate = (dact * up * (sg * (1.0 + gate * (1.0 - sg)))).astype(BF16)
        dh = _mm_nt(dgate, wg) + _mm_nt(dup, wu)
        dxn, dgpre = _rms_bwd(dh, xn1, rstd1, gpre)
        dx1_o[...] = dy + dxn
        h_o[...] = h
        dgate_o[...] = dgate
        dup_o[...] = dup
        act_o[...] = act
        df_o[...] = dfb
        first = pl.program_id(0) == 0
        part = jnp.sum(jnp.sum(diff * diff, axis=1, keepdims=True), axis=0, keepdims=True) * (0.5 / D_MODEL)
        _acc_out(loss_o, jnp.broadcast_to(part, (8, 128)), first)
        _acc_out(dgpre_o, dgpre, first)
        _acc_out(dgpost_o, dgpost, first)

    acc = lambda r, n: pl.BlockSpec((r, n), lambda i: (0, 0))
    sd = jax.ShapeDtypeStruct
    return pl.pallas_call(
        body, name="ffn", grid=(T // tm,),
        in_specs=[_row_spec(tm, D_MODEL), _row_spec(tm, D_MODEL), _const_spec((1, D_MODEL)), _const_spec((1, D_MODEL)),
                  _const_spec((D_MODEL, D_FF)), _const_spec((D_MODEL, D_FF)), _const_spec((D_FF, D_MODEL))],
        out_specs=[acc(8, 128), _row_spec(tm, D_MODEL), _row_spec(tm, D_MODEL), _row_spec(tm, D_FF), _row_spec(tm, D_FF),
                   _row_spec(tm, D_FF), _row_spec(tm, D_MODEL), acc(1, D_MODEL), acc(1, D_MODEL)],
        out_shape=[sd((8, 128), F32), sd((T, D_MODEL), F32), sd((T, D_MODEL), BF16), sd((T, D_FF), BF16),
                   sd((T, D_FF), BF16), sd((T, D_FF), BF16), sd((T, D_MODEL), BF16), sd((1, D_MODEL), F32),
                   sd((1, D_MODEL), F32)],
        compiler_params=_params(("arbitrary",)),
    )(x1, target, g_pre, g_post, w_gate, w_up, w_down)


def _local_step(x, target, sm, wt, late=None):
    bl, seq, _ = x.shape
    T = bl * seq
    tm = min(ROW_TILE, T)
    x2 = x.reshape(T, D_MODEL)
    tgt2 = target.reshape(T, D_MODEL)
    h, qkv, prw, gates = _in_proj_fwd(x2, sm["norm_mix_pre"], wt["w_in"], sm["b_gate"], tm)
    if late is None:
        o_sb, lsum, sb_stop = _sb_fwd(qkv, bl, seq)
    else:
        o_sb, lsum, sb_stop, gathered = _sb_fwd(qkv, bl, seq, late)
        wt = {**wt, **_unpack_gathered(gathered, slice(_EARLY, None))}
    prep_params = [sm["mu_rw"], sm["w0"], wt["w_up"].astype(F32), sm["a0"], wt["a_up"].astype(F32),
                   wt["g_up"].astype(F32), sm["k_k"], sm["k_a"]]
    prep = _rw_prep_fwd(prw, prep_params, seq, tm)
    by_head = lambda t: t.reshape(bl, HEADS, seq, HEAD_DIM)
    seqs = [t.reshape(bl * HEADS, seq, HEAD_DIM) for t in prep[:6]]
    g_rw = prep[6]
    lnw, lnb, rk = (sm[n].reshape(HEADS, 1, HEAD_DIM) for n in ("lnx_w", "lnx_b", "r_k"))
    rw_out_h, states = _wkv_fwd(seqs, lnw, lnb, rk)
    rw_out = by_head(rw_out_h)
    x1 = _merge_fwd(x2, o_sb, rw_out, g_rw, gates, wt["w_sb_out"], wt["w_rw_out"], wt["w_o"], sm["norm_mix_post"],
                    seq, tm)
    (loss_part, dx1, h2, dffg, dffu, act, dff, d_nfpre, d_nfpost) = _ffn(
        x1, tgt2, sm["norm_ffn_pre"], sm["norm_ffn_post"], wt["w_ffn_gate"], wt["w_ffn_up"], wt["w_ffn_down"], tm)
    (o_rw, merged, dz, da, db, dgate, do_sb, d_rw_out, d_g_rw, d_npost, d_bgate) = _merge_bwd(
        dx1, o_sb, rw_out, g_rw, gates, wt["w_sb_out"], wt["w_rw_out"], wt["w_o"], sm["norm_mix_post"], seq, tm)
    gw = {
        "w_sb_out": _grad_w(o_sb, da, "gw_sb_out"), "w_rw_out": _grad_w(o_rw, db, "gw_rw_out"),
        "w_o": _grad_w(merged, dz, "gw_o"),
        "w_ffn_gate": _grad_w(h2, dffg, "gw_ffn_gate"), "w_ffn_up": _grad_w(h2, dffu, "gw_ffn_up"),
        "w_ffn_down": _grad_w(act, dff, "gw_ffn_down"),
    }
    dqkv = jnp.concatenate(_sb_bwd(qkv, do_sb, lsum, sb_stop, bl, seq), axis=1)
    ride = None if late is None else _pack_full_grads(gw, slice(_EARLY, None)).astype(BF16)
    wkv_g = _wkv_bwd(seqs, states, d_rw_out.reshape(bl * HEADS, seq, HEAD_DIM), lnw, lnb, rk, ride)
    late_grads = None if late is None else (ride, wkv_g[9])
    cts = [by_head(t) for t in wkv_g[:6]] + [d_g_rw]
    prep_g = _rw_prep_bwd(prw, prep_params, cts, seq, tm)
    dprw = prep_g[0]
    d_mu, d_w0, d_wup, d_a0, d_aup, d_gup, d_kk, d_ka = prep_g[1:]
    grad_x, d_npre = _in_proj_bwd(x2, sm["norm_mix_pre"], dx1, dqkv, dprw, dgate, wt["w_in"], tm)
    gw = {
        **gw,
        "w_in": jnp.concatenate([_grad_w(h, dqkv, "gw_in_qkv"), _grad_w(h, dprw, "gw_in_rw"), _grad_w(h, dgate, "gw_in_gate")], axis=1),
        "w_up": d_wup, "a_up": d_aup, "g_up": d_gup,
    }
    gs = {
        "norm_mix_pre": d_npre, "b_gate": d_bgate, "mu_rw": d_mu, "w0": d_w0, "a0": d_a0, "k_k": d_kk, "k_a": d_ka,
        "r_k": wkv_g[8].reshape(1, WIDTH), "lnx_w": wkv_g[6].reshape(1, WIDTH), "lnx_b": wkv_g[7].reshape(1, WIDTH),
        "norm_mix_post": d_npost, "norm_ffn_pre": d_nfpre, "norm_ffn_post": d_nfpost,
    }
    return loss_part, grad_x.reshape(x.shape), gw, gs, late_grads


_SHARDED = [("w_in", 1, (D_MODEL, (SB_COLS + RW_COLS + GATE_COLS) // N_DEV)), ("w_up", 1, (W_LORA, WIDTH // N_DEV)),
            ("a_up", 1, (A_LORA, WIDTH // N_DEV)), ("g_up", 1, (G_LORA, WIDTH // N_DEV)),
            ("w_sb_out", 1, (WIDTH, D_MODEL // N_DEV)), ("w_rw_out", 1, (WIDTH, D_MODEL // N_DEV)),
            ("w_o", 0, (D_MODEL // N_DEV, D_MODEL)), ("w_ffn_gate", 1, (D_MODEL, D_FF // N_DEV)),
            ("w_ffn_up", 1, (D_MODEL, D_FF // N_DEV)), ("w_ffn_down", 0, (D_FF // N_DEV, D_MODEL))]
_LANES = 128
_PACK_ROWS = [s[0] * s[1] // _LANES for _, _, s in _SHARDED]
_PACK_TOTAL = sum(_PACK_ROWS)
_SMALL = [("norm_mix_pre", D_MODEL), ("b_gate", GATE_COLS), ("mu_rw", RW_COLS), ("w0", WIDTH), ("a0", WIDTH),
          ("k_k", WIDTH), ("k_a", WIDTH), ("r_k", WIDTH), ("lnx_w", WIDTH), ("lnx_b", WIDTH),
          ("norm_mix_post", D_MODEL), ("norm_ffn_pre", D_MODEL), ("norm_ffn_post", D_MODEL)]
_SMALL_ROWS = 96


def _pack_shards(shards, dtype):
    return jnp.concatenate([shards[n].astype(dtype).reshape(-1, _LANES) for n, _, _ in _SHARDED], axis=0)


def _unpack_shards(packed, which):
    out, r0 = {}, 0
    for (n, _, shp), rows in zip(_SHARDED[which], _PACK_ROWS[which]):
        out[n] = packed[r0:r0 + rows].reshape(shp)
        r0 += rows
    return out


_EARLY = 4
_EARLY_ROWS = sum(_PACK_ROWS[:_EARLY])


def _unpack_gathered(g, which):
    out, r0 = {}, 0
    for (n, axis, shp), rows in zip(_SHARDED[which], _PACK_ROWS[which]):
        blk = g[:, r0:r0 + rows].reshape((N_DEV,) + shp)
        out[n] = blk.reshape(N_DEV * shp[0], shp[1]) if axis == 0 else blk.transpose(1, 0, 2).reshape(shp[0], N_DEV * shp[1])
        r0 += rows
    return out


def _pack_full_grads(gw, which):
    parts = []
    for n, axis, shp in _SHARDED[which]:
        g = gw[n]
        blk = g.reshape((N_DEV,) + shp) if axis == 0 else g.reshape(shp[0], N_DEV, shp[1]).transpose(1, 0, 2)
        parts.append(blk.reshape(N_DEV, -1, _LANES))
    return jnp.concatenate(parts, axis=1)


def _pack_small(vals, extra=None):
    used = sum(sz for _, sz in _SMALL)
    tail = jnp.zeros((1, _SMALL_ROWS * _LANES - used), F32).at[0, 0].set(extra)
    return jnp.concatenate([vals[n].reshape(1, -1) for n, _ in _SMALL] + [tail], axis=1)


_ANY = pl.BlockSpec(memory_space=pl.ANY)


def _all_gather(block):
    rows, lanes = block.shape

    def body(x_ref, out_ref, send_sems, recv_sems, local_sem):
        start, forward, finish = _gather_steps(x_ref, out_ref, send_sems, recv_sems, local_sem)
        start()
        forward()
        finish()

    return pl.pallas_call(
        body, name="all_gather_weights", in_specs=[_ANY], out_specs=_ANY,
        out_shape=jax.ShapeDtypeStruct((N_DEV, rows, lanes), block.dtype), scratch_shapes=_GATHER_SEMS,
    )(block)


def _scatter_steps(pack_ref, got_ref, send_sems, recv_sems):
    x, y, c = lax.axis_index("x"), lax.axis_index("y"), lax.axis_index("c")

    def copies():
        out = []
        for k in range(1, N_DEV):
            px, py, pc = x ^ (k >> 2), y ^ ((k >> 1) & 1), c ^ (k & 1)
            out.append(pltpu.make_async_remote_copy(
                src_ref=pack_ref.at[4 * px + 2 * py + pc], dst_ref=got_ref.at[k - 1], send_sem=send_sems.at[k - 1],
                recv_sem=recv_sems.at[k - 1], device_id=(px, py, pc), device_id_type=MESH))
        return out

    def start():
        for cp in copies():
            cp.start()

    def finish():
        for cp in copies():
            cp.wait_recv()
        for cp in copies():
            cp.wait_send()

    return start, finish


def _sum_direct(pack, got, me):
    _, rows, lanes = pack.shape
    tr = _pick_rows(rows)

    def body(me_ref, own_ref, got_ref, o_ref):
        total = own_ref[...].astype(F32)
        for k in range(N_DEV - 1):
            total = total + got_ref[k].astype(F32)
        o_ref[...] = total

    return pl.pallas_call(
        body, name="sum_direct",
        grid_spec=pltpu.PrefetchScalarGridSpec(
            num_scalar_prefetch=1, grid=(rows // tr,),
            in_specs=[pl.BlockSpec((None, tr, lanes), lambda i, me_ref: (me_ref[0], i, 0)),
                      pl.BlockSpec((N_DEV - 1, tr, lanes), lambda i, me_ref: (0, i, 0))],
            out_specs=pl.BlockSpec((tr, lanes), lambda i, me_ref: (i, 0))),
        out_shape=jax.ShapeDtypeStruct((rows, lanes), F32),
        compiler_params=_params(("parallel",)),
    )(me, pack, got)


def _pick_rows(rows, cap=2048):
    return max(t for t in range(16, cap + 1, 16) if rows % t == 0)


_GATHER_SEMS = [pltpu.SemaphoreType.DMA((7,)), pltpu.SemaphoreType.DMA((7,)), pltpu.SemaphoreType.DMA]


def _gather_steps(x_ref, out_ref, send_sems, recv_sems, local_sem):
    x, y, c = lax.axis_index("x"), lax.axis_index("y"), lax.axis_index("c")
    me, sibling = (x, y, c), (x, y, 1 - c)
    chips = [(1 - x, y), (x, 1 - y), (1 - x, 1 - y)]

    def slot(px, py, pc):
        return out_ref.at[4 * px + 2 * py + pc]

    def copy(k, blk, to, src=None):
        return pltpu.make_async_remote_copy(
            src_ref=slot(*blk) if src is None else src, dst_ref=slot(*blk),
            send_sem=send_sems.at[k], recv_sem=recv_sems.at[k], device_id=to, device_id_type=MESH)

    def first():
        return [copy(0, me, sibling, src=x_ref)] + [copy(1 + j, me, (*chip, c), src=x_ref) for j, chip in enumerate(chips)]

    def passed():
        return [copy(4 + j, (*chip, c), sibling) for j, chip in enumerate(chips)]

    def start():
        pltpu.make_async_copy(x_ref, slot(*me), local_sem).start()
        for cp in first():
            cp.start()

    def forward():
        for j, (chip, cp) in enumerate(zip(chips, passed())):
            copy(1 + j, (*chip, c), me).wait_recv()
            cp.start()

    def finish():
        copy(0, sibling, me).wait_recv()
        for j, chip in enumerate(chips):
            copy(4 + j, (*chip, 1 - c), me).wait_recv()
        for cp in first() + passed():
            cp.wait_send()
        pltpu.make_async_copy(x_ref, slot(*me), local_sem).wait()

    return start, forward, finish


def _exchange_core(pack, small):
    _, _, rows, lanes = pack.shape

    def body(pack_ref, small_ref, got_ref, parts_ref, send_sems, recv_sems, s_send, s_recv, local_sem):
        x, y, c = lax.axis_index("x"), lax.axis_index("y"), lax.axis_index("c")
        sibling = (x, y, 1 - c)
        me = 4 * x + 2 * y + c
        mine = pltpu.make_async_copy(small_ref, parts_ref.at[me], local_sem)
        mine.start()
        big = [pltpu.make_async_remote_copy(
            src_ref=pack_ref.at[1 - c, j], dst_ref=got_ref.at[j], send_sem=send_sems.at[j], recv_sem=recv_sems.at[j],
            device_id=sibling, device_id_type=MESH) for j in range(4)]
        for cp in big:
            cp.start()
        others = [(k, (x ^ (k >> 2), y ^ ((k >> 1) & 1), c ^ (k & 1))) for k in range(1, N_DEV)]
        tiny = [pltpu.make_async_remote_copy(
            src_ref=small_ref, dst_ref=parts_ref.at[me], send_sem=s_send.at[k], recv_sem=s_recv.at[k],
            device_id=to, device_id_type=MESH) for k, to in others]
        for cp in tiny:
            cp.start()
        for cp in big:
            cp.wait_recv()
        for (k, (px, py, pc)), cp in zip(others, tiny):
            pltpu.make_async_remote_copy(
                src_ref=small_ref, dst_ref=parts_ref.at[4 * px + 2 * py + pc], send_sem=s_send.at[k],
                recv_sem=s_recv.at[k], device_id=(px, py, pc), device_id_type=MESH).wait_recv()
        for cp in big + tiny:
            cp.wait_send()
        mine.wait()

    return pl.pallas_call(
        body, name="exchange_core", in_specs=[_ANY, _ANY], out_specs=[_ANY, _ANY],
        out_shape=[jax.ShapeDtypeStruct((4, rows, lanes), F32), jax.ShapeDtypeStruct((N_DEV,) + small.shape, F32)],
        scratch_shapes=[pltpu.SemaphoreType.DMA((4,)), pltpu.SemaphoreType.DMA((4,)), pltpu.SemaphoreType.DMA((N_DEV,)),
                        pltpu.SemaphoreType.DMA((N_DEV,)), pltpu.SemaphoreType.DMA],
    )(pack, small)


def _add_core_parts(pack, got, core):
    _, _, rows, lanes = pack.shape
    tr = _pick_rows(rows)

    def body(core_ref, a_ref, b_ref, o_ref):
        o_ref[...] = (a_ref[...] + b_ref[...]).astype(BF16)

    return pl.pallas_call(
        body, name="add_core_parts",
        grid_spec=pltpu.PrefetchScalarGridSpec(
            num_scalar_prefetch=1, grid=(4, rows // tr),
            in_specs=[pl.BlockSpec((None, None, tr, lanes), lambda j, i, core_ref: (core_ref[0], j, i, 0)),
                      pl.BlockSpec((None, tr, lanes), lambda j, i, core_ref: (j, i, 0))],
            out_specs=pl.BlockSpec((None, tr, lanes), lambda j, i, core_ref: (j, i, 0))),
        out_shape=jax.ShapeDtypeStruct((4, rows, lanes), BF16),
        compiler_params=_params(("parallel", "parallel")),
    )(core, pack, got)


def _exchange_chips(chip_sums):
    _, rows, lanes = chip_sums.shape

    def body(src_ref, got_ref, send_sems, recv_sems):
        x, y, c = lax.axis_index("x"), lax.axis_index("y"), lax.axis_index("c")
        flips = [(1, 0), (0, 1), (1, 1)]
        copies = []
        for k, (fx, fy) in enumerate(flips):
            px, py = x ^ fx, y ^ fy
            copies.append(pltpu.make_async_remote_copy(
                src_ref=src_ref.at[2 * px + py], dst_ref=got_ref.at[k], send_sem=send_sems.at[k],
                recv_sem=recv_sems.at[k], device_id=(px, py, c), device_id_type=MESH))
        for cp in copies:
            cp.start()
        for cp in copies:
            cp.wait_recv()
        for cp in copies:
            cp.wait_send()

    return pl.pallas_call(
        body, name="exchange_chips", in_specs=[_ANY], out_specs=_ANY,
        out_shape=jax.ShapeDtypeStruct((3, rows, lanes), chip_sums.dtype),
        scratch_shapes=[pltpu.SemaphoreType.DMA((3,)), pltpu.SemaphoreType.DMA((3,))],
    )(chip_sums)


def _sum_chip_parts(chip_sums, got, chip):
    _, rows, lanes = chip_sums.shape
    tr = _pick_rows(rows)

    def body(chip_ref, own_ref, got_ref, o_ref):
        f32 = lambda t: t.astype(F32)
        o_ref[...] = ((f32(own_ref[...]) + f32(got_ref[0])) + f32(got_ref[1])) + f32(got_ref[2])

    return pl.pallas_call(
        body, name="sum_chip_parts",
        grid_spec=pltpu.PrefetchScalarGridSpec(
            num_scalar_prefetch=1, grid=(rows // tr,),
            in_specs=[pl.BlockSpec((None, tr, lanes), lambda i, chip_ref: (chip_ref[0], i, 0)),
                      pl.BlockSpec((3, tr, lanes), lambda i, chip_ref: (0, i, 0))],
            out_specs=pl.BlockSpec((tr, lanes), lambda i, chip_ref: (i, 0))),
        out_shape=jax.ShapeDtypeStruct((rows, lanes), F32),
        compiler_params=_params(("parallel",)),
    )(chip, chip_sums, got)


def _adamw_math(w, g, m, v):
    m = ADAM_B1 * m + (1.0 - ADAM_B1) * g
    v = ADAM_B2 * v + (1.0 - ADAM_B2) * (g * g)
    m_hat = m / (1.0 - ADAM_B1 ** ADAM_STEP)
    v_hat = v / (1.0 - ADAM_B2 ** ADAM_STEP)
    return -ADAM_LR * (m_hat / (jnp.sqrt(v_hat) + ADAM_EPS) + ADAM_WD * w), m, v


def _adamw(w, g, m, v, name):
    rows, cols = w.shape
    tr = 256 if rows % 256 == 0 and rows * cols > 2 ** 19 else rows

    def body(w_ref, g_ref, m_ref, v_ref, d_o, m_o, v_o):
        d_o[...], m_o[...], v_o[...] = _adamw_math(w_ref[...], g_ref[...], m_ref[...], v_ref[...])

    spec = pl.BlockSpec((tr, cols), lambda i: (i, 0))
    return pl.pallas_call(
        body, name=name, grid=(rows // tr,), in_specs=[spec] * 4, out_specs=[spec] * 3,
        out_shape=[jax.ShapeDtypeStruct((rows, cols), F32)] * 3, compiler_params=_params(("parallel",)),
    )(w, g, m, v)


def _adamw_small(parts, ws, ms, vs):
    k = len(_SMALL)

    def body(p_ref, *refs):
        w_refs, m_refs, v_refs = refs[:k], refs[k:2 * k], refs[2 * k:3 * k]
        outs = refs[3 * k:]
        g = p_ref[0]
        for d in range(1, N_DEV):
            g = g + p_ref[d]
        o = 0
        for i, (_, n) in enumerate(_SMALL):
            gp = g[:, o:o + n]
            outs[1 + i][...] = gp
            outs[1 + k + i][...], outs[1 + 2 * k + i][...], outs[1 + 3 * k + i][...] = _adamw_math(
                w_refs[i][...], gp, m_refs[i][...], v_refs[i][...])
            o += n
        outs[0][...] = g[:, o:o + _LANES]

    shapes = [jax.ShapeDtypeStruct((1, n), F32) for _, n in _SMALL]
    out = pl.pallas_call(
        body, name="adamw_small", out_shape=[jax.ShapeDtypeStruct((1, _LANES), F32)] + shapes * 4,
        compiler_params=_params(),
    )(parts, *ws, *ms, *vs)
    return out[0][0, 0], out[1:1 + k], out[1 + k:1 + 2 * k], out[1 + 2 * k:1 + 3 * k], out[1 + 3 * k:]


_WEIGHT_NAMES = ['norm_mix_pre', 'w_in', 'b_gate', 'mu_rw', 'w0', 'w_up', 'a0', 'a_up', 'g_up', 'k_k', 'k_a', 'r_k',
                 'lnx_w', 'lnx_b', 'w_sb_out', 'w_rw_out', 'w_o', 'norm_mix_post', 'norm_ffn_pre', 'w_ffn_gate',
                 'w_ffn_up', 'w_ffn_down', 'norm_ffn_post']


def _step(x, target, w, m, v):
    sharded = [n for n, _, _ in _SHARDED]
    sm = {n: w[n].reshape(1, -1) for n, _ in _SMALL}
    own = {n: w[n][0] for n in sharded}
    packed = _pack_shards(own, BF16)
    wt = _unpack_gathered(_all_gather(packed[:_EARLY_ROWS]), slice(0, _EARLY))
    loss_part, grad_x, gw, gs, (late_pack, late_got) = _local_step(x, target, sm, wt, packed[_EARLY_ROWS:])

    cx, cy, cc = lax.axis_index("x"), lax.axis_index("y"), lax.axis_index("c")
    core = jnp.reshape(cc, (1,)).astype(jnp.int32)
    chip = jnp.reshape(2 * cx + cy, (1,)).astype(jnp.int32)
    me = jnp.reshape(4 * cx + 2 * cy + cc, (1,)).astype(jnp.int32)
    pack = _pack_full_grads(gw, slice(0, _EARLY)).reshape(4, 2, _EARLY_ROWS, _LANES).transpose(1, 0, 2, 3)
    got_core, small_parts = _exchange_core(pack, _pack_small(gs, loss_part[0, 0]))
    chip_sums = _add_core_parts(pack, got_core, core)
    early = _sum_chip_parts(chip_sums, _exchange_chips(chip_sums), chip)
    g_sh = {**_unpack_shards(early, slice(0, _EARLY)),
            **_unpack_shards(_sum_direct(late_pack, late_got, me), slice(_EARLY, None))}

    row = lambda t: [t[n].reshape(1, -1) for n, _ in _SMALL]
    loss, *by_kind = _adamw_small(small_parts, row(w), row(m), row(v))
    g_s, d_s, m_s, v_s = ({n: t[i] for i, (n, _) in enumerate(_SMALL)} for t in by_kind)

    grads, deltas, new_m, new_v = {}, {}, {}, {}
    for n in _WEIGHT_NAMES:
        if n in g_sh:
            d_, m_, v_ = _adamw(own[n], g_sh[n], m[n][0], v[n][0], "adamw_" + n)
            grads[n], deltas[n], new_m[n], new_v[n] = (t.reshape(w[n].shape) for t in (g_sh[n], d_, m_, v_))
        else:
            grads[n], deltas[n], new_m[n], new_v[n] = (t[n].reshape(w[n].shape) for t in (g_s, d_s, m_s, v_s))
    return (loss, grad_x, *[grads[n] for n in _WEIGHT_NAMES], *[deltas[n] for n in _WEIGHT_NAMES],
            *[new_m[n] for n in _WEIGHT_NAMES], *[new_v[n] for n in _WEIGHT_NAMES])


def kernel(x, norm_mix_pre, w_in, b_gate, mu_rw, w0, w_up, a0, a_up, g_up, k_k, k_a, r_k, lnx_w, lnx_b, w_sb_out, w_rw_out, w_o, norm_mix_post, norm_ffn_pre, w_ffn_gate, w_ffn_up, w_ffn_down, norm_ffn_post, loss_target, m_norm_mix_pre, m_w_in, m_b_gate, m_mu_rw, m_w0, m_w_up, m_a0, m_a_up, m_g_up, m_k_k, m_k_a, m_r_k, m_lnx_w, m_lnx_b, m_w_sb_out, m_w_rw_out, m_w_o, m_norm_mix_post, m_norm_ffn_pre, m_w_ffn_gate, m_w_ffn_up, m_w_ffn_down, m_norm_ffn_post, v_norm_mix_pre, v_w_in, v_b_gate, v_mu_rw, v_w0, v_w_up, v_a0, v_a_up, v_g_up, v_k_k, v_k_a, v_r_k, v_lnx_w, v_lnx_b, v_w_sb_out, v_w_rw_out, v_w_o, v_norm_mix_post, v_norm_ffn_pre, v_w_ffn_gate, v_w_ffn_up, v_w_ffn_down, v_norm_ffn_post):
    args = locals()
    w = {n: args[n] for n in _WEIGHT_NAMES}
    m = {n: args["m_" + n] for n in _WEIGHT_NAMES}
    v = {n: args["v_" + n] for n in _WEIGHT_NAMES}
    return _step(x, loss_target, w, m, v)
```

```python
import functools

import jax
import jax.numpy as jnp
from jax import lax
from jax.experimental import pallas as pl
from jax.experimental.pallas import tpu as pltpu

F32 = jnp.float32
BF16 = jnp.bfloat16

D_MODEL = 1024
HEADS = 8
HEAD_DIM = 64
WIDTH = HEADS * HEAD_DIM
W_LORA, A_LORA, G_LORA = 64, 64, 128
SB_COLS = 3 * WIDTH
RW_COLS = 3 * WIDTH + W_LORA + A_LORA + G_LORA
GATE_COLS = 2 * D_MODEL
D_FF = 2816
RMS_EPS = 1e-6
GN_EPS = HEAD_DIM * 1e-5
N_DEV = 8

ADAM_LR, ADAM_B1, ADAM_B2, ADAM_EPS, ADAM_WD, ADAM_STEP = 0.001, 0.9, 0.999, 1e-08, 0.01, 10

ROW_TILE = 512
ROW_TILE_FFN = 256
SCAN_CHUNK = 64
ATT_ALIGN = 128
ATT_WINDOW = 384
ATT_Q = 128
ATT_PAIRS = 2
SB_DEAD = -104.0
SCAN_SEQS_FWD = 4
SCAN_SEQS_BWD = 2
SCAN_PASSES = 1
VMEM_LIMIT = 56 * 2 ** 20

MESH = pl.DeviceIdType.MESH


def _params(sem=None, vmem=VMEM_LIMIT):
    kw = dict(vmem_limit_bytes=vmem)
    if sem is not None:
        kw["dimension_semantics"] = sem
    return pltpu.CompilerParams(**kw)


def _const_spec(shape):
    nd = len(shape)
    return pl.BlockSpec(shape, lambda *_: (0,) * nd, pipeline_mode=pl.Buffered(1))


def _row_spec(tm, n):
    return pl.BlockSpec((tm, n), lambda i: (i, 0))


def _mm(a, b):
    return lax.dot_general(a, b, (((1,), (0,)), ((), ())), preferred_element_type=F32)


def _mm_nt(a, b):
    return lax.dot_general(a, b, (((1,), (1,)), ((), ())), preferred_element_type=F32)


def _mm_tn(a, b):
    return lax.dot_general(a, b, (((0,), (0,)), ((), ())), preferred_element_type=F32)


def _softplus(z):
    return jnp.maximum(z, 0.0) + jnp.log1p(jnp.exp(-jnp.abs(z)))


def _rms_fwd(x, gain):
    rstd = lax.rsqrt(jnp.mean(x * x, axis=-1, keepdims=True) + RMS_EPS)
    xn = x * rstd
    return xn * gain, xn, rstd


def _rms_bwd(dy, xn, rstd, gain):
    u = dy * gain
    dx = rstd * (u - xn * jnp.mean(u * xn, axis=-1, keepdims=True))
    return dx, jnp.sum(dy * xn, axis=0, keepdims=True)


def _acc_out(ref, val, first):
    @pl.when(first)
    def _():
        ref[...] = val

    @pl.when(jnp.logical_not(first))
    def _():
        ref[...] += val


_IN_COLS = SB_COLS + RW_COLS + GATE_COLS
_QKV_OF, _RW_OF, _GATE_OF = slice(0, SB_COLS), slice(SB_COLS, SB_COLS + RW_COLS), slice(SB_COLS + RW_COLS, _IN_COLS)


def _in_proj_fwd(x2, g_pre, w_in, b_gate, tm):
    T = x2.shape[0]

    def body(x_ref, g_ref, w_ref, b_ref, h_ref, qkv_ref, prw_ref, gate_ref):
        h = _rms_fwd(x_ref[...], g_ref[...])[0].astype(BF16)
        h_ref[...] = h
        qkv_ref[...] = _mm(h, w_ref[:, _QKV_OF]).astype(BF16)
        prw_ref[...] = _mm(h, w_ref[:, _RW_OF])
        gate_ref[...] = jax.nn.sigmoid(_mm(h, w_ref[:, _GATE_OF]) + b_ref[...])

    return pl.pallas_call(
        body, name="in_proj_fwd", grid=(T // tm,),
        in_specs=[_row_spec(tm, D_MODEL), _const_spec((1, D_MODEL)), _const_spec((D_MODEL, _IN_COLS)),
                  _const_spec((1, GATE_COLS))],
        out_specs=[_row_spec(tm, D_MODEL), _row_spec(tm, SB_COLS), _row_spec(tm, RW_COLS), _row_spec(tm, GATE_COLS)],
        out_shape=[jax.ShapeDtypeStruct((T, D_MODEL), BF16), jax.ShapeDtypeStruct((T, SB_COLS), BF16),
                   jax.ShapeDtypeStruct((T, RW_COLS), F32), jax.ShapeDtypeStruct((T, GATE_COLS), F32)],
        compiler_params=_params(("parallel",)),
    )(x2, g_pre, w_in, b_gate)


def _in_proj_bwd(x2, g_pre, dx1, dqkv, dprw, dgate, w_in, tm):
    T = x2.shape[0]

    def body(x_ref, g_ref, dx1_ref, dq_ref, dr_ref, dg_ref, w_ref, gx_ref, dgain_ref):
        dh = (_mm_nt(dq_ref[...], w_ref[:, _QKV_OF]) + _mm_nt(dr_ref[...], w_ref[:, _RW_OF])
              + _mm_nt(dg_ref[...], w_ref[:, _GATE_OF]))
        gain = g_ref[...]
        _, xn, rstd = _rms_fwd(x_ref[...], gain)
        dx, dgain = _rms_bwd(dh, xn, rstd, gain)
        gx_ref[...] = dx1_ref[...] + dx
        _acc_out(dgain_ref, dgain, pl.program_id(0) == 0)

    return pl.pallas_call(
        body, name="in_proj_bwd", grid=(T // tm,),
        in_specs=[_row_spec(tm, D_MODEL), _const_spec((1, D_MODEL)), _row_spec(tm, D_MODEL), _row_spec(tm, SB_COLS),
                  _row_spec(tm, RW_COLS), _row_spec(tm, GATE_COLS), _const_spec((D_MODEL, _IN_COLS))],
        out_specs=[_row_spec(tm, D_MODEL), pl.BlockSpec((1, D_MODEL), lambda i: (0, 0))],
        out_shape=[jax.ShapeDtypeStruct((T, D_MODEL), F32), jax.ShapeDtypeStruct((1, D_MODEL), F32)],
        compiler_params=_params(("arbitrary",)),
    )(x2, g_pre, dx1, dqkv, dprw, dgate, w_in)


def _pick_tile(n, cap):
    best = None
    for t in range(128, min(n, cap) + 1, 128):
        if n % t == 0:
            best = t
    return n if best is None else best


def _grad_w(a, b, name, dtype=F32):
    T, K = a.shape
    N = b.shape[1]
    tk, tn, tt = _pick_tile(K, 1408), _pick_tile(N, 2048), min(T, 2048)
    steps = T // tt

    def body(a_ref, b_ref, o_ref, *acc):
        t = pl.program_id(2)
        part = _mm_tn(a_ref[...], b_ref[...])
        if not acc:
            _acc_out(o_ref, part, t == 0)
        else:
            _acc_out(acc[0], part, t == 0)

            @pl.when(t == steps - 1)
            def _():
                o_ref[...] = acc[0][...].astype(dtype)

    return pl.pallas_call(
        body, name=name, grid=(K // tk, N // tn, steps),
        in_specs=[pl.BlockSpec((tt, tk), lambda i, j, t: (t, i)), pl.BlockSpec((tt, tn), lambda i, j, t: (t, j))],
        out_specs=pl.BlockSpec((tk, tn), lambda i, j, t: (i, j)),
        out_shape=jax.ShapeDtypeStruct((K, N), dtype),
        scratch_shapes=[] if dtype == F32 else [pltpu.VMEM((tk, tn), F32)],
        compiler_params=_params(("parallel", "parallel", "arbitrary")),
    )(a, b)


def _tri(n, kind):
    r = lax.broadcasted_iota(jnp.int32, (n, n), 0)
    c = lax.broadcasted_iota(jnp.int32, (n, n), 1)
    return {"gt": r > c, "le": r <= c, "lt": r < c, "ge": r >= c}[kind]


def _split_mm(x, u):
    hi = x.astype(BF16)
    lo = (x - hi.astype(F32)).astype(BF16)
    return _mm(hi, u) + _mm(lo, u)


def _running_sums(x, carry, tri, kb, reverse=False):
    blocks = range(x.shape[1] // kb)
    parts = {}
    for b in (reversed(blocks) if reverse else blocks):
        piece = x[:, b * kb:(b + 1) * kb]
        parts[b] = carry + _split_mm(piece, tri)
        carry = carry + jnp.sum(piece, axis=1, keepdims=True)
    return jnp.concatenate([parts[b] for b in blocks], axis=1), carry


def _sb_valid(row0, col0, first, last, qb, kb):
    ahead = lax.broadcasted_iota(jnp.int32, (qb, kb), 1) - lax.broadcasted_iota(jnp.int32, (qb, kb), 0)
    col = lax.broadcasted_iota(jnp.int32, (1, kb), 1)
    return jnp.logical_and(ahead < row0 - col0, jnp.logical_and(col >= first - col0, col < last - col0))


def _sb_softplus(z):
    return jnp.maximum(z, 0.0) + jnp.log(1.0 + jnp.exp(-jnp.abs(z)))


_PAIR = 2 * HEAD_DIM
_PAIRS = WIDTH // _PAIR


def _first_head_lanes():
    return lax.broadcasted_iota(jnp.int32, (1, _PAIR), 1) < HEAD_DIM


def _per_head(t, first_head):
    zero = jnp.zeros_like(t)
    return jnp.where(first_head, t, zero), jnp.where(first_head, zero, t)


def _sb_fwd(qkv, bl, seq, ride=None):
    qb, win, kb = min(ATT_Q, seq), min(ATT_WINDOW, seq), ATT_ALIGN
    nq = seq // qb
    nh, width, groups = 2 * ATT_PAIRS, ATT_PAIRS * _PAIR, _PAIRS // ATT_PAIRS
    pair_of = lambda h: slice((h // 2) * _PAIR, (h // 2 + 1) * _PAIR)

    steps = bl * groups
    pass_on_at = (5 * steps) // 8

    def body(q_ref, k_ref, v_ref, *rest):
        g = pl.program_id(0) * groups + pl.program_id(1)
        if ride is None:
            o_ref, l_ref, stop_ref = rest
        else:
            ride_ref, o_ref, l_ref, stop_ref, gathered_ref, *sems = rest
            start, forward, finish = _gather_steps(ride_ref, gathered_ref, *sems)
            pl.when(g == 0)(start)
            pl.when(g == pass_on_at)(forward)
        first_head = _first_head_lanes()
        u_after = _tri(kb, "gt").astype(BF16)

        def qblock(i, _):
            rows = pl.ds(pl.multiple_of(i * qb, qb), qb)
            qs = q_ref[rows, :] * (HEAD_DIM ** -0.5)
            qh = [_per_head(qs[:, pair_of(h)], first_head)[h % 2] for h in range(nh)]

            def live(carry):
                return jnp.logical_and(carry[0] > 0, carry[3] > 0)

            def window(carry):
                hi, accs, cs, _ = carry
                lo = pl.multiple_of(jnp.maximum(hi - win, 0), kb)
                cols = pl.ds(lo, win)
                kv, vv = k_ref[cols, :], v_ref[cols, :]
                valid = _sb_valid(i * qb, lo, lo, hi, qb, win)
                accs, cs = list(accs), list(cs)
                for h in range(nh):
                    z = _mm_nt(qh[h], kv[:, pair_of(h)])
                    sp = _sb_softplus(z)
                    spm = jnp.where(valid, sp, 0.0)
                    after, cs[h] = _running_sums(spm, cs[h], u_after, kb, reverse=True)
                    w = jnp.where(valid, jnp.exp(z - sp - after), 0.0)
                    accs[h] = accs[h] + _mm(w.astype(BF16), vv[:, pair_of(h)])
                alive = functools.reduce(jnp.minimum, [jnp.min(c) for c in cs]) < -SB_DEAD
                return lo, tuple(accs), tuple(cs), alive.astype(jnp.int32)

            zero_acc, zero_c = jnp.zeros((qb, _PAIR), F32), jnp.zeros((qb, 1), F32)
            lo, accs, cs, _ = lax.while_loop(
                live, window, ((i + 1) * qb, (zero_acc,) * nh, (zero_c,) * nh, jnp.int32(1)))
            for pp in range(ATT_PAIRS):
                o_ref[rows, pp * _PAIR:(pp + 1) * _PAIR] = jnp.where(
                    first_head, accs[2 * pp], accs[2 * pp + 1]).astype(BF16)
            for h in range(nh):
                l_ref[h, rows, :] = cs[h]
            stop_ref[g, i] = lo
            return 0

        lax.fori_loop(0, nq, qblock, 0)
        if ride is not None:
            pl.when(g == steps - 1)(finish)

    col = lambda off: pl.BlockSpec((seq, width), lambda b, p: (b, off + p))
    in_specs = [col(0), col(groups), col(2 * groups)]
    out_specs = [col(0), pl.BlockSpec((None, nh, seq, 1), lambda b, p: (b, p, 0, 0)), pl.BlockSpec(memory_space=pltpu.SMEM)]
    out_shape = [jax.ShapeDtypeStruct((bl * seq, WIDTH), BF16), jax.ShapeDtypeStruct((bl, HEADS, seq, 1), F32),
                 jax.ShapeDtypeStruct((bl * groups, nq), jnp.int32)]
    if ride is None:
        return pl.pallas_call(body, name="sb_fwd", grid=(bl, groups), in_specs=in_specs, out_specs=out_specs,
                              out_shape=out_shape, compiler_params=_params(("arbitrary", "arbitrary")))(qkv, qkv, qkv)
    return pl.pallas_call(
        body, name="sb_fwd", grid=(bl, groups), in_specs=in_specs + [_ANY], out_specs=out_specs + [_ANY],
        out_shape=out_shape + [jax.ShapeDtypeStruct((N_DEV,) + ride.shape, ride.dtype)], scratch_shapes=_GATHER_SEMS,
        compiler_params=_params(("arbitrary", "arbitrary")),
    )(qkv, qkv, qkv, ride)


def _sb_bwd(qkv, do, lsum, stop, bl, seq):
    qb, win, kb = min(ATT_Q, seq), min(ATT_WINDOW, seq), ATT_ALIGN
    nq = seq // qb
    nh, width, groups = 2 * ATT_PAIRS, ATT_PAIRS * _PAIR, _PAIRS // ATT_PAIRS
    pair_of = lambda h: slice((h // 2) * _PAIR, (h // 2 + 1) * _PAIR)

    def body(stop_ref, q_ref, k_ref, v_ref, do_ref, l_ref, dq_ref, dk_ref, dv_ref, dk_acc, dv_acc):
        g = pl.program_id(0) * groups + pl.program_id(1)
        first_head = _first_head_lanes()
        u_incl = _tri(kb, "le").astype(BF16)
        u_excl = _tri(kb, "lt").astype(BF16)
        dk_acc[...] = jnp.zeros_like(dk_acc)
        dv_acc[...] = jnp.zeros_like(dv_acc)

        def qblock(i, _):
            rows = pl.ds(pl.multiple_of(i * qb, qb), qb)
            qv = q_ref[rows, :]
            qs = qv * (HEAD_DIM ** -0.5)
            dob = do_ref[rows, :]
            qh = [_per_head(qs[:, pair_of(h)], first_head)[h % 2] for h in range(nh)]
            doh = [_per_head(dob[:, pair_of(h)], first_head)[h % 2] for h in range(nh)]
            ltot = [l_ref[h, rows, :] for h in range(nh)]

            first = (jnp.clip(stop_ref[g, i], 0, i * qb) // ATT_ALIGN) * ATT_ALIGN

            def window(n, carry):
                dqs, ps, es = (list(t) for t in carry)
                start = first + n * win
                lo = pl.multiple_of(jnp.minimum(start, seq - win), kb)
                cols = pl.ds(lo, win)
                kv, vv = k_ref[cols, :], v_ref[cols, :]
                valid = _sb_valid(i * qb, lo, start, seq, qb, win)
                dks, dvs = [], []
                for h in range(nh):
                    kp, vp = kv[:, pair_of(h)], vv[:, pair_of(h)]
                    z = _mm_nt(qh[h], kp)
                    sp = _sb_softplus(z)
                    spm = jnp.where(valid, sp, 0.0)
                    upto, ps[h] = _running_sums(spm, ps[h], u_incl, kb)
                    w = jnp.where(valid, jnp.exp(z - sp - (ltot[h] - upto)), 0.0)
                    e = _mm_nt(doh[h], vp) * w
                    dlf, es[h] = _running_sums(e, es[h], u_excl, kb)
                    sig = jnp.exp(z - sp)
                    dz = jnp.where(valid, e * (1.0 - sig) - dlf * sig, 0.0) * (HEAD_DIM ** -0.5)
                    dzb = dz.astype(BF16)
                    dvs.append(_mm_tn(w.astype(BF16), dob[:, pair_of(h)]))
                    dks.append(_mm_tn(dzb, qv[:, pair_of(h)]))
                    dqs[h] = dqs[h] + _mm(dzb, kp)
                for pp in range(ATT_PAIRS):
                    lanes = slice(pp * _PAIR, (pp + 1) * _PAIR)
                    dv_acc[cols, lanes] += jnp.where(first_head, dvs[2 * pp], dvs[2 * pp + 1])
                    dk_acc[cols, lanes] += jnp.where(first_head, dks[2 * pp], dks[2 * pp + 1])
                return tuple(dqs), tuple(ps), tuple(es)

            zero_q, zero_c = jnp.zeros((qb, _PAIR), F32), jnp.zeros((qb, 1), F32)
            dqs, _, _ = lax.fori_loop(0, ((i + 1) * qb - first + win - 1) // win, window,
                                      ((zero_q,) * nh, (zero_c,) * nh, (zero_c,) * nh))
            for pp in range(ATT_PAIRS):
                dq_ref[rows, pp * _PAIR:(pp + 1) * _PAIR] = jnp.where(
                    first_head, dqs[2 * pp], dqs[2 * pp + 1]).astype(BF16)
            return 0

        lax.fori_loop(0, nq, qblock, 0)
        dk_ref[...] = dk_acc[...].astype(BF16)
        dv_ref[...] = dv_acc[...].astype(BF16)

    col = lambda off: pl.BlockSpec((seq, width), lambda b, p, stop_ref: (b, off + p))
    return pl.pallas_call(
        body, name="sb_bwd",
        grid_spec=pltpu.PrefetchScalarGridSpec(
            num_scalar_prefetch=1, grid=(bl, groups),
            in_specs=[col(0), col(groups), col(2 * groups), col(0),
                      pl.BlockSpec((None, nh, seq, 1), lambda b, p, stop_ref: (b, p, 0, 0))],
            out_specs=[col(0), col(0), col(0)],
            scratch_shapes=[pltpu.VMEM((seq, width), F32), pltpu.VMEM((seq, width), F32)]),
        out_shape=[jax.ShapeDtypeStruct((bl * seq, WIDTH), BF16)] * 3,
        compiler_params=_params(("parallel", "parallel")),
    )(stop, qkv, qkv, qkv, do, lsum)


@jax.custom_vjp
def _lora_mm(x, w):
    return _mm(x.astype(BF16), w.astype(BF16))


_lora_mm.defvjp(
    lambda x, w: (_mm(x.astype(BF16), w.astype(BF16)), (x, w)),
    lambda res, ct: (_mm_nt(ct.astype(BF16), res[1].astype(BF16)), _mm_tn(res[0].astype(BF16), ct.astype(BF16))))


def _rw_prep_math(p, ps, mu, w0, w_up, a0, a_up, g_up, k_k, k_a):
    pm = p + (ps - p) * mu
    r, k, v = pm[:, :WIDTH], pm[:, WIDTH:2 * WIDTH], pm[:, 2 * WIDTH:3 * WIDTH]
    o = 3 * WIDTH
    xw, xa, xg = pm[:, o:o + W_LORA], pm[:, o + W_LORA:o + W_LORA + A_LORA], pm[:, o + W_LORA + A_LORA:]
    w_raw = w0 + _lora_mm(jnp.tanh(xw), w_up)
    lw = -jnp.exp(-_softplus(-w_raw) - 0.5)
    a = jax.nn.sigmoid(a0 + _lora_mm(xa, a_up))
    g = _lora_mm(jax.nn.sigmoid(xg), g_up)
    kk = k * k_k
    k2 = k * (1.0 + (a - 1.0) * k_a)
    return r, lw, k2, v, kk, a, g


def _shift_down(p, first_row):
    row = lax.broadcasted_iota(jnp.int32, p.shape, 0)
    return jnp.where(row == 0, first_row, pltpu.roll(p, 1, 0))


def _shift_up(p, last_row):
    row = lax.broadcasted_iota(jnp.int32, p.shape, 0)
    return jnp.where(row == p.shape[0] - 1, last_row, pltpu.roll(p, p.shape[0] - 1, 0))


_PREP_PARAM_SHAPES = [(1, RW_COLS), (1, WIDTH), (W_LORA, WIDTH), (1, WIDTH), (A_LORA, WIDTH), (G_LORA, WIDTH),
                      (1, WIDTH), (1, WIDTH)]


def _prev_rows_spec(tm):
    return pl.BlockSpec((8, RW_COLS), lambda i: (jnp.maximum(i * (tm // 8) - 1, 0), 0))


def _head_spec(tm, seq, tile_of=lambda i: i):
    per_seq = seq // tm
    return pl.BlockSpec((None, HEADS, tm, HEAD_DIM),
                        lambda i: (tile_of(i) // per_seq, 0, tile_of(i) % per_seq, 0))


def _split_heads(val, ref):
    for h in range(HEADS):
        ref[h] = val[:, h * HEAD_DIM:(h + 1) * HEAD_DIM]


def _join_heads(ref):
    return jnp.concatenate([ref[h] for h in range(HEADS)], axis=1)


def _rw_prep_fwd(prw, params, seq, tm):
    T = prw.shape[0]

    def body(p_ref, prev_ref, *rest):
        prm = [r_[...] for r_ in rest[:8]]
        outs = rest[8:]
        i = pl.program_id(0)
        first = jnp.where((i * tm) % seq == 0, 0.0, prev_ref[7:8, :])
        p = p_ref[...]
        vals = _rw_prep_math(p, _shift_down(p, first), *prm)
        for o_ref, val in zip(outs[:6], vals[:6]):
            _split_heads(val, o_ref)
        outs[6][...] = vals[6]

    by_head = jax.ShapeDtypeStruct((T // seq, HEADS, seq, HEAD_DIM), F32)
    return pl.pallas_call(
        body, name="rw_prep_fwd", grid=(T // tm,),
        in_specs=[_row_spec(tm, RW_COLS), _prev_rows_spec(tm)] + [_const_spec(s) for s in _PREP_PARAM_SHAPES],
        out_specs=[_head_spec(tm, seq)] * 6 + [_row_spec(tm, WIDTH)],
        out_shape=[by_head] * 6 + [jax.ShapeDtypeStruct((T, WIDTH), F32)],
        compiler_params=_params(("parallel",)),
    )(prw, prw, *params)


def _rw_prep_bwd(prw, params, cts, seq, tm):
    T = prw.shape[0]
    n = T // tm

    def body(p_ref, prev_ref, *rest):
        prm = [r_[...] for r_ in rest[:8]]
        ct = tuple(_join_heads(r_) for r_ in rest[8:14]) + (rest[14][...],)
        dp_ref = rest[15]
        dprm_refs = rest[16:24]
        carry = rest[24]
        step = pl.program_id(0)
        i = n - 1 - step
        first = jnp.where((i * tm) % seq == 0, 0.0, prev_ref[7:8, :])
        p = p_ref[...]
        _, vjp = jax.vjp(_rw_prep_math, p, _shift_down(p, first), *prm)
        grads = vjp(ct)
        dp, dps = grads[0], grads[1]
        nxt = jnp.where(jnp.logical_or(step == 0, ((i + 1) * tm) % seq == 0), 0.0, carry[0:1, :])
        dp_ref[...] = (dp + _shift_up(dps, nxt)).astype(BF16)
        carry[...] = dps[0:8, :]
        for ref, gval in zip(dprm_refs, grads[2:]):
            _acc_out(ref, gval, step == 0)

    rev = lambda w: pl.BlockSpec((tm, w), lambda s: (n - 1 - s, 0))
    prev = pl.BlockSpec((8, RW_COLS), lambda s: (jnp.maximum((n - 1 - s) * (tm // 8) - 1, 0), 0))
    return pl.pallas_call(
        body, name="rw_prep_bwd", grid=(n,),
        in_specs=([rev(RW_COLS), prev] + [_const_spec(s) for s in _PREP_PARAM_SHAPES]
                  + [_head_spec(tm, seq, lambda s: n - 1 - s)] * 6 + [rev(WIDTH)]),
        out_specs=[rev(RW_COLS)] + [pl.BlockSpec(s, lambda s_: (0, 0)) for s in _PREP_PARAM_SHAPES],
        out_shape=[jax.ShapeDtypeStruct((T, RW_COLS), BF16)] + [jax.ShapeDtypeStruct(s, F32) for s in _PREP_PARAM_SHAPES],
        scratch_shapes=[pltpu.VMEM((8, RW_COLS), F32)],
        compiler_params=_params(("arbitrary",)),
    )(prw, prw, *params, *cts)


def _make_bmm(passes):
    def raw(dn, a, b):
        d = lambda x, y: lax.dot_general(x, y, dn, preferred_element_type=F32)
        ah = a.astype(BF16)
        bh = b.astype(BF16)
        if passes == 1:
            return d(ah, bh)
        al = (a - ah.astype(F32)).astype(BF16)
        bl = (b - bh.astype(F32)).astype(BF16)
        return d(ah, bh) + (d(ah, bl) + d(al, bh))

    dn_nn = (((2,), (1,)), ((0,), (0,)))
    dn_nt = (((2,), (2,)), ((0,), (0,)))
    dn_tn = (((1,), (1,)), ((0,), (0,)))

    @jax.custom_vjp
    def nn(a, b):
        return raw(dn_nn, a, b)

    @jax.custom_vjp
    def nt(a, b):
        return raw(dn_nt, a, b)

    @jax.custom_vjp
    def tn(a, b):
        return raw(dn_tn, a, b)

    nn.defvjp(lambda a, b: (raw(dn_nn, a, b), (a, b)), lambda res, ct: (nt(ct, res[1]), tn(res[0], ct)))
    nt.defvjp(lambda a, b: (raw(dn_nt, a, b), (a, b)), lambda res, ct: (nn(ct, res[1]), tn(ct, res[0])))
    tn.defvjp(lambda a, b: (raw(dn_tn, a, b), (a, b)), lambda res, ct: (nt(res[1], ct), nn(res[0], ct)))

    def unit_lower_inverse(m):
        n = m.shape[-1]
        row = lax.broadcasted_iota(jnp.int32, (n, n), 0)
        col = lax.broadcasted_iota(jnp.int32, (n, n), 1)
        m16 = ((row // 16) == (col // 16)).astype(F32)
        m32 = ((row // 32) == (col // 32)).astype(F32)
        a1 = m * m16
        a2 = nn(a1, a1)
        a4 = nn(a2, a2)
        a8 = nn(a4, a4)
        inv = (row == col).astype(F32) - a1
        inv = inv + nn(inv, a2)
        inv = inv + nn(inv, a4)
        inv = inv + nn(inv, a8)
        inv = inv - nn(nn(inv, m * (m32 - m16)), inv)
        return inv - nn(nn(inv, m * (1.0 - m32)), inv)

    @jax.custom_vjp
    def inverse(m):
        return unit_lower_inverse(m)

    def inverse_fwd(m):
        inv = unit_lower_inverse(m)
        return inv, inv

    inverse.defvjp(inverse_fwd, lambda inv, ct: (-nt(tn(inv, ct), inv),))
    return nn, nt, tn, inverse


def _wkv_chunk(s0, r, lw, k, v, kk, a, lnw, lnb, rk):
    nn, nt, tn, inverse = _make_bmm(SCAN_PASSES)
    G, L, N = r.shape
    rep = lambda t: jnp.broadcast_to(t[None], (G // HEADS, HEADS, 1, N)).reshape(G, 1, N)
    kap = kk * lax.rsqrt(jnp.maximum(jnp.sum(kk * kk, axis=-1, keepdims=True), 1e-24))
    b = a * kap
    row = lax.broadcasted_iota(jnp.int32, (L, L), 0)
    col = lax.broadcasted_iota(jnp.int32, (L, L), 1)
    low_incl = (col <= row).astype(F32)
    low_strict = (col < row).astype(F32)
    c = _make_bmm(3)[0](jnp.broadcast_to(low_incl[None], (G, L, L)), lw)
    c_all = jnp.sum(lw, axis=1, keepdims=True)
    g_inv = jnp.exp(-c)
    kap_t = kap * jnp.exp(c - lw)
    b_t = b * g_inv
    k_t = k * g_inv
    r_t = r * jnp.exp(c)
    g_all = jnp.exp(c_all)
    m_b = nt(kap_t, b_t) * low_strict
    m_k = nt(kap_t, k_t) * low_strict
    n_b = nt(r_t, b_t) * low_incl
    n_k = nt(r_t, k_t) * low_incl
    rhs = -(nt(kap_t, s0) + nn(m_k, v))
    sa = nn(inverse(m_b), rhs)
    y = nt(r_t, s0) + nn(n_b, sa) + nn(n_k, v)
    s1 = s0 * g_all + tn(sa, b_t * g_all) + tn(v, k_t * g_all)
    mean = jnp.mean(y, axis=-1, keepdims=True)
    yc = y - mean
    var = jnp.mean(yc * yc, axis=-1, keepdims=True)
    out = yc * lax.rsqrt(var + GN_EPS) * rep(lnw) + rep(lnb)
    out = out + jnp.sum(r * k * rep(rk), axis=-1, keepdims=True) * v
    return out, s1


def _scan_heads_per_step(total_heads, seqs_wanted):
    n_seq = total_heads // HEADS
    return HEADS * max(d for d in range(1, seqs_wanted + 1) if n_seq % d == 0)


def _wkv_fwd(seqs, lnw, lnb, rk):
    G, S, N = seqs[0].shape
    L = SCAN_CHUNK
    nc = S // L

    def body(*refs):
        ins = [r_[...] for r_ in refs[:6]]
        prm = [r_[...] for r_ in refs[6:9]]
        out_ref, st_ref, state = refs[9], refs[10], refs[11]

        @pl.when(pl.program_id(1) == 0)
        def _():
            state[...] = jnp.zeros_like(state)

        s0 = state[...]
        st_ref[...] = s0
        out, s1 = _wkv_chunk(s0, *ins, *prm)
        out_ref[...] = out
        state[...] = s1

    gb = _scan_heads_per_step(G, SCAN_SEQS_FWD)
    blk = pl.BlockSpec((gb, L, N), lambda b, i: (b, i, 0))
    pspec = _const_spec((HEADS, 1, N))
    return pl.pallas_call(
        body, name="wkv_fwd", grid=(G // gb, nc), in_specs=[blk] * 6 + [pspec] * 3,
        out_specs=[blk, pl.BlockSpec((None, gb, N, N), lambda b, i: (i, b, 0, 0))],
        out_shape=[jax.ShapeDtypeStruct((G, S, N), F32), jax.ShapeDtypeStruct((nc, G, N, N), F32)],
        scratch_shapes=[pltpu.VMEM((gb, N, N), F32)],
        compiler_params=_params(("parallel", "arbitrary")),
    )(*seqs, lnw, lnb, rk)


def _wkv_bwd(seqs, states, dout, lnw, lnb, rk, ride=None):
    G, S, N = seqs[0].shape
    L = SCAN_CHUNK
    nc = S // L
    gb = _scan_heads_per_step(G, SCAN_SEQS_BWD)

    def body(*refs):
        ins = [r_[...] for r_ in refs[:6]]
        s0 = refs[6][...]
        ct_out = refs[7][...]
        prm = [r_[...] for r_ in refs[8:11]]
        refs = refs[11:]
        if ride is not None:
            start, finish = _scatter_steps(refs[0], refs[10], refs[12], refs[13])
            pl.when(jnp.logical_and(pl.program_id(0) == 0, pl.program_id(1) == 0))(start)
            refs = refs[1:]
        d_refs = refs[0:6]
        dprm_refs = refs[6:9]
        dstate = refs[10] if ride is not None else refs[9]
        step = pl.program_id(1)

        @pl.when(step == 0)
        def _():
            dstate[...] = jnp.zeros_like(dstate)

        _, vjp = jax.vjp(_wkv_chunk, s0, *ins, *prm)
        grads = vjp((ct_out, dstate[...]))
        dstate[...] = grads[0]
        for ref, gval in zip(d_refs, grads[1:7]):
            ref[...] = gval
        for ref, gval in zip(dprm_refs, grads[7:]):
            _acc_out(ref, gval, jnp.logical_and(step == 0, pl.program_id(0) == 0))
        if ride is not None:
            pl.when(jnp.logical_and(pl.program_id(0) == G // gb - 1, step == nc - 1))(finish)

    blk = pl.BlockSpec((gb, L, N), lambda b, s: (b, nc - 1 - s, 0))
    pspec = _const_spec((HEADS, 1, N))
    pout = pl.BlockSpec((HEADS, 1, N), lambda b, s: (0, 0, 0))
    in_specs = [blk] * 6 + [pl.BlockSpec((None, gb, N, N), lambda b, s: (nc - 1 - s, b, 0, 0)), blk] + [pspec] * 3
    out_specs = [blk] * 6 + [pout] * 3
    out_shape = [jax.ShapeDtypeStruct((G, S, N), F32)] * 6 + [jax.ShapeDtypeStruct((HEADS, 1, N), F32)] * 3
    scratch = [pltpu.VMEM((gb, N, N), F32)]
    args = (*seqs, states, dout, lnw, lnb, rk)
    if ride is not None:
        in_specs, out_specs, args = in_specs + [_ANY], out_specs + [_ANY], args + (ride,)
        out_shape = out_shape + [jax.ShapeDtypeStruct((N_DEV - 1,) + ride.shape[1:], ride.dtype)]
        scratch = scratch + [pltpu.SemaphoreType.DMA((N_DEV - 1,)), pltpu.SemaphoreType.DMA((N_DEV - 1,))]
    return pl.pallas_call(
        body, name="wkv_bwd", grid=(G // gb, nc), in_specs=in_specs, out_specs=out_specs, out_shape=out_shape,
        scratch_shapes=scratch, compiler_params=_params(("arbitrary", "arbitrary")),
    )(*args)


def _merge_math(o_sb, rw_out, g_rw, gates, w_sb, w_rw, w_o):
    o_rw = (rw_out * g_rw).astype(BF16)
    a = _mm(o_sb, w_sb)
    b = _mm(o_rw, w_rw)
    g1, g2 = gates[:, :D_MODEL], gates[:, D_MODEL:]
    merged = (g1 * a + g2 * b).astype(BF16)
    return o_rw, a, b, g1, g2, merged, _mm(merged, w_o)


def _merge_fwd(x2, o_sb, rw_out, g_rw, gates, w_sb, w_rw, w_o, g_post, seq, tm):
    T = x2.shape[0]

    def body(x_ref, osb_ref, rw_ref, g_ref, gate_ref, wsb_ref, wrw_ref, wo_ref, gp_ref, x1_ref):
        z = _merge_math(osb_ref[...], _join_heads(rw_ref), g_ref[...], gate_ref[...], wsb_ref[...], wrw_ref[...], wo_ref[...])[-1]
        x1_ref[...] = x_ref[...] + _rms_fwd(z, gp_ref[...])[0]

    return pl.pallas_call(
        body, name="merge_fwd", grid=(T // tm,),
        in_specs=[_row_spec(tm, D_MODEL), _row_spec(tm, WIDTH), _head_spec(tm, seq), _row_spec(tm, WIDTH),
                  _row_spec(tm, GATE_COLS), _const_spec((WIDTH, D_MODEL)), _const_spec((WIDTH, D_MODEL)),
                  _const_spec((D_MODEL, D_MODEL)), _const_spec((1, D_MODEL))],
        out_specs=_row_spec(tm, D_MODEL),
        out_shape=jax.ShapeDtypeStruct((T, D_MODEL), F32),
        compiler_params=_params(("parallel",)),
    )(x2, o_sb, rw_out, g_rw, gates, w_sb, w_rw, w_o, g_post)


def _merge_bwd(dx1, o_sb, rw_out, g_rw, gates, w_sb, w_rw, w_o, g_post, seq, tm):
    T = dx1.shape[0]

    def body(dx1_ref, osb_ref, rw_ref, g_ref, gate_ref, wsb_ref, wrw_ref, wo_ref, gp_ref,
             orw_o, mrg_o, dz_o, da_o, db_o, dgate_o, dosb_o, drw_o, dg_o, dgp_o, dbg_o):
        rw_out_v, g_rw_v = _join_heads(rw_ref), g_ref[...]
        w_sb_v, w_rw_v, w_o_v = wsb_ref[...], wrw_ref[...], wo_ref[...]
        o_rw, a, b, g1, g2, merged, z = _merge_math(osb_ref[...], rw_out_v, g_rw_v, gate_ref[...], w_sb_v, w_rw_v, w_o_v)
        gain = gp_ref[...]
        _, zn, rstd = _rms_fwd(z, gain)
        dz, dgain = _rms_bwd(dx1_ref[...], zn, rstd, gain)
        dzb = dz.astype(BF16)
        dm = _mm_nt(dzb, w_o_v)
        dab = (dm * g1).astype(BF16)
        dbb = (dm * g2).astype(BF16)
        dgate = jnp.concatenate([dm * a * g1 * (1.0 - g1), dm * b * g2 * (1.0 - g2)], axis=1)
        do_rw = _mm_nt(dbb, w_rw_v)
        orw_o[...] = o_rw
        mrg_o[...] = merged
        dz_o[...] = dzb
        da_o[...] = dab
        db_o[...] = dbb
        dgate_o[...] = dgate.astype(BF16)
        dosb_o[...] = _mm_nt(dab, w_sb_v).astype(BF16)
        _split_heads(do_rw * g_rw_v, drw_o)
        dg_o[...] = do_rw * rw_out_v
        first = pl.program_id(0) == 0
        _acc_out(dgp_o, dgain, first)
        _acc_out(dbg_o, jnp.sum(dgate, axis=0, keepdims=True), first)

    acc = lambda n: pl.BlockSpec((1, n), lambda i: (0, 0))
    sd = jax.ShapeDtypeStruct
    return pl.pallas_call(
        body, name="merge_bwd", grid=(T // tm,),
        in_specs=[_row_spec(tm, D_MODEL), _row_spec(tm, WIDTH), _head_spec(tm, seq), _row_spec(tm, WIDTH),
                  _row_spec(tm, GATE_COLS), _const_spec((WIDTH, D_MODEL)), _const_spec((WIDTH, D_MODEL)),
                  _const_spec((D_MODEL, D_MODEL)), _const_spec((1, D_MODEL))],
        out_specs=[_row_spec(tm, WIDTH), _row_spec(tm, D_MODEL), _row_spec(tm, D_MODEL), _row_spec(tm, D_MODEL),
                   _row_spec(tm, D_MODEL), _row_spec(tm, GATE_COLS), _row_spec(tm, WIDTH), _head_spec(tm, seq),
                   _row_spec(tm, WIDTH), acc(D_MODEL), acc(GATE_COLS)],
        out_shape=[sd((T, WIDTH), BF16), sd((T, D_MODEL), BF16), sd((T, D_MODEL), BF16), sd((T, D_MODEL), BF16),
                   sd((T, D_MODEL), BF16), sd((T, GATE_COLS), BF16), sd((T, WIDTH), BF16),
                   sd((T // seq, HEADS, seq, HEAD_DIM), F32), sd((T, WIDTH), F32), sd((1, D_MODEL), F32),
                   sd((1, GATE_COLS), F32)],
        compiler_params=_params(("arbitrary",)),
    )(dx1, o_sb, rw_out, g_rw, gates, w_sb, w_rw, w_o, g_post)


def _ffn(x1, target, g_pre, g_post, w_gate, w_up, w_down, tm):
    T = x1.shape[0]

    def body(x1_ref, tgt_ref, gpre_ref, gpost_ref, wg_ref, wu_ref, wd_ref,
             loss_o, dx1_o, h_o, dgate_o, dup_o, act_o, df_o, dgpre_o, dgpost_o):
        x1v = x1_ref[...]
        gpre, gpost = gpre_ref[...], gpost_ref[...]
        wg, wu, wd = wg_ref[...], wu_ref[...], wd_ref[...]
        hn, xn1, rstd1 = _rms_fwd(x1v, gpre)
        h = hn.astype(BF16)
        gate = _mm(h, wg)
        up = _mm(h, wu)
        sg = jax.nn.sigmoid(gate)
        act = (gate * sg * up).astype(BF16)
        f = _mm(act, wd)
        fo, fn, rstd2 = _rms_fwd(f, gpost)
        diff = x1v + fo - tgt_ref[...]
        dy = diff * (1.0 / D_MODEL)
        df, dgpost = _rms_bwd(dy, fn, rstd2, gpost)
        dfb = df.astype(BF16)
        dact = _mm_nt(dfb, wd)
        dup = (dact * gate * sg).astype(BF16)
        dgate = (dact * up * (sg * (1.0 + gate * (1.0 - sg)))).astype(BF16)
        dh = _mm_nt(dgate, wg) + _mm_nt(dup, wu)
        dxn, dgpre = _rms_bwd(dh, xn1, rstd1, gpre)
        dx1_o[...] = dy + dxn
        h_o[...] = h
        dgate_o[...] = dgate
        dup_o[...] = dup
        act_o[...] = act
        df_o[...] = dfb
        first = pl.program_id(0) == 0
        part = jnp.sum(jnp.sum(diff * diff, axis=1, keepdims=True), axis=0, keepdims=True) * (0.5 / D_MODEL)
        _acc_out(loss_o, jnp.broadcast_to(part, (8, 128)), first)
        _acc_out(dgpre_o, dgpre, first)
        _acc_out(dgpost_o, dgpost, first)

    acc = lambda r, n: pl.BlockSpec((r, n), lambda i: (0, 0))
    sd = jax.ShapeDtypeStruct
    return pl.pallas_call(
        body, name="ffn", grid=(T // tm,),
        in_specs=[_row_spec(tm, D_MODEL), _row_spec(tm, D_MODEL), _const_spec((1, D_MODEL)), _const_spec((1, D_MODEL)),
                  _const_spec((D_MODEL, D_FF)), _const_spec((D_MODEL, D_FF)), _const_spec((D_FF, D_MODEL))],
        out_specs=[acc(8, 128), _row_spec(tm, D_MODEL), _row_spec(tm, D_MODEL), _row_spec(tm, D_FF), _row_spec(tm, D_FF),
                   _row_spec(tm, D_FF), _row_spec(tm, D_MODEL), acc(1, D_MODEL), acc(1, D_MODEL)],
        out_shape=[sd((8, 128), F32), sd((T, D_MODEL), F32), sd((T, D_MODEL), BF16), sd((T, D_FF), BF16),
                   sd((T, D_FF), BF16), sd((T, D_FF), BF16), sd((T, D_MODEL), BF16), sd((1, D_MODEL), F32),
                   sd((1, D_MODEL), F32)],
        compiler_params=_params(("arbitrary",)),
    )(x1, target, g_pre, g_post, w_gate, w_up, w_down)


def _local_step(x, target, sm, wt, late=None):
    bl, seq, _ = x.shape
    T = bl * seq
    tm = min(ROW_TILE, T)
    x2 = x.reshape(T, D_MODEL)
    tgt2 = target.reshape(T, D_MODEL)
    h, qkv, prw, gates = _in_proj_fwd(x2, sm["norm_mix_pre"], wt["w_in"], sm["b_gate"], tm)
    if late is None:
        o_sb, lsum, sb_stop = _sb_fwd(qkv, bl, seq)
    else:
        o_sb, lsum, sb_stop, gathered = _sb_fwd(qkv, bl, seq, late)
        wt = {**wt, **_unpack_gathered(gathered, slice(_EARLY, None))}
    prep_params = [sm["mu_rw"], sm["w0"], wt["w_up"].astype(F32), sm["a0"], wt["a_up"].astype(F32),
                   wt["g_up"].astype(F32), sm["k_k"], sm["k_a"]]
    prep = _rw_prep_fwd(prw, prep_params, seq, tm)
    by_head = lambda t: t.reshape(bl, HEADS, seq, HEAD_DIM)
    seqs = [t.reshape(bl * HEADS, seq, HEAD_DIM) for t in prep[:6]]
    g_rw = prep[6]
    lnw, lnb, rk = (sm[n].reshape(HEADS, 1, HEAD_DIM) for n in ("lnx_w", "lnx_b", "r_k"))
    rw_out_h, states = _wkv_fwd(seqs, lnw, lnb, rk)
    rw_out = by_head(rw_out_h)
    x1 = _merge_fwd(x2, o_sb, rw_out, g_rw, gates, wt["w_sb_out"], wt["w_rw_out"], wt["w_o"], sm["norm_mix_post"],
                    seq, tm)
    (loss_part, dx1, h2, dffg, dffu, act, dff, d_nfpre, d_nfpost) = _ffn(
        x1, tgt2, sm["norm_ffn_pre"], sm["norm_ffn_post"], wt["w_ffn_gate"], wt["w_ffn_up"], wt["w_ffn_down"],
        min(ROW_TILE_FFN, T))
    (o_rw, merged, dz, da, db, dgate, do_sb, d_rw_out, d_g_rw, d_npost, d_bgate) = _merge_bwd(
        dx1, o_sb, rw_out, g_rw, gates, wt["w_sb_out"], wt["w_rw_out"], wt["w_o"], sm["norm_mix_post"], seq, tm)
    gdt = F32 if late is None else BF16
    gw = {
        "w_sb_out": _grad_w(o_sb, da, "gw_sb_out", gdt), "w_rw_out": _grad_w(o_rw, db, "gw_rw_out", gdt),
        "w_o": _grad_w(merged, dz, "gw_o", gdt),
        "w_ffn_gate": _grad_w(h2, dffg, "gw_ffn_gate", gdt), "w_ffn_up": _grad_w(h2, dffu, "gw_ffn_up", gdt),
        "w_ffn_down": _grad_w(act, dff, "gw_ffn_down", gdt),
    }
    dqkv = jnp.concatenate(_sb_bwd(qkv, do_sb, lsum, sb_stop, bl, seq), axis=1)
    ride = None if late is None else _pack_full_grads(gw, slice(_EARLY, None))
    wkv_g = _wkv_bwd(seqs, states, d_rw_out.reshape(bl * HEADS, seq, HEAD_DIM), lnw, lnb, rk, ride)
    late_grads = None if late is None else (ride, wkv_g[9])
    cts = [by_head(t) for t in wkv_g[:6]] + [d_g_rw]
    prep_g = _rw_prep_bwd(prw, prep_params, cts, seq, tm)
    dprw = prep_g[0]
    d_mu, d_w0, d_wup, d_a0, d_aup, d_gup, d_kk, d_ka = prep_g[1:]
    grad_x, d_npre = _in_proj_bwd(x2, sm["norm_mix_pre"], dx1, dqkv, dprw, dgate, wt["w_in"], tm)
    gw = {
        **gw,
        "w_in": jnp.concatenate([_grad_w(h, dqkv, "gw_in_qkv"), _grad_w(h, dprw, "gw_in_rw"), _grad_w(h, dgate, "gw_in_gate")], axis=1),
        "w_up": d_wup, "a_up": d_aup, "g_up": d_gup,
    }
    gs = {
        "norm_mix_pre": d_npre, "b_gate": d_bgate, "mu_rw": d_mu, "w0": d_w0, "a0": d_a0, "k_k": d_kk, "k_a": d_ka,
        "r_k": wkv_g[8].reshape(1, WIDTH), "lnx_w": wkv_g[6].reshape(1, WIDTH), "lnx_b": wkv_g[7].reshape(1, WIDTH),
        "norm_mix_post": d_npost, "norm_ffn_pre": d_nfpre, "norm_ffn_post": d_nfpost,
    }
    return loss_part, grad_x.reshape(x.shape), gw, gs, late_grads


_SHARDED = [("w_in", 1, (D_MODEL, (SB_COLS + RW_COLS + GATE_COLS) // N_DEV)), ("w_up", 1, (W_LORA, WIDTH // N_DEV)),
            ("a_up", 1, (A_LORA, WIDTH // N_DEV)), ("g_up", 1, (G_LORA, WIDTH // N_DEV)),
            ("w_sb_out", 1, (WIDTH, D_MODEL // N_DEV)), ("w_rw_out", 1, (WIDTH, D_MODEL // N_DEV)),
            ("w_o", 0, (D_MODEL // N_DEV, D_MODEL)), ("w_ffn_gate", 1, (D_MODEL, D_FF // N_DEV)),
            ("w_ffn_up", 1, (D_MODEL, D_FF // N_DEV)), ("w_ffn_down", 0, (D_FF // N_DEV, D_MODEL))]
_LANES = 128
_PACK_ROWS = [s[0] * s[1] // _LANES for _, _, s in _SHARDED]
_PACK_TOTAL = sum(_PACK_ROWS)
_SMALL = [("norm_mix_pre", D_MODEL), ("b_gate", GATE_COLS), ("mu_rw", RW_COLS), ("w0", WIDTH), ("a0", WIDTH),
          ("k_k", WIDTH), ("k_a", WIDTH), ("r_k", WIDTH), ("lnx_w", WIDTH), ("lnx_b", WIDTH),
          ("norm_mix_post", D_MODEL), ("norm_ffn_pre", D_MODEL), ("norm_ffn_post", D_MODEL)]
_SMALL_ROWS = 96


def _pack_shards(shards, dtype):
    return jnp.concatenate([shards[n].astype(dtype).reshape(-1, _LANES) for n, _, _ in _SHARDED], axis=0)


def _unpack_shards(packed, which):
    out, r0 = {}, 0
    for (n, _, shp), rows in zip(_SHARDED[which], _PACK_ROWS[which]):
        out[n] = packed[r0:r0 + rows].reshape(shp)
        r0 += rows
    return out


_EARLY = 4
_EARLY_ROWS = sum(_PACK_ROWS[:_EARLY])


def _unpack_gathered(g, which):
    out, r0 = {}, 0
    for (n, axis, shp), rows in zip(_SHARDED[which], _PACK_ROWS[which]):
        blk = g[:, r0:r0 + rows].reshape((N_DEV,) + shp)
        out[n] = blk.reshape(N_DEV * shp[0], shp[1]) if axis == 0 else blk.transpose(1, 0, 2).reshape(shp[0], N_DEV * shp[1])
        r0 += rows
    return out


def _pack_full_grads(gw, which, by_core=False):
    parts = []
    for n, axis, shp in _SHARDED[which]:
        g = gw[n]
        if by_core:
            blk = (g.reshape((4, 2) + shp).transpose(1, 0, 2, 3) if axis == 0 else
                   g.reshape(shp[0], 4, 2, shp[1]).transpose(2, 1, 0, 3))
            parts.append(blk.reshape(2, 4, -1, _LANES))
        else:
            blk = g.reshape((N_DEV,) + shp) if axis == 0 else g.reshape(shp[0], N_DEV, shp[1]).transpose(1, 0, 2)
            parts.append(blk.reshape(N_DEV, -1, _LANES))
    return jnp.concatenate(parts, axis=-2)


def _pack_small(vals, extra=None):
    used = sum(sz for _, sz in _SMALL)
    tail = jnp.zeros((1, _SMALL_ROWS * _LANES - used), F32).at[0, 0].set(extra)
    return jnp.concatenate([vals[n].reshape(1, -1) for n, _ in _SMALL] + [tail], axis=1)


_ANY = pl.BlockSpec(memory_space=pl.ANY)


def _all_gather(block):
    rows, lanes = block.shape

    def body(x_ref, out_ref, send_sems, recv_sems, local_sem):
        start, forward, finish = _gather_steps(x_ref, out_ref, send_sems, recv_sems, local_sem)
        start()
        forward()
        finish()

    return pl.pallas_call(
        body, name="all_gather_weights", in_specs=[_ANY], out_specs=_ANY,
        out_shape=jax.ShapeDtypeStruct((N_DEV, rows, lanes), block.dtype), scratch_shapes=_GATHER_SEMS,
    )(block)


def _scatter_steps(pack_ref, got_ref, send_sems, recv_sems):
    x, y, c = lax.axis_index("x"), lax.axis_index("y"), lax.axis_index("c")

    def copies():
        out = []
        for k in range(1, N_DEV):
            px, py, pc = x ^ (k >> 2), y ^ ((k >> 1) & 1), c ^ (k & 1)
            out.append(pltpu.make_async_remote_copy(
                src_ref=pack_ref.at[4 * px + 2 * py + pc], dst_ref=got_ref.at[k - 1], send_sem=send_sems.at[k - 1],
                recv_sem=recv_sems.at[k - 1], device_id=(px, py, pc), device_id_type=MESH))
        return out

    def start():
        for cp in copies():
            cp.start()

    def finish():
        for cp in copies():
            cp.wait_recv()
        for cp in copies():
            cp.wait_send()

    return start, finish


def _sum_direct(pack, got, me):
    _, rows, lanes = pack.shape
    tr = _pick_rows(rows)

    def body(me_ref, own_ref, got_ref, o_ref):
        total = own_ref[...].astype(F32)
        for k in range(N_DEV - 1):
            total = total + got_ref[k].astype(F32)
        o_ref[...] = total

    return pl.pallas_call(
        body, name="sum_direct",
        grid_spec=pltpu.PrefetchScalarGridSpec(
            num_scalar_prefetch=1, grid=(rows // tr,),
            in_specs=[pl.BlockSpec((None, tr, lanes), lambda i, me_ref: (me_ref[0], i, 0)),
                      pl.BlockSpec((N_DEV - 1, tr, lanes), lambda i, me_ref: (0, i, 0))],
            out_specs=pl.BlockSpec((tr, lanes), lambda i, me_ref: (i, 0))),
        out_shape=jax.ShapeDtypeStruct((rows, lanes), F32),
        compiler_params=_params(("parallel",)),
    )(me, pack, got)


def _pick_rows(rows, cap=2048):
    return max(t for t in range(16, cap + 1, 16) if rows % t == 0)


_GATHER_SEMS = [pltpu.SemaphoreType.DMA((7,)), pltpu.SemaphoreType.DMA((7,)), pltpu.SemaphoreType.DMA]


def _gather_steps(x_ref, out_ref, send_sems, recv_sems, local_sem):
    x, y, c = lax.axis_index("x"), lax.axis_index("y"), lax.axis_index("c")
    me, sibling = (x, y, c), (x, y, 1 - c)
    chips = [(1 - x, y), (x, 1 - y), (1 - x, 1 - y)]

    def slot(px, py, pc):
        return out_ref.at[4 * px + 2 * py + pc]

    def copy(k, blk, to, src=None):
        return pltpu.make_async_remote_copy(
            src_ref=slot(*blk) if src is None else src, dst_ref=slot(*blk),
            send_sem=send_sems.at[k], recv_sem=recv_sems.at[k], device_id=to, device_id_type=MESH)

    def first():
        return [copy(0, me, sibling, src=x_ref)] + [copy(1 + j, me, (*chip, c), src=x_ref) for j, chip in enumerate(chips)]

    def passed():
        return [copy(4 + j, (*chip, c), sibling) for j, chip in enumerate(chips)]

    def start():
        pltpu.make_async_copy(x_ref, slot(*me), local_sem).start()
        for cp in first():
            cp.start()

    def forward():
        for j, (chip, cp) in enumerate(zip(chips, passed())):
            copy(1 + j, (*chip, c), me).wait_recv()
            cp.start()

    def finish():
        copy(0, sibling, me).wait_recv()
        for j, chip in enumerate(chips):
            copy(4 + j, (*chip, 1 - c), me).wait_recv()
        for cp in first() + passed():
            cp.wait_send()
        pltpu.make_async_copy(x_ref, slot(*me), local_sem).wait()

    return start, forward, finish


def _exchange_core(pack, small):
    _, _, rows, lanes = pack.shape

    def body(pack_ref, small_ref, got_ref, parts_ref, send_sems, recv_sems, s_send, s_recv, local_sem):
        x, y, c = lax.axis_index("x"), lax.axis_index("y"), lax.axis_index("c")
        sibling = (x, y, 1 - c)
        me = 4 * x + 2 * y + c
        mine = pltpu.make_async_copy(small_ref, parts_ref.at[me], local_sem)
        mine.start()
        big = [pltpu.make_async_remote_copy(
            src_ref=pack_ref.at[1 - c, j], dst_ref=got_ref.at[j], send_sem=send_sems.at[j], recv_sem=recv_sems.at[j],
            device_id=sibling, device_id_type=MESH) for j in range(4)]
        for cp in big:
            cp.start()
        others = [(k, (x ^ (k >> 2), y ^ ((k >> 1) & 1), c ^ (k & 1))) for k in range(1, N_DEV)]
        tiny = [pltpu.make_async_remote_copy(
            src_ref=small_ref, dst_ref=parts_ref.at[me], send_sem=s_send.at[k], recv_sem=s_recv.at[k],
            device_id=to, device_id_type=MESH) for k, to in others]
        for cp in tiny:
            cp.start()
        for cp in big:
            cp.wait_recv()
        for (k, (px, py, pc)), cp in zip(others, tiny):
            pltpu.make_async_remote_copy(
                src_ref=small_ref, dst_ref=parts_ref.at[4 * px + 2 * py + pc], send_sem=s_send.at[k],
                recv_sem=s_recv.at[k], device_id=(px, py, pc), device_id_type=MESH).wait_recv()
        for cp in big + tiny:
            cp.wait_send()
        mine.wait()

    return pl.pallas_call(
        body, name="exchange_core", in_specs=[_ANY, _ANY], out_specs=[_ANY, _ANY],
        out_shape=[jax.ShapeDtypeStruct((4, rows, lanes), F32), jax.ShapeDtypeStruct((N_DEV,) + small.shape, F32)],
        scratch_shapes=[pltpu.SemaphoreType.DMA((4,)), pltpu.SemaphoreType.DMA((4,)), pltpu.SemaphoreType.DMA((N_DEV,)),
                        pltpu.SemaphoreType.DMA((N_DEV,)), pltpu.SemaphoreType.DMA],
    )(pack, small)


def _add_core_parts(pack, got, core):
    _, _, rows, lanes = pack.shape
    tr = _pick_rows(rows)

    def body(core_ref, a_ref, b_ref, o_ref):
        o_ref[...] = (a_ref[...] + b_ref[...]).astype(BF16)

    return pl.pallas_call(
        body, name="add_core_parts",
        grid_spec=pltpu.PrefetchScalarGridSpec(
            num_scalar_prefetch=1, grid=(4, rows // tr),
            in_specs=[pl.BlockSpec((None, None, tr, lanes), lambda j, i, core_ref: (core_ref[0], j, i, 0)),
                      pl.BlockSpec((None, tr, lanes), lambda j, i, core_ref: (j, i, 0))],
            out_specs=pl.BlockSpec((None, tr, lanes), lambda j, i, core_ref: (j, i, 0))),
        out_shape=jax.ShapeDtypeStruct((4, rows, lanes), BF16),
        compiler_params=_params(("parallel", "parallel")),
    )(core, pack, got)


def _exchange_chips(chip_sums):
    _, rows, lanes = chip_sums.shape

    def body(src_ref, got_ref, send_sems, recv_sems):
        x, y, c = lax.axis_index("x"), lax.axis_index("y"), lax.axis_index("c")
        flips = [(1, 0), (0, 1), (1, 1)]
        copies = []
        for k, (fx, fy) in enumerate(flips):
            px, py = x ^ fx, y ^ fy
            copies.append(pltpu.make_async_remote_copy(
                src_ref=src_ref.at[2 * px + py], dst_ref=got_ref.at[k], send_sem=send_sems.at[k],
                recv_sem=recv_sems.at[k], device_id=(px, py, c), device_id_type=MESH))
        for cp in copies:
            cp.start()
        for cp in copies:
            cp.wait_recv()
        for cp in copies:
            cp.wait_send()

    return pl.pallas_call(
        body, name="exchange_chips", in_specs=[_ANY], out_specs=_ANY,
        out_shape=jax.ShapeDtypeStruct((3, rows, lanes), chip_sums.dtype),
        scratch_shapes=[pltpu.SemaphoreType.DMA((3,)), pltpu.SemaphoreType.DMA((3,))],
    )(chip_sums)


def _sum_chip_parts(chip_sums, got, chip):
    _, rows, lanes = chip_sums.shape
    tr = _pick_rows(rows)

    def body(chip_ref, own_ref, got_ref, o_ref):
        f32 = lambda t: t.astype(F32)
        o_ref[...] = ((f32(own_ref[...]) + f32(got_ref[0])) + f32(got_ref[1])) + f32(got_ref[2])

    return pl.pallas_call(
        body, name="sum_chip_parts",
        grid_spec=pltpu.PrefetchScalarGridSpec(
            num_scalar_prefetch=1, grid=(rows // tr,),
            in_specs=[pl.BlockSpec((None, tr, lanes), lambda i, chip_ref: (chip_ref[0], i, 0)),
                      pl.BlockSpec((3, tr, lanes), lambda i, chip_ref: (0, i, 0))],
            out_specs=pl.BlockSpec((tr, lanes), lambda i, chip_ref: (i, 0))),
        out_shape=jax.ShapeDtypeStruct((rows, lanes), F32),
        compiler_params=_params(("parallel",)),
    )(chip, chip_sums, got)


def _adamw_math(w, g, m, v):
    m = ADAM_B1 * m + (1.0 - ADAM_B1) * g
    v = ADAM_B2 * v + (1.0 - ADAM_B2) * (g * g)
    m_hat = m / (1.0 - ADAM_B1 ** ADAM_STEP)
    v_hat = v / (1.0 - ADAM_B2 ** ADAM_STEP)
    return -ADAM_LR * (m_hat / (jnp.sqrt(v_hat) + ADAM_EPS) + ADAM_WD * w), m, v


def _adamw(w, g, m, v, name):
    rows, cols = w.shape
    tr = 256 if rows % 256 == 0 and rows * cols > 2 ** 19 else rows

    def body(w_ref, g_ref, m_ref, v_ref, d_o, m_o, v_o):
        d_o[...], m_o[...], v_o[...] = _adamw_math(w_ref[...], g_ref[...], m_ref[...], v_ref[...])

    spec = pl.BlockSpec((tr, cols), lambda i: (i, 0))
    return pl.pallas_call(
        body, name=name, grid=(rows // tr,), in_specs=[spec] * 4, out_specs=[spec] * 3,
        out_shape=[jax.ShapeDtypeStruct((rows, cols), F32)] * 3, compiler_params=_params(("parallel",)),
    )(w, g, m, v)


def _adamw_small(parts, ws, ms, vs):
    k = len(_SMALL)

    def body(p_ref, *refs):
        w_refs, m_refs, v_refs = refs[:k], refs[k:2 * k], refs[2 * k:3 * k]
        outs = refs[3 * k:]
        g = p_ref[0]
        for d in range(1, N_DEV):
            g = g + p_ref[d]
        o = 0
        for i, (_, n) in enumerate(_SMALL):
            gp = g[:, o:o + n]
            outs[1 + i][...] = gp
            outs[1 + k + i][...], outs[1 + 2 * k + i][...], outs[1 + 3 * k + i][...] = _adamw_math(
                w_refs[i][...], gp, m_refs[i][...], v_refs[i][...])
            o += n
        outs[0][...] = g[:, o:o + _LANES]

    shapes = [jax.ShapeDtypeStruct((1, n), F32) for _, n in _SMALL]
    out = pl.pallas_call(
        body, name="adamw_small", out_shape=[jax.ShapeDtypeStruct((1, _LANES), F32)] + shapes * 4,
        compiler_params=_params(),
    )(parts, *ws, *ms, *vs)
    return out[0][0, 0], out[1:1 + k], out[1 + k:1 + 2 * k], out[1 + 2 * k:1 + 3 * k], out[1 + 3 * k:]


_WEIGHT_NAMES = ['norm_mix_pre', 'w_in', 'b_gate', 'mu_rw', 'w0', 'w_up', 'a0', 'a_up', 'g_up', 'k_k', 'k_a', 'r_k',
                 'lnx_w', 'lnx_b', 'w_sb_out', 'w_rw_out', 'w_o', 'norm_mix_post', 'norm_ffn_pre', 'w_ffn_gate',
                 'w_ffn_up', 'w_ffn_down', 'norm_ffn_post']


def _step(x, target, w, m, v):
    sharded = [n for n, _, _ in _SHARDED]
    sm = {n: w[n].reshape(1, -1) for n, _ in _SMALL}
    own = {n: w[n][0] for n in sharded}
    packed = _pack_shards(own, BF16)
    wt = _unpack_gathered(_all_gather(packed[:_EARLY_ROWS]), slice(0, _EARLY))
    loss_part, grad_x, gw, gs, (late_pack, late_got) = _local_step(x, target, sm, wt, packed[_EARLY_ROWS:])

    cx, cy, cc = lax.axis_index("x"), lax.axis_index("y"), lax.axis_index("c")
    core = jnp.reshape(cc, (1,)).astype(jnp.int32)
    chip = jnp.reshape(2 * cx + cy, (1,)).astype(jnp.int32)
    me = jnp.reshape(4 * cx + 2 * cy + cc, (1,)).astype(jnp.int32)
    pack = _pack_full_grads(gw, slice(0, _EARLY), by_core=True)
    got_core, small_parts = _exchange_core(pack, _pack_small(gs, loss_part[0, 0]))
    chip_sums = _add_core_parts(pack, got_core, core)
    early = _sum_chip_parts(chip_sums, _exchange_chips(chip_sums), chip)
    g_sh = {**_unpack_shards(early, slice(0, _EARLY)),
            **_unpack_shards(_sum_direct(late_pack, late_got, me), slice(_EARLY, None))}

    row = lambda t: [t[n].reshape(1, -1) for n, _ in _SMALL]
    loss, *by_kind = _adamw_small(small_parts, row(w), row(m), row(v))
    g_s, d_s, m_s, v_s = ({n: t[i] for i, (n, _) in enumerate(_SMALL)} for t in by_kind)

    grads, deltas, new_m, new_v = {}, {}, {}, {}
    for n in _WEIGHT_NAMES:
        if n in g_sh:
            d_, m_, v_ = _adamw(own[n], g_sh[n], m[n][0], v[n][0], "adamw_" + n)
            grads[n], deltas[n], new_m[n], new_v[n] = (t.reshape(w[n].shape) for t in (g_sh[n], d_, m_, v_))
        else:
            grads[n], deltas[n], new_m[n], new_v[n] = (t[n].reshape(w[n].shape) for t in (g_s, d_s, m_s, v_s))
    return (loss, grad_x, *[grads[n] for n in _WEIGHT_NAMES], *[deltas[n] for n in _WEIGHT_NAMES],
            *[new_m[n] for n in _WEIGHT_NAMES], *[new_v[n] for n in _WEIGHT_NAMES])


def kernel(x, norm_mix_pre, w_in, b_gate, mu_rw, w0, w_up, a0, a_up, g_up, k_k, k_a, r_k, lnx_w, lnx_b, w_sb_out, w_rw_out, w_o, norm_mix_post, norm_ffn_pre, w_ffn_gate, w_ffn_up, w_ffn_down, norm_ffn_post, loss_target, m_norm_mix_pre, m_w_in, m_b_gate, m_mu_rw, m_w0, m_w_up, m_a0, m_a_up, m_g_up, m_k_k, m_k_a, m_r_k, m_lnx_w, m_lnx_b, m_w_sb_out, m_w_rw_out, m_w_o, m_norm_mix_post, m_norm_ffn_pre, m_w_ffn_gate, m_w_ffn_up, m_w_ffn_down, m_norm_ffn_post, v_norm_mix_pre, v_w_in, v_b_gate, v_mu_rw, v_w0, v_w_up, v_a0, v_a_up, v_g_up, v_k_k, v_k_a, v_r_k, v_lnx_w, v_lnx_b, v_w_sb_out, v_w_rw_out, v_w_o, v_norm_mix_post, v_norm_ffn_pre, v_w_ffn_gate, v_w_ffn_up, v_w_ffn_down, v_norm_ffn_post):
    args = locals()
    w = {n: args[n] for n in _WEIGHT_NAMES}
    m = {n: args["m_" + n] for n in _WEIGHT_NAMES}
    v = {n: args["v_" + n] for n in _WEIGHT_NAMES}
    return _step(x, loss_target, w, m, v)
```

```python
import functools

import jax
import jax.numpy as jnp
from jax import lax
from jax.experimental import pallas as pl
from jax.experimental.pallas import tpu as pltpu

F32 = jnp.float32
BF16 = jnp.bfloat16

D_MODEL = 1024
HEADS = 8
HEAD_DIM = 64
WIDTH = HEADS * HEAD_DIM
W_LORA, A_LORA, G_LORA = 64, 64, 128
SB_COLS = 3 * WIDTH
RW_COLS = 3 * WIDTH + W_LORA + A_LORA + G_LORA
GATE_COLS = 2 * D_MODEL
D_FF = 2816
RMS_EPS = 1e-6
GN_EPS = HEAD_DIM * 1e-5
N_DEV = 8

ADAM_LR, ADAM_B1, ADAM_B2, ADAM_EPS, ADAM_WD, ADAM_STEP = 0.001, 0.9, 0.999, 1e-08, 0.01, 10

ROW_TILE = 512
ROW_TILE_FFN = 256
SCAN_CHUNK = 64
ATT_ALIGN = 128
ATT_WINDOW = 384
ATT_Q = 128
ATT_PAIRS = 2
SB_DEAD = -104.0
SCAN_SEQS_FWD = 4
SCAN_SEQS_BWD = 2
SCAN_PASSES = 1
VMEM_LIMIT = 56 * 2 ** 20

MESH = pl.DeviceIdType.MESH


def _params(sem=None, vmem=VMEM_LIMIT):
    kw = dict(vmem_limit_bytes=vmem)
    if sem is not None:
        kw["dimension_semantics"] = sem
    return pltpu.CompilerParams(**kw)


def _const_spec(shape):
    nd = len(shape)
    return pl.BlockSpec(shape, lambda *_: (0,) * nd, pipeline_mode=pl.Buffered(1))


def _row_spec(tm, n):
    return pl.BlockSpec((tm, n), lambda i: (i, 0))


def _mm(a, b):
    return lax.dot_general(a, b, (((1,), (0,)), ((), ())), preferred_element_type=F32)


def _mm_nt(a, b):
    return lax.dot_general(a, b, (((1,), (1,)), ((), ())), preferred_element_type=F32)


def _mm_tn(a, b):
    return lax.dot_general(a, b, (((0,), (0,)), ((), ())), preferred_element_type=F32)


def _softplus(z):
    return jnp.maximum(z, 0.0) + jnp.log1p(jnp.exp(-jnp.abs(z)))


def _rms_fwd(x, gain):
    rstd = lax.rsqrt(jnp.mean(x * x, axis=-1, keepdims=True) + RMS_EPS)
    xn = x * rstd
    return xn * gain, xn, rstd


def _rms_bwd(dy, xn, rstd, gain):
    u = dy * gain
    dx = rstd * (u - xn * jnp.mean(u * xn, axis=-1, keepdims=True))
    return dx, jnp.sum(dy * xn, axis=0, keepdims=True)


def _acc_out(ref, val, first):
    @pl.when(first)
    def _():
        ref[...] = val

    @pl.when(jnp.logical_not(first))
    def _():
        ref[...] += val


_IN_COLS = SB_COLS + RW_COLS + GATE_COLS
_QKV_OF, _RW_OF, _GATE_OF = slice(0, SB_COLS), slice(SB_COLS, SB_COLS + RW_COLS), slice(SB_COLS + RW_COLS, _IN_COLS)


def _in_proj_fwd(x2, g_pre, w_in, b_gate, tm):
    T = x2.shape[0]

    def body(x_ref, g_ref, w_ref, b_ref, h_ref, qkv_ref, prw_ref, gate_ref):
        h = _rms_fwd(x_ref[...], g_ref[...])[0].astype(BF16)
        h_ref[...] = h
        qkv_ref[...] = _mm(h, w_ref[:, _QKV_OF]).astype(BF16)
        prw_ref[...] = _mm(h, w_ref[:, _RW_OF])
        gate_ref[...] = jax.nn.sigmoid(_mm(h, w_ref[:, _GATE_OF]) + b_ref[...])

    return pl.pallas_call(
        body, name="in_proj_fwd", grid=(T // tm,),
        in_specs=[_row_spec(tm, D_MODEL), _const_spec((1, D_MODEL)), _const_spec((D_MODEL, _IN_COLS)),
                  _const_spec((1, GATE_COLS))],
        out_specs=[_row_spec(tm, D_MODEL), _row_spec(tm, SB_COLS), _row_spec(tm, RW_COLS), _row_spec(tm, GATE_COLS)],
        out_shape=[jax.ShapeDtypeStruct((T, D_MODEL), BF16), jax.ShapeDtypeStruct((T, SB_COLS), BF16),
                   jax.ShapeDtypeStruct((T, RW_COLS), F32), jax.ShapeDtypeStruct((T, GATE_COLS), F32)],
        compiler_params=_params(("parallel",)),
    )(x2, g_pre, w_in, b_gate)


def _in_proj_bwd(x2, g_pre, dx1, dqkv, dprw, dgate, w_in, tm):
    T = x2.shape[0]

    def body(x_ref, g_ref, dx1_ref, dq_ref, dr_ref, dg_ref, w_ref, gx_ref, dgain_ref):
        dh = (_mm_nt(dq_ref[...], w_ref[:, _QKV_OF]) + _mm_nt(dr_ref[...], w_ref[:, _RW_OF])
              + _mm_nt(dg_ref[...], w_ref[:, _GATE_OF]))
        gain = g_ref[...]
        _, xn, rstd = _rms_fwd(x_ref[...], gain)
        dx, dgain = _rms_bwd(dh, xn, rstd, gain)
        gx_ref[...] = dx1_ref[...] + dx
        _acc_out(dgain_ref, dgain, pl.program_id(0) == 0)

    return pl.pallas_call(
        body, name="in_proj_bwd", grid=(T // tm,),
        in_specs=[_row_spec(tm, D_MODEL), _const_spec((1, D_MODEL)), _row_spec(tm, D_MODEL), _row_spec(tm, SB_COLS),
                  _row_spec(tm, RW_COLS), _row_spec(tm, GATE_COLS), _const_spec((D_MODEL, _IN_COLS))],
        out_specs=[_row_spec(tm, D_MODEL), pl.BlockSpec((1, D_MODEL), lambda i: (0, 0))],
        out_shape=[jax.ShapeDtypeStruct((T, D_MODEL), F32), jax.ShapeDtypeStruct((1, D_MODEL), F32)],
        compiler_params=_params(("arbitrary",)),
    )(x2, g_pre, dx1, dqkv, dprw, dgate, w_in)


def _pick_tile(n, cap):
    best = None
    for t in range(128, min(n, cap) + 1, 128):
        if n % t == 0:
            best = t
    return n if best is None else best


def _grad_w(a, b, name, dtype=F32):
    T, K = a.shape
    N = b.shape[1]
    tk, tn, tt = _pick_tile(K, 1408), _pick_tile(N, 2048), min(T, 2048)
    steps = T // tt

    def body(a_ref, b_ref, o_ref, *acc):
        t = pl.program_id(2)
        part = _mm_tn(a_ref[...], b_ref[...])
        if not acc:
            _acc_out(o_ref, part, t == 0)
        else:
            _acc_out(acc[0], part, t == 0)

            @pl.when(t == steps - 1)
            def _():
                o_ref[...] = acc[0][...].astype(dtype)

    return pl.pallas_call(
        body, name=name, grid=(K // tk, N // tn, steps),
        in_specs=[pl.BlockSpec((tt, tk), lambda i, j, t: (t, i)), pl.BlockSpec((tt, tn), lambda i, j, t: (t, j))],
        out_specs=pl.BlockSpec((tk, tn), lambda i, j, t: (i, j)),
        out_shape=jax.ShapeDtypeStruct((K, N), dtype),
        scratch_shapes=[] if dtype == F32 else [pltpu.VMEM((tk, tn), F32)],
        compiler_params=_params(("parallel", "parallel", "arbitrary")),
    )(a, b)


def _tri(n, kind):
    r = lax.broadcasted_iota(jnp.int32, (n, n), 0)
    c = lax.broadcasted_iota(jnp.int32, (n, n), 1)
    return {"gt": r > c, "le": r <= c, "lt": r < c, "ge": r >= c}[kind]


def _running_sums(x, carry, tri, kb, reverse=False):
    blocks = range(x.shape[1] // kb)
    parts = {}
    for b in (reversed(blocks) if reverse else blocks):
        piece = x[:, b * kb:(b + 1) * kb]
        parts[b] = carry + _mm(piece.astype(BF16), tri)
        carry = carry + jnp.sum(piece, axis=1, keepdims=True)
    return jnp.concatenate([parts[b] for b in blocks], axis=1), carry


def _sb_valid(row0, col0, first, last, qb, kb):
    ahead = lax.broadcasted_iota(jnp.int32, (qb, kb), 1) - lax.broadcasted_iota(jnp.int32, (qb, kb), 0)
    col = lax.broadcasted_iota(jnp.int32, (1, kb), 1)
    return jnp.logical_and(ahead < row0 - col0, jnp.logical_and(col >= first - col0, col < last - col0))


def _sb_softplus(z):
    return jnp.maximum(z, 0.0) + jnp.log(1.0 + jnp.exp(-jnp.abs(z)))


_PAIR = 2 * HEAD_DIM
_PAIRS = WIDTH // _PAIR


def _first_head_lanes():
    return lax.broadcasted_iota(jnp.int32, (1, _PAIR), 1) < HEAD_DIM


def _per_head(t, first_head):
    zero = jnp.zeros_like(t)
    return jnp.where(first_head, t, zero), jnp.where(first_head, zero, t)


def _sb_fwd(qkv, bl, seq, ride=None):
    qb, win, kb = min(ATT_Q, seq), min(ATT_WINDOW, seq), ATT_ALIGN
    nq = seq // qb
    nh, width, groups = 2 * ATT_PAIRS, ATT_PAIRS * _PAIR, _PAIRS // ATT_PAIRS
    pair_of = lambda h: slice((h // 2) * _PAIR, (h // 2 + 1) * _PAIR)

    steps = bl * groups
    pass_on_at = (5 * steps) // 8

    def body(q_ref, k_ref, v_ref, *rest):
        g = pl.program_id(0) * groups + pl.program_id(1)
        if ride is None:
            o_ref, l_ref, stop_ref = rest
        else:
            ride_ref, o_ref, l_ref, stop_ref, gathered_ref, *sems = rest
            start, forward, finish = _gather_steps(ride_ref, gathered_ref, *sems)
            pl.when(g == 0)(start)
            pl.when(g == pass_on_at)(forward)
        first_head = _first_head_lanes()
        u_after = _tri(kb, "gt").astype(BF16)

        def qblock(i, _):
            rows = pl.ds(pl.multiple_of(i * qb, qb), qb)
            qs = q_ref[rows, :] * (HEAD_DIM ** -0.5)
            qh = [_per_head(qs[:, pair_of(h)], first_head)[h % 2] for h in range(nh)]

            def live(carry):
                return jnp.logical_and(carry[0] > 0, carry[3] > 0)

            def window(carry):
                hi, accs, cs, _ = carry
                lo = pl.multiple_of(jnp.maximum(hi - win, 0), kb)
                cols = pl.ds(lo, win)
                kv, vv = k_ref[cols, :], v_ref[cols, :]
                valid = _sb_valid(i * qb, lo, lo, hi, qb, win)
                accs, cs = list(accs), list(cs)
                for h in range(nh):
                    z = _mm_nt(qh[h], kv[:, pair_of(h)])
                    sp = _sb_softplus(z)
                    spm = jnp.where(valid, sp, 0.0)
                    after, cs[h] = _running_sums(spm, cs[h], u_after, kb, reverse=True)
                    w = jnp.where(valid, jnp.exp(z - sp - after), 0.0)
                    accs[h] = accs[h] + _mm(w.astype(BF16), vv[:, pair_of(h)])
                alive = functools.reduce(jnp.minimum, [jnp.min(c) for c in cs]) < -SB_DEAD
                return lo, tuple(accs), tuple(cs), alive.astype(jnp.int32)

            zero_acc, zero_c = jnp.zeros((qb, _PAIR), F32), jnp.zeros((qb, 1), F32)
            lo, accs, cs, _ = lax.while_loop(
                live, window, ((i + 1) * qb, (zero_acc,) * nh, (zero_c,) * nh, jnp.int32(1)))
            for pp in range(ATT_PAIRS):
                o_ref[rows, pp * _PAIR:(pp + 1) * _PAIR] = jnp.where(
                    first_head, accs[2 * pp], accs[2 * pp + 1]).astype(BF16)
            for h in range(nh):
                l_ref[h, rows, :] = cs[h]
            stop_ref[g, i] = lo
            return 0

        lax.fori_loop(0, nq, qblock, 0)
        if ride is not None:
            pl.when(g == steps - 1)(finish)

    col = lambda off: pl.BlockSpec((seq, width), lambda b, p: (b, off + p))
    in_specs = [col(0), col(groups), col(2 * groups)]
    out_specs = [col(0), pl.BlockSpec((None, nh, seq, 1), lambda b, p: (b, p, 0, 0)), pl.BlockSpec(memory_space=pltpu.SMEM)]
    out_shape = [jax.ShapeDtypeStruct((bl * seq, WIDTH), BF16), jax.ShapeDtypeStruct((bl, HEADS, seq, 1), F32),
                 jax.ShapeDtypeStruct((bl * groups, nq), jnp.int32)]
    if ride is None:
        return pl.pallas_call(body, name="sb_fwd", grid=(bl, groups), in_specs=in_specs, out_specs=out_specs,
                              out_shape=out_shape, compiler_params=_params(("arbitrary", "arbitrary")))(qkv, qkv, qkv)
    return pl.pallas_call(
        body, name="sb_fwd", grid=(bl, groups), in_specs=in_specs + [_ANY], out_specs=out_specs + [_ANY],
        out_shape=out_shape + [jax.ShapeDtypeStruct((N_DEV,) + ride.shape, ride.dtype)], scratch_shapes=_GATHER_SEMS,
        compiler_params=_params(("arbitrary", "arbitrary")),
    )(qkv, qkv, qkv, ride)


def _sb_bwd(qkv, do, lsum, stop, bl, seq):
    qb, win, kb = min(ATT_Q, seq), min(ATT_WINDOW, seq), ATT_ALIGN
    nq = seq // qb
    nh, width, groups = 2 * ATT_PAIRS, ATT_PAIRS * _PAIR, _PAIRS // ATT_PAIRS
    pair_of = lambda h: slice((h // 2) * _PAIR, (h // 2 + 1) * _PAIR)

    def body(stop_ref, q_ref, k_ref, v_ref, do_ref, l_ref, dq_ref, dk_ref, dv_ref, dk_acc, dv_acc):
        g = pl.program_id(0) * groups + pl.program_id(1)
        first_head = _first_head_lanes()
        u_incl = _tri(kb, "le").astype(BF16)
        u_excl = _tri(kb, "lt").astype(BF16)
        dk_acc[...] = jnp.zeros_like(dk_acc)
        dv_acc[...] = jnp.zeros_like(dv_acc)

        def qblock(i, _):
            rows = pl.ds(pl.multiple_of(i * qb, qb), qb)
            qv = q_ref[rows, :]
            qs = qv * (HEAD_DIM ** -0.5)
            dob = do_ref[rows, :]
            qh = [_per_head(qs[:, pair_of(h)], first_head)[h % 2] for h in range(nh)]
            doh = [_per_head(dob[:, pair_of(h)], first_head)[h % 2] for h in range(nh)]
            ltot = [l_ref[h, rows, :] for h in range(nh)]

            first = (jnp.clip(stop_ref[g, i], 0, i * qb) // ATT_ALIGN) * ATT_ALIGN

            def window(n, carry):
                dqs, ps, es = (list(t) for t in carry)
                start = first + n * win
                lo = pl.multiple_of(jnp.minimum(start, seq - win), kb)
                cols = pl.ds(lo, win)
                kv, vv = k_ref[cols, :], v_ref[cols, :]
                valid = _sb_valid(i * qb, lo, start, seq, qb, win)
                dks, dvs = [], []
                for h in range(nh):
                    kp, vp = kv[:, pair_of(h)], vv[:, pair_of(h)]
                    z = _mm_nt(qh[h], kp)
                    sp = _sb_softplus(z)
                    spm = jnp.where(valid, sp, 0.0)
                    upto, ps[h] = _running_sums(spm, ps[h], u_incl, kb)
                    w = jnp.where(valid, jnp.exp(z - sp - (ltot[h] - upto)), 0.0)
                    e = _mm_nt(doh[h], vp) * w
                    dlf, es[h] = _running_sums(e, es[h], u_excl, kb)
                    sig = jnp.exp(z - sp)
                    dz = jnp.where(valid, e * (1.0 - sig) - dlf * sig, 0.0) * (HEAD_DIM ** -0.5)
                    dzb = dz.astype(BF16)
                    dvs.append(_mm_tn(w.astype(BF16), dob[:, pair_of(h)]))
                    dks.append(_mm_tn(dzb, qv[:, pair_of(h)]))
                    dqs[h] = dqs[h] + _mm(dzb, kp)
                for pp in range(ATT_PAIRS):
                    lanes = slice(pp * _PAIR, (pp + 1) * _PAIR)
                    dv_acc[cols, lanes] += jnp.where(first_head, dvs[2 * pp], dvs[2 * pp + 1])
                    dk_acc[cols, lanes] += jnp.where(first_head, dks[2 * pp], dks[2 * pp + 1])
                return tuple(dqs), tuple(ps), tuple(es)

            zero_q, zero_c = jnp.zeros((qb, _PAIR), F32), jnp.zeros((qb, 1), F32)
            dqs, _, _ = lax.fori_loop(0, ((i + 1) * qb - first + win - 1) // win, window,
                                      ((zero_q,) * nh, (zero_c,) * nh, (zero_c,) * nh))
            for pp in range(ATT_PAIRS):
                dq_ref[rows, pp * _PAIR:(pp + 1) * _PAIR] = jnp.where(
                    first_head, dqs[2 * pp], dqs[2 * pp + 1]).astype(BF16)
            return 0

        lax.fori_loop(0, nq, qblock, 0)
        dk_ref[...] = dk_acc[...].astype(BF16)
        dv_ref[...] = dv_acc[...].astype(BF16)

    col = lambda off: pl.BlockSpec((seq, width), lambda b, p, stop_ref: (b, off + p))
    return pl.pallas_call(
        body, name="sb_bwd",
        grid_spec=pltpu.PrefetchScalarGridSpec(
            num_scalar_prefetch=1, grid=(bl, groups),
            in_specs=[col(0), col(groups), col(2 * groups), col(0),
                      pl.BlockSpec((None, nh, seq, 1), lambda b, p, stop_ref: (b, p, 0, 0))],
            out_specs=[col(0), col(0), col(0)],
            scratch_shapes=[pltpu.VMEM((seq, width), F32), pltpu.VMEM((seq, width), F32)]),
        out_shape=[jax.ShapeDtypeStruct((bl * seq, WIDTH), BF16)] * 3,
        compiler_params=_params(("parallel", "parallel")),
    )(stop, qkv, qkv, qkv, do, lsum)


@jax.custom_vjp
def _lora_mm(x, w):
    return _mm(x.astype(BF16), w.astype(BF16))


_lora_mm.defvjp(
    lambda x, w: (_mm(x.astype(BF16), w.astype(BF16)), (x, w)),
    lambda res, ct: (_mm_nt(ct.astype(BF16), res[1].astype(BF16)), _mm_tn(res[0].astype(BF16), ct.astype(BF16))))


def _rw_prep_math(p, ps, mu, w0, w_up, a0, a_up, g_up, k_k, k_a):
    pm = p + (ps - p) * mu
    r, k, v = pm[:, :WIDTH], pm[:, WIDTH:2 * WIDTH], pm[:, 2 * WIDTH:3 * WIDTH]
    o = 3 * WIDTH
    xw, xa, xg = pm[:, o:o + W_LORA], pm[:, o + W_LORA:o + W_LORA + A_LORA], pm[:, o + W_LORA + A_LORA:]
    w_raw = w0 + _lora_mm(jnp.tanh(xw), w_up)
    lw = -jnp.exp(-_softplus(-w_raw) - 0.5)
    a = jax.nn.sigmoid(a0 + _lora_mm(xa, a_up))
    g = _lora_mm(jax.nn.sigmoid(xg), g_up)
    kk = k * k_k
    k2 = k * (1.0 + (a - 1.0) * k_a)
    return r, lw, k2, v, kk, a, g


def _shift_down(p, first_row):
    row = lax.broadcasted_iota(jnp.int32, p.shape, 0)
    return jnp.where(row == 0, first_row, pltpu.roll(p, 1, 0))


def _shift_up(p, last_row):
    row = lax.broadcasted_iota(jnp.int32, p.shape, 0)
    return jnp.where(row == p.shape[0] - 1, last_row, pltpu.roll(p, p.shape[0] - 1, 0))


_PREP_PARAM_SHAPES = [(1, RW_COLS), (1, WIDTH), (W_LORA, WIDTH), (1, WIDTH), (A_LORA, WIDTH), (G_LORA, WIDTH),
                      (1, WIDTH), (1, WIDTH)]


def _prev_rows_spec(tm):
    return pl.BlockSpec((8, RW_COLS), lambda i: (jnp.maximum(i * (tm // 8) - 1, 0), 0))


def _head_spec(tm, seq, tile_of=lambda i: i):
    per_seq = seq // tm
    return pl.BlockSpec((None, HEADS, tm, HEAD_DIM),
                        lambda i: (tile_of(i) // per_seq, 0, tile_of(i) % per_seq, 0))


def _split_heads(val, ref):
    for h in range(HEADS):
        ref[h] = val[:, h * HEAD_DIM:(h + 1) * HEAD_DIM]


def _join_heads(ref):
    return jnp.concatenate([ref[h] for h in range(HEADS)], axis=1)


def _rw_prep_fwd(prw, params, seq, tm):
    T = prw.shape[0]

    def body(p_ref, prev_ref, *rest):
        prm = [r_[...] for r_ in rest[:8]]
        outs = rest[8:]
        i = pl.program_id(0)
        first = jnp.where((i * tm) % seq == 0, 0.0, prev_ref[7:8, :])
        p = p_ref[...]
        vals = _rw_prep_math(p, _shift_down(p, first), *prm)
        for o_ref, val in zip(outs[:6], vals[:6]):
            _split_heads(val, o_ref)
        outs[6][...] = vals[6]

    by_head = jax.ShapeDtypeStruct((T // seq, HEADS, seq, HEAD_DIM), F32)
    return pl.pallas_call(
        body, name="rw_prep_fwd", grid=(T // tm,),
        in_specs=[_row_spec(tm, RW_COLS), _prev_rows_spec(tm)] + [_const_spec(s) for s in _PREP_PARAM_SHAPES],
        out_specs=[_head_spec(tm, seq)] * 6 + [_row_spec(tm, WIDTH)],
        out_shape=[by_head] * 6 + [jax.ShapeDtypeStruct((T, WIDTH), F32)],
        compiler_params=_params(("parallel",)),
    )(prw, prw, *params)


def _rw_prep_bwd(prw, params, cts, seq, tm):
    T = prw.shape[0]
    n = T // tm

    def body(p_ref, prev_ref, *rest):
        prm = [r_[...] for r_ in rest[:8]]
        ct = tuple(_join_heads(r_) for r_ in rest[8:14]) + (rest[14][...],)
        dp_ref = rest[15]
        dprm_refs = rest[16:24]
        carry = rest[24]
        step = pl.program_id(0)
        i = n - 1 - step
        first = jnp.where((i * tm) % seq == 0, 0.0, prev_ref[7:8, :])
        p = p_ref[...]
        _, vjp = jax.vjp(_rw_prep_math, p, _shift_down(p, first), *prm)
        grads = vjp(ct)
        dp, dps = grads[0], grads[1]
        nxt = jnp.where(jnp.logical_or(step == 0, ((i + 1) * tm) % seq == 0), 0.0, carry[0:1, :])
        dp_ref[...] = (dp + _shift_up(dps, nxt)).astype(BF16)
        carry[...] = dps[0:8, :]
        for ref, gval in zip(dprm_refs, grads[2:]):
            _acc_out(ref, gval, step == 0)

    rev = lambda w: pl.BlockSpec((tm, w), lambda s: (n - 1 - s, 0))
    prev = pl.BlockSpec((8, RW_COLS), lambda s: (jnp.maximum((n - 1 - s) * (tm // 8) - 1, 0), 0))
    return pl.pallas_call(
        body, name="rw_prep_bwd", grid=(n,),
        in_specs=([rev(RW_COLS), prev] + [_const_spec(s) for s in _PREP_PARAM_SHAPES]
                  + [_head_spec(tm, seq, lambda s: n - 1 - s)] * 6 + [rev(WIDTH)]),
        out_specs=[rev(RW_COLS)] + [pl.BlockSpec(s, lambda s_: (0, 0)) for s in _PREP_PARAM_SHAPES],
        out_shape=[jax.ShapeDtypeStruct((T, RW_COLS), BF16)] + [jax.ShapeDtypeStruct(s, F32) for s in _PREP_PARAM_SHAPES],
        scratch_shapes=[pltpu.VMEM((8, RW_COLS), F32)],
        compiler_params=_params(("arbitrary",)),
    )(prw, prw, *params, *cts)


def _make_bmm(passes):
    def raw(dn, a, b):
        d = lambda x, y: lax.dot_general(x, y, dn, preferred_element_type=F32)
        ah = a.astype(BF16)
        bh = b.astype(BF16)
        if passes == 1:
            return d(ah, bh)
        al = (a - ah.astype(F32)).astype(BF16)
        bl = (b - bh.astype(F32)).astype(BF16)
        return d(ah, bh) + (d(ah, bl) + d(al, bh))

    dn_nn = (((2,), (1,)), ((0,), (0,)))
    dn_nt = (((2,), (2,)), ((0,), (0,)))
    dn_tn = (((1,), (1,)), ((0,), (0,)))

    @jax.custom_vjp
    def nn(a, b):
        return raw(dn_nn, a, b)

    @jax.custom_vjp
    def nt(a, b):
        return raw(dn_nt, a, b)

    @jax.custom_vjp
    def tn(a, b):
        return raw(dn_tn, a, b)

    nn.defvjp(lambda a, b: (raw(dn_nn, a, b), (a, b)), lambda res, ct: (nt(ct, res[1]), tn(res[0], ct)))
    nt.defvjp(lambda a, b: (raw(dn_nt, a, b), (a, b)), lambda res, ct: (nn(ct, res[1]), tn(ct, res[0])))
    tn.defvjp(lambda a, b: (raw(dn_tn, a, b), (a, b)), lambda res, ct: (nt(res[1], ct), nn(res[0], ct)))

    def unit_lower_inverse(m):
        n = m.shape[-1]
        row = lax.broadcasted_iota(jnp.int32, (n, n), 0)
        col = lax.broadcasted_iota(jnp.int32, (n, n), 1)
        m16 = ((row // 16) == (col // 16)).astype(F32)
        m32 = ((row // 32) == (col // 32)).astype(F32)
        a1 = m * m16
        a2 = nn(a1, a1)
        a4 = nn(a2, a2)
        a8 = nn(a4, a4)
        inv = (row == col).astype(F32) - a1
        inv = inv + nn(inv, a2)
        inv = inv + nn(inv, a4)
        inv = inv + nn(inv, a8)
        inv = inv - nn(nn(inv, m * (m32 - m16)), inv)
        return inv - nn(nn(inv, m * (1.0 - m32)), inv)

    @jax.custom_vjp
    def inverse(m):
        return unit_lower_inverse(m)

    def inverse_fwd(m):
        inv = unit_lower_inverse(m)
        return inv, inv

    inverse.defvjp(inverse_fwd, lambda inv, ct: (-nt(tn(inv, ct), inv),))
    return nn, nt, tn, inverse


def _wkv_chunk(s0, r, lw, k, v, kk, a, lnw, lnb, rk):
    nn, nt, tn, inverse = _make_bmm(SCAN_PASSES)
    G, L, N = r.shape
    rep = lambda t: jnp.broadcast_to(t[None], (G // HEADS, HEADS, 1, N)).reshape(G, 1, N)
    kap = kk * lax.rsqrt(jnp.maximum(jnp.sum(kk * kk, axis=-1, keepdims=True), 1e-24))
    b = a * kap
    row = lax.broadcasted_iota(jnp.int32, (L, L), 0)
    col = lax.broadcasted_iota(jnp.int32, (L, L), 1)
    low_incl = (col <= row).astype(F32)
    low_strict = (col < row).astype(F32)
    c = _make_bmm(3)[0](jnp.broadcast_to(low_incl[None], (G, L, L)), lw)
    c_all = jnp.sum(lw, axis=1, keepdims=True)
    g_inv = jnp.exp(-c)
    kap_t = kap * jnp.exp(c - lw)
    b_t = b * g_inv
    k_t = k * g_inv
    r_t = r * jnp.exp(c)
    g_all = jnp.exp(c_all)
    m_b = nt(kap_t, b_t) * low_strict
    m_k = nt(kap_t, k_t) * low_strict
    n_b = nt(r_t, b_t) * low_incl
    n_k = nt(r_t, k_t) * low_incl
    rhs = -(nt(kap_t, s0) + nn(m_k, v))
    sa = nn(inverse(m_b), rhs)
    y = nt(r_t, s0) + nn(n_b, sa) + nn(n_k, v)
    s1 = s0 * g_all + tn(sa, b_t * g_all) + tn(v, k_t * g_all)
    mean = jnp.mean(y, axis=-1, keepdims=True)
    yc = y - mean
    var = jnp.mean(yc * yc, axis=-1, keepdims=True)
    out = yc * lax.rsqrt(var + GN_EPS) * rep(lnw) + rep(lnb)
    out = out + jnp.sum(r * k * rep(rk), axis=-1, keepdims=True) * v
    return out, s1


def _scan_heads_per_step(total_heads, seqs_wanted):
    n_seq = total_heads // HEADS
    return HEADS * max(d for d in range(1, seqs_wanted + 1) if n_seq % d == 0)


def _wkv_fwd(seqs, lnw, lnb, rk):
    G, S, N = seqs[0].shape
    L = SCAN_CHUNK
    nc = S // L

    def body(*refs):
        ins = [r_[...] for r_ in refs[:6]]
        prm = [r_[...] for r_ in refs[6:9]]
        out_ref, st_ref, state = refs[9], refs[10], refs[11]

        @pl.when(pl.program_id(1) == 0)
        def _():
            state[...] = jnp.zeros_like(state)

        s0 = state[...]
        st_ref[...] = s0
        out, s1 = _wkv_chunk(s0, *ins, *prm)
        out_ref[...] = out
        state[...] = s1

    gb = _scan_heads_per_step(G, SCAN_SEQS_FWD)
    blk = pl.BlockSpec((gb, L, N), lambda b, i: (b, i, 0))
    pspec = _const_spec((HEADS, 1, N))
    return pl.pallas_call(
        body, name="wkv_fwd", grid=(G // gb, nc), in_specs=[blk] * 6 + [pspec] * 3,
        out_specs=[blk, pl.BlockSpec((None, gb, N, N), lambda b, i: (i, b, 0, 0))],
        out_shape=[jax.ShapeDtypeStruct((G, S, N), F32), jax.ShapeDtypeStruct((nc, G, N, N), F32)],
        scratch_shapes=[pltpu.VMEM((gb, N, N), F32)],
        compiler_params=_params(("parallel", "arbitrary")),
    )(*seqs, lnw, lnb, rk)


def _wkv_bwd(seqs, states, dout, lnw, lnb, rk, ride=None):
    G, S, N = seqs[0].shape
    L = SCAN_CHUNK
    nc = S // L
    gb = _scan_heads_per_step(G, SCAN_SEQS_BWD)

    def body(*refs):
        ins = [r_[...] for r_ in refs[:6]]
        s0 = refs[6][...]
        ct_out = refs[7][...]
        prm = [r_[...] for r_ in refs[8:11]]
        refs = refs[11:]
        if ride is not None:
            start, finish = _scatter_steps(refs[0], refs[10], refs[12], refs[13])
            pl.when(jnp.logical_and(pl.program_id(0) == 0, pl.program_id(1) == 0))(start)
            refs = refs[1:]
        d_refs = refs[0:6]
        dprm_refs = refs[6:9]
        dstate = refs[10] if ride is not None else refs[9]
        step = pl.program_id(1)

        @pl.when(step == 0)
        def _():
            dstate[...] = jnp.zeros_like(dstate)

        _, vjp = jax.vjp(_wkv_chunk, s0, *ins, *prm)
        grads = vjp((ct_out, dstate[...]))
        dstate[...] = grads[0]
        for ref, gval in zip(d_refs, grads[1:7]):
            ref[...] = gval
        for ref, gval in zip(dprm_refs, grads[7:]):
            _acc_out(ref, gval, jnp.logical_and(step == 0, pl.program_id(0) == 0))
        if ride is not None:
            pl.when(jnp.logical_and(pl.program_id(0) == G // gb - 1, step == nc - 1))(finish)

    blk = pl.BlockSpec((gb, L, N), lambda b, s: (b, nc - 1 - s, 0))
    pspec = _const_spec((HEADS, 1, N))
    pout = pl.BlockSpec((HEADS, 1, N), lambda b, s: (0, 0, 0))
    in_specs = [blk] * 6 + [pl.BlockSpec((None, gb, N, N), lambda b, s: (nc - 1 - s, b, 0, 0)), blk] + [pspec] * 3
    out_specs = [blk] * 6 + [pout] * 3
    out_shape = [jax.ShapeDtypeStruct((G, S, N), F32)] * 6 + [jax.ShapeDtypeStruct((HEADS, 1, N), F32)] * 3
    scratch = [pltpu.VMEM((gb, N, N), F32)]
    args = (*seqs, states, dout, lnw, lnb, rk)
    if ride is not None:
        in_specs, out_specs, args = in_specs + [_ANY], out_specs + [_ANY], args + (ride,)
        out_shape = out_shape + [jax.ShapeDtypeStruct((N_DEV - 1,) + ride.shape[1:], ride.dtype)]
        scratch = scratch + [pltpu.SemaphoreType.DMA((N_DEV - 1,)), pltpu.SemaphoreType.DMA((N_DEV - 1,))]
    return pl.pallas_call(
        body, name="wkv_bwd", grid=(G // gb, nc), in_specs=in_specs, out_specs=out_specs, out_shape=out_shape,
        scratch_shapes=scratch, compiler_params=_params(("arbitrary", "arbitrary")),
    )(*args)


def _merge_math(o_sb, rw_out, g_rw, gates, w_sb, w_rw, w_o):
    o_rw = (rw_out * g_rw).astype(BF16)
    a = _mm(o_sb, w_sb)
    b = _mm(o_rw, w_rw)
    g1, g2 = gates[:, :D_MODEL], gates[:, D_MODEL:]
    merged = (g1 * a + g2 * b).astype(BF16)
    return o_rw, a, b, g1, g2, merged, _mm(merged, w_o)


def _merge_fwd(x2, o_sb, rw_out, g_rw, gates, w_sb, w_rw, w_o, g_post, seq, tm):
    T = x2.shape[0]

    def body(x_ref, osb_ref, rw_ref, g_ref, gate_ref, wsb_ref, wrw_ref, wo_ref, gp_ref, x1_ref):
        z = _merge_math(osb_ref[...], _join_heads(rw_ref), g_ref[...], gate_ref[...], wsb_ref[...], wrw_ref[...], wo_ref[...])[-1]
        x1_ref[...] = x_ref[...] + _rms_fwd(z, gp_ref[...])[0]

    return pl.pallas_call(
        body, name="merge_fwd", grid=(T // tm,),
        in_specs=[_row_spec(tm, D_MODEL), _row_spec(tm, WIDTH), _head_spec(tm, seq), _row_spec(tm, WIDTH),
                  _row_spec(tm, GATE_COLS), _const_spec((WIDTH, D_MODEL)), _const_spec((WIDTH, D_MODEL)),
                  _const_spec((D_MODEL, D_MODEL)), _const_spec((1, D_MODEL))],
        out_specs=_row_spec(tm, D_MODEL),
        out_shape=jax.ShapeDtypeStruct((T, D_MODEL), F32),
        compiler_params=_params(("parallel",)),
    )(x2, o_sb, rw_out, g_rw, gates, w_sb, w_rw, w_o, g_post)


def _merge_bwd(dx1, o_sb, rw_out, g_rw, gates, w_sb, w_rw, w_o, g_post, seq, tm):
    T = dx1.shape[0]

    def body(dx1_ref, osb_ref, rw_ref, g_ref, gate_ref, wsb_ref, wrw_ref, wo_ref, gp_ref,
             orw_o, mrg_o, dz_o, da_o, db_o, dgate_o, dosb_o, drw_o, dg_o, dgp_o, dbg_o):
        rw_out_v, g_rw_v = _join_heads(rw_ref), g_ref[...]
        w_sb_v, w_rw_v, w_o_v = wsb_ref[...], wrw_ref[...], wo_ref[...]
        o_rw, a, b, g1, g2, merged, z = _merge_math(osb_ref[...], rw_out_v, g_rw_v, gate_ref[...], w_sb_v, w_rw_v, w_o_v)
        gain = gp_ref[...]
        _, zn, rstd = _rms_fwd(z, gain)
        dz, dgain = _rms_bwd(dx1_ref[...], zn, rstd, gain)
        dzb = dz.astype(BF16)
        dm = _mm_nt(dzb, w_o_v)
        dab = (dm * g1).astype(BF16)
        dbb = (dm * g2).astype(BF16)
        dgate = jnp.concatenate([dm * a * g1 * (1.0 - g1), dm * b * g2 * (1.0 - g2)], axis=1)
        do_rw = _mm_nt(dbb, w_rw_v)
        orw_o[...] = o_rw
        mrg_o[...] = merged
        dz_o[...] = dzb
        da_o[...] = dab
        db_o[...] = dbb
        dgate_o[...] = dgate.astype(BF16)
        dosb_o[...] = _mm_nt(dab, w_sb_v).astype(BF16)
        _split_heads(do_rw * g_rw_v, drw_o)
        dg_o[...] = do_rw * rw_out_v
        first = pl.program_id(0) == 0
        _acc_out(dgp_o, dgain, first)
        _acc_out(dbg_o, jnp.sum(dgate, axis=0, keepdims=True), first)

    acc = lambda n: pl.BlockSpec((1, n), lambda i: (0, 0))
    sd = jax.ShapeDtypeStruct
    return pl.pallas_call(
        body, name="merge_bwd", grid=(T // tm,),
        in_specs=[_row_spec(tm, D_MODEL), _row_spec(tm, WIDTH), _head_spec(tm, seq), _row_spec(tm, WIDTH),
                  _row_spec(tm, GATE_COLS), _const_spec((WIDTH, D_MODEL)), _const_spec((WIDTH, D_MODEL)),
                  _const_spec((D_MODEL, D_MODEL)), _const_spec((1, D_MODEL))],
        out_specs=[_row_spec(tm, WIDTH), _row_spec(tm, D_MODEL), _row_spec(tm, D_MODEL), _row_spec(tm, D_MODEL),
                   _row_spec(tm, D_MODEL), _row_spec(tm, GATE_COLS), _row_spec(tm, WIDTH), _head_spec(tm, seq),
                   _row_spec(tm, WIDTH), acc(D_MODEL), acc(GATE_COLS)],
        out_shape=[sd((T, WIDTH), BF16), sd((T, D_MODEL), BF16), sd((T, D_MODEL), BF16), sd((T, D_MODEL), BF16),
                   sd((T, D_MODEL), BF16), sd((T, GATE_COLS), BF16), sd((T, WIDTH), BF16),
                   sd((T // seq, HEADS, seq, HEAD_DIM), F32), sd((T, WIDTH), F32), sd((1, D_MODEL), F32),
                   sd((1, GATE_COLS), F32)],
        compiler_params=_params(("arbitrary",)),
    )(dx1, o_sb, rw_out, g_rw, gates, w_sb, w_rw, w_o, g_post)


def _ffn(x1, target, g_pre, g_post, w_gate, w_up, w_down, tm):
    T = x1.shape[0]

    def body(x1_ref, tgt_ref, gpre_ref, gpost_ref, wg_ref, wu_ref, wd_ref,
             loss_o, dx1_o, h_o, dgate_o, dup_o, act_o, df_o, dgpre_o, dgpost_o):
        x1v = x1_ref[...]
        gpre, gpost = gpre_ref[...], gpost_ref[...]
        wg, wu, wd = wg_ref[...], wu_ref[...], wd_ref[...]
        hn, xn1, rstd1 = _rms_fwd(x1v, gpre)
        h = hn.astype(BF16)
        gate = _mm(h, wg)
        up = _mm(h, wu)
        sg = jax.nn.sigmoid(gate)
        act = (gate * sg * up).astype(BF16)
        f = _mm(act, wd)
        fo, fn, rstd2 = _rms_fwd(f, gpost)
        diff = x1v + fo - tgt_ref[...]
        dy = diff * (1.0 / D_MODEL)
        df, dgpost = _rms_bwd(dy, fn, rstd2, gpost)
        dfb = df.astype(BF16)
        dact = _mm_nt(dfb, wd)
        dup = (dact * gate * sg).astype(BF16)
        dgate = (dact * up * (sg * (1.0 + gate * (1.0 - sg)))).astype(BF16)
        dh = _mm_nt(dgate, wg) + _mm_nt(dup, wu)
        dxn, dgpre = _rms_bwd(dh, xn1, rstd1, gpre)
        dx1_o[...] = dy + dxn
        h_o[...] = h
        dgate_o[...] = dgate
        dup_o[...] = dup
        act_o[...] = act
        df_o[...] = dfb
        first = pl.program_id(0) == 0
        part = jnp.sum(jnp.sum(diff * diff, axis=1, keepdims=True), axis=0, keepdims=True) * (0.5 / D_MODEL)
        _acc_out(loss_o, jnp.broadcast_to(part, (8, 128)), first)
        _acc_out(dgpre_o, dgpre, first)
        _acc_out(dgpost_o, dgpost, first)

    acc = lambda r, n: pl.BlockSpec((r, n), lambda i: (0, 0))
    sd = jax.ShapeDtypeStruct
    return pl.pallas_call(
        body, name="ffn", grid=(T // tm,),
        in_specs=[_row_spec(tm, D_MODEL), _row_spec(tm, D_MODEL), _const_spec((1, D_MODEL)), _const_spec((1, D_MODEL)),
                  _const_spec((D_MODEL, D_FF)), _const_spec((D_MODEL, D_FF)), _const_spec((D_FF, D_MODEL))],
        out_specs=[acc(8, 128), _row_spec(tm, D_MODEL), _row_spec(tm, D_MODEL), _row_spec(tm, D_FF), _row_spec(tm, D_FF),
                   _row_spec(tm, D_FF), _row_spec(tm, D_MODEL), acc(1, D_MODEL), acc(1, D_MODEL)],
        out_shape=[sd((8, 128), F32), sd((T, D_MODEL), F32), sd((T, D_MODEL), BF16), sd((T, D_FF), BF16),
                   sd((T, D_FF), BF16), sd((T, D_FF), BF16), sd((T, D_MODEL), BF16), sd((1, D_MODEL), F32),
                   sd((1, D_MODEL), F32)],
        compiler_params=_params(("arbitrary",)),
    )(x1, target, g_pre, g_post, w_gate, w_up, w_down)


def _local_step(x, target, sm, wt, late=None):
    bl, seq, _ = x.shape
    T = bl * seq
    tm = min(ROW_TILE, T)
    x2 = x.reshape(T, D_MODEL)
    tgt2 = target.reshape(T, D_MODEL)
    h, qkv, prw, gates = _in_proj_fwd(x2, sm["norm_mix_pre"], wt["w_in"], sm["b_gate"], tm)
    if late is None:
        o_sb, lsum, sb_stop = _sb_fwd(qkv, bl, seq)
    else:
        o_sb, lsum, sb_stop, gathered = _sb_fwd(qkv, bl, seq, late)
        wt = {**wt, **_unpack_gathered(gathered, slice(_EARLY, None))}
    prep_params = [sm["mu_rw"], sm["w0"], wt["w_up"].astype(F32), sm["a0"], wt["a_up"].astype(F32),
                   wt["g_up"].astype(F32), sm["k_k"], sm["k_a"]]
    prep = _rw_prep_fwd(prw, prep_params, seq, tm)
    by_head = lambda t: t.reshape(bl, HEADS, seq, HEAD_DIM)
    seqs = [t.reshape(bl * HEADS, seq, HEAD_DIM) for t in prep[:6]]
    g_rw = prep[6]
    lnw, lnb, rk = (sm[n].reshape(HEADS, 1, HEAD_DIM) for n in ("lnx_w", "lnx_b", "r_k"))
    rw_out_h, states = _wkv_fwd(seqs, lnw, lnb, rk)
    rw_out = by_head(rw_out_h)
    x1 = _merge_fwd(x2, o_sb, rw_out, g_rw, gates, wt["w_sb_out"], wt["w_rw_out"], wt["w_o"], sm["norm_mix_post"],
                    seq, tm)
    (loss_part, dx1, h2, dffg, dffu, act, dff, d_nfpre, d_nfpost) = _ffn(
        x1, tgt2, sm["norm_ffn_pre"], sm["norm_ffn_post"], wt["w_ffn_gate"], wt["w_ffn_up"], wt["w_ffn_down"],
        min(ROW_TILE_FFN, T))
    (o_rw, merged, dz, da, db, dgate, do_sb, d_rw_out, d_g_rw, d_npost, d_bgate) = _merge_bwd(
        dx1, o_sb, rw_out, g_rw, gates, wt["w_sb_out"], wt["w_rw_out"], wt["w_o"], sm["norm_mix_post"], seq, tm)
    gdt = F32 if late is None else BF16
    gw = {
        "w_sb_out": _grad_w(o_sb, da, "gw_sb_out", gdt), "w_rw_out": _grad_w(o_rw, db, "gw_rw_out", gdt),
        "w_o": _grad_w(merged, dz, "gw_o", gdt),
        "w_ffn_gate": _grad_w(h2, dffg, "gw_ffn_gate", gdt), "w_ffn_up": _grad_w(h2, dffu, "gw_ffn_up", gdt),
        "w_ffn_down": _grad_w(act, dff, "gw_ffn_down", gdt),
    }
    dqkv = jnp.concatenate(_sb_bwd(qkv, do_sb, lsum, sb_stop, bl, seq), axis=1)
    ride = None if late is None else _pack_full_grads(gw, slice(_EARLY, None))
    wkv_g = _wkv_bwd(seqs, states, d_rw_out.reshape(bl * HEADS, seq, HEAD_DIM), lnw, lnb, rk, ride)
    late_grads = None if late is None else (ride, wkv_g[9])
    cts = [by_head(t) for t in wkv_g[:6]] + [d_g_rw]
    prep_g = _rw_prep_bwd(prw, prep_params, cts, seq, tm)
    dprw = prep_g[0]
    d_mu, d_w0, d_wup, d_a0, d_aup, d_gup, d_kk, d_ka = prep_g[1:]
    grad_x, d_npre = _in_proj_bwd(x2, sm["norm_mix_pre"], dx1, dqkv, dprw, dgate, wt["w_in"], tm)
    gw = {
        **gw,
        "w_in": jnp.concatenate([_grad_w(h, dqkv, "gw_in_qkv"), _grad_w(h, dprw, "gw_in_rw"), _grad_w(h, dgate, "gw_in_gate")], axis=1),
        "w_up": d_wup, "a_up": d_aup, "g_up": d_gup,
    }
    gs = {
        "norm_mix_pre": d_npre, "b_gate": d_bgate, "mu_rw": d_mu, "w0": d_w0, "a0": d_a0, "k_k": d_kk, "k_a": d_ka,
        "r_k": wkv_g[8].reshape(1, WIDTH), "lnx_w": wkv_g[6].reshape(1, WIDTH), "lnx_b": wkv_g[7].reshape(1, WIDTH),
        "norm_mix_post": d_npost, "norm_ffn_pre": d_nfpre, "norm_ffn_post": d_nfpost,
    }
    return loss_part, grad_x.reshape(x.shape), gw, gs, late_grads


_SHARDED = [("w_in", 1, (D_MODEL, (SB_COLS + RW_COLS + GATE_COLS) // N_DEV)), ("w_up", 1, (W_LORA, WIDTH // N_DEV)),
            ("a_up", 1, (A_LORA, WIDTH // N_DEV)), ("g_up", 1, (G_LORA, WIDTH // N_DEV)),
            ("w_sb_out", 1, (WIDTH, D_MODEL // N_DEV)), ("w_rw_out", 1, (WIDTH, D_MODEL // N_DEV)),
            ("w_o", 0, (D_MODEL // N_DEV, D_MODEL)), ("w_ffn_gate", 1, (D_MODEL, D_FF // N_DEV)),
            ("w_ffn_up", 1, (D_MODEL, D_FF // N_DEV)), ("w_ffn_down", 0, (D_FF // N_DEV, D_MODEL))]
_LANES = 128
_PACK_ROWS = [s[0] * s[1] // _LANES for _, _, s in _SHARDED]
_PACK_TOTAL = sum(_PACK_ROWS)
_SMALL = [("norm_mix_pre", D_MODEL), ("b_gate", GATE_COLS), ("mu_rw", RW_COLS), ("w0", WIDTH), ("a0", WIDTH),
          ("k_k", WIDTH), ("k_a", WIDTH), ("r_k", WIDTH), ("lnx_w", WIDTH), ("lnx_b", WIDTH),
          ("norm_mix_post", D_MODEL), ("norm_ffn_pre", D_MODEL), ("norm_ffn_post", D_MODEL)]
_SMALL_ROWS = 96


def _pack_shards(shards, dtype):
    return jnp.concatenate([shards[n].astype(dtype).reshape(-1, _LANES) for n, _, _ in _SHARDED], axis=0)


def _unpack_shards(packed, which):
    out, r0 = {}, 0
    for (n, _, shp), rows in zip(_SHARDED[which], _PACK_ROWS[which]):
        out[n] = packed[r0:r0 + rows].reshape(shp)
        r0 += rows
    return out


_EARLY = 4
_EARLY_ROWS = sum(_PACK_ROWS[:_EARLY])


def _unpack_gathered(g, which):
    out, r0 = {}, 0
    for (n, axis, shp), rows in zip(_SHARDED[which], _PACK_ROWS[which]):
        blk = g[:, r0:r0 + rows].reshape((N_DEV,) + shp)
        out[n] = blk.reshape(N_DEV * shp[0], shp[1]) if axis == 0 else blk.transpose(1, 0, 2).reshape(shp[0], N_DEV * shp[1])
        r0 += rows
    return out


def _pack_full_grads(gw, which, by_core=False):
    parts = []
    for n, axis, shp in _SHARDED[which]:
        g = gw[n]
        if by_core:
            blk = (g.reshape((4, 2) + shp).transpose(1, 0, 2, 3) if axis == 0 else
                   g.reshape(shp[0], 4, 2, shp[1]).transpose(2, 1, 0, 3))
            parts.append(blk.reshape(2, 4, -1, _LANES))
        else:
            blk = g.reshape((N_DEV,) + shp) if axis == 0 else g.reshape(shp[0], N_DEV, shp[1]).transpose(1, 0, 2)
            parts.append(blk.reshape(N_DEV, -1, _LANES))
    return jnp.concatenate(parts, axis=-2)


def _pack_small(vals, extra=None):
    used = sum(sz for _, sz in _SMALL)
    tail = jnp.zeros((1, _SMALL_ROWS * _LANES - used), F32).at[0, 0].set(extra)
    return jnp.concatenate([vals[n].reshape(1, -1) for n, _ in _SMALL] + [tail], axis=1)


_ANY = pl.BlockSpec(memory_space=pl.ANY)


def _all_gather(block):
    rows, lanes = block.shape

    def body(x_ref, out_ref, send_sems, recv_sems, local_sem):
        start, forward, finish = _gather_steps(x_ref, out_ref, send_sems, recv_sems, local_sem)
        start()
        forward()
        finish()

    return pl.pallas_call(
        body, name="all_gather_weights", in_specs=[_ANY], out_specs=_ANY,
        out_shape=jax.ShapeDtypeStruct((N_DEV, rows, lanes), block.dtype), scratch_shapes=_GATHER_SEMS,
    )(block)


def _scatter_steps(pack_ref, got_ref, send_sems, recv_sems):
    x, y, c = lax.axis_index("x"), lax.axis_index("y"), lax.axis_index("c")

    def copies():
        out = []
        for k in range(1, N_DEV):
            px, py, pc = x ^ (k >> 2), y ^ ((k >> 1) & 1), c ^ (k & 1)
            out.append(pltpu.make_async_remote_copy(
                src_ref=pack_ref.at[4 * px + 2 * py + pc], dst_ref=got_ref.at[k - 1], send_sem=send_sems.at[k - 1],
                recv_sem=recv_sems.at[k - 1], device_id=(px, py, pc), device_id_type=MESH))
        return out

    def start():
        for cp in copies():
            cp.start()

    def finish():
        for cp in copies():
            cp.wait_recv()
        for cp in copies():
            cp.wait_send()

    return start, finish


def _sum_direct(pack, got, me):
    _, rows, lanes = pack.shape
    tr = _pick_rows(rows)

    def body(me_ref, own_ref, got_ref, o_ref):
        total = own_ref[...].astype(F32)
        for k in range(N_DEV - 1):
            total = total + got_ref[k].astype(F32)
        o_ref[...] = total

    return pl.pallas_call(
        body, name="sum_direct",
        grid_spec=pltpu.PrefetchScalarGridSpec(
            num_scalar_prefetch=1, grid=(rows // tr,),
            in_specs=[pl.BlockSpec((None, tr, lanes), lambda i, me_ref: (me_ref[0], i, 0)),
                      pl.BlockSpec((N_DEV - 1, tr, lanes), lambda i, me_ref: (0, i, 0))],
            out_specs=pl.BlockSpec((tr, lanes), lambda i, me_ref: (i, 0))),
        out_shape=jax.ShapeDtypeStruct((rows, lanes), F32),
        compiler_params=_params(("parallel",)),
    )(me, pack, got)


def _pick_rows(rows, cap=2048):
    return max(t for t in range(16, cap + 1, 16) if rows % t == 0)


_GATHER_SEMS = [pltpu.SemaphoreType.DMA((7,)), pltpu.SemaphoreType.DMA((7,)), pltpu.SemaphoreType.DMA]


def _gather_steps(x_ref, out_ref, send_sems, recv_sems, local_sem):
    x, y, c = lax.axis_index("x"), lax.axis_index("y"), lax.axis_index("c")
    me, sibling = (x, y, c), (x, y, 1 - c)
    chips = [(1 - x, y), (x, 1 - y), (1 - x, 1 - y)]

    def slot(px, py, pc):
        return out_ref.at[4 * px + 2 * py + pc]

    def copy(k, blk, to, src=None):
        return pltpu.make_async_remote_copy(
            src_ref=slot(*blk) if src is None else src, dst_ref=slot(*blk),
            send_sem=send_sems.at[k], recv_sem=recv_sems.at[k], device_id=to, device_id_type=MESH)

    def first():
        return [copy(0, me, sibling, src=x_ref)] + [copy(1 + j, me, (*chip, c), src=x_ref) for j, chip in enumerate(chips)]

    def passed():
        return [copy(4 + j, (*chip, c), sibling) for j, chip in enumerate(chips)]

    def start():
        pltpu.make_async_copy(x_ref, slot(*me), local_sem).start()
        for cp in first():
            cp.start()

    def forward():
        for j, (chip, cp) in enumerate(zip(chips, passed())):
            copy(1 + j, (*chip, c), me).wait_recv()
            cp.start()

    def finish():
        copy(0, sibling, me).wait_recv()
        for j, chip in enumerate(chips):
            copy(4 + j, (*chip, 1 - c), me).wait_recv()
        for cp in first() + passed():
            cp.wait_send()
        pltpu.make_async_copy(x_ref, slot(*me), local_sem).wait()

    return start, forward, finish


def _exchange_core(pack, small):
    _, _, rows, lanes = pack.shape

    def body(pack_ref, small_ref, got_ref, parts_ref, send_sems, recv_sems, s_send, s_recv, local_sem):
        x, y, c = lax.axis_index("x"), lax.axis_index("y"), lax.axis_index("c")
        sibling = (x, y, 1 - c)
        me = 4 * x + 2 * y + c
        mine = pltpu.make_async_copy(small_ref, parts_ref.at[me], local_sem)
        mine.start()
        big = [pltpu.make_async_remote_copy(
            src_ref=pack_ref.at[1 - c, j], dst_ref=got_ref.at[j], send_sem=send_sems.at[j], recv_sem=recv_sems.at[j],
            device_id=sibling, device_id_type=MESH) for j in range(4)]
        for cp in big:
            cp.start()
        others = [(k, (x ^ (k >> 2), y ^ ((k >> 1) & 1), c ^ (k & 1))) for k in range(1, N_DEV)]
        tiny = [pltpu.make_async_remote_copy(
            src_ref=small_ref, dst_ref=parts_ref.at[me], send_sem=s_send.at[k], recv_sem=s_recv.at[k],
            device_id=to, device_id_type=MESH) for k, to in others]
        for cp in tiny:
            cp.start()
        for cp in big:
            cp.wait_recv()
        for (k, (px, py, pc)), cp in zip(others, tiny):
            pltpu.make_async_remote_copy(
                src_ref=small_ref, dst_ref=parts_ref.at[4 * px + 2 * py + pc], send_sem=s_send.at[k],
                recv_sem=s_recv.at[k], device_id=(px, py, pc), device_id_type=MESH).wait_recv()
        for cp in big + tiny:
            cp.wait_send()
        mine.wait()

    return pl.pallas_call(
        body, name="exchange_core", in_specs=[_ANY, _ANY], out_specs=[_ANY, _ANY],
        out_shape=[jax.ShapeDtypeStruct((4, rows, lanes), F32), jax.ShapeDtypeStruct((N_DEV,) + small.shape, F32)],
        scratch_shapes=[pltpu.SemaphoreType.DMA((4,)), pltpu.SemaphoreType.DMA((4,)), pltpu.SemaphoreType.DMA((N_DEV,)),
                        pltpu.SemaphoreType.DMA((N_DEV,)), pltpu.SemaphoreType.DMA],
    )(pack, small)


def _add_core_parts(pack, got, core):
    _, _, rows, lanes = pack.shape
    tr = _pick_rows(rows)

    def body(core_ref, a_ref, b_ref, o_ref):
        o_ref[...] = (a_ref[...] + b_ref[...]).astype(BF16)

    return pl.pallas_call(
        body, name="add_core_parts",
        grid_spec=pltpu.PrefetchScalarGridSpec(
            num_scalar_prefetch=1, grid=(4, rows // tr),
            in_specs=[pl.BlockSpec((None, None, tr, lanes), lambda j, i, core_ref: (core_ref[0], j, i, 0)),
                      pl.BlockSpec((None, tr, lanes), lambda j, i, core_ref: (j, i, 0))],
            out_specs=pl.BlockSpec((None, tr, lanes), lambda j, i, core_ref: (j, i, 0))),
        out_shape=jax.ShapeDtypeStruct((4, rows, lanes), BF16),
        compiler_params=_params(("parallel", "parallel")),
    )(core, pack, got)


def _exchange_chips(chip_sums):
    _, rows, lanes = chip_sums.shape

    def body(src_ref, got_ref, send_sems, recv_sems):
        x, y, c = lax.axis_index("x"), lax.axis_index("y"), lax.axis_index("c")
        flips = [(1, 0), (0, 1), (1, 1)]
        copies = []
        for k, (fx, fy) in enumerate(flips):
            px, py = x ^ fx, y ^ fy
            copies.append(pltpu.make_async_remote_copy(
                src_ref=src_ref.at[2 * px + py], dst_ref=got_ref.at[k], send_sem=send_sems.at[k],
                recv_sem=recv_sems.at[k], device_id=(px, py, c), device_id_type=MESH))
        for cp in copies:
            cp.start()
        for cp in copies:
            cp.wait_recv()
        for cp in copies:
            cp.wait_send()

    return pl.pallas_call(
        body, name="exchange_chips", in_specs=[_ANY], out_specs=_ANY,
        out_shape=jax.ShapeDtypeStruct((3, rows, lanes), chip_sums.dtype),
        scratch_shapes=[pltpu.SemaphoreType.DMA((3,)), pltpu.SemaphoreType.DMA((3,))],
    )(chip_sums)


def _sum_chip_parts(chip_sums, got, chip):
    _, rows, lanes = chip_sums.shape
    tr = _pick_rows(rows)

    def body(chip_ref, own_ref, got_ref, o_ref):
        f32 = lambda t: t.astype(F32)
        o_ref[...] = ((f32(own_ref[...]) + f32(got_ref[0])) + f32(got_ref[1])) + f32(got_ref[2])

    return pl.pallas_call(
        body, name="sum_chip_parts",
        grid_spec=pltpu.PrefetchScalarGridSpec(
            num_scalar_prefetch=1, grid=(rows // tr,),
            in_specs=[pl.BlockSpec((None, tr, lanes), lambda i, chip_ref: (chip_ref[0], i, 0)),
                      pl.BlockSpec((3, tr, lanes), lambda i, chip_ref: (0, i, 0))],
            out_specs=pl.BlockSpec((tr, lanes), lambda i, chip_ref: (i, 0))),
        out_shape=jax.ShapeDtypeStruct((rows, lanes), F32),
        compiler_params=_params(("parallel",)),
    )(chip, chip_sums, got)


def _adamw_math(w, g, m, v):
    m = ADAM_B1 * m + (1.0 - ADAM_B1) * g
    v = ADAM_B2 * v + (1.0 - ADAM_B2) * (g * g)
    m_hat = m / (1.0 - ADAM_B1 ** ADAM_STEP)
    v_hat = v / (1.0 - ADAM_B2 ** ADAM_STEP)
    return -ADAM_LR * (m_hat / (jnp.sqrt(v_hat) + ADAM_EPS) + ADAM_WD * w), m, v


def _adamw(w, g, m, v, name):
    rows, cols = w.shape
    tr = 256 if rows % 256 == 0 and rows * cols > 2 ** 19 else rows

    def body(w_ref, g_ref, m_ref, v_ref, d_o, m_o, v_o):
        d_o[...], m_o[...], v_o[...] = _adamw_math(w_ref[...], g_ref[...], m_ref[...], v_ref[...])

    spec = pl.BlockSpec((tr, cols), lambda i: (i, 0))
    return pl.pallas_call(
        body, name=name, grid=(rows // tr,), in_specs=[spec] * 4, out_specs=[spec] * 3,
        out_shape=[jax.ShapeDtypeStruct((rows, cols), F32)] * 3, compiler_params=_params(("parallel",)),
    )(w, g, m, v)


def _adamw_small(parts, ws, ms, vs):
    k = len(_SMALL)

    def body(p_ref, *refs):
        w_refs, m_refs, v_refs = refs[:k], refs[k:2 * k], refs[2 * k:3 * k]
        outs = refs[3 * k:]
        g = p_ref[0]
        for d in range(1, N_DEV):
            g = g + p_ref[d]
        o = 0
        for i, (_, n) in enumerate(_SMALL):
            gp = g[:, o:o + n]
            outs[1 + i][...] = gp
            outs[1 + k + i][...], outs[1 + 2 * k + i][...], outs[1 + 3 * k + i][...] = _adamw_math(
                w_refs[i][...], gp, m_refs[i][...], v_refs[i][...])
            o += n
        outs[0][...] = g[:, o:o + _LANES]

    shapes = [jax.ShapeDtypeStruct((1, n), F32) for _, n in _SMALL]
    out = pl.pallas_call(
        body, name="adamw_small", out_shape=[jax.ShapeDtypeStruct((1, _LANES), F32)] + shapes * 4,
        compiler_params=_params(),
    )(parts, *ws, *ms, *vs)
    return out[0][0, 0], out[1:1 + k], out[1 + k:1 + 2 * k], out[1 + 2 * k:1 + 3 * k], out[1 + 3 * k:]


_WEIGHT_NAMES = ['norm_mix_pre', 'w_in', 'b_gate', 'mu_rw', 'w0', 'w_up', 'a0', 'a_up', 'g_up', 'k_k', 'k_a', 'r_k',
                 'lnx_w', 'lnx_b', 'w_sb_out', 'w_rw_out', 'w_o', 'norm_mix_post', 'norm_ffn_pre', 'w_ffn_gate',
                 'w_ffn_up', 'w_ffn_down', 'norm_ffn_post']


def _step(x, target, w, m, v):
    sharded = [n for n, _, _ in _SHARDED]
    sm = {n: w[n].reshape(1, -1) for n, _ in _SMALL}
    own = {n: w[n][0] for n in sharded}
    packed = _pack_shards(own, BF16)
    wt = _unpack_gathered(_all_gather(packed[:_EARLY_ROWS]), slice(0, _EARLY))
    loss_part, grad_x, gw, gs, (late_pack, late_got) = _local_step(x, target, sm, wt, packed[_EARLY_ROWS:])

    cx, cy, cc = lax.axis_index("x"), lax.axis_index("y"), lax.axis_index("c")
    core = jnp.reshape(cc, (1,)).astype(jnp.int32)
    chip = jnp.reshape(2 * cx + cy, (1,)).astype(jnp.int32)
    me = jnp.reshape(4 * cx + 2 * cy + cc, (1,)).astype(jnp.int32)
    pack = _pack_full_grads(gw, slice(0, _EARLY), by_core=True)
    got_core, small_parts = _exchange_core(pack, _pack_small(gs, loss_part[0, 0]))
    chip_sums = _add_core_parts(pack, got_core, core)
    early = _sum_chip_parts(chip_sums, _exchange_chips(chip_sums), chip)
    g_sh = {**_unpack_shards(early, slice(0, _EARLY)),
            **_unpack_shards(_sum_direct(late_pack, late_got, me), slice(_EARLY, None))}

    row = lambda t: [t[n].reshape(1, -1) for n, _ in _SMALL]
    loss, *by_kind = _adamw_small(small_parts, row(w), row(m), row(v))
    g_s, d_s, m_s, v_s = ({n: t[i] for i, (n, _) in enumerate(_SMALL)} for t in by_kind)

    grads, deltas, new_m, new_v = {}, {}, {}, {}
    for n in _WEIGHT_NAMES:
        if n in g_sh:
            d_, m_, v_ = _adamw(own[n], g_sh[n], m[n][0], v[n][0], "adamw_" + n)
            grads[n], deltas[n], new_m[n], new_v[n] = (t.reshape(w[n].shape) for t in (g_sh[n], d_, m_, v_))
        else:
            grads[n], deltas[n], new_m[n], new_v[n] = (t[n].reshape(w[n].shape) for t in (g_s, d_s, m_s, v_s))
    return (loss, grad_x, *[grads[n] for n in _WEIGHT_NAMES], *[deltas[n] for n in _WEIGHT_NAMES],
            *[new_m[n] for n in _WEIGHT_NAMES], *[new_v[n] for n in _WEIGHT_NAMES])


def kernel(x, norm_mix_pre, w_in, b_gate, mu_rw, w0, w_up, a0, a_up, g_up, k_k, k_a, r_k, lnx_w, lnx_b, w_sb_out, w_rw_out, w_o, norm_mix_post, norm_ffn_pre, w_ffn_gate, w_ffn_up, w_ffn_down, norm_ffn_post, loss_target, m_norm_mix_pre, m_w_in, m_b_gate, m_mu_rw, m_w0, m_w_up, m_a0, m_a_up, m_g_up, m_k_k, m_k_a, m_r_k, m_lnx_w, m_lnx_b, m_w_sb_out, m_w_rw_out, m_w_o, m_norm_mix_post, m_norm_ffn_pre, m_w_ffn_gate, m_w_ffn_up, m_w_ffn_down, m_norm_ffn_post, v_norm_mix_pre, v_w_in, v_b_gate, v_mu_rw, v_w0, v_w_up, v_a0, v_a_up, v_g_up, v_k_k, v_k_a, v_r_k, v_lnx_w, v_lnx_b, v_w_sb_out, v_w_rw_out, v_w_o, v_norm_mix_post, v_norm_ffn_pre, v_w_ffn_gate, v_w_ffn_up, v_w_ffn_down, v_norm_ffn_post):
    args = locals()
    w = {n: args[n] for n in _WEIGHT_NAMES}
    m = {n: args["m_" + n] for n in _WEIGHT_NAMES}
    v = {n: args["v_" + n] for n in _WEIGHT_NAMES}
    return _step(x, loss_target, w, m, v)
```

```python
import functools

import jax
import jax.numpy as jnp
from jax import lax
from jax.experimental import pallas as pl
from jax.experimental.pallas import tpu as pltpu

F32 = jnp.float32
BF16 = jnp.bfloat16

D_MODEL = 1024
HEADS = 8
HEAD_DIM = 64
WIDTH = HEADS * HEAD_DIM
W_LORA, A_LORA, G_LORA = 64, 64, 128
SB_COLS = 3 * WIDTH
RW_COLS = 3 * WIDTH + W_LORA + A_LORA + G_LORA
GATE_COLS = 2 * D_MODEL
D_FF = 2816
RMS_EPS = 1e-6
GN_EPS = HEAD_DIM * 1e-5
N_DEV = 8

ADAM_LR, ADAM_B1, ADAM_B2, ADAM_EPS, ADAM_WD, ADAM_STEP = 0.001, 0.9, 0.999, 1e-08, 0.01, 10

ROW_TILE = 512
ROW_TILE_FFN = 256
SCAN_CHUNK = 64
ATT_ALIGN = 128
ATT_WINDOW = 384
ATT_Q = 128
ATT_PAIRS = 2
SB_DEAD = -104.0
SCAN_SEQS_FWD = 4
SCAN_SEQS_BWD = 2
SCAN_PASSES = 1
VMEM_LIMIT = 56 * 2 ** 20

MESH = pl.DeviceIdType.MESH


def _params(sem=None, vmem=VMEM_LIMIT):
    kw = dict(vmem_limit_bytes=vmem)
    if sem is not None:
        kw["dimension_semantics"] = sem
    return pltpu.CompilerParams(**kw)


def _const_spec(shape):
    nd = len(shape)
    return pl.BlockSpec(shape, lambda *_: (0,) * nd, pipeline_mode=pl.Buffered(1))


def _row_spec(tm, n):
    return pl.BlockSpec((tm, n), lambda i: (i, 0))


def _mm(a, b):
    return lax.dot_general(a, b, (((1,), (0,)), ((), ())), preferred_element_type=F32)


def _mm_nt(a, b):
    return lax.dot_general(a, b, (((1,), (1,)), ((), ())), preferred_element_type=F32)


def _mm_tn(a, b):
    return lax.dot_general(a, b, (((0,), (0,)), ((), ())), preferred_element_type=F32)


def _softplus(z):
    return jnp.maximum(z, 0.0) + jnp.log1p(jnp.exp(-jnp.abs(z)))


def _rms_fwd(x, gain):
    rstd = lax.rsqrt(jnp.mean(x * x, axis=-1, keepdims=True) + RMS_EPS)
    xn = x * rstd
    return xn * gain, xn, rstd


def _rms_bwd(dy, xn, rstd, gain):
    u = dy * gain
    dx = rstd * (u - xn * jnp.mean(u * xn, axis=-1, keepdims=True))
    return dx, jnp.sum(dy * xn, axis=0, keepdims=True)


def _acc_out(ref, val, first):
    @pl.when(first)
    def _():
        ref[...] = val

    @pl.when(jnp.logical_not(first))
    def _():
        ref[...] += val


_IN_COLS = SB_COLS + RW_COLS + GATE_COLS
_QKV_OF, _RW_OF, _GATE_OF = slice(0, SB_COLS), slice(SB_COLS, SB_COLS + RW_COLS), slice(SB_COLS + RW_COLS, _IN_COLS)


def _in_proj_fwd(x2, g_pre, w_in, b_gate, tm):
    T = x2.shape[0]

    def body(x_ref, g_ref, w_ref, b_ref, h_ref, qkv_ref, prw_ref, gate_ref):
        h = _rms_fwd(x_ref[...], g_ref[...])[0].astype(BF16)
        h_ref[...] = h
        qkv_ref[...] = _mm(h, w_ref[:, _QKV_OF]).astype(BF16)
        prw_ref[...] = _mm(h, w_ref[:, _RW_OF])
        gate_ref[...] = jax.nn.sigmoid(_mm(h, w_ref[:, _GATE_OF]) + b_ref[...])

    return pl.pallas_call(
        body, name="in_proj_fwd", grid=(T // tm,),
        in_specs=[_row_spec(tm, D_MODEL), _const_spec((1, D_MODEL)), _const_spec((D_MODEL, _IN_COLS)),
                  _const_spec((1, GATE_COLS))],
        out_specs=[_row_spec(tm, D_MODEL), _row_spec(tm, SB_COLS), _row_spec(tm, RW_COLS), _row_spec(tm, GATE_COLS)],
        out_shape=[jax.ShapeDtypeStruct((T, D_MODEL), BF16), jax.ShapeDtypeStruct((T, SB_COLS), BF16),
                   jax.ShapeDtypeStruct((T, RW_COLS), F32), jax.ShapeDtypeStruct((T, GATE_COLS), F32)],
        compiler_params=_params(("parallel",)),
    )(x2, g_pre, w_in, b_gate)


def _in_proj_bwd(x2, g_pre, dx1, dqkv, dprw, dgate, w_in, tm, ride=None):
    T = x2.shape[0]
    steps = T // tm

    def body(x_ref, g_ref, dx1_ref, dq_ref, dr_ref, dg_ref, w_ref, *rest):
        if ride is None:
            gx_ref, dgain_ref = rest
        else:
            ride_ref, gx_ref, dgain_ref, got_ref, send_sems, recv_sems = rest
            start, finish = _scatter_steps(ride_ref, got_ref, send_sems, recv_sems)
            pl.when(pl.program_id(0) == 0)(start)
        dh = (_mm_nt(dq_ref[...], w_ref[:, _QKV_OF]) + _mm_nt(dr_ref[...], w_ref[:, _RW_OF])
              + _mm_nt(dg_ref[...], w_ref[:, _GATE_OF]))
        gain = g_ref[...]
        _, xn, rstd = _rms_fwd(x_ref[...], gain)
        dx, dgain = _rms_bwd(dh, xn, rstd, gain)
        gx_ref[...] = dx1_ref[...] + dx
        _acc_out(dgain_ref, dgain, pl.program_id(0) == 0)
        if ride is not None:
            pl.when(pl.program_id(0) == steps - 1)(finish)

    in_specs = [_row_spec(tm, D_MODEL), _const_spec((1, D_MODEL)), _row_spec(tm, D_MODEL), _row_spec(tm, SB_COLS),
                _row_spec(tm, RW_COLS), _row_spec(tm, GATE_COLS), _const_spec((D_MODEL, _IN_COLS))]
    out_specs = [_row_spec(tm, D_MODEL), pl.BlockSpec((1, D_MODEL), lambda i: (0, 0))]
    out_shape = [jax.ShapeDtypeStruct((T, D_MODEL), F32), jax.ShapeDtypeStruct((1, D_MODEL), F32)]
    args, scratch = (x2, g_pre, dx1, dqkv, dprw, dgate, w_in), []
    if ride is not None:
        in_specs, out_specs, args = in_specs + [_ANY], out_specs + [_ANY], args + (ride,)
        out_shape = out_shape + [jax.ShapeDtypeStruct((N_DEV - 1,) + ride.shape[1:], ride.dtype)]
        scratch = [pltpu.SemaphoreType.DMA((N_DEV - 1,)), pltpu.SemaphoreType.DMA((N_DEV - 1,))]
    return pl.pallas_call(
        body, name="in_proj_bwd", grid=(steps,), in_specs=in_specs, out_specs=out_specs, out_shape=out_shape,
        scratch_shapes=scratch, compiler_params=_params(("arbitrary",)),
    )(*args)


def _pick_tile(n, cap):
    best = None
    for t in range(128, min(n, cap) + 1, 128):
        if n % t == 0:
            best = t
    return n if best is None else best


def _grad_w(a, b, name, dtype=F32):
    T, K = a.shape
    N = b.shape[1]
    tk, tn, tt = _pick_tile(K, 1408), _pick_tile(N, 2048), min(T, 2048)
    steps = T // tt

    def body(a_ref, b_ref, o_ref, *acc):
        t = pl.program_id(2)
        part = _mm_tn(a_ref[...], b_ref[...])
        if not acc:
            _acc_out(o_ref, part, t == 0)
        else:
            _acc_out(acc[0], part, t == 0)

            @pl.when(t == steps - 1)
            def _():
                o_ref[...] = acc[0][...].astype(dtype)

    return pl.pallas_call(
        body, name=name, grid=(K // tk, N // tn, steps),
        in_specs=[pl.BlockSpec((tt, tk), lambda i, j, t: (t, i)), pl.BlockSpec((tt, tn), lambda i, j, t: (t, j))],
        out_specs=pl.BlockSpec((tk, tn), lambda i, j, t: (i, j)),
        out_shape=jax.ShapeDtypeStruct((K, N), dtype),
        scratch_shapes=[] if dtype == F32 else [pltpu.VMEM((tk, tn), F32)],
        compiler_params=_params(("parallel", "parallel", "arbitrary")),
    )(a, b)


def _tri(n, kind):
    r = lax.broadcasted_iota(jnp.int32, (n, n), 0)
    c = lax.broadcasted_iota(jnp.int32, (n, n), 1)
    return {"gt": r > c, "le": r <= c, "lt": r < c, "ge": r >= c}[kind]


def _running_sums(x, carry, tri, kb, reverse=False):
    blocks = range(x.shape[1] // kb)
    parts = {}
    for b in (reversed(blocks) if reverse else blocks):
        piece = x[:, b * kb:(b + 1) * kb]
        parts[b] = carry + _mm(piece.astype(BF16), tri)
        carry = carry + jnp.sum(piece, axis=1, keepdims=True)
    return jnp.concatenate([parts[b] for b in blocks], axis=1), carry


def _sb_valid(row0, col0, first, last, qb, kb):
    ahead = lax.broadcasted_iota(jnp.int32, (qb, kb), 1) - lax.broadcasted_iota(jnp.int32, (qb, kb), 0)
    col = lax.broadcasted_iota(jnp.int32, (1, kb), 1)
    return jnp.logical_and(ahead < row0 - col0, jnp.logical_and(col >= first - col0, col < last - col0))


def _sb_softplus(z):
    return jnp.maximum(z, 0.0) + jnp.log(1.0 + jnp.exp(-jnp.abs(z)))


_PAIR = 2 * HEAD_DIM
_PAIRS = WIDTH // _PAIR


def _first_head_lanes():
    return lax.broadcasted_iota(jnp.int32, (1, _PAIR), 1) < HEAD_DIM


def _per_head(t, first_head):
    zero = jnp.zeros_like(t)
    return jnp.where(first_head, t, zero), jnp.where(first_head, zero, t)


def _sb_fwd(qkv, bl, seq, ride=None):
    qb, win, kb = min(ATT_Q, seq), min(ATT_WINDOW, seq), ATT_ALIGN
    nq = seq // qb
    nh, width, groups = 2 * ATT_PAIRS, ATT_PAIRS * _PAIR, _PAIRS // ATT_PAIRS
    pair_of = lambda h: slice((h // 2) * _PAIR, (h // 2 + 1) * _PAIR)

    steps = bl * groups
    pass_on_at = (5 * steps) // 8

    def body(q_ref, k_ref, v_ref, *rest):
        g = pl.program_id(0) * groups + pl.program_id(1)
        if ride is None:
            o_ref, l_ref, stop_ref = rest
        else:
            ride_ref, o_ref, l_ref, stop_ref, gathered_ref, *sems = rest
            start, forward, finish = _gather_steps(ride_ref, gathered_ref, *sems)
            pl.when(g == 0)(start)
            pl.when(g == pass_on_at)(forward)
        first_head = _first_head_lanes()
        u_after = _tri(kb, "gt").astype(BF16)

        def qblock(i, _):
            rows = pl.ds(pl.multiple_of(i * qb, qb), qb)
            qs = q_ref[rows, :] * (HEAD_DIM ** -0.5)
            qh = [_per_head(qs[:, pair_of(h)], first_head)[h % 2] for h in range(nh)]

            def live(carry):
                return jnp.logical_and(carry[0] > 0, carry[3] > 0)

            def window(carry):
                hi, accs, cs, _ = carry
                lo = pl.multiple_of(jnp.maximum(hi - win, 0), kb)
                cols = pl.ds(lo, win)
                kv, vv = k_ref[cols, :], v_ref[cols, :]
                valid = _sb_valid(i * qb, lo, lo, hi, qb, win)
                accs, cs = list(accs), list(cs)
                for h in range(nh):
                    z = _mm_nt(qh[h], kv[:, pair_of(h)])
                    sp = _sb_softplus(z)
                    spm = jnp.where(valid, sp, 0.0)
                    after, cs[h] = _running_sums(spm, cs[h], u_after, kb, reverse=True)
                    w = jnp.where(valid, jnp.exp(z - sp - after), 0.0)
                    accs[h] = accs[h] + _mm(w.astype(BF16), vv[:, pair_of(h)])
                alive = functools.reduce(jnp.minimum, [jnp.min(c) for c in cs]) < -SB_DEAD
                return lo, tuple(accs), tuple(cs), alive.astype(jnp.int32)

            zero_acc, zero_c = jnp.zeros((qb, _PAIR), F32), jnp.zeros((qb, 1), F32)
            lo, accs, cs, _ = lax.while_loop(
                live, window, ((i + 1) * qb, (zero_acc,) * nh, (zero_c,) * nh, jnp.int32(1)))
            for pp in range(ATT_PAIRS):
                o_ref[rows, pp * _PAIR:(pp + 1) * _PAIR] = jnp.where(
                    first_head, accs[2 * pp], accs[2 * pp + 1]).astype(BF16)
            for h in range(nh):
                l_ref[h, rows, :] = cs[h]
            stop_ref[g, i] = lo
            return 0

        lax.fori_loop(0, nq, qblock, 0)
        if ride is not None:
            pl.when(g == steps - 1)(finish)

    col = lambda off: pl.BlockSpec((seq, width), lambda b, p: (b, off + p))
    in_specs = [col(0), col(groups), col(2 * groups)]
    out_specs = [col(0), pl.BlockSpec((None, nh, seq, 1), lambda b, p: (b, p, 0, 0)), pl.BlockSpec(memory_space=pltpu.SMEM)]
    out_shape = [jax.ShapeDtypeStruct((bl * seq, WIDTH), BF16), jax.ShapeDtypeStruct((bl, HEADS, seq, 1), F32),
                 jax.ShapeDtypeStruct((bl * groups, nq), jnp.int32)]
    if ride is None:
        return pl.pallas_call(body, name="sb_fwd", grid=(bl, groups), in_specs=in_specs, out_specs=out_specs,
                              out_shape=out_shape, compiler_params=_params(("arbitrary", "arbitrary")))(qkv, qkv, qkv)
    return pl.pallas_call(
        body, name="sb_fwd", grid=(bl, groups), in_specs=in_specs + [_ANY], out_specs=out_specs + [_ANY],
        out_shape=out_shape + [jax.ShapeDtypeStruct((N_DEV,) + ride.shape, ride.dtype)], scratch_shapes=_GATHER_SEMS,
        compiler_params=_params(("arbitrary", "arbitrary")),
    )(qkv, qkv, qkv, ride)


def _sb_bwd(qkv, do, lsum, stop, bl, seq):
    qb, win, kb = min(ATT_Q, seq), min(ATT_WINDOW, seq), ATT_ALIGN
    nq = seq // qb
    nh, width, groups = 2 * ATT_PAIRS, ATT_PAIRS * _PAIR, _PAIRS // ATT_PAIRS
    pair_of = lambda h: slice((h // 2) * _PAIR, (h // 2 + 1) * _PAIR)

    def body(stop_ref, q_ref, k_ref, v_ref, do_ref, l_ref, dq_ref, dk_ref, dv_ref, dk_acc, dv_acc):
        g = pl.program_id(0) * groups + pl.program_id(1)
        first_head = _first_head_lanes()
        u_incl = _tri(kb, "le").astype(BF16)
        u_excl = _tri(kb, "lt").astype(BF16)
        dk_acc[...] = jnp.zeros_like(dk_acc)
        dv_acc[...] = jnp.zeros_like(dv_acc)

        def qblock(i, _):
            rows = pl.ds(pl.multiple_of(i * qb, qb), qb)
            qv = q_ref[rows, :]
            qs = qv * (HEAD_DIM ** -0.5)
            dob = do_ref[rows, :]
            qh = [_per_head(qs[:, pair_of(h)], first_head)[h % 2] for h in range(nh)]
            doh = [_per_head(dob[:, pair_of(h)], first_head)[h % 2] for h in range(nh)]
            ltot = [l_ref[h, rows, :] for h in range(nh)]

            first = (jnp.clip(stop_ref[g, i], 0, i * qb) // ATT_ALIGN) * ATT_ALIGN

            def window(n, carry):
                dqs, ps, es = (list(t) for t in carry)
                start = first + n * win
                lo = pl.multiple_of(jnp.minimum(start, seq - win), kb)
                cols = pl.ds(lo, win)
                kv, vv = k_ref[cols, :], v_ref[cols, :]
                valid = _sb_valid(i * qb, lo, start, seq, qb, win)
                dks, dvs = [], []
                for h in range(nh):
                    kp, vp = kv[:, pair_of(h)], vv[:, pair_of(h)]
                    z = _mm_nt(qh[h], kp)
                    sp = _sb_softplus(z)
                    spm = jnp.where(valid, sp, 0.0)
                    upto, ps[h] = _running_sums(spm, ps[h], u_incl, kb)
                    w = jnp.where(valid, jnp.exp(z - sp - (ltot[h] - upto)), 0.0)
                    e = _mm_nt(doh[h], vp) * w
                    dlf, es[h] = _running_sums(e, es[h], u_excl, kb)
                    sig = jnp.exp(z - sp)
                    dz = jnp.where(valid, e * (1.0 - sig) - dlf * sig, 0.0) * (HEAD_DIM ** -0.5)
                    dzb = dz.astype(BF16)
                    dvs.append(_mm_tn(w.astype(BF16), dob[:, pair_of(h)]))
                    dks.append(_mm_tn(dzb, qv[:, pair_of(h)]))
                    dqs[h] = dqs[h] + _mm(dzb, kp)
                for pp in range(ATT_PAIRS):
                    lanes = slice(pp * _PAIR, (pp + 1) * _PAIR)
                    dv_acc[cols, lanes] += jnp.where(first_head, dvs[2 * pp], dvs[2 * pp + 1])
                    dk_acc[cols, lanes] += jnp.where(first_head, dks[2 * pp], dks[2 * pp + 1])
                return tuple(dqs), tuple(ps), tuple(es)

            zero_q, zero_c = jnp.zeros((qb, _PAIR), F32), jnp.zeros((qb, 1), F32)
            dqs, _, _ = lax.fori_loop(0, ((i + 1) * qb - first + win - 1) // win, window,
                                      ((zero_q,) * nh, (zero_c,) * nh, (zero_c,) * nh))
            for pp in range(ATT_PAIRS):
                dq_ref[rows, pp * _PAIR:(pp + 1) * _PAIR] = jnp.where(
                    first_head, dqs[2 * pp], dqs[2 * pp + 1]).astype(BF16)
            return 0

        lax.fori_loop(0, nq, qblock, 0)
        dk_ref[...] = dk_acc[...].astype(BF16)
        dv_ref[...] = dv_acc[...].astype(BF16)

    col = lambda off: pl.BlockSpec((seq, width), lambda b, p, stop_ref: (b, off + p))
    return pl.pallas_call(
        body, name="sb_bwd",
        grid_spec=pltpu.PrefetchScalarGridSpec(
            num_scalar_prefetch=1, grid=(bl, groups),
            in_specs=[col(0), col(groups), col(2 * groups), col(0),
                      pl.BlockSpec((None, nh, seq, 1), lambda b, p, stop_ref: (b, p, 0, 0))],
            out_specs=[col(0), col(0), col(0)],
            scratch_shapes=[pltpu.VMEM((seq, width), F32), pltpu.VMEM((seq, width), F32)]),
        out_shape=[jax.ShapeDtypeStruct((bl * seq, WIDTH), BF16)] * 3,
        compiler_params=_params(("parallel", "parallel")),
    )(stop, qkv, qkv, qkv, do, lsum)


@jax.custom_vjp
def _lora_mm(x, w):
    return _mm(x.astype(BF16), w.astype(BF16))


_lora_mm.defvjp(
    lambda x, w: (_mm(x.astype(BF16), w.astype(BF16)), (x, w)),
    lambda res, ct: (_mm_nt(ct.astype(BF16), res[1].astype(BF16)), _mm_tn(res[0].astype(BF16), ct.astype(BF16))))


def _rw_prep_math(p, ps, mu, w0, w_up, a0, a_up, g_up, k_k, k_a):
    pm = p + (ps - p) * mu
    r, k, v = pm[:, :WIDTH], pm[:, WIDTH:2 * WIDTH], pm[:, 2 * WIDTH:3 * WIDTH]
    o = 3 * WIDTH
    xw, xa, xg = pm[:, o:o + W_LORA], pm[:, o + W_LORA:o + W_LORA + A_LORA], pm[:, o + W_LORA + A_LORA:]
    w_raw = w0 + _lora_mm(jnp.tanh(xw), w_up)
    lw = -jnp.exp(-_softplus(-w_raw) - 0.5)
    a = jax.nn.sigmoid(a0 + _lora_mm(xa, a_up))
    g = _lora_mm(jax.nn.sigmoid(xg), g_up)
    kk = k * k_k
    k2 = k * (1.0 + (a - 1.0) * k_a)
    return r, lw, k2, v, kk, a, g


def _shift_down(p, first_row):
    row = lax.broadcasted_iota(jnp.int32, p.shape, 0)
    return jnp.where(row == 0, first_row, pltpu.roll(p, 1, 0))


def _shift_up(p, last_row):
    row = lax.broadcasted_iota(jnp.int32, p.shape, 0)
    return jnp.where(row == p.shape[0] - 1, last_row, pltpu.roll(p, p.shape[0] - 1, 0))


_PREP_PARAM_SHAPES = [(1, RW_COLS), (1, WIDTH), (W_LORA, WIDTH), (1, WIDTH), (A_LORA, WIDTH), (G_LORA, WIDTH),
                      (1, WIDTH), (1, WIDTH)]


def _prev_rows_spec(tm):
    return pl.BlockSpec((8, RW_COLS), lambda i: (jnp.maximum(i * (tm // 8) - 1, 0), 0))


def _head_spec(tm, seq, tile_of=lambda i: i):
    per_seq = seq // tm
    return pl.BlockSpec((None, HEADS, tm, HEAD_DIM),
                        lambda i: (tile_of(i) // per_seq, 0, tile_of(i) % per_seq, 0))


def _split_heads(val, ref):
    for h in range(HEADS):
        ref[h] = val[:, h * HEAD_DIM:(h + 1) * HEAD_DIM]


def _join_heads(ref):
    return jnp.concatenate([ref[h] for h in range(HEADS)], axis=1)


def _rw_prep_fwd(prw, params, seq, tm):
    T = prw.shape[0]

    def body(p_ref, prev_ref, *rest):
        prm = [r_[...] for r_ in rest[:8]]
        outs = rest[8:]
        i = pl.program_id(0)
        first = jnp.where((i * tm) % seq == 0, 0.0, prev_ref[7:8, :])
        p = p_ref[...]
        vals = _rw_prep_math(p, _shift_down(p, first), *prm)
        for o_ref, val in zip(outs[:6], vals[:6]):
            _split_heads(val, o_ref)
        outs[6][...] = vals[6]

    by_head = jax.ShapeDtypeStruct((T // seq, HEADS, seq, HEAD_DIM), F32)
    return pl.pallas_call(
        body, name="rw_prep_fwd", grid=(T // tm,),
        in_specs=[_row_spec(tm, RW_COLS), _prev_rows_spec(tm)] + [_const_spec(s) for s in _PREP_PARAM_SHAPES],
        out_specs=[_head_spec(tm, seq)] * 6 + [_row_spec(tm, WIDTH)],
        out_shape=[by_head] * 6 + [jax.ShapeDtypeStruct((T, WIDTH), F32)],
        compiler_params=_params(("parallel",)),
    )(prw, prw, *params)


def _rw_prep_bwd(prw, params, cts, seq, tm):
    T = prw.shape[0]
    n = T // tm

    def body(p_ref, prev_ref, *rest):
        prm = [r_[...] for r_ in rest[:8]]
        ct = tuple(_join_heads(r_) for r_ in rest[8:14]) + (rest[14][...],)
        dp_ref = rest[15]
        dprm_refs = rest[16:24]
        carry = rest[24]
        step = pl.program_id(0)
        i = n - 1 - step
        first = jnp.where((i * tm) % seq == 0, 0.0, prev_ref[7:8, :])
        p = p_ref[...]
        _, vjp = jax.vjp(_rw_prep_math, p, _shift_down(p, first), *prm)
        grads = vjp(ct)
        dp, dps = grads[0], grads[1]
        nxt = jnp.where(jnp.logical_or(step == 0, ((i + 1) * tm) % seq == 0), 0.0, carry[0:1, :])
        dp_ref[...] = (dp + _shift_up(dps, nxt)).astype(BF16)
        carry[...] = dps[0:8, :]
        for ref, gval in zip(dprm_refs, grads[2:]):
            _acc_out(ref, gval, step == 0)

    rev = lambda w: pl.BlockSpec((tm, w), lambda s: (n - 1 - s, 0))
    prev = pl.BlockSpec((8, RW_COLS), lambda s: (jnp.maximum((n - 1 - s) * (tm // 8) - 1, 0), 0))
    return pl.pallas_call(
        body, name="rw_prep_bwd", grid=(n,),
        in_specs=([rev(RW_COLS), prev] + [_const_spec(s) for s in _PREP_PARAM_SHAPES]
                  + [_head_spec(tm, seq, lambda s: n - 1 - s)] * 6 + [rev(WIDTH)]),
        out_specs=[rev(RW_COLS)] + [pl.BlockSpec(s, lambda s_: (0, 0)) for s in _PREP_PARAM_SHAPES],
        out_shape=[jax.ShapeDtypeStruct((T, RW_COLS), BF16)] + [jax.ShapeDtypeStruct(s, F32) for s in _PREP_PARAM_SHAPES],
        scratch_shapes=[pltpu.VMEM((8, RW_COLS), F32)],
        compiler_params=_params(("arbitrary",)),
    )(prw, prw, *params, *cts)


def _make_bmm(passes):
    def raw(dn, a, b):
        d = lambda x, y: lax.dot_general(x, y, dn, preferred_element_type=F32)
        ah = a.astype(BF16)
        bh = b.astype(BF16)
        if passes == 1:
            return d(ah, bh)
        al = (a - ah.astype(F32)).astype(BF16)
        bl = (b - bh.astype(F32)).astype(BF16)
        return d(ah, bh) + (d(ah, bl) + d(al, bh))

    dn_nn = (((2,), (1,)), ((0,), (0,)))
    dn_nt = (((2,), (2,)), ((0,), (0,)))
    dn_tn = (((1,), (1,)), ((0,), (0,)))

    @jax.custom_vjp
    def nn(a, b):
        return raw(dn_nn, a, b)

    @jax.custom_vjp
    def nt(a, b):
        return raw(dn_nt, a, b)

    @jax.custom_vjp
    def tn(a, b):
        return raw(dn_tn, a, b)

    nn.defvjp(lambda a, b: (raw(dn_nn, a, b), (a, b)), lambda res, ct: (nt(ct, res[1]), tn(res[0], ct)))
    nt.defvjp(lambda a, b: (raw(dn_nt, a, b), (a, b)), lambda res, ct: (nn(ct, res[1]), tn(ct, res[0])))
    tn.defvjp(lambda a, b: (raw(dn_tn, a, b), (a, b)), lambda res, ct: (nt(res[1], ct), nn(res[0], ct)))

    def unit_lower_inverse(m):
        n = m.shape[-1]
        row = lax.broadcasted_iota(jnp.int32, (n, n), 0)
        col = lax.broadcasted_iota(jnp.int32, (n, n), 1)
        m16 = ((row // 16) == (col // 16)).astype(F32)
        m32 = ((row // 32) == (col // 32)).astype(F32)
        a1 = m * m16
        a2 = nn(a1, a1)
        a4 = nn(a2, a2)
        a8 = nn(a4, a4)
        inv = (row == col).astype(F32) - a1
        inv = inv + nn(inv, a2)
        inv = inv + nn(inv, a4)
        inv = inv + nn(inv, a8)
        inv = inv - nn(nn(inv, m * (m32 - m16)), inv)
        return inv - nn(nn(inv, m * (1.0 - m32)), inv)

    @jax.custom_vjp
    def inverse(m):
        return unit_lower_inverse(m)

    def inverse_fwd(m):
        inv = unit_lower_inverse(m)
        return inv, inv

    inverse.defvjp(inverse_fwd, lambda inv, ct: (-nt(tn(inv, ct), inv),))
    return nn, nt, tn, inverse


def _wkv_chunk(s0, r, lw, k, v, kk, a, lnw, lnb, rk):
    nn, nt, tn, inverse = _make_bmm(SCAN_PASSES)
    G, L, N = r.shape
    rep = lambda t: jnp.broadcast_to(t[None], (G // HEADS, HEADS, 1, N)).reshape(G, 1, N)
    kap = kk * lax.rsqrt(jnp.maximum(jnp.sum(kk * kk, axis=-1, keepdims=True), 1e-24))
    b = a * kap
    row = lax.broadcasted_iota(jnp.int32, (L, L), 0)
    col = lax.broadcasted_iota(jnp.int32, (L, L), 1)
    low_incl = (col <= row).astype(F32)
    low_strict = (col < row).astype(F32)
    c = _make_bmm(3)[0](jnp.broadcast_to(low_incl[None], (G, L, L)), lw)
    c_all = jnp.sum(lw, axis=1, keepdims=True)
    g_inv = jnp.exp(-c)
    kap_t = kap * jnp.exp(c - lw)
    b_t = b * g_inv
    k_t = k * g_inv
    r_t = r * jnp.exp(c)
    g_all = jnp.exp(c_all)
    m_b = nt(kap_t, b_t) * low_strict
    m_k = nt(kap_t, k_t) * low_strict
    n_b = nt(r_t, b_t) * low_incl
    n_k = nt(r_t, k_t) * low_incl
    rhs = -(nt(kap_t, s0) + nn(m_k, v))
    sa = nn(inverse(m_b), rhs)
    y = nt(r_t, s0) + nn(n_b, sa) + nn(n_k, v)
    s1 = s0 * g_all + tn(sa, b_t * g_all) + tn(v, k_t * g_all)
    mean = jnp.mean(y, axis=-1, keepdims=True)
    yc = y - mean
    var = jnp.mean(yc * yc, axis=-1, keepdims=True)
    out = yc * lax.rsqrt(var + GN_EPS) * rep(lnw) + rep(lnb)
    out = out + jnp.sum(r * k * rep(rk), axis=-1, keepdims=True) * v
    return out, s1


def _scan_heads_per_step(total_heads, seqs_wanted):
    n_seq = total_heads // HEADS
    return HEADS * max(d for d in range(1, seqs_wanted + 1) if n_seq % d == 0)


def _wkv_fwd(seqs, lnw, lnb, rk):
    G, S, N = seqs[0].shape
    L = SCAN_CHUNK
    nc = S // L

    def body(*refs):
        ins = [r_[...] for r_ in refs[:6]]
        prm = [r_[...] for r_ in refs[6:9]]
        out_ref, st_ref, state = refs[9], refs[10], refs[11]

        @pl.when(pl.program_id(1) == 0)
        def _():
            state[...] = jnp.zeros_like(state)

        s0 = state[...]
        st_ref[...] = s0
        out, s1 = _wkv_chunk(s0, *ins, *prm)
        out_ref[...] = out
        state[...] = s1

    gb = _scan_heads_per_step(G, SCAN_SEQS_FWD)
    blk = pl.BlockSpec((gb, L, N), lambda b, i: (b, i, 0))
    pspec = _const_spec((HEADS, 1, N))
    return pl.pallas_call(
        body, name="wkv_fwd", grid=(G // gb, nc), in_specs=[blk] * 6 + [pspec] * 3,
        out_specs=[blk, pl.BlockSpec((None, gb, N, N), lambda b, i: (i, b, 0, 0))],
        out_shape=[jax.ShapeDtypeStruct((G, S, N), F32), jax.ShapeDtypeStruct((nc, G, N, N), F32)],
        scratch_shapes=[pltpu.VMEM((gb, N, N), F32)],
        compiler_params=_params(("parallel", "arbitrary")),
    )(*seqs, lnw, lnb, rk)


def _wkv_bwd(seqs, states, dout, lnw, lnb, rk, ride=None):
    G, S, N = seqs[0].shape
    L = SCAN_CHUNK
    nc = S // L
    gb = _scan_heads_per_step(G, SCAN_SEQS_BWD)

    def body(*refs):
        ins = [r_[...] for r_ in refs[:6]]
        s0 = refs[6][...]
        ct_out = refs[7][...]
        prm = [r_[...] for r_ in refs[8:11]]
        refs = refs[11:]
        if ride is not None:
            start, finish = _scatter_steps(refs[0], refs[10], refs[12], refs[13])
            pl.when(jnp.logical_and(pl.program_id(0) == 0, pl.program_id(1) == 0))(start)
            refs = refs[1:]
        d_refs = refs[0:6]
        dprm_refs = refs[6:9]
        dstate = refs[10] if ride is not None else refs[9]
        step = pl.program_id(1)

        @pl.when(step == 0)
        def _():
            dstate[...] = jnp.zeros_like(dstate)

        _, vjp = jax.vjp(_wkv_chunk, s0, *ins, *prm)
        grads = vjp((ct_out, dstate[...]))
        dstate[...] = grads[0]
        for ref, gval in zip(d_refs, grads[1:7]):
            ref[...] = gval
        for ref, gval in zip(dprm_refs, grads[7:]):
            _acc_out(ref, gval, jnp.logical_and(step == 0, pl.program_id(0) == 0))
        if ride is not None:
            pl.when(jnp.logical_and(pl.program_id(0) == G // gb - 1, step == nc - 1))(finish)

    blk = pl.BlockSpec((gb, L, N), lambda b, s: (b, nc - 1 - s, 0))
    pspec = _const_spec((HEADS, 1, N))
    pout = pl.BlockSpec((HEADS, 1, N), lambda b, s: (0, 0, 0))
    in_specs = [blk] * 6 + [pl.BlockSpec((None, gb, N, N), lambda b, s: (nc - 1 - s, b, 0, 0)), blk] + [pspec] * 3
    out_specs = [blk] * 6 + [pout] * 3
    out_shape = [jax.ShapeDtypeStruct((G, S, N), F32)] * 6 + [jax.ShapeDtypeStruct((HEADS, 1, N), F32)] * 3
    scratch = [pltpu.VMEM((gb, N, N), F32)]
    args = (*seqs, states, dout, lnw, lnb, rk)
    if ride is not None:
        in_specs, out_specs, args = in_specs + [_ANY], out_specs + [_ANY], args + (ride,)
        out_shape = out_shape + [jax.ShapeDtypeStruct((N_DEV - 1,) + ride.shape[1:], ride.dtype)]
        scratch = scratch + [pltpu.SemaphoreType.DMA((N_DEV - 1,)), pltpu.SemaphoreType.DMA((N_DEV - 1,))]
    return pl.pallas_call(
        body, name="wkv_bwd", grid=(G // gb, nc), in_specs=in_specs, out_specs=out_specs, out_shape=out_shape,
        scratch_shapes=scratch, compiler_params=_params(("arbitrary", "arbitrary")),
    )(*args)


def _merge_math(o_sb, rw_out, g_rw, gates, w_sb, w_rw, w_o):
    o_rw = (rw_out * g_rw).astype(BF16)
    a = _mm(o_sb, w_sb)
    b = _mm(o_rw, w_rw)
    g1, g2 = gates[:, :D_MODEL], gates[:, D_MODEL:]
    merged = (g1 * a + g2 * b).astype(BF16)
    return o_rw, a, b, g1, g2, merged, _mm(merged, w_o)


def _merge_fwd(x2, o_sb, rw_out, g_rw, gates, w_sb, w_rw, w_o, g_post, seq, tm):
    T = x2.shape[0]

    def body(x_ref, osb_ref, rw_ref, g_ref, gate_ref, wsb_ref, wrw_ref, wo_ref, gp_ref, x1_ref):
        z = _merge_math(osb_ref[...], _join_heads(rw_ref), g_ref[...], gate_ref[...], wsb_ref[...], wrw_ref[...], wo_ref[...])[-1]
        x1_ref[...] = x_ref[...] + _rms_fwd(z, gp_ref[...])[0]

    return pl.pallas_call(
        body, name="merge_fwd", grid=(T // tm,),
        in_specs=[_row_spec(tm, D_MODEL), _row_spec(tm, WIDTH), _head_spec(tm, seq), _row_spec(tm, WIDTH),
                  _row_spec(tm, GATE_COLS), _const_spec((WIDTH, D_MODEL)), _const_spec((WIDTH, D_MODEL)),
                  _const_spec((D_MODEL, D_MODEL)), _const_spec((1, D_MODEL))],
        out_specs=_row_spec(tm, D_MODEL),
        out_shape=jax.ShapeDtypeStruct((T, D_MODEL), F32),
        compiler_params=_params(("parallel",)),
    )(x2, o_sb, rw_out, g_rw, gates, w_sb, w_rw, w_o, g_post)


def _merge_bwd(dx1, o_sb, rw_out, g_rw, gates, w_sb, w_rw, w_o, g_post, seq, tm):
    T = dx1.shape[0]

    def body(dx1_ref, osb_ref, rw_ref, g_ref, gate_ref, wsb_ref, wrw_ref, wo_ref, gp_ref,
             orw_o, mrg_o, dz_o, da_o, db_o, dgate_o, dosb_o, drw_o, dg_o, dgp_o, dbg_o):
        rw_out_v, g_rw_v = _join_heads(rw_ref), g_ref[...]
        w_sb_v, w_rw_v, w_o_v = wsb_ref[...], wrw_ref[...], wo_ref[...]
        o_rw, a, b, g1, g2, merged, z = _merge_math(osb_ref[...], rw_out_v, g_rw_v, gate_ref[...], w_sb_v, w_rw_v, w_o_v)
        gain = gp_ref[...]
        _, zn, rstd = _rms_fwd(z, gain)
        dz, dgain = _rms_bwd(dx1_ref[...], zn, rstd, gain)
        dzb = dz.astype(BF16)
        dm = _mm_nt(dzb, w_o_v)
        dab = (dm * g1).astype(BF16)
        dbb = (dm * g2).astype(BF16)
        dgate = jnp.concatenate([dm * a * g1 * (1.0 - g1), dm * b * g2 * (1.0 - g2)], axis=1)
        do_rw = _mm_nt(dbb, w_rw_v)
        orw_o[...] = o_rw
        mrg_o[...] = merged
        dz_o[...] = dzb
        da_o[...] = dab
        db_o[...] = dbb
        dgate_o[...] = dgate.astype(BF16)
        dosb_o[...] = _mm_nt(dab, w_sb_v).astype(BF16)
        _split_heads(do_rw * g_rw_v, drw_o)
        dg_o[...] = do_rw * rw_out_v
        first = pl.program_id(0) == 0
        _acc_out(dgp_o, dgain, first)
        _acc_out(dbg_o, jnp.sum(dgate, axis=0, keepdims=True), first)

    acc = lambda n: pl.BlockSpec((1, n), lambda i: (0, 0))
    sd = jax.ShapeDtypeStruct
    return pl.pallas_call(
        body, name="merge_bwd", grid=(T // tm,),
        in_specs=[_row_spec(tm, D_MODEL), _row_spec(tm, WIDTH), _head_spec(tm, seq), _row_spec(tm, WIDTH),
                  _row_spec(tm, GATE_COLS), _const_spec((WIDTH, D_MODEL)), _const_spec((WIDTH, D_MODEL)),
                  _const_spec((D_MODEL, D_MODEL)), _const_spec((1, D_MODEL))],
        out_specs=[_row_spec(tm, WIDTH), _row_spec(tm, D_MODEL), _row_spec(tm, D_MODEL), _row_spec(tm, D_MODEL),
                   _row_spec(tm, D_MODEL), _row_spec(tm, GATE_COLS), _row_spec(tm, WIDTH), _head_spec(tm, seq),
                   _row_spec(tm, WIDTH), acc(D_MODEL), acc(GATE_COLS)],
        out_shape=[sd((T, WIDTH), BF16), sd((T, D_MODEL), BF16), sd((T, D_MODEL), BF16), sd((T, D_MODEL), BF16),
                   sd((T, D_MODEL), BF16), sd((T, GATE_COLS), BF16), sd((T, WIDTH), BF16),
                   sd((T // seq, HEADS, seq, HEAD_DIM), F32), sd((T, WIDTH), F32), sd((1, D_MODEL), F32),
                   sd((1, GATE_COLS), F32)],
        compiler_params=_params(("arbitrary",)),
    )(dx1, o_sb, rw_out, g_rw, gates, w_sb, w_rw, w_o, g_post)


def _ffn(x1, target, g_pre, g_post, w_gate, w_up, w_down, tm):
    T = x1.shape[0]

    def body(x1_ref, tgt_ref, gpre_ref, gpost_ref, wg_ref, wu_ref, wd_ref,
             loss_o, dx1_o, h_o, dgate_o, dup_o, act_o, df_o, dgpre_o, dgpost_o):
        x1v = x1_ref[...]
        gpre, gpost = gpre_ref[...], gpost_ref[...]
        wg, wu, wd = wg_ref[...], wu_ref[...], wd_ref[...]
        hn, xn1, rstd1 = _rms_fwd(x1v, gpre)
        h = hn.astype(BF16)
        gate = _mm(h, wg)
        up = _mm(h, wu)
        sg = jax.nn.sigmoid(gate)
        act = (gate * sg * up).astype(BF16)
        f = _mm(act, wd)
        fo, fn, rstd2 = _rms_fwd(f, gpost)
        diff = x1v + fo - tgt_ref[...]
        dy = diff * (1.0 / D_MODEL)
        df, dgpost = _rms_bwd(dy, fn, rstd2, gpost)
        dfb = df.astype(BF16)
        dact = _mm_nt(dfb, wd)
        dup = (dact * gate * sg).astype(BF16)
        dgate = (dact * up * (sg * (1.0 + gate * (1.0 - sg)))).astype(BF16)
        dh = _mm_nt(dgate, wg) + _mm_nt(dup, wu)
        dxn, dgpre = _rms_bwd(dh, xn1, rstd1, gpre)
        dx1_o[...] = dy + dxn
        h_o[...] = h
        dgate_o[...] = dgate
        dup_o[...] = dup
        act_o[...] = act
        df_o[...] = dfb
        first = pl.program_id(0) == 0
        part = jnp.sum(jnp.sum(diff * diff, axis=1, keepdims=True), axis=0, keepdims=True) * (0.5 / D_MODEL)
        _acc_out(loss_o, jnp.broadcast_to(part, (8, 128)), first)
        _acc_out(dgpre_o, dgpre, first)
        _acc_out(dgpost_o, dgpost, first)

    acc = lambda r, n: pl.BlockSpec((r, n), lambda i: (0, 0))
    sd = jax.ShapeDtypeStruct
    return pl.pallas_call(
        body, name="ffn", grid=(T // tm,),
        in_specs=[_row_spec(tm, D_MODEL), _row_spec(tm, D_MODEL), _const_spec((1, D_MODEL)), _const_spec((1, D_MODEL)),
                  _const_spec((D_MODEL, D_FF)), _const_spec((D_MODEL, D_FF)), _const_spec((D_FF, D_MODEL))],
        out_specs=[acc(8, 128), _row_spec(tm, D_MODEL), _row_spec(tm, D_MODEL), _row_spec(tm, D_FF), _row_spec(tm, D_FF),
                   _row_spec(tm, D_FF), _row_spec(tm, D_MODEL), acc(1, D_MODEL), acc(1, D_MODEL)],
        out_shape=[sd((8, 128), F32), sd((T, D_MODEL), F32), sd((T, D_MODEL), BF16), sd((T, D_FF), BF16),
                   sd((T, D_FF), BF16), sd((T, D_FF), BF16), sd((T, D_MODEL), BF16), sd((1, D_MODEL), F32),
                   sd((1, D_MODEL), F32)],
        compiler_params=_params(("arbitrary",)),
    )(x1, target, g_pre, g_post, w_gate, w_up, w_down)


def _local_step(x, target, sm, wt, late=None):
    bl, seq, _ = x.shape
    T = bl * seq
    tm = min(ROW_TILE, T)
    x2 = x.reshape(T, D_MODEL)
    tgt2 = target.reshape(T, D_MODEL)
    h, qkv, prw, gates = _in_proj_fwd(x2, sm["norm_mix_pre"], wt["w_in"], sm["b_gate"], tm)
    if late is None:
        o_sb, lsum, sb_stop = _sb_fwd(qkv, bl, seq)
    else:
        o_sb, lsum, sb_stop, gathered = _sb_fwd(qkv, bl, seq, late)
        wt = {**wt, **_unpack_gathered(gathered, slice(_EARLY, None))}
    prep_params = [sm["mu_rw"], sm["w0"], wt["w_up"].astype(F32), sm["a0"], wt["a_up"].astype(F32),
                   wt["g_up"].astype(F32), sm["k_k"], sm["k_a"]]
    prep = _rw_prep_fwd(prw, prep_params, seq, tm)
    by_head = lambda t: t.reshape(bl, HEADS, seq, HEAD_DIM)
    seqs = [t.reshape(bl * HEADS, seq, HEAD_DIM) for t in prep[:6]]
    g_rw = prep[6]
    lnw, lnb, rk = (sm[n].reshape(HEADS, 1, HEAD_DIM) for n in ("lnx_w", "lnx_b", "r_k"))
    rw_out_h, states = _wkv_fwd(seqs, lnw, lnb, rk)
    rw_out = by_head(rw_out_h)
    x1 = _merge_fwd(x2, o_sb, rw_out, g_rw, gates, wt["w_sb_out"], wt["w_rw_out"], wt["w_o"], sm["norm_mix_post"],
                    seq, tm)
    (loss_part, dx1, h2, dffg, dffu, act, dff, d_nfpre, d_nfpost) = _ffn(
        x1, tgt2, sm["norm_ffn_pre"], sm["norm_ffn_post"], wt["w_ffn_gate"], wt["w_ffn_up"], wt["w_ffn_down"],
        min(ROW_TILE_FFN, T))
    (o_rw, merged, dz, da, db, dgate, do_sb, d_rw_out, d_g_rw, d_npost, d_bgate) = _merge_bwd(
        dx1, o_sb, rw_out, g_rw, gates, wt["w_sb_out"], wt["w_rw_out"], wt["w_o"], sm["norm_mix_post"], seq, tm)
    gdt = F32 if late is None else BF16
    gw = {
        "w_sb_out": _grad_w(o_sb, da, "gw_sb_out", gdt), "w_rw_out": _grad_w(o_rw, db, "gw_rw_out", gdt),
        "w_o": _grad_w(merged, dz, "gw_o", gdt),
        "w_ffn_gate": _grad_w(h2, dffg, "gw_ffn_gate", gdt), "w_ffn_up": _grad_w(h2, dffu, "gw_ffn_up", gdt),
        "w_ffn_down": _grad_w(act, dff, "gw_ffn_down", gdt),
    }
    dqkv = jnp.concatenate(_sb_bwd(qkv, do_sb, lsum, sb_stop, bl, seq), axis=1)
    ride = None if late is None else _pack_full_grads(gw, slice(_EARLY, None))
    wkv_g = _wkv_bwd(seqs, states, d_rw_out.reshape(bl * HEADS, seq, HEAD_DIM), lnw, lnb, rk, ride)
    late_grads = None if late is None else (ride, wkv_g[9])
    cts = [by_head(t) for t in wkv_g[:6]] + [d_g_rw]
    prep_g = _rw_prep_bwd(prw, prep_params, cts, seq, tm)
    dprw = prep_g[0]
    d_mu, d_w0, d_wup, d_a0, d_aup, d_gup, d_kk, d_ka = prep_g[1:]
    gw = {
        **gw,
        "w_in": jnp.concatenate([_grad_w(h, dqkv, "gw_in_qkv", gdt), _grad_w(h, dprw, "gw_in_rw", gdt),
                                 _grad_w(h, dgate, "gw_in_gate", gdt)], axis=1),
        "w_up": d_wup.astype(gdt), "a_up": d_aup.astype(gdt), "g_up": d_gup.astype(gdt),
    }
    ride = None if late is None else _pack_full_grads(gw, slice(0, _EARLY))
    grad_x, d_npre, *got = _in_proj_bwd(x2, sm["norm_mix_pre"], dx1, dqkv, dprw, dgate, wt["w_in"], tm, ride)
    if late is not None:
        late_grads = late_grads + (ride, got[0])
    gs = {
        "norm_mix_pre": d_npre, "b_gate": d_bgate, "mu_rw": d_mu, "w0": d_w0, "a0": d_a0, "k_k": d_kk, "k_a": d_ka,
        "r_k": wkv_g[8].reshape(1, WIDTH), "lnx_w": wkv_g[6].reshape(1, WIDTH), "lnx_b": wkv_g[7].reshape(1, WIDTH),
        "norm_mix_post": d_npost, "norm_ffn_pre": d_nfpre, "norm_ffn_post": d_nfpost,
    }
    return loss_part, grad_x.reshape(x.shape), gw, gs, late_grads


_SHARDED = [("w_in", 1, (D_MODEL, (SB_COLS + RW_COLS + GATE_COLS) // N_DEV)), ("w_up", 1, (W_LORA, WIDTH // N_DEV)),
            ("a_up", 1, (A_LORA, WIDTH // N_DEV)), ("g_up", 1, (G_LORA, WIDTH // N_DEV)),
            ("w_sb_out", 1, (WIDTH, D_MODEL // N_DEV)), ("w_rw_out", 1, (WIDTH, D_MODEL // N_DEV)),
            ("w_o", 0, (D_MODEL // N_DEV, D_MODEL)), ("w_ffn_gate", 1, (D_MODEL, D_FF // N_DEV)),
            ("w_ffn_up", 1, (D_MODEL, D_FF // N_DEV)), ("w_ffn_down", 0, (D_FF // N_DEV, D_MODEL))]
_LANES = 128
_PACK_ROWS = [s[0] * s[1] // _LANES for _, _, s in _SHARDED]
_PACK_TOTAL = sum(_PACK_ROWS)
_SMALL = [("norm_mix_pre", D_MODEL), ("b_gate", GATE_COLS), ("mu_rw", RW_COLS), ("w0", WIDTH), ("a0", WIDTH),
          ("k_k", WIDTH), ("k_a", WIDTH), ("r_k", WIDTH), ("lnx_w", WIDTH), ("lnx_b", WIDTH),
          ("norm_mix_post", D_MODEL), ("norm_ffn_pre", D_MODEL), ("norm_ffn_post", D_MODEL)]
_SMALL_ROWS = 96


def _pack_shards(shards, dtype):
    return jnp.concatenate([shards[n].astype(dtype).reshape(-1, _LANES) for n, _, _ in _SHARDED], axis=0)


def _unpack_shards(packed, which):
    out, r0 = {}, 0
    for (n, _, shp), rows in zip(_SHARDED[which], _PACK_ROWS[which]):
        out[n] = packed[r0:r0 + rows].reshape(shp)
        r0 += rows
    return out


_EARLY = 4
_EARLY_ROWS = sum(_PACK_ROWS[:_EARLY])


def _unpack_gathered(g, which):
    out, r0 = {}, 0
    for (n, axis, shp), rows in zip(_SHARDED[which], _PACK_ROWS[which]):
        blk = g[:, r0:r0 + rows].reshape((N_DEV,) + shp)
        out[n] = blk.reshape(N_DEV * shp[0], shp[1]) if axis == 0 else blk.transpose(1, 0, 2).reshape(shp[0], N_DEV * shp[1])
        r0 += rows
    return out


def _pack_full_grads(gw, which):
    parts = []
    for n, axis, shp in _SHARDED[which]:
        g = gw[n]
        blk = g.reshape((N_DEV,) + shp) if axis == 0 else g.reshape(shp[0], N_DEV, shp[1]).transpose(1, 0, 2)
        parts.append(blk.reshape(N_DEV, -1, _LANES))
    return jnp.concatenate(parts, axis=1)


def _pack_small(vals, extra=None):
    used = sum(sz for _, sz in _SMALL)
    tail = jnp.zeros((1, _SMALL_ROWS * _LANES - used), F32).at[0, 0].set(extra)
    return jnp.concatenate([vals[n].reshape(1, -1) for n, _ in _SMALL] + [tail], axis=1)


_ANY = pl.BlockSpec(memory_space=pl.ANY)


def _all_gather(block):
    rows, lanes = block.shape

    def body(x_ref, out_ref, send_sems, recv_sems, local_sem):
        start, forward, finish = _gather_steps(x_ref, out_ref, send_sems, recv_sems, local_sem)
        start()
        forward()
        finish()

    return pl.pallas_call(
        body, name="all_gather_weights", in_specs=[_ANY], out_specs=_ANY,
        out_shape=jax.ShapeDtypeStruct((N_DEV, rows, lanes), block.dtype), scratch_shapes=_GATHER_SEMS,
    )(block)


def _scatter_steps(pack_ref, got_ref, send_sems, recv_sems):
    x, y, c = lax.axis_index("x"), lax.axis_index("y"), lax.axis_index("c")

    def copies():
        out = []
        for k in range(1, N_DEV):
            px, py, pc = x ^ (k >> 2), y ^ ((k >> 1) & 1), c ^ (k & 1)
            out.append(pltpu.make_async_remote_copy(
                src_ref=pack_ref.at[4 * px + 2 * py + pc], dst_ref=got_ref.at[k - 1], send_sem=send_sems.at[k - 1],
                recv_sem=recv_sems.at[k - 1], device_id=(px, py, pc), device_id_type=MESH))
        return out

    def start():
        for cp in copies():
            cp.start()

    def finish():
        for cp in copies():
            cp.wait_recv()
        for cp in copies():
            cp.wait_send()

    return start, finish


def _sum_direct(pack, got, me, name):
    _, rows, lanes = pack.shape
    tr = _pick_rows(rows)

    def body(me_ref, own_ref, got_ref, o_ref):
        total = own_ref[...].astype(F32)
        for k in range(N_DEV - 1):
            total = total + got_ref[k].astype(F32)
        o_ref[...] = total

    return pl.pallas_call(
        body, name=name,
        grid_spec=pltpu.PrefetchScalarGridSpec(
            num_scalar_prefetch=1, grid=(rows // tr,),
            in_specs=[pl.BlockSpec((None, tr, lanes), lambda i, me_ref: (me_ref[0], i, 0)),
                      pl.BlockSpec((N_DEV - 1, tr, lanes), lambda i, me_ref: (0, i, 0))],
            out_specs=pl.BlockSpec((tr, lanes), lambda i, me_ref: (i, 0))),
        out_shape=jax.ShapeDtypeStruct((rows, lanes), F32),
        compiler_params=_params(("parallel",)),
    )(me, pack, got)


def _pick_rows(rows, cap=2048):
    return max(t for t in range(16, cap + 1, 16) if rows % t == 0)


_GATHER_SEMS = [pltpu.SemaphoreType.DMA((7,)), pltpu.SemaphoreType.DMA((7,)), pltpu.SemaphoreType.DMA]


def _gather_steps(x_ref, out_ref, send_sems, recv_sems, local_sem):
    x, y, c = lax.axis_index("x"), lax.axis_index("y"), lax.axis_index("c")
    me, sibling = (x, y, c), (x, y, 1 - c)
    chips = [(1 - x, y), (x, 1 - y), (1 - x, 1 - y)]

    def slot(px, py, pc):
        return out_ref.at[4 * px + 2 * py + pc]

    def copy(k, blk, to, src=None):
        return pltpu.make_async_remote_copy(
            src_ref=slot(*blk) if src is None else src, dst_ref=slot(*blk),
            send_sem=send_sems.at[k], recv_sem=recv_sems.at[k], device_id=to, device_id_type=MESH)

    def first():
        return [copy(0, me, sibling, src=x_ref)] + [copy(1 + j, me, (*chip, c), src=x_ref) for j, chip in enumerate(chips)]

    def passed():
        return [copy(4 + j, (*chip, c), sibling) for j, chip in enumerate(chips)]

    def start():
        pltpu.make_async_copy(x_ref, slot(*me), local_sem).start()
        for cp in first():
            cp.start()

    def forward():
        for j, (chip, cp) in enumerate(zip(chips, passed())):
            copy(1 + j, (*chip, c), me).wait_recv()
            cp.start()

    def finish():
        copy(0, sibling, me).wait_recv()
        for j, chip in enumerate(chips):
            copy(4 + j, (*chip, 1 - c), me).wait_recv()
        for cp in first() + passed():
            cp.wait_send()
        pltpu.make_async_copy(x_ref, slot(*me), local_sem).wait()

    return start, forward, finish


def _gather_small(small):
    def body(small_ref, parts_ref, send_sems, recv_sems, local_sem):
        x, y, c = lax.axis_index("x"), lax.axis_index("y"), lax.axis_index("c")
        me = 4 * x + 2 * y + c
        mine = pltpu.make_async_copy(small_ref, parts_ref.at[me], local_sem)
        mine.start()
        peers = [(x ^ (k >> 2), y ^ ((k >> 1) & 1), c ^ (k & 1)) for k in range(1, N_DEV)]
        sends = [pltpu.make_async_remote_copy(
            src_ref=small_ref, dst_ref=parts_ref.at[me], send_sem=send_sems.at[k], recv_sem=recv_sems.at[k],
            device_id=to, device_id_type=MESH) for k, to in enumerate(peers)]
        for cp in sends:
            cp.start()
        for k, (px, py, pc) in enumerate(peers):
            pltpu.make_async_remote_copy(
                src_ref=small_ref, dst_ref=parts_ref.at[4 * px + 2 * py + pc], send_sem=send_sems.at[k],
                recv_sem=recv_sems.at[k], device_id=(px, py, pc), device_id_type=MESH).wait_recv()
        for cp in sends:
            cp.wait_send()
        mine.wait()

    return pl.pallas_call(
        body, name="gather_small", in_specs=[_ANY], out_specs=_ANY,
        out_shape=jax.ShapeDtypeStruct((N_DEV,) + small.shape, F32),
        scratch_shapes=[pltpu.SemaphoreType.DMA((N_DEV - 1,)), pltpu.SemaphoreType.DMA((N_DEV - 1,)),
                        pltpu.SemaphoreType.DMA],
    )(small)


def _adamw_math(w, g, m, v):
    m = ADAM_B1 * m + (1.0 - ADAM_B1) * g
    v = ADAM_B2 * v + (1.0 - ADAM_B2) * (g * g)
    m_hat = m / (1.0 - ADAM_B1 ** ADAM_STEP)
    v_hat = v / (1.0 - ADAM_B2 ** ADAM_STEP)
    return -ADAM_LR * (m_hat / (jnp.sqrt(v_hat) + ADAM_EPS) + ADAM_WD * w), m, v


def _adamw(w, g, m, v, name):
    rows, cols = w.shape
    tr = 256 if rows % 256 == 0 and rows * cols > 2 ** 19 else rows

    def body(w_ref, g_ref, m_ref, v_ref, d_o, m_o, v_o):
        d_o[...], m_o[...], v_o[...] = _adamw_math(w_ref[...], g_ref[...], m_ref[...], v_ref[...])

    spec = pl.BlockSpec((tr, cols), lambda i: (i, 0))
    return pl.pallas_call(
        body, name=name, grid=(rows // tr,), in_specs=[spec] * 4, out_specs=[spec] * 3,
        out_shape=[jax.ShapeDtypeStruct((rows, cols), F32)] * 3, compiler_params=_params(("parallel",)),
    )(w, g, m, v)


def _adamw_small(parts, ws, ms, vs):
    k = len(_SMALL)

    def body(p_ref, *refs):
        w_refs, m_refs, v_refs = refs[:k], refs[k:2 * k], refs[2 * k:3 * k]
        outs = refs[3 * k:]
        g = p_ref[0]
        for d in range(1, N_DEV):
            g = g + p_ref[d]
        o = 0
        for i, (_, n) in enumerate(_SMALL):
            gp = g[:, o:o + n]
            outs[1 + i][...] = gp
            outs[1 + k + i][...], outs[1 + 2 * k + i][...], outs[1 + 3 * k + i][...] = _adamw_math(
                w_refs[i][...], gp, m_refs[i][...], v_refs[i][...])
            o += n
        outs[0][...] = g[:, o:o + _LANES]

    shapes = [jax.ShapeDtypeStruct((1, n), F32) for _, n in _SMALL]
    out = pl.pallas_call(
        body, name="adamw_small", out_shape=[jax.ShapeDtypeStruct((1, _LANES), F32)] + shapes * 4,
        compiler_params=_params(),
    )(parts, *ws, *ms, *vs)
    return out[0][0, 0], out[1:1 + k], out[1 + k:1 + 2 * k], out[1 + 2 * k:1 + 3 * k], out[1 + 3 * k:]


_WEIGHT_NAMES = ['norm_mix_pre', 'w_in', 'b_gate', 'mu_rw', 'w0', 'w_up', 'a0', 'a_up', 'g_up', 'k_k', 'k_a', 'r_k',
                 'lnx_w', 'lnx_b', 'w_sb_out', 'w_rw_out', 'w_o', 'norm_mix_post', 'norm_ffn_pre', 'w_ffn_gate',
                 'w_ffn_up', 'w_ffn_down', 'norm_ffn_post']


def _step(x, target, w, m, v):
    sharded = [n for n, _, _ in _SHARDED]
    sm = {n: w[n].reshape(1, -1) for n, _ in _SMALL}
    own = {n: w[n][0] for n in sharded}
    packed = _pack_shards(own, BF16)
    wt = _unpack_gathered(_all_gather(packed[:_EARLY_ROWS]), slice(0, _EARLY))
    loss_part, grad_x, _, gs, scattered = _local_step(x, target, sm, wt, packed[_EARLY_ROWS:])
    late_pack, late_got, early_pack, early_got = scattered

    me = 4 * lax.axis_index("x") + 2 * lax.axis_index("y") + lax.axis_index("c")
    me = jnp.reshape(me, (1,)).astype(jnp.int32)
    small_parts = _gather_small(_pack_small(gs, loss_part[0, 0]))
    g_sh = {**_unpack_shards(_sum_direct(early_pack, early_got, me, "sum_grads_a"), slice(0, _EARLY)),
            **_unpack_shards(_sum_direct(late_pack, late_got, me, "sum_grads_b"), slice(_EARLY, None))}

    row = lambda t: [t[n].reshape(1, -1) for n, _ in _SMALL]
    loss, *by_kind = _adamw_small(small_parts, row(w), row(m), row(v))
    g_s, d_s, m_s, v_s = ({n: t[i] for i, (n, _) in enumerate(_SMALL)} for t in by_kind)

    grads, deltas, new_m, new_v = {}, {}, {}, {}
    for n in _WEIGHT_NAMES:
        if n in g_sh:
            d_, m_, v_ = _adamw(own[n], g_sh[n], m[n][0], v[n][0], "adamw_" + n)
            grads[n], deltas[n], new_m[n], new_v[n] = (t.reshape(w[n].shape) for t in (g_sh[n], d_, m_, v_))
        else:
            grads[n], deltas[n], new_m[n], new_v[n] = (t[n].reshape(w[n].shape) for t in (g_s, d_s, m_s, v_s))
    return (loss, grad_x, *[grads[n] for n in _WEIGHT_NAMES], *[deltas[n] for n in _WEIGHT_NAMES],
            *[new_m[n] for n in _WEIGHT_NAMES], *[new_v[n] for n in _WEIGHT_NAMES])


def kernel(x, norm_mix_pre, w_in, b_gate, mu_rw, w0, w_up, a0, a_up, g_up, k_k, k_a, r_k, lnx_w, lnx_b, w_sb_out, w_rw_out, w_o, norm_mix_post, norm_ffn_pre, w_ffn_gate, w_ffn_up, w_ffn_down, norm_ffn_post, loss_target, m_norm_mix_pre, m_w_in, m_b_gate, m_mu_rw, m_w0, m_w_up, m_a0, m_a_up, m_g_up, m_k_k, m_k_a, m_r_k, m_lnx_w, m_lnx_b, m_w_sb_out, m_w_rw_out, m_w_o, m_norm_mix_post, m_norm_ffn_pre, m_w_ffn_gate, m_w_ffn_up, m_w_ffn_down, m_norm_ffn_post, v_norm_mix_pre, v_w_in, v_b_gate, v_mu_rw, v_w0, v_w_up, v_a0, v_a_up, v_g_up, v_k_k, v_k_a, v_r_k, v_lnx_w, v_lnx_b, v_w_sb_out, v_w_rw_out, v_w_o, v_norm_mix_post, v_norm_ffn_pre, v_w_ffn_gate, v_w_ffn_up, v_w_ffn_down, v_norm_ffn_post):
    args = locals()
    w = {n: args[n] for n in _WEIGHT_NAMES}
    m = {n: args["m_" + n] for n in _WEIGHT_NAMES}
    v = {n: args["v_" + n] for n in _WEIGHT_NAMES}
    return _step(x, loss_target, w, m, v)
```

```python
import functools

import jax
import jax.numpy as jnp
from jax import lax
from jax.experimental import pallas as pl
from jax.experimental.pallas import tpu as pltpu

F32 = jnp.float32
BF16 = jnp.bfloat16

D_MODEL = 1024
HEADS = 8
HEAD_DIM = 64
WIDTH = HEADS * HEAD_DIM
W_LORA, A_LORA, G_LORA = 64, 64, 128
SB_COLS = 3 * WIDTH
RW_COLS = 3 * WIDTH + W_LORA + A_LORA + G_LORA
GATE_COLS = 2 * D_MODEL
D_FF = 2816
RMS_EPS = 1e-6
GN_EPS = HEAD_DIM * 1e-5
N_DEV = 8

ADAM_LR, ADAM_B1, ADAM_B2, ADAM_EPS, ADAM_WD, ADAM_STEP = 0.001, 0.9, 0.999, 1e-08, 0.01, 10

ROW_TILE = 512
ROW_TILE_FFN = 256
SCAN_CHUNK = 64
ATT_ALIGN = 128
ATT_WINDOW = 384
ATT_Q = 128
ATT_PAIRS = 2
SB_DEAD = -104.0
SCAN_SEQS_FWD = 4
SCAN_SEQS_BWD = 2
SCAN_PASSES = 1
VMEM_LIMIT = 56 * 2 ** 20

MESH = pl.DeviceIdType.MESH


def _params(sem=None, vmem=VMEM_LIMIT):
    kw = dict(vmem_limit_bytes=vmem)
    if sem is not None:
        kw["dimension_semantics"] = sem
    return pltpu.CompilerParams(**kw)


def _const_spec(shape):
    nd = len(shape)
    return pl.BlockSpec(shape, lambda *_: (0,) * nd, pipeline_mode=pl.Buffered(1))


def _row_spec(tm, n):
    return pl.BlockSpec((tm, n), lambda i: (i, 0))


def _mm(a, b):
    return lax.dot_general(a, b, (((1,), (0,)), ((), ())), preferred_element_type=F32)


def _mm_nt(a, b):
    return lax.dot_general(a, b, (((1,), (1,)), ((), ())), preferred_element_type=F32)


def _mm_tn(a, b):
    return lax.dot_general(a, b, (((0,), (0,)), ((), ())), preferred_element_type=F32)


def _softplus(z):
    return jnp.maximum(z, 0.0) + jnp.log1p(jnp.exp(-jnp.abs(z)))


def _rms_fwd(x, gain):
    rstd = lax.rsqrt(jnp.mean(x * x, axis=-1, keepdims=True) + RMS_EPS)
    xn = x * rstd
    return xn * gain, xn, rstd


def _rms_bwd(dy, xn, rstd, gain):
    u = dy * gain
    dx = rstd * (u - xn * jnp.mean(u * xn, axis=-1, keepdims=True))
    return dx, jnp.sum(dy * xn, axis=0, keepdims=True)


def _acc_out(ref, val, first):
    @pl.when(first)
    def _():
        ref[...] = val

    @pl.when(jnp.logical_not(first))
    def _():
        ref[...] += val


_IN_COLS = SB_COLS + RW_COLS + GATE_COLS
_QKV_OF, _RW_OF, _GATE_OF = slice(0, SB_COLS), slice(SB_COLS, SB_COLS + RW_COLS), slice(SB_COLS + RW_COLS, _IN_COLS)


def _in_proj_fwd(x2, g_pre, w_in, b_gate, tm):
    T = x2.shape[0]

    def body(x_ref, g_ref, w_ref, b_ref, h_ref, qkv_ref, prw_ref, gate_ref):
        h = _rms_fwd(x_ref[...], g_ref[...])[0].astype(BF16)
        h_ref[...] = h
        qkv_ref[...] = _mm(h, w_ref[:, _QKV_OF]).astype(BF16)
        prw_ref[...] = _mm(h, w_ref[:, _RW_OF])
        gate_ref[...] = jax.nn.sigmoid(_mm(h, w_ref[:, _GATE_OF]) + b_ref[...])

    return pl.pallas_call(
        body, name="in_proj_fwd", grid=(T // tm,),
        in_specs=[_row_spec(tm, D_MODEL), _const_spec((1, D_MODEL)), _const_spec((D_MODEL, _IN_COLS)),
                  _const_spec((1, GATE_COLS))],
        out_specs=[_row_spec(tm, D_MODEL), _row_spec(tm, SB_COLS), _row_spec(tm, RW_COLS), _row_spec(tm, GATE_COLS)],
        out_shape=[jax.ShapeDtypeStruct((T, D_MODEL), BF16), jax.ShapeDtypeStruct((T, SB_COLS), BF16),
                   jax.ShapeDtypeStruct((T, RW_COLS), F32), jax.ShapeDtypeStruct((T, GATE_COLS), F32)],
        compiler_params=_params(("parallel",)),
    )(x2, g_pre, w_in, b_gate)


def _in_proj_bwd(x2, g_pre, dx1, dqkv, dprw, dgate, w_in, tm, ride=None):
    T = x2.shape[0]
    steps = T // tm
    nt = 0 if ride is None else len(ride)

    def body(x_ref, g_ref, dx1_ref, dq_ref, dr_ref, dg_ref, w_ref, *rest):
        gx_ref, dgain_ref = rest[nt:nt + 2]
        if ride is not None:
            start, finish = _scatter_steps(rest[:nt], rest[nt + 2:2 * nt + 2], rest[-2], rest[-1])
            pl.when(pl.program_id(0) == 0)(start)
        dh = (_mm_nt(dq_ref[...], w_ref[:, _QKV_OF]) + _mm_nt(dr_ref[...], w_ref[:, _RW_OF])
              + _mm_nt(dg_ref[...], w_ref[:, _GATE_OF]))
        gain = g_ref[...]
        _, xn, rstd = _rms_fwd(x_ref[...], gain)
        dx, dgain = _rms_bwd(dh, xn, rstd, gain)
        gx_ref[...] = dx1_ref[...] + dx
        _acc_out(dgain_ref, dgain, pl.program_id(0) == 0)
        if ride is not None:
            pl.when(pl.program_id(0) == steps - 1)(finish)

    in_specs = [_row_spec(tm, D_MODEL), _const_spec((1, D_MODEL)), _row_spec(tm, D_MODEL), _row_spec(tm, SB_COLS),
                _row_spec(tm, RW_COLS), _row_spec(tm, GATE_COLS), _const_spec((D_MODEL, _IN_COLS))]
    out_specs = [_row_spec(tm, D_MODEL), pl.BlockSpec((1, D_MODEL), lambda i: (0, 0))]
    out_shape = [jax.ShapeDtypeStruct((T, D_MODEL), F32), jax.ShapeDtypeStruct((1, D_MODEL), F32)]
    args, scratch = (x2, g_pre, dx1, dqkv, dprw, dgate, w_in), []
    if ride is not None:
        in_specs, out_specs, args = in_specs + [_ANY] * nt, out_specs + [_ANY] * nt, args + tuple(ride)
        got_shapes, scratch = _scatter_results(ride)
        out_shape = out_shape + got_shapes
    return pl.pallas_call(
        body, name="in_proj_bwd", grid=(steps,), in_specs=in_specs, out_specs=out_specs, out_shape=out_shape,
        scratch_shapes=scratch, compiler_params=_params(("arbitrary",)),
    )(*args)


def _pick_tile(n, cap):
    best = None
    for t in range(128, min(n, cap) + 1, 128):
        if n % t == 0:
            best = t
    return n if best is None else best


def _grad_w(a, b, name, dtype=F32):
    T, K = a.shape
    N = b.shape[1]
    tk, tn, tt = _pick_tile(K, 1408), _pick_tile(N, 2048), min(T, 2048)
    steps = T // tt

    def body(a_ref, b_ref, o_ref, *acc):
        t = pl.program_id(2)
        part = _mm_tn(a_ref[...], b_ref[...])
        if not acc:
            _acc_out(o_ref, part, t == 0)
        else:
            _acc_out(acc[0], part, t == 0)

            @pl.when(t == steps - 1)
            def _():
                o_ref[...] = acc[0][...].astype(dtype)

    return pl.pallas_call(
        body, name=name, grid=(K // tk, N // tn, steps),
        in_specs=[pl.BlockSpec((tt, tk), lambda i, j, t: (t, i)), pl.BlockSpec((tt, tn), lambda i, j, t: (t, j))],
        out_specs=pl.BlockSpec((tk, tn), lambda i, j, t: (i, j)),
        out_shape=jax.ShapeDtypeStruct((K, N), dtype),
        scratch_shapes=[] if dtype == F32 else [pltpu.VMEM((tk, tn), F32)],
        compiler_params=_params(("parallel", "parallel", "arbitrary")),
    )(a, b)


def _tri(n, kind):
    r = lax.broadcasted_iota(jnp.int32, (n, n), 0)
    c = lax.broadcasted_iota(jnp.int32, (n, n), 1)
    return {"gt": r > c, "le": r <= c, "lt": r < c, "ge": r >= c}[kind]


def _running_sums(x, carry, tri, kb, reverse=False):
    blocks = range(x.shape[1] // kb)
    parts = {}
    for b in (reversed(blocks) if reverse else blocks):
        piece = x[:, b * kb:(b + 1) * kb]
        parts[b] = carry + _mm(piece.astype(BF16), tri)
        carry = carry + jnp.sum(piece, axis=1, keepdims=True)
    return jnp.concatenate([parts[b] for b in blocks], axis=1), carry


def _sb_valid(row0, col0, first, last, qb, kb):
    ahead = lax.broadcasted_iota(jnp.int32, (qb, kb), 1) - lax.broadcasted_iota(jnp.int32, (qb, kb), 0)
    col = lax.broadcasted_iota(jnp.int32, (1, kb), 1)
    return jnp.logical_and(ahead < row0 - col0, jnp.logical_and(col >= first - col0, col < last - col0))


def _sb_softplus(z):
    return jnp.maximum(z, 0.0) + jnp.log(1.0 + jnp.exp(-jnp.abs(z)))


_PAIR = 2 * HEAD_DIM
_PAIRS = WIDTH // _PAIR


def _first_head_lanes():
    return lax.broadcasted_iota(jnp.int32, (1, _PAIR), 1) < HEAD_DIM


def _per_head(t, first_head):
    zero = jnp.zeros_like(t)
    return jnp.where(first_head, t, zero), jnp.where(first_head, zero, t)


def _sb_fwd(qkv, bl, seq, ride=None):
    qb, win, kb = min(ATT_Q, seq), min(ATT_WINDOW, seq), ATT_ALIGN
    nq = seq // qb
    nh, width, groups = 2 * ATT_PAIRS, ATT_PAIRS * _PAIR, _PAIRS // ATT_PAIRS
    pair_of = lambda h: slice((h // 2) * _PAIR, (h // 2 + 1) * _PAIR)

    steps = bl * groups
    pass_on_at = (5 * steps) // 8

    def body(q_ref, k_ref, v_ref, *rest):
        g = pl.program_id(0) * groups + pl.program_id(1)
        if ride is None:
            o_ref, l_ref, stop_ref = rest
        else:
            ride_ref, o_ref, l_ref, stop_ref, gathered_ref, *sems = rest
            start, forward, finish = _gather_steps(ride_ref, gathered_ref, *sems)
            pl.when(g == 0)(start)
            pl.when(g == pass_on_at)(forward)
        first_head = _first_head_lanes()
        u_after = _tri(kb, "gt").astype(BF16)

        def qblock(i, _):
            rows = pl.ds(pl.multiple_of(i * qb, qb), qb)
            qs = q_ref[rows, :] * (HEAD_DIM ** -0.5)
            qh = [_per_head(qs[:, pair_of(h)], first_head)[h % 2] for h in range(nh)]

            def live(carry):
                return jnp.logical_and(carry[0] > 0, carry[3] > 0)

            def window(carry):
                hi, accs, cs, _ = carry
                lo = pl.multiple_of(jnp.maximum(hi - win, 0), kb)
                cols = pl.ds(lo, win)
                kv, vv = k_ref[cols, :], v_ref[cols, :]
                valid = _sb_valid(i * qb, lo, lo, hi, qb, win)
                accs, cs = list(accs), list(cs)
                for h in range(nh):
                    z = _mm_nt(qh[h], kv[:, pair_of(h)])
                    sp = _sb_softplus(z)
                    spm = jnp.where(valid, sp, 0.0)
                    after, cs[h] = _running_sums(spm, cs[h], u_after, kb, reverse=True)
                    w = jnp.where(valid, jnp.exp(z - sp - after), 0.0)
                    accs[h] = accs[h] + _mm(w.astype(BF16), vv[:, pair_of(h)])
                alive = functools.reduce(jnp.minimum, [jnp.min(c) for c in cs]) < -SB_DEAD
                return lo, tuple(accs), tuple(cs), alive.astype(jnp.int32)

            zero_acc, zero_c = jnp.zeros((qb, _PAIR), F32), jnp.zeros((qb, 1), F32)
            lo, accs, cs, _ = lax.while_loop(
                live, window, ((i + 1) * qb, (zero_acc,) * nh, (zero_c,) * nh, jnp.int32(1)))
            for pp in range(ATT_PAIRS):
                o_ref[rows, pp * _PAIR:(pp + 1) * _PAIR] = jnp.where(
                    first_head, accs[2 * pp], accs[2 * pp + 1]).astype(BF16)
            for h in range(nh):
                l_ref[h, rows, :] = cs[h]
            stop_ref[g, i] = lo
            return 0

        lax.fori_loop(0, nq, qblock, 0)
        if ride is not None:
            pl.when(g == steps - 1)(finish)

    col = lambda off: pl.BlockSpec((seq, width), lambda b, p: (b, off + p))
    in_specs = [col(0), col(groups), col(2 * groups)]
    out_specs = [col(0), pl.BlockSpec((None, nh, seq, 1), lambda b, p: (b, p, 0, 0)), pl.BlockSpec(memory_space=pltpu.SMEM)]
    out_shape = [jax.ShapeDtypeStruct((bl * seq, WIDTH), BF16), jax.ShapeDtypeStruct((bl, HEADS, seq, 1), F32),
                 jax.ShapeDtypeStruct((bl * groups, nq), jnp.int32)]
    if ride is None:
        return pl.pallas_call(body, name="sb_fwd", grid=(bl, groups), in_specs=in_specs, out_specs=out_specs,
                              out_shape=out_shape, compiler_params=_params(("arbitrary", "arbitrary")))(qkv, qkv, qkv)
    return pl.pallas_call(
        body, name="sb_fwd", grid=(bl, groups), in_specs=in_specs + [_ANY], out_specs=out_specs + [_ANY],
        out_shape=out_shape + [jax.ShapeDtypeStruct((N_DEV,) + ride.shape, ride.dtype)], scratch_shapes=_GATHER_SEMS,
        compiler_params=_params(("arbitrary", "arbitrary")),
    )(qkv, qkv, qkv, ride)


def _sb_bwd(qkv, do, lsum, stop, bl, seq):
    qb, win, kb = min(ATT_Q, seq), min(ATT_WINDOW, seq), ATT_ALIGN
    nq = seq // qb
    nh, width, groups = 2 * ATT_PAIRS, ATT_PAIRS * _PAIR, _PAIRS // ATT_PAIRS
    pair_of = lambda h: slice((h // 2) * _PAIR, (h // 2 + 1) * _PAIR)

    def body(stop_ref, q_ref, k_ref, v_ref, do_ref, l_ref, dq_ref, dk_ref, dv_ref, dk_acc, dv_acc):
        g = pl.program_id(0) * groups + pl.program_id(1)
        first_head = _first_head_lanes()
        u_incl = _tri(kb, "le").astype(BF16)
        u_excl = _tri(kb, "lt").astype(BF16)
        dk_acc[...] = jnp.zeros_like(dk_acc)
        dv_acc[...] = jnp.zeros_like(dv_acc)

        def qblock(i, _):
            rows = pl.ds(pl.multiple_of(i * qb, qb), qb)
            qv = q_ref[rows, :]
            qs = qv * (HEAD_DIM ** -0.5)
            dob = do_ref[rows, :]
            qh = [_per_head(qs[:, pair_of(h)], first_head)[h % 2] for h in range(nh)]
            doh = [_per_head(dob[:, pair_of(h)], first_head)[h % 2] for h in range(nh)]
            ltot = [l_ref[h, rows, :] for h in range(nh)]

            first = (jnp.clip(stop_ref[g, i], 0, i * qb) // ATT_ALIGN) * ATT_ALIGN

            def window(n, carry):
                dqs, ps, es = (list(t) for t in carry)
                start = first + n * win
                lo = pl.multiple_of(jnp.minimum(start, seq - win), kb)
                cols = pl.ds(lo, win)
                kv, vv = k_ref[cols, :], v_ref[cols, :]
                valid = _sb_valid(i * qb, lo, start, seq, qb, win)
                dks, dvs = [], []
                for h in range(nh):
                    kp, vp = kv[:, pair_of(h)], vv[:, pair_of(h)]
                    z = _mm_nt(qh[h], kp)
                    sp = _sb_softplus(z)
                    spm = jnp.where(valid, sp, 0.0)
                    upto, ps[h] = _running_sums(spm, ps[h], u_incl, kb)
                    w = jnp.where(valid, jnp.exp(z - sp - (ltot[h] - upto)), 0.0)
                    e = _mm_nt(doh[h], vp) * w
                    dlf, es[h] = _running_sums(e, es[h], u_excl, kb)
                    sig = jnp.exp(z - sp)
                    dz = jnp.where(valid, e * (1.0 - sig) - dlf * sig, 0.0) * (HEAD_DIM ** -0.5)
                    dzb = dz.astype(BF16)
                    dvs.append(_mm_tn(w.astype(BF16), dob[:, pair_of(h)]))
                    dks.append(_mm_tn(dzb, qv[:, pair_of(h)]))
                    dqs[h] = dqs[h] + _mm(dzb, kp)
                for pp in range(ATT_PAIRS):
                    lanes = slice(pp * _PAIR, (pp + 1) * _PAIR)
                    dv_acc[cols, lanes] += jnp.where(first_head, dvs[2 * pp], dvs[2 * pp + 1])
                    dk_acc[cols, lanes] += jnp.where(first_head, dks[2 * pp], dks[2 * pp + 1])
                return tuple(dqs), tuple(ps), tuple(es)

            zero_q, zero_c = jnp.zeros((qb, _PAIR), F32), jnp.zeros((qb, 1), F32)
            dqs, _, _ = lax.fori_loop(0, ((i + 1) * qb - first + win - 1) // win, window,
                                      ((zero_q,) * nh, (zero_c,) * nh, (zero_c,) * nh))
            for pp in range(ATT_PAIRS):
                dq_ref[rows, pp * _PAIR:(pp + 1) * _PAIR] = jnp.where(
                    first_head, dqs[2 * pp], dqs[2 * pp + 1]).astype(BF16)
            return 0

        lax.fori_loop(0, nq, qblock, 0)
        dk_ref[...] = dk_acc[...].astype(BF16)
        dv_ref[...] = dv_acc[...].astype(BF16)

    col = lambda off: pl.BlockSpec((seq, width), lambda b, p, stop_ref: (b, off + p))
    return pl.pallas_call(
        body, name="sb_bwd",
        grid_spec=pltpu.PrefetchScalarGridSpec(
            num_scalar_prefetch=1, grid=(bl, groups),
            in_specs=[col(0), col(groups), col(2 * groups), col(0),
                      pl.BlockSpec((None, nh, seq, 1), lambda b, p, stop_ref: (b, p, 0, 0))],
            out_specs=[col(0), col(0), col(0)],
            scratch_shapes=[pltpu.VMEM((seq, width), F32), pltpu.VMEM((seq, width), F32)]),
        out_shape=[jax.ShapeDtypeStruct((bl * seq, WIDTH), BF16)] * 3,
        compiler_params=_params(("parallel", "parallel")),
    )(stop, qkv, qkv, qkv, do, lsum)


@jax.custom_vjp
def _lora_mm(x, w):
    return _mm(x.astype(BF16), w.astype(BF16))


_lora_mm.defvjp(
    lambda x, w: (_mm(x.astype(BF16), w.astype(BF16)), (x, w)),
    lambda res, ct: (_mm_nt(ct.astype(BF16), res[1].astype(BF16)), _mm_tn(res[0].astype(BF16), ct.astype(BF16))))


def _rw_prep_math(p, ps, mu, w0, w_up, a0, a_up, g_up, k_k, k_a):
    pm = p + (ps - p) * mu
    r, k, v = pm[:, :WIDTH], pm[:, WIDTH:2 * WIDTH], pm[:, 2 * WIDTH:3 * WIDTH]
    o = 3 * WIDTH
    xw, xa, xg = pm[:, o:o + W_LORA], pm[:, o + W_LORA:o + W_LORA + A_LORA], pm[:, o + W_LORA + A_LORA:]
    w_raw = w0 + _lora_mm(jnp.tanh(xw), w_up)
    lw = -jnp.exp(-_softplus(-w_raw) - 0.5)
    a = jax.nn.sigmoid(a0 + _lora_mm(xa, a_up))
    g = _lora_mm(jax.nn.sigmoid(xg), g_up)
    kk = k * k_k
    k2 = k * (1.0 + (a - 1.0) * k_a)
    return r, lw, k2, v, kk, a, g


def _shift_down(p, first_row):
    row = lax.broadcasted_iota(jnp.int32, p.shape, 0)
    return jnp.where(row == 0, first_row, pltpu.roll(p, 1, 0))


def _shift_up(p, last_row):
    row = lax.broadcasted_iota(jnp.int32, p.shape, 0)
    return jnp.where(row == p.shape[0] - 1, last_row, pltpu.roll(p, p.shape[0] - 1, 0))


_PREP_PARAM_SHAPES = [(1, RW_COLS), (1, WIDTH), (W_LORA, WIDTH), (1, WIDTH), (A_LORA, WIDTH), (G_LORA, WIDTH),
                      (1, WIDTH), (1, WIDTH)]


def _prev_rows_spec(tm):
    return pl.BlockSpec((8, RW_COLS), lambda i: (jnp.maximum(i * (tm // 8) - 1, 0), 0))


def _head_spec(tm, seq, tile_of=lambda i: i):
    per_seq = seq // tm
    return pl.BlockSpec((None, HEADS, tm, HEAD_DIM),
                        lambda i: (tile_of(i) // per_seq, 0, tile_of(i) % per_seq, 0))


def _split_heads(val, ref):
    for h in range(HEADS):
        ref[h] = val[:, h * HEAD_DIM:(h + 1) * HEAD_DIM]


def _join_heads(ref):
    return jnp.concatenate([ref[h] for h in range(HEADS)], axis=1)


def _rw_prep_fwd(prw, params, seq, tm):
    T = prw.shape[0]

    def body(p_ref, prev_ref, *rest):
        prm = [r_[...] for r_ in rest[:8]]
        outs = rest[8:]
        i = pl.program_id(0)
        first = jnp.where((i * tm) % seq == 0, 0.0, prev_ref[7:8, :])
        p = p_ref[...]
        vals = _rw_prep_math(p, _shift_down(p, first), *prm)
        for o_ref, val in zip(outs[:6], vals[:6]):
            _split_heads(val, o_ref)
        outs[6][...] = vals[6]

    by_head = jax.ShapeDtypeStruct((T // seq, HEADS, seq, HEAD_DIM), F32)
    return pl.pallas_call(
        body, name="rw_prep_fwd", grid=(T // tm,),
        in_specs=[_row_spec(tm, RW_COLS), _prev_rows_spec(tm)] + [_const_spec(s) for s in _PREP_PARAM_SHAPES],
        out_specs=[_head_spec(tm, seq)] * 6 + [_row_spec(tm, WIDTH)],
        out_shape=[by_head] * 6 + [jax.ShapeDtypeStruct((T, WIDTH), F32)],
        compiler_params=_params(("parallel",)),
    )(prw, prw, *params)


def _rw_prep_bwd(prw, params, cts, seq, tm):
    T = prw.shape[0]
    n = T // tm

    def body(p_ref, prev_ref, *rest):
        prm = [r_[...] for r_ in rest[:8]]
        ct = tuple(_join_heads(r_) for r_ in rest[8:14]) + (rest[14][...],)
        dp_ref = rest[15]
        dprm_refs = rest[16:24]
        carry = rest[24]
        step = pl.program_id(0)
        i = n - 1 - step
        first = jnp.where((i * tm) % seq == 0, 0.0, prev_ref[7:8, :])
        p = p_ref[...]
        _, vjp = jax.vjp(_rw_prep_math, p, _shift_down(p, first), *prm)
        grads = vjp(ct)
        dp, dps = grads[0], grads[1]
        nxt = jnp.where(jnp.logical_or(step == 0, ((i + 1) * tm) % seq == 0), 0.0, carry[0:1, :])
        dp_ref[...] = (dp + _shift_up(dps, nxt)).astype(BF16)
        carry[...] = dps[0:8, :]
        for ref, gval in zip(dprm_refs, grads[2:]):
            _acc_out(ref, gval, step == 0)

    rev = lambda w: pl.BlockSpec((tm, w), lambda s: (n - 1 - s, 0))
    prev = pl.BlockSpec((8, RW_COLS), lambda s: (jnp.maximum((n - 1 - s) * (tm // 8) - 1, 0), 0))
    return pl.pallas_call(
        body, name="rw_prep_bwd", grid=(n,),
        in_specs=([rev(RW_COLS), prev] + [_const_spec(s) for s in _PREP_PARAM_SHAPES]
                  + [_head_spec(tm, seq, lambda s: n - 1 - s)] * 6 + [rev(WIDTH)]),
        out_specs=[rev(RW_COLS)] + [pl.BlockSpec(s, lambda s_: (0, 0)) for s in _PREP_PARAM_SHAPES],
        out_shape=[jax.ShapeDtypeStruct((T, RW_COLS), BF16)] + [jax.ShapeDtypeStruct(s, F32) for s in _PREP_PARAM_SHAPES],
        scratch_shapes=[pltpu.VMEM((8, RW_COLS), F32)],
        compiler_params=_params(("arbitrary",)),
    )(prw, prw, *params, *cts)


def _make_bmm(passes):
    def raw(dn, a, b):
        d = lambda x, y: lax.dot_general(x, y, dn, preferred_element_type=F32)
        ah = a.astype(BF16)
        bh = b.astype(BF16)
        if passes == 1:
            return d(ah, bh)
        al = (a - ah.astype(F32)).astype(BF16)
        bl = (b - bh.astype(F32)).astype(BF16)
        return d(ah, bh) + (d(ah, bl) + d(al, bh))

    dn_nn = (((2,), (1,)), ((0,), (0,)))
    dn_nt = (((2,), (2,)), ((0,), (0,)))
    dn_tn = (((1,), (1,)), ((0,), (0,)))

    @jax.custom_vjp
    def nn(a, b):
        return raw(dn_nn, a, b)

    @jax.custom_vjp
    def nt(a, b):
        return raw(dn_nt, a, b)

    @jax.custom_vjp
    def tn(a, b):
        return raw(dn_tn, a, b)

    nn.defvjp(lambda a, b: (raw(dn_nn, a, b), (a, b)), lambda res, ct: (nt(ct, res[1]), tn(res[0], ct)))
    nt.defvjp(lambda a, b: (raw(dn_nt, a, b), (a, b)), lambda res, ct: (nn(ct, res[1]), tn(ct, res[0])))
    tn.defvjp(lambda a, b: (raw(dn_tn, a, b), (a, b)), lambda res, ct: (nt(res[1], ct), nn(res[0], ct)))

    def unit_lower_inverse(m):
        n = m.shape[-1]
        row = lax.broadcasted_iota(jnp.int32, (n, n), 0)
        col = lax.broadcasted_iota(jnp.int32, (n, n), 1)
        m16 = ((row // 16) == (col // 16)).astype(F32)
        m32 = ((row // 32) == (col // 32)).astype(F32)
        a1 = m * m16
        a2 = nn(a1, a1)
        a4 = nn(a2, a2)
        a8 = nn(a4, a4)
        inv = (row == col).astype(F32) - a1
        inv = inv + nn(inv, a2)
        inv = inv + nn(inv, a4)
        inv = inv + nn(inv, a8)
        inv = inv - nn(nn(inv, m * (m32 - m16)), inv)
        return inv - nn(nn(inv, m * (1.0 - m32)), inv)

    @jax.custom_vjp
    def inverse(m):
        return unit_lower_inverse(m)

    def inverse_fwd(m):
        inv = unit_lower_inverse(m)
        return inv, inv

    inverse.defvjp(inverse_fwd, lambda inv, ct: (-nt(tn(inv, ct), inv),))
    return nn, nt, tn, inverse


def _wkv_chunk(s0, r, lw, k, v, kk, a, lnw, lnb, rk):
    nn, nt, tn, inverse = _make_bmm(SCAN_PASSES)
    G, L, N = r.shape
    rep = lambda t: jnp.broadcast_to(t[None], (G // HEADS, HEADS, 1, N)).reshape(G, 1, N)
    kap = kk * lax.rsqrt(jnp.maximum(jnp.sum(kk * kk, axis=-1, keepdims=True), 1e-24))
    b = a * kap
    row = lax.broadcasted_iota(jnp.int32, (L, L), 0)
    col = lax.broadcasted_iota(jnp.int32, (L, L), 1)
    low_incl = (col <= row).astype(F32)
    low_strict = (col < row).astype(F32)
    c = _make_bmm(3)[0](jnp.broadcast_to(low_incl[None], (G, L, L)), lw)
    c_all = jnp.sum(lw, axis=1, keepdims=True)
    g_inv = jnp.exp(-c)
    kap_t = kap * jnp.exp(c - lw)
    b_t = b * g_inv
    k_t = k * g_inv
    r_t = r * jnp.exp(c)
    g_all = jnp.exp(c_all)
    m_b = nt(kap_t, b_t) * low_strict
    m_k = nt(kap_t, k_t) * low_strict
    n_b = nt(r_t, b_t) * low_incl
    n_k = nt(r_t, k_t) * low_incl
    rhs = -(nt(kap_t, s0) + nn(m_k, v))
    sa = nn(inverse(m_b), rhs)
    y = nt(r_t, s0) + nn(n_b, sa) + nn(n_k, v)
    s1 = s0 * g_all + tn(sa, b_t * g_all) + tn(v, k_t * g_all)
    mean = jnp.mean(y, axis=-1, keepdims=True)
    yc = y - mean
    var = jnp.mean(yc * yc, axis=-1, keepdims=True)
    out = yc * lax.rsqrt(var + GN_EPS) * rep(lnw) + rep(lnb)
    out = out + jnp.sum(r * k * rep(rk), axis=-1, keepdims=True) * v
    return out, s1


def _scan_heads_per_step(total_heads, seqs_wanted):
    n_seq = total_heads // HEADS
    return HEADS * max(d for d in range(1, seqs_wanted + 1) if n_seq % d == 0)


def _wkv_fwd(seqs, lnw, lnb, rk):
    G, S, N = seqs[0].shape
    L = SCAN_CHUNK
    nc = S // L

    def body(*refs):
        ins = [r_[...] for r_ in refs[:6]]
        prm = [r_[...] for r_ in refs[6:9]]
        out_ref, st_ref, state = refs[9], refs[10], refs[11]

        @pl.when(pl.program_id(1) == 0)
        def _():
            state[...] = jnp.zeros_like(state)

        s0 = state[...]
        st_ref[...] = s0
        out, s1 = _wkv_chunk(s0, *ins, *prm)
        out_ref[...] = out
        state[...] = s1

    gb = _scan_heads_per_step(G, SCAN_SEQS_FWD)
    blk = pl.BlockSpec((gb, L, N), lambda b, i: (b, i, 0))
    pspec = _const_spec((HEADS, 1, N))
    return pl.pallas_call(
        body, name="wkv_fwd", grid=(G // gb, nc), in_specs=[blk] * 6 + [pspec] * 3,
        out_specs=[blk, pl.BlockSpec((None, gb, N, N), lambda b, i: (i, b, 0, 0))],
        out_shape=[jax.ShapeDtypeStruct((G, S, N), F32), jax.ShapeDtypeStruct((nc, G, N, N), F32)],
        scratch_shapes=[pltpu.VMEM((gb, N, N), F32)],
        compiler_params=_params(("parallel", "arbitrary")),
    )(*seqs, lnw, lnb, rk)


def _wkv_bwd(seqs, states, dout, lnw, lnb, rk, ride=None):
    G, S, N = seqs[0].shape
    L = SCAN_CHUNK
    nc = S // L
    gb = _scan_heads_per_step(G, SCAN_SEQS_BWD)
    nt = 0 if ride is None else len(ride)

    def body(*refs):
        ins = [r_[...] for r_ in refs[:6]]
        s0 = refs[6][...]
        ct_out = refs[7][...]
        prm = [r_[...] for r_ in refs[8:11]]
        refs = refs[11:]
        if ride is not None:
            start, finish = _scatter_steps(refs[:nt], refs[nt + 9:2 * nt + 9], refs[-2], refs[-1])
            pl.when(jnp.logical_and(pl.program_id(0) == 0, pl.program_id(1) == 0))(start)
            refs = refs[nt:]
        d_refs = refs[0:6]
        dprm_refs = refs[6:9]
        dstate = refs[9 + nt]
        step = pl.program_id(1)

        @pl.when(step == 0)
        def _():
            dstate[...] = jnp.zeros_like(dstate)

        _, vjp = jax.vjp(_wkv_chunk, s0, *ins, *prm)
        grads = vjp((ct_out, dstate[...]))
        dstate[...] = grads[0]
        for ref, gval in zip(d_refs, grads[1:7]):
            ref[...] = gval
        for ref, gval in zip(dprm_refs, grads[7:]):
            _acc_out(ref, gval, jnp.logical_and(step == 0, pl.program_id(0) == 0))
        if ride is not None:
            pl.when(jnp.logical_and(pl.program_id(0) == G // gb - 1, step == nc - 1))(finish)

    blk = pl.BlockSpec((gb, L, N), lambda b, s: (b, nc - 1 - s, 0))
    pspec = _const_spec((HEADS, 1, N))
    pout = pl.BlockSpec((HEADS, 1, N), lambda b, s: (0, 0, 0))
    in_specs = [blk] * 6 + [pl.BlockSpec((None, gb, N, N), lambda b, s: (nc - 1 - s, b, 0, 0)), blk] + [pspec] * 3
    out_specs = [blk] * 6 + [pout] * 3
    out_shape = [jax.ShapeDtypeStruct((G, S, N), F32)] * 6 + [jax.ShapeDtypeStruct((HEADS, 1, N), F32)] * 3
    scratch = [pltpu.VMEM((gb, N, N), F32)]
    args = (*seqs, states, dout, lnw, lnb, rk)
    if ride is not None:
        in_specs, out_specs, args = in_specs + [_ANY] * nt, out_specs + [_ANY] * nt, args + tuple(ride)
        got_shapes, sems = _scatter_results(ride)
        out_shape, scratch = out_shape + got_shapes, scratch + sems
    return pl.pallas_call(
        body, name="wkv_bwd", grid=(G // gb, nc), in_specs=in_specs, out_specs=out_specs, out_shape=out_shape,
        scratch_shapes=scratch, compiler_params=_params(("arbitrary", "arbitrary")),
    )(*args)


def _merge_math(o_sb, rw_out, g_rw, gates, w_sb, w_rw, w_o):
    o_rw = (rw_out * g_rw).astype(BF16)
    a = _mm(o_sb, w_sb)
    b = _mm(o_rw, w_rw)
    g1, g2 = gates[:, :D_MODEL], gates[:, D_MODEL:]
    merged = (g1 * a + g2 * b).astype(BF16)
    return o_rw, a, b, g1, g2, merged, _mm(merged, w_o)


def _merge_fwd(x2, o_sb, rw_out, g_rw, gates, w_sb, w_rw, w_o, g_post, seq, tm):
    T = x2.shape[0]

    def body(x_ref, osb_ref, rw_ref, g_ref, gate_ref, wsb_ref, wrw_ref, wo_ref, gp_ref, x1_ref):
        z = _merge_math(osb_ref[...], _join_heads(rw_ref), g_ref[...], gate_ref[...], wsb_ref[...], wrw_ref[...], wo_ref[...])[-1]
        x1_ref[...] = x_ref[...] + _rms_fwd(z, gp_ref[...])[0]

    return pl.pallas_call(
        body, name="merge_fwd", grid=(T // tm,),
        in_specs=[_row_spec(tm, D_MODEL), _row_spec(tm, WIDTH), _head_spec(tm, seq), _row_spec(tm, WIDTH),
                  _row_spec(tm, GATE_COLS), _const_spec((WIDTH, D_MODEL)), _const_spec((WIDTH, D_MODEL)),
                  _const_spec((D_MODEL, D_MODEL)), _const_spec((1, D_MODEL))],
        out_specs=_row_spec(tm, D_MODEL),
        out_shape=jax.ShapeDtypeStruct((T, D_MODEL), F32),
        compiler_params=_params(("parallel",)),
    )(x2, o_sb, rw_out, g_rw, gates, w_sb, w_rw, w_o, g_post)


def _merge_bwd(dx1, o_sb, rw_out, g_rw, gates, w_sb, w_rw, w_o, g_post, seq, tm):
    T = dx1.shape[0]

    def body(dx1_ref, osb_ref, rw_ref, g_ref, gate_ref, wsb_ref, wrw_ref, wo_ref, gp_ref,
             orw_o, mrg_o, dz_o, da_o, db_o, dgate_o, dosb_o, drw_o, dg_o, dgp_o, dbg_o):
        rw_out_v, g_rw_v = _join_heads(rw_ref), g_ref[...]
        w_sb_v, w_rw_v, w_o_v = wsb_ref[...], wrw_ref[...], wo_ref[...]
        o_rw, a, b, g1, g2, merged, z = _merge_math(osb_ref[...], rw_out_v, g_rw_v, gate_ref[...], w_sb_v, w_rw_v, w_o_v)
        gain = gp_ref[...]
        _, zn, rstd = _rms_fwd(z, gain)
        dz, dgain = _rms_bwd(dx1_ref[...], zn, rstd, gain)
        dzb = dz.astype(BF16)
        dm = _mm_nt(dzb, w_o_v)
        dab = (dm * g1).astype(BF16)
        dbb = (dm * g2).astype(BF16)
        dgate = jnp.concatenate([dm * a * g1 * (1.0 - g1), dm * b * g2 * (1.0 - g2)], axis=1)
        do_rw = _mm_nt(dbb, w_rw_v)
        orw_o[...] = o_rw
        mrg_o[...] = merged
        dz_o[...] = dzb
        da_o[...] = dab
        db_o[...] = dbb
        dgate_o[...] = dgate.astype(BF16)
        dosb_o[...] = _mm_nt(dab, w_sb_v).astype(BF16)
        _split_heads(do_rw * g_rw_v, drw_o)
        dg_o[...] = do_rw * rw_out_v
        first = pl.program_id(0) == 0
        _acc_out(dgp_o, dgain, first)
        _acc_out(dbg_o, jnp.sum(dgate, axis=0, keepdims=True), first)

    acc = lambda n: pl.BlockSpec((1, n), lambda i: (0, 0))
    sd = jax.ShapeDtypeStruct
    return pl.pallas_call(
        body, name="merge_bwd", grid=(T // tm,),
        in_specs=[_row_spec(tm, D_MODEL), _row_spec(tm, WIDTH), _head_spec(tm, seq), _row_spec(tm, WIDTH),
                  _row_spec(tm, GATE_COLS), _const_spec((WIDTH, D_MODEL)), _const_spec((WIDTH, D_MODEL)),
                  _const_spec((D_MODEL, D_MODEL)), _const_spec((1, D_MODEL))],
        out_specs=[_row_spec(tm, WIDTH), _row_spec(tm, D_MODEL), _row_spec(tm, D_MODEL), _row_spec(tm, D_MODEL),
                   _row_spec(tm, D_MODEL), _row_spec(tm, GATE_COLS), _row_spec(tm, WIDTH), _head_spec(tm, seq),
                   _row_spec(tm, WIDTH), acc(D_MODEL), acc(GATE_COLS)],
        out_shape=[sd((T, WIDTH), BF16), sd((T, D_MODEL), BF16), sd((T, D_MODEL), BF16), sd((T, D_MODEL), BF16),
                   sd((T, D_MODEL), BF16), sd((T, GATE_COLS), BF16), sd((T, WIDTH), BF16),
                   sd((T // seq, HEADS, seq, HEAD_DIM), F32), sd((T, WIDTH), F32), sd((1, D_MODEL), F32),
                   sd((1, GATE_COLS), F32)],
        compiler_params=_params(("arbitrary",)),
    )(dx1, o_sb, rw_out, g_rw, gates, w_sb, w_rw, w_o, g_post)


def _ffn(x1, target, g_pre, g_post, w_gate, w_up, w_down, tm):
    T = x1.shape[0]

    def body(x1_ref, tgt_ref, gpre_ref, gpost_ref, wg_ref, wu_ref, wd_ref,
             loss_o, dx1_o, h_o, dgate_o, dup_o, act_o, df_o, dgpre_o, dgpost_o):
        x1v = x1_ref[...]
        gpre, gpost = gpre_ref[...], gpost_ref[...]
        wg, wu, wd = wg_ref[...], wu_ref[...], wd_ref[...]
        hn, xn1, rstd1 = _rms_fwd(x1v, gpre)
        h = hn.astype(BF16)
        gate = _mm(h, wg)
        up = _mm(h, wu)
        sg = jax.nn.sigmoid(gate)
        act = (gate * sg * up).astype(BF16)
        f = _mm(act, wd)
        fo, fn, rstd2 = _rms_fwd(f, gpost)
        diff = x1v + fo - tgt_ref[...]
        dy = diff * (1.0 / D_MODEL)
        df, dgpost = _rms_bwd(dy, fn, rstd2, gpost)
        dfb = df.astype(BF16)
        dact = _mm_nt(dfb, wd)
        dup = (dact * gate * sg).astype(BF16)
        dgate = (dact * up * (sg * (1.0 + gate * (1.0 - sg)))).astype(BF16)
        dh = _mm_nt(dgate, wg) + _mm_nt(dup, wu)
        dxn, dgpre = _rms_bwd(dh, xn1, rstd1, gpre)
        dx1_o[...] = dy + dxn
        h_o[...] = h
        dgate_o[...] = dgate
        dup_o[...] = dup
        act_o[...] = act
        df_o[...] = dfb
        first = pl.program_id(0) == 0
        part = jnp.sum(jnp.sum(diff * diff, axis=1, keepdims=True), axis=0, keepdims=True) * (0.5 / D_MODEL)
        _acc_out(loss_o, jnp.broadcast_to(part, (8, 128)), first)
        _acc_out(dgpre_o, dgpre, first)
        _acc_out(dgpost_o, dgpost, first)

    acc = lambda r, n: pl.BlockSpec((r, n), lambda i: (0, 0))
    sd = jax.ShapeDtypeStruct
    return pl.pallas_call(
        body, name="ffn", grid=(T // tm,),
        in_specs=[_row_spec(tm, D_MODEL), _row_spec(tm, D_MODEL), _const_spec((1, D_MODEL)), _const_spec((1, D_MODEL)),
                  _const_spec((D_MODEL, D_FF)), _const_spec((D_MODEL, D_FF)), _const_spec((D_FF, D_MODEL))],
        out_specs=[acc(8, 128), _row_spec(tm, D_MODEL), _row_spec(tm, D_MODEL), _row_spec(tm, D_FF), _row_spec(tm, D_FF),
                   _row_spec(tm, D_FF), _row_spec(tm, D_MODEL), acc(1, D_MODEL), acc(1, D_MODEL)],
        out_shape=[sd((8, 128), F32), sd((T, D_MODEL), F32), sd((T, D_MODEL), BF16), sd((T, D_FF), BF16),
                   sd((T, D_FF), BF16), sd((T, D_FF), BF16), sd((T, D_MODEL), BF16), sd((1, D_MODEL), F32),
                   sd((1, D_MODEL), F32)],
        compiler_params=_params(("arbitrary",)),
    )(x1, target, g_pre, g_post, w_gate, w_up, w_down)


def _local_step(x, target, sm, wt, late=None):
    bl, seq, _ = x.shape
    T = bl * seq
    tm = min(ROW_TILE, T)
    x2 = x.reshape(T, D_MODEL)
    tgt2 = target.reshape(T, D_MODEL)
    h, qkv, prw, gates = _in_proj_fwd(x2, sm["norm_mix_pre"], wt["w_in"], sm["b_gate"], tm)
    if late is None:
        o_sb, lsum, sb_stop = _sb_fwd(qkv, bl, seq)
    else:
        o_sb, lsum, sb_stop, gathered = _sb_fwd(qkv, bl, seq, late)
        wt = {**wt, **_unpack_gathered(gathered, slice(_EARLY, None))}
    prep_params = [sm["mu_rw"], sm["w0"], wt["w_up"].astype(F32), sm["a0"], wt["a_up"].astype(F32),
                   wt["g_up"].astype(F32), sm["k_k"], sm["k_a"]]
    prep = _rw_prep_fwd(prw, prep_params, seq, tm)
    by_head = lambda t: t.reshape(bl, HEADS, seq, HEAD_DIM)
    seqs = [t.reshape(bl * HEADS, seq, HEAD_DIM) for t in prep[:6]]
    g_rw = prep[6]
    lnw, lnb, rk = (sm[n].reshape(HEADS, 1, HEAD_DIM) for n in ("lnx_w", "lnx_b", "r_k"))
    rw_out_h, states = _wkv_fwd(seqs, lnw, lnb, rk)
    rw_out = by_head(rw_out_h)
    x1 = _merge_fwd(x2, o_sb, rw_out, g_rw, gates, wt["w_sb_out"], wt["w_rw_out"], wt["w_o"], sm["norm_mix_post"],
                    seq, tm)
    (loss_part, dx1, h2, dffg, dffu, act, dff, d_nfpre, d_nfpost) = _ffn(
        x1, tgt2, sm["norm_ffn_pre"], sm["norm_ffn_post"], wt["w_ffn_gate"], wt["w_ffn_up"], wt["w_ffn_down"],
        min(ROW_TILE_FFN, T))
    (o_rw, merged, dz, da, db, dgate, do_sb, d_rw_out, d_g_rw, d_npost, d_bgate) = _merge_bwd(
        dx1, o_sb, rw_out, g_rw, gates, wt["w_sb_out"], wt["w_rw_out"], wt["w_o"], sm["norm_mix_post"], seq, tm)
    gdt = F32 if late is None else BF16
    gw = {
        "w_sb_out": _grad_w(o_sb, da, "gw_sb_out", gdt), "w_rw_out": _grad_w(o_rw, db, "gw_rw_out", gdt),
        "w_o": _grad_w(merged, dz, "gw_o", gdt),
        "w_ffn_gate": _grad_w(h2, dffg, "gw_ffn_gate", gdt), "w_ffn_up": _grad_w(h2, dffu, "gw_ffn_up", gdt),
        "w_ffn_down": _grad_w(act, dff, "gw_ffn_down", gdt),
    }
    dqkv = jnp.concatenate(_sb_bwd(qkv, do_sb, lsum, sb_stop, bl, seq), axis=1)
    ride = None if late is None else _blocks_by_owner(gw, slice(_EARLY, None))
    wkv_g = _wkv_bwd(seqs, states, d_rw_out.reshape(bl * HEADS, seq, HEAD_DIM), lnw, lnb, rk, ride)
    scattered = {} if late is None else {n: (p, g) for (n, _, _), p, g in zip(_SHARDED[_EARLY:], ride, wkv_g[9:])}
    cts = [by_head(t) for t in wkv_g[:6]] + [d_g_rw]
    prep_g = _rw_prep_bwd(prw, prep_params, cts, seq, tm)
    dprw = prep_g[0]
    d_mu, d_w0, d_wup, d_a0, d_aup, d_gup, d_kk, d_ka = prep_g[1:]
    gw = {
        **gw,
        "w_in": jnp.concatenate([_grad_w(h, dqkv, "gw_in_qkv", gdt), _grad_w(h, dprw, "gw_in_rw", gdt),
                                 _grad_w(h, dgate, "gw_in_gate", gdt)], axis=1),
        "w_up": d_wup.astype(gdt), "a_up": d_aup.astype(gdt), "g_up": d_gup.astype(gdt),
    }
    ride = None if late is None else _blocks_by_owner(gw, slice(0, _EARLY))
    grad_x, d_npre, *got = _in_proj_bwd(x2, sm["norm_mix_pre"], dx1, dqkv, dprw, dgate, wt["w_in"], tm, ride)
    if late is not None:
        scattered.update({n: (p, g) for (n, _, _), p, g in zip(_SHARDED[:_EARLY], ride, got)})
    gs = {
        "norm_mix_pre": d_npre, "b_gate": d_bgate, "mu_rw": d_mu, "w0": d_w0, "a0": d_a0, "k_k": d_kk, "k_a": d_ka,
        "r_k": wkv_g[8].reshape(1, WIDTH), "lnx_w": wkv_g[6].reshape(1, WIDTH), "lnx_b": wkv_g[7].reshape(1, WIDTH),
        "norm_mix_post": d_npost, "norm_ffn_pre": d_nfpre, "norm_ffn_post": d_nfpost,
    }
    return loss_part, grad_x.reshape(x.shape), gw, gs, scattered


_SHARDED = [("w_in", 1, (D_MODEL, (SB_COLS + RW_COLS + GATE_COLS) // N_DEV)), ("w_up", 1, (W_LORA, WIDTH // N_DEV)),
            ("a_up", 1, (A_LORA, WIDTH // N_DEV)), ("g_up", 1, (G_LORA, WIDTH // N_DEV)),
            ("w_sb_out", 1, (WIDTH, D_MODEL // N_DEV)), ("w_rw_out", 1, (WIDTH, D_MODEL // N_DEV)),
            ("w_o", 0, (D_MODEL // N_DEV, D_MODEL)), ("w_ffn_gate", 1, (D_MODEL, D_FF // N_DEV)),
            ("w_ffn_up", 1, (D_MODEL, D_FF // N_DEV)), ("w_ffn_down", 0, (D_FF // N_DEV, D_MODEL))]
_LANES = 128
_PACK_ROWS = [s[0] * s[1] // _LANES for _, _, s in _SHARDED]
_PACK_TOTAL = sum(_PACK_ROWS)
_SMALL = [("norm_mix_pre", D_MODEL), ("b_gate", GATE_COLS), ("mu_rw", RW_COLS), ("w0", WIDTH), ("a0", WIDTH),
          ("k_k", WIDTH), ("k_a", WIDTH), ("r_k", WIDTH), ("lnx_w", WIDTH), ("lnx_b", WIDTH),
          ("norm_mix_post", D_MODEL), ("norm_ffn_pre", D_MODEL), ("norm_ffn_post", D_MODEL)]
_SMALL_ROWS = 96


def _pack_shards(shards, dtype):
    return jnp.concatenate([shards[n].astype(dtype).reshape(-1, _LANES) for n, _, _ in _SHARDED], axis=0)


_EARLY = 4
_EARLY_ROWS = sum(_PACK_ROWS[:_EARLY])


def _unpack_gathered(g, which):
    out, r0 = {}, 0
    for (n, axis, shp), rows in zip(_SHARDED[which], _PACK_ROWS[which]):
        blk = g[:, r0:r0 + rows].reshape((N_DEV,) + shp)
        out[n] = blk.reshape(N_DEV * shp[0], shp[1]) if axis == 0 else blk.transpose(1, 0, 2).reshape(shp[0], N_DEV * shp[1])
        r0 += rows
    return out


def _blocks_by_owner(gw, which):
    return [gw[n].reshape((N_DEV,) + shp) if axis == 0 else gw[n].reshape(shp[0], N_DEV, shp[1]).transpose(1, 0, 2)
            for n, axis, shp in _SHARDED[which]]


def _pack_small(vals, extra=None):
    used = sum(sz for _, sz in _SMALL)
    tail = jnp.zeros((1, _SMALL_ROWS * _LANES - used), F32).at[0, 0].set(extra)
    return jnp.concatenate([vals[n].reshape(1, -1) for n, _ in _SMALL] + [tail], axis=1)


_ANY = pl.BlockSpec(memory_space=pl.ANY)


def _all_gather(block):
    rows, lanes = block.shape

    def body(x_ref, out_ref, send_sems, recv_sems, local_sem):
        start, forward, finish = _gather_steps(x_ref, out_ref, send_sems, recv_sems, local_sem)
        start()
        forward()
        finish()

    return pl.pallas_call(
        body, name="all_gather_weights", in_specs=[_ANY], out_specs=_ANY,
        out_shape=jax.ShapeDtypeStruct((N_DEV, rows, lanes), block.dtype), scratch_shapes=_GATHER_SEMS,
    )(block)


def _scatter_steps(pack_refs, got_refs, send_sems, recv_sems):
    x, y, c = lax.axis_index("x"), lax.axis_index("y"), lax.axis_index("c")

    def copies():
        out = []
        for t, (pack_ref, got_ref) in enumerate(zip(pack_refs, got_refs)):
            for k in range(1, N_DEV):
                px, py, pc = x ^ (k >> 2), y ^ ((k >> 1) & 1), c ^ (k & 1)
                sem = (N_DEV - 1) * t + k - 1
                out.append(pltpu.make_async_remote_copy(
                    src_ref=pack_ref.at[4 * px + 2 * py + pc], dst_ref=got_ref.at[k - 1], send_sem=send_sems.at[sem],
                    recv_sem=recv_sems.at[sem], device_id=(px, py, pc), device_id_type=MESH))
        return out

    def start():
        for cp in copies():
            cp.start()

    def finish():
        for cp in copies():
            cp.wait_recv()
        for cp in copies():
            cp.wait_send()

    return start, finish


def _scatter_results(ride):
    n = (N_DEV - 1) * len(ride)
    return ([jax.ShapeDtypeStruct((N_DEV - 1,) + t.shape[1:], t.dtype) for t in ride],
            [pltpu.SemaphoreType.DMA((n,)), pltpu.SemaphoreType.DMA((n,))])


_GATHER_SEMS = [pltpu.SemaphoreType.DMA((7,)), pltpu.SemaphoreType.DMA((7,)), pltpu.SemaphoreType.DMA]


def _gather_steps(x_ref, out_ref, send_sems, recv_sems, local_sem):
    x, y, c = lax.axis_index("x"), lax.axis_index("y"), lax.axis_index("c")
    me, sibling = (x, y, c), (x, y, 1 - c)
    chips = [(1 - x, y), (x, 1 - y), (1 - x, 1 - y)]

    def slot(px, py, pc):
        return out_ref.at[4 * px + 2 * py + pc]

    def copy(k, blk, to, src=None):
        return pltpu.make_async_remote_copy(
            src_ref=slot(*blk) if src is None else src, dst_ref=slot(*blk),
            send_sem=send_sems.at[k], recv_sem=recv_sems.at[k], device_id=to, device_id_type=MESH)

    def first():
        return [copy(0, me, sibling, src=x_ref)] + [copy(1 + j, me, (*chip, c), src=x_ref) for j, chip in enumerate(chips)]

    def passed():
        return [copy(4 + j, (*chip, c), sibling) for j, chip in enumerate(chips)]

    def start():
        pltpu.make_async_copy(x_ref, slot(*me), local_sem).start()
        for cp in first():
            cp.start()

    def forward():
        for j, (chip, cp) in enumerate(zip(chips, passed())):
            copy(1 + j, (*chip, c), me).wait_recv()
            cp.start()

    def finish():
        copy(0, sibling, me).wait_recv()
        for j, chip in enumerate(chips):
            copy(4 + j, (*chip, 1 - c), me).wait_recv()
        for cp in first() + passed():
            cp.wait_send()
        pltpu.make_async_copy(x_ref, slot(*me), local_sem).wait()

    return start, forward, finish


def _gather_small(small):
    def body(small_ref, parts_ref, send_sems, recv_sems, local_sem):
        x, y, c = lax.axis_index("x"), lax.axis_index("y"), lax.axis_index("c")
        me = 4 * x + 2 * y + c
        mine = pltpu.make_async_copy(small_ref, parts_ref.at[me], local_sem)
        mine.start()
        peers = [(x ^ (k >> 2), y ^ ((k >> 1) & 1), c ^ (k & 1)) for k in range(1, N_DEV)]
        sends = [pltpu.make_async_remote_copy(
            src_ref=small_ref, dst_ref=parts_ref.at[me], send_sem=send_sems.at[k], recv_sem=recv_sems.at[k],
            device_id=to, device_id_type=MESH) for k, to in enumerate(peers)]
        for cp in sends:
            cp.start()
        for k, (px, py, pc) in enumerate(peers):
            pltpu.make_async_remote_copy(
                src_ref=small_ref, dst_ref=parts_ref.at[4 * px + 2 * py + pc], send_sem=send_sems.at[k],
                recv_sem=recv_sems.at[k], device_id=(px, py, pc), device_id_type=MESH).wait_recv()
        for cp in sends:
            cp.wait_send()
        mine.wait()

    return pl.pallas_call(
        body, name="gather_small", in_specs=[_ANY], out_specs=_ANY,
        out_shape=jax.ShapeDtypeStruct((N_DEV,) + small.shape, F32),
        scratch_shapes=[pltpu.SemaphoreType.DMA((N_DEV - 1,)), pltpu.SemaphoreType.DMA((N_DEV - 1,)),
                        pltpu.SemaphoreType.DMA],
    )(small)


def _adamw_math(w, g, m, v):
    m = ADAM_B1 * m + (1.0 - ADAM_B1) * g
    v = ADAM_B2 * v + (1.0 - ADAM_B2) * (g * g)
    m_hat = m / (1.0 - ADAM_B1 ** ADAM_STEP)
    v_hat = v / (1.0 - ADAM_B2 ** ADAM_STEP)
    return -ADAM_LR * (m_hat / (jnp.sqrt(v_hat) + ADAM_EPS) + ADAM_WD * w), m, v


def _adamw_scattered(w, m, v, parts, got, me, name):
    rows, cols = w.shape
    tr = 256 if rows % 256 == 0 and rows * cols > 2 ** 19 else rows

    def body(me_ref, w_ref, m_ref, v_ref, own_ref, got_ref, g_o, d_o, m_o, v_o):
        g = own_ref[...].astype(F32)
        for k in range(N_DEV - 1):
            g = g + got_ref[k].astype(F32)
        g_o[...] = g
        d_o[...], m_o[...], v_o[...] = _adamw_math(w_ref[...], g, m_ref[...], v_ref[...])

    spec = pl.BlockSpec((tr, cols), lambda i, me_ref: (i, 0))
    return pl.pallas_call(
        body, name=name,
        grid_spec=pltpu.PrefetchScalarGridSpec(
            num_scalar_prefetch=1, grid=(rows // tr,),
            in_specs=[spec, spec, spec, pl.BlockSpec((None, tr, cols), lambda i, me_ref: (me_ref[0], i, 0)),
                      pl.BlockSpec((N_DEV - 1, tr, cols), lambda i, me_ref: (0, i, 0))],
            out_specs=[spec] * 4),
        out_shape=[jax.ShapeDtypeStruct((rows, cols), F32)] * 4, compiler_params=_params(("parallel",)),
    )(me, w, m, v, parts, got)


def _adamw_small(parts, ws, ms, vs):
    k = len(_SMALL)

    def body(p_ref, *refs):
        w_refs, m_refs, v_refs = refs[:k], refs[k:2 * k], refs[2 * k:3 * k]
        outs = refs[3 * k:]
        g = p_ref[0]
        for d in range(1, N_DEV):
            g = g + p_ref[d]
        o = 0
        for i, (_, n) in enumerate(_SMALL):
            gp = g[:, o:o + n]
            outs[1 + i][...] = gp
            outs[1 + k + i][...], outs[1 + 2 * k + i][...], outs[1 + 3 * k + i][...] = _adamw_math(
                w_refs[i][...], gp, m_refs[i][...], v_refs[i][...])
            o += n
        outs[0][...] = g[:, o:o + _LANES]

    shapes = [jax.ShapeDtypeStruct((1, n), F32) for _, n in _SMALL]
    out = pl.pallas_call(
        body, name="adamw_small", out_shape=[jax.ShapeDtypeStruct((1, _LANES), F32)] + shapes * 4,
        compiler_params=_params(),
    )(parts, *ws, *ms, *vs)
    return out[0][0, 0], out[1:1 + k], out[1 + k:1 + 2 * k], out[1 + 2 * k:1 + 3 * k], out[1 + 3 * k:]


_WEIGHT_NAMES = ['norm_mix_pre', 'w_in', 'b_gate', 'mu_rw', 'w0', 'w_up', 'a0', 'a_up', 'g_up', 'k_k', 'k_a', 'r_k',
                 'lnx_w', 'lnx_b', 'w_sb_out', 'w_rw_out', 'w_o', 'norm_mix_post', 'norm_ffn_pre', 'w_ffn_gate',
                 'w_ffn_up', 'w_ffn_down', 'norm_ffn_post']


def _step(x, target, w, m, v):
    sharded = [n for n, _, _ in _SHARDED]
    sm = {n: w[n].reshape(1, -1) for n, _ in _SMALL}
    own = {n: w[n][0] for n in sharded}
    packed = _pack_shards(own, BF16)
    wt = _unpack_gathered(_all_gather(packed[:_EARLY_ROWS]), slice(0, _EARLY))
    loss_part, grad_x, _, gs, scattered = _local_step(x, target, sm, wt, packed[_EARLY_ROWS:])

    me = 4 * lax.axis_index("x") + 2 * lax.axis_index("y") + lax.axis_index("c")
    me = jnp.reshape(me, (1,)).astype(jnp.int32)
    small_parts = _gather_small(_pack_small(gs, loss_part[0, 0]))
    row = lambda t: [t[n].reshape(1, -1) for n, _ in _SMALL]
    loss, *by_kind = _adamw_small(small_parts, row(w), row(m), row(v))
    g_s, d_s, m_s, v_s = ({n: t[i] for i, (n, _) in enumerate(_SMALL)} for t in by_kind)

    grads, deltas, new_m, new_v = {}, {}, {}, {}
    for n in _WEIGHT_NAMES:
        if n in scattered:
            out = _adamw_scattered(own[n], m[n][0], v[n][0], *scattered[n], me, "adamw_" + n)
            grads[n], deltas[n], new_m[n], new_v[n] = (t.reshape(w[n].shape) for t in out)
        else:
            grads[n], deltas[n], new_m[n], new_v[n] = (t[n].reshape(w[n].shape) for t in (g_s, d_s, m_s, v_s))
    return (loss, grad_x, *[grads[n] for n in _WEIGHT_NAMES], *[deltas[n] for n in _WEIGHT_NAMES],
            *[new_m[n] for n in _WEIGHT_NAMES], *[new_v[n] for n in _WEIGHT_NAMES])


def kernel(x, norm_mix_pre, w_in, b_gate, mu_rw, w0, w_up, a0, a_up, g_up, k_k, k_a, r_k, lnx_w, lnx_b, w_sb_out, w_rw_out, w_o, norm_mix_post, norm_ffn_pre, w_ffn_gate, w_ffn_up, w_ffn_down, norm_ffn_post, loss_target, m_norm_mix_pre, m_w_in, m_b_gate, m_mu_rw, m_w0, m_w_up, m_a0, m_a_up, m_g_up, m_k_k, m_k_a, m_r_k, m_lnx_w, m_lnx_b, m_w_sb_out, m_w_rw_out, m_w_o, m_norm_mix_post, m_norm_ffn_pre, m_w_ffn_gate, m_w_ffn_up, m_w_ffn_down, m_norm_ffn_post, v_norm_mix_pre, v_w_in, v_b_gate, v_mu_rw, v_w0, v_w_up, v_a0, v_a_up, v_g_up, v_k_k, v_k_a, v_r_k, v_lnx_w, v_lnx_b, v_w_sb_out, v_w_rw_out, v_w_o, v_norm_mix_post, v_norm_ffn_pre, v_w_ffn_gate, v_w_ffn_up, v_w_ffn_down, v_norm_ffn_post):
    args = locals()
    w = {n: args[n] for n in _WEIGHT_NAMES}
    m = {n: args["m_" + n] for n in _WEIGHT_NAMES}
    v = {n: args["v_" + n] for n in _WEIGHT_NAMES}
    return _step(x, loss_target, w, m, v)
```

```python
import functools

import jax
import jax.numpy as jnp
from jax import lax
from jax.experimental import pallas as pl
from jax.experimental.pallas import tpu as pltpu

F32 = jnp.float32
BF16 = jnp.bfloat16

D_MODEL = 1024
HEADS = 8
HEAD_DIM = 64
WIDTH = HEADS * HEAD_DIM
W_LORA, A_LORA, G_LORA = 64, 64, 128
SB_COLS = 3 * WIDTH
RW_COLS = 3 * WIDTH + W_LORA + A_LORA + G_LORA
GATE_COLS = 2 * D_MODEL
D_FF = 2816
RMS_EPS = 1e-6
GN_EPS = HEAD_DIM * 1e-5
N_DEV = 8

ADAM_LR, ADAM_B1, ADAM_B2, ADAM_EPS, ADAM_WD, ADAM_STEP = 0.001, 0.9, 0.999, 1e-08, 0.01, 10

ROW_TILE = 512
ROW_TILE_FFN = 256
SCAN_CHUNK = 64
ATT_ALIGN = 128
ATT_WINDOW = 384
ATT_Q = 128
ATT_PAIRS = 2
SB_DEAD = -104.0
SCAN_SEQS_FWD = 4
SCAN_SEQS_BWD = 2
SCAN_PASSES = 1
GW_TILE_K, GW_TILE_N, GW_ROWS = 1408, 2048, 2048
ADAMW_ROWS, ADAMW_WHOLE_BELOW = 256, 2 ** 19
VMEM_LIMIT = 56 * 2 ** 20

MESH = pl.DeviceIdType.MESH


def _params(sem=None, vmem=VMEM_LIMIT):
    kw = dict(vmem_limit_bytes=vmem)
    if sem is not None:
        kw["dimension_semantics"] = sem
    return pltpu.CompilerParams(**kw)


def _const_spec(shape):
    nd = len(shape)
    return pl.BlockSpec(shape, lambda *_: (0,) * nd, pipeline_mode=pl.Buffered(1))


def _row_spec(tm, n):
    return pl.BlockSpec((tm, n), lambda i: (i, 0))


def _mm(a, b):
    return lax.dot_general(a, b, (((1,), (0,)), ((), ())), preferred_element_type=F32)


def _mm_nt(a, b):
    return lax.dot_general(a, b, (((1,), (1,)), ((), ())), preferred_element_type=F32)


def _mm_tn(a, b):
    return lax.dot_general(a, b, (((0,), (0,)), ((), ())), preferred_element_type=F32)


def _softplus(z):
    return jnp.maximum(z, 0.0) + jnp.log1p(jnp.exp(-jnp.abs(z)))


def _rms_fwd(x, gain):
    rstd = lax.rsqrt(jnp.mean(x * x, axis=-1, keepdims=True) + RMS_EPS)
    xn = x * rstd
    return xn * gain, xn, rstd


def _rms_bwd(dy, xn, rstd, gain):
    u = dy * gain
    dx = rstd * (u - xn * jnp.mean(u * xn, axis=-1, keepdims=True))
    return dx, jnp.sum(dy * xn, axis=0, keepdims=True)


def _acc_out(ref, val, first):
    @pl.when(first)
    def _():
        ref[...] = val

    @pl.when(jnp.logical_not(first))
    def _():
        ref[...] += val


_IN_COLS = SB_COLS + RW_COLS + GATE_COLS
_QKV_OF, _RW_OF, _GATE_OF = slice(0, SB_COLS), slice(SB_COLS, SB_COLS + RW_COLS), slice(SB_COLS + RW_COLS, _IN_COLS)


def _in_proj_fwd(x2, g_pre, w_in, b_gate, tm):
    T = x2.shape[0]

    def body(x_ref, g_ref, w_ref, b_ref, h_ref, qkv_ref, prw_ref, gate_ref):
        h = _rms_fwd(x_ref[...], g_ref[...])[0].astype(BF16)
        h_ref[...] = h
        qkv_ref[...] = _mm(h, w_ref[:, _QKV_OF]).astype(BF16)
        prw_ref[...] = _mm(h, w_ref[:, _RW_OF])
        gate_ref[...] = jax.nn.sigmoid(_mm(h, w_ref[:, _GATE_OF]) + b_ref[...])

    return pl.pallas_call(
        body, name="in_proj_fwd", grid=(T // tm,),
        in_specs=[_row_spec(tm, D_MODEL), _const_spec((1, D_MODEL)), _const_spec((D_MODEL, _IN_COLS)),
                  _const_spec((1, GATE_COLS))],
        out_specs=[_row_spec(tm, D_MODEL), _row_spec(tm, SB_COLS), _row_spec(tm, RW_COLS), _row_spec(tm, GATE_COLS)],
        out_shape=[jax.ShapeDtypeStruct((T, D_MODEL), BF16), jax.ShapeDtypeStruct((T, SB_COLS), BF16),
                   jax.ShapeDtypeStruct((T, RW_COLS), F32), jax.ShapeDtypeStruct((T, GATE_COLS), F32)],
        compiler_params=_params(("parallel",)),
    )(x2, g_pre, w_in, b_gate)


def _in_proj_bwd(x2, g_pre, dx1, dqkv, dprw, dgate, w_in, tm, ride=None):
    T = x2.shape[0]
    steps = T // tm
    nt = 0 if ride is None else len(ride)

    def body(x_ref, g_ref, dx1_ref, dq_ref, dr_ref, dg_ref, w_ref, *rest):
        gx_ref, dgain_ref = rest[nt:nt + 2]
        if ride is not None:
            start, finish = _scatter_steps(rest[:nt], rest[nt + 2:2 * nt + 2], rest[-2], rest[-1])
            pl.when(pl.program_id(0) == 0)(start)
        dh = (_mm_nt(dq_ref[...], w_ref[:, _QKV_OF]) + _mm_nt(dr_ref[...], w_ref[:, _RW_OF])
              + _mm_nt(dg_ref[...], w_ref[:, _GATE_OF]))
        gain = g_ref[...]
        _, xn, rstd = _rms_fwd(x_ref[...], gain)
        dx, dgain = _rms_bwd(dh, xn, rstd, gain)
        gx_ref[...] = dx1_ref[...] + dx
        _acc_out(dgain_ref, dgain, pl.program_id(0) == 0)
        if ride is not None:
            pl.when(pl.program_id(0) == steps - 1)(finish)

    in_specs = [_row_spec(tm, D_MODEL), _const_spec((1, D_MODEL)), _row_spec(tm, D_MODEL), _row_spec(tm, SB_COLS),
                _row_spec(tm, RW_COLS), _row_spec(tm, GATE_COLS), _const_spec((D_MODEL, _IN_COLS))]
    out_specs = [_row_spec(tm, D_MODEL), pl.BlockSpec((1, D_MODEL), lambda i: (0, 0))]
    out_shape = [jax.ShapeDtypeStruct((T, D_MODEL), F32), jax.ShapeDtypeStruct((1, D_MODEL), F32)]
    args, scratch = (x2, g_pre, dx1, dqkv, dprw, dgate, w_in), []
    if ride is not None:
        in_specs, out_specs, args = in_specs + [_ANY] * nt, out_specs + [_ANY] * nt, args + tuple(ride)
        got_shapes, scratch = _scatter_results(ride)
        out_shape = out_shape + got_shapes
    return pl.pallas_call(
        body, name="in_proj_bwd", grid=(steps,), in_specs=in_specs, out_specs=out_specs, out_shape=out_shape,
        scratch_shapes=scratch, compiler_params=_params(("arbitrary",)),
    )(*args)


def _pick_tile(n, cap):
    best = None
    for t in range(128, min(n, cap) + 1, 128):
        if n % t == 0:
            best = t
    return n if best is None else best


def _grad_w(a, b, name, dtype=F32):
    T, K = a.shape
    N = b.shape[1]
    tk, tn, tt = _pick_tile(K, GW_TILE_K), _pick_tile(N, GW_TILE_N), min(T, GW_ROWS)
    steps = T // tt

    def body(a_ref, b_ref, o_ref, *acc):
        t = pl.program_id(2)
        part = _mm_tn(a_ref[...], b_ref[...])
        if not acc:
            _acc_out(o_ref, part, t == 0)
        else:
            _acc_out(acc[0], part, t == 0)

            @pl.when(t == steps - 1)
            def _():
                o_ref[...] = acc[0][...].astype(dtype)

    return pl.pallas_call(
        body, name=name, grid=(K // tk, N // tn, steps),
        in_specs=[pl.BlockSpec((tt, tk), lambda i, j, t: (t, i)), pl.BlockSpec((tt, tn), lambda i, j, t: (t, j))],
        out_specs=pl.BlockSpec((tk, tn), lambda i, j, t: (i, j)),
        out_shape=jax.ShapeDtypeStruct((K, N), dtype),
        scratch_shapes=[] if dtype == F32 else [pltpu.VMEM((tk, tn), F32)],
        compiler_params=_params(("parallel", "parallel", "arbitrary")),
    )(a, b)


def _tri(n, kind):
    r = lax.broadcasted_iota(jnp.int32, (n, n), 0)
    c = lax.broadcasted_iota(jnp.int32, (n, n), 1)
    return {"gt": r > c, "le": r <= c, "lt": r < c, "ge": r >= c}[kind]


def _running_sums(x, carry, tri, kb, reverse=False):
    blocks = range(x.shape[1] // kb)
    parts = {}
    for b in (reversed(blocks) if reverse else blocks):
        piece = x[:, b * kb:(b + 1) * kb]
        parts[b] = carry + _mm(piece.astype(BF16), tri)
        carry = carry + jnp.sum(piece, axis=1, keepdims=True)
    return jnp.concatenate([parts[b] for b in blocks], axis=1), carry


def _sb_valid(row0, col0, first, last, qb, kb):
    ahead = lax.broadcasted_iota(jnp.int32, (qb, kb), 1) - lax.broadcasted_iota(jnp.int32, (qb, kb), 0)
    col = lax.broadcasted_iota(jnp.int32, (1, kb), 1)
    return jnp.logical_and(ahead < row0 - col0, jnp.logical_and(col >= first - col0, col < last - col0))


def _sb_softplus(z):
    return jnp.maximum(z, 0.0) + jnp.log(1.0 + jnp.exp(-jnp.abs(z)))


_PAIR = 2 * HEAD_DIM
_PAIRS = WIDTH // _PAIR


def _first_head_lanes():
    return lax.broadcasted_iota(jnp.int32, (1, _PAIR), 1) < HEAD_DIM


def _per_head(t, first_head):
    zero = jnp.zeros_like(t)
    return jnp.where(first_head, t, zero), jnp.where(first_head, zero, t)


def _sb_fwd(qkv, bl, seq, ride=None):
    qb, win, kb = min(ATT_Q, seq), min(ATT_WINDOW, seq), ATT_ALIGN
    nq = seq // qb
    nh, width, groups = 2 * ATT_PAIRS, ATT_PAIRS * _PAIR, _PAIRS // ATT_PAIRS
    pair_of = lambda h: slice((h // 2) * _PAIR, (h // 2 + 1) * _PAIR)

    steps = bl * groups
    pass_on_at = (5 * steps) // 8
    nt = 0 if ride is None else len(ride)

    def body(q_ref, k_ref, v_ref, *rest):
        g = pl.program_id(0) * groups + pl.program_id(1)
        o_ref, l_ref, stop_ref = rest[nt:nt + 3]
        if ride is not None:
            start, forward, finish = _gather_steps(rest[:nt], rest[nt + 3:2 * nt + 3], *rest[2 * nt + 3:])
            pl.when(g == 0)(start)
            pl.when(g == pass_on_at)(forward)
        first_head = _first_head_lanes()
        u_after = _tri(kb, "gt").astype(BF16)

        def qblock(i, _):
            rows = pl.ds(pl.multiple_of(i * qb, qb), qb)
            qs = q_ref[rows, :] * (HEAD_DIM ** -0.5)
            qh = [_per_head(qs[:, pair_of(h)], first_head)[h % 2] for h in range(nh)]

            def live(carry):
                return jnp.logical_and(carry[0] > 0, carry[3] > 0)

            def window(carry):
                hi, accs, cs, _ = carry
                lo = pl.multiple_of(jnp.maximum(hi - win, 0), kb)
                cols = pl.ds(lo, win)
                kv, vv = k_ref[cols, :], v_ref[cols, :]
                valid = _sb_valid(i * qb, lo, lo, hi, qb, win)
                accs, cs = list(accs), list(cs)
                for h in range(nh):
                    z = _mm_nt(qh[h], kv[:, pair_of(h)])
                    sp = _sb_softplus(z)
                    spm = jnp.where(valid, sp, 0.0)
                    after, cs[h] = _running_sums(spm, cs[h], u_after, kb, reverse=True)
                    w = jnp.where(valid, jnp.exp(z - sp - after), 0.0)
                    accs[h] = accs[h] + _mm(w.astype(BF16), vv[:, pair_of(h)])
                alive = functools.reduce(jnp.minimum, [jnp.min(c) for c in cs]) < -SB_DEAD
                return lo, tuple(accs), tuple(cs), alive.astype(jnp.int32)

            zero_acc, zero_c = jnp.zeros((qb, _PAIR), F32), jnp.zeros((qb, 1), F32)
            lo, accs, cs, _ = lax.while_loop(
                live, window, ((i + 1) * qb, (zero_acc,) * nh, (zero_c,) * nh, jnp.int32(1)))
            for pp in range(ATT_PAIRS):
                o_ref[rows, pp * _PAIR:(pp + 1) * _PAIR] = jnp.where(
                    first_head, accs[2 * pp], accs[2 * pp + 1]).astype(BF16)
            for h in range(nh):
                l_ref[h, rows, :] = cs[h]
            stop_ref[g, i] = lo
            return 0

        lax.fori_loop(0, nq, qblock, 0)
        if ride is not None:
            pl.when(g == steps - 1)(finish)

    col = lambda off: pl.BlockSpec((seq, width), lambda b, p: (b, off + p))
    in_specs = [col(0), col(groups), col(2 * groups)]
    out_specs = [col(0), pl.BlockSpec((None, nh, seq, 1), lambda b, p: (b, p, 0, 0)), pl.BlockSpec(memory_space=pltpu.SMEM)]
    out_shape = [jax.ShapeDtypeStruct((bl * seq, WIDTH), BF16), jax.ShapeDtypeStruct((bl, HEADS, seq, 1), F32),
                 jax.ShapeDtypeStruct((bl * groups, nq), jnp.int32)]
    if ride is None:
        return pl.pallas_call(body, name="sb_fwd", grid=(bl, groups), in_specs=in_specs, out_specs=out_specs,
                              out_shape=out_shape, compiler_params=_params(("arbitrary", "arbitrary")))(qkv, qkv, qkv)
    gathered, sems = _gather_results(ride)
    return pl.pallas_call(
        body, name="sb_fwd", grid=(bl, groups), in_specs=in_specs + [_ANY] * nt, out_specs=out_specs + [_ANY] * nt,
        out_shape=out_shape + gathered, scratch_shapes=sems, compiler_params=_params(("arbitrary", "arbitrary")),
    )(qkv, qkv, qkv, *ride)


def _sb_bwd(qkv, do, lsum, stop, bl, seq):
    qb, win, kb = min(ATT_Q, seq), min(ATT_WINDOW, seq), ATT_ALIGN
    nq = seq // qb
    nh, width, groups = 2 * ATT_PAIRS, ATT_PAIRS * _PAIR, _PAIRS // ATT_PAIRS
    pair_of = lambda h: slice((h // 2) * _PAIR, (h // 2 + 1) * _PAIR)

    def body(stop_ref, q_ref, k_ref, v_ref, do_ref, l_ref, dq_ref, dk_ref, dv_ref, dk_acc, dv_acc):
        g = pl.program_id(0) * groups + pl.program_id(1)
        first_head = _first_head_lanes()
        u_incl = _tri(kb, "le").astype(BF16)
        u_excl = _tri(kb, "lt").astype(BF16)
        dk_acc[...] = jnp.zeros_like(dk_acc)
        dv_acc[...] = jnp.zeros_like(dv_acc)

        def qblock(i, _):
            rows = pl.ds(pl.multiple_of(i * qb, qb), qb)
            qv = q_ref[rows, :]
            qs = qv * (HEAD_DIM ** -0.5)
            dob = do_ref[rows, :]
            qh = [_per_head(qs[:, pair_of(h)], first_head)[h % 2] for h in range(nh)]
            doh = [_per_head(dob[:, pair_of(h)], first_head)[h % 2] for h in range(nh)]
            ltot = [l_ref[h, rows, :] for h in range(nh)]

            first = (jnp.clip(stop_ref[g, i], 0, i * qb) // ATT_ALIGN) * ATT_ALIGN

            def window(n, carry):
                dqs, ps, es = (list(t) for t in carry)
                start = first + n * win
                lo = pl.multiple_of(jnp.minimum(start, seq - win), kb)
                cols = pl.ds(lo, win)
                kv, vv = k_ref[cols, :], v_ref[cols, :]
                valid = _sb_valid(i * qb, lo, start, seq, qb, win)
                dks, dvs = [], []
                for h in range(nh):
                    kp, vp = kv[:, pair_of(h)], vv[:, pair_of(h)]
                    z = _mm_nt(qh[h], kp)
                    sp = _sb_softplus(z)
                    spm = jnp.where(valid, sp, 0.0)
                    upto, ps[h] = _running_sums(spm, ps[h], u_incl, kb)
                    w = jnp.where(valid, jnp.exp(z - sp - (ltot[h] - upto)), 0.0)
                    e = _mm_nt(doh[h], vp) * w
                    dlf, es[h] = _running_sums(e, es[h], u_excl, kb)
                    sig = jnp.exp(z - sp)
                    dz = jnp.where(valid, e * (1.0 - sig) - dlf * sig, 0.0) * (HEAD_DIM ** -0.5)
                    dzb = dz.astype(BF16)
                    dvs.append(_mm_tn(w.astype(BF16), dob[:, pair_of(h)]))
                    dks.append(_mm_tn(dzb, qv[:, pair_of(h)]))
                    dqs[h] = dqs[h] + _mm(dzb, kp)
                for pp in range(ATT_PAIRS):
                    lanes = slice(pp * _PAIR, (pp + 1) * _PAIR)
                    dv_acc[cols, lanes] += jnp.where(first_head, dvs[2 * pp], dvs[2 * pp + 1])
                    dk_acc[cols, lanes] += jnp.where(first_head, dks[2 * pp], dks[2 * pp + 1])
                return tuple(dqs), tuple(ps), tuple(es)

            zero_q, zero_c = jnp.zeros((qb, _PAIR), F32), jnp.zeros((qb, 1), F32)
            dqs, _, _ = lax.fori_loop(0, ((i + 1) * qb - first + win - 1) // win, window,
                                      ((zero_q,) * nh, (zero_c,) * nh, (zero_c,) * nh))
            for pp in range(ATT_PAIRS):
                dq_ref[rows, pp * _PAIR:(pp + 1) * _PAIR] = jnp.where(
                    first_head, dqs[2 * pp], dqs[2 * pp + 1]).astype(BF16)
            return 0

        lax.fori_loop(0, nq, qblock, 0)
        dk_ref[...] = dk_acc[...].astype(BF16)
        dv_ref[...] = dv_acc[...].astype(BF16)

    col = lambda off: pl.BlockSpec((seq, width), lambda b, p, stop_ref: (b, off + p))
    return pl.pallas_call(
        body, name="sb_bwd",
        grid_spec=pltpu.PrefetchScalarGridSpec(
            num_scalar_prefetch=1, grid=(bl, groups),
            in_specs=[col(0), col(groups), col(2 * groups), col(0),
                      pl.BlockSpec((None, nh, seq, 1), lambda b, p, stop_ref: (b, p, 0, 0))],
            out_specs=[col(0), col(0), col(0)],
            scratch_shapes=[pltpu.VMEM((seq, width), F32), pltpu.VMEM((seq, width), F32)]),
        out_shape=[jax.ShapeDtypeStruct((bl * seq, WIDTH), BF16)] * 3,
        compiler_params=_params(("parallel", "parallel")),
    )(stop, qkv, qkv, qkv, do, lsum)


@jax.custom_vjp
def _lora_mm(x, w):
    return _mm(x.astype(BF16), w.astype(BF16))


_lora_mm.defvjp(
    lambda x, w: (_mm(x.astype(BF16), w.astype(BF16)), (x, w)),
    lambda res, ct: (_mm_nt(ct.astype(BF16), res[1].astype(BF16)), _mm_tn(res[0].astype(BF16), ct.astype(BF16))))


def _rw_prep_math(p, ps, mu, w0, w_up, a0, a_up, g_up, k_k, k_a):
    pm = p + (ps - p) * mu
    r, k, v = pm[:, :WIDTH], pm[:, WIDTH:2 * WIDTH], pm[:, 2 * WIDTH:3 * WIDTH]
    o = 3 * WIDTH
    xw, xa, xg = pm[:, o:o + W_LORA], pm[:, o + W_LORA:o + W_LORA + A_LORA], pm[:, o + W_LORA + A_LORA:]
    w_raw = w0 + _lora_mm(jnp.tanh(xw), w_up)
    lw = -jnp.exp(-_softplus(-w_raw) - 0.5)
    a = jax.nn.sigmoid(a0 + _lora_mm(xa, a_up))
    g = _lora_mm(jax.nn.sigmoid(xg), g_up)
    kk = k * k_k
    k2 = k * (1.0 + (a - 1.0) * k_a)
    return r, lw, k2, v, kk, a, g


def _shift_down(p, first_row):
    row = lax.broadcasted_iota(jnp.int32, p.shape, 0)
    return jnp.where(row == 0, first_row, pltpu.roll(p, 1, 0))


def _shift_up(p, last_row):
    row = lax.broadcasted_iota(jnp.int32, p.shape, 0)
    return jnp.where(row == p.shape[0] - 1, last_row, pltpu.roll(p, p.shape[0] - 1, 0))


_PREP_PARAM_SHAPES = [(1, RW_COLS), (1, WIDTH), (W_LORA, WIDTH), (1, WIDTH), (A_LORA, WIDTH), (G_LORA, WIDTH),
                      (1, WIDTH), (1, WIDTH)]


def _prev_rows_spec(tm):
    return pl.BlockSpec((8, RW_COLS), lambda i: (jnp.maximum(i * (tm // 8) - 1, 0), 0))


def _head_spec(tm, seq, tile_of=lambda i: i):
    per_seq = seq // tm
    return pl.BlockSpec((None, HEADS, tm, HEAD_DIM),
                        lambda i: (tile_of(i) // per_seq, 0, tile_of(i) % per_seq, 0))


def _split_heads(val, ref):
    for h in range(HEADS):
        ref[h] = val[:, h * HEAD_DIM:(h + 1) * HEAD_DIM]


def _join_heads(ref):
    return jnp.concatenate([ref[h] for h in range(HEADS)], axis=1)


def _rw_prep_fwd(prw, params, seq, tm):
    T = prw.shape[0]

    def body(p_ref, prev_ref, *rest):
        prm = [r_[...] for r_ in rest[:8]]
        outs = rest[8:]
        i = pl.program_id(0)
        first = jnp.where((i * tm) % seq == 0, 0.0, prev_ref[7:8, :])
        p = p_ref[...]
        vals = _rw_prep_math(p, _shift_down(p, first), *prm)
        for o_ref, val in zip(outs[:6], vals[:6]):
            _split_heads(val, o_ref)
        outs[6][...] = vals[6]

    by_head = jax.ShapeDtypeStruct((T // seq, HEADS, seq, HEAD_DIM), F32)
    return pl.pallas_call(
        body, name="rw_prep_fwd", grid=(T // tm,),
        in_specs=[_row_spec(tm, RW_COLS), _prev_rows_spec(tm)] + [_const_spec(s) for s in _PREP_PARAM_SHAPES],
        out_specs=[_head_spec(tm, seq)] * 6 + [_row_spec(tm, WIDTH)],
        out_shape=[by_head] * 6 + [jax.ShapeDtypeStruct((T, WIDTH), F32)],
        compiler_params=_params(("parallel",)),
    )(prw, prw, *params)


def _rw_prep_bwd(prw, params, cts, seq, tm):
    T = prw.shape[0]
    n = T // tm

    def body(p_ref, prev_ref, *rest):
        prm = [r_[...] for r_ in rest[:8]]
        ct = tuple(_join_heads(r_) for r_ in rest[8:14]) + (rest[14][...],)
        dp_ref = rest[15]
        dprm_refs = rest[16:24]
        carry = rest[24]
        step = pl.program_id(0)
        i = n - 1 - step
        first = jnp.where((i * tm) % seq == 0, 0.0, prev_ref[7:8, :])
        p = p_ref[...]
        _, vjp = jax.vjp(_rw_prep_math, p, _shift_down(p, first), *prm)
        grads = vjp(ct)
        dp, dps = grads[0], grads[1]
        nxt = jnp.where(jnp.logical_or(step == 0, ((i + 1) * tm) % seq == 0), 0.0, carry[0:1, :])
        dp_ref[...] = (dp + _shift_up(dps, nxt)).astype(BF16)
        carry[...] = dps[0:8, :]
        for ref, gval in zip(dprm_refs, grads[2:]):
            _acc_out(ref, gval, step == 0)

    rev = lambda w: pl.BlockSpec((tm, w), lambda s: (n - 1 - s, 0))
    prev = pl.BlockSpec((8, RW_COLS), lambda s: (jnp.maximum((n - 1 - s) * (tm // 8) - 1, 0), 0))
    return pl.pallas_call(
        body, name="rw_prep_bwd", grid=(n,),
        in_specs=([rev(RW_COLS), prev] + [_const_spec(s) for s in _PREP_PARAM_SHAPES]
                  + [_head_spec(tm, seq, lambda s: n - 1 - s)] * 6 + [rev(WIDTH)]),
        out_specs=[rev(RW_COLS)] + [pl.BlockSpec(s, lambda s_: (0, 0)) for s in _PREP_PARAM_SHAPES],
        out_shape=[jax.ShapeDtypeStruct((T, RW_COLS), BF16)] + [jax.ShapeDtypeStruct(s, F32) for s in _PREP_PARAM_SHAPES],
        scratch_shapes=[pltpu.VMEM((8, RW_COLS), F32)],
        compiler_params=_params(("arbitrary",)),
    )(prw, prw, *params, *cts)


def _make_bmm(passes):
    def raw(dn, a, b):
        d = lambda x, y: lax.dot_general(x, y, dn, preferred_element_type=F32)
        ah = a.astype(BF16)
        bh = b.astype(BF16)
        if passes == 1:
            return d(ah, bh)
        al = (a - ah.astype(F32)).astype(BF16)
        bl = (b - bh.astype(F32)).astype(BF16)
        return d(ah, bh) + (d(ah, bl) + d(al, bh))

    dn_nn = (((2,), (1,)), ((0,), (0,)))
    dn_nt = (((2,), (2,)), ((0,), (0,)))
    dn_tn = (((1,), (1,)), ((0,), (0,)))

    @jax.custom_vjp
    def nn(a, b):
        return raw(dn_nn, a, b)

    @jax.custom_vjp
    def nt(a, b):
        return raw(dn_nt, a, b)

    @jax.custom_vjp
    def tn(a, b):
        return raw(dn_tn, a, b)

    nn.defvjp(lambda a, b: (raw(dn_nn, a, b), (a, b)), lambda res, ct: (nt(ct, res[1]), tn(res[0], ct)))
    nt.defvjp(lambda a, b: (raw(dn_nt, a, b), (a, b)), lambda res, ct: (nn(ct, res[1]), tn(ct, res[0])))
    tn.defvjp(lambda a, b: (raw(dn_tn, a, b), (a, b)), lambda res, ct: (nt(res[1], ct), nn(res[0], ct)))

    def unit_lower_inverse(m):
        n = m.shape[-1]
        row = lax.broadcasted_iota(jnp.int32, (n, n), 0)
        col = lax.broadcasted_iota(jnp.int32, (n, n), 1)
        m16 = ((row // 16) == (col // 16)).astype(F32)
        m32 = ((row // 32) == (col // 32)).astype(F32)
        a1 = m * m16
        a2 = nn(a1, a1)
        a4 = nn(a2, a2)
        a8 = nn(a4, a4)
        inv = (row == col).astype(F32) - a1
        inv = inv + nn(inv, a2)
        inv = inv + nn(inv, a4)
        inv = inv + nn(inv, a8)
        inv = inv - nn(nn(inv, m * (m32 - m16)), inv)
        return inv - nn(nn(inv, m * (1.0 - m32)), inv)

    @jax.custom_vjp
    def inverse(m):
        return unit_lower_inverse(m)

    def inverse_fwd(m):
        inv = unit_lower_inverse(m)
        return inv, inv

    inverse.defvjp(inverse_fwd, lambda inv, ct: (-nt(tn(inv, ct), inv),))
    return nn, nt, tn, inverse


def _wkv_chunk(s0, r, lw, k, v, kk, a, lnw, lnb, rk):
    nn, nt, tn, inverse = _make_bmm(SCAN_PASSES)
    G, L, N = r.shape
    rep = lambda t: jnp.broadcast_to(t[None], (G // HEADS, HEADS, 1, N)).reshape(G, 1, N)
    kap = kk * lax.rsqrt(jnp.maximum(jnp.sum(kk * kk, axis=-1, keepdims=True), 1e-24))
    b = a * kap
    row = lax.broadcasted_iota(jnp.int32, (L, L), 0)
    col = lax.broadcasted_iota(jnp.int32, (L, L), 1)
    low_incl = (col <= row).astype(F32)
    low_strict = (col < row).astype(F32)
    c = _make_bmm(3)[0](jnp.broadcast_to(low_incl[None], (G, L, L)), lw)
    c_all = jnp.sum(lw, axis=1, keepdims=True)
    g_inv = jnp.exp(-c)
    kap_t = kap * jnp.exp(c - lw)
    b_t = b * g_inv
    k_t = k * g_inv
    r_t = r * jnp.exp(c)
    g_all = jnp.exp(c_all)
    m_b = nt(kap_t, b_t) * low_strict
    m_k = nt(kap_t, k_t) * low_strict
    n_b = nt(r_t, b_t) * low_incl
    n_k = nt(r_t, k_t) * low_incl
    rhs = -(nt(kap_t, s0) + nn(m_k, v))
    sa = nn(inverse(m_b), rhs)
    y = nt(r_t, s0) + nn(n_b, sa) + nn(n_k, v)
    s1 = s0 * g_all + tn(sa, b_t * g_all) + tn(v, k_t * g_all)
    mean = jnp.mean(y, axis=-1, keepdims=True)
    yc = y - mean
    var = jnp.mean(yc * yc, axis=-1, keepdims=True)
    out = yc * lax.rsqrt(var + GN_EPS) * rep(lnw) + rep(lnb)
    out = out + jnp.sum(r * k * rep(rk), axis=-1, keepdims=True) * v
    return out, s1


def _scan_heads_per_step(total_heads, seqs_wanted):
    n_seq = total_heads // HEADS
    return HEADS * max(d for d in range(1, seqs_wanted + 1) if n_seq % d == 0)


def _wkv_fwd(seqs, lnw, lnb, rk):
    G, S, N = seqs[0].shape
    L = SCAN_CHUNK
    nc = S // L

    def body(*refs):
        ins = [r_[...] for r_ in refs[:6]]
        prm = [r_[...] for r_ in refs[6:9]]
        out_ref, st_ref, state = refs[9], refs[10], refs[11]

        @pl.when(pl.program_id(1) == 0)
        def _():
            state[...] = jnp.zeros_like(state)

        s0 = state[...]
        st_ref[...] = s0
        out, s1 = _wkv_chunk(s0, *ins, *prm)
        out_ref[...] = out
        state[...] = s1

    gb = _scan_heads_per_step(G, SCAN_SEQS_FWD)
    blk = pl.BlockSpec((gb, L, N), lambda b, i: (b, i, 0))
    pspec = _const_spec((HEADS, 1, N))
    return pl.pallas_call(
        body, name="wkv_fwd", grid=(G // gb, nc), in_specs=[blk] * 6 + [pspec] * 3,
        out_specs=[blk, pl.BlockSpec((None, gb, N, N), lambda b, i: (i, b, 0, 0))],
        out_shape=[jax.ShapeDtypeStruct((G, S, N), F32), jax.ShapeDtypeStruct((nc, G, N, N), F32)],
        scratch_shapes=[pltpu.VMEM((gb, N, N), F32)],
        compiler_params=_params(("parallel", "arbitrary")),
    )(*seqs, lnw, lnb, rk)


def _wkv_bwd(seqs, states, dout, lnw, lnb, rk, ride=None):
    G, S, N = seqs[0].shape
    L = SCAN_CHUNK
    nc = S // L
    gb = _scan_heads_per_step(G, SCAN_SEQS_BWD)
    nt = 0 if ride is None else len(ride)

    def body(*refs):
        ins = [r_[...] for r_ in refs[:6]]
        s0 = refs[6][...]
        ct_out = refs[7][...]
        prm = [r_[...] for r_ in refs[8:11]]
        refs = refs[11:]
        if ride is not None:
            start, finish = _scatter_steps(refs[:nt], refs[nt + 9:2 * nt + 9], refs[-2], refs[-1])
            pl.when(jnp.logical_and(pl.program_id(0) == 0, pl.program_id(1) == 0))(start)
            refs = refs[nt:]
        d_refs = refs[0:6]
        dprm_refs = refs[6:9]
        dstate = refs[9 + nt]
        step = pl.program_id(1)

        @pl.when(step == 0)
        def _():
            dstate[...] = jnp.zeros_like(dstate)

        _, vjp = jax.vjp(_wkv_chunk, s0, *ins, *prm)
        grads = vjp((ct_out, dstate[...]))
        dstate[...] = grads[0]
        for ref, gval in zip(d_refs, grads[1:7]):
            ref[...] = gval
        for ref, gval in zip(dprm_refs, grads[7:]):
            _acc_out(ref, gval, jnp.logical_and(step == 0, pl.program_id(0) == 0))
        if ride is not None:
            pl.when(jnp.logical_and(pl.program_id(0) == G // gb - 1, step == nc - 1))(finish)

    blk = pl.BlockSpec((gb, L, N), lambda b, s: (b, nc - 1 - s, 0))
    pspec = _const_spec((HEADS, 1, N))
    pout = pl.BlockSpec((HEADS, 1, N), lambda b, s: (0, 0, 0))
    in_specs = [blk] * 6 + [pl.BlockSpec((None, gb, N, N), lambda b, s: (nc - 1 - s, b, 0, 0)), blk] + [pspec] * 3
    out_specs = [blk] * 6 + [pout] * 3
    out_shape = [jax.ShapeDtypeStruct((G, S, N), F32)] * 6 + [jax.ShapeDtypeStruct((HEADS, 1, N), F32)] * 3
    scratch = [pltpu.VMEM((gb, N, N), F32)]
    args = (*seqs, states, dout, lnw, lnb, rk)
    if ride is not None:
        in_specs, out_specs, args = in_specs + [_ANY] * nt, out_specs + [_ANY] * nt, args + tuple(ride)
        got_shapes, sems = _scatter_results(ride)
        out_shape, scratch = out_shape + got_shapes, scratch + sems
    return pl.pallas_call(
        body, name="wkv_bwd", grid=(G // gb, nc), in_specs=in_specs, out_specs=out_specs, out_shape=out_shape,
        scratch_shapes=scratch, compiler_params=_params(("arbitrary", "arbitrary")),
    )(*args)


def _merge_math(o_sb, rw_out, g_rw, gates, w_sb, w_rw, w_o):
    o_rw = (rw_out * g_rw).astype(BF16)
    a = _mm(o_sb, w_sb)
    b = _mm(o_rw, w_rw)
    g1, g2 = gates[:, :D_MODEL], gates[:, D_MODEL:]
    merged = (g1 * a + g2 * b).astype(BF16)
    return o_rw, a, b, g1, g2, merged, _mm(merged, w_o)


def _merge_fwd(x2, o_sb, rw_out, g_rw, gates, w_sb, w_rw, w_o, g_post, seq, tm):
    T = x2.shape[0]

    def body(x_ref, osb_ref, rw_ref, g_ref, gate_ref, wsb_ref, wrw_ref, wo_ref, gp_ref, x1_ref):
        z = _merge_math(osb_ref[...], _join_heads(rw_ref), g_ref[...], gate_ref[...], wsb_ref[...], wrw_ref[...], wo_ref[...])[-1]
        x1_ref[...] = x_ref[...] + _rms_fwd(z, gp_ref[...])[0]

    return pl.pallas_call(
        body, name="merge_fwd", grid=(T // tm,),
        in_specs=[_row_spec(tm, D_MODEL), _row_spec(tm, WIDTH), _head_spec(tm, seq), _row_spec(tm, WIDTH),
                  _row_spec(tm, GATE_COLS), _const_spec((WIDTH, D_MODEL)), _const_spec((WIDTH, D_MODEL)),
                  _const_spec((D_MODEL, D_MODEL)), _const_spec((1, D_MODEL))],
        out_specs=_row_spec(tm, D_MODEL),
        out_shape=jax.ShapeDtypeStruct((T, D_MODEL), F32),
        compiler_params=_params(("parallel",)),
    )(x2, o_sb, rw_out, g_rw, gates, w_sb, w_rw, w_o, g_post)


def _merge_bwd(dx1, o_sb, rw_out, g_rw, gates, w_sb, w_rw, w_o, g_post, seq, tm):
    T = dx1.shape[0]

    def body(dx1_ref, osb_ref, rw_ref, g_ref, gate_ref, wsb_ref, wrw_ref, wo_ref, gp_ref,
             orw_o, mrg_o, dz_o, da_o, db_o, dgate_o, dosb_o, drw_o, dg_o, dgp_o, dbg_o):
        rw_out_v, g_rw_v = _join_heads(rw_ref), g_ref[...]
        w_sb_v, w_rw_v, w_o_v = wsb_ref[...], wrw_ref[...], wo_ref[...]
        o_rw, a, b, g1, g2, merged, z = _merge_math(osb_ref[...], rw_out_v, g_rw_v, gate_ref[...], w_sb_v, w_rw_v, w_o_v)
        gain = gp_ref[...]
        _, zn, rstd = _rms_fwd(z, gain)
        dz, dgain = _rms_bwd(dx1_ref[...], zn, rstd, gain)
        dzb = dz.astype(BF16)
        dm = _mm_nt(dzb, w_o_v)
        dab = (dm * g1).astype(BF16)
        dbb = (dm * g2).astype(BF16)
        dgate = jnp.concatenate([dm * a * g1 * (1.0 - g1), dm * b * g2 * (1.0 - g2)], axis=1)
        do_rw = _mm_nt(dbb, w_rw_v)
        orw_o[...] = o_rw
        mrg_o[...] = merged
        dz_o[...] = dzb
        da_o[...] = dab
        db_o[...] = dbb
        dgate_o[...] = dgate.astype(BF16)
        dosb_o[...] = _mm_nt(dab, w_sb_v).astype(BF16)
        _split_heads(do_rw * g_rw_v, drw_o)
        dg_o[...] = do_rw * rw_out_v
        first = pl.program_id(0) == 0
        _acc_out(dgp_o, dgain, first)
        _acc_out(dbg_o, jnp.sum(dgate, axis=0, keepdims=True), first)

    acc = lambda n: pl.BlockSpec((1, n), lambda i: (0, 0))
    sd = jax.ShapeDtypeStruct
    return pl.pallas_call(
        body, name="merge_bwd", grid=(T // tm,),
        in_specs=[_row_spec(tm, D_MODEL), _row_spec(tm, WIDTH), _head_spec(tm, seq), _row_spec(tm, WIDTH),
                  _row_spec(tm, GATE_COLS), _const_spec((WIDTH, D_MODEL)), _const_spec((WIDTH, D_MODEL)),
                  _const_spec((D_MODEL, D_MODEL)), _const_spec((1, D_MODEL))],
        out_specs=[_row_spec(tm, WIDTH), _row_spec(tm, D_MODEL), _row_spec(tm, D_MODEL), _row_spec(tm, D_MODEL),
                   _row_spec(tm, D_MODEL), _row_spec(tm, GATE_COLS), _row_spec(tm, WIDTH), _head_spec(tm, seq),
                   _row_spec(tm, WIDTH), acc(D_MODEL), acc(GATE_COLS)],
        out_shape=[sd((T, WIDTH), BF16), sd((T, D_MODEL), BF16), sd((T, D_MODEL), BF16), sd((T, D_MODEL), BF16),
                   sd((T, D_MODEL), BF16), sd((T, GATE_COLS), BF16), sd((T, WIDTH), BF16),
                   sd((T // seq, HEADS, seq, HEAD_DIM), F32), sd((T, WIDTH), F32), sd((1, D_MODEL), F32),
                   sd((1, GATE_COLS), F32)],
        compiler_params=_params(("arbitrary",)),
    )(dx1, o_sb, rw_out, g_rw, gates, w_sb, w_rw, w_o, g_post)


def _ffn(x1, target, g_pre, g_post, w_gate, w_up, w_down, tm):
    T = x1.shape[0]

    def body(x1_ref, tgt_ref, gpre_ref, gpost_ref, wg_ref, wu_ref, wd_ref,
             loss_o, dx1_o, h_o, dgate_o, dup_o, act_o, df_o, dgpre_o, dgpost_o):
        x1v = x1_ref[...]
        gpre, gpost = gpre_ref[...], gpost_ref[...]
        wg, wu, wd = wg_ref[...], wu_ref[...], wd_ref[...]
        hn, xn1, rstd1 = _rms_fwd(x1v, gpre)
        h = hn.astype(BF16)
        gate = _mm(h, wg)
        up = _mm(h, wu)
        sg = jax.nn.sigmoid(gate)
        act = (gate * sg * up).astype(BF16)
        f = _mm(act, wd)
        fo, fn, rstd2 = _rms_fwd(f, gpost)
        diff = x1v + fo - tgt_ref[...]
        dy = diff * (1.0 / D_MODEL)
        df, dgpost = _rms_bwd(dy, fn, rstd2, gpost)
        dfb = df.astype(BF16)
        dact = _mm_nt(dfb, wd)
        dup = (dact * gate * sg).astype(BF16)
        dgate = (dact * up * (sg * (1.0 + gate * (1.0 - sg)))).astype(BF16)
        dh = _mm_nt(dgate, wg) + _mm_nt(dup, wu)
        dxn, dgpre = _rms_bwd(dh, xn1, rstd1, gpre)
        dx1_o[...] = dy + dxn
        h_o[...] = h
        dgate_o[...] = dgate
        dup_o[...] = dup
        act_o[...] = act
        df_o[...] = dfb
        first = pl.program_id(0) == 0
        part = jnp.sum(jnp.sum(diff * diff, axis=1, keepdims=True), axis=0, keepdims=True) * (0.5 / D_MODEL)
        _acc_out(loss_o, jnp.broadcast_to(part, (8, 128)), first)
        _acc_out(dgpre_o, dgpre, first)
        _acc_out(dgpost_o, dgpost, first)

    acc = lambda r, n: pl.BlockSpec((r, n), lambda i: (0, 0))
    sd = jax.ShapeDtypeStruct
    return pl.pallas_call(
        body, name="ffn", grid=(T // tm,),
        in_specs=[_row_spec(tm, D_MODEL), _row_spec(tm, D_MODEL), _const_spec((1, D_MODEL)), _const_spec((1, D_MODEL)),
                  _const_spec((D_MODEL, D_FF)), _const_spec((D_MODEL, D_FF)), _const_spec((D_FF, D_MODEL))],
        out_specs=[acc(8, 128), _row_spec(tm, D_MODEL), _row_spec(tm, D_MODEL), _row_spec(tm, D_FF), _row_spec(tm, D_FF),
                   _row_spec(tm, D_FF), _row_spec(tm, D_MODEL), acc(1, D_MODEL), acc(1, D_MODEL)],
        out_shape=[sd((8, 128), F32), sd((T, D_MODEL), F32), sd((T, D_MODEL), BF16), sd((T, D_FF), BF16),
                   sd((T, D_FF), BF16), sd((T, D_FF), BF16), sd((T, D_MODEL), BF16), sd((1, D_MODEL), F32),
                   sd((1, D_MODEL), F32)],
        compiler_params=_params(("arbitrary",)),
    )(x1, target, g_pre, g_post, w_gate, w_up, w_down)


def _local_step(x, target, sm, wt, late=None):
    bl, seq, _ = x.shape
    T = bl * seq
    tm = min(ROW_TILE, T)
    x2 = x.reshape(T, D_MODEL)
    tgt2 = target.reshape(T, D_MODEL)
    h, qkv, prw, gates = _in_proj_fwd(x2, sm["norm_mix_pre"], wt["w_in"], sm["b_gate"], tm)
    if late is None:
        o_sb, lsum, sb_stop = _sb_fwd(qkv, bl, seq)
    else:
        o_sb, lsum, sb_stop, *gathered = _sb_fwd(qkv, bl, seq, late)
        wt = {**wt, **_whole_weights(gathered, slice(_EARLY, None))}
    prep_params = [sm["mu_rw"], sm["w0"], wt["w_up"].astype(F32), sm["a0"], wt["a_up"].astype(F32),
                   wt["g_up"].astype(F32), sm["k_k"], sm["k_a"]]
    prep = _rw_prep_fwd(prw, prep_params, seq, tm)
    by_head = lambda t: t.reshape(bl, HEADS, seq, HEAD_DIM)
    seqs = [t.reshape(bl * HEADS, seq, HEAD_DIM) for t in prep[:6]]
    g_rw = prep[6]
    lnw, lnb, rk = (sm[n].reshape(HEADS, 1, HEAD_DIM) for n in ("lnx_w", "lnx_b", "r_k"))
    rw_out_h, states = _wkv_fwd(seqs, lnw, lnb, rk)
    rw_out = by_head(rw_out_h)
    x1 = _merge_fwd(x2, o_sb, rw_out, g_rw, gates, wt["w_sb_out"], wt["w_rw_out"], wt["w_o"], sm["norm_mix_post"],
                    seq, tm)
    (loss_part, dx1, h2, dffg, dffu, act, dff, d_nfpre, d_nfpost) = _ffn(
        x1, tgt2, sm["norm_ffn_pre"], sm["norm_ffn_post"], wt["w_ffn_gate"], wt["w_ffn_up"], wt["w_ffn_down"],
        min(ROW_TILE_FFN, T))
    (o_rw, merged, dz, da, db, dgate, do_sb, d_rw_out, d_g_rw, d_npost, d_bgate) = _merge_bwd(
        dx1, o_sb, rw_out, g_rw, gates, wt["w_sb_out"], wt["w_rw_out"], wt["w_o"], sm["norm_mix_post"], seq, tm)
    gdt = F32 if late is None else BF16
    gw = {
        "w_sb_out": _grad_w(o_sb, da, "gw_sb_out", gdt), "w_rw_out": _grad_w(o_rw, db, "gw_rw_out", gdt),
        "w_o": _grad_w(merged, dz, "gw_o", gdt),
        "w_ffn_gate": _grad_w(h2, dffg, "gw_ffn_gate", gdt), "w_ffn_up": _grad_w(h2, dffu, "gw_ffn_up", gdt),
        "w_ffn_down": _grad_w(act, dff, "gw_ffn_down", gdt),
    }
    dqkv = jnp.concatenate(_sb_bwd(qkv, do_sb, lsum, sb_stop, bl, seq), axis=1)
    ride = None if late is None else _blocks_by_owner(gw, slice(_EARLY, None))
    wkv_g = _wkv_bwd(seqs, states, d_rw_out.reshape(bl * HEADS, seq, HEAD_DIM), lnw, lnb, rk, ride)
    scattered = {} if late is None else {n: (p, g) for (n, _, _), p, g in zip(_SHARDED[_EARLY:], ride, wkv_g[9:])}
    cts = [by_head(t) for t in wkv_g[:6]] + [d_g_rw]
    prep_g = _rw_prep_bwd(prw, prep_params, cts, seq, tm)
    dprw = prep_g[0]
    d_mu, d_w0, d_wup, d_a0, d_aup, d_gup, d_kk, d_ka = prep_g[1:]
    gw = {
        **gw,
        "w_in": jnp.concatenate([_grad_w(h, dqkv, "gw_in_qkv", gdt), _grad_w(h, dprw, "gw_in_rw", gdt),
                                 _grad_w(h, dgate, "gw_in_gate", gdt)], axis=1),
        "w_up": d_wup.astype(gdt), "a_up": d_aup.astype(gdt), "g_up": d_gup.astype(gdt),
    }
    ride = None if late is None else _blocks_by_owner(gw, slice(0, _EARLY))
    grad_x, d_npre, *got = _in_proj_bwd(x2, sm["norm_mix_pre"], dx1, dqkv, dprw, dgate, wt["w_in"], tm, ride)
    if late is not None:
        scattered.update({n: (p, g) for (n, _, _), p, g in zip(_SHARDED[:_EARLY], ride, got)})
    gs = {
        "norm_mix_pre": d_npre, "b_gate": d_bgate, "mu_rw": d_mu, "w0": d_w0, "a0": d_a0, "k_k": d_kk, "k_a": d_ka,
        "r_k": wkv_g[8].reshape(1, WIDTH), "lnx_w": wkv_g[6].reshape(1, WIDTH), "lnx_b": wkv_g[7].reshape(1, WIDTH),
        "norm_mix_post": d_npost, "norm_ffn_pre": d_nfpre, "norm_ffn_post": d_nfpost,
    }
    return loss_part, grad_x.reshape(x.shape), gw, gs, scattered


_SHARDED = [("w_in", 1, (D_MODEL, (SB_COLS + RW_COLS + GATE_COLS) // N_DEV)), ("w_up", 1, (W_LORA, WIDTH // N_DEV)),
            ("a_up", 1, (A_LORA, WIDTH // N_DEV)), ("g_up", 1, (G_LORA, WIDTH // N_DEV)),
            ("w_sb_out", 1, (WIDTH, D_MODEL // N_DEV)), ("w_rw_out", 1, (WIDTH, D_MODEL // N_DEV)),
            ("w_o", 0, (D_MODEL // N_DEV, D_MODEL)), ("w_ffn_gate", 1, (D_MODEL, D_FF // N_DEV)),
            ("w_ffn_up", 1, (D_MODEL, D_FF // N_DEV)), ("w_ffn_down", 0, (D_FF // N_DEV, D_MODEL))]
_LANES = 128
_SMALL = [("norm_mix_pre", D_MODEL), ("b_gate", GATE_COLS), ("mu_rw", RW_COLS), ("w0", WIDTH), ("a0", WIDTH),
          ("k_k", WIDTH), ("k_a", WIDTH), ("r_k", WIDTH), ("lnx_w", WIDTH), ("lnx_b", WIDTH),
          ("norm_mix_post", D_MODEL), ("norm_ffn_pre", D_MODEL), ("norm_ffn_post", D_MODEL)]
_SMALL_ROWS = 96


_EARLY = 4


def _whole_weights(gathered, which):
    return {n: blk.reshape(N_DEV * shp[0], shp[1]) if axis == 0 else blk.transpose(1, 0, 2).reshape(shp[0], N_DEV * shp[1])
            for (n, axis, shp), blk in zip(_SHARDED[which], gathered)}


def _blocks_by_owner(gw, which):
    return [gw[n].reshape((N_DEV,) + shp) if axis == 0 else gw[n].reshape(shp[0], N_DEV, shp[1]).transpose(1, 0, 2)
            for n, axis, shp in _SHARDED[which]]


def _pack_small(vals, extra=None):
    used = sum(sz for _, sz in _SMALL)
    tail = jnp.zeros((1, _SMALL_ROWS * _LANES - used), F32).at[0, 0].set(extra)
    return jnp.concatenate([vals[n].reshape(1, -1) for n, _ in _SMALL] + [tail], axis=1)


_ANY = pl.BlockSpec(memory_space=pl.ANY)


def _all_gather(blocks):
    n = len(blocks)

    def body(*refs):
        start, forward, finish = _gather_steps(refs[:n], refs[n:2 * n], *refs[2 * n:])
        start()
        forward()
        finish()

    out_shape, sems = _gather_results(blocks)
    return pl.pallas_call(
        body, name="all_gather_weights", in_specs=[_ANY] * n, out_specs=[_ANY] * n, out_shape=out_shape,
        scratch_shapes=sems,
    )(*blocks)


def _scatter_steps(pack_refs, got_refs, send_sems, recv_sems):
    x, y, c = lax.axis_index("x"), lax.axis_index("y"), lax.axis_index("c")

    def copies():
        out = []
        for t, (pack_ref, got_ref) in enumerate(zip(pack_refs, got_refs)):
            for k in range(1, N_DEV):
                px, py, pc = x ^ (k >> 2), y ^ ((k >> 1) & 1), c ^ (k & 1)
                sem = (N_DEV - 1) * t + k - 1
                out.append(pltpu.make_async_remote_copy(
                    src_ref=pack_ref.at[4 * px + 2 * py + pc], dst_ref=got_ref.at[k - 1], send_sem=send_sems.at[sem],
                    recv_sem=recv_sems.at[sem], device_id=(px, py, pc), device_id_type=MESH))
        return out

    def start():
        for cp in copies():
            cp.start()

    def finish():
        for cp in copies():
            cp.wait_recv()
        for cp in copies():
            cp.wait_send()

    return start, finish


def _scatter_results(ride):
    n = (N_DEV - 1) * len(ride)
    return ([jax.ShapeDtypeStruct((N_DEV - 1,) + t.shape[1:], t.dtype) for t in ride],
            [pltpu.SemaphoreType.DMA((n,)), pltpu.SemaphoreType.DMA((n,))])


def _gather_steps(x_refs, out_refs, send_sems, recv_sems, local_sems):
    x, y, c = lax.axis_index("x"), lax.axis_index("y"), lax.axis_index("c")
    me, sibling = (x, y, c), (x, y, 1 - c)
    chips = [(1 - x, y), (x, 1 - y), (1 - x, 1 - y)]
    tensors = range(len(x_refs))

    def slot(t, px, py, pc):
        return out_refs[t].at[4 * px + 2 * py + pc]

    def copy(t, k, blk, to, src=None):
        return pltpu.make_async_remote_copy(
            src_ref=slot(t, *blk) if src is None else src, dst_ref=slot(t, *blk), send_sem=send_sems.at[7 * t + k],
            recv_sem=recv_sems.at[7 * t + k], device_id=to, device_id_type=MESH)

    def mine(t):
        return pltpu.make_async_copy(x_refs[t], slot(t, *me), local_sems.at[t])

    def first():
        return [cp for t in tensors for cp in
                [copy(t, 0, me, sibling, src=x_refs[t])]
                + [copy(t, 1 + j, me, (*chip, c), src=x_refs[t]) for j, chip in enumerate(chips)]]

    def passed(t):
        return [copy(t, 4 + j, (*chip, c), sibling) for j, chip in enumerate(chips)]

    def start():
        for t in tensors:
            mine(t).start()
        for cp in first():
            cp.start()

    def forward():
        for t in tensors:
            for j, (chip, cp) in enumerate(zip(chips, passed(t))):
                copy(t, 1 + j, (*chip, c), me).wait_recv()
                cp.start()

    def finish():
        for t in tensors:
            copy(t, 0, sibling, me).wait_recv()
            for j, chip in enumerate(chips):
                copy(t, 4 + j, (*chip, 1 - c), me).wait_recv()
        for cp in first() + [cp for t in tensors for cp in passed(t)]:
            cp.wait_send()
        for t in tensors:
            mine(t).wait()

    return start, forward, finish


def _gather_results(blocks):
    n = len(blocks)
    return ([jax.ShapeDtypeStruct((N_DEV,) + t.shape, t.dtype) for t in blocks],
            [pltpu.SemaphoreType.DMA((7 * n,)), pltpu.SemaphoreType.DMA((7 * n,)), pltpu.SemaphoreType.DMA((n,))])


def _gather_small(small):
    def body(small_ref, parts_ref, send_sems, recv_sems, local_sem):
        x, y, c = lax.axis_index("x"), lax.axis_index("y"), lax.axis_index("c")
        me = 4 * x + 2 * y + c
        mine = pltpu.make_async_copy(small_ref, parts_ref.at[me], local_sem)
        mine.start()
        peers = [(x ^ (k >> 2), y ^ ((k >> 1) & 1), c ^ (k & 1)) for k in range(1, N_DEV)]
        sends = [pltpu.make_async_remote_copy(
            src_ref=small_ref, dst_ref=parts_ref.at[me], send_sem=send_sems.at[k], recv_sem=recv_sems.at[k],
            device_id=to, device_id_type=MESH) for k, to in enumerate(peers)]
        for cp in sends:
            cp.start()
        for k, (px, py, pc) in enumerate(peers):
            pltpu.make_async_remote_copy(
                src_ref=small_ref, dst_ref=parts_ref.at[4 * px + 2 * py + pc], send_sem=send_sems.at[k],
                recv_sem=recv_sems.at[k], device_id=(px, py, pc), device_id_type=MESH).wait_recv()
        for cp in sends:
            cp.wait_send()
        mine.wait()

    return pl.pallas_call(
        body, name="gather_small", in_specs=[_ANY], out_specs=_ANY,
        out_shape=jax.ShapeDtypeStruct((N_DEV,) + small.shape, F32),
        scratch_shapes=[pltpu.SemaphoreType.DMA((N_DEV - 1,)), pltpu.SemaphoreType.DMA((N_DEV - 1,)),
                        pltpu.SemaphoreType.DMA],
    )(small)


def _adamw_math(w, g, m, v):
    m = ADAM_B1 * m + (1.0 - ADAM_B1) * g
    v = ADAM_B2 * v + (1.0 - ADAM_B2) * (g * g)
    m_hat = m / (1.0 - ADAM_B1 ** ADAM_STEP)
    v_hat = v / (1.0 - ADAM_B2 ** ADAM_STEP)
    return -ADAM_LR * (m_hat / (jnp.sqrt(v_hat) + ADAM_EPS) + ADAM_WD * w), m, v


def _adamw_scattered(w, m, v, parts, got, me, name):
    rows, cols = w.shape
    tr = ADAMW_ROWS if rows % ADAMW_ROWS == 0 and rows * cols > ADAMW_WHOLE_BELOW else rows

    def body(me_ref, w_ref, m_ref, v_ref, own_ref, got_ref, g_o, d_o, m_o, v_o):
        g = own_ref[...].astype(F32)
        for k in range(N_DEV - 1):
            g = g + got_ref[k].astype(F32)
        g_o[...] = g
        d_o[...], m_o[...], v_o[...] = _adamw_math(w_ref[...], g, m_ref[...], v_ref[...])

    spec = pl.BlockSpec((tr, cols), lambda i, me_ref: (i, 0))
    return pl.pallas_call(
        body, name=name,
        grid_spec=pltpu.PrefetchScalarGridSpec(
            num_scalar_prefetch=1, grid=(rows // tr,),
            in_specs=[spec, spec, spec, pl.BlockSpec((None, tr, cols), lambda i, me_ref: (me_ref[0], i, 0)),
                      pl.BlockSpec((N_DEV - 1, tr, cols), lambda i, me_ref: (0, i, 0))],
            out_specs=[spec] * 4),
        out_shape=[jax.ShapeDtypeStruct((rows, cols), F32)] * 4, compiler_params=_params(("parallel",)),
    )(me, w, m, v, parts, got)


def _adamw_small(parts, ws, ms, vs):
    k = len(_SMALL)

    def body(p_ref, *refs):
        w_refs, m_refs, v_refs = refs[:k], refs[k:2 * k], refs[2 * k:3 * k]
        outs = refs[3 * k:]
        g = p_ref[0]
        for d in range(1, N_DEV):
            g = g + p_ref[d]
        o = 0
        for i, (_, n) in enumerate(_SMALL):
            gp = g[:, o:o + n]
            outs[1 + i][...] = gp
            outs[1 + k + i][...], outs[1 + 2 * k + i][...], outs[1 + 3 * k + i][...] = _adamw_math(
                w_refs[i][...], gp, m_refs[i][...], v_refs[i][...])
            o += n
        outs[0][...] = g[:, o:o + _LANES]

    shapes = [jax.ShapeDtypeStruct((1, n), F32) for _, n in _SMALL]
    out = pl.pallas_call(
        body, name="adamw_small", out_shape=[jax.ShapeDtypeStruct((1, _LANES), F32)] + shapes * 4,
        compiler_params=_params(),
    )(parts, *ws, *ms, *vs)
    return out[0][0, 0], out[1:1 + k], out[1 + k:1 + 2 * k], out[1 + 2 * k:1 + 3 * k], out[1 + 3 * k:]


_WEIGHT_NAMES = ['norm_mix_pre', 'w_in', 'b_gate', 'mu_rw', 'w0', 'w_up', 'a0', 'a_up', 'g_up', 'k_k', 'k_a', 'r_k',
                 'lnx_w', 'lnx_b', 'w_sb_out', 'w_rw_out', 'w_o', 'norm_mix_post', 'norm_ffn_pre', 'w_ffn_gate',
                 'w_ffn_up', 'w_ffn_down', 'norm_ffn_post']


def _step(x, target, w, m, v):
    sharded = [n for n, _, _ in _SHARDED]
    sm = {n: w[n].reshape(1, -1) for n, _ in _SMALL}
    own = {n: w[n][0] for n in sharded}
    mine = [own[n].astype(BF16) for n in sharded]
    wt = _whole_weights(_all_gather(mine[:_EARLY]), slice(0, _EARLY))
    loss_part, grad_x, _, gs, scattered = _local_step(x, target, sm, wt, mine[_EARLY:])

    me = 4 * lax.axis_index("x") + 2 * lax.axis_index("y") + lax.axis_index("c")
    me = jnp.reshape(me, (1,)).astype(jnp.int32)
    small_parts = _gather_small(_pack_small(gs, loss_part[0, 0]))
    row = lambda t: [t[n].reshape(1, -1) for n, _ in _SMALL]
    loss, *by_kind = _adamw_small(small_parts, row(w), row(m), row(v))
    g_s, d_s, m_s, v_s = ({n: t[i] for i, (n, _) in enumerate(_SMALL)} for t in by_kind)

    grads, deltas, new_m, new_v = {}, {}, {}, {}
    for n in _WEIGHT_NAMES:
        if n in scattered:
            out = _adamw_scattered(own[n], m[n][0], v[n][0], *scattered[n], me, "adamw_" + n)
            grads[n], deltas[n], new_m[n], new_v[n] = (t.reshape(w[n].shape) for t in out)
        else:
            grads[n], deltas[n], new_m[n], new_v[n] = (t[n].reshape(w[n].shape) for t in (g_s, d_s, m_s, v_s))
    return (loss, grad_x, *[grads[n] for n in _WEIGHT_NAMES], *[deltas[n] for n in _WEIGHT_NAMES],
            *[new_m[n] for n in _WEIGHT_NAMES], *[new_v[n] for n in _WEIGHT_NAMES])


def kernel(x, norm_mix_pre, w_in, b_gate, mu_rw, w0, w_up, a0, a_up, g_up, k_k, k_a, r_k, lnx_w, lnx_b, w_sb_out, w_rw_out, w_o, norm_mix_post, norm_ffn_pre, w_ffn_gate, w_ffn_up, w_ffn_down, norm_ffn_post, loss_target, m_norm_mix_pre, m_w_in, m_b_gate, m_mu_rw, m_w0, m_w_up, m_a0, m_a_up, m_g_up, m_k_k, m_k_a, m_r_k, m_lnx_w, m_lnx_b, m_w_sb_out, m_w_rw_out, m_w_o, m_norm_mix_post, m_norm_ffn_pre, m_w_ffn_gate, m_w_ffn_up, m_w_ffn_down, m_norm_ffn_post, v_norm_mix_pre, v_w_in, v_b_gate, v_mu_rw, v_w0, v_w_up, v_a0, v_a_up, v_g_up, v_k_k, v_k_a, v_r_k, v_lnx_w, v_lnx_b, v_w_sb_out, v_w_rw_out, v_w_o, v_norm_mix_post, v_norm_ffn_pre, v_w_ffn_gate, v_w_ffn_up, v_w_ffn_down, v_norm_ffn_post):
    args = locals()
    w = {n: args[n] for n in _WEIGHT_NAMES}
    m = {n: args["m_" + n] for n in _WEIGHT_NAMES}
    v = {n: args["v_" + n] for n in _WEIGHT_NAMES}
    return _step(x, loss_target, w, m, v)
```

```python
import functools

import jax
import jax.numpy as jnp
from jax import lax
from jax.experimental import pallas as pl
from jax.experimental.pallas import tpu as pltpu

F32 = jnp.float32
BF16 = jnp.bfloat16

D_MODEL = 1024
HEADS = 8
HEAD_DIM = 64
WIDTH = HEADS * HEAD_DIM
W_LORA, A_LORA, G_LORA = 64, 64, 128
SB_COLS = 3 * WIDTH
RW_COLS = 3 * WIDTH + W_LORA + A_LORA + G_LORA
GATE_COLS = 2 * D_MODEL
D_FF = 2816
RMS_EPS = 1e-6
GN_EPS = HEAD_DIM * 1e-5
N_DEV = 8

ADAM_LR, ADAM_B1, ADAM_B2, ADAM_EPS, ADAM_WD, ADAM_STEP = 0.001, 0.9, 0.999, 1e-08, 0.01, 10

ROW_TILE = 512
ROW_TILE_FFN = 256
SCAN_CHUNK = 64
ATT_ALIGN = 128
ATT_WINDOW = 512
ATT_Q = 256
ATT_PAIRS = 2
SB_DEAD = -104.0
SCAN_SEQS_FWD = 4
SCAN_SEQS_BWD = 2
SCAN_PASSES = 1
GW_TILE_K, GW_TILE_N, GW_ROWS = 1408, 2048, 2048
ADAMW_ROWS, ADAMW_WHOLE_BELOW = 256, 2 ** 19
VMEM_LIMIT = 56 * 2 ** 20

MESH = pl.DeviceIdType.MESH


def _params(sem=None, vmem=VMEM_LIMIT):
    kw = dict(vmem_limit_bytes=vmem)
    if sem is not None:
        kw["dimension_semantics"] = sem
    return pltpu.CompilerParams(**kw)


def _const_spec(shape):
    nd = len(shape)
    return pl.BlockSpec(shape, lambda *_: (0,) * nd, pipeline_mode=pl.Buffered(1))


def _row_spec(tm, n):
    return pl.BlockSpec((tm, n), lambda i: (i, 0))


def _mm(a, b):
    return lax.dot_general(a, b, (((1,), (0,)), ((), ())), preferred_element_type=F32)


def _mm_nt(a, b):
    return lax.dot_general(a, b, (((1,), (1,)), ((), ())), preferred_element_type=F32)


def _mm_tn(a, b):
    return lax.dot_general(a, b, (((0,), (0,)), ((), ())), preferred_element_type=F32)


def _softplus(z):
    return jnp.maximum(z, 0.0) + jnp.log1p(jnp.exp(-jnp.abs(z)))


def _rms_fwd(x, gain):
    rstd = lax.rsqrt(jnp.mean(x * x, axis=-1, keepdims=True) + RMS_EPS)
    xn = x * rstd
    return xn * gain, xn, rstd


def _rms_bwd(dy, xn, rstd, gain):
    u = dy * gain
    dx = rstd * (u - xn * jnp.mean(u * xn, axis=-1, keepdims=True))
    return dx, jnp.sum(dy * xn, axis=0, keepdims=True)


def _acc_out(ref, val, first):
    @pl.when(first)
    def _():
        ref[...] = val

    @pl.when(jnp.logical_not(first))
    def _():
        ref[...] += val


_IN_COLS = SB_COLS + RW_COLS + GATE_COLS
_QKV_OF, _RW_OF, _GATE_OF = slice(0, SB_COLS), slice(SB_COLS, SB_COLS + RW_COLS), slice(SB_COLS + RW_COLS, _IN_COLS)


def _in_proj_fwd(x2, g_pre, w_in, b_gate, tm):
    T = x2.shape[0]

    def body(x_ref, g_ref, w_ref, b_ref, h_ref, qkv_ref, prw_ref, gate_ref):
        h = _rms_fwd(x_ref[...], g_ref[...])[0].astype(BF16)
        h_ref[...] = h
        qkv_ref[...] = _mm(h, w_ref[:, _QKV_OF]).astype(BF16)
        prw_ref[...] = _mm(h, w_ref[:, _RW_OF])
        gate_ref[...] = jax.nn.sigmoid(_mm(h, w_ref[:, _GATE_OF]) + b_ref[...])

    return pl.pallas_call(
        body, name="in_proj_fwd", grid=(T // tm,),
        in_specs=[_row_spec(tm, D_MODEL), _const_spec((1, D_MODEL)), _const_spec((D_MODEL, _IN_COLS)),
                  _const_spec((1, GATE_COLS))],
        out_specs=[_row_spec(tm, D_MODEL), _row_spec(tm, SB_COLS), _row_spec(tm, RW_COLS), _row_spec(tm, GATE_COLS)],
        out_shape=[jax.ShapeDtypeStruct((T, D_MODEL), BF16), jax.ShapeDtypeStruct((T, SB_COLS), BF16),
                   jax.ShapeDtypeStruct((T, RW_COLS), F32), jax.ShapeDtypeStruct((T, GATE_COLS), F32)],
        compiler_params=_params(("parallel",)),
    )(x2, g_pre, w_in, b_gate)


def _in_proj_bwd(x2, g_pre, dx1, dqkv, dprw, dgate, w_in, tm, ride=None):
    T = x2.shape[0]
    steps = T // tm
    nt = 0 if ride is None else len(ride)

    def body(x_ref, g_ref, dx1_ref, dq_ref, dr_ref, dg_ref, w_ref, *rest):
        gx_ref, dgain_ref = rest[nt:nt + 2]
        if ride is not None:
            start, finish = _scatter_steps(rest[:nt], rest[nt + 2:2 * nt + 2], rest[-2], rest[-1])
            pl.when(pl.program_id(0) == 0)(start)
        dh = (_mm_nt(dq_ref[...], w_ref[:, _QKV_OF]) + _mm_nt(dr_ref[...], w_ref[:, _RW_OF])
              + _mm_nt(dg_ref[...], w_ref[:, _GATE_OF]))
        gain = g_ref[...]
        _, xn, rstd = _rms_fwd(x_ref[...], gain)
        dx, dgain = _rms_bwd(dh, xn, rstd, gain)
        gx_ref[...] = dx1_ref[...] + dx
        _acc_out(dgain_ref, dgain, pl.program_id(0) == 0)
        if ride is not None:
            pl.when(pl.program_id(0) == steps - 1)(finish)

    in_specs = [_row_spec(tm, D_MODEL), _const_spec((1, D_MODEL)), _row_spec(tm, D_MODEL), _row_spec(tm, SB_COLS),
                _row_spec(tm, RW_COLS), _row_spec(tm, GATE_COLS), _const_spec((D_MODEL, _IN_COLS))]
    out_specs = [_row_spec(tm, D_MODEL), pl.BlockSpec((1, D_MODEL), lambda i: (0, 0))]
    out_shape = [jax.ShapeDtypeStruct((T, D_MODEL), F32), jax.ShapeDtypeStruct((1, D_MODEL), F32)]
    args, scratch = (x2, g_pre, dx1, dqkv, dprw, dgate, w_in), []
    if ride is not None:
        in_specs, out_specs, args = in_specs + [_ANY] * nt, out_specs + [_ANY] * nt, args + tuple(ride)
        got_shapes, scratch = _scatter_results(ride)
        out_shape = out_shape + got_shapes
    return pl.pallas_call(
        body, name="in_proj_bwd", grid=(steps,), in_specs=in_specs, out_specs=out_specs, out_shape=out_shape,
        scratch_shapes=scratch, compiler_params=_params(("arbitrary",)),
    )(*args)


def _pick_tile(n, cap):
    best = None
    for t in range(128, min(n, cap) + 1, 128):
        if n % t == 0:
            best = t
    return n if best is None else best


def _grad_w(a, b, name, dtype=F32):
    T, K = a.shape
    N = b.shape[1]
    tk, tn, tt = _pick_tile(K, GW_TILE_K), _pick_tile(N, GW_TILE_N), min(T, GW_ROWS)
    steps = T // tt

    def body(a_ref, b_ref, o_ref, *acc):
        t = pl.program_id(2)
        part = _mm_tn(a_ref[...], b_ref[...])
        if not acc:
            _acc_out(o_ref, part, t == 0)
        else:
            _acc_out(acc[0], part, t == 0)

            @pl.when(t == steps - 1)
            def _():
                o_ref[...] = acc[0][...].astype(dtype)

    return pl.pallas_call(
        body, name=name, grid=(K // tk, N // tn, steps),
        in_specs=[pl.BlockSpec((tt, tk), lambda i, j, t: (t, i)), pl.BlockSpec((tt, tn), lambda i, j, t: (t, j))],
        out_specs=pl.BlockSpec((tk, tn), lambda i, j, t: (i, j)),
        out_shape=jax.ShapeDtypeStruct((K, N), dtype),
        scratch_shapes=[] if dtype == F32 else [pltpu.VMEM((tk, tn), F32)],
        compiler_params=_params(("parallel", "parallel", "arbitrary")),
    )(a, b)


def _tri(n, kind):
    r = lax.broadcasted_iota(jnp.int32, (n, n), 0)
    c = lax.broadcasted_iota(jnp.int32, (n, n), 1)
    return {"gt": r > c, "le": r <= c, "lt": r < c, "ge": r >= c}[kind]


def _running_sums(x, carry, tri, kb, reverse=False):
    blocks = range(x.shape[1] // kb)
    parts = {}
    for b in (reversed(blocks) if reverse else blocks):
        piece = x[:, b * kb:(b + 1) * kb]
        parts[b] = carry + _mm(piece.astype(BF16), tri)
        carry = carry + jnp.sum(piece, axis=1, keepdims=True)
    return jnp.concatenate([parts[b] for b in blocks], axis=1), carry


def _sb_valid(row0, col0, first, last, qb, kb):
    ahead = lax.broadcasted_iota(jnp.int32, (qb, kb), 1) - lax.broadcasted_iota(jnp.int32, (qb, kb), 0)
    col = lax.broadcasted_iota(jnp.int32, (1, kb), 1)
    return jnp.logical_and(ahead < row0 - col0, jnp.logical_and(col >= first - col0, col < last - col0))


def _sb_softplus(z):
    return jnp.maximum(z, 0.0) + jnp.log(1.0 + jnp.exp(-jnp.abs(z)))


_PAIR = 2 * HEAD_DIM
_PAIRS = WIDTH // _PAIR


def _first_head_lanes():
    return lax.broadcasted_iota(jnp.int32, (1, _PAIR), 1) < HEAD_DIM


def _per_head(t, first_head):
    zero = jnp.zeros_like(t)
    return jnp.where(first_head, t, zero), jnp.where(first_head, zero, t)


def _sb_fwd(qkv, bl, seq, ride=None):
    qb, win, kb = min(ATT_Q, seq), min(ATT_WINDOW, seq), ATT_ALIGN
    nq = seq // qb
    nh, width, groups = 2 * ATT_PAIRS, ATT_PAIRS * _PAIR, _PAIRS // ATT_PAIRS
    pair_of = lambda h: slice((h // 2) * _PAIR, (h // 2 + 1) * _PAIR)

    steps = bl * groups
    pass_on_at = (5 * steps) // 8
    nt = 0 if ride is None else len(ride)

    def body(q_ref, k_ref, v_ref, *rest):
        g = pl.program_id(0) * groups + pl.program_id(1)
        o_ref, l_ref, stop_ref = rest[nt:nt + 3]
        if ride is not None:
            start, forward, finish = _gather_steps(rest[:nt], rest[nt + 3:2 * nt + 3], *rest[2 * nt + 3:])
            pl.when(g == 0)(start)
            pl.when(g == pass_on_at)(forward)
        first_head = _first_head_lanes()
        u_after = _tri(kb, "gt").astype(BF16)

        def qblock(i, _):
            rows = pl.ds(pl.multiple_of(i * qb, qb), qb)
            qs = q_ref[rows, :] * (HEAD_DIM ** -0.5)
            qh = [_per_head(qs[:, pair_of(h)], first_head)[h % 2] for h in range(nh)]

            def live(carry):
                return jnp.logical_and(carry[0] > 0, carry[3] > 0)

            def window(carry):
                hi, accs, cs, _ = carry
                lo = pl.multiple_of(jnp.maximum(hi - win, 0), kb)
                cols = pl.ds(lo, win)
                kv, vv = k_ref[cols, :], v_ref[cols, :]
                valid = _sb_valid(i * qb, lo, lo, hi, qb, win)
                accs, cs = list(accs), list(cs)
                for h in range(nh):
                    z = _mm_nt(qh[h], kv[:, pair_of(h)])
                    sp = _sb_softplus(z)
                    spm = jnp.where(valid, sp, 0.0)
                    after, cs[h] = _running_sums(spm, cs[h], u_after, kb, reverse=True)
                    w = jnp.where(valid, jnp.exp(z - sp - after), 0.0)
                    accs[h] = accs[h] + _mm(w.astype(BF16), vv[:, pair_of(h)])
                alive = functools.reduce(jnp.minimum, [jnp.min(c) for c in cs]) < -SB_DEAD
                return lo, tuple(accs), tuple(cs), alive.astype(jnp.int32)

            zero_acc, zero_c = jnp.zeros((qb, _PAIR), F32), jnp.zeros((qb, 1), F32)
            lo, accs, cs, _ = lax.while_loop(
                live, window, ((i + 1) * qb, (zero_acc,) * nh, (zero_c,) * nh, jnp.int32(1)))
            for pp in range(ATT_PAIRS):
                o_ref[rows, pp * _PAIR:(pp + 1) * _PAIR] = jnp.where(
                    first_head, accs[2 * pp], accs[2 * pp + 1]).astype(BF16)
            for h in range(nh):
                l_ref[h, rows, :] = cs[h]
            stop_ref[g, i] = lo
            return 0

        lax.fori_loop(0, nq, qblock, 0)
        if ride is not None:
            pl.when(g == steps - 1)(finish)

    col = lambda off: pl.BlockSpec((seq, width), lambda b, p: (b, off + p))
    in_specs = [col(0), col(groups), col(2 * groups)]
    out_specs = [col(0), pl.BlockSpec((None, nh, seq, 1), lambda b, p: (b, p, 0, 0)), pl.BlockSpec(memory_space=pltpu.SMEM)]
    out_shape = [jax.ShapeDtypeStruct((bl * seq, WIDTH), BF16), jax.ShapeDtypeStruct((bl, HEADS, seq, 1), F32),
                 jax.ShapeDtypeStruct((bl * groups, nq), jnp.int32)]
    if ride is None:
        return pl.pallas_call(body, name="sb_fwd", grid=(bl, groups), in_specs=in_specs, out_specs=out_specs,
                              out_shape=out_shape, compiler_params=_params(("arbitrary", "arbitrary")))(qkv, qkv, qkv)
    gathered, sems = _gather_results(ride)
    return pl.pallas_call(
        body, name="sb_fwd", grid=(bl, groups), in_specs=in_specs + [_ANY] * nt, out_specs=out_specs + [_ANY] * nt,
        out_shape=out_shape + gathered, scratch_shapes=sems, compiler_params=_params(("arbitrary", "arbitrary")),
    )(qkv, qkv, qkv, *ride)


def _sb_bwd(qkv, do, lsum, stop, bl, seq):
    qb, win, kb = min(ATT_Q, seq), min(ATT_WINDOW, seq), ATT_ALIGN
    nq = seq // qb
    nh, width, groups = 2 * ATT_PAIRS, ATT_PAIRS * _PAIR, _PAIRS // ATT_PAIRS
    pair_of = lambda h: slice((h // 2) * _PAIR, (h // 2 + 1) * _PAIR)

    def body(stop_ref, q_ref, k_ref, v_ref, do_ref, l_ref, dq_ref, dk_ref, dv_ref, dk_acc, dv_acc):
        g = pl.program_id(0) * groups + pl.program_id(1)
        first_head = _first_head_lanes()
        u_incl = _tri(kb, "le").astype(BF16)
        u_excl = _tri(kb, "lt").astype(BF16)
        dk_acc[...] = jnp.zeros_like(dk_acc)
        dv_acc[...] = jnp.zeros_like(dv_acc)

        def qblock(i, _):
            rows = pl.ds(pl.multiple_of(i * qb, qb), qb)
            qv = q_ref[rows, :]
            qs = qv * (HEAD_DIM ** -0.5)
            dob = do_ref[rows, :]
            qh = [_per_head(qs[:, pair_of(h)], first_head)[h % 2] for h in range(nh)]
            doh = [_per_head(dob[:, pair_of(h)], first_head)[h % 2] for h in range(nh)]
            ltot = [l_ref[h, rows, :] for h in range(nh)]

            first = (jnp.clip(stop_ref[g, i], 0, i * qb) // ATT_ALIGN) * ATT_ALIGN

            def window(n, carry):
                dqs, ps, es = (list(t) for t in carry)
                start = first + n * win
                lo = pl.multiple_of(jnp.minimum(start, seq - win), kb)
                cols = pl.ds(lo, win)
                kv, vv = k_ref[cols, :], v_ref[cols, :]
                valid = _sb_valid(i * qb, lo, start, seq, qb, win)
                dks, dvs = [], []
                for h in range(nh):
                    kp, vp = kv[:, pair_of(h)], vv[:, pair_of(h)]
                    z = _mm_nt(qh[h], kp)
                    sp = _sb_softplus(z)
                    spm = jnp.where(valid, sp, 0.0)
                    upto, ps[h] = _running_sums(spm, ps[h], u_incl, kb)
                    w = jnp.where(valid, jnp.exp(z - sp - (ltot[h] - upto)), 0.0)
                    e = _mm_nt(doh[h], vp) * w
                    dlf, es[h] = _running_sums(e, es[h], u_excl, kb)
                    sig = jnp.exp(z - sp)
                    dz = jnp.where(valid, e * (1.0 - sig) - dlf * sig, 0.0) * (HEAD_DIM ** -0.5)
                    dzb = dz.astype(BF16)
                    dvs.append(_mm_tn(w.astype(BF16), dob[:, pair_of(h)]))
                    dks.append(_mm_tn(dzb, qv[:, pair_of(h)]))
                    dqs[h] = dqs[h] + _mm(dzb, kp)
                for pp in range(ATT_PAIRS):
                    lanes = slice(pp * _PAIR, (pp + 1) * _PAIR)
                    dv_acc[cols, lanes] += jnp.where(first_head, dvs[2 * pp], dvs[2 * pp + 1])
                    dk_acc[cols, lanes] += jnp.where(first_head, dks[2 * pp], dks[2 * pp + 1])
                return tuple(dqs), tuple(ps), tuple(es)

            zero_q, zero_c = jnp.zeros((qb, _PAIR), F32), jnp.zeros((qb, 1), F32)
            dqs, _, _ = lax.fori_loop(0, ((i + 1) * qb - first + win - 1) // win, window,
                                      ((zero_q,) * nh, (zero_c,) * nh, (zero_c,) * nh))
            for pp in range(ATT_PAIRS):
                dq_ref[rows, pp * _PAIR:(pp + 1) * _PAIR] = jnp.where(
                    first_head, dqs[2 * pp], dqs[2 * pp + 1]).astype(BF16)
            return 0

        lax.fori_loop(0, nq, qblock, 0)
        dk_ref[...] = dk_acc[...].astype(BF16)
        dv_ref[...] = dv_acc[...].astype(BF16)

    col = lambda off: pl.BlockSpec((seq, width), lambda b, p, stop_ref: (b, off + p))
    return pl.pallas_call(
        body, name="sb_bwd",
        grid_spec=pltpu.PrefetchScalarGridSpec(
            num_scalar_prefetch=1, grid=(bl, groups),
            in_specs=[col(0), col(groups), col(2 * groups), col(0),
                      pl.BlockSpec((None, nh, seq, 1), lambda b, p, stop_ref: (b, p, 0, 0))],
            out_specs=[col(0), col(0), col(0)],
            scratch_shapes=[pltpu.VMEM((seq, width), F32), pltpu.VMEM((seq, width), F32)]),
        out_shape=[jax.ShapeDtypeStruct((bl * seq, WIDTH), BF16)] * 3,
        compiler_params=_params(("parallel", "parallel")),
    )(stop, qkv, qkv, qkv, do, lsum)


@jax.custom_vjp
def _lora_mm(x, w):
    return _mm(x.astype(BF16), w.astype(BF16))


_lora_mm.defvjp(
    lambda x, w: (_mm(x.astype(BF16), w.astype(BF16)), (x, w)),
    lambda res, ct: (_mm_nt(ct.astype(BF16), res[1].astype(BF16)), _mm_tn(res[0].astype(BF16), ct.astype(BF16))))


def _rw_prep_math(p, ps, mu, w0, w_up, a0, a_up, g_up, k_k, k_a):
    pm = p + (ps - p) * mu
    r, k, v = pm[:, :WIDTH], pm[:, WIDTH:2 * WIDTH], pm[:, 2 * WIDTH:3 * WIDTH]
    o = 3 * WIDTH
    xw, xa, xg = pm[:, o:o + W_LORA], pm[:, o + W_LORA:o + W_LORA + A_LORA], pm[:, o + W_LORA + A_LORA:]
    w_raw = w0 + _lora_mm(jnp.tanh(xw), w_up)
    lw = -jnp.exp(-_softplus(-w_raw) - 0.5)
    a = jax.nn.sigmoid(a0 + _lora_mm(xa, a_up))
    g = _lora_mm(jax.nn.sigmoid(xg), g_up)
    kk = k * k_k
    k2 = k * (1.0 + (a - 1.0) * k_a)
    return r, lw, k2, v, kk, a, g


def _shift_down(p, first_row):
    row = lax.broadcasted_iota(jnp.int32, p.shape, 0)
    return jnp.where(row == 0, first_row, pltpu.roll(p, 1, 0))


def _shift_up(p, last_row):
    row = lax.broadcasted_iota(jnp.int32, p.shape, 0)
    return jnp.where(row == p.shape[0] - 1, last_row, pltpu.roll(p, p.shape[0] - 1, 0))


_PREP_PARAM_SHAPES = [(1, RW_COLS), (1, WIDTH), (W_LORA, WIDTH), (1, WIDTH), (A_LORA, WIDTH), (G_LORA, WIDTH),
                      (1, WIDTH), (1, WIDTH)]


def _prev_rows_spec(tm):
    return pl.BlockSpec((8, RW_COLS), lambda i: (jnp.maximum(i * (tm // 8) - 1, 0), 0))


def _head_spec(tm, seq, tile_of=lambda i: i):
    per_seq = seq // tm
    return pl.BlockSpec((None, HEADS, tm, HEAD_DIM),
                        lambda i: (tile_of(i) // per_seq, 0, tile_of(i) % per_seq, 0))


def _split_heads(val, ref):
    for h in range(HEADS):
        ref[h] = val[:, h * HEAD_DIM:(h + 1) * HEAD_DIM]


def _join_heads(ref):
    return jnp.concatenate([ref[h] for h in range(HEADS)], axis=1)


def _rw_prep_fwd(prw, params, seq, tm):
    T = prw.shape[0]

    def body(p_ref, prev_ref, *rest):
        prm = [r_[...] for r_ in rest[:8]]
        outs = rest[8:]
        i = pl.program_id(0)
        first = jnp.where((i * tm) % seq == 0, 0.0, prev_ref[7:8, :])
        p = p_ref[...]
        vals = _rw_prep_math(p, _shift_down(p, first), *prm)
        for o_ref, val in zip(outs[:6], vals[:6]):
            _split_heads(val, o_ref)
        outs[6][...] = vals[6]

    by_head = jax.ShapeDtypeStruct((T // seq, HEADS, seq, HEAD_DIM), F32)
    return pl.pallas_call(
        body, name="rw_prep_fwd", grid=(T // tm,),
        in_specs=[_row_spec(tm, RW_COLS), _prev_rows_spec(tm)] + [_const_spec(s) for s in _PREP_PARAM_SHAPES],
        out_specs=[_head_spec(tm, seq)] * 6 + [_row_spec(tm, WIDTH)],
        out_shape=[by_head] * 6 + [jax.ShapeDtypeStruct((T, WIDTH), F32)],
        compiler_params=_params(("parallel",)),
    )(prw, prw, *params)


def _rw_prep_bwd(prw, params, cts, seq, tm):
    T = prw.shape[0]
    n = T // tm

    def body(p_ref, prev_ref, *rest):
        prm = [r_[...] for r_ in rest[:8]]
        ct = tuple(_join_heads(r_) for r_ in rest[8:14]) + (rest[14][...],)
        dp_ref = rest[15]
        dprm_refs = rest[16:24]
        carry = rest[24]
        step = pl.program_id(0)
        i = n - 1 - step
        first = jnp.where((i * tm) % seq == 0, 0.0, prev_ref[7:8, :])
        p = p_ref[...]
        _, vjp = jax.vjp(_rw_prep_math, p, _shift_down(p, first), *prm)
        grads = vjp(ct)
        dp, dps = grads[0], grads[1]
        nxt = jnp.where(jnp.logical_or(step == 0, ((i + 1) * tm) % seq == 0), 0.0, carry[0:1, :])
        dp_ref[...] = (dp + _shift_up(dps, nxt)).astype(BF16)
        carry[...] = dps[0:8, :]
        for ref, gval in zip(dprm_refs, grads[2:]):
            _acc_out(ref, gval, step == 0)

    rev = lambda w: pl.BlockSpec((tm, w), lambda s: (n - 1 - s, 0))
    prev = pl.BlockSpec((8, RW_COLS), lambda s: (jnp.maximum((n - 1 - s) * (tm // 8) - 1, 0), 0))
    return pl.pallas_call(
        body, name="rw_prep_bwd", grid=(n,),
        in_specs=([rev(RW_COLS), prev] + [_const_spec(s) for s in _PREP_PARAM_SHAPES]
                  + [_head_spec(tm, seq, lambda s: n - 1 - s)] * 6 + [rev(WIDTH)]),
        out_specs=[rev(RW_COLS)] + [pl.BlockSpec(s, lambda s_: (0, 0)) for s in _PREP_PARAM_SHAPES],
        out_shape=[jax.ShapeDtypeStruct((T, RW_COLS), BF16)] + [jax.ShapeDtypeStruct(s, F32) for s in _PREP_PARAM_SHAPES],
        scratch_shapes=[pltpu.VMEM((8, RW_COLS), F32)],
        compiler_params=_params(("arbitrary",)),
    )(prw, prw, *params, *cts)


def _make_bmm(passes):
    def raw(dn, a, b):
        d = lambda x, y: lax.dot_general(x, y, dn, preferred_element_type=F32)
        ah = a.astype(BF16)
        bh = b.astype(BF16)
        if passes == 1:
            return d(ah, bh)
        al = (a - ah.astype(F32)).astype(BF16)
        bl = (b - bh.astype(F32)).astype(BF16)
        return d(ah, bh) + (d(ah, bl) + d(al, bh))

    dn_nn = (((2,), (1,)), ((0,), (0,)))
    dn_nt = (((2,), (2,)), ((0,), (0,)))
    dn_tn = (((1,), (1,)), ((0,), (0,)))

    @jax.custom_vjp
    def nn(a, b):
        return raw(dn_nn, a, b)

    @jax.custom_vjp
    def nt(a, b):
        return raw(dn_nt, a, b)

    @jax.custom_vjp
    def tn(a, b):
        return raw(dn_tn, a, b)

    nn.defvjp(lambda a, b: (raw(dn_nn, a, b), (a, b)), lambda res, ct: (nt(ct, res[1]), tn(res[0], ct)))
    nt.defvjp(lambda a, b: (raw(dn_nt, a, b), (a, b)), lambda res, ct: (nn(ct, res[1]), tn(ct, res[0])))
    tn.defvjp(lambda a, b: (raw(dn_tn, a, b), (a, b)), lambda res, ct: (nt(res[1], ct), nn(res[0], ct)))

    def unit_lower_inverse(m):
        n = m.shape[-1]
        row = lax.broadcasted_iota(jnp.int32, (n, n), 0)
        col = lax.broadcasted_iota(jnp.int32, (n, n), 1)
        m16 = ((row // 16) == (col // 16)).astype(F32)
        m32 = ((row // 32) == (col // 32)).astype(F32)
        a1 = m * m16
        a2 = nn(a1, a1)
        a4 = nn(a2, a2)
        a8 = nn(a4, a4)
        inv = (row == col).astype(F32) - a1
        inv = inv + nn(inv, a2)
        inv = inv + nn(inv, a4)
        inv = inv + nn(inv, a8)
        inv = inv - nn(nn(inv, m * (m32 - m16)), inv)
        return inv - nn(nn(inv, m * (1.0 - m32)), inv)

    @jax.custom_vjp
    def inverse(m):
        return unit_lower_inverse(m)

    def inverse_fwd(m):
        inv = unit_lower_inverse(m)
        return inv, inv

    inverse.defvjp(inverse_fwd, lambda inv, ct: (-nt(tn(inv, ct), inv),))
    return nn, nt, tn, inverse


def _wkv_chunk(s0, r, lw, k, v, kk, a, lnw, lnb, rk):
    nn, nt, tn, inverse = _make_bmm(SCAN_PASSES)
    G, L, N = r.shape
    rep = lambda t: jnp.broadcast_to(t[None], (G // HEADS, HEADS, 1, N)).reshape(G, 1, N)
    kap = kk * lax.rsqrt(jnp.maximum(jnp.sum(kk * kk, axis=-1, keepdims=True), 1e-24))
    b = a * kap
    row = lax.broadcasted_iota(jnp.int32, (L, L), 0)
    col = lax.broadcasted_iota(jnp.int32, (L, L), 1)
    low_incl = (col <= row).astype(F32)
    low_strict = (col < row).astype(F32)
    c = _make_bmm(3)[0](jnp.broadcast_to(low_incl[None], (G, L, L)), lw)
    c_all = jnp.sum(lw, axis=1, keepdims=True)
    g_inv = jnp.exp(-c)
    kap_t = kap * jnp.exp(c - lw)
    b_t = b * g_inv
    k_t = k * g_inv
    r_t = r * jnp.exp(c)
    g_all = jnp.exp(c_all)
    m_b = nt(kap_t, b_t) * low_strict
    m_k = nt(kap_t, k_t) * low_strict
    n_b = nt(r_t, b_t) * low_incl
    n_k = nt(r_t, k_t) * low_incl
    rhs = -(nt(kap_t, s0) + nn(m_k, v))
    sa = nn(inverse(m_b), rhs)
    y = nt(r_t, s0) + nn(n_b, sa) + nn(n_k, v)
    s1 = s0 * g_all + tn(sa, b_t * g_all) + tn(v, k_t * g_all)
    mean = jnp.mean(y, axis=-1, keepdims=True)
    yc = y - mean
    var = jnp.mean(yc * yc, axis=-1, keepdims=True)
    out = yc * lax.rsqrt(var + GN_EPS) * rep(lnw) + rep(lnb)
    out = out + jnp.sum(r * k * rep(rk), axis=-1, keepdims=True) * v
    return out, s1


def _scan_heads_per_step(total_heads, seqs_wanted):
    n_seq = total_heads // HEADS
    return HEADS * max(d for d in range(1, seqs_wanted + 1) if n_seq % d == 0)


def _wkv_fwd(seqs, lnw, lnb, rk):
    G, S, N = seqs[0].shape
    L = SCAN_CHUNK
    nc = S // L

    def body(*refs):
        ins = [r_[...] for r_ in refs[:6]]
        prm = [r_[...] for r_ in refs[6:9]]
        out_ref, st_ref, state = refs[9], refs[10], refs[11]

        @pl.when(pl.program_id(1) == 0)
        def _():
            state[...] = jnp.zeros_like(state)

        s0 = state[...]
        st_ref[...] = s0
        out, s1 = _wkv_chunk(s0, *ins, *prm)
        out_ref[...] = out
        state[...] = s1

    gb = _scan_heads_per_step(G, SCAN_SEQS_FWD)
    blk = pl.BlockSpec((gb, L, N), lambda b, i: (b, i, 0))
    pspec = _const_spec((HEADS, 1, N))
    return pl.pallas_call(
        body, name="wkv_fwd", grid=(G // gb, nc), in_specs=[blk] * 6 + [pspec] * 3,
        out_specs=[blk, pl.BlockSpec((None, gb, N, N), lambda b, i: (i, b, 0, 0))],
        out_shape=[jax.ShapeDtypeStruct((G, S, N), F32), jax.ShapeDtypeStruct((nc, G, N, N), F32)],
        scratch_shapes=[pltpu.VMEM((gb, N, N), F32)],
        compiler_params=_params(("parallel", "arbitrary")),
    )(*seqs, lnw, lnb, rk)


def _wkv_bwd(seqs, states, dout, lnw, lnb, rk, ride=None):
    G, S, N = seqs[0].shape
    L = SCAN_CHUNK
    nc = S // L
    gb = _scan_heads_per_step(G, SCAN_SEQS_BWD)
    nt = 0 if ride is None else len(ride)

    def body(*refs):
        ins = [r_[...] for r_ in refs[:6]]
        s0 = refs[6][...]
        ct_out = refs[7][...]
        prm = [r_[...] for r_ in refs[8:11]]
        refs = refs[11:]
        if ride is not None:
            start, finish = _scatter_steps(refs[:nt], refs[nt + 9:2 * nt + 9], refs[-2], refs[-1])
            pl.when(jnp.logical_and(pl.program_id(0) == 0, pl.program_id(1) == 0))(start)
            refs = refs[nt:]
        d_refs = refs[0:6]
        dprm_refs = refs[6:9]
        dstate = refs[9 + nt]
        step = pl.program_id(1)

        @pl.when(step == 0)
        def _():
            dstate[...] = jnp.zeros_like(dstate)

        _, vjp = jax.vjp(_wkv_chunk, s0, *ins, *prm)
        grads = vjp((ct_out, dstate[...]))
        dstate[...] = grads[0]
        for ref, gval in zip(d_refs, grads[1:7]):
            ref[...] = gval
        for ref, gval in zip(dprm_refs, grads[7:]):
            _acc_out(ref, gval, jnp.logical_and(step == 0, pl.program_id(0) == 0))
        if ride is not None:
            pl.when(jnp.logical_and(pl.program_id(0) == G // gb - 1, step == nc - 1))(finish)

    blk = pl.BlockSpec((gb, L, N), lambda b, s: (b, nc - 1 - s, 0))
    pspec = _const_spec((HEADS, 1, N))
    pout = pl.BlockSpec((HEADS, 1, N), lambda b, s: (0, 0, 0))
    in_specs = [blk] * 6 + [pl.BlockSpec((None, gb, N, N), lambda b, s: (nc - 1 - s, b, 0, 0)), blk] + [pspec] * 3
    out_specs = [blk] * 6 + [pout] * 3
    out_shape = [jax.ShapeDtypeStruct((G, S, N), F32)] * 6 + [jax.ShapeDtypeStruct((HEADS, 1, N), F32)] * 3
    scratch = [pltpu.VMEM((gb, N, N), F32)]
    args = (*seqs, states, dout, lnw, lnb, rk)
    if ride is not None:
        in_specs, out_specs, args = in_specs + [_ANY] * nt, out_specs + [_ANY] * nt, args + tuple(ride)
        got_shapes, sems = _scatter_results(ride)
        out_shape, scratch = out_shape + got_shapes, scratch + sems
    return pl.pallas_call(
        body, name="wkv_bwd", grid=(G // gb, nc), in_specs=in_specs, out_specs=out_specs, out_shape=out_shape,
        scratch_shapes=scratch, compiler_params=_params(("arbitrary", "arbitrary")),
    )(*args)


def _merge_math(o_sb, rw_out, g_rw, gates, w_sb, w_rw, w_o):
    o_rw = (rw_out * g_rw).astype(BF16)
    a = _mm(o_sb, w_sb)
    b = _mm(o_rw, w_rw)
    g1, g2 = gates[:, :D_MODEL], gates[:, D_MODEL:]
    merged = (g1 * a + g2 * b).astype(BF16)
    return o_rw, a, b, g1, g2, merged, _mm(merged, w_o)


def _merge_fwd(x2, o_sb, rw_out, g_rw, gates, w_sb, w_rw, w_o, g_post, seq, tm):
    T = x2.shape[0]

    def body(x_ref, osb_ref, rw_ref, g_ref, gate_ref, wsb_ref, wrw_ref, wo_ref, gp_ref, x1_ref):
        z = _merge_math(osb_ref[...], _join_heads(rw_ref), g_ref[...], gate_ref[...], wsb_ref[...], wrw_ref[...], wo_ref[...])[-1]
        x1_ref[...] = x_ref[...] + _rms_fwd(z, gp_ref[...])[0]

    return pl.pallas_call(
        body, name="merge_fwd", grid=(T // tm,),
        in_specs=[_row_spec(tm, D_MODEL), _row_spec(tm, WIDTH), _head_spec(tm, seq), _row_spec(tm, WIDTH),
                  _row_spec(tm, GATE_COLS), _const_spec((WIDTH, D_MODEL)), _const_spec((WIDTH, D_MODEL)),
                  _const_spec((D_MODEL, D_MODEL)), _const_spec((1, D_MODEL))],
        out_specs=_row_spec(tm, D_MODEL),
        out_shape=jax.ShapeDtypeStruct((T, D_MODEL), F32),
        compiler_params=_params(("parallel",)),
    )(x2, o_sb, rw_out, g_rw, gates, w_sb, w_rw, w_o, g_post)


def _merge_bwd(dx1, o_sb, rw_out, g_rw, gates, w_sb, w_rw, w_o, g_post, seq, tm):
    T = dx1.shape[0]

    def body(dx1_ref, osb_ref, rw_ref, g_ref, gate_ref, wsb_ref, wrw_ref, wo_ref, gp_ref,
             orw_o, mrg_o, dz_o, da_o, db_o, dgate_o, dosb_o, drw_o, dg_o, dgp_o, dbg_o):
        rw_out_v, g_rw_v = _join_heads(rw_ref), g_ref[...]
        w_sb_v, w_rw_v, w_o_v = wsb_ref[...], wrw_ref[...], wo_ref[...]
        o_rw, a, b, g1, g2, merged, z = _merge_math(osb_ref[...], rw_out_v, g_rw_v, gate_ref[...], w_sb_v, w_rw_v, w_o_v)
        gain = gp_ref[...]
        _, zn, rstd = _rms_fwd(z, gain)
        dz, dgain = _rms_bwd(dx1_ref[...], zn, rstd, gain)
        dzb = dz.astype(BF16)
        dm = _mm_nt(dzb, w_o_v)
        dab = (dm * g1).astype(BF16)
        dbb = (dm * g2).astype(BF16)
        dgate = jnp.concatenate([dm * a * g1 * (1.0 - g1), dm * b * g2 * (1.0 - g2)], axis=1)
        do_rw = _mm_nt(dbb, w_rw_v)
        orw_o[...] = o_rw
        mrg_o[...] = merged
        dz_o[...] = dzb
        da_o[...] = dab
        db_o[...] = dbb
        dgate_o[...] = dgate.astype(BF16)
        dosb_o[...] = _mm_nt(dab, w_sb_v).astype(BF16)
        _split_heads(do_rw * g_rw_v, drw_o)
        dg_o[...] = do_rw * rw_out_v
        first = pl.program_id(0) == 0
        _acc_out(dgp_o, dgain, first)
        _acc_out(dbg_o, jnp.sum(dgate, axis=0, keepdims=True), first)

    acc = lambda n: pl.BlockSpec((1, n), lambda i: (0, 0))
    sd = jax.ShapeDtypeStruct
    return pl.pallas_call(
        body, name="merge_bwd", grid=(T // tm,),
        in_specs=[_row_spec(tm, D_MODEL), _row_spec(tm, WIDTH), _head_spec(tm, seq), _row_spec(tm, WIDTH),
                  _row_spec(tm, GATE_COLS), _const_spec((WIDTH, D_MODEL)), _const_spec((WIDTH, D_MODEL)),
                  _const_spec((D_MODEL, D_MODEL)), _const_spec((1, D_MODEL))],
        out_specs=[_row_spec(tm, WIDTH), _row_spec(tm, D_MODEL), _row_spec(tm, D_MODEL), _row_spec(tm, D_MODEL),
                   _row_spec(tm, D_MODEL), _row_spec(tm, GATE_COLS), _row_spec(tm, WIDTH), _head_spec(tm, seq),
                   _row_spec(tm, WIDTH), acc(D_MODEL), acc(GATE_COLS)],
        out_shape=[sd((T, WIDTH), BF16), sd((T, D_MODEL), BF16), sd((T, D_MODEL), BF16), sd((T, D_MODEL), BF16),
                   sd((T, D_MODEL), BF16), sd((T, GATE_COLS), BF16), sd((T, WIDTH), BF16),
                   sd((T // seq, HEADS, seq, HEAD_DIM), F32), sd((T, WIDTH), F32), sd((1, D_MODEL), F32),
                   sd((1, GATE_COLS), F32)],
        compiler_params=_params(("arbitrary",)),
    )(dx1, o_sb, rw_out, g_rw, gates, w_sb, w_rw, w_o, g_post)


def _ffn(x1, target, g_pre, g_post, w_gate, w_up, w_down, tm):
    T = x1.shape[0]

    def body(x1_ref, tgt_ref, gpre_ref, gpost_ref, wg_ref, wu_ref, wd_ref,
             loss_o, dx1_o, h_o, dgate_o, dup_o, act_o, df_o, dgpre_o, dgpost_o):
        x1v = x1_ref[...]
        gpre, gpost = gpre_ref[...], gpost_ref[...]
        wg, wu, wd = wg_ref[...], wu_ref[...], wd_ref[...]
        hn, xn1, rstd1 = _rms_fwd(x1v, gpre)
        h = hn.astype(BF16)
        gate = _mm(h, wg)
        up = _mm(h, wu)
        sg = jax.nn.sigmoid(gate)
        act = (gate * sg * up).astype(BF16)
        f = _mm(act, wd)
        fo, fn, rstd2 = _rms_fwd(f, gpost)
        diff = x1v + fo - tgt_ref[...]
        dy = diff * (1.0 / D_MODEL)
        df, dgpost = _rms_bwd(dy, fn, rstd2, gpost)
        dfb = df.astype(BF16)
        dact = _mm_nt(dfb, wd)
        dup = (dact * gate * sg).astype(BF16)
        dgate = (dact * up * (sg * (1.0 + gate * (1.0 - sg)))).astype(BF16)
        dh = _mm_nt(dgate, wg) + _mm_nt(dup, wu)
        dxn, dgpre = _rms_bwd(dh, xn1, rstd1, gpre)
        dx1_o[...] = dy + dxn
        h_o[...] = h
        dgate_o[...] = dgate
        dup_o[...] = dup
        act_o[...] = act
        df_o[...] = dfb
        first = pl.program_id(0) == 0
        part = jnp.sum(jnp.sum(diff * diff, axis=1, keepdims=True), axis=0, keepdims=True) * (0.5 / D_MODEL)
        _acc_out(loss_o, jnp.broadcast_to(part, (8, 128)), first)
        _acc_out(dgpre_o, dgpre, first)
        _acc_out(dgpost_o, dgpost, first)

    acc = lambda r, n: pl.BlockSpec((r, n), lambda i: (0, 0))
    sd = jax.ShapeDtypeStruct
    return pl.pallas_call(
        body, name="ffn", grid=(T // tm,),
        in_specs=[_row_spec(tm, D_MODEL), _row_spec(tm, D_MODEL), _const_spec((1, D_MODEL)), _const_spec((1, D_MODEL)),
                  _const_spec((D_MODEL, D_FF)), _const_spec((D_MODEL, D_FF)), _const_spec((D_FF, D_MODEL))],
        out_specs=[acc(8, 128), _row_spec(tm, D_MODEL), _row_spec(tm, D_MODEL), _row_spec(tm, D_FF), _row_spec(tm, D_FF),
                   _row_spec(tm, D_FF), _row_spec(tm, D_MODEL), acc(1, D_MODEL), acc(1, D_MODEL)],
        out_shape=[sd((8, 128), F32), sd((T, D_MODEL), F32), sd((T, D_MODEL), BF16), sd((T, D_FF), BF16),
                   sd((T, D_FF), BF16), sd((T, D_FF), BF16), sd((T, D_MODEL), BF16), sd((1, D_MODEL), F32),
                   sd((1, D_MODEL), F32)],
        compiler_params=_params(("arbitrary",)),
    )(x1, target, g_pre, g_post, w_gate, w_up, w_down)


def _local_step(x, target, sm, wt, late=None):
    bl, seq, _ = x.shape
    T = bl * seq
    tm = min(ROW_TILE, T)
    x2 = x.reshape(T, D_MODEL)
    tgt2 = target.reshape(T, D_MODEL)
    h, qkv, prw, gates = _in_proj_fwd(x2, sm["norm_mix_pre"], wt["w_in"], sm["b_gate"], tm)
    if late is None:
        o_sb, lsum, sb_stop = _sb_fwd(qkv, bl, seq)
    else:
        o_sb, lsum, sb_stop, *gathered = _sb_fwd(qkv, bl, seq, late)
        wt = {**wt, **_whole_weights(gathered, slice(_EARLY, None))}
    prep_params = [sm["mu_rw"], sm["w0"], wt["w_up"].astype(F32), sm["a0"], wt["a_up"].astype(F32),
                   wt["g_up"].astype(F32), sm["k_k"], sm["k_a"]]
    prep = _rw_prep_fwd(prw, prep_params, seq, tm)
    by_head = lambda t: t.reshape(bl, HEADS, seq, HEAD_DIM)
    seqs = [t.reshape(bl * HEADS, seq, HEAD_DIM) for t in prep[:6]]
    g_rw = prep[6]
    lnw, lnb, rk = (sm[n].reshape(HEADS, 1, HEAD_DIM) for n in ("lnx_w", "lnx_b", "r_k"))
    rw_out_h, states = _wkv_fwd(seqs, lnw, lnb, rk)
    rw_out = by_head(rw_out_h)
    x1 = _merge_fwd(x2, o_sb, rw_out, g_rw, gates, wt["w_sb_out"], wt["w_rw_out"], wt["w_o"], sm["norm_mix_post"],
                    seq, tm)
    (loss_part, dx1, h2, dffg, dffu, act, dff, d_nfpre, d_nfpost) = _ffn(
        x1, tgt2, sm["norm_ffn_pre"], sm["norm_ffn_post"], wt["w_ffn_gate"], wt["w_ffn_up"], wt["w_ffn_down"],
        min(ROW_TILE_FFN, T))
    (o_rw, merged, dz, da, db, dgate, do_sb, d_rw_out, d_g_rw, d_npost, d_bgate) = _merge_bwd(
        dx1, o_sb, rw_out, g_rw, gates, wt["w_sb_out"], wt["w_rw_out"], wt["w_o"], sm["norm_mix_post"], seq, tm)
    gdt = F32 if late is None else BF16
    gw = {
        "w_sb_out": _grad_w(o_sb, da, "gw_sb_out", gdt), "w_rw_out": _grad_w(o_rw, db, "gw_rw_out", gdt),
        "w_o": _grad_w(merged, dz, "gw_o", gdt),
        "w_ffn_gate": _grad_w(h2, dffg, "gw_ffn_gate", gdt), "w_ffn_up": _grad_w(h2, dffu, "gw_ffn_up", gdt),
        "w_ffn_down": _grad_w(act, dff, "gw_ffn_down", gdt),
    }
    dqkv = jnp.concatenate(_sb_bwd(qkv, do_sb, lsum, sb_stop, bl, seq), axis=1)
    ride = None if late is None else _blocks_by_owner(gw, slice(_EARLY, None))
    wkv_g = _wkv_bwd(seqs, states, d_rw_out.reshape(bl * HEADS, seq, HEAD_DIM), lnw, lnb, rk, ride)
    scattered = {} if late is None else {n: (p, g) for (n, _, _), p, g in zip(_SHARDED[_EARLY:], ride, wkv_g[9:])}
    cts = [by_head(t) for t in wkv_g[:6]] + [d_g_rw]
    prep_g = _rw_prep_bwd(prw, prep_params, cts, seq, tm)
    dprw = prep_g[0]
    d_mu, d_w0, d_wup, d_a0, d_aup, d_gup, d_kk, d_ka = prep_g[1:]
    gw = {
        **gw,
        "w_in": jnp.concatenate([_grad_w(h, dqkv, "gw_in_qkv", gdt), _grad_w(h, dprw, "gw_in_rw", gdt),
                                 _grad_w(h, dgate, "gw_in_gate", gdt)], axis=1),
        "w_up": d_wup.astype(gdt), "a_up": d_aup.astype(gdt), "g_up": d_gup.astype(gdt),
    }
    ride = None if late is None else _blocks_by_owner(gw, slice(0, _EARLY))
    grad_x, d_npre, *got = _in_proj_bwd(x2, sm["norm_mix_pre"], dx1, dqkv, dprw, dgate, wt["w_in"], tm, ride)
    if late is not None:
        scattered.update({n: (p, g) for (n, _, _), p, g in zip(_SHARDED[:_EARLY], ride, got)})
    gs = {
        "norm_mix_pre": d_npre, "b_gate": d_bgate, "mu_rw": d_mu, "w0": d_w0, "a0": d_a0, "k_k": d_kk, "k_a": d_ka,
        "r_k": wkv_g[8].reshape(1, WIDTH), "lnx_w": wkv_g[6].reshape(1, WIDTH), "lnx_b": wkv_g[7].reshape(1, WIDTH),
        "norm_mix_post": d_npost, "norm_ffn_pre": d_nfpre, "norm_ffn_post": d_nfpost,
    }
    return loss_part, grad_x.reshape(x.shape), gw, gs, scattered


_SHARDED = [("w_in", 1, (D_MODEL, (SB_COLS + RW_COLS + GATE_COLS) // N_DEV)), ("w_up", 1, (W_LORA, WIDTH // N_DEV)),
            ("a_up", 1, (A_LORA, WIDTH // N_DEV)), ("g_up", 1, (G_LORA, WIDTH // N_DEV)),
            ("w_sb_out", 1, (WIDTH, D_MODEL // N_DEV)), ("w_rw_out", 1, (WIDTH, D_MODEL // N_DEV)),
            ("w_o", 0, (D_MODEL // N_DEV, D_MODEL)), ("w_ffn_gate", 1, (D_MODEL, D_FF // N_DEV)),
            ("w_ffn_up", 1, (D_MODEL, D_FF // N_DEV)), ("w_ffn_down", 0, (D_FF // N_DEV, D_MODEL))]
_LANES = 128
_SMALL = [("norm_mix_pre", D_MODEL), ("b_gate", GATE_COLS), ("mu_rw", RW_COLS), ("w0", WIDTH), ("a0", WIDTH),
          ("k_k", WIDTH), ("k_a", WIDTH), ("r_k", WIDTH), ("lnx_w", WIDTH), ("lnx_b", WIDTH),
          ("norm_mix_post", D_MODEL), ("norm_ffn_pre", D_MODEL), ("norm_ffn_post", D_MODEL)]
_SMALL_ROWS = 96


_EARLY = 4


def _whole_weights(gathered, which):
    return {n: blk.reshape(N_DEV * shp[0], shp[1]) if axis == 0 else blk.transpose(1, 0, 2).reshape(shp[0], N_DEV * shp[1])
            for (n, axis, shp), blk in zip(_SHARDED[which], gathered)}


def _blocks_by_owner(gw, which):
    return [gw[n].reshape((N_DEV,) + shp) if axis == 0 else gw[n].reshape(shp[0], N_DEV, shp[1]).transpose(1, 0, 2)
            for n, axis, shp in _SHARDED[which]]


def _pack_small(vals, extra=None):
    used = sum(sz for _, sz in _SMALL)
    tail = jnp.zeros((1, _SMALL_ROWS * _LANES - used), F32).at[0, 0].set(extra)
    return jnp.concatenate([vals[n].reshape(1, -1) for n, _ in _SMALL] + [tail], axis=1)


_ANY = pl.BlockSpec(memory_space=pl.ANY)


def _all_gather(blocks):
    n = len(blocks)

    def body(*refs):
        start, forward, finish = _gather_steps(refs[:n], refs[n:2 * n], *refs[2 * n:])
        start()
        forward()
        finish()

    out_shape, sems = _gather_results(blocks)
    return pl.pallas_call(
        body, name="all_gather_weights", in_specs=[_ANY] * n, out_specs=[_ANY] * n, out_shape=out_shape,
        scratch_shapes=sems,
    )(*blocks)


def _scatter_steps(pack_refs, got_refs, send_sems, recv_sems):
    x, y, c = lax.axis_index("x"), lax.axis_index("y"), lax.axis_index("c")

    def copies():
        out = []
        for t, (pack_ref, got_ref) in enumerate(zip(pack_refs, got_refs)):
            for k in range(1, N_DEV):
                px, py, pc = x ^ (k >> 2), y ^ ((k >> 1) & 1), c ^ (k & 1)
                sem = (N_DEV - 1) * t + k - 1
                out.append(pltpu.make_async_remote_copy(
                    src_ref=pack_ref.at[4 * px + 2 * py + pc], dst_ref=got_ref.at[k - 1], send_sem=send_sems.at[sem],
                    recv_sem=recv_sems.at[sem], device_id=(px, py, pc), device_id_type=MESH))
        return out

    def start():
        for cp in copies():
            cp.start()

    def finish():
        for cp in copies():
            cp.wait_recv()
        for cp in copies():
            cp.wait_send()

    return start, finish


def _scatter_results(ride):
    n = (N_DEV - 1) * len(ride)
    return ([jax.ShapeDtypeStruct((N_DEV - 1,) + t.shape[1:], t.dtype) for t in ride],
            [pltpu.SemaphoreType.DMA((n,)), pltpu.SemaphoreType.DMA((n,))])


def _gather_steps(x_refs, out_refs, send_sems, recv_sems, local_sems):
    x, y, c = lax.axis_index("x"), lax.axis_index("y"), lax.axis_index("c")
    me, sibling = (x, y, c), (x, y, 1 - c)
    chips = [(1 - x, y), (x, 1 - y), (1 - x, 1 - y)]
    tensors = range(len(x_refs))

    def slot(t, px, py, pc):
        return out_refs[t].at[4 * px + 2 * py + pc]

    def copy(t, k, blk, to, src=None):
        return pltpu.make_async_remote_copy(
            src_ref=slot(t, *blk) if src is None else src, dst_ref=slot(t, *blk), send_sem=send_sems.at[7 * t + k],
            recv_sem=recv_sems.at[7 * t + k], device_id=to, device_id_type=MESH)

    def mine(t):
        return pltpu.make_async_copy(x_refs[t], slot(t, *me), local_sems.at[t])

    def first():
        return [cp for t in tensors for cp in
                [copy(t, 0, me, sibling, src=x_refs[t])]
                + [copy(t, 1 + j, me, (*chip, c), src=x_refs[t]) for j, chip in enumerate(chips)]]

    def passed(t):
        return [copy(t, 4 + j, (*chip, c), sibling) for j, chip in enumerate(chips)]

    def start():
        for t in tensors:
            mine(t).start()
        for cp in first():
            cp.start()

    def forward():
        for t in tensors:
            for j, (chip, cp) in enumerate(zip(chips, passed(t))):
                copy(t, 1 + j, (*chip, c), me).wait_recv()
                cp.start()

    def finish():
        for t in tensors:
            copy(t, 0, sibling, me).wait_recv()
            for j, chip in enumerate(chips):
                copy(t, 4 + j, (*chip, 1 - c), me).wait_recv()
        for cp in first() + [cp for t in tensors for cp in passed(t)]:
            cp.wait_send()
        for t in tensors:
            mine(t).wait()

    return start, forward, finish


def _gather_results(blocks):
    n = len(blocks)
    return ([jax.ShapeDtypeStruct((N_DEV,) + t.shape, t.dtype) for t in blocks],
            [pltpu.SemaphoreType.DMA((7 * n,)), pltpu.SemaphoreType.DMA((7 * n,)), pltpu.SemaphoreType.DMA((n,))])


def _gather_small(small):
    def body(small_ref, parts_ref, send_sems, recv_sems, local_sem):
        x, y, c = lax.axis_index("x"), lax.axis_index("y"), lax.axis_index("c")
        me = 4 * x + 2 * y + c
        mine = pltpu.make_async_copy(small_ref, parts_ref.at[me], local_sem)
        mine.start()
        peers = [(x ^ (k >> 2), y ^ ((k >> 1) & 1), c ^ (k & 1)) for k in range(1, N_DEV)]
        sends = [pltpu.make_async_remote_copy(
            src_ref=small_ref, dst_ref=parts_ref.at[me], send_sem=send_sems.at[k], recv_sem=recv_sems.at[k],
            device_id=to, device_id_type=MESH) for k, to in enumerate(peers)]
        for cp in sends:
            cp.start()
        for k, (px, py, pc) in enumerate(peers):
            pltpu.make_async_remote_copy(
                src_ref=small_ref, dst_ref=parts_ref.at[4 * px + 2 * py + pc], send_sem=send_sems.at[k],
                recv_sem=recv_sems.at[k], device_id=(px, py, pc), device_id_type=MESH).wait_recv()
        for cp in sends:
            cp.wait_send()
        mine.wait()

    return pl.pallas_call(
        body, name="gather_small", in_specs=[_ANY], out_specs=_ANY,
        out_shape=jax.ShapeDtypeStruct((N_DEV,) + small.shape, F32),
        scratch_shapes=[pltpu.SemaphoreType.DMA((N_DEV - 1,)), pltpu.SemaphoreType.DMA((N_DEV - 1,)),
                        pltpu.SemaphoreType.DMA],
    )(small)


def _adamw_math(w, g, m, v):
    m = ADAM_B1 * m + (1.0 - ADAM_B1) * g
    v = ADAM_B2 * v + (1.0 - ADAM_B2) * (g * g)
    m_hat = m / (1.0 - ADAM_B1 ** ADAM_STEP)
    v_hat = v / (1.0 - ADAM_B2 ** ADAM_STEP)
    return -ADAM_LR * (m_hat / (jnp.sqrt(v_hat) + ADAM_EPS) + ADAM_WD * w), m, v


def _adamw_scattered(w, m, v, parts, got, me, name):
    rows, cols = w.shape
    tr = ADAMW_ROWS if rows % ADAMW_ROWS == 0 and rows * cols > ADAMW_WHOLE_BELOW else rows

    def body(me_ref, w_ref, m_ref, v_ref, own_ref, got_ref, g_o, d_o, m_o, v_o):
        g = own_ref[...].astype(F32)
        for k in range(N_DEV - 1):
            g = g + got_ref[k].astype(F32)
        g_o[...] = g
        d_o[...], m_o[...], v_o[...] = _adamw_math(w_ref[...], g, m_ref[...], v_ref[...])

    spec = pl.BlockSpec((tr, cols), lambda i, me_ref: (i, 0))
    return pl.pallas_call(
        body, name=name,
        grid_spec=pltpu.PrefetchScalarGridSpec(
            num_scalar_prefetch=1, grid=(rows // tr,),
            in_specs=[spec, spec, spec, pl.BlockSpec((None, tr, cols), lambda i, me_ref: (me_ref[0], i, 0)),
                      pl.BlockSpec((N_DEV - 1, tr, cols), lambda i, me_ref: (0, i, 0))],
            out_specs=[spec] * 4),
        out_shape=[jax.ShapeDtypeStruct((rows, cols), F32)] * 4, compiler_params=_params(("parallel",)),
    )(me, w, m, v, parts, got)


def _adamw_small(parts, ws, ms, vs):
    k = len(_SMALL)

    def body(p_ref, *refs):
        w_refs, m_refs, v_refs = refs[:k], refs[k:2 * k], refs[2 * k:3 * k]
        outs = refs[3 * k:]
        g = p_ref[0]
        for d in range(1, N_DEV):
            g = g + p_ref[d]
        o = 0
        for i, (_, n) in enumerate(_SMALL):
            gp = g[:, o:o + n]
            outs[1 + i][...] = gp
            outs[1 + k + i][...], outs[1 + 2 * k + i][...], outs[1 + 3 * k + i][...] = _adamw_math(
                w_refs[i][...], gp, m_refs[i][...], v_refs[i][...])
            o += n
        outs[0][...] = g[:, o:o + _LANES]

    shapes = [jax.ShapeDtypeStruct((1, n), F32) for _, n in _SMALL]
    out = pl.pallas_call(
        body, name="adamw_small", out_shape=[jax.ShapeDtypeStruct((1, _LANES), F32)] + shapes * 4,
        compiler_params=_params(),
    )(parts, *ws, *ms, *vs)
    return out[0][0, 0], out[1:1 + k], out[1 + k:1 + 2 * k], out[1 + 2 * k:1 + 3 * k], out[1 + 3 * k:]


_WEIGHT_NAMES = ['norm_mix_pre', 'w_in', 'b_gate', 'mu_rw', 'w0', 'w_up', 'a0', 'a_up', 'g_up', 'k_k', 'k_a', 'r_k',
                 'lnx_w', 'lnx_b', 'w_sb_out', 'w_rw_out', 'w_o', 'norm_mix_post', 'norm_ffn_pre', 'w_ffn_gate',
                 'w_ffn_up', 'w_ffn_down', 'norm_ffn_post']


def _step(x, target, w, m, v):
    sharded = [n for n, _, _ in _SHARDED]
    sm = {n: w[n].reshape(1, -1) for n, _ in _SMALL}
    own = {n: w[n][0] for n in sharded}
    mine = [own[n].astype(BF16) for n in sharded]
    wt = _whole_weights(_all_gather(mine[:_EARLY]), slice(0, _EARLY))
    loss_part, grad_x, _, gs, scattered = _local_step(x, target, sm, wt, mine[_EARLY:])

    me = 4 * lax.axis_index("x") + 2 * lax.axis_index("y") + lax.axis_index("c")
    me = jnp.reshape(me, (1,)).astype(jnp.int32)
    small_parts = _gather_small(_pack_small(gs, loss_part[0, 0]))
    row = lambda t: [t[n].reshape(1, -1) for n, _ in _SMALL]
    loss, *by_kind = _adamw_small(small_parts, row(w), row(m), row(v))
    g_s, d_s, m_s, v_s = ({n: t[i] for i, (n, _) in enumerate(_SMALL)} for t in by_kind)

    grads, deltas, new_m, new_v = {}, {}, {}, {}
    for n in _WEIGHT_NAMES:
        if n in scattered:
            out = _adamw_scattered(own[n], m[n][0], v[n][0], *scattered[n], me, "adamw_" + n)
            grads[n], deltas[n], new_m[n], new_v[n] = (t.reshape(w[n].shape) for t in out)
        else:
            grads[n], deltas[n], new_m[n], new_v[n] = (t[n].reshape(w[n].shape) for t in (g_s, d_s, m_s, v_s))
    return (loss, grad_x, *[grads[n] for n in _WEIGHT_NAMES], *[deltas[n] for n in _WEIGHT_NAMES],
            *[new_m[n] for n in _WEIGHT_NAMES], *[new_v[n] for n in _WEIGHT_NAMES])


def kernel(x, norm_mix_pre, w_in, b_gate, mu_rw, w0, w_up, a0, a_up, g_up, k_k, k_a, r_k, lnx_w, lnx_b, w_sb_out, w_rw_out, w_o, norm_mix_post, norm_ffn_pre, w_ffn_gate, w_ffn_up, w_ffn_down, norm_ffn_post, loss_target, m_norm_mix_pre, m_w_in, m_b_gate, m_mu_rw, m_w0, m_w_up, m_a0, m_a_up, m_g_up, m_k_k, m_k_a, m_r_k, m_lnx_w, m_lnx_b, m_w_sb_out, m_w_rw_out, m_w_o, m_norm_mix_post, m_norm_ffn_pre, m_w_ffn_gate, m_w_ffn_up, m_w_ffn_down, m_norm_ffn_post, v_norm_mix_pre, v_w_in, v_b_gate, v_mu_rw, v_w0, v_w_up, v_a0, v_a_up, v_g_up, v_k_k, v_k_a, v_r_k, v_lnx_w, v_lnx_b, v_w_sb_out, v_w_rw_out, v_w_o, v_norm_mix_post, v_norm_ffn_pre, v_w_ffn_gate, v_w_ffn_up, v_w_ffn_down, v_norm_ffn_post):
    args = locals()
    w = {n: args[n] for n in _WEIGHT_NAMES}
    m = {n: args["m_" + n] for n in _WEIGHT_NAMES}
    v = {n: args["v_" + n] for n in _WEIGHT_NAMES}
    return _step(x, loss_target, w, m, v)
```

```python
import functools

import jax
import jax.numpy as jnp
from jax import lax
from jax.experimental import pallas as pl
from jax.experimental.pallas import tpu as pltpu

F32 = jnp.float32
BF16 = jnp.bfloat16

D_MODEL = 1024
HEADS = 8
HEAD_DIM = 64
WIDTH = HEADS * HEAD_DIM
W_LORA, A_LORA, G_LORA = 64, 64, 128
SB_COLS = 3 * WIDTH
RW_COLS = 3 * WIDTH + W_LORA + A_LORA + G_LORA
GATE_COLS = 2 * D_MODEL
D_FF = 2816
RMS_EPS = 1e-6
GN_EPS = HEAD_DIM * 1e-5
N_DEV = 8

ADAM_LR, ADAM_B1, ADAM_B2, ADAM_EPS, ADAM_WD, ADAM_STEP = 0.001, 0.9, 0.999, 1e-08, 0.01, 10

ROW_TILE = 512
ROW_TILE_FFN = 256
SCAN_CHUNK = 64
ATT_ALIGN = 128
ATT_SUM_BWD = 256
ATT_WINDOW = 512
ATT_Q = 256
ATT_PAIRS = 2
SB_DEAD = -104.0
SCAN_SEQS_FWD = 4
SCAN_SEQS_BWD = 2
SCAN_PASSES = 1
GW_TILE_K, GW_TILE_N, GW_ROWS = 1408, 2048, 2048
ADAMW_ROWS, ADAMW_WHOLE_BELOW = 256, 2 ** 19
VMEM_LIMIT = 56 * 2 ** 20

MESH = pl.DeviceIdType.MESH


def _params(sem=None, vmem=VMEM_LIMIT):
    kw = dict(vmem_limit_bytes=vmem)
    if sem is not None:
        kw["dimension_semantics"] = sem
    return pltpu.CompilerParams(**kw)


def _const_spec(shape):
    nd = len(shape)
    return pl.BlockSpec(shape, lambda *_: (0,) * nd, pipeline_mode=pl.Buffered(1))


def _row_spec(tm, n):
    return pl.BlockSpec((tm, n), lambda i: (i, 0))


def _mm(a, b):
    return lax.dot_general(a, b, (((1,), (0,)), ((), ())), preferred_element_type=F32)


def _mm_nt(a, b):
    return lax.dot_general(a, b, (((1,), (1,)), ((), ())), preferred_element_type=F32)


def _mm_tn(a, b):
    return lax.dot_general(a, b, (((0,), (0,)), ((), ())), preferred_element_type=F32)


def _softplus(z):
    return jnp.maximum(z, 0.0) + jnp.log1p(jnp.exp(-jnp.abs(z)))


def _rms_fwd(x, gain):
    rstd = lax.rsqrt(jnp.mean(x * x, axis=-1, keepdims=True) + RMS_EPS)
    xn = x * rstd
    return xn * gain, xn, rstd


def _rms_bwd(dy, xn, rstd, gain):
    u = dy * gain
    dx = rstd * (u - xn * jnp.mean(u * xn, axis=-1, keepdims=True))
    return dx, jnp.sum(dy * xn, axis=0, keepdims=True)


def _acc_out(ref, val, first):
    @pl.when(first)
    def _():
        ref[...] = val

    @pl.when(jnp.logical_not(first))
    def _():
        ref[...] += val


_IN_COLS = SB_COLS + RW_COLS + GATE_COLS
_QKV_OF, _RW_OF, _GATE_OF = slice(0, SB_COLS), slice(SB_COLS, SB_COLS + RW_COLS), slice(SB_COLS + RW_COLS, _IN_COLS)


def _in_proj_fwd(x2, g_pre, w_in, b_gate, tm):
    T = x2.shape[0]

    def body(x_ref, g_ref, w_ref, b_ref, h_ref, qkv_ref, prw_ref, gate_ref):
        h = _rms_fwd(x_ref[...], g_ref[...])[0].astype(BF16)
        h_ref[...] = h
        qkv_ref[...] = _mm(h, w_ref[:, _QKV_OF]).astype(BF16)
        prw_ref[...] = _mm(h, w_ref[:, _RW_OF])
        gate_ref[...] = jax.nn.sigmoid(_mm(h, w_ref[:, _GATE_OF]) + b_ref[...])

    return pl.pallas_call(
        body, name="in_proj_fwd", grid=(T // tm,),
        in_specs=[_row_spec(tm, D_MODEL), _const_spec((1, D_MODEL)), _const_spec((D_MODEL, _IN_COLS)),
                  _const_spec((1, GATE_COLS))],
        out_specs=[_row_spec(tm, D_MODEL), _row_spec(tm, SB_COLS), _row_spec(tm, RW_COLS), _row_spec(tm, GATE_COLS)],
        out_shape=[jax.ShapeDtypeStruct((T, D_MODEL), BF16), jax.ShapeDtypeStruct((T, SB_COLS), BF16),
                   jax.ShapeDtypeStruct((T, RW_COLS), F32), jax.ShapeDtypeStruct((T, GATE_COLS), F32)],
        compiler_params=_params(("parallel",)),
    )(x2, g_pre, w_in, b_gate)


def _in_proj_bwd(x2, g_pre, dx1, dqkv, dprw, dgate, w_in, tm, ride=None):
    T = x2.shape[0]
    steps = T // tm
    nt = 0 if ride is None else len(ride)

    def body(x_ref, g_ref, dx1_ref, dq_ref, dr_ref, dg_ref, w_ref, *rest):
        gx_ref, dgain_ref = rest[nt:nt + 2]
        if ride is not None:
            start, finish = _scatter_steps(rest[:nt], rest[nt + 2:2 * nt + 2], rest[-2], rest[-1])
            pl.when(pl.program_id(0) == 0)(start)
        dh = (_mm_nt(dq_ref[...], w_ref[:, _QKV_OF]) + _mm_nt(dr_ref[...], w_ref[:, _RW_OF])
              + _mm_nt(dg_ref[...], w_ref[:, _GATE_OF]))
        gain = g_ref[...]
        _, xn, rstd = _rms_fwd(x_ref[...], gain)
        dx, dgain = _rms_bwd(dh, xn, rstd, gain)
        gx_ref[...] = dx1_ref[...] + dx
        _acc_out(dgain_ref, dgain, pl.program_id(0) == 0)
        if ride is not None:
            pl.when(pl.program_id(0) == steps - 1)(finish)

    in_specs = [_row_spec(tm, D_MODEL), _const_spec((1, D_MODEL)), _row_spec(tm, D_MODEL), _row_spec(tm, SB_COLS),
                _row_spec(tm, RW_COLS), _row_spec(tm, GATE_COLS), _const_spec((D_MODEL, _IN_COLS))]
    out_specs = [_row_spec(tm, D_MODEL), pl.BlockSpec((1, D_MODEL), lambda i: (0, 0))]
    out_shape = [jax.ShapeDtypeStruct((T, D_MODEL), F32), jax.ShapeDtypeStruct((1, D_MODEL), F32)]
    args, scratch = (x2, g_pre, dx1, dqkv, dprw, dgate, w_in), []
    if ride is not None:
        in_specs, out_specs, args = in_specs + [_ANY] * nt, out_specs + [_ANY] * nt, args + tuple(ride)
        got_shapes, scratch = _scatter_results(ride)
        out_shape = out_shape + got_shapes
    return pl.pallas_call(
        body, name="in_proj_bwd", grid=(steps,), in_specs=in_specs, out_specs=out_specs, out_shape=out_shape,
        scratch_shapes=scratch, compiler_params=_params(("arbitrary",)),
    )(*args)


def _pick_tile(n, cap):
    best = None
    for t in range(128, min(n, cap) + 1, 128):
        if n % t == 0:
            best = t
    return n if best is None else best


def _grad_w(a, b, name, dtype=F32):
    T, K = a.shape
    N = b.shape[1]
    tk, tn, tt = _pick_tile(K, GW_TILE_K), _pick_tile(N, GW_TILE_N), min(T, GW_ROWS)
    steps = T // tt

    def body(a_ref, b_ref, o_ref, *acc):
        t = pl.program_id(2)
        part = _mm_tn(a_ref[...], b_ref[...])
        if not acc:
            _acc_out(o_ref, part, t == 0)
        else:
            _acc_out(acc[0], part, t == 0)

            @pl.when(t == steps - 1)
            def _():
                o_ref[...] = acc[0][...].astype(dtype)

    return pl.pallas_call(
        body, name=name, grid=(K // tk, N // tn, steps),
        in_specs=[pl.BlockSpec((tt, tk), lambda i, j, t: (t, i)), pl.BlockSpec((tt, tn), lambda i, j, t: (t, j))],
        out_specs=pl.BlockSpec((tk, tn), lambda i, j, t: (i, j)),
        out_shape=jax.ShapeDtypeStruct((K, N), dtype),
        scratch_shapes=[] if dtype == F32 else [pltpu.VMEM((tk, tn), F32)],
        compiler_params=_params(("parallel", "parallel", "arbitrary")),
    )(a, b)


def _tri(n, kind):
    r = lax.broadcasted_iota(jnp.int32, (n, n), 0)
    c = lax.broadcasted_iota(jnp.int32, (n, n), 1)
    return {"gt": r > c, "le": r <= c, "lt": r < c, "ge": r >= c}[kind]


def _running_sums(x, carry, tri, kb, reverse=False):
    blocks = range(x.shape[1] // kb)
    parts = {}
    for b in (reversed(blocks) if reverse else blocks):
        piece = x[:, b * kb:(b + 1) * kb]
        parts[b] = carry + _mm(piece.astype(BF16), tri)
        carry = carry + jnp.sum(piece, axis=1, keepdims=True)
    return jnp.concatenate([parts[b] for b in blocks], axis=1), carry


def _sb_valid(row0, col0, first, last, qb, kb):
    ahead = lax.broadcasted_iota(jnp.int32, (qb, kb), 1) - lax.broadcasted_iota(jnp.int32, (qb, kb), 0)
    col = lax.broadcasted_iota(jnp.int32, (1, kb), 1)
    return jnp.logical_and(ahead < row0 - col0, jnp.logical_and(col >= first - col0, col < last - col0))


def _sb_softplus(z):
    return jnp.maximum(z, 0.0) + jnp.log(1.0 + jnp.exp(-jnp.abs(z)))


_PAIR = 2 * HEAD_DIM
_PAIRS = WIDTH // _PAIR


def _first_head_lanes():
    return lax.broadcasted_iota(jnp.int32, (1, _PAIR), 1) < HEAD_DIM


def _per_head(t, first_head):
    zero = jnp.zeros_like(t)
    return jnp.where(first_head, t, zero), jnp.where(first_head, zero, t)


def _sb_fwd(qkv, bl, seq, ride=None):
    qb, win, kb = min(ATT_Q, seq), min(ATT_WINDOW, seq), ATT_ALIGN
    nq = seq // qb
    nh, width, groups = 2 * ATT_PAIRS, ATT_PAIRS * _PAIR, _PAIRS // ATT_PAIRS
    pair_of = lambda h: slice((h // 2) * _PAIR, (h // 2 + 1) * _PAIR)

    steps = bl * groups
    pass_on_at = (5 * steps) // 8
    nt = 0 if ride is None else len(ride)

    def body(q_ref, k_ref, v_ref, *rest):
        g = pl.program_id(0) * groups + pl.program_id(1)
        o_ref, l_ref, stop_ref = rest[nt:nt + 3]
        if ride is not None:
            start, forward, finish = _gather_steps(rest[:nt], rest[nt + 3:2 * nt + 3], *rest[2 * nt + 3:])
            pl.when(g == 0)(start)
            pl.when(g == pass_on_at)(forward)
        first_head = _first_head_lanes()
        u_after = _tri(kb, "gt").astype(BF16)

        def qblock(i, _):
            rows = pl.ds(pl.multiple_of(i * qb, qb), qb)
            qs = q_ref[rows, :] * (HEAD_DIM ** -0.5)
            qh = [_per_head(qs[:, pair_of(h)], first_head)[h % 2] for h in range(nh)]

            def live(carry):
                return jnp.logical_and(carry[0] > 0, carry[3] > 0)

            def window(carry):
                hi, accs, cs, _ = carry
                lo = pl.multiple_of(jnp.maximum(hi - win, 0), kb)
                cols = pl.ds(lo, win)
                kv, vv = k_ref[cols, :], v_ref[cols, :]
                valid = _sb_valid(i * qb, lo, lo, hi, qb, win)
                accs, cs = list(accs), list(cs)
                for h in range(nh):
                    z = _mm_nt(qh[h], kv[:, pair_of(h)])
                    sp = _sb_softplus(z)
                    spm = jnp.where(valid, sp, 0.0)
                    after, cs[h] = _running_sums(spm, cs[h], u_after, kb, reverse=True)
                    w = jnp.where(valid, jnp.exp(z - sp - after), 0.0)
                    accs[h] = accs[h] + _mm(w.astype(BF16), vv[:, pair_of(h)])
                alive = functools.reduce(jnp.minimum, [jnp.min(c) for c in cs]) < -SB_DEAD
                return lo, tuple(accs), tuple(cs), alive.astype(jnp.int32)

            zero_acc, zero_c = jnp.zeros((qb, _PAIR), F32), jnp.zeros((qb, 1), F32)
            lo, accs, cs, _ = lax.while_loop(
                live, window, ((i + 1) * qb, (zero_acc,) * nh, (zero_c,) * nh, jnp.int32(1)))
            for pp in range(ATT_PAIRS):
                o_ref[rows, pp * _PAIR:(pp + 1) * _PAIR] = jnp.where(
                    first_head, accs[2 * pp], accs[2 * pp + 1]).astype(BF16)
            for h in range(nh):
                l_ref[h, rows, :] = cs[h]
            stop_ref[g, i] = lo
            return 0

        lax.fori_loop(0, nq, qblock, 0)
        if ride is not None:
            pl.when(g == steps - 1)(finish)

    col = lambda off: pl.BlockSpec((seq, width), lambda b, p: (b, off + p))
    in_specs = [col(0), col(groups), col(2 * groups)]
    out_specs = [col(0), pl.BlockSpec((None, nh, seq, 1), lambda b, p: (b, p, 0, 0)), pl.BlockSpec(memory_space=pltpu.SMEM)]
    out_shape = [jax.ShapeDtypeStruct((bl * seq, WIDTH), BF16), jax.ShapeDtypeStruct((bl, HEADS, seq, 1), F32),
                 jax.ShapeDtypeStruct((bl * groups, nq), jnp.int32)]
    if ride is None:
        return pl.pallas_call(body, name="sb_fwd", grid=(bl, groups), in_specs=in_specs, out_specs=out_specs,
                              out_shape=out_shape, compiler_params=_params(("arbitrary", "arbitrary")))(qkv, qkv, qkv)
    gathered, sems = _gather_results(ride)
    return pl.pallas_call(
        body, name="sb_fwd", grid=(bl, groups), in_specs=in_specs + [_ANY] * nt, out_specs=out_specs + [_ANY] * nt,
        out_shape=out_shape + gathered, scratch_shapes=sems, compiler_params=_params(("arbitrary", "arbitrary")),
    )(qkv, qkv, qkv, *ride)


def _sb_bwd(qkv, do, lsum, stop, bl, seq):
    qb, win, kb = min(ATT_Q, seq), min(ATT_WINDOW, seq), ATT_ALIGN
    nq = seq // qb
    nh, width, groups = 2 * ATT_PAIRS, ATT_PAIRS * _PAIR, _PAIRS // ATT_PAIRS
    pair_of = lambda h: slice((h // 2) * _PAIR, (h // 2 + 1) * _PAIR)

    def body(stop_ref, q_ref, k_ref, v_ref, do_ref, l_ref, dq_ref, dk_ref, dv_ref, dk_acc, dv_acc):
        g = pl.program_id(0) * groups + pl.program_id(1)
        first_head = _first_head_lanes()
        sb = min(ATT_SUM_BWD, win)
        u_incl = _tri(sb, "le").astype(BF16)
        u_excl = _tri(sb, "lt").astype(BF16)
        dk_acc[...] = jnp.zeros_like(dk_acc)
        dv_acc[...] = jnp.zeros_like(dv_acc)

        def qblock(i, _):
            rows = pl.ds(pl.multiple_of(i * qb, qb), qb)
            qv = q_ref[rows, :]
            qs = qv * (HEAD_DIM ** -0.5)
            dob = do_ref[rows, :]
            qh = [_per_head(qs[:, pair_of(h)], first_head)[h % 2] for h in range(nh)]
            doh = [_per_head(dob[:, pair_of(h)], first_head)[h % 2] for h in range(nh)]
            ltot = [l_ref[h, rows, :] for h in range(nh)]

            first = (jnp.clip(stop_ref[g, i], 0, i * qb) // ATT_ALIGN) * ATT_ALIGN

            def window(n, carry):
                dqs, ps, es = (list(t) for t in carry)
                start = first + n * win
                lo = pl.multiple_of(jnp.minimum(start, seq - win), kb)
                cols = pl.ds(lo, win)
                kv, vv = k_ref[cols, :], v_ref[cols, :]
                valid = _sb_valid(i * qb, lo, start, seq, qb, win)
                dks, dvs = [], []
                for h in range(nh):
                    kp, vp = kv[:, pair_of(h)], vv[:, pair_of(h)]
                    z = _mm_nt(qh[h], kp)
                    sp = _sb_softplus(z)
                    spm = jnp.where(valid, sp, 0.0)
                    upto, ps[h] = _running_sums(spm, ps[h], u_incl, sb)
                    w = jnp.where(valid, jnp.exp(z - sp - (ltot[h] - upto)), 0.0)
                    e = _mm_nt(doh[h], vp) * w
                    dlf, es[h] = _running_sums(e, es[h], u_excl, sb)
                    sig = jnp.exp(z - sp)
                    dz = jnp.where(valid, e * (1.0 - sig) - dlf * sig, 0.0) * (HEAD_DIM ** -0.5)
                    dzb = dz.astype(BF16)
                    dvs.append(_mm_tn(w.astype(BF16), dob[:, pair_of(h)]))
                    dks.append(_mm_tn(dzb, qv[:, pair_of(h)]))
                    dqs[h] = dqs[h] + _mm(dzb, kp)
                for pp in range(ATT_PAIRS):
                    lanes = slice(pp * _PAIR, (pp + 1) * _PAIR)
                    dv_acc[cols, lanes] += jnp.where(first_head, dvs[2 * pp], dvs[2 * pp + 1])
                    dk_acc[cols, lanes] += jnp.where(first_head, dks[2 * pp], dks[2 * pp + 1])
                return tuple(dqs), tuple(ps), tuple(es)

            zero_q, zero_c = jnp.zeros((qb, _PAIR), F32), jnp.zeros((qb, 1), F32)
            dqs, _, _ = lax.fori_loop(0, ((i + 1) * qb - first + win - 1) // win, window,
                                      ((zero_q,) * nh, (zero_c,) * nh, (zero_c,) * nh))
            for pp in range(ATT_PAIRS):
                dq_ref[rows, pp * _PAIR:(pp + 1) * _PAIR] = jnp.where(
                    first_head, dqs[2 * pp], dqs[2 * pp + 1]).astype(BF16)
            return 0

        lax.fori_loop(0, nq, qblock, 0)
        dk_ref[...] = dk_acc[...].astype(BF16)
        dv_ref[...] = dv_acc[...].astype(BF16)

    col = lambda off: pl.BlockSpec((seq, width), lambda b, p, stop_ref: (b, off + p))
    return pl.pallas_call(
        body, name="sb_bwd",
        grid_spec=pltpu.PrefetchScalarGridSpec(
            num_scalar_prefetch=1, grid=(bl, groups),
            in_specs=[col(0), col(groups), col(2 * groups), col(0),
                      pl.BlockSpec((None, nh, seq, 1), lambda b, p, stop_ref: (b, p, 0, 0))],
            out_specs=[col(0), col(0), col(0)],
            scratch_shapes=[pltpu.VMEM((seq, width), F32), pltpu.VMEM((seq, width), F32)]),
        out_shape=[jax.ShapeDtypeStruct((bl * seq, WIDTH), BF16)] * 3,
        compiler_params=_params(("parallel", "parallel")),
    )(stop, qkv, qkv, qkv, do, lsum)


@jax.custom_vjp
def _lora_mm(x, w):
    return _mm(x.astype(BF16), w.astype(BF16))


_lora_mm.defvjp(
    lambda x, w: (_mm(x.astype(BF16), w.astype(BF16)), (x, w)),
    lambda res, ct: (_mm_nt(ct.astype(BF16), res[1].astype(BF16)), _mm_tn(res[0].astype(BF16), ct.astype(BF16))))


def _rw_prep_math(p, ps, mu, w0, w_up, a0, a_up, g_up, k_k, k_a):
    pm = p + (ps - p) * mu
    r, k, v = pm[:, :WIDTH], pm[:, WIDTH:2 * WIDTH], pm[:, 2 * WIDTH:3 * WIDTH]
    o = 3 * WIDTH
    xw, xa, xg = pm[:, o:o + W_LORA], pm[:, o + W_LORA:o + W_LORA + A_LORA], pm[:, o + W_LORA + A_LORA:]
    w_raw = w0 + _lora_mm(jnp.tanh(xw), w_up)
    lw = -jnp.exp(-_softplus(-w_raw) - 0.5)
    a = jax.nn.sigmoid(a0 + _lora_mm(xa, a_up))
    g = _lora_mm(jax.nn.sigmoid(xg), g_up)
    kk = k * k_k
    k2 = k * (1.0 + (a - 1.0) * k_a)
    return r, lw, k2, v, kk, a, g


def _shift_down(p, first_row):
    row = lax.broadcasted_iota(jnp.int32, p.shape, 0)
    return jnp.where(row == 0, first_row, pltpu.roll(p, 1, 0))


def _shift_up(p, last_row):
    row = lax.broadcasted_iota(jnp.int32, p.shape, 0)
    return jnp.where(row == p.shape[0] - 1, last_row, pltpu.roll(p, p.shape[0] - 1, 0))


_PREP_PARAM_SHAPES = [(1, RW_COLS), (1, WIDTH), (W_LORA, WIDTH), (1, WIDTH), (A_LORA, WIDTH), (G_LORA, WIDTH),
                      (1, WIDTH), (1, WIDTH)]


def _prev_rows_spec(tm):
    return pl.BlockSpec((8, RW_COLS), lambda i: (jnp.maximum(i * (tm // 8) - 1, 0), 0))


def _head_spec(tm, seq, tile_of=lambda i: i):
    per_seq = seq // tm
    return pl.BlockSpec((None, HEADS, tm, HEAD_DIM),
                        lambda i: (tile_of(i) // per_seq, 0, tile_of(i) % per_seq, 0))


def _split_heads(val, ref):
    for h in range(HEADS):
        ref[h] = val[:, h * HEAD_DIM:(h + 1) * HEAD_DIM]


def _join_heads(ref):
    return jnp.concatenate([ref[h] for h in range(HEADS)], axis=1)


def _rw_prep_fwd(prw, params, seq, tm):
    T = prw.shape[0]

    def body(p_ref, prev_ref, *rest):
        prm = [r_[...] for r_ in rest[:8]]
        outs = rest[8:]
        i = pl.program_id(0)
        first = jnp.where((i * tm) % seq == 0, 0.0, prev_ref[7:8, :])
        p = p_ref[...]
        vals = _rw_prep_math(p, _shift_down(p, first), *prm)
        for o_ref, val in zip(outs[:6], vals[:6]):
            _split_heads(val, o_ref)
        outs[6][...] = vals[6]

    by_head = jax.ShapeDtypeStruct((T // seq, HEADS, seq, HEAD_DIM), F32)
    return pl.pallas_call(
        body, name="rw_prep_fwd", grid=(T // tm,),
        in_specs=[_row_spec(tm, RW_COLS), _prev_rows_spec(tm)] + [_const_spec(s) for s in _PREP_PARAM_SHAPES],
        out_specs=[_head_spec(tm, seq)] * 6 + [_row_spec(tm, WIDTH)],
        out_shape=[by_head] * 6 + [jax.ShapeDtypeStruct((T, WIDTH), F32)],
        compiler_params=_params(("parallel",)),
    )(prw, prw, *params)


def _rw_prep_bwd(prw, params, cts, seq, tm):
    T = prw.shape[0]
    n = T // tm

    def body(p_ref, prev_ref, *rest):
        prm = [r_[...] for r_ in rest[:8]]
        ct = tuple(_join_heads(r_) for r_ in rest[8:14]) + (rest[14][...],)
        dp_ref = rest[15]
        dprm_refs = rest[16:24]
        carry = rest[24]
        step = pl.program_id(0)
        i = n - 1 - step
        first = jnp.where((i * tm) % seq == 0, 0.0, prev_ref[7:8, :])
        p = p_ref[...]
        _, vjp = jax.vjp(_rw_prep_math, p, _shift_down(p, first), *prm)
        grads = vjp(ct)
        dp, dps = grads[0], grads[1]
        nxt = jnp.where(jnp.logical_or(step == 0, ((i + 1) * tm) % seq == 0), 0.0, carry[0:1, :])
        dp_ref[...] = (dp + _shift_up(dps, nxt)).astype(BF16)
        carry[...] = dps[0:8, :]
        for ref, gval in zip(dprm_refs, grads[2:]):
            _acc_out(ref, gval, step == 0)

    rev = lambda w: pl.BlockSpec((tm, w), lambda s: (n - 1 - s, 0))
    prev = pl.BlockSpec((8, RW_COLS), lambda s: (jnp.maximum((n - 1 - s) * (tm // 8) - 1, 0), 0))
    return pl.pallas_call(
        body, name="rw_prep_bwd", grid=(n,),
        in_specs=([rev(RW_COLS), prev] + [_const_spec(s) for s in _PREP_PARAM_SHAPES]
                  + [_head_spec(tm, seq, lambda s: n - 1 - s)] * 6 + [rev(WIDTH)]),
        out_specs=[rev(RW_COLS)] + [pl.BlockSpec(s, lambda s_: (0, 0)) for s in _PREP_PARAM_SHAPES],
        out_shape=[jax.ShapeDtypeStruct((T, RW_COLS), BF16)] + [jax.ShapeDtypeStruct(s, F32) for s in _PREP_PARAM_SHAPES],
        scratch_shapes=[pltpu.VMEM((8, RW_COLS), F32)],
        compiler_params=_params(("arbitrary",)),
    )(prw, prw, *params, *cts)


def _make_bmm(passes):
    def raw(dn, a, b):
        d = lambda x, y: lax.dot_general(x, y, dn, preferred_element_type=F32)
        ah = a.astype(BF16)
        bh = b.astype(BF16)
        if passes == 1:
            return d(ah, bh)
        al = (a - ah.astype(F32)).astype(BF16)
        bl = (b - bh.astype(F32)).astype(BF16)
        return d(ah, bh) + (d(ah, bl) + d(al, bh))

    dn_nn = (((2,), (1,)), ((0,), (0,)))
    dn_nt = (((2,), (2,)), ((0,), (0,)))
    dn_tn = (((1,), (1,)), ((0,), (0,)))

    @jax.custom_vjp
    def nn(a, b):
        return raw(dn_nn, a, b)

    @jax.custom_vjp
    def nt(a, b):
        return raw(dn_nt, a, b)

    @jax.custom_vjp
    def tn(a, b):
        return raw(dn_tn, a, b)

    nn.defvjp(lambda a, b: (raw(dn_nn, a, b), (a, b)), lambda res, ct: (nt(ct, res[1]), tn(res[0], ct)))
    nt.defvjp(lambda a, b: (raw(dn_nt, a, b), (a, b)), lambda res, ct: (nn(ct, res[1]), tn(ct, res[0])))
    tn.defvjp(lambda a, b: (raw(dn_tn, a, b), (a, b)), lambda res, ct: (nt(res[1], ct), nn(res[0], ct)))

    def unit_lower_inverse(m):
        n = m.shape[-1]
        row = lax.broadcasted_iota(jnp.int32, (n, n), 0)
        col = lax.broadcasted_iota(jnp.int32, (n, n), 1)
        m16 = ((row // 16) == (col // 16)).astype(F32)
        m32 = ((row // 32) == (col // 32)).astype(F32)
        a1 = m * m16
        a2 = nn(a1, a1)
        a4 = nn(a2, a2)
        a8 = nn(a4, a4)
        inv = (row == col).astype(F32) - a1
        inv = inv + nn(inv, a2)
        inv = inv + nn(inv, a4)
        inv = inv + nn(inv, a8)
        inv = inv - nn(nn(inv, m * (m32 - m16)), inv)
        return inv - nn(nn(inv, m * (1.0 - m32)), inv)

    @jax.custom_vjp
    def inverse(m):
        return unit_lower_inverse(m)

    def inverse_fwd(m):
        inv = unit_lower_inverse(m)
        return inv, inv

    inverse.defvjp(inverse_fwd, lambda inv, ct: (-nt(tn(inv, ct), inv),))
    return nn, nt, tn, inverse


def _wkv_chunk(s0, r, lw, k, v, kk, a, lnw, lnb, rk):
    nn, nt, tn, inverse = _make_bmm(SCAN_PASSES)
    G, L, N = r.shape
    rep = lambda t: jnp.broadcast_to(t[None], (G // HEADS, HEADS, 1, N)).reshape(G, 1, N)
    kap = kk * lax.rsqrt(jnp.maximum(jnp.sum(kk * kk, axis=-1, keepdims=True), 1e-24))
    b = a * kap
    row = lax.broadcasted_iota(jnp.int32, (L, L), 0)
    col = lax.broadcasted_iota(jnp.int32, (L, L), 1)
    low_incl = (col <= row).astype(F32)
    low_strict = (col < row).astype(F32)
    c = _make_bmm(3)[0](jnp.broadcast_to(low_incl[None], (G, L, L)), lw)
    c_all = jnp.sum(lw, axis=1, keepdims=True)
    g_inv = jnp.exp(-c)
    kap_t = kap * jnp.exp(c - lw)
    b_t = b * g_inv
    k_t = k * g_inv
    r_t = r * jnp.exp(c)
    g_all = jnp.exp(c_all)
    m_b = nt(kap_t, b_t) * low_strict
    m_k = nt(kap_t, k_t) * low_strict
    n_b = nt(r_t, b_t) * low_incl
    n_k = nt(r_t, k_t) * low_incl
    rhs = -(nt(kap_t, s0) + nn(m_k, v))
    sa = nn(inverse(m_b), rhs)
    y = nt(r_t, s0) + nn(n_b, sa) + nn(n_k, v)
    s1 = s0 * g_all + tn(sa, b_t * g_all) + tn(v, k_t * g_all)
    mean = jnp.mean(y, axis=-1, keepdims=True)
    yc = y - mean
    var = jnp.mean(yc * yc, axis=-1, keepdims=True)
    out = yc * lax.rsqrt(var + GN_EPS) * rep(lnw) + rep(lnb)
    out = out + jnp.sum(r * k * rep(rk), axis=-1, keepdims=True) * v
    return out, s1


def _scan_heads_per_step(total_heads, seqs_wanted):
    n_seq = total_heads // HEADS
    return HEADS * max(d for d in range(1, seqs_wanted + 1) if n_seq % d == 0)


def _wkv_fwd(seqs, lnw, lnb, rk):
    G, S, N = seqs[0].shape
    L = SCAN_CHUNK
    nc = S // L

    def body(*refs):
        ins = [r_[...] for r_ in refs[:6]]
        prm = [r_[...] for r_ in refs[6:9]]
        out_ref, st_ref, state = refs[9], refs[10], refs[11]

        @pl.when(pl.program_id(1) == 0)
        def _():
            state[...] = jnp.zeros_like(state)

        s0 = state[...]
        st_ref[...] = s0
        out, s1 = _wkv_chunk(s0, *ins, *prm)
        out_ref[...] = out
        state[...] = s1

    gb = _scan_heads_per_step(G, SCAN_SEQS_FWD)
    blk = pl.BlockSpec((gb, L, N), lambda b, i: (b, i, 0))
    pspec = _const_spec((HEADS, 1, N))
    return pl.pallas_call(
        body, name="wkv_fwd", grid=(G // gb, nc), in_specs=[blk] * 6 + [pspec] * 3,
        out_specs=[blk, pl.BlockSpec((None, gb, N, N), lambda b, i: (i, b, 0, 0))],
        out_shape=[jax.ShapeDtypeStruct((G, S, N), F32), jax.ShapeDtypeStruct((nc, G, N, N), F32)],
        scratch_shapes=[pltpu.VMEM((gb, N, N), F32)],
        compiler_params=_params(("parallel", "arbitrary")),
    )(*seqs, lnw, lnb, rk)


def _wkv_bwd(seqs, states, dout, lnw, lnb, rk, ride=None):
    G, S, N = seqs[0].shape
    L = SCAN_CHUNK
    nc = S // L
    gb = _scan_heads_per_step(G, SCAN_SEQS_BWD)
    nt = 0 if ride is None else len(ride)

    def body(*refs):
        ins = [r_[...] for r_ in refs[:6]]
        s0 = refs[6][...]
        ct_out = refs[7][...]
        prm = [r_[...] for r_ in refs[8:11]]
        refs = refs[11:]
        if ride is not None:
            start, finish = _scatter_steps(refs[:nt], refs[nt + 9:2 * nt + 9], refs[-2], refs[-1])
            pl.when(jnp.logical_and(pl.program_id(0) == 0, pl.program_id(1) == 0))(start)
            refs = refs[nt:]
        d_refs = refs[0:6]
        dprm_refs = refs[6:9]
        dstate = refs[9 + nt]
        step = pl.program_id(1)

        @pl.when(step == 0)
        def _():
            dstate[...] = jnp.zeros_like(dstate)

        _, vjp = jax.vjp(_wkv_chunk, s0, *ins, *prm)
        grads = vjp((ct_out, dstate[...]))
        dstate[...] = grads[0]
        for ref, gval in zip(d_refs, grads[1:7]):
            ref[...] = gval
        for ref, gval in zip(dprm_refs, grads[7:]):
            _acc_out(ref, gval, jnp.logical_and(step == 0, pl.program_id(0) == 0))
        if ride is not None:
            pl.when(jnp.logical_and(pl.program_id(0) == G // gb - 1, step == nc - 1))(finish)

    blk = pl.BlockSpec((gb, L, N), lambda b, s: (b, nc - 1 - s, 0))
    pspec = _const_spec((HEADS, 1, N))
    pout = pl.BlockSpec((HEADS, 1, N), lambda b, s: (0, 0, 0))
    in_specs = [blk] * 6 + [pl.BlockSpec((None, gb, N, N), lambda b, s: (nc - 1 - s, b, 0, 0)), blk] + [pspec] * 3
    out_specs = [blk] * 6 + [pout] * 3
    out_shape = [jax.ShapeDtypeStruct((G, S, N), F32)] * 6 + [jax.ShapeDtypeStruct((HEADS, 1, N), F32)] * 3
    scratch = [pltpu.VMEM((gb, N, N), F32)]
    args = (*seqs, states, dout, lnw, lnb, rk)
    if ride is not None:
        in_specs, out_specs, args = in_specs + [_ANY] * nt, out_specs + [_ANY] * nt, args + tuple(ride)
        got_shapes, sems = _scatter_results(ride)
        out_shape, scratch = out_shape + got_shapes, scratch + sems
    return pl.pallas_call(
        body, name="wkv_bwd", grid=(G // gb, nc), in_specs=in_specs, out_specs=out_specs, out_shape=out_shape,
        scratch_shapes=scratch, compiler_params=_params(("arbitrary", "arbitrary")),
    )(*args)


def _merge_math(o_sb, rw_out, g_rw, gates, w_sb, w_rw, w_o):
    o_rw = (rw_out * g_rw).astype(BF16)
    a = _mm(o_sb, w_sb)
    b = _mm(o_rw, w_rw)
    g1, g2 = gates[:, :D_MODEL], gates[:, D_MODEL:]
    merged = (g1 * a + g2 * b).astype(BF16)
    return o_rw, a, b, g1, g2, merged, _mm(merged, w_o)


def _merge_fwd(x2, o_sb, rw_out, g_rw, gates, w_sb, w_rw, w_o, g_post, seq, tm):
    T = x2.shape[0]

    def body(x_ref, osb_ref, rw_ref, g_ref, gate_ref, wsb_ref, wrw_ref, wo_ref, gp_ref, x1_ref):
        z = _merge_math(osb_ref[...], _join_heads(rw_ref), g_ref[...], gate_ref[...], wsb_ref[...], wrw_ref[...], wo_ref[...])[-1]
        x1_ref[...] = x_ref[...] + _rms_fwd(z, gp_ref[...])[0]

    return pl.pallas_call(
        body, name="merge_fwd", grid=(T // tm,),
        in_specs=[_row_spec(tm, D_MODEL), _row_spec(tm, WIDTH), _head_spec(tm, seq), _row_spec(tm, WIDTH),
                  _row_spec(tm, GATE_COLS), _const_spec((WIDTH, D_MODEL)), _const_spec((WIDTH, D_MODEL)),
                  _const_spec((D_MODEL, D_MODEL)), _const_spec((1, D_MODEL))],
        out_specs=_row_spec(tm, D_MODEL),
        out_shape=jax.ShapeDtypeStruct((T, D_MODEL), F32),
        compiler_params=_params(("parallel",)),
    )(x2, o_sb, rw_out, g_rw, gates, w_sb, w_rw, w_o, g_post)


def _merge_bwd(dx1, o_sb, rw_out, g_rw, gates, w_sb, w_rw, w_o, g_post, seq, tm):
    T = dx1.shape[0]

    def body(dx1_ref, osb_ref, rw_ref, g_ref, gate_ref, wsb_ref, wrw_ref, wo_ref, gp_ref,
             orw_o, mrg_o, dz_o, da_o, db_o, dgate_o, dosb_o, drw_o, dg_o, dgp_o, dbg_o):
        rw_out_v, g_rw_v = _join_heads(rw_ref), g_ref[...]
        w_sb_v, w_rw_v, w_o_v = wsb_ref[...], wrw_ref[...], wo_ref[...]
        o_rw, a, b, g1, g2, merged, z = _merge_math(osb_ref[...], rw_out_v, g_rw_v, gate_ref[...], w_sb_v, w_rw_v, w_o_v)
        gain = gp_ref[...]
        _, zn, rstd = _rms_fwd(z, gain)
        dz, dgain = _rms_bwd(dx1_ref[...], zn, rstd, gain)
        dzb = dz.astype(BF16)
        dm = _mm_nt(dzb, w_o_v)
        dab = (dm * g1).astype(BF16)
        dbb = (dm * g2).astype(BF16)
        dgate = jnp.concatenate([dm * a * g1 * (1.0 - g1), dm * b * g2 * (1.0 - g2)], axis=1)
        do_rw = _mm_nt(dbb, w_rw_v)
        orw_o[...] = o_rw
        mrg_o[...] = merged
        dz_o[...] = dzb
        da_o[...] = dab
        db_o[...] = dbb
        dgate_o[...] = dgate.astype(BF16)
        dosb_o[...] = _mm_nt(dab, w_sb_v).astype(BF16)
        _split_heads(do_rw * g_rw_v, drw_o)
        dg_o[...] = do_rw * rw_out_v
        first = pl.program_id(0) == 0
        _acc_out(dgp_o, dgain, first)
        _acc_out(dbg_o, jnp.sum(dgate, axis=0, keepdims=True), first)

    acc = lambda n: pl.BlockSpec((1, n), lambda i: (0, 0))
    sd = jax.ShapeDtypeStruct
    return pl.pallas_call(
        body, name="merge_bwd", grid=(T // tm,),
        in_specs=[_row_spec(tm, D_MODEL), _row_spec(tm, WIDTH), _head_spec(tm, seq), _row_spec(tm, WIDTH),
                  _row_spec(tm, GATE_COLS), _const_spec((WIDTH, D_MODEL)), _const_spec((WIDTH, D_MODEL)),
                  _const_spec((D_MODEL, D_MODEL)), _const_spec((1, D_MODEL))],
        out_specs=[_row_spec(tm, WIDTH), _row_spec(tm, D_MODEL), _row_spec(tm, D_MODEL), _row_spec(tm, D_MODEL),
                   _row_spec(tm, D_MODEL), _row_spec(tm, GATE_COLS), _row_spec(tm, WIDTH), _head_spec(tm, seq),
                   _row_spec(tm, WIDTH), acc(D_MODEL), acc(GATE_COLS)],
        out_shape=[sd((T, WIDTH), BF16), sd((T, D_MODEL), BF16), sd((T, D_MODEL), BF16), sd((T, D_MODEL), BF16),
                   sd((T, D_MODEL), BF16), sd((T, GATE_COLS), BF16), sd((T, WIDTH), BF16),
                   sd((T // seq, HEADS, seq, HEAD_DIM), F32), sd((T, WIDTH), F32), sd((1, D_MODEL), F32),
                   sd((1, GATE_COLS), F32)],
        compiler_params=_params(("arbitrary",)),
    )(dx1, o_sb, rw_out, g_rw, gates, w_sb, w_rw, w_o, g_post)


def _ffn(x1, target, g_pre, g_post, w_gate, w_up, w_down, tm):
    T = x1.shape[0]

    def body(x1_ref, tgt_ref, gpre_ref, gpost_ref, wg_ref, wu_ref, wd_ref,
             loss_o, dx1_o, h_o, dgate_o, dup_o, act_o, df_o, dgpre_o, dgpost_o):
        x1v = x1_ref[...]
        gpre, gpost = gpre_ref[...], gpost_ref[...]
        wg, wu, wd = wg_ref[...], wu_ref[...], wd_ref[...]
        hn, xn1, rstd1 = _rms_fwd(x1v, gpre)
        h = hn.astype(BF16)
        gate = _mm(h, wg)
        up = _mm(h, wu)
        sg = jax.nn.sigmoid(gate)
        act = (gate * sg * up).astype(BF16)
        f = _mm(act, wd)
        fo, fn, rstd2 = _rms_fwd(f, gpost)
        diff = x1v + fo - tgt_ref[...]
        dy = diff * (1.0 / D_MODEL)
        df, dgpost = _rms_bwd(dy, fn, rstd2, gpost)
        dfb = df.astype(BF16)
        dact = _mm_nt(dfb, wd)
        dup = (dact * gate * sg).astype(BF16)
        dgate = (dact * up * (sg * (1.0 + gate * (1.0 - sg)))).astype(BF16)
        dh = _mm_nt(dgate, wg) + _mm_nt(dup, wu)
        dxn, dgpre = _rms_bwd(dh, xn1, rstd1, gpre)
        dx1_o[...] = dy + dxn
        h_o[...] = h
        dgate_o[...] = dgate
        dup_o[...] = dup
        act_o[...] = act
        df_o[...] = dfb
        first = pl.program_id(0) == 0
        part = jnp.sum(jnp.sum(diff * diff, axis=1, keepdims=True), axis=0, keepdims=True) * (0.5 / D_MODEL)
        _acc_out(loss_o, jnp.broadcast_to(part, (8, 128)), first)
        _acc_out(dgpre_o, dgpre, first)
        _acc_out(dgpost_o, dgpost, first)

    acc = lambda r, n: pl.BlockSpec((r, n), lambda i: (0, 0))
    sd = jax.ShapeDtypeStruct
    return pl.pallas_call(
        body, name="ffn", grid=(T // tm,),
        in_specs=[_row_spec(tm, D_MODEL), _row_spec(tm, D_MODEL), _const_spec((1, D_MODEL)), _const_spec((1, D_MODEL)),
                  _const_spec((D_MODEL, D_FF)), _const_spec((D_MODEL, D_FF)), _const_spec((D_FF, D_MODEL))],
        out_specs=[acc(8, 128), _row_spec(tm, D_MODEL), _row_spec(tm, D_MODEL), _row_spec(tm, D_FF), _row_spec(tm, D_FF),
                   _row_spec(tm, D_FF), _row_spec(tm, D_MODEL), acc(1, D_MODEL), acc(1, D_MODEL)],
        out_shape=[sd((8, 128), F32), sd((T, D_MODEL), F32), sd((T, D_MODEL), BF16), sd((T, D_FF), BF16),
                   sd((T, D_FF), BF16), sd((T, D_FF), BF16), sd((T, D_MODEL), BF16), sd((1, D_MODEL), F32),
                   sd((1, D_MODEL), F32)],
        compiler_params=_params(("arbitrary",)),
    )(x1, target, g_pre, g_post, w_gate, w_up, w_down)


def _local_step(x, target, sm, wt, late=None):
    bl, seq, _ = x.shape
    T = bl * seq
    tm = min(ROW_TILE, T)
    x2 = x.reshape(T, D_MODEL)
    tgt2 = target.reshape(T, D_MODEL)
    h, qkv, prw, gates = _in_proj_fwd(x2, sm["norm_mix_pre"], wt["w_in"], sm["b_gate"], tm)
    if late is None:
        o_sb, lsum, sb_stop = _sb_fwd(qkv, bl, seq)
    else:
        o_sb, lsum, sb_stop, *gathered = _sb_fwd(qkv, bl, seq, late)
        wt = {**wt, **_whole_weights(gathered, slice(_EARLY, None))}
    prep_params = [sm["mu_rw"], sm["w0"], wt["w_up"].astype(F32), sm["a0"], wt["a_up"].astype(F32),
                   wt["g_up"].astype(F32), sm["k_k"], sm["k_a"]]
    prep = _rw_prep_fwd(prw, prep_params, seq, tm)
    by_head = lambda t: t.reshape(bl, HEADS, seq, HEAD_DIM)
    seqs = [t.reshape(bl * HEADS, seq, HEAD_DIM) for t in prep[:6]]
    g_rw = prep[6]
    lnw, lnb, rk = (sm[n].reshape(HEADS, 1, HEAD_DIM) for n in ("lnx_w", "lnx_b", "r_k"))
    rw_out_h, states = _wkv_fwd(seqs, lnw, lnb, rk)
    rw_out = by_head(rw_out_h)
    x1 = _merge_fwd(x2, o_sb, rw_out, g_rw, gates, wt["w_sb_out"], wt["w_rw_out"], wt["w_o"], sm["norm_mix_post"],
                    seq, tm)
    (loss_part, dx1, h2, dffg, dffu, act, dff, d_nfpre, d_nfpost) = _ffn(
        x1, tgt2, sm["norm_ffn_pre"], sm["norm_ffn_post"], wt["w_ffn_gate"], wt["w_ffn_up"], wt["w_ffn_down"],
        min(ROW_TILE_FFN, T))
    (o_rw, merged, dz, da, db, dgate, do_sb, d_rw_out, d_g_rw, d_npost, d_bgate) = _merge_bwd(
        dx1, o_sb, rw_out, g_rw, gates, wt["w_sb_out"], wt["w_rw_out"], wt["w_o"], sm["norm_mix_post"], seq, tm)
    gdt = F32 if late is None else BF16
    gw = {
        "w_sb_out": _grad_w(o_sb, da, "gw_sb_out", gdt), "w_rw_out": _grad_w(o_rw, db, "gw_rw_out", gdt),
        "w_o": _grad_w(merged, dz, "gw_o", gdt),
        "w_ffn_gate": _grad_w(h2, dffg, "gw_ffn_gate", gdt), "w_ffn_up": _grad_w(h2, dffu, "gw_ffn_up", gdt),
        "w_ffn_down": _grad_w(act, dff, "gw_ffn_down", gdt),
    }
    dqkv = jnp.concatenate(_sb_bwd(qkv, do_sb, lsum, sb_stop, bl, seq), axis=1)
    ride = None if late is None else _blocks_by_owner(gw, slice(_EARLY, None))
    wkv_g = _wkv_bwd(seqs, states, d_rw_out.reshape(bl * HEADS, seq, HEAD_DIM), lnw, lnb, rk, ride)
    scattered = {} if late is None else {n: (p, g) for (n, _, _), p, g in zip(_SHARDED[_EARLY:], ride, wkv_g[9:])}
    cts = [by_head(t) for t in wkv_g[:6]] + [d_g_rw]
    prep_g = _rw_prep_bwd(prw, prep_params, cts, seq, tm)
    dprw = prep_g[0]
    d_mu, d_w0, d_wup, d_a0, d_aup, d_gup, d_kk, d_ka = prep_g[1:]
    gw = {
        **gw,
        "w_in": jnp.concatenate([_grad_w(h, dqkv, "gw_in_qkv", gdt), _grad_w(h, dprw, "gw_in_rw", gdt),
                                 _grad_w(h, dgate, "gw_in_gate", gdt)], axis=1),
        "w_up": d_wup.astype(gdt), "a_up": d_aup.astype(gdt), "g_up": d_gup.astype(gdt),
    }
    ride = None if late is None else _blocks_by_owner(gw, slice(0, _EARLY))
    grad_x, d_npre, *got = _in_proj_bwd(x2, sm["norm_mix_pre"], dx1, dqkv, dprw, dgate, wt["w_in"], tm, ride)
    if late is not None:
        scattered.update({n: (p, g) for (n, _, _), p, g in zip(_SHARDED[:_EARLY], ride, got)})
    gs = {
        "norm_mix_pre": d_npre, "b_gate": d_bgate, "mu_rw": d_mu, "w0": d_w0, "a0": d_a0, "k_k": d_kk, "k_a": d_ka,
        "r_k": wkv_g[8].reshape(1, WIDTH), "lnx_w": wkv_g[6].reshape(1, WIDTH), "lnx_b": wkv_g[7].reshape(1, WIDTH),
        "norm_mix_post": d_npost, "norm_ffn_pre": d_nfpre, "norm_ffn_post": d_nfpost,
    }
    return loss_part, grad_x.reshape(x.shape), gw, gs, scattered


_SHARDED = [("w_in", 1, (D_MODEL, (SB_COLS + RW_COLS + GATE_COLS) // N_DEV)), ("w_up", 1, (W_LORA, WIDTH // N_DEV)),
            ("a_up", 1, (A_LORA, WIDTH // N_DEV)), ("g_up", 1, (G_LORA, WIDTH // N_DEV)),
            ("w_sb_out", 1, (WIDTH, D_MODEL // N_DEV)), ("w_rw_out", 1, (WIDTH, D_MODEL // N_DEV)),
            ("w_o", 0, (D_MODEL // N_DEV, D_MODEL)), ("w_ffn_gate", 1, (D_MODEL, D_FF // N_DEV)),
            ("w_ffn_up", 1, (D_MODEL, D_FF // N_DEV)), ("w_ffn_down", 0, (D_FF // N_DEV, D_MODEL))]
_LANES = 128
_SMALL = [("norm_mix_pre", D_MODEL), ("b_gate", GATE_COLS), ("mu_rw", RW_COLS), ("w0", WIDTH), ("a0", WIDTH),
          ("k_k", WIDTH), ("k_a", WIDTH), ("r_k", WIDTH), ("lnx_w", WIDTH), ("lnx_b", WIDTH),
          ("norm_mix_post", D_MODEL), ("norm_ffn_pre", D_MODEL), ("norm_ffn_post", D_MODEL)]
_SMALL_ROWS = 96


_EARLY = 4


def _whole_weights(gathered, which):
    return {n: blk.reshape(N_DEV * shp[0], shp[1]) if axis == 0 else blk.transpose(1, 0, 2).reshape(shp[0], N_DEV * shp[1])
            for (n, axis, shp), blk in zip(_SHARDED[which], gathered)}


def _blocks_by_owner(gw, which):
    return [gw[n].reshape((N_DEV,) + shp) if axis == 0 else gw[n].reshape(shp[0], N_DEV, shp[1]).transpose(1, 0, 2)
            for n, axis, shp in _SHARDED[which]]


def _pack_small(vals, extra=None):
    used = sum(sz for _, sz in _SMALL)
    tail = jnp.zeros((1, _SMALL_ROWS * _LANES - used), F32).at[0, 0].set(extra)
    return jnp.concatenate([vals[n].reshape(1, -1) for n, _ in _SMALL] + [tail], axis=1)


_ANY = pl.BlockSpec(memory_space=pl.ANY)


def _all_gather(blocks):
    n = len(blocks)

    def body(*refs):
        start, forward, finish = _gather_steps(refs[:n], refs[n:2 * n], *refs[2 * n:])
        start()
        forward()
        finish()

    out_shape, sems = _gather_results(blocks)
    return pl.pallas_call(
        body, name="all_gather_weights", in_specs=[_ANY] * n, out_specs=[_ANY] * n, out_shape=out_shape,
        scratch_shapes=sems,
    )(*blocks)


def _scatter_steps(pack_refs, got_refs, send_sems, recv_sems):
    x, y, c = lax.axis_index("x"), lax.axis_index("y"), lax.axis_index("c")

    def copies():
        out = []
        for t, (pack_ref, got_ref) in enumerate(zip(pack_refs, got_refs)):
            for k in range(1, N_DEV):
                px, py, pc = x ^ (k >> 2), y ^ ((k >> 1) & 1), c ^ (k & 1)
                sem = (N_DEV - 1) * t + k - 1
                out.append(pltpu.make_async_remote_copy(
                    src_ref=pack_ref.at[4 * px + 2 * py + pc], dst_ref=got_ref.at[k - 1], send_sem=send_sems.at[sem],
                    recv_sem=recv_sems.at[sem], device_id=(px, py, pc), device_id_type=MESH))
        return out

    def start():
        for cp in copies():
            cp.start()

    def finish():
        for cp in copies():
            cp.wait_recv()
        for cp in copies():
            cp.wait_send()

    return start, finish


def _scatter_results(ride):
    n = (N_DEV - 1) * len(ride)
    return ([jax.ShapeDtypeStruct((N_DEV - 1,) + t.shape[1:], t.dtype) for t in ride],
            [pltpu.SemaphoreType.DMA((n,)), pltpu.SemaphoreType.DMA((n,))])


def _gather_steps(x_refs, out_refs, send_sems, recv_sems, local_sems):
    x, y, c = lax.axis_index("x"), lax.axis_index("y"), lax.axis_index("c")
    me, sibling = (x, y, c), (x, y, 1 - c)
    chips = [(1 - x, y), (x, 1 - y), (1 - x, 1 - y)]
    tensors = range(len(x_refs))

    def slot(t, px, py, pc):
        return out_refs[t].at[4 * px + 2 * py + pc]

    def copy(t, k, blk, to, src=None):
        return pltpu.make_async_remote_copy(
            src_ref=slot(t, *blk) if src is None else src, dst_ref=slot(t, *blk), send_sem=send_sems.at[7 * t + k],
            recv_sem=recv_sems.at[7 * t + k], device_id=to, device_id_type=MESH)

    def mine(t):
        return pltpu.make_async_copy(x_refs[t], slot(t, *me), local_sems.at[t])

    def first():
        return [cp for t in tensors for cp in
                [copy(t, 0, me, sibling, src=x_refs[t])]
                + [copy(t, 1 + j, me, (*chip, c), src=x_refs[t]) for j, chip in enumerate(chips)]]

    def passed(t):
        return [copy(t, 4 + j, (*chip, c), sibling) for j, chip in enumerate(chips)]

    def start():
        for t in tensors:
            mine(t).start()
        for cp in first():
            cp.start()

    def forward():
        for t in tensors:
            for j, (chip, cp) in enumerate(zip(chips, passed(t))):
                copy(t, 1 + j, (*chip, c), me).wait_recv()
                cp.start()

    def finish():
        for t in tensors:
            copy(t, 0, sibling, me).wait_recv()
            for j, chip in enumerate(chips):
                copy(t, 4 + j, (*chip, 1 - c), me).wait_recv()
        for cp in first() + [cp for t in tensors for cp in passed(t)]:
            cp.wait_send()
        for t in tensors:
            mine(t).wait()

    return start, forward, finish


def _gather_results(blocks):
    n = len(blocks)
    return ([jax.ShapeDtypeStruct((N_DEV,) + t.shape, t.dtype) for t in blocks],
            [pltpu.SemaphoreType.DMA((7 * n,)), pltpu.SemaphoreType.DMA((7 * n,)), pltpu.SemaphoreType.DMA((n,))])


def _gather_small(small):
    def body(small_ref, parts_ref, send_sems, recv_sems, local_sem):
        x, y, c = lax.axis_index("x"), lax.axis_index("y"), lax.axis_index("c")
        me = 4 * x + 2 * y + c
        mine = pltpu.make_async_copy(small_ref, parts_ref.at[me], local_sem)
        mine.start()
        peers = [(x ^ (k >> 2), y ^ ((k >> 1) & 1), c ^ (k & 1)) for k in range(1, N_DEV)]
        sends = [pltpu.make_async_remote_copy(
            src_ref=small_ref, dst_ref=parts_ref.at[me], send_sem=send_sems.at[k], recv_sem=recv_sems.at[k],
            device_id=to, device_id_type=MESH) for k, to in enumerate(peers)]
        for cp in sends:
            cp.start()
        for k, (px, py, pc) in enumerate(peers):
            pltpu.make_async_remote_copy(
                src_ref=small_ref, dst_ref=parts_ref.at[4 * px + 2 * py + pc], send_sem=send_sems.at[k],
                recv_sem=recv_sems.at[k], device_id=(px, py, pc), device_id_type=MESH).wait_recv()
        for cp in sends:
            cp.wait_send()
        mine.wait()

    return pl.pallas_call(
        body, name="gather_small", in_specs=[_ANY], out_specs=_ANY,
        out_shape=jax.ShapeDtypeStruct((N_DEV,) + small.shape, F32),
        scratch_shapes=[pltpu.SemaphoreType.DMA((N_DEV - 1,)), pltpu.SemaphoreType.DMA((N_DEV - 1,)),
                        pltpu.SemaphoreType.DMA],
    )(small)


def _adamw_math(w, g, m, v):
    m = ADAM_B1 * m + (1.0 - ADAM_B1) * g
    v = ADAM_B2 * v + (1.0 - ADAM_B2) * (g * g)
    m_hat = m / (1.0 - ADAM_B1 ** ADAM_STEP)
    v_hat = v / (1.0 - ADAM_B2 ** ADAM_STEP)
    return -ADAM_LR * (m_hat / (jnp.sqrt(v_hat) + ADAM_EPS) + ADAM_WD * w), m, v


def _adamw_scattered(w, m, v, parts, got, me, name):
    rows, cols = w.shape
    tr = ADAMW_ROWS if rows % ADAMW_ROWS == 0 and rows * cols > ADAMW_WHOLE_BELOW else rows

    def body(me_ref, w_ref, m_ref, v_ref, own_ref, got_ref, g_o, d_o, m_o, v_o):
        g = own_ref[...].astype(F32)
        for k in range(N_DEV - 1):
            g = g + got_ref[k].astype(F32)
        g_o[...] = g
        d_o[...], m_o[...], v_o[...] = _adamw_math(w_ref[...], g, m_ref[...], v_ref[...])

    spec = pl.BlockSpec((tr, cols), lambda i, me_ref: (i, 0))
    return pl.pallas_call(
        body, name=name,
        grid_spec=pltpu.PrefetchScalarGridSpec(
            num_scalar_prefetch=1, grid=(rows // tr,),
            in_specs=[spec, spec, spec, pl.BlockSpec((None, tr, cols), lambda i, me_ref: (me_ref[0], i, 0)),
                      pl.BlockSpec((N_DEV - 1, tr, cols), lambda i, me_ref: (0, i, 0))],
            out_specs=[spec] * 4),
        out_shape=[jax.ShapeDtypeStruct((rows, cols), F32)] * 4, compiler_params=_params(("parallel",)),
    )(me, w, m, v, parts, got)


def _adamw_small(parts, ws, ms, vs):
    k = len(_SMALL)

    def body(p_ref, *refs):
        w_refs, m_refs, v_refs = refs[:k], refs[k:2 * k], refs[2 * k:3 * k]
        outs = refs[3 * k:]
        g = p_ref[0]
        for d in range(1, N_DEV):
            g = g + p_ref[d]
        o = 0
        for i, (_, n) in enumerate(_SMALL):
            gp = g[:, o:o + n]
            outs[1 + i][...] = gp
            outs[1 + k + i][...], outs[1 + 2 * k + i][...], outs[1 + 3 * k + i][...] = _adamw_math(
                w_refs[i][...], gp, m_refs[i][...], v_refs[i][...])
            o += n
        outs[0][...] = g[:, o:o + _LANES]

    shapes = [jax.ShapeDtypeStruct((1, n), F32) for _, n in _SMALL]
    out = pl.pallas_call(
        body, name="adamw_small", out_shape=[jax.ShapeDtypeStruct((1, _LANES), F32)] + shapes * 4,
        compiler_params=_params(),
    )(parts, *ws, *ms, *vs)
    return out[0][0, 0], out[1:1 + k], out[1 + k:1 + 2 * k], out[1 + 2 * k:1 + 3 * k], out[1 + 3 * k:]


_WEIGHT_NAMES = ['norm_mix_pre', 'w_in', 'b_gate', 'mu_rw', 'w0', 'w_up', 'a0', 'a_up', 'g_up', 'k_k', 'k_a', 'r_k',
                 'lnx_w', 'lnx_b', 'w_sb_out', 'w_rw_out', 'w_o', 'norm_mix_post', 'norm_ffn_pre', 'w_ffn_gate',
                 'w_ffn_up', 'w_ffn_down', 'norm_ffn_post']


def _step(x, target, w, m, v):
    sharded = [n for n, _, _ in _SHARDED]
    sm = {n: w[n].reshape(1, -1) for n, _ in _SMALL}
    own = {n: w[n][0] for n in sharded}
    mine = [own[n].astype(BF16) for n in sharded]
    wt = _whole_weights(_all_gather(mine[:_EARLY]), slice(0, _EARLY))
    loss_part, grad_x, _, gs, scattered = _local_step(x, target, sm, wt, mine[_EARLY:])

    me = 4 * lax.axis_index("x") + 2 * lax.axis_index("y") + lax.axis_index("c")
    me = jnp.reshape(me, (1,)).astype(jnp.int32)
    small_parts = _gather_small(_pack_small(gs, loss_part[0, 0]))
    row = lambda t: [t[n].reshape(1, -1) for n, _ in _SMALL]
    loss, *by_kind = _adamw_small(small_parts, row(w), row(m), row(v))
    g_s, d_s, m_s, v_s = ({n: t[i] for i, (n, _) in enumerate(_SMALL)} for t in by_kind)

    grads, deltas, new_m, new_v = {}, {}, {}, {}
    for n in _WEIGHT_NAMES:
        if n in scattered:
            out = _adamw_scattered(own[n], m[n][0], v[n][0], *scattered[n], me, "adamw_" + n)
            grads[n], deltas[n], new_m[n], new_v[n] = (t.reshape(w[n].shape) for t in out)
        else:
            grads[n], deltas[n], new_m[n], new_v[n] = (t[n].reshape(w[n].shape) for t in (g_s, d_s, m_s, v_s))
    return (loss, grad_x, *[grads[n] for n in _WEIGHT_NAMES], *[deltas[n] for n in _WEIGHT_NAMES],
            *[new_m[n] for n in _WEIGHT_NAMES], *[new_v[n] for n in _WEIGHT_NAMES])


def kernel(x, norm_mix_pre, w_in, b_gate, mu_rw, w0, w_up, a0, a_up, g_up, k_k, k_a, r_k, lnx_w, lnx_b, w_sb_out, w_rw_out, w_o, norm_mix_post, norm_ffn_pre, w_ffn_gate, w_ffn_up, w_ffn_down, norm_ffn_post, loss_target, m_norm_mix_pre, m_w_in, m_b_gate, m_mu_rw, m_w0, m_w_up, m_a0, m_a_up, m_g_up, m_k_k, m_k_a, m_r_k, m_lnx_w, m_lnx_b, m_w_sb_out, m_w_rw_out, m_w_o, m_norm_mix_post, m_norm_ffn_pre, m_w_ffn_gate, m_w_ffn_up, m_w_ffn_down, m_norm_ffn_post, v_norm_mix_pre, v_w_in, v_b_gate, v_mu_rw, v_w0, v_w_up, v_a0, v_a_up, v_g_up, v_k_k, v_k_a, v_r_k, v_lnx_w, v_lnx_b, v_w_sb_out, v_w_rw_out, v_w_o, v_norm_mix_post, v_norm_ffn_pre, v_w_ffn_gate, v_w_ffn_up, v_w_ffn_down, v_norm_ffn_post):
    args = locals()
    w = {n: args[n] for n in _WEIGHT_NAMES}
    m = {n: args["m_" + n] for n in _WEIGHT_NAMES}
    v = {n: args["v_" + n] for n in _WEIGHT_NAMES}
    return _step(x, loss_target, w, m, v)
```

```python
import functools

import jax
import jax.numpy as jnp
from jax import lax
from jax.experimental import pallas as pl
from jax.experimental.pallas import tpu as pltpu

F32 = jnp.float32
BF16 = jnp.bfloat16

D_MODEL = 1024
HEADS = 8
HEAD_DIM = 64
WIDTH = HEADS * HEAD_DIM
W_LORA, A_LORA, G_LORA = 64, 64, 128
SB_COLS = 3 * WIDTH
RW_COLS = 3 * WIDTH + W_LORA + A_LORA + G_LORA
GATE_COLS = 2 * D_MODEL
D_FF = 2816
RMS_EPS = 1e-6
GN_EPS = HEAD_DIM * 1e-5
N_DEV = 8

ADAM_LR, ADAM_B1, ADAM_B2, ADAM_EPS, ADAM_WD, ADAM_STEP = 0.001, 0.9, 0.999, 1e-08, 0.01, 10

ROW_TILE = 512
ROW_TILE_FFN = 256
SCAN_CHUNK = 64
ATT_ALIGN = 128
ATT_SUM_BWD = 256
ATT_WINDOW = 512
ATT_Q = 256
ATT_PAIRS = 2
SB_DEAD = -104.0
SCAN_SEQS_FWD = 4
SCAN_SEQS_BWD = 2
SCAN_PASSES = 1
GW_TILE_K, GW_TILE_N, GW_ROWS = 1408, 2048, 2048
ADAMW_ROWS, ADAMW_WHOLE_BELOW = 256, 2 ** 19
VMEM_LIMIT = 56 * 2 ** 20

MESH = pl.DeviceIdType.MESH


def _params(sem=None, vmem=VMEM_LIMIT):
    kw = dict(vmem_limit_bytes=vmem)
    if sem is not None:
        kw["dimension_semantics"] = sem
    return pltpu.CompilerParams(**kw)


def _const_spec(shape):
    nd = len(shape)
    return pl.BlockSpec(shape, lambda *_: (0,) * nd, pipeline_mode=pl.Buffered(1))


def _row_spec(tm, n):
    return pl.BlockSpec((tm, n), lambda i: (i, 0))


def _mm(a, b):
    return lax.dot_general(a, b, (((1,), (0,)), ((), ())), preferred_element_type=F32)


def _mm_nt(a, b):
    return lax.dot_general(a, b, (((1,), (1,)), ((), ())), preferred_element_type=F32)


def _mm_tn(a, b):
    return lax.dot_general(a, b, (((0,), (0,)), ((), ())), preferred_element_type=F32)


def _softplus(z):
    return jnp.maximum(z, 0.0) + jnp.log1p(jnp.exp(-jnp.abs(z)))


def _rms_fwd(x, gain):
    rstd = lax.rsqrt(jnp.mean(x * x, axis=-1, keepdims=True) + RMS_EPS)
    xn = x * rstd
    return xn * gain, xn, rstd


def _rms_bwd(dy, xn, rstd, gain):
    u = dy * gain
    dx = rstd * (u - xn * jnp.mean(u * xn, axis=-1, keepdims=True))
    return dx, jnp.sum(dy * xn, axis=0, keepdims=True)


def _acc_out(ref, val, first):
    @pl.when(first)
    def _():
        ref[...] = val

    @pl.when(jnp.logical_not(first))
    def _():
        ref[...] += val


_IN_COLS = SB_COLS + RW_COLS + GATE_COLS
_QKV_OF, _RW_OF, _GATE_OF = slice(0, SB_COLS), slice(SB_COLS, SB_COLS + RW_COLS), slice(SB_COLS + RW_COLS, _IN_COLS)


def _in_proj_fwd(x2, g_pre, w_in, b_gate, tm):
    T = x2.shape[0]

    def body(x_ref, g_ref, w_ref, b_ref, h_ref, qkv_ref, prw_ref, gate_ref):
        h = _rms_fwd(x_ref[...], g_ref[...])[0].astype(BF16)
        h_ref[...] = h
        qkv_ref[...] = _mm(h, w_ref[:, _QKV_OF]).astype(BF16)
        prw_ref[...] = _mm(h, w_ref[:, _RW_OF])
        gate_ref[...] = jax.nn.sigmoid(_mm(h, w_ref[:, _GATE_OF]) + b_ref[...])

    return pl.pallas_call(
        body, name="in_proj_fwd", grid=(T // tm,),
        in_specs=[_row_spec(tm, D_MODEL), _const_spec((1, D_MODEL)), _const_spec((D_MODEL, _IN_COLS)),
                  _const_spec((1, GATE_COLS))],
        out_specs=[_row_spec(tm, D_MODEL), _row_spec(tm, SB_COLS), _row_spec(tm, RW_COLS), _row_spec(tm, GATE_COLS)],
        out_shape=[jax.ShapeDtypeStruct((T, D_MODEL), BF16), jax.ShapeDtypeStruct((T, SB_COLS), BF16),
                   jax.ShapeDtypeStruct((T, RW_COLS), F32), jax.ShapeDtypeStruct((T, GATE_COLS), F32)],
        compiler_params=_params(("parallel",)),
    )(x2, g_pre, w_in, b_gate)


def _in_proj_bwd(x2, g_pre, dx1, dqkv, dprw, dgate, w_in, tm, ride=None):
    T = x2.shape[0]
    steps = T // tm
    nt = 0 if ride is None else len(ride)

    def body(x_ref, g_ref, dx1_ref, dq_ref, dr_ref, dg_ref, w_ref, *rest):
        gx_ref, dgain_ref = rest[nt:nt + 2]
        if ride is not None:
            start, finish = _scatter_steps(rest[:nt], rest[nt + 2:2 * nt + 2], rest[-2], rest[-1])
            pl.when(pl.program_id(0) == 0)(start)
        dh = (_mm_nt(dq_ref[...], w_ref[:, _QKV_OF]) + _mm_nt(dr_ref[...], w_ref[:, _RW_OF])
              + _mm_nt(dg_ref[...], w_ref[:, _GATE_OF]))
        gain = g_ref[...]
        _, xn, rstd = _rms_fwd(x_ref[...], gain)
        dx, dgain = _rms_bwd(dh, xn, rstd, gain)
        gx_ref[...] = dx1_ref[...] + dx
        _acc_out(dgain_ref, dgain, pl.program_id(0) == 0)
        if ride is not None:
            pl.when(pl.program_id(0) == steps - 1)(finish)

    in_specs = [_row_spec(tm, D_MODEL), _const_spec((1, D_MODEL)), _row_spec(tm, D_MODEL), _row_spec(tm, SB_COLS),
                _row_spec(tm, RW_COLS), _row_spec(tm, GATE_COLS), _const_spec((D_MODEL, _IN_COLS))]
    out_specs = [_row_spec(tm, D_MODEL), pl.BlockSpec((1, D_MODEL), lambda i: (0, 0))]
    out_shape = [jax.ShapeDtypeStruct((T, D_MODEL), F32), jax.ShapeDtypeStruct((1, D_MODEL), F32)]
    args, scratch = (x2, g_pre, dx1, dqkv, dprw, dgate, w_in), []
    if ride is not None:
        in_specs, out_specs, args = in_specs + [_ANY] * nt, out_specs + [_ANY] * nt, args + tuple(ride)
        got_shapes, scratch = _scatter_results(ride)
        out_shape = out_shape + got_shapes
    return pl.pallas_call(
        body, name="in_proj_bwd", grid=(steps,), in_specs=in_specs, out_specs=out_specs, out_shape=out_shape,
        scratch_shapes=scratch, compiler_params=_params(("arbitrary",)),
    )(*args)


def _pick_tile(n, cap):
    best = None
    for t in range(128, min(n, cap) + 1, 128):
        if n % t == 0:
            best = t
    return n if best is None else best


def _grad_w(a, b, name, dtype=F32):
    T, K = a.shape
    N = b.shape[1]
    tk, tn, tt = _pick_tile(K, GW_TILE_K), _pick_tile(N, GW_TILE_N), min(T, GW_ROWS)
    steps = T // tt

    def body(a_ref, b_ref, o_ref, *acc):
        t = pl.program_id(2)
        part = _mm_tn(a_ref[...], b_ref[...])
        if not acc:
            _acc_out(o_ref, part, t == 0)
        else:
            _acc_out(acc[0], part, t == 0)

            @pl.when(t == steps - 1)
            def _():
                o_ref[...] = acc[0][...].astype(dtype)

    return pl.pallas_call(
        body, name=name, grid=(K // tk, N // tn, steps),
        in_specs=[pl.BlockSpec((tt, tk), lambda i, j, t: (t, i)), pl.BlockSpec((tt, tn), lambda i, j, t: (t, j))],
        out_specs=pl.BlockSpec((tk, tn), lambda i, j, t: (i, j)),
        out_shape=jax.ShapeDtypeStruct((K, N), dtype),
        scratch_shapes=[] if dtype == F32 else [pltpu.VMEM((tk, tn), F32)],
        compiler_params=_params(("parallel", "parallel", "arbitrary")),
    )(a, b)


def _tri(n, kind):
    r = lax.broadcasted_iota(jnp.int32, (n, n), 0)
    c = lax.broadcasted_iota(jnp.int32, (n, n), 1)
    return {"gt": r > c, "le": r <= c, "lt": r < c, "ge": r >= c}[kind]


def _running_sums(x, carry, tri, kb, reverse=False):
    blocks = range(x.shape[1] // kb)
    parts = {}
    for b in (reversed(blocks) if reverse else blocks):
        piece = x[:, b * kb:(b + 1) * kb]
        parts[b] = carry + _mm(piece.astype(BF16), tri)
        carry = carry + jnp.sum(piece, axis=1, keepdims=True)
    return jnp.concatenate([parts[b] for b in blocks], axis=1), carry


def _sb_valid(row0, col0, first, last, qb, kb):
    ahead = lax.broadcasted_iota(jnp.int32, (qb, kb), 1) - lax.broadcasted_iota(jnp.int32, (qb, kb), 0)
    col = lax.broadcasted_iota(jnp.int32, (1, kb), 1)
    return jnp.logical_and(ahead < row0 - col0, jnp.logical_and(col >= first - col0, col < last - col0))


def _sb_softplus(z):
    return jnp.maximum(z, 0.0) + jnp.log(1.0 + jnp.exp(-jnp.abs(z)))


_PAIR = 2 * HEAD_DIM
_PAIRS = WIDTH // _PAIR


def _first_head_lanes():
    return lax.broadcasted_iota(jnp.int32, (1, _PAIR), 1) < HEAD_DIM


def _per_head(t, first_head):
    zero = jnp.zeros_like(t)
    return jnp.where(first_head, t, zero), jnp.where(first_head, zero, t)


def _sb_fwd(qkv, bl, seq, ride=None):
    qb, win, kb = min(ATT_Q, seq), min(ATT_WINDOW, seq), ATT_ALIGN
    nq = seq // qb
    nh, width, groups = 2 * ATT_PAIRS, ATT_PAIRS * _PAIR, _PAIRS // ATT_PAIRS
    pair_of = lambda h: slice((h // 2) * _PAIR, (h // 2 + 1) * _PAIR)

    steps = bl * groups
    pass_on_at = (5 * steps) // 8
    nt = 0 if ride is None else len(ride)

    def body(q_ref, k_ref, v_ref, *rest):
        g = pl.program_id(0) * groups + pl.program_id(1)
        o_ref, l_ref, stop_ref = rest[nt:nt + 3]
        if ride is not None:
            start, forward, finish = _gather_steps(rest[:nt], rest[nt + 3:2 * nt + 3], *rest[2 * nt + 3:])
            pl.when(g == 0)(start)
            pl.when(g == pass_on_at)(forward)
        first_head = _first_head_lanes()
        u_after = _tri(kb, "gt").astype(BF16)

        def qblock(i, _):
            rows = pl.ds(pl.multiple_of(i * qb, qb), qb)
            qs = q_ref[rows, :] * (HEAD_DIM ** -0.5)
            qh = [_per_head(qs[:, pair_of(h)], first_head)[h % 2] for h in range(nh)]

            def live(carry):
                return jnp.logical_and(carry[0] > 0, carry[3] > 0)

            def window(carry):
                hi, accs, cs, _ = carry
                lo = pl.multiple_of(jnp.maximum(hi - win, 0), kb)
                cols = pl.ds(lo, win)
                kv, vv = k_ref[cols, :], v_ref[cols, :]
                valid = _sb_valid(i * qb, lo, lo, hi, qb, win)
                accs, cs = list(accs), list(cs)
                for h in range(nh):
                    z = _mm_nt(qh[h], kv[:, pair_of(h)])
                    sp = _sb_softplus(z)
                    spm = jnp.where(valid, sp, 0.0)
                    after, cs[h] = _running_sums(spm, cs[h], u_after, kb, reverse=True)
                    w = jnp.where(valid, jnp.exp(z - sp - after), 0.0)
                    accs[h] = accs[h] + _mm(w.astype(BF16), vv[:, pair_of(h)])
                alive = functools.reduce(jnp.minimum, [jnp.min(c) for c in cs]) < -SB_DEAD
                return lo, tuple(accs), tuple(cs), alive.astype(jnp.int32)

            zero_acc, zero_c = jnp.zeros((qb, _PAIR), F32), jnp.zeros((qb, 1), F32)
            lo, accs, cs, _ = lax.while_loop(
                live, window, ((i + 1) * qb, (zero_acc,) * nh, (zero_c,) * nh, jnp.int32(1)))
            for pp in range(ATT_PAIRS):
                o_ref[rows, pp * _PAIR:(pp + 1) * _PAIR] = jnp.where(
                    first_head, accs[2 * pp], accs[2 * pp + 1]).astype(BF16)
            for h in range(nh):
                l_ref[h, rows, :] = cs[h]
            stop_ref[g, i] = lo
            return 0

        lax.fori_loop(0, nq, qblock, 0)
        if ride is not None:
            pl.when(g == steps - 1)(finish)

    col = lambda off: pl.BlockSpec((seq, width), lambda b, p: (b, off + p))
    in_specs = [col(0), col(groups), col(2 * groups)]
    out_specs = [col(0), pl.BlockSpec((None, nh, seq, 1), lambda b, p: (b, p, 0, 0)), pl.BlockSpec(memory_space=pltpu.SMEM)]
    out_shape = [jax.ShapeDtypeStruct((bl * seq, WIDTH), BF16), jax.ShapeDtypeStruct((bl, HEADS, seq, 1), F32),
                 jax.ShapeDtypeStruct((bl * groups, nq), jnp.int32)]
    if ride is None:
        return pl.pallas_call(body, name="sb_fwd", grid=(bl, groups), in_specs=in_specs, out_specs=out_specs,
                              out_shape=out_shape, compiler_params=_params(("arbitrary", "arbitrary")))(qkv, qkv, qkv)
    gathered, sems = _gather_results(ride)
    return pl.pallas_call(
        body, name="sb_fwd", grid=(bl, groups), in_specs=in_specs + [_ANY] * nt, out_specs=out_specs + [_ANY] * nt,
        out_shape=out_shape + gathered, scratch_shapes=sems, compiler_params=_params(("arbitrary", "arbitrary")),
    )(qkv, qkv, qkv, *ride)


def _sb_bwd(qkv, do, lsum, stop, bl, seq):
    qb, win, kb = min(ATT_Q, seq), min(ATT_WINDOW, seq), ATT_ALIGN
    nq = seq // qb
    nh, width, groups = 2 * ATT_PAIRS, ATT_PAIRS * _PAIR, _PAIRS // ATT_PAIRS
    pair_of = lambda h: slice((h // 2) * _PAIR, (h // 2 + 1) * _PAIR)

    def body(stop_ref, q_ref, k_ref, v_ref, do_ref, l_ref, dq_ref, dk_ref, dv_ref, dk_acc, dv_acc):
        g = pl.program_id(0) * groups + pl.program_id(1)
        first_head = _first_head_lanes()
        sb = min(ATT_SUM_BWD, win)
        u_incl = _tri(sb, "le").astype(BF16)
        u_excl = _tri(sb, "lt").astype(BF16)
        dk_acc[...] = jnp.zeros_like(dk_acc)
        dv_acc[...] = jnp.zeros_like(dv_acc)

        def qblock(i, _):
            rows = pl.ds(pl.multiple_of(i * qb, qb), qb)
            qv = q_ref[rows, :]
            qs = qv * (HEAD_DIM ** -0.5)
            dob = do_ref[rows, :]
            qh = [_per_head(qs[:, pair_of(h)], first_head)[h % 2] for h in range(nh)]
            doh = [_per_head(dob[:, pair_of(h)], first_head)[h % 2] for h in range(nh)]
            ltot = [l_ref[h, rows, :] for h in range(nh)]

            first = (jnp.clip(stop_ref[g, i], 0, i * qb) // ATT_ALIGN) * ATT_ALIGN

            def window(n, carry):
                dqs, ps, es = (list(t) for t in carry)
                start = first + n * win
                lo = pl.multiple_of(jnp.minimum(start, seq - win), kb)
                cols = pl.ds(lo, win)
                kv, vv = k_ref[cols, :], v_ref[cols, :]
                valid = _sb_valid(i * qb, lo, start, seq, qb, win)
                dks, dvs = [], []
                for h in range(nh):
                    kp, vp = kv[:, pair_of(h)], vv[:, pair_of(h)]
                    z = _mm_nt(qh[h], kp)
                    sp = _sb_softplus(z)
                    spm = jnp.where(valid, sp, 0.0)
                    upto, ps[h] = _running_sums(spm, ps[h], u_incl, sb)
                    w = jnp.where(valid, jnp.exp(z - sp - (ltot[h] - upto)), 0.0)
                    e = _mm_nt(doh[h], vp) * w
                    dlf, es[h] = _running_sums(e, es[h], u_excl, sb)
                    sig = jnp.exp(z - sp)
                    dz = jnp.where(valid, e * (1.0 - sig) - dlf * sig, 0.0) * (HEAD_DIM ** -0.5)
                    dzb = dz.astype(BF16)
                    dvs.append(_mm_tn(w.astype(BF16), dob[:, pair_of(h)]))
                    dks.append(_mm_tn(dzb, qv[:, pair_of(h)]))
                    dqs[h] = dqs[h] + _mm(dzb, kp)
                for pp in range(ATT_PAIRS):
                    lanes = slice(pp * _PAIR, (pp + 1) * _PAIR)
                    dv_acc[cols, lanes] += jnp.where(first_head, dvs[2 * pp], dvs[2 * pp + 1])
                    dk_acc[cols, lanes] += jnp.where(first_head, dks[2 * pp], dks[2 * pp + 1])
                return tuple(dqs), tuple(ps), tuple(es)

            zero_q, zero_c = jnp.zeros((qb, _PAIR), F32), jnp.zeros((qb, 1), F32)
            dqs, _, _ = lax.fori_loop(0, ((i + 1) * qb - first + win - 1) // win, window,
                                      ((zero_q,) * nh, (zero_c,) * nh, (zero_c,) * nh))
            for pp in range(ATT_PAIRS):
                dq_ref[rows, pp * _PAIR:(pp + 1) * _PAIR] = jnp.where(
                    first_head, dqs[2 * pp], dqs[2 * pp + 1]).astype(BF16)
            return 0

        lax.fori_loop(0, nq, qblock, 0)
        dk_ref[...] = dk_acc[...].astype(BF16)
        dv_ref[...] = dv_acc[...].astype(BF16)

    col = lambda off: pl.BlockSpec((seq, width), lambda b, p, stop_ref: (b, off + p))
    return pl.pallas_call(
        body, name="sb_bwd",
        grid_spec=pltpu.PrefetchScalarGridSpec(
            num_scalar_prefetch=1, grid=(bl, groups),
            in_specs=[col(0), col(groups), col(2 * groups), col(0),
                      pl.BlockSpec((None, nh, seq, 1), lambda b, p, stop_ref: (b, p, 0, 0))],
            out_specs=[col(0), col(0), col(0)],
            scratch_shapes=[pltpu.VMEM((seq, width), F32), pltpu.VMEM((seq, width), F32)]),
        out_shape=[jax.ShapeDtypeStruct((bl * seq, WIDTH), BF16)] * 3,
        compiler_params=_params(("parallel", "parallel")),
    )(stop, qkv, qkv, qkv, do, lsum)


@jax.custom_vjp
def _lora_mm(x, w):
    return _mm(x.astype(BF16), w.astype(BF16))


_lora_mm.defvjp(
    lambda x, w: (_mm(x.astype(BF16), w.astype(BF16)), (x, w)),
    lambda res, ct: (_mm_nt(ct.astype(BF16), res[1].astype(BF16)), _mm_tn(res[0].astype(BF16), ct.astype(BF16))))


def _rw_prep_math(p, ps, mu, w0, w_up, a0, a_up, g_up, k_k, k_a):
    pm = p + (ps - p) * mu
    r, k, v = pm[:, :WIDTH], pm[:, WIDTH:2 * WIDTH], pm[:, 2 * WIDTH:3 * WIDTH]
    o = 3 * WIDTH
    xw, xa, xg = pm[:, o:o + W_LORA], pm[:, o + W_LORA:o + W_LORA + A_LORA], pm[:, o + W_LORA + A_LORA:]
    w_raw = w0 + _lora_mm(jnp.tanh(xw), w_up)
    lw = -jnp.exp(-_softplus(-w_raw) - 0.5)
    a = jax.nn.sigmoid(a0 + _lora_mm(xa, a_up))
    g = _lora_mm(jax.nn.sigmoid(xg), g_up)
    kk = k * k_k
    k2 = k * (1.0 + (a - 1.0) * k_a)
    return r, lw, k2, v, kk, a, g


def _shift_down(p, first_row):
    row = lax.broadcasted_iota(jnp.int32, p.shape, 0)
    return jnp.where(row == 0, first_row, pltpu.roll(p, 1, 0))


def _shift_up(p, last_row):
    row = lax.broadcasted_iota(jnp.int32, p.shape, 0)
    return jnp.where(row == p.shape[0] - 1, last_row, pltpu.roll(p, p.shape[0] - 1, 0))


_PREP_PARAM_SHAPES = [(1, RW_COLS), (1, WIDTH), (W_LORA, WIDTH), (1, WIDTH), (A_LORA, WIDTH), (G_LORA, WIDTH),
                      (1, WIDTH), (1, WIDTH)]


def _prev_rows_spec(tm):
    return pl.BlockSpec((8, RW_COLS), lambda i: (jnp.maximum(i * (tm // 8) - 1, 0), 0))


def _head_spec(tm, seq, tile_of=lambda i: i):
    per_seq = seq // tm
    return pl.BlockSpec((None, HEADS, tm, HEAD_DIM),
                        lambda i: (tile_of(i) // per_seq, 0, tile_of(i) % per_seq, 0))


def _split_heads(val, ref):
    for h in range(HEADS):
        ref[h] = val[:, h * HEAD_DIM:(h + 1) * HEAD_DIM]


def _join_heads(ref):
    return jnp.concatenate([ref[h] for h in range(HEADS)], axis=1)


def _rw_prep_fwd(prw, params, seq, tm):
    T = prw.shape[0]

    def body(p_ref, prev_ref, *rest):
        prm = [r_[...] for r_ in rest[:8]]
        outs = rest[8:]
        i = pl.program_id(0)
        first = jnp.where((i * tm) % seq == 0, 0.0, prev_ref[7:8, :])
        p = p_ref[...]
        vals = _rw_prep_math(p, _shift_down(p, first), *prm)
        for o_ref, val in zip(outs[:6], vals[:6]):
            _split_heads(val, o_ref)
        outs[6][...] = vals[6]

    by_head = jax.ShapeDtypeStruct((T // seq, HEADS, seq, HEAD_DIM), F32)
    return pl.pallas_call(
        body, name="rw_prep_fwd", grid=(T // tm,),
        in_specs=[_row_spec(tm, RW_COLS), _prev_rows_spec(tm)] + [_const_spec(s) for s in _PREP_PARAM_SHAPES],
        out_specs=[_head_spec(tm, seq)] * 6 + [_row_spec(tm, WIDTH)],
        out_shape=[by_head] * 6 + [jax.ShapeDtypeStruct((T, WIDTH), F32)],
        compiler_params=_params(("parallel",)),
    )(prw, prw, *params)


def _rw_prep_bwd(prw, params, cts, seq, tm):
    T = prw.shape[0]
    n = T // tm

    def body(p_ref, prev_ref, *rest):
        prm = [r_[...] for r_ in rest[:8]]
        ct = tuple(_join_heads(r_) for r_ in rest[8:14]) + (rest[14][...],)
        dp_ref = rest[15]
        dprm_refs = rest[16:24]
        carry = rest[24]
        step = pl.program_id(0)
        i = n - 1 - step
        first = jnp.where((i * tm) % seq == 0, 0.0, prev_ref[7:8, :])
        p = p_ref[...]
        _, vjp = jax.vjp(_rw_prep_math, p, _shift_down(p, first), *prm)
        grads = vjp(ct)
        dp, dps = grads[0], grads[1]
        nxt = jnp.where(jnp.logical_or(step == 0, ((i + 1) * tm) % seq == 0), 0.0, carry[0:1, :])
        dp_ref[...] = (dp + _shift_up(dps, nxt)).astype(BF16)
        carry[...] = dps[0:8, :]
        for ref, gval in zip(dprm_refs, grads[2:]):
            _acc_out(ref, gval, step == 0)

    rev = lambda w: pl.BlockSpec((tm, w), lambda s: (n - 1 - s, 0))
    prev = pl.BlockSpec((8, RW_COLS), lambda s: (jnp.maximum((n - 1 - s) * (tm // 8) - 1, 0), 0))
    return pl.pallas_call(
        body, name="rw_prep_bwd", grid=(n,),
        in_specs=([rev(RW_COLS), prev] + [_const_spec(s) for s in _PREP_PARAM_SHAPES]
                  + [_head_spec(tm, seq, lambda s: n - 1 - s)] * 6 + [rev(WIDTH)]),
        out_specs=[rev(RW_COLS)] + [pl.BlockSpec(s, lambda s_: (0, 0)) for s in _PREP_PARAM_SHAPES],
        out_shape=[jax.ShapeDtypeStruct((T, RW_COLS), BF16)] + [jax.ShapeDtypeStruct(s, F32) for s in _PREP_PARAM_SHAPES],
        scratch_shapes=[pltpu.VMEM((8, RW_COLS), F32)],
        compiler_params=_params(("arbitrary",)),
    )(prw, prw, *params, *cts)


def _make_bmm(passes):
    def raw(dn, a, b):
        d = lambda x, y: lax.dot_general(x, y, dn, preferred_element_type=F32)
        ah = a.astype(BF16)
        bh = b.astype(BF16)
        if passes == 1:
            return d(ah, bh)
        al = (a - ah.astype(F32)).astype(BF16)
        bl = (b - bh.astype(F32)).astype(BF16)
        return d(ah, bh) + (d(ah, bl) + d(al, bh))

    dn_nn = (((2,), (1,)), ((0,), (0,)))
    dn_nt = (((2,), (2,)), ((0,), (0,)))
    dn_tn = (((1,), (1,)), ((0,), (0,)))

    @jax.custom_vjp
    def nn(a, b):
        return raw(dn_nn, a, b)

    @jax.custom_vjp
    def nt(a, b):
        return raw(dn_nt, a, b)

    @jax.custom_vjp
    def tn(a, b):
        return raw(dn_tn, a, b)

    nn.defvjp(lambda a, b: (raw(dn_nn, a, b), (a, b)), lambda res, ct: (nt(ct, res[1]), tn(res[0], ct)))
    nt.defvjp(lambda a, b: (raw(dn_nt, a, b), (a, b)), lambda res, ct: (nn(ct, res[1]), tn(ct, res[0])))
    tn.defvjp(lambda a, b: (raw(dn_tn, a, b), (a, b)), lambda res, ct: (nt(res[1], ct), nn(res[0], ct)))

    def unit_lower_inverse(m):
        n = m.shape[-1]
        row = lax.broadcasted_iota(jnp.int32, (n, n), 0)
        col = lax.broadcasted_iota(jnp.int32, (n, n), 1)
        m16 = ((row // 16) == (col // 16)).astype(F32)
        m32 = ((row // 32) == (col // 32)).astype(F32)
        a1 = m * m16
        a2 = nn(a1, a1)
        a4 = nn(a2, a2)
        a8 = nn(a4, a4)
        inv = (row == col).astype(F32) - a1
        inv = inv + nn(inv, a2)
        inv = inv + nn(inv, a4)
        inv = inv + nn(inv, a8)
        inv = inv - nn(nn(inv, m * (m32 - m16)), inv)
        return inv - nn(nn(inv, m * (1.0 - m32)), inv)

    @jax.custom_vjp
    def inverse(m):
        return unit_lower_inverse(m)

    def inverse_fwd(m):
        inv = unit_lower_inverse(m)
        return inv, inv

    inverse.defvjp(inverse_fwd, lambda inv, ct: (-nt(tn(inv, ct), inv),))
    return nn, nt, tn, inverse


def _wkv_chunk(s0, r, lw, k, v, kk, a, lnw, lnb, rk):
    nn, nt, tn, inverse = _make_bmm(SCAN_PASSES)
    G, L, N = r.shape
    rep = lambda t: jnp.broadcast_to(t[None], (G // HEADS, HEADS, 1, N)).reshape(G, 1, N)
    kap = kk * lax.rsqrt(jnp.maximum(jnp.sum(kk * kk, axis=-1, keepdims=True), 1e-24))
    b = a * kap
    row = lax.broadcasted_iota(jnp.int32, (L, L), 0)
    col = lax.broadcasted_iota(jnp.int32, (L, L), 1)
    low_incl = (col <= row).astype(F32)
    low_strict = (col < row).astype(F32)
    c = _make_bmm(3)[0](jnp.broadcast_to(low_incl[None], (G, L, L)), lw)
    c_all = jnp.sum(lw, axis=1, keepdims=True)
    g_inv = jnp.exp(-c)
    kap_t = kap * jnp.exp(c - lw)
    b_t = b * g_inv
    k_t = k * g_inv
    r_t = r * jnp.exp(c)
    g_all = jnp.exp(c_all)
    m_b = nt(kap_t, b_t) * low_strict
    m_k = nt(kap_t, k_t) * low_strict
    n_b = nt(r_t, b_t) * low_incl
    n_k = nt(r_t, k_t) * low_incl
    rhs = -(nt(kap_t, s0) + nn(m_k, v))
    sa = nn(inverse(m_b), rhs)
    y = nt(r_t, s0) + nn(n_b, sa) + nn(n_k, v)
    s1 = s0 * g_all + tn(sa, b_t * g_all) + tn(v, k_t * g_all)
    mean = jnp.mean(y, axis=-1, keepdims=True)
    yc = y - mean
    var = jnp.mean(yc * yc, axis=-1, keepdims=True)
    out = yc * lax.rsqrt(var + GN_EPS) * rep(lnw) + rep(lnb)
    out = out + jnp.sum(r * k * rep(rk), axis=-1, keepdims=True) * v
    return out, s1


def _scan_heads_per_step(total_heads, seqs_wanted):
    n_seq = total_heads // HEADS
    return HEADS * max(d for d in range(1, seqs_wanted + 1) if n_seq % d == 0)


def _wkv_fwd(seqs, lnw, lnb, rk):
    G, S, N = seqs[0].shape
    L = SCAN_CHUNK
    nc = S // L

    def body(*refs):
        ins = [r_[...] for r_ in refs[:6]]
        prm = [r_[...] for r_ in refs[6:9]]
        out_ref, st_ref, state = refs[9], refs[10], refs[11]

        @pl.when(pl.program_id(1) == 0)
        def _():
            state[...] = jnp.zeros_like(state)

        s0 = state[...]
        st_ref[...] = s0
        out, s1 = _wkv_chunk(s0, *ins, *prm)
        out_ref[...] = out
        state[...] = s1

    gb = _scan_heads_per_step(G, SCAN_SEQS_FWD)
    blk = pl.BlockSpec((gb, L, N), lambda b, i: (b, i, 0))
    pspec = _const_spec((HEADS, 1, N))
    return pl.pallas_call(
        body, name="wkv_fwd", grid=(G // gb, nc), in_specs=[blk] * 6 + [pspec] * 3,
        out_specs=[blk, pl.BlockSpec((None, gb, N, N), lambda b, i: (i, b, 0, 0))],
        out_shape=[jax.ShapeDtypeStruct((G, S, N), F32), jax.ShapeDtypeStruct((nc, G, N, N), F32)],
        scratch_shapes=[pltpu.VMEM((gb, N, N), F32)],
        compiler_params=_params(("parallel", "arbitrary")),
    )(*seqs, lnw, lnb, rk)


def _wkv_bwd(seqs, states, dout, lnw, lnb, rk, ride=None):
    G, S, N = seqs[0].shape
    L = SCAN_CHUNK
    nc = S // L
    gb = _scan_heads_per_step(G, SCAN_SEQS_BWD)
    nt = 0 if ride is None else len(ride)

    def body(*refs):
        ins = [r_[...] for r_ in refs[:6]]
        s0 = refs[6][...]
        ct_out = refs[7][...]
        prm = [r_[...] for r_ in refs[8:11]]
        refs = refs[11:]
        if ride is not None:
            start, finish = _scatter_steps(refs[:nt], refs[nt + 9:2 * nt + 9], refs[-2], refs[-1])
            pl.when(jnp.logical_and(pl.program_id(0) == 0, pl.program_id(1) == 0))(start)
            refs = refs[nt:]
        d_refs = refs[0:6]
        dprm_refs = refs[6:9]
        dstate = refs[9 + nt]
        step = pl.program_id(1)

        @pl.when(step == 0)
        def _():
            dstate[...] = jnp.zeros_like(dstate)

        _, vjp = jax.vjp(_wkv_chunk, s0, *ins, *prm)
        grads = vjp((ct_out, dstate[...]))
        dstate[...] = grads[0]
        for ref, gval in zip(d_refs, grads[1:7]):
            ref[...] = gval
        for ref, gval in zip(dprm_refs, grads[7:]):
            _acc_out(ref, gval, jnp.logical_and(step == 0, pl.program_id(0) == 0))
        if ride is not None:
            pl.when(jnp.logical_and(pl.program_id(0) == G // gb - 1, step == nc - 1))(finish)

    blk = pl.BlockSpec((gb, L, N), lambda b, s: (b, nc - 1 - s, 0))
    pspec = _const_spec((HEADS, 1, N))
    pout = pl.BlockSpec((HEADS, 1, N), lambda b, s: (0, 0, 0))
    in_specs = [blk] * 6 + [pl.BlockSpec((None, gb, N, N), lambda b, s: (nc - 1 - s, b, 0, 0)), blk] + [pspec] * 3
    out_specs = [blk] * 6 + [pout] * 3
    out_shape = [jax.ShapeDtypeStruct((G, S, N), F32)] * 6 + [jax.ShapeDtypeStruct((HEADS, 1, N), F32)] * 3
    scratch = [pltpu.VMEM((gb, N, N), F32)]
    args = (*seqs, states, dout, lnw, lnb, rk)
    if ride is not None:
        in_specs, out_specs, args = in_specs + [_ANY] * nt, out_specs + [_ANY] * nt, args + tuple(ride)
        got_shapes, sems = _scatter_results(ride)
        out_shape, scratch = out_shape + got_shapes, scratch + sems
    return pl.pallas_call(
        body, name="wkv_bwd", grid=(G // gb, nc), in_specs=in_specs, out_specs=out_specs, out_shape=out_shape,
        scratch_shapes=scratch, compiler_params=_params(("arbitrary", "arbitrary")),
    )(*args)


def _merge_math(o_sb, rw_out, g_rw, gates, w_sb, w_rw, w_o):
    o_rw = (rw_out * g_rw).astype(BF16)
    a = _mm(o_sb, w_sb)
    b = _mm(o_rw, w_rw)
    g1, g2 = gates[:, :D_MODEL], gates[:, D_MODEL:]
    merged = (g1 * a + g2 * b).astype(BF16)
    return o_rw, a, b, g1, g2, merged, _mm(merged, w_o)


def _merge_fwd(x2, o_sb, rw_out, g_rw, gates, w_sb, w_rw, w_o, g_post, seq, tm):
    T = x2.shape[0]

    def body(x_ref, osb_ref, rw_ref, g_ref, gate_ref, wsb_ref, wrw_ref, wo_ref, gp_ref, x1_ref):
        z = _merge_math(osb_ref[...], _join_heads(rw_ref), g_ref[...], gate_ref[...], wsb_ref[...], wrw_ref[...], wo_ref[...])[-1]
        x1_ref[...] = x_ref[...] + _rms_fwd(z, gp_ref[...])[0]

    return pl.pallas_call(
        body, name="merge_fwd", grid=(T // tm,),
        in_specs=[_row_spec(tm, D_MODEL), _row_spec(tm, WIDTH), _head_spec(tm, seq), _row_spec(tm, WIDTH),
                  _row_spec(tm, GATE_COLS), _const_spec((WIDTH, D_MODEL)), _const_spec((WIDTH, D_MODEL)),
                  _const_spec((D_MODEL, D_MODEL)), _const_spec((1, D_MODEL))],
        out_specs=_row_spec(tm, D_MODEL),
        out_shape=jax.ShapeDtypeStruct((T, D_MODEL), F32),
        compiler_params=_params(("parallel",)),
    )(x2, o_sb, rw_out, g_rw, gates, w_sb, w_rw, w_o, g_post)


def _merge_bwd(dx1, o_sb, rw_out, g_rw, gates, w_sb, w_rw, w_o, g_post, seq, tm):
    T = dx1.shape[0]

    def body(dx1_ref, osb_ref, rw_ref, g_ref, gate_ref, wsb_ref, wrw_ref, wo_ref, gp_ref,
             orw_o, mrg_o, dz_o, da_o, db_o, dgate_o, dosb_o, drw_o, dg_o, dgp_o, dbg_o):
        rw_out_v, g_rw_v = _join_heads(rw_ref), g_ref[...]
        w_sb_v, w_rw_v, w_o_v = wsb_ref[...], wrw_ref[...], wo_ref[...]
        o_rw, a, b, g1, g2, merged, z = _merge_math(osb_ref[...], rw_out_v, g_rw_v, gate_ref[...], w_sb_v, w_rw_v, w_o_v)
        gain = gp_ref[...]
        _, zn, rstd = _rms_fwd(z, gain)
        dz, dgain = _rms_bwd(dx1_ref[...], zn, rstd, gain)
        dzb = dz.astype(BF16)
        dm = _mm_nt(dzb, w_o_v)
        dab = (dm * g1).astype(BF16)
        dbb = (dm * g2).astype(BF16)
        dgate = jnp.concatenate([dm * a * g1 * (1.0 - g1), dm * b * g2 * (1.0 - g2)], axis=1)
        do_rw = _mm_nt(dbb, w_rw_v)
        orw_o[...] = o_rw
        mrg_o[...] = merged
        dz_o[...] = dzb
        da_o[...] = dab
        db_o[...] = dbb
        dgate_o[...] = dgate.astype(BF16)
        dosb_o[...] = _mm_nt(dab, w_sb_v).astype(BF16)
        _split_heads(do_rw * g_rw_v, drw_o)
        dg_o[...] = do_rw * rw_out_v
        first = pl.program_id(0) == 0
        _acc_out(dgp_o, dgain, first)
        _acc_out(dbg_o, jnp.sum(dgate, axis=0, keepdims=True), first)

    acc = lambda n: pl.BlockSpec((1, n), lambda i: (0, 0))
    sd = jax.ShapeDtypeStruct
    return pl.pallas_call(
        body, name="merge_bwd", grid=(T // tm,),
        in_specs=[_row_spec(tm, D_MODEL), _row_spec(tm, WIDTH), _head_spec(tm, seq), _row_spec(tm, WIDTH),
                  _row_spec(tm, GATE_COLS), _const_spec((WIDTH, D_MODEL)), _const_spec((WIDTH, D_MODEL)),
                  _const_spec((D_MODEL, D_MODEL)), _const_spec((1, D_MODEL))],
        out_specs=[_row_spec(tm, WIDTH), _row_spec(tm, D_MODEL), _row_spec(tm, D_MODEL), _row_spec(tm, D_MODEL),
                   _row_spec(tm, D_MODEL), _row_spec(tm, GATE_COLS), _row_spec(tm, WIDTH), _head_spec(tm, seq),
                   _row_spec(tm, WIDTH), acc(D_MODEL), acc(GATE_COLS)],
        out_shape=[sd((T, WIDTH), BF16), sd((T, D_MODEL), BF16), sd((T, D_MODEL), BF16), sd((T, D_MODEL), BF16),
                   sd((T, D_MODEL), BF16), sd((T, GATE_COLS), BF16), sd((T, WIDTH), BF16),
                   sd((T // seq, HEADS, seq, HEAD_DIM), F32), sd((T, WIDTH), F32), sd((1, D_MODEL), F32),
                   sd((1, GATE_COLS), F32)],
        compiler_params=_params(("arbitrary",)),
    )(dx1, o_sb, rw_out, g_rw, gates, w_sb, w_rw, w_o, g_post)


def _ffn(x1, target, g_pre, g_post, w_gate, w_up, w_down, tm):
    T = x1.shape[0]

    def body(x1_ref, tgt_ref, gpre_ref, gpost_ref, wg_ref, wu_ref, wd_ref,
             loss_o, dx1_o, h_o, dgate_o, dup_o, act_o, df_o, dgpre_o, dgpost_o):
        x1v = x1_ref[...]
        gpre, gpost = gpre_ref[...], gpost_ref[...]
        wg, wu, wd = wg_ref[...], wu_ref[...], wd_ref[...]
        hn, xn1, rstd1 = _rms_fwd(x1v, gpre)
        h = hn.astype(BF16)
        gate = _mm(h, wg)
        up = _mm(h, wu)
        sg = jax.nn.sigmoid(gate)
        act = (gate * sg * up).astype(BF16)
        f = _mm(act, wd)
        fo, fn, rstd2 = _rms_fwd(f, gpost)
        diff = x1v + fo - tgt_ref[...]
        dy = diff * (1.0 / D_MODEL)
        df, dgpost = _rms_bwd(dy, fn, rstd2, gpost)
        dfb = df.astype(BF16)
        dact = _mm_nt(dfb, wd)
        dup = (dact * gate * sg).astype(BF16)
        dgate = (dact * up * (sg * (1.0 + gate * (1.0 - sg)))).astype(BF16)
        dh = _mm_nt(dgate, wg) + _mm_nt(dup, wu)
        dxn, dgpre = _rms_bwd(dh, xn1, rstd1, gpre)
        dx1_o[...] = dy + dxn
        h_o[...] = h
        dgate_o[...] = dgate
        dup_o[...] = dup
        act_o[...] = act
        df_o[...] = dfb
        first = pl.program_id(0) == 0
        part = jnp.sum(jnp.sum(diff * diff, axis=1, keepdims=True), axis=0, keepdims=True) * (0.5 / D_MODEL)
        _acc_out(loss_o, jnp.broadcast_to(part, (8, 128)), first)
        _acc_out(dgpre_o, dgpre, first)
        _acc_out(dgpost_o, dgpost, first)

    acc = lambda r, n: pl.BlockSpec((r, n), lambda i: (0, 0))
    sd = jax.ShapeDtypeStruct
    return pl.pallas_call(
        body, name="ffn", grid=(T // tm,),
        in_specs=[_row_spec(tm, D_MODEL), _row_spec(tm, D_MODEL), _const_spec((1, D_MODEL)), _const_spec((1, D_MODEL)),
                  _const_spec((D_MODEL, D_FF)), _const_spec((D_MODEL, D_FF)), _const_spec((D_FF, D_MODEL))],
        out_specs=[acc(8, 128), _row_spec(tm, D_MODEL), _row_spec(tm, D_MODEL), _row_spec(tm, D_FF), _row_spec(tm, D_FF),
                   _row_spec(tm, D_FF), _row_spec(tm, D_MODEL), acc(1, D_MODEL), acc(1, D_MODEL)],
        out_shape=[sd((8, 128), F32), sd((T, D_MODEL), F32), sd((T, D_MODEL), BF16), sd((T, D_FF), BF16),
                   sd((T, D_FF), BF16), sd((T, D_FF), BF16), sd((T, D_MODEL), BF16), sd((1, D_MODEL), F32),
                   sd((1, D_MODEL), F32)],
        compiler_params=_params(("arbitrary",)),
    )(x1, target, g_pre, g_post, w_gate, w_up, w_down)


def _local_step(x, target, sm, wt, late=None):
    bl, seq, _ = x.shape
    T = bl * seq
    tm = min(ROW_TILE, T)
    x2 = x.reshape(T, D_MODEL)
    tgt2 = target.reshape(T, D_MODEL)
    h, qkv, prw, gates = _in_proj_fwd(x2, sm["norm_mix_pre"], wt["w_in"], sm["b_gate"], tm)
    if late is None:
        o_sb, lsum, sb_stop = _sb_fwd(qkv, bl, seq)
    else:
        o_sb, lsum, sb_stop, *gathered = _sb_fwd(qkv, bl, seq, late)
        wt = {**wt, **_whole_weights(gathered, slice(_EARLY, None))}
    prep_params = [sm["mu_rw"], sm["w0"], wt["w_up"].astype(F32), sm["a0"], wt["a_up"].astype(F32),
                   wt["g_up"].astype(F32), sm["k_k"], sm["k_a"]]
    prep = _rw_prep_fwd(prw, prep_params, seq, tm)
    by_head = lambda t: t.reshape(bl, HEADS, seq, HEAD_DIM)
    seqs = [t.reshape(bl * HEADS, seq, HEAD_DIM) for t in prep[:6]]
    g_rw = prep[6]
    lnw, lnb, rk = (sm[n].reshape(HEADS, 1, HEAD_DIM) for n in ("lnx_w", "lnx_b", "r_k"))
    rw_out_h, states = _wkv_fwd(seqs, lnw, lnb, rk)
    rw_out = by_head(rw_out_h)
    x1 = _merge_fwd(x2, o_sb, rw_out, g_rw, gates, wt["w_sb_out"], wt["w_rw_out"], wt["w_o"], sm["norm_mix_post"],
                    seq, tm)
    (loss_part, dx1, h2, dffg, dffu, act, dff, d_nfpre, d_nfpost) = _ffn(
        x1, tgt2, sm["norm_ffn_pre"], sm["norm_ffn_post"], wt["w_ffn_gate"], wt["w_ffn_up"], wt["w_ffn_down"],
        min(ROW_TILE_FFN, T))
    (o_rw, merged, dz, da, db, dgate, do_sb, d_rw_out, d_g_rw, d_npost, d_bgate) = _merge_bwd(
        dx1, o_sb, rw_out, g_rw, gates, wt["w_sb_out"], wt["w_rw_out"], wt["w_o"], sm["norm_mix_post"], seq, tm)
    gdt = F32 if late is None else BF16
    gw = {
        "w_sb_out": _grad_w(o_sb, da, "gw_sb_out", gdt), "w_rw_out": _grad_w(o_rw, db, "gw_rw_out", gdt),
        "w_o": _grad_w(merged, dz, "gw_o", gdt),
        "w_ffn_gate": _grad_w(h2, dffg, "gw_ffn_gate", gdt), "w_ffn_up": _grad_w(h2, dffu, "gw_ffn_up", gdt),
        "w_ffn_down": _grad_w(act, dff, "gw_ffn_down", gdt),
    }
    dqkv = jnp.concatenate(_sb_bwd(qkv, do_sb, lsum, sb_stop, bl, seq), axis=1)
    ride = None if late is None else _blocks_by_owner(gw, slice(_EARLY, None))
    wkv_g = _wkv_bwd(seqs, states, d_rw_out.reshape(bl * HEADS, seq, HEAD_DIM), lnw, lnb, rk, ride)
    scattered = {} if late is None else {n: (p, g) for (n, _, _), p, g in zip(_SHARDED[_EARLY:], ride, wkv_g[9:])}
    cts = [by_head(t) for t in wkv_g[:6]] + [d_g_rw]
    prep_g = _rw_prep_bwd(prw, prep_params, cts, seq, tm)
    dprw = prep_g[0]
    d_mu, d_w0, d_wup, d_a0, d_aup, d_gup, d_kk, d_ka = prep_g[1:]
    gw = {
        **gw,
        "w_in": jnp.concatenate([_grad_w(h, dqkv, "gw_in_qkv", gdt), _grad_w(h, dprw, "gw_in_rw", gdt),
                                 _grad_w(h, dgate, "gw_in_gate", gdt)], axis=1),
        "w_up": d_wup.astype(gdt), "a_up": d_aup.astype(gdt), "g_up": d_gup.astype(gdt),
    }
    ride = None if late is None else _blocks_by_owner(gw, slice(0, _EARLY))
    grad_x, d_npre, *got = _in_proj_bwd(x2, sm["norm_mix_pre"], dx1, dqkv, dprw, dgate, wt["w_in"], tm, ride)
    if late is not None:
        scattered.update({n: (p, g) for (n, _, _), p, g in zip(_SHARDED[:_EARLY], ride, got)})
    gs = {
        "norm_mix_pre": d_npre, "b_gate": d_bgate, "mu_rw": d_mu, "w0": d_w0, "a0": d_a0, "k_k": d_kk, "k_a": d_ka,
        "r_k": wkv_g[8].reshape(1, WIDTH), "lnx_w": wkv_g[6].reshape(1, WIDTH), "lnx_b": wkv_g[7].reshape(1, WIDTH),
        "norm_mix_post": d_npost, "norm_ffn_pre": d_nfpre, "norm_ffn_post": d_nfpost,
    }
    return loss_part, grad_x.reshape(x.shape), gw, gs, scattered


_SHARDED = [("w_in", 1, (D_MODEL, (SB_COLS + RW_COLS + GATE_COLS) // N_DEV)), ("w_up", 1, (W_LORA, WIDTH // N_DEV)),
            ("a_up", 1, (A_LORA, WIDTH // N_DEV)), ("g_up", 1, (G_LORA, WIDTH // N_DEV)),
            ("w_sb_out", 1, (WIDTH, D_MODEL // N_DEV)), ("w_rw_out", 1, (WIDTH, D_MODEL // N_DEV)),
            ("w_o", 0, (D_MODEL // N_DEV, D_MODEL)), ("w_ffn_gate", 1, (D_MODEL, D_FF // N_DEV)),
            ("w_ffn_up", 1, (D_MODEL, D_FF // N_DEV)), ("w_ffn_down", 0, (D_FF // N_DEV, D_MODEL))]
_LANES = 128
_SMALL = [("norm_mix_pre", D_MODEL), ("b_gate", GATE_COLS), ("mu_rw", RW_COLS), ("w0", WIDTH), ("a0", WIDTH),
          ("k_k", WIDTH), ("k_a", WIDTH), ("r_k", WIDTH), ("lnx_w", WIDTH), ("lnx_b", WIDTH),
          ("norm_mix_post", D_MODEL), ("norm_ffn_pre", D_MODEL), ("norm_ffn_post", D_MODEL)]
_SMALL_ROWS = 96


_EARLY = 4


def _whole_weights(gathered, which):
    return {n: blk.reshape(N_DEV * shp[0], shp[1]) if axis == 0 else blk.transpose(1, 0, 2).reshape(shp[0], N_DEV * shp[1])
            for (n, axis, shp), blk in zip(_SHARDED[which], gathered)}


def _blocks_by_owner(gw, which):
    return [gw[n].reshape((N_DEV,) + shp) if axis == 0 else gw[n].reshape(shp[0], N_DEV, shp[1]).transpose(1, 0, 2)
            for n, axis, shp in _SHARDED[which]]


def _pack_small(vals, extra=None):
    used = sum(sz for _, sz in _SMALL)
    tail = jnp.zeros((1, _SMALL_ROWS * _LANES - used), F32).at[0, 0].set(extra)
    return jnp.concatenate([vals[n].reshape(1, -1) for n, _ in _SMALL] + [tail], axis=1)


_ANY = pl.BlockSpec(memory_space=pl.ANY)


def _all_gather(blocks):
    n = len(blocks)

    def body(*refs):
        start, forward, finish = _gather_steps(refs[:n], refs[n:2 * n], *refs[2 * n:])
        start()
        forward()
        finish()

    out_shape, sems = _gather_results(blocks)
    return pl.pallas_call(
        body, name="all_gather_weights", in_specs=[_ANY] * n, out_specs=[_ANY] * n, out_shape=out_shape,
        scratch_shapes=sems,
    )(*blocks)


def _scatter_steps(pack_refs, got_refs, send_sems, recv_sems):
    x, y, c = lax.axis_index("x"), lax.axis_index("y"), lax.axis_index("c")

    def copies():
        out = []
        for t, (pack_ref, got_ref) in enumerate(zip(pack_refs, got_refs)):
            for k in range(1, N_DEV):
                px, py, pc = x ^ (k >> 2), y ^ ((k >> 1) & 1), c ^ (k & 1)
                sem = (N_DEV - 1) * t + k - 1
                out.append(pltpu.make_async_remote_copy(
                    src_ref=pack_ref.at[4 * px + 2 * py + pc], dst_ref=got_ref.at[k - 1], send_sem=send_sems.at[sem],
                    recv_sem=recv_sems.at[sem], device_id=(px, py, pc), device_id_type=MESH))
        return out

    def start():
        for cp in copies():
            cp.start()

    def finish():
        for cp in copies():
            cp.wait_recv()
        for cp in copies():
            cp.wait_send()

    return start, finish


def _scatter_results(ride):
    n = (N_DEV - 1) * len(ride)
    return ([jax.ShapeDtypeStruct((N_DEV - 1,) + t.shape[1:], t.dtype) for t in ride],
            [pltpu.SemaphoreType.DMA((n,)), pltpu.SemaphoreType.DMA((n,))])


def _gather_steps(x_refs, out_refs, send_sems, recv_sems, local_sems):
    x, y, c = lax.axis_index("x"), lax.axis_index("y"), lax.axis_index("c")
    me, sibling = (x, y, c), (x, y, 1 - c)
    chips = [(1 - x, y), (x, 1 - y), (1 - x, 1 - y)]
    tensors = range(len(x_refs))

    def slot(t, px, py, pc):
        return out_refs[t].at[4 * px + 2 * py + pc]

    def copy(t, k, blk, to, src=None):
        return pltpu.make_async_remote_copy(
            src_ref=slot(t, *blk) if src is None else src, dst_ref=slot(t, *blk), send_sem=send_sems.at[7 * t + k],
            recv_sem=recv_sems.at[7 * t + k], device_id=to, device_id_type=MESH)

    def mine(t):
        return pltpu.make_async_copy(x_refs[t], slot(t, *me), local_sems.at[t])

    def first():
        return [cp for t in tensors for cp in
                [copy(t, 0, me, sibling, src=x_refs[t])]
                + [copy(t, 1 + j, me, (*chip, c), src=x_refs[t]) for j, chip in enumerate(chips)]]

    def passed(t):
        return [copy(t, 4 + j, (*chip, c), sibling) for j, chip in enumerate(chips)]

    def start():
        for t in tensors:
            mine(t).start()
        for cp in first():
            cp.start()

    def forward():
        for t in tensors:
            for j, (chip, cp) in enumerate(zip(chips, passed(t))):
                copy(t, 1 + j, (*chip, c), me).wait_recv()
                cp.start()

    def finish():
        for t in tensors:
            copy(t, 0, sibling, me).wait_recv()
            for j, chip in enumerate(chips):
                copy(t, 4 + j, (*chip, 1 - c), me).wait_recv()
        for cp in first() + [cp for t in tensors for cp in passed(t)]:
            cp.wait_send()
        for t in tensors:
            mine(t).wait()

    return start, forward, finish


def _gather_results(blocks):
    n = len(blocks)
    return ([jax.ShapeDtypeStruct((N_DEV,) + t.shape, t.dtype) for t in blocks],
            [pltpu.SemaphoreType.DMA((7 * n,)), pltpu.SemaphoreType.DMA((7 * n,)), pltpu.SemaphoreType.DMA((n,))])


def _adamw_math(w, g, m, v):
    m = ADAM_B1 * m + (1.0 - ADAM_B1) * g
    v = ADAM_B2 * v + (1.0 - ADAM_B2) * (g * g)
    m_hat = m / (1.0 - ADAM_B1 ** ADAM_STEP)
    v_hat = v / (1.0 - ADAM_B2 ** ADAM_STEP)
    return -ADAM_LR * (m_hat / (jnp.sqrt(v_hat) + ADAM_EPS) + ADAM_WD * w), m, v


def _adamw_scattered(w, m, v, parts, got, me, name):
    rows, cols = w.shape
    tr = ADAMW_ROWS if rows % ADAMW_ROWS == 0 and rows * cols > ADAMW_WHOLE_BELOW else rows

    def body(me_ref, w_ref, m_ref, v_ref, own_ref, got_ref, g_o, d_o, m_o, v_o):
        g = own_ref[...].astype(F32)
        for k in range(N_DEV - 1):
            g = g + got_ref[k].astype(F32)
        g_o[...] = g
        d_o[...], m_o[...], v_o[...] = _adamw_math(w_ref[...], g, m_ref[...], v_ref[...])

    spec = pl.BlockSpec((tr, cols), lambda i, me_ref: (i, 0))
    return pl.pallas_call(
        body, name=name,
        grid_spec=pltpu.PrefetchScalarGridSpec(
            num_scalar_prefetch=1, grid=(rows // tr,),
            in_specs=[spec, spec, spec, pl.BlockSpec((None, tr, cols), lambda i, me_ref: (me_ref[0], i, 0)),
                      pl.BlockSpec((N_DEV - 1, tr, cols), lambda i, me_ref: (0, i, 0))],
            out_specs=[spec] * 4),
        out_shape=[jax.ShapeDtypeStruct((rows, cols), F32)] * 4, compiler_params=_params(("parallel",)),
    )(me, w, m, v, parts, got)


def _adamw_small(small, ws, ms, vs):
    k = len(_SMALL)

    def body(small_ref, *refs):
        w_refs, m_refs, v_refs = refs[:k], refs[k:2 * k], refs[2 * k:3 * k]
        outs = refs[3 * k:3 * k + 1 + 4 * k]
        p_ref, send_sems, recv_sems, local_sem = refs[3 * k + 1 + 4 * k:]
        x, y, c = lax.axis_index("x"), lax.axis_index("y"), lax.axis_index("c")
        me = 4 * x + 2 * y + c
        mine = pltpu.make_async_copy(small_ref, p_ref.at[me], local_sem)
        mine.start()
        peers = [(x ^ (j >> 2), y ^ ((j >> 1) & 1), c ^ (j & 1)) for j in range(1, N_DEV)]
        sends = [pltpu.make_async_remote_copy(
            src_ref=small_ref, dst_ref=p_ref.at[me], send_sem=send_sems.at[j], recv_sem=recv_sems.at[j],
            device_id=to, device_id_type=MESH) for j, to in enumerate(peers)]
        for cp in sends:
            cp.start()
        for j, (px, py, pc) in enumerate(peers):
            pltpu.make_async_remote_copy(
                src_ref=small_ref, dst_ref=p_ref.at[4 * px + 2 * py + pc], send_sem=send_sems.at[j],
                recv_sem=recv_sems.at[j], device_id=(px, py, pc), device_id_type=MESH).wait_recv()
        for cp in sends:
            cp.wait_send()
        mine.wait()
        g = p_ref[0]
        for d in range(1, N_DEV):
            g = g + p_ref[d]
        o = 0
        for i, (_, n) in enumerate(_SMALL):
            gp = g[:, o:o + n]
            outs[1 + i][...] = gp
            outs[1 + k + i][...], outs[1 + 2 * k + i][...], outs[1 + 3 * k + i][...] = _adamw_math(
                w_refs[i][...], gp, m_refs[i][...], v_refs[i][...])
            o += n
        outs[0][...] = g[:, o:o + _LANES]

    shapes = [jax.ShapeDtypeStruct((1, n), F32) for _, n in _SMALL]
    vmem = pl.BlockSpec(memory_space=pltpu.VMEM)
    out = pl.pallas_call(
        body, name="adamw_small", out_shape=[jax.ShapeDtypeStruct((1, _LANES), F32)] + shapes * 4,
        in_specs=[_ANY] + [vmem] * (3 * k), out_specs=[vmem] * (1 + 4 * k),
        scratch_shapes=[pltpu.VMEM((N_DEV,) + small.shape, F32), pltpu.SemaphoreType.DMA((N_DEV - 1,)),
                        pltpu.SemaphoreType.DMA((N_DEV - 1,)), pltpu.SemaphoreType.DMA],
        compiler_params=_params(),
    )(small, *ws, *ms, *vs)
    return out[0][0, 0], out[1:1 + k], out[1 + k:1 + 2 * k], out[1 + 2 * k:1 + 3 * k], out[1 + 3 * k:]


_WEIGHT_NAMES = ['norm_mix_pre', 'w_in', 'b_gate', 'mu_rw', 'w0', 'w_up', 'a0', 'a_up', 'g_up', 'k_k', 'k_a', 'r_k',
                 'lnx_w', 'lnx_b', 'w_sb_out', 'w_rw_out', 'w_o', 'norm_mix_post', 'norm_ffn_pre', 'w_ffn_gate',
                 'w_ffn_up', 'w_ffn_down', 'norm_ffn_post']


def _step(x, target, w, m, v):
    sharded = [n for n, _, _ in _SHARDED]
    sm = {n: w[n].reshape(1, -1) for n, _ in _SMALL}
    own = {n: w[n][0] for n in sharded}
    mine = [own[n].astype(BF16) for n in sharded]
    wt = _whole_weights(_all_gather(mine[:_EARLY]), slice(0, _EARLY))
    loss_part, grad_x, _, gs, scattered = _local_step(x, target, sm, wt, mine[_EARLY:])

    me = 4 * lax.axis_index("x") + 2 * lax.axis_index("y") + lax.axis_index("c")
    me = jnp.reshape(me, (1,)).astype(jnp.int32)
    small_parts = _pack_small(gs, loss_part[0, 0])
    row = lambda t: [t[n].reshape(1, -1) for n, _ in _SMALL]
    loss, *by_kind = _adamw_small(small_parts, row(w), row(m), row(v))
    g_s, d_s, m_s, v_s = ({n: t[i] for i, (n, _) in enumerate(_SMALL)} for t in by_kind)

    grads, deltas, new_m, new_v = {}, {}, {}, {}
    for n in _WEIGHT_NAMES:
        if n in scattered:
            out = _adamw_scattered(own[n], m[n][0], v[n][0], *scattered[n], me, "adamw_" + n)
            grads[n], deltas[n], new_m[n], new_v[n] = (t.reshape(w[n].shape) for t in out)
        else:
            grads[n], deltas[n], new_m[n], new_v[n] = (t[n].reshape(w[n].shape) for t in (g_s, d_s, m_s, v_s))
    return (loss, grad_x, *[grads[n] for n in _WEIGHT_NAMES], *[deltas[n] for n in _WEIGHT_NAMES],
            *[new_m[n] for n in _WEIGHT_NAMES], *[new_v[n] for n in _WEIGHT_NAMES])


def kernel(x, norm_mix_pre, w_in, b_gate, mu_rw, w0, w_up, a0, a_up, g_up, k_k, k_a, r_k, lnx_w, lnx_b, w_sb_out, w_rw_out, w_o, norm_mix_post, norm_ffn_pre, w_ffn_gate, w_ffn_up, w_ffn_down, norm_ffn_post, loss_target, m_norm_mix_pre, m_w_in, m_b_gate, m_mu_rw, m_w0, m_w_up, m_a0, m_a_up, m_g_up, m_k_k, m_k_a, m_r_k, m_lnx_w, m_lnx_b, m_w_sb_out, m_w_rw_out, m_w_o, m_norm_mix_post, m_norm_ffn_pre, m_w_ffn_gate, m_w_ffn_up, m_w_ffn_down, m_norm_ffn_post, v_norm_mix_pre, v_w_in, v_b_gate, v_mu_rw, v_w0, v_w_up, v_a0, v_a_up, v_g_up, v_k_k, v_k_a, v_r_k, v_lnx_w, v_lnx_b, v_w_sb_out, v_w_rw_out, v_w_o, v_norm_mix_post, v_norm_ffn_pre, v_w_ffn_gate, v_w_ffn_up, v_w_ffn_down, v_norm_ffn_post):
    args = locals()
    w = {n: args[n] for n in _WEIGHT_NAMES}
    m = {n: args["m_" + n] for n in _WEIGHT_NAMES}
    v = {n: args["v_" + n] for n in _WEIGHT_NAMES}
    return _step(x, loss_target, w, m, v)
```
